```python
import math
import jax, jax.numpy as jnp
from jax import lax
import numpy as np

D_MODEL = 1024
BATCH = 8
SEQ = 4096
DEPTH = 2

BRANCH_WIDTH = 512
N_BRANCH = 3
S5_GROUP = 16
S5_GROUPS = BRANCH_WIDTH // S5_GROUP
S5_STATE = 64
S5_DT_MIN = 1e-3
S5_DT_MAX = 1e-1
S5_EIG_MAX = -1e-4
HG_HEADS = 4
HG_DK = 128
HG_DV = BRANCH_WIDTH // HG_HEADS
HG_KEY_WIDTH = HG_HEADS * HG_DK
HG_CHUNK = 64
RG_BLOCKS = 8
RG_BLOCK = BRANCH_WIDTH // RG_BLOCKS
RG_C = 8.0
CONV_WIDTH = 4
D_FF = 2816
EPS = 1e-6
IN_SPLIT_SIZES = (BRANCH_WIDTH, HG_KEY_WIDTH, HG_KEY_WIDTH, BRANCH_WIDTH, BRANCH_WIDTH, BRANCH_WIDTH, BRANCH_WIDTH)
IN_TOTAL = BRANCH_WIDTH + 2 * HG_KEY_WIDTH + 2 * BRANCH_WIDTH + 2 * BRANCH_WIDTH + N_BRANCH * D_MODEL

kernel_name = 'hybrid_s5_hgrn2_rglru_macaron'


def rms_norm(x, w):
    xf = x.astype(jnp.float32)
    y = xf * lax.rsqrt(jnp.mean(xf * xf, axis=-1, keepdims=True) + EPS)
    return (y * w.astype(jnp.float32)).astype(x.dtype)


def swiglu(h, w_gate, w_up, w_down):
    return (jax.nn.silu(h @ w_gate) * (h @ w_up)) @ w_down


def _complex_affine_combine(e1, e2):
    a1r, a1i, b1r, b1i = e1
    a2r, a2i, b2r, b2i = e2
    return (a2r * a1r - a2i * a1i,
            a2r * a1i + a2i * a1r,
            a2r * b1r - a2i * b1i + b2r,
            a2r * b1i + a2i * b1r + b2i)


def _real_affine_combine(e1, e2):
    a1, b1 = e1
    a2, b2 = e2
    return (a2 * a1, a2 * b1 + b2)


def s5_mixer(u, lam_re, lam_im, log_dt, b_re, b_im, c_re, c_im, d_skip, glu_w, glu_b):
    f32 = jnp.float32
    bsz, seq, _ = u.shape
    uf = u.astype(f32)
    ug = uf.reshape(bsz, seq, S5_GROUPS, S5_GROUP)
    lr = jnp.minimum(lam_re.astype(f32), S5_EIG_MAX)
    li = lam_im.astype(f32)
    dt = jnp.exp(log_dt.astype(f32))[:, None]
    mag = jnp.exp(lr * dt)
    ar = mag * jnp.cos(li * dt)
    ai = mag * jnp.sin(li * dt)
    den = lr * lr + li * li
    fr = ((ar - 1.0) * lr + ai * li) / den
    fi = (ai * lr - (ar - 1.0) * li) / den
    br, bi = b_re.astype(f32), b_im.astype(f32)
    bbr = fr[..., None] * br - fi[..., None] * bi
    bbi = fr[..., None] * bi + fi[..., None] * br
    bu_r = jnp.einsum('blgc,gpc->blgp', ug, bbr)
    bu_i = jnp.einsum('blgc,gpc->blgp', ug, bbi)
    a_r = jnp.broadcast_to(ar, bu_r.shape)
    a_i = jnp.broadcast_to(ai, bu_i.shape)
    _, _, xr, xi = lax.associative_scan(_complex_affine_combine, (a_r, a_i, bu_r, bu_i), axis=1)
    y = (jnp.einsum('blgp,gcp->blgc', xr, c_re.astype(f32))
         - jnp.einsum('blgp,gcp->blgc', xi, c_im.astype(f32)))
    y = y.reshape(bsz, seq, BRANCH_WIDTH) + d_skip.astype(f32) * uf
    z = jax.nn.gelu(y)
    out = z * jax.nn.sigmoid(z @ glu_w.astype(f32) + glu_b.astype(f32))
    return out.astype(u.dtype)


def hgrn2_mixer(q, z_f, v, g, lb, norm_w):
    f32 = jnp.float32
    bsz, seq, _ = q.shape
    n_chunks = seq // HG_CHUNK
    lb = lb.astype(f32).reshape(HG_HEADS, HG_DK)
    qh = jax.nn.silu(q.astype(f32)).reshape(bsz, seq, HG_HEADS, HG_DK)
    zf = z_f.astype(f32).reshape(bsz, seq, HG_HEADS, HG_DK)
    log_f = jnp.log(lb + (1.0 - lb) * jax.nn.sigmoid(zf))
    kh = (1.0 - lb) * jax.nn.sigmoid(-zf)
    vh = v.astype(f32).reshape(bsz, seq, HG_HEADS, HG_DV)

    def to_chunks(t):
        return t.reshape(bsz, n_chunks, HG_CHUNK, HG_HEADS, t.shape[-1]).transpose(1, 0, 3, 2, 4)

    causal = jnp.tril(jnp.ones((HG_CHUNK, HG_CHUNK), dtype=bool))[:, :, None]

    def chunk_step(state, inp):
        qc, kc, vc, lfc = inp
        b = jnp.cumsum(lfc, axis=2)
        o_inter = jnp.einsum('bhcd,bhde->bhce', qc * jnp.exp(b), state)
        diff = b[:, :, :, None, :] - b[:, :, None, :, :]
        decay = jnp.where(causal, jnp.exp(jnp.where(causal, diff, 0.0)), 0.0)
        scores = jnp.einsum('bhtd,bhtsd,bhsd->bhts', qc, decay, kc)
        o_intra = jnp.einsum('bhts,bhse->bhte', scores, vc)
        b_last = b[:, :, -1:, :]
        new_state = (jnp.exp(b_last[:, :, 0, :])[..., None] * state
                     + jnp.einsum('bhsd,bhse->bhde', kc * jnp.exp(b_last - b), vc))
        return new_state, o_inter + o_intra

    s0 = jnp.zeros((bsz, HG_HEADS, HG_DK, HG_DV), f32)
    _, o = lax.scan(chunk_step, s0, (to_chunks(qh), to_chunks(kh), to_chunks(vh), to_chunks(log_f)))
    o = o.transpose(1, 0, 3, 2, 4).reshape(bsz, seq, HG_HEADS, HG_DV)
    o = o * lax.rsqrt(jnp.mean(o * o, axis=-1, keepdims=True) + EPS)
    o = o * norm_w.astype(f32).reshape(HG_HEADS, HG_DV)
    out = o.reshape(bsz, seq, BRANCH_WIDTH) * jax.nn.silu(g.astype(f32))
    return out.astype(q.dtype)


def rglru_mixer(xb, gate, conv_w, conv_b, wa, ba, wx, bx, lam):
    f32 = jnp.float32
    bsz, seq, _ = xb.shape
    xc = lax.conv_general_dilated(
        xb, conv_w[:, None, :], window_strides=(1,), padding=[(CONV_WIDTH - 1, 0)],
        dimension_numbers=('NWC', 'WIO', 'NWC'), feature_group_count=BRANCH_WIDTH) + conv_b
    xcf = xc.astype(f32)
    xblk = xcf.reshape(bsz, seq, RG_BLOCKS, RG_BLOCK)
    r = jax.nn.sigmoid(jnp.einsum('blhi,hij->blhj', xblk, wa.astype(f32)).reshape(bsz, seq, BRANCH_WIDTH) + ba.astype(f32))
    i = jax.nn.sigmoid(jnp.einsum('blhi,hij->blhj', xblk, wx.astype(f32)).reshape(bsz, seq, BRANCH_WIDTH) + bx.astype(f32))
    log_a = -RG_C * jax.nn.softplus(-lam.astype(f32)) * r
    a = jnp.exp(log_a)
    b = jnp.sqrt(-jnp.expm1(2.0 * log_a)) * (i * xcf)
    _, hseq = lax.associative_scan(_real_affine_combine, (a, b), axis=1)
    return (hseq * jax.nn.gelu(gate.astype(f32))).astype(xb.dtype)


def hybrid_mixer(h, w_in, branch_proj, w_out,
                 s5_lambda_re, s5_lambda_im, s5_log_dt, s5_b_re, s5_b_im, s5_c_re, s5_c_im,
                 s5_d, s5_glu_w, s5_glu_b, hg_lb, hg_norm_w,
                 rg_conv_w, rg_conv_b, rg_wa, rg_ba, rg_wx, rg_bx, rg_lambda):
    bsz, seq, _ = h.shape
    proj = h @ w_in
    points, acc = [], 0
    for s in IN_SPLIT_SIZES:
        acc += s
        points.append(acc)
    u_a, q_b, f_b, v_b, g_b, x_c, gate_c, gate_merge = jnp.split(proj, points, axis=-1)
    y_a = s5_mixer(u_a, s5_lambda_re, s5_lambda_im, s5_log_dt, s5_b_re, s5_b_im,
                   s5_c_re, s5_c_im, s5_d, s5_glu_w, s5_glu_b)
    y_b = hgrn2_mixer(q_b, f_b, v_b, g_b, hg_lb, hg_norm_w)
    y_c = rglru_mixer(x_c, gate_c, rg_conv_w, rg_conv_b, rg_wa, rg_ba, rg_wx, rg_bx, rg_lambda)
    branches = jnp.stack([y_a, y_b, y_c], axis=2)
    up = jnp.einsum('blnw,nwd->blnd', branches, branch_proj)
    gates = jax.nn.sigmoid(gate_merge.astype(jnp.float32)).reshape(bsz, seq, N_BRANCH, D_MODEL)
    merged = jnp.sum(gates * up.astype(jnp.float32), axis=2).astype(h.dtype)
    return merged @ w_out


def _fwd_setup_inputs(seed: int = 0) -> dict:
    key = jax.random.key(seed)
    ks = jax.random.split(key, 32)
    f32 = jnp.float32

    def nrm(k, shape, scale):
        return jax.random.normal(k, shape, f32) * scale

    x = nrm(ks[0], (BATCH, SEQ, D_MODEL), 1.0)
    norm_w = 1.0 + nrm(ks[1], (DEPTH, 3, D_MODEL), 0.02)
    final_norm_w = 1.0 + nrm(ks[2], (D_MODEL,), 0.02)
    ffn_gate = nrm(ks[3], (DEPTH, 2, D_MODEL, D_FF), D_MODEL ** -0.5)
    ffn_up = nrm(ks[4], (DEPTH, 2, D_MODEL, D_FF), D_MODEL ** -0.5)
    ffn_down = nrm(ks[5], (DEPTH, 2, D_FF, D_MODEL), D_FF ** -0.5)
    w_in = nrm(ks[6], (DEPTH, D_MODEL, IN_TOTAL), D_MODEL ** -0.5)
    branch_proj = nrm(ks[7], (DEPTH, N_BRANCH, BRANCH_WIDTH, D_MODEL), BRANCH_WIDTH ** -0.5)
    w_out = nrm(ks[8], (DEPTH, D_MODEL, D_MODEL), D_MODEL ** -0.5)
    s5_lambda_re = -0.5 + nrm(ks[9], (DEPTH, S5_GROUPS, S5_STATE), 0.01)
    s5_lambda_im = (math.pi * jnp.arange(S5_STATE, dtype=f32)) + nrm(ks[10], (DEPTH, S5_GROUPS, S5_STATE), 0.01)
    s5_log_dt = jax.random.uniform(ks[11], (DEPTH, S5_GROUPS), f32, math.log(S5_DT_MIN), math.log(S5_DT_MAX))
    s5_b_re = nrm(ks[12], (DEPTH, S5_GROUPS, S5_STATE, S5_GROUP), (2.0 * S5_GROUP) ** -0.5)
    s5_b_im = nrm(ks[13], (DEPTH, S5_GROUPS, S5_STATE, S5_GROUP), (2.0 * S5_GROUP) ** -0.5)
    s5_c_re = nrm(ks[14], (DEPTH, S5_GROUPS, S5_GROUP, S5_STATE), S5_STATE ** -0.5)
    s5_c_im = nrm(ks[15], (DEPTH, S5_GROUPS, S5_GROUP, S5_STATE), S5_STATE ** -0.5)
    s5_d = nrm(ks[16], (DEPTH, BRANCH_WIDTH), 1.0)
    s5_glu_w = nrm(ks[17], (DEPTH, BRANCH_WIDTH, BRANCH_WIDTH), BRANCH_WIDTH ** -0.5)
    s5_glu_b = nrm(ks[18], (DEPTH, BRANCH_WIDTH), 0.01)
    hg_lb_logits = 1.0 + nrm(ks[19], (DEPTH, HG_KEY_WIDTH), 0.1)
    hg_norm_w = 1.0 + nrm(ks[20], (DEPTH, BRANCH_WIDTH), 0.02)
    rg_conv_w = nrm(ks[21], (DEPTH, CONV_WIDTH, BRANCH_WIDTH), CONV_WIDTH ** -0.5)
    rg_conv_b = nrm(ks[22], (DEPTH, BRANCH_WIDTH), 0.01)
    rg_wa = nrm(ks[23], (DEPTH, RG_BLOCKS, RG_BLOCK, RG_BLOCK), RG_BLOCK ** -0.5)
    rg_ba = nrm(ks[24], (DEPTH, BRANCH_WIDTH), 0.01)
    rg_wx = nrm(ks[25], (DEPTH, RG_BLOCKS, RG_BLOCK, RG_BLOCK), RG_BLOCK ** -0.5)
    rg_bx = nrm(ks[26], (DEPTH, BRANCH_WIDTH), 0.01)
    a_c = jax.random.uniform(ks[27], (DEPTH, BRANCH_WIDTH), f32, 0.9, 0.999)
    s = a_c ** (1.0 / RG_C)
    rg_lambda = jnp.log(s) - jnp.log1p(-s)
    return {'x': x, 'norm_w': norm_w, 'final_norm_w': final_norm_w,
            'ffn_gate': ffn_gate, 'ffn_up': ffn_up, 'ffn_down': ffn_down,
            'w_in': w_in, 'branch_proj': branch_proj, 'w_out': w_out,
            's5_lambda_re': s5_lambda_re, 's5_lambda_im': s5_lambda_im, 's5_log_dt': s5_log_dt,
            's5_b_re': s5_b_re, 's5_b_im': s5_b_im, 's5_c_re': s5_c_re, 's5_c_im': s5_c_im,
            's5_d': s5_d, 's5_glu_w': s5_glu_w, 's5_glu_b': s5_glu_b,
            'hg_lb_logits': hg_lb_logits, 'hg_norm_w': hg_norm_w,
            'rg_conv_w': rg_conv_w, 'rg_conv_b': rg_conv_b, 'rg_wa': rg_wa, 'rg_ba': rg_ba,
            'rg_wx': rg_wx, 'rg_bx': rg_bx, 'rg_lambda': rg_lambda}


def _fwd_reference(x, norm_w, final_norm_w, ffn_gate, ffn_up, ffn_down, w_in, branch_proj, w_out,
              s5_lambda_re, s5_lambda_im, s5_log_dt, s5_b_re, s5_b_im, s5_c_re, s5_c_im,
              s5_d, s5_glu_w, s5_glu_b, hg_lb_logits, hg_norm_w,
              rg_conv_w, rg_conv_b, rg_wa, rg_ba, rg_wx, rg_bx, rg_lambda):
    p = jax.nn.softmax(hg_lb_logits.astype(jnp.float32), axis=0)
    lower_bounds = jnp.cumsum(p, axis=0) - p[0]
    for l in range(DEPTH):
        h = rms_norm(x, norm_w[l, 0])
        x = x + 0.5 * swiglu(h, ffn_gate[l, 0], ffn_up[l, 0], ffn_down[l, 0])
        h = rms_norm(x, norm_w[l, 1])
        x = x + hybrid_mixer(h, w_in[l], branch_proj[l], w_out[l],
                             s5_lambda_re[l], s5_lambda_im[l], s5_log_dt[l], s5_b_re[l], s5_b_im[l],
                             s5_c_re[l], s5_c_im[l], s5_d[l], s5_glu_w[l], s5_glu_b[l],
                             lower_bounds[l], hg_norm_w[l],
                             rg_conv_w[l], rg_conv_b[l], rg_wa[l], rg_ba[l], rg_wx[l], rg_bx[l], rg_lambda[l])
        h = rms_norm(x, norm_w[l, 2])
        x = x + 0.5 * swiglu(h, ffn_gate[l, 1], ffn_up[l, 1], ffn_down[l, 1])
    return rms_norm(x, final_norm_w)


import jax as _jax
import jax.numpy as _jnp

TWIN_FORMAT = 'train_step'
FWD_PARAMS = ['x', 'norm_w', 'final_norm_w', 'ffn_gate', 'ffn_up', 'ffn_down', 'w_in', 'branch_proj', 'w_out', 's5_lambda_re', 's5_lambda_im', 's5_log_dt', 's5_b_re', 's5_b_im', 's5_c_re', 's5_c_im', 's5_d', 's5_glu_w', 's5_glu_b', 'hg_lb_logits', 'hg_norm_w', 'rg_conv_w', 'rg_conv_b', 'rg_wa', 'rg_ba', 'rg_wx', 'rg_bx', 'rg_lambda']
TWIN_WEIGHTS = ['norm_w', 'final_norm_w', 'ffn_gate', 'ffn_up', 'ffn_down', 'w_in', 'branch_proj', 'w_out', 's5_lambda_re', 's5_lambda_im', 's5_log_dt', 's5_b_re', 's5_b_im', 's5_c_re', 's5_c_im', 's5_d', 's5_glu_w', 's5_glu_b', 'hg_lb_logits', 'hg_norm_w', 'rg_conv_w', 'rg_conv_b', 'rg_wa', 'rg_ba', 'rg_wx', 'rg_bx', 'rg_lambda']
TWIN_DIFF_INPUT = 'x'
TWIN_INPUTS = ['x', 'norm_w', 'final_norm_w', 'ffn_gate', 'ffn_up', 'ffn_down', 'w_in', 'branch_proj', 'w_out', 's5_lambda_re', 's5_lambda_im', 's5_log_dt', 's5_b_re', 's5_b_im', 's5_c_re', 's5_c_im', 's5_d', 's5_glu_w', 's5_glu_b', 'hg_lb_logits', 'hg_norm_w', 'rg_conv_w', 'rg_conv_b', 'rg_wa', 'rg_ba', 'rg_wx', 'rg_bx', 'rg_lambda', 'loss_target', 'm_norm_w', 'm_final_norm_w', 'm_ffn_gate', 'm_ffn_up', 'm_ffn_down', 'm_w_in', 'm_branch_proj', 'm_w_out', 'm_s5_lambda_re', 'm_s5_lambda_im', 'm_s5_log_dt', 'm_s5_b_re', 'm_s5_b_im', 'm_s5_c_re', 'm_s5_c_im', 'm_s5_d', 'm_s5_glu_w', 'm_s5_glu_b', 'm_hg_lb_logits', 'm_hg_norm_w', 'm_rg_conv_w', 'm_rg_conv_b', 'm_rg_wa', 'm_rg_ba', 'm_rg_wx', 'm_rg_bx', 'm_rg_lambda', 'v_norm_w', 'v_final_norm_w', 'v_ffn_gate', 'v_ffn_up', 'v_ffn_down', 'v_w_in', 'v_branch_proj', 'v_w_out', 'v_s5_lambda_re', 'v_s5_lambda_im', 'v_s5_log_dt', 'v_s5_b_re', 'v_s5_b_im', 'v_s5_c_re', 'v_s5_c_im', 'v_s5_d', 'v_s5_glu_w', 'v_s5_glu_b', 'v_hg_lb_logits', 'v_hg_norm_w', 'v_rg_conv_w', 'v_rg_conv_b', 'v_rg_wa', 'v_rg_ba', 'v_rg_wx', 'v_rg_bx', 'v_rg_lambda']
TWIN_OUTPUTS = ['loss', 'grad_x', 'grad_norm_w', 'grad_final_norm_w', 'grad_ffn_gate', 'grad_ffn_up', 'grad_ffn_down', 'grad_w_in', 'grad_branch_proj', 'grad_w_out', 'grad_s5_lambda_re', 'grad_s5_lambda_im', 'grad_s5_log_dt', 'grad_s5_b_re', 'grad_s5_b_im', 'grad_s5_c_re', 'grad_s5_c_im', 'grad_s5_d', 'grad_s5_glu_w', 'grad_s5_glu_b', 'grad_hg_lb_logits', 'grad_hg_norm_w', 'grad_rg_conv_w', 'grad_rg_conv_b', 'grad_rg_wa', 'grad_rg_ba', 'grad_rg_wx', 'grad_rg_bx', 'grad_rg_lambda', 'delta_norm_w', 'delta_final_norm_w', 'delta_ffn_gate', 'delta_ffn_up', 'delta_ffn_down', 'delta_w_in', 'delta_branch_proj', 'delta_w_out', 'delta_s5_lambda_re', 'delta_s5_lambda_im', 'delta_s5_log_dt', 'delta_s5_b_re', 'delta_s5_b_im', 'delta_s5_c_re', 'delta_s5_c_im', 'delta_s5_d', 'delta_s5_glu_w', 'delta_s5_glu_b', 'delta_hg_lb_logits', 'delta_hg_norm_w', 'delta_rg_conv_w', 'delta_rg_conv_b', 'delta_rg_wa', 'delta_rg_ba', 'delta_rg_wx', 'delta_rg_bx', 'delta_rg_lambda', 'new_m_norm_w', 'new_m_final_norm_w', 'new_m_ffn_gate', 'new_m_ffn_up', 'new_m_ffn_down', 'new_m_w_in', 'new_m_branch_proj', 'new_m_w_out', 'new_m_s5_lambda_re', 'new_m_s5_lambda_im', 'new_m_s5_log_dt', 'new_m_s5_b_re', 'new_m_s5_b_im', 'new_m_s5_c_re', 'new_m_s5_c_im', 'new_m_s5_d', 'new_m_s5_glu_w', 'new_m_s5_glu_b', 'new_m_hg_lb_logits', 'new_m_hg_norm_w', 'new_m_rg_conv_w', 'new_m_rg_conv_b', 'new_m_rg_wa', 'new_m_rg_ba', 'new_m_rg_wx', 'new_m_rg_bx', 'new_m_rg_lambda', 'new_v_norm_w', 'new_v_final_norm_w', 'new_v_ffn_gate', 'new_v_ffn_up', 'new_v_ffn_down', 'new_v_w_in', 'new_v_branch_proj', 'new_v_w_out', 'new_v_s5_lambda_re', 'new_v_s5_lambda_im', 'new_v_s5_log_dt', 'new_v_s5_b_re', 'new_v_s5_b_im', 'new_v_s5_c_re', 'new_v_s5_c_im', 'new_v_s5_d', 'new_v_s5_glu_w', 'new_v_s5_glu_b', 'new_v_hg_lb_logits', 'new_v_hg_norm_w', 'new_v_rg_conv_w', 'new_v_rg_conv_b', 'new_v_rg_wa', 'new_v_rg_ba', 'new_v_rg_wx', 'new_v_rg_bx', 'new_v_rg_lambda']
TWIN_LEAF_KINDS = {'loss': 'loss', 'grad_x': 'grad_x', 'grad_norm_w': 'grad_w', 'grad_final_norm_w': 'grad_w', 'grad_ffn_gate': 'grad_w', 'grad_ffn_up': 'grad_w', 'grad_ffn_down': 'grad_w', 'grad_w_in': 'grad_w', 'grad_branch_proj': 'grad_w', 'grad_w_out': 'grad_w', 'grad_s5_lambda_re': 'grad_w', 'grad_s5_lambda_im': 'grad_w', 'grad_s5_log_dt': 'grad_w', 'grad_s5_b_re': 'grad_w', 'grad_s5_b_im': 'grad_w', 'grad_s5_c_re': 'grad_w', 'grad_s5_c_im': 'grad_w', 'grad_s5_d': 'grad_w', 'grad_s5_glu_w': 'grad_w', 'grad_s5_glu_b': 'grad_w', 'grad_hg_lb_logits': 'grad_w', 'grad_hg_norm_w': 'grad_w', 'grad_rg_conv_w': 'grad_w', 'grad_rg_conv_b': 'grad_w', 'grad_rg_wa': 'grad_w', 'grad_rg_ba': 'grad_w', 'grad_rg_wx': 'grad_w', 'grad_rg_bx': 'grad_w', 'grad_rg_lambda': 'grad_w', 'delta_norm_w': 'delta_w', 'delta_final_norm_w': 'delta_w', 'delta_ffn_gate': 'delta_w', 'delta_ffn_up': 'delta_w', 'delta_ffn_down': 'delta_w', 'delta_w_in': 'delta_w', 'delta_branch_proj': 'delta_w', 'delta_w_out': 'delta_w', 'delta_s5_lambda_re': 'delta_w', 'delta_s5_lambda_im': 'delta_w', 'delta_s5_log_dt': 'delta_w', 'delta_s5_b_re': 'delta_w', 'delta_s5_b_im': 'delta_w', 'delta_s5_c_re': 'delta_w', 'delta_s5_c_im': 'delta_w', 'delta_s5_d': 'delta_w', 'delta_s5_glu_w': 'delta_w', 'delta_s5_glu_b': 'delta_w', 'delta_hg_lb_logits': 'delta_w', 'delta_hg_norm_w': 'delta_w', 'delta_rg_conv_w': 'delta_w', 'delta_rg_conv_b': 'delta_w', 'delta_rg_wa': 'delta_w', 'delta_rg_ba': 'delta_w', 'delta_rg_wx': 'delta_w', 'delta_rg_bx': 'delta_w', 'delta_rg_lambda': 'delta_w', 'new_m_norm_w': 'new_m', 'new_m_final_norm_w': 'new_m', 'new_m_ffn_gate': 'new_m', 'new_m_ffn_up': 'new_m', 'new_m_ffn_down': 'new_m', 'new_m_w_in': 'new_m', 'new_m_branch_proj': 'new_m', 'new_m_w_out': 'new_m', 'new_m_s5_lambda_re': 'new_m', 'new_m_s5_lambda_im': 'new_m', 'new_m_s5_log_dt': 'new_m', 'new_m_s5_b_re': 'new_m', 'new_m_s5_b_im': 'new_m', 'new_m_s5_c_re': 'new_m', 'new_m_s5_c_im': 'new_m', 'new_m_s5_d': 'new_m', 'new_m_s5_glu_w': 'new_m', 'new_m_s5_glu_b': 'new_m', 'new_m_hg_lb_logits': 'new_m', 'new_m_hg_norm_w': 'new_m', 'new_m_rg_conv_w': 'new_m', 'new_m_rg_conv_b': 'new_m', 'new_m_rg_wa': 'new_m', 'new_m_rg_ba': 'new_m', 'new_m_rg_wx': 'new_m', 'new_m_rg_bx': 'new_m', 'new_m_rg_lambda': 'new_m', 'new_v_norm_w': 'new_v', 'new_v_final_norm_w': 'new_v', 'new_v_ffn_gate': 'new_v', 'new_v_ffn_up': 'new_v', 'new_v_ffn_down': 'new_v', 'new_v_w_in': 'new_v', 'new_v_branch_proj': 'new_v', 'new_v_w_out': 'new_v', 'new_v_s5_lambda_re': 'new_v', 'new_v_s5_lambda_im': 'new_v', 'new_v_s5_log_dt': 'new_v', 'new_v_s5_b_re': 'new_v', 'new_v_s5_b_im': 'new_v', 'new_v_s5_c_re': 'new_v', 'new_v_s5_c_im': 'new_v', 'new_v_s5_d': 'new_v', 'new_v_s5_glu_w': 'new_v', 'new_v_s5_glu_b': 'new_v', 'new_v_hg_lb_logits': 'new_v', 'new_v_hg_norm_w': 'new_v', 'new_v_rg_conv_w': 'new_v', 'new_v_rg_conv_b': 'new_v', 'new_v_rg_wa': 'new_v', 'new_v_rg_ba': 'new_v', 'new_v_rg_wx': 'new_v', 'new_v_rg_bx': 'new_v', 'new_v_rg_lambda': 'new_v'}


def _forward(args):
    return _fwd_reference(*[args[k] for k in FWD_PARAMS])


def _output_shape():
    def fwd():
        inp = _fwd_setup_inputs(0)
        return _fwd_reference(*[inp[k] for k in FWD_PARAMS])
    out = _jax.eval_shape(fwd)
    return out.shape, out.dtype

N_MICROBATCH = 1
ADAM_LR = 0.001
ADAM_B1 = 0.9
ADAM_B2 = 0.999
ADAM_EPS = 1e-08
ADAM_WD = 0.01
ADAM_STEP = 10
PER_EXAMPLE_BATCH_AXIS = {'x': 0, 'loss_target': 0}
SHARED_INPUTS = []
_WEIGHT_DTYPES = {'norm_w': _jnp.float32, 'final_norm_w': _jnp.float32, 'ffn_gate': _jnp.float32, 'ffn_up': _jnp.float32, 'ffn_down': _jnp.float32, 'w_in': _jnp.float32, 'branch_proj': _jnp.float32, 'w_out': _jnp.float32, 's5_lambda_re': _jnp.float32, 's5_lambda_im': _jnp.float32, 's5_log_dt': _jnp.float32, 's5_b_re': _jnp.float32, 's5_b_im': _jnp.float32, 's5_c_re': _jnp.float32, 's5_c_im': _jnp.float32, 's5_d': _jnp.float32, 's5_glu_w': _jnp.float32, 's5_glu_b': _jnp.float32, 'hg_lb_logits': _jnp.float32, 'hg_norm_w': _jnp.float32, 'rg_conv_w': _jnp.float32, 'rg_conv_b': _jnp.float32, 'rg_wa': _jnp.float32, 'rg_ba': _jnp.float32, 'rg_wx': _jnp.float32, 'rg_bx': _jnp.float32, 'rg_lambda': _jnp.float32}
MOMENT_SCALE = {'norm_w': 8.832047e-02, 'final_norm_w': 3.199016e+01, 'ffn_gate': 3.178521e-02, 'ffn_up': 3.079240e-02, 'ffn_down': 5.101199e-02, 'w_in': 4.329908e-02, 'branch_proj': 4.763497e-02, 'w_out': 8.258257e-02, 's5_lambda_re': 3.511479e-03, 's5_lambda_im': 3.544431e-03, 's5_log_dt': 2.552440e+00, 's5_b_re': 2.096447e-03, 's5_b_im': 2.103364e-03, 's5_c_re': 2.964384e-03, 's5_c_im': 3.008267e-03, 's5_d': 6.476645e-02, 's5_glu_w': 1.356921e-02, 's5_glu_b': 2.504627e-02, 'hg_lb_logits': 5.909353e-03, 'hg_norm_w': 7.522063e-02, 'rg_conv_w': 6.908784e-02, 'rg_conv_b': 7.514595e-01, 'rg_wa': 2.866623e-02, 'rg_ba': 2.392350e-02, 'rg_wx': 5.279752e-02, 'rg_bx': 2.975932e-02, 'rg_lambda': 4.306764e-02}


def _to_microbatches(a, axis):
    t = _jnp.moveaxis(a, axis, 0)
    t = t.reshape((N_MICROBATCH, t.shape[0] // N_MICROBATCH) + t.shape[1:])
    return _jnp.moveaxis(t, 1, axis + 1)


def setup_inputs(seed: int = 0) -> dict:
    inp = _fwd_setup_inputs(seed)
    key = _jax.random.fold_in(_jax.random.key(seed), 7919)
    shape, _ = _output_shape()
    out = dict(inp)
    out["loss_target"] = _jax.random.normal(_jax.random.fold_in(key, 0), shape, _jnp.float32)
    for i, name in enumerate(TWIN_WEIGHTS):
        w = inp[name].astype(_jnp.float32)
        if MOMENT_SCALE is None:
            s = _jnp.sqrt(_jnp.mean(_jnp.square(w)) + 1e-30)
        else:
            s = MOMENT_SCALE[name]
        km, kv = _jax.random.split(_jax.random.fold_in(key, i + 1))
        out[name] = w
        out["m_" + name] = s * _jax.random.normal(km, w.shape, _jnp.float32)
        out["v_" + name] = (s * s) * _jax.random.uniform(kv, w.shape, _jnp.float32, 0.5, 1.5)
    if N_MICROBATCH > 1:
        for name, axis in PER_EXAMPLE_BATCH_AXIS.items():
            out[name] = _to_microbatches(out[name], axis)
    return {'x': out['x'], 'norm_w': out['norm_w'], 'final_norm_w': out['final_norm_w'], 'ffn_gate': out['ffn_gate'], 'ffn_up': out['ffn_up'], 'ffn_down': out['ffn_down'], 'w_in': out['w_in'], 'branch_proj': out['branch_proj'], 'w_out': out['w_out'], 's5_lambda_re': out['s5_lambda_re'], 's5_lambda_im': out['s5_lambda_im'], 's5_log_dt': out['s5_log_dt'], 's5_b_re': out['s5_b_re'], 's5_b_im': out['s5_b_im'], 's5_c_re': out['s5_c_re'], 's5_c_im': out['s5_c_im'], 's5_d': out['s5_d'], 's5_glu_w': out['s5_glu_w'], 's5_glu_b': out['s5_glu_b'], 'hg_lb_logits': out['hg_lb_logits'], 'hg_norm_w': out['hg_norm_w'], 'rg_conv_w': out['rg_conv_w'], 'rg_conv_b': out['rg_conv_b'], 'rg_wa': out['rg_wa'], 'rg_ba': out['rg_ba'], 'rg_wx': out['rg_wx'], 'rg_bx': out['rg_bx'], 'rg_lambda': out['rg_lambda'], 'loss_target': out['loss_target'], 'm_norm_w': out['m_norm_w'], 'm_final_norm_w': out['m_final_norm_w'], 'm_ffn_gate': out['m_ffn_gate'], 'm_ffn_up': out['m_ffn_up'], 'm_ffn_down': out['m_ffn_down'], 'm_w_in': out['m_w_in'], 'm_branch_proj': out['m_branch_proj'], 'm_w_out': out['m_w_out'], 'm_s5_lambda_re': out['m_s5_lambda_re'], 'm_s5_lambda_im': out['m_s5_lambda_im'], 'm_s5_log_dt': out['m_s5_log_dt'], 'm_s5_b_re': out['m_s5_b_re'], 'm_s5_b_im': out['m_s5_b_im'], 'm_s5_c_re': out['m_s5_c_re'], 'm_s5_c_im': out['m_s5_c_im'], 'm_s5_d': out['m_s5_d'], 'm_s5_glu_w': out['m_s5_glu_w'], 'm_s5_glu_b': out['m_s5_glu_b'], 'm_hg_lb_logits': out['m_hg_lb_logits'], 'm_hg_norm_w': out['m_hg_norm_w'], 'm_rg_conv_w': out['m_rg_conv_w'], 'm_rg_conv_b': out['m_rg_conv_b'], 'm_rg_wa': out['m_rg_wa'], 'm_rg_ba': out['m_rg_ba'], 'm_rg_wx': out['m_rg_wx'], 'm_rg_bx': out['m_rg_bx'], 'm_rg_lambda': out['m_rg_lambda'], 'v_norm_w': out['v_norm_w'], 'v_final_norm_w': out['v_final_norm_w'], 'v_ffn_gate': out['v_ffn_gate'], 'v_ffn_up': out['v_ffn_up'], 'v_ffn_down': out['v_ffn_down'], 'v_w_in': out['v_w_in'], 'v_branch_proj': out['v_branch_proj'], 'v_w_out': out['v_w_out'], 'v_s5_lambda_re': out['v_s5_lambda_re'], 'v_s5_lambda_im': out['v_s5_lambda_im'], 'v_s5_log_dt': out['v_s5_log_dt'], 'v_s5_b_re': out['v_s5_b_re'], 'v_s5_b_im': out['v_s5_b_im'], 'v_s5_c_re': out['v_s5_c_re'], 'v_s5_c_im': out['v_s5_c_im'], 'v_s5_d': out['v_s5_d'], 'v_s5_glu_w': out['v_s5_glu_w'], 'v_s5_glu_b': out['v_s5_glu_b'], 'v_hg_lb_logits': out['v_hg_lb_logits'], 'v_hg_norm_w': out['v_hg_norm_w'], 'v_rg_conv_w': out['v_rg_conv_w'], 'v_rg_conv_b': out['v_rg_conv_b'], 'v_rg_wa': out['v_rg_wa'], 'v_rg_ba': out['v_rg_ba'], 'v_rg_wx': out['v_rg_wx'], 'v_rg_bx': out['v_rg_bx'], 'v_rg_lambda': out['v_rg_lambda']}


def _loss(weights, diff, rest, loss_target):
    with _jax.named_scope("forward"):
        args = {**rest, TWIN_DIFF_INPUT: diff, **{k: w.astype(_WEIGHT_DTYPES[k]) for k, w in weights.items()}}
        y = _forward(args)
    with _jax.named_scope("loss_head"):
        err = _jnp.square(y.astype(_jnp.float32) - loss_target)
        return 0.5 * _jnp.sum(_jnp.mean(err, axis=-1)) if err.ndim else 0.5 * err


def _adamw(w, g, m, v):
    m = ADAM_B1 * m + (1.0 - ADAM_B1) * g
    v = ADAM_B2 * v + (1.0 - ADAM_B2) * _jnp.square(g)
    m_hat = m / (1.0 - ADAM_B1 ** ADAM_STEP)
    v_hat = v / (1.0 - ADAM_B2 ** ADAM_STEP)
    delta = -ADAM_LR * (m_hat / (_jnp.sqrt(v_hat) + ADAM_EPS) + ADAM_WD * w)
    return delta, m, v


def reference(x, norm_w, final_norm_w, ffn_gate, ffn_up, ffn_down, w_in, branch_proj, w_out, s5_lambda_re, s5_lambda_im, s5_log_dt, s5_b_re, s5_b_im, s5_c_re, s5_c_im, s5_d, s5_glu_w, s5_glu_b, hg_lb_logits, hg_norm_w, rg_conv_w, rg_conv_b, rg_wa, rg_ba, rg_wx, rg_bx, rg_lambda, loss_target, m_norm_w, m_final_norm_w, m_ffn_gate, m_ffn_up, m_ffn_down, m_w_in, m_branch_proj, m_w_out, m_s5_lambda_re, m_s5_lambda_im, m_s5_log_dt, m_s5_b_re, m_s5_b_im, m_s5_c_re, m_s5_c_im, m_s5_d, m_s5_glu_w, m_s5_glu_b, m_hg_lb_logits, m_hg_norm_w, m_rg_conv_w, m_rg_conv_b, m_rg_wa, m_rg_ba, m_rg_wx, m_rg_bx, m_rg_lambda, v_norm_w, v_final_norm_w, v_ffn_gate, v_ffn_up, v_ffn_down, v_w_in, v_branch_proj, v_w_out, v_s5_lambda_re, v_s5_lambda_im, v_s5_log_dt, v_s5_b_re, v_s5_b_im, v_s5_c_re, v_s5_c_im, v_s5_d, v_s5_glu_w, v_s5_glu_b, v_hg_lb_logits, v_hg_norm_w, v_rg_conv_w, v_rg_conv_b, v_rg_wa, v_rg_ba, v_rg_wx, v_rg_bx, v_rg_lambda):
    given = dict(x=x, norm_w=norm_w, final_norm_w=final_norm_w, ffn_gate=ffn_gate, ffn_up=ffn_up, ffn_down=ffn_down, w_in=w_in, branch_proj=branch_proj, w_out=w_out, s5_lambda_re=s5_lambda_re, s5_lambda_im=s5_lambda_im, s5_log_dt=s5_log_dt, s5_b_re=s5_b_re, s5_b_im=s5_b_im, s5_c_re=s5_c_re, s5_c_im=s5_c_im, s5_d=s5_d, s5_glu_w=s5_glu_w, s5_glu_b=s5_glu_b, hg_lb_logits=hg_lb_logits, hg_norm_w=hg_norm_w, rg_conv_w=rg_conv_w, rg_conv_b=rg_conv_b, rg_wa=rg_wa, rg_ba=rg_ba, rg_wx=rg_wx, rg_bx=rg_bx, rg_lambda=rg_lambda, loss_target=loss_target, m_norm_w=m_norm_w, m_final_norm_w=m_final_norm_w, m_ffn_gate=m_ffn_gate, m_ffn_up=m_ffn_up, m_ffn_down=m_ffn_down, m_w_in=m_w_in, m_branch_proj=m_branch_proj, m_w_out=m_w_out, m_s5_lambda_re=m_s5_lambda_re, m_s5_lambda_im=m_s5_lambda_im, m_s5_log_dt=m_s5_log_dt, m_s5_b_re=m_s5_b_re, m_s5_b_im=m_s5_b_im, m_s5_c_re=m_s5_c_re, m_s5_c_im=m_s5_c_im, m_s5_d=m_s5_d, m_s5_glu_w=m_s5_glu_w, m_s5_glu_b=m_s5_glu_b, m_hg_lb_logits=m_hg_lb_logits, m_hg_norm_w=m_hg_norm_w, m_rg_conv_w=m_rg_conv_w, m_rg_conv_b=m_rg_conv_b, m_rg_wa=m_rg_wa, m_rg_ba=m_rg_ba, m_rg_wx=m_rg_wx, m_rg_bx=m_rg_bx, m_rg_lambda=m_rg_lambda, v_norm_w=v_norm_w, v_final_norm_w=v_final_norm_w, v_ffn_gate=v_ffn_gate, v_ffn_up=v_ffn_up, v_ffn_down=v_ffn_down, v_w_in=v_w_in, v_branch_proj=v_branch_proj, v_w_out=v_w_out, v_s5_lambda_re=v_s5_lambda_re, v_s5_lambda_im=v_s5_lambda_im, v_s5_log_dt=v_s5_log_dt, v_s5_b_re=v_s5_b_re, v_s5_b_im=v_s5_b_im, v_s5_c_re=v_s5_c_re, v_s5_c_im=v_s5_c_im, v_s5_d=v_s5_d, v_s5_glu_w=v_s5_glu_w, v_s5_glu_b=v_s5_glu_b, v_hg_lb_logits=v_hg_lb_logits, v_hg_norm_w=v_hg_norm_w, v_rg_conv_w=v_rg_conv_w, v_rg_conv_b=v_rg_conv_b, v_rg_wa=v_rg_wa, v_rg_ba=v_rg_ba, v_rg_wx=v_rg_wx, v_rg_bx=v_rg_bx, v_rg_lambda=v_rg_lambda)
    weights = {n: given[n] for n in TWIN_WEIGHTS}
    shared = {n: given[n] for n in SHARED_INPUTS}
    per_example = {n: given[n] for n in ['x']}
    grad_fn = _jax.value_and_grad(_loss, argnums=(0, 1))

    def one_microbatch(ex, loss_target):
        ex = dict(ex)
        diff = ex.pop(TWIN_DIFF_INPUT)
        return grad_fn(weights, diff, {**shared, **ex}, loss_target)

    if N_MICROBATCH == 1:
        loss, (grad_w, grad_x) = one_microbatch(per_example, given["loss_target"])
    else:
        def body(carry, xs):
            loss_sum, grad_sum = carry
            l_k, (gw_k, gx_k) = one_microbatch(xs[0], xs[1])
            with _jax.named_scope("update"):
                return (loss_sum + l_k, _jax.tree.map(_jnp.add, grad_sum, gw_k)), gx_k

        init = (_jnp.zeros((), _jnp.float32), _jax.tree.map(_jnp.zeros_like, weights))
        (loss, grad_w), grad_x = _jax.lax.scan(body, init, (per_example, given["loss_target"]))
    with _jax.named_scope("update"):
        delta_w, new_m, new_v = {}, {}, {}
        for n in TWIN_WEIGHTS:
            delta_w[n], new_m[n], new_v[n] = _adamw(weights[n], grad_w[n], given["m_" + n], given["v_" + n])
    return (loss, grad_x, *[grad_w[n] for n in TWIN_WEIGHTS], *[delta_w[n] for n in TWIN_WEIGHTS],
            *[new_m[n] for n in TWIN_WEIGHTS], *[new_v[n] for n in TWIN_WEIGHTS])
```

```python
import functools
import math

import jax
import jax.numpy as jnp
from jax import lax
from jax.experimental import pallas as pl
from jax.experimental.pallas import tpu as pltpu

f32 = jnp.float32
bf16 = jnp.bfloat16

D_MODEL = 1024
DEPTH = 2
BRANCH = 512
N_BRANCH = 3
S5_GROUP = 16
S5_GROUPS = 32
S5_STATE = 64
S5_LANES = S5_GROUPS * S5_STATE
S5_EIG_MAX = -1e-4
HG_HEADS = 4
HG_DK = 128
HG_CHUNK = 32
RG_BLOCKS = 8
RG_BLOCK = 64
RG_C = 8.0
D_FF = 2816
EPS = 1e-6
IN_TOTAL = 6656
GM_WIDTH = N_BRANCH * D_MODEL
N_DEV = 8

ADAM_LR = 0.001
ADAM_B1 = 0.9
ADAM_B2 = 0.999
ADAM_EPS = 1e-08
ADAM_WD = 0.01
ADAM_STEP = 10

VMEM_LIMIT_V7X = 56 * 1024 * 1024
ROW_TILE = 256
FF_TILE = 1408


def _cparams(sem):
    return pltpu.CompilerParams(dimension_semantics=sem, vmem_limit_bytes=VMEM_LIMIT_V7X)


def _sigmoid(x):
    return 1.0 / (1.0 + jnp.exp(-x))


_GELU_C = math.sqrt(2.0 / math.pi)


def _gelu(x):
    t = jnp.tanh(_GELU_C * (x + 0.044715 * x * x * x))
    return 0.5 * x * (1.0 + t)


def _gelu_grad(x):
    t = jnp.tanh(_GELU_C * (x + 0.044715 * x * x * x))
    return 0.5 * (1.0 + t) + 0.5 * x * (1.0 - t * t) * _GELU_C * (1.0 + 3.0 * 0.044715 * x * x)


def _expm1(x):
    p = x * (1.0 + x * (0.5 + x * (1.0 / 6 + x * (1.0 / 24 + x * (1.0 / 120 + x * (1.0 / 720))))))
    return jnp.where(jnp.abs(x) < 0.3, p, jnp.exp(x) - 1.0)


def _dot(a, b):
    return jnp.dot(a, b, preferred_element_type=f32)


def _dot_nt(a, b):
    return lax.dot_general(a, b, (((1,), (1,)), ((), ())), preferred_element_type=f32)


def _dot_tn(a, b):
    return lax.dot_general(a, b, (((0,), (0,)), ((), ())), preferred_element_type=f32)


def _rows(shape):
    return lax.broadcasted_iota(jnp.int32, shape, 0)


def _scan_fwd(a, b, n):
    row = _rows(a.shape)
    s = 1
    while s < n:
        valid = row >= s
        sh_a = pltpu.roll(a, s, 0)
        sh_b = pltpu.roll(b, s, 0)
        b = b + a * jnp.where(valid, sh_b, 0.0)
        a = a * jnp.where(valid, sh_a, 1.0)
        s *= 2
    return a, b


def _scan_bwd(a, b, n):
    row = _rows(a.shape)
    s = 1
    while s < n:
        valid = row < n - s
        sh_a = pltpu.roll(a, n - s, 0)
        sh_b = pltpu.roll(b, n - s, 0)
        b = b + a * jnp.where(valid, sh_b, 0.0)
        a = a * jnp.where(valid, sh_a, 1.0)
        s *= 2
    return a, b


def _seg_cumsum(x, n, seg):
    pos = _rows(x.shape) % seg
    s = 1
    while s < seg:
        x = x + jnp.where(pos >= s, pltpu.roll(x, s, 0), 0.0)
        s *= 2
    return x


def _seg_rev_cumsum(x, n, seg):
    pos = _rows(x.shape) % seg
    s = 1
    while s < seg:
        x = x + jnp.where(pos < seg - s, pltpu.roll(x, n - s, 0), 0.0)
        s *= 2
    return x


def _head_mean(x):
    parts = []
    for h in range(HG_HEADS):
        m = jnp.mean(x[:, h * HG_DK:(h + 1) * HG_DK], axis=1, keepdims=True)
        parts.append(jnp.broadcast_to(m, (x.shape[0], HG_DK)))
    return jnp.concatenate(parts, axis=1)


def _mm(name, a_list, b_list, terms, n_acc, mode, m, n, k, tm, tn, tk, out_dtypes, epilogue, extras=()):
    tm, tn, tk = min(tm, m), min(tn, n), min(tk, k)
    assert m % tm == 0 and n % tn == 0 and k % tk == 0, (name, m, n, k, tm, tn, tk)
    gk = k // tk
    if mode == "tn":
        a_spec = pl.BlockSpec((tk, tm), lambda i, j, kk: (kk, i))
    else:
        a_spec = pl.BlockSpec((tm, tk), lambda i, j, kk: (i, kk))
    if mode == "nt":
        b_spec = pl.BlockSpec((tn, tk), lambda i, j, kk: (j, kk))
    else:
        b_spec = pl.BlockSpec((tk, tn), lambda i, j, kk: (kk, j))
    o_spec = pl.BlockSpec((tm, tn), lambda i, j, kk: (i, j))
    dot = {"nn": _dot, "nt": _dot_nt, "tn": _dot_tn}[mode]
    na, nb, ne, no = len(a_list), len(b_list), len(extras), len(out_dtypes)

    def kern(*refs):
        a_refs = refs[:na]
        b_refs = refs[na:na + nb]
        e_refs = refs[na + nb:na + nb + ne]
        o_refs = refs[na + nb + ne:na + nb + ne + no]
        acc = refs[na + nb + ne + no]
        kk = pl.program_id(2)

        @pl.when(kk == 0)
        def _():
            acc[...] = jnp.zeros_like(acc)

        for ai, bi, ci in terms:
            acc[ci] += dot(a_refs[ai][...].astype(bf16), b_refs[bi][...].astype(bf16))

        @pl.when(kk == gk - 1)
        def _():
            outs = epilogue([acc[c] for c in range(n_acc)], [e[...] for e in e_refs])
            for o, val in zip(o_refs, outs):
                o[...] = val.astype(o.dtype)

    res = pl.pallas_call(
        kern, name=name,
        grid=(m // tm, n // tn, gk),
        in_specs=[a_spec] * na + [b_spec] * nb + [o_spec] * ne,
        out_specs=[o_spec] * no,
        out_shape=[jax.ShapeDtypeStruct((m, n), dt) for dt in out_dtypes],
        scratch_shapes=[pltpu.VMEM((n_acc, tm, tn), f32)],
        compiler_params=_cparams(("parallel", "parallel", "arbitrary")),
    )(*a_list, *b_list, *extras)
    return res


def _mm1(name, a, b, mode, m, n, k, tm, tn, tk, out_dtype=f32, scale=None):
    def epi(accs, extras):
        return [accs[0] if scale is None else accs[0] * scale]
    return _mm(name, [a], [b], [(0, 0, 0)], 1, mode, m, n, k, tm, tn, tk, [out_dtype], epi)[0]


def _rt(name, body, rows, tm, row_ins, consts, row_outs, acc_outs=(), scratch=(), reverse=False):
    tm = min(tm, rows)
    assert rows % tm == 0
    nt = rows // tm

    def tile(i):
        return nt - 1 - i if reverse else i

    in_specs, args = [], []
    for spec in row_ins:
        arr = spec[0]
        if isinstance(spec[1], int):
            in_specs.append(pl.BlockSpec((tm, spec[1]), lambda i, cb=spec[2]: (tile(i), cb)))
        else:
            in_specs.append(pl.BlockSpec(spec[1], lambda i, fn=spec[2]: fn(tile(i))))
        args.append(arr)
    for c in consts:
        in_specs.append(pl.BlockSpec(c.shape, lambda i, nd=c.ndim: (0,) * nd))
        args.append(c)
    out_specs, out_shape = [], []
    for spec in row_outs:
        if isinstance(spec[0], int):
            out_specs.append(pl.BlockSpec((tm, spec[0]), lambda i: (tile(i), 0)))
            out_shape.append(jax.ShapeDtypeStruct((rows, spec[0]), spec[1]))
        else:
            out_specs.append(pl.BlockSpec(spec[1], lambda i, fn=spec[2]: fn(tile(i))))
            out_shape.append(jax.ShapeDtypeStruct(spec[0], spec[3]))
    for shp in acc_outs:
        out_specs.append(pl.BlockSpec(shp, lambda i, nd=len(shp): (0,) * nd))
        out_shape.append(jax.ShapeDtypeStruct(shp, f32))
    n_in = len(args)
    n_row_out = len(row_outs)
    n_acc = len(acc_outs)

    def kern(*refs):
        i = pl.program_id(0)
        acc_refs = refs[n_in + n_row_out:n_in + n_row_out + n_acc]

        @pl.when(i == 0)
        def _():
            for r in acc_refs:
                r[...] = jnp.zeros_like(r)

        body(i, *refs)

    return pl.pallas_call(
        kern, name=name, grid=(nt,), in_specs=in_specs, out_specs=out_specs, out_shape=out_shape,
        scratch_shapes=list(scratch), compiler_params=_cparams(("arbitrary",)),
    )(*args)


def _rms_fwd(name, x, w, rows):
    def body(i, x_ref, w_ref, h_ref):
        xv = x_ref[...]
        r = lax.rsqrt(jnp.mean(xv * xv, axis=1, keepdims=True) + EPS)
        h_ref[...] = (xv * r * w_ref[...]).astype(bf16)
    return _rt(name, body, rows, ROW_TILE, [(x, D_MODEL, 0)], [w], [(D_MODEL, bf16)])[0]


def _rms_bwd(name, x, dh, w, dres, rows):
    def body(i, x_ref, dh_ref, dres_ref, w_ref, dx_ref, dxb_ref, dw_ref):
        xv = x_ref[...]
        r = lax.rsqrt(jnp.mean(xv * xv, axis=1, keepdims=True) + EPS)
        xn = xv * r
        dhv = dh_ref[...]
        dxn = dhv * w_ref[...]
        dx = dres_ref[...] + r * (dxn - xn * jnp.mean(dxn * xn, axis=1, keepdims=True))
        dx_ref[...] = dx
        dxb_ref[...] = dx.astype(bf16)
        dw_ref[...] += jnp.sum(dhv * xn, axis=0, keepdims=True)
    return _rt(name, body, rows, ROW_TILE, [(x, D_MODEL, 0), (dh, D_MODEL, 0), (dres, D_MODEL, 0)], [w],
               [(D_MODEL, f32), (D_MODEL, bf16)], acc_outs=[(1, D_MODEL)])


def _loss_head(x, w, target, rows):
    def body(i, x_ref, t_ref, w_ref, dx_ref, dxb_ref, loss_ref, dw_ref):
        xv = x_ref[...]
        r = lax.rsqrt(jnp.mean(xv * xv, axis=1, keepdims=True) + EPS)
        xn = xv * r
        wv = w_ref[...]
        err = xn * wv - t_ref[...]
        part = 0.5 * jnp.sum(jnp.mean(err * err, axis=1, keepdims=True), axis=0, keepdims=True)
        loss_ref[...] += jnp.broadcast_to(part, (1, 128))
        dy = err * (1.0 / D_MODEL)
        dxn = dy * wv
        dx = r * (dxn - xn * jnp.mean(dxn * xn, axis=1, keepdims=True))
        dx_ref[...] = dx
        dxb_ref[...] = dx.astype(bf16)
        dw_ref[...] += jnp.sum(dy * xn, axis=0, keepdims=True)
    return _rt("loss_head", body, rows, ROW_TILE, [(x, D_MODEL, 0), (target, D_MODEL, 0)], [w],
               [(D_MODEL, f32), (D_MODEL, bf16)], acc_outs=[(1, 128), (1, D_MODEL)])


def _ffn_fwd(tag, x, nw, wg, wu, wd, rows):
    hb = _rms_fwd("ffn_norm_" + tag, x, nw, rows)

    def epi_up(accs, extras):
        a, b = accs
        return [a, b, a * _sigmoid(a) * b]
    a, b, s = _mm("ffn_up_" + tag, [hb], [wg, wu], [(0, 0, 0), (0, 1, 1)], 2, "nn", rows, D_FF, D_MODEL,
                  512, FF_TILE, D_MODEL, [bf16, bf16, bf16], epi_up)

    def epi_down(accs, extras):
        return [extras[0] + 0.5 * accs[0]]
    x_out = _mm("ffn_down_" + tag, [s], [wd], [(0, 0, 0)], 1, "nn", rows, D_MODEL, D_FF,
                512, D_MODEL, FF_TILE, [f32], epi_down, extras=[x])[0]
    return x_out, (x, hb, a, b, s)


def _ffn_bwd(tag, saved, nw, wg, wu, wd, dx, dxb, rows):
    x, hb, a, b, s = saved

    def epi_mid(accs, extras):
        ds = 0.5 * accs[0]
        av = extras[0].astype(f32)
        bv = extras[1].astype(f32)
        sg = _sigmoid(av)
        return [ds * bv * sg * (1.0 + av * (1.0 - sg)), ds * av * sg]
    da, db = _mm("ffn_bwd_mid_" + tag, [dxb], [wd], [(0, 0, 0)], 1, "nt", rows, D_FF, D_MODEL,
                 512, FF_TILE, D_MODEL, [bf16, bf16], epi_mid, extras=[a, b])
    d_wd = _mm1("ffn_dwd_" + tag, s, dxb, "tn", D_FF, D_MODEL, rows, FF_TILE, D_MODEL, 512, scale=0.5)
    d_wg = _mm1("ffn_dwg_" + tag, hb, da, "tn", D_MODEL, D_FF, rows, D_MODEL, FF_TILE, 512)
    d_wu = _mm1("ffn_dwu_" + tag, hb, db, "tn", D_MODEL, D_FF, rows, D_MODEL, FF_TILE, 512)
    dh = _mm("ffn_dh_" + tag, [da, db], [wg, wu], [(0, 0, 0), (1, 1, 0)], 1, "nt", rows, D_MODEL, D_FF,
             512, D_MODEL, FF_TILE, [f32], lambda accs, extras: [accs[0]])[0]
    dx_in, dxb_in, d_nw = _rms_bwd("ffn_norm_bwd_" + tag, x, dh, nw, dx, rows)
    return dx_in, dxb_in, d_nw, d_wg, d_wu, d_wd


S5_CB = 512
U_COL = GM_WIDTH // BRANCH


def _s5_scan_fwd(tag, proj, b_re, b_im, a_re, a_im, rows):
    tm = min(ROW_TILE, rows)
    nt = rows // tm
    nc = S5_LANES // S5_CB

    def kern(u_ref, bre_ref, bim_ref, ar_ref, ai_ref, xr_ref, xi_ref, pr_s, pi_s, cr_s, ci_s):
        t = pl.program_id(1)
        row = _rows((tm, S5_CB))

        @pl.when(t == 0)
        def _():
            pr = jnp.broadcast_to(ar_ref[...], (tm, S5_CB))
            pi = jnp.broadcast_to(ai_ref[...], (tm, S5_CB))
            s = 1
            while s < tm:
                sr = pltpu.roll(pr, s, 0)
                si = pltpu.roll(pi, s, 0)
                valid = row >= s
                pr, pi = jnp.where(valid, pr * sr - pi * si, pr), jnp.where(valid, pr * si + pi * sr, pi)
                s *= 2
            pr_s[...] = pr
            pi_s[...] = pi
            cr_s[...] = jnp.zeros_like(cr_s)
            ci_s[...] = jnp.zeros_like(ci_s)

        ub = u_ref[...].astype(bf16)
        br = _dot(ub, bre_ref[...])
        bi = _dot(ub, bim_ref[...])
        s = 1
        while s < tm:
            mr = pr_s[s - 1:s, :]
            mi = pi_s[s - 1:s, :]
            sr = pltpu.roll(br, s, 0)
            si = pltpu.roll(bi, s, 0)
            valid = row >= s
            br, bi = (br + jnp.where(valid, mr * sr - mi * si, 0.0),
                      bi + jnp.where(valid, mr * si + mi * sr, 0.0))
            s *= 2
        cr = cr_s[...]
        ci = ci_s[...]
        pr = pr_s[...]
        pi = pi_s[...]
        xr = br + pr * cr - pi * ci
        xi = bi + pr * ci + pi * cr
        xr_ref[...] = xr
        xi_ref[...] = xi
        cr_s[...] = xr[tm - 1:tm, :]
        ci_s[...] = xi[tm - 1:tm, :]

    return pl.pallas_call(
        kern, name="s5_scan_fwd_" + tag, grid=(nc, nt),
        in_specs=[pl.BlockSpec((tm, BRANCH), lambda c, t: (t, U_COL)),
                  pl.BlockSpec((BRANCH, S5_CB), lambda c, t: (0, c)),
                  pl.BlockSpec((BRANCH, S5_CB), lambda c, t: (0, c)),
                  pl.BlockSpec((1, S5_CB), lambda c, t: (0, c)),
                  pl.BlockSpec((1, S5_CB), lambda c, t: (0, c))],
        out_specs=[pl.BlockSpec((tm, S5_CB), lambda c, t: (t, c))] * 2,
        out_shape=[jax.ShapeDtypeStruct((rows, S5_LANES), f32)] * 2,
        scratch_shapes=[pltpu.VMEM((tm, S5_CB), f32), pltpu.VMEM((tm, S5_CB), f32),
                        pltpu.VMEM((1, S5_CB), f32), pltpu.VMEM((1, S5_CB), f32)],
        compiler_params=_cparams(("parallel", "arbitrary")),
    )(proj, b_re, b_im, a_re, a_im)


def _s5_scan_bwd(tag, dxr, dxi, xr, xi, a_re, a_im, rows):
    tm = min(ROW_TILE, rows)
    nt = rows // tm
    nc = S5_LANES // S5_CB

    def kern(dxr_ref, dxi_ref, xr_ref, xi_ref, ar_ref, ai_ref, gr_ref, gi_ref, dar_ref, dai_ref,
             qr_s, qi_s, cr_s, ci_s):
        t = pl.program_id(1)
        row = _rows((tm, S5_CB))

        @pl.when(t == 0)
        def _():
            qr = jnp.broadcast_to(ar_ref[...], (tm, S5_CB))
            qi = jnp.broadcast_to(-ai_ref[...], (tm, S5_CB))
            s = 1
            while s < tm:
                sr = pltpu.roll(qr, tm - s, 0)
                si = pltpu.roll(qi, tm - s, 0)
                valid = row < tm - s
                qr, qi = jnp.where(valid, qr * sr - qi * si, qr), jnp.where(valid, qr * si + qi * sr, qi)
                s *= 2
            qr_s[...] = qr
            qi_s[...] = qi
            cr_s[...] = jnp.zeros_like(cr_s)
            ci_s[...] = jnp.zeros_like(ci_s)
            dar_ref[...] = jnp.zeros_like(dar_ref)
            dai_ref[...] = jnp.zeros_like(dai_ref)

        br = dxr_ref[...]
        bi = dxi_ref[...]
        s = 1
        while s < tm:
            mr = qr_s[tm - s:tm - s + 1, :]
            mi = qi_s[tm - s:tm - s + 1, :]
            sr = pltpu.roll(br, tm - s, 0)
            si = pltpu.roll(bi, tm - s, 0)
            valid = row < tm - s
            br, bi = (br + jnp.where(valid, mr * sr - mi * si, 0.0),
                      bi + jnp.where(valid, mr * si + mi * sr, 0.0))
            s *= 2
        cr = cr_s[...]
        ci = ci_s[...]
        qr = qr_s[...]
        qi = qi_s[...]
        gr = br + qr * cr - qi * ci
        gi = bi + qr * ci + qi * cr
        gr_ref[...] = gr.astype(bf16)
        gi_ref[...] = gi.astype(bf16)
        last = row == tm - 1
        gnr = jnp.where(last, cr, pltpu.roll(gr, tm - 1, 0))
        gni = jnp.where(last, ci, pltpu.roll(gi, tm - 1, 0))
        xr_v = xr_ref[...]
        xi_v = xi_ref[...]
        dar_ref[...] += jnp.sum(gnr * xr_v + gni * xi_v, axis=0, keepdims=True)
        dai_ref[...] += jnp.sum(gni * xr_v - gnr * xi_v, axis=0, keepdims=True)
        cr_s[...] = gr[0:1, :]
        ci_s[...] = gi[0:1, :]

    rev = lambda c, t: (nt - 1 - t, c)
    return pl.pallas_call(
        kern, name="s5_scan_bwd_" + tag, grid=(nc, nt),
        in_specs=[pl.BlockSpec((tm, S5_CB), rev)] * 4 + [pl.BlockSpec((1, S5_CB), lambda c, t: (0, c))] * 2,
        out_specs=[pl.BlockSpec((tm, S5_CB), rev)] * 2 + [pl.BlockSpec((1, S5_CB), lambda c, t: (0, c))] * 2,
        out_shape=[jax.ShapeDtypeStruct((rows, S5_LANES), bf16)] * 2 + [jax.ShapeDtypeStruct((1, S5_LANES), f32)] * 2,
        scratch_shapes=[pltpu.VMEM((tm, S5_CB), f32), pltpu.VMEM((tm, S5_CB), f32),
                        pltpu.VMEM((1, S5_CB), f32), pltpu.VMEM((1, S5_CB), f32)],
        compiler_params=_cparams(("parallel", "arbitrary")),
    )(dxr, dxi, xr, xi, a_re, a_im)


def _s5_fwd(tag, proj, cst, rows):
    xr, xi = _s5_scan_fwd(tag, proj, cst["b_re"].astype(bf16), cst["b_im"].astype(bf16), cst["a_re"], cst["a_im"], rows)

    def body(i, xr_ref, xi_ref, u_ref, cre_ref, cim_ref, d_ref, gw_ref, gb_ref, y_ref, out_ref):
        y = (_dot(xr_ref[...].astype(bf16), cre_ref[...]) + _dot(xi_ref[...].astype(bf16), cim_ref[...])
             + d_ref[...] * u_ref[...])
        y_ref[...] = y
        z = _gelu(y)
        zg = _dot(z.astype(bf16), gw_ref[...]) + gb_ref[...]
        out_ref[...] = (z * _sigmoid(zg)).astype(bf16)

    y, out = _rt("s5_out_" + tag, body, rows, ROW_TILE,
                 [(xr, S5_LANES, 0), (xi, S5_LANES, 0), (proj, BRANCH, U_COL)],
                 [cst["c_re"].astype(bf16), cst["c_im"].astype(bf16), cst["s5_d"], cst["glu_w"], cst["glu_b"]],
                 [(BRANCH, f32), (BRANCH, bf16)])
    return out, (xr, xi, y)


def _s5_bwd(tag, saved, proj, cst, d_out, rows):
    xr, xi, y = saved
    c_re = cst["c_re"].astype(bf16)
    c_im = cst["c_im"].astype(bf16)

    def body(i, do_ref, y_ref, u_ref, xr_ref, xi_ref, cre_ref, cim_ref, gw_ref, gb_ref,
             dxr_ref, dxi_ref, dy_ref, dgw_ref, dgb_ref, dd_ref, dcre_ref, dcim_ref):
        yv = y_ref[...]
        z = _gelu(yv)
        zb = z.astype(bf16)
        gt = _sigmoid(_dot(zb, gw_ref[...]) + gb_ref[...])
        dov = do_ref[...]
        dzg = dov * z * gt * (1.0 - gt)
        dzgb = dzg.astype(bf16)
        dz = dov * gt + _dot_nt(dzgb, gw_ref[...])
        dgw_ref[...] += _dot_tn(zb, dzgb)
        dgb_ref[...] += jnp.sum(dzg, axis=0, keepdims=True)
        dy = dz * _gelu_grad(yv)
        dy_ref[...] = dy
        dd_ref[...] += jnp.sum(dy * u_ref[...], axis=0, keepdims=True)
        dyb = dy.astype(bf16)
        dxr_ref[...] = _dot_nt(dyb, cre_ref[...])
        dxi_ref[...] = _dot_nt(dyb, cim_ref[...])
        dcre_ref[...] += _dot_tn(xr_ref[...].astype(bf16), dyb)
        dcim_ref[...] += _dot_tn(xi_ref[...].astype(bf16), dyb)

    dxr, dxi, dy, d_gw, d_gb, d_d, d_cre, d_cim = _rt(
        "s5_out_bwd_" + tag, body, rows, ROW_TILE,
        [(d_out, BRANCH, 0), (y, BRANCH, 0), (proj, BRANCH, U_COL), (xr, S5_LANES, 0), (xi, S5_LANES, 0)],
        [c_re, c_im, cst["glu_w"], cst["glu_b"]],
        [(S5_LANES, f32), (S5_LANES, f32), (BRANCH, f32)],
        acc_outs=[(BRANCH, BRANCH), (1, BRANCH), (1, BRANCH), (S5_LANES, BRANCH), (S5_LANES, BRANCH)])

    gr, gi, d_ar, d_ai = _s5_scan_bwd(tag, dxr, dxi, xr, xi, cst["a_re"], cst["a_im"], rows)
    b_re = cst["b_re"].astype(bf16)
    b_im = cst["b_im"].astype(bf16)

    def body_in(i, gr_ref, gi_ref, dy_ref, u_ref, bre_ref, bim_ref, d_ref, du_ref, dbre_ref, dbim_ref):
        grv = gr_ref[...]
        giv = gi_ref[...]
        du = _dot_nt(grv, bre_ref[...]) + _dot_nt(giv, bim_ref[...]) + dy_ref[...] * d_ref[...]
        du_ref[...] = du.astype(bf16)
        ub = u_ref[...].astype(bf16)
        dbre_ref[...] += _dot_tn(ub, grv)
        dbim_ref[...] += _dot_tn(ub, giv)

    du, d_bre, d_bim = _rt("s5_in_bwd_" + tag, body_in, rows, ROW_TILE,
                           [(gr, S5_LANES, 0), (gi, S5_LANES, 0), (dy, BRANCH, 0), (proj, BRANCH, U_COL)],
                           [b_re, b_im, cst["s5_d"]], [(BRANCH, bf16)],
                           acc_outs=[(BRANCH, S5_LANES), (BRANCH, S5_LANES)])
    dcst = {"b_re": d_bre, "b_im": d_bim, "a_re": d_ar, "a_im": d_ai, "c_re": d_cre, "c_im": d_cim,
            "s5_d": d_d, "glu_b": d_gb}
    return du, dcst, d_gw


def _hg_prep(q, z, lb):
    qs = _sigmoid(q)
    qh = q * qs
    sg = _sigmoid(z)
    fg = lb + (1.0 - lb) * sg
    kk = (1.0 - lb) * (1.0 - sg)
    return qs, qh, sg, fg, kk


def _hg_fwd(tag, proj, cst, rows):
    tm = min(ROW_TILE, rows)
    c_sz = HG_CHUNK
    nch = tm // c_sz
    n_chunks = rows // c_sz

    def body(i, q_ref, z_ref, v_ref, g_ref, lb_ref, nw_ref, out_ref, o_ref, ss_ref, sn_ref,
             st_s, qh_s, kh_s, vb_s, b_s, k_s):
        @pl.when(i == 0)
        def _():
            st_s[...] = jnp.zeros_like(st_s)

        lb = lb_ref[...]
        _, qh, sg, fg, kk = _hg_prep(q_ref[...], z_ref[...], lb)
        b = _seg_cumsum(jnp.log(fg), tm, c_sz)
        b_s[...] = b
        k_s[...] = kk
        qh_s[...] = (qh * jnp.exp(b)).astype(bf16)
        kh_s[...] = (kk * jnp.exp(-b)).astype(bf16)
        vb_s[...] = v_ref[...].astype(bf16)
        tril = _rows((c_sz, c_sz)) >= lax.broadcasted_iota(jnp.int32, (c_sz, c_sz), 1)

        def chunk(ci, carry):
            sl = pl.ds(pl.multiple_of(ci * c_sz, c_sz), c_sz)
            for h in range(HG_HEADS):
                hl = slice(h * HG_DK, (h + 1) * HG_DK)
                qb = qh_s[sl, hl]
                kb = kh_s[sl, hl]
                vb = vb_s[sl, hl]
                a_mat = jnp.where(tril, _dot_nt(qb, kb), 0.0)
                st = st_s[hl, :]
                stb = st.astype(bf16)
                o_ref[sl, hl] = _dot_nt(qb, stb) + _dot(a_mat.astype(bf16), vb)
                ss_ref[ci, hl, :] = stb
                bb = b_s[sl, hl]
                bl = bb[c_sz - 1:c_sz, :]
                kd = (k_s[sl, hl] * jnp.exp(bl - bb)).astype(bf16)
                st_new = st * jnp.exp(bl) + _dot_tn(vb, kd)
                st_s[hl, :] = st_new
                sn_ref[ci, hl, :] = st_new.astype(bf16)
            return carry

        lax.fori_loop(0, nch, chunk, 0)
        o = o_ref[...]
        r = lax.rsqrt(_head_mean(o * o) + EPS)
        g = g_ref[...]
        out_ref[...] = (o * r * nw_ref[...] * (g * _sigmoid(g))).astype(bf16)

    out, o, ss, sn = _rt(
        "hg_fwd_" + tag, body, rows, tm,
        [(proj, BRANCH, U_COL + 1), (proj, BRANCH, U_COL + 2), (proj, BRANCH, U_COL + 3), (proj, BRANCH, U_COL + 4)],
        [cst["hg_lb"], cst["hg_nw"]],
        [(BRANCH, bf16), (BRANCH, f32),
         ((n_chunks, BRANCH, HG_DK), (nch, BRANCH, HG_DK), lambda t: (t, 0, 0), bf16),
         ((n_chunks, BRANCH, HG_DK), (nch, BRANCH, HG_DK), lambda t: (t, 0, 0), bf16)],
        scratch=[pltpu.VMEM((BRANCH, HG_DK), f32), pltpu.VMEM((tm, BRANCH), bf16), pltpu.VMEM((tm, BRANCH), bf16),
                 pltpu.VMEM((tm, BRANCH), bf16), pltpu.VMEM((tm, BRANCH), f32), pltpu.VMEM((tm, BRANCH), f32)])
    return out, (o, ss, sn)


def _hg_bwd(tag, saved, proj, cst, d_out, rows):
    o_saved, ss, sn = saved
    tm = min(ROW_TILE, rows)
    c_sz = HG_CHUNK
    nch = tm // c_sz

    def body(i, do_ref, q_ref, z_ref, v_ref, g_ref, o_ref, ss_ref, sn_ref, lb_ref, nw_ref,
             dq_ref, dz_ref, dv_ref, dg_ref, dlb_ref, dnw_ref,
             dst_s, flux_s, qh_s, kh_s, vb_s, b_s, k_s, dob_s, dqh_s, dk_s, db_s):
        @pl.when(i == 0)
        def _():
            dst_s[...] = jnp.zeros_like(dst_s)

        lb = lb_ref[...]
        q = q_ref[...]
        qs, qh, sg, fg, kk = _hg_prep(q, z_ref[...], lb)
        b = _seg_cumsum(jnp.log(fg), tm, c_sz)
        b_s[...] = b
        k_s[...] = kk
        qh_s[...] = (qh * jnp.exp(b)).astype(bf16)
        kh_s[...] = (kk * jnp.exp(-b)).astype(bf16)
        vb_s[...] = v_ref[...].astype(bf16)
        g = g_ref[...]
        gs = _sigmoid(g)
        o = o_ref[...]
        r = lax.rsqrt(_head_mean(o * o) + EPS)
        oh = o * r
        nw = nw_ref[...]
        dov = do_ref[...]
        don = dov * (g * gs)
        dg_ref[...] = (dov * oh * nw * (gs * (1.0 + g * (1.0 - gs)))).astype(bf16)
        dnw_ref[...] += jnp.sum(don * oh, axis=0, keepdims=True)
        doh = don * nw
        d_o = r * (doh - oh * _head_mean(doh * oh))
        dob_s[...] = d_o.astype(bf16)
        tril = _rows((c_sz, c_sz)) >= lax.broadcasted_iota(jnp.int32, (c_sz, c_sz), 1)

        def chunk(cj, carry):
            ci = nch - 1 - cj
            sl = pl.ds(pl.multiple_of(ci * c_sz, c_sz), c_sz)
            for h in range(HG_HEADS):
                hl = slice(h * HG_DK, (h + 1) * HG_DK)
                qb = qh_s[sl, hl]
                kb = kh_s[sl, hl]
                vb = vb_s[sl, hl]
                dob = dob_s[sl, hl]
                stb = ss_ref[ci, hl, :]
                dst = dst_s[hl, :]
                dstb = dst.astype(bf16)
                a_mat = jnp.where(tril, _dot_nt(qb, kb), 0.0).astype(bf16)
                da_mat = jnp.where(tril, _dot_nt(dob, vb), 0.0).astype(bf16)
                bb = b_s[sl, hl]
                bl = bb[c_sz - 1:c_sz, :]
                ebl = jnp.exp(bl - bb)
                dqhat = _dot(dob, stb) + _dot(da_mat, kb)
                dkhat = _dot_tn(da_mat, qb)
                dk_inter = _dot(vb, dstb) * ebl
                kv = k_s[sl, hl]
                dqh_s[sl, hl] = dqhat * jnp.exp(bb)
                dk_s[sl, hl] = dkhat * jnp.exp(-bb) + dk_inter
                db_s[sl, hl] = qb.astype(f32) * dqhat - kb.astype(f32) * dkhat - kv * dk_inter
                flux = jnp.sum(sn_ref[ci, hl, :].astype(f32) * dst, axis=0, keepdims=True)
                flux_s[sl, hl] = jnp.broadcast_to(flux, (c_sz, HG_DK))
                kd = (kv * ebl).astype(bf16)
                dv_ref[sl, hl] = (_dot_tn(a_mat, dob) + _dot_nt(kd, dstb)).astype(bf16)
                dst_s[hl, :] = dst * jnp.exp(bl) + _dot_tn(dob, qb)
            return carry

        lax.fori_loop(0, nch, chunk, 0)
        dqh = dqh_s[...]
        dk = dk_s[...]
        dlf = _seg_rev_cumsum(db_s[...], tm, c_sz) + flux_s[...]
        tt = (1.0 - lb) * sg * (1.0 - sg)
        dz_ref[...] = (dlf * tt / fg - dk * tt).astype(bf16)
        dlb_ref[...] += jnp.sum(dlf * (1.0 - sg) / fg - dk * (1.0 - sg), axis=0, keepdims=True)
        dq_ref[...] = (dqh * (qs * (1.0 + q * (1.0 - qs)))).astype(bf16)

    dq, dz, dv, dg, d_lb, d_nw = _rt(
        "hg_bwd_" + tag, body, rows, tm,
        [(d_out, BRANCH, 0), (proj, BRANCH, U_COL + 1), (proj, BRANCH, U_COL + 2), (proj, BRANCH, U_COL + 3),
         (proj, BRANCH, U_COL + 4), (o_saved, BRANCH, 0), (ss, (nch, BRANCH, HG_DK), lambda t: (t, 0, 0)),
         (sn, (nch, BRANCH, HG_DK), lambda t: (t, 0, 0))],
        [cst["hg_lb"], cst["hg_nw"]],
        [(BRANCH, bf16)] * 4, acc_outs=[(1, BRANCH), (1, BRANCH)],
        scratch=[pltpu.VMEM((BRANCH, HG_DK), f32), pltpu.VMEM((tm, BRANCH), f32),
                 pltpu.VMEM((tm, BRANCH), bf16), pltpu.VMEM((tm, BRANCH), bf16), pltpu.VMEM((tm, BRANCH), bf16),
                 pltpu.VMEM((tm, BRANCH), f32), pltpu.VMEM((tm, BRANCH), f32), pltpu.VMEM((tm, BRANCH), bf16),
                 pltpu.VMEM((tm, BRANCH), f32), pltpu.VMEM((tm, BRANCH), f32), pltpu.VMEM((tm, BRANCH), f32)],
        reverse=True)
    return dq, dz, dv, dg, {"hg_lb": d_lb, "hg_nw": d_nw}


def _rg_gates(xc, wa_ref, ba_ref, wx_ref, bx_ref, sp8):
    xcb = xc.astype(bf16)
    r = _sigmoid(_dot(xcb, wa_ref[...]) + ba_ref[...])
    ig = _sigmoid(_dot(xcb, wx_ref[...]) + bx_ref[...])
    la = -sp8 * r
    a = jnp.exp(la)
    mult = jnp.sqrt(-_expm1(2.0 * la))
    return xcb, r, ig, a, mult


def _rg_fwd(tag, proj, cst, rows):
    tm = min(ROW_TILE, rows)

    def body(i, xb_ref, gate_ref, cw_ref, cb_ref, wa_ref, ba_ref, wx_ref, bx_ref, sp_ref,
             out_ref, xc_ref, h_ref, hp_ref, prev_s, hc_s):
        @pl.when(i == 0)
        def _():
            prev_s[...] = jnp.zeros_like(prev_s)
            hc_s[...] = jnp.zeros_like(hc_s)

        row = _rows((tm, BRANCH))
        xb = xb_ref[...]
        prev = prev_s[...]
        xc = cb_ref[...] + cw_ref[3:4, :] * xb
        for j in range(1, 4):
            sh = jnp.where(row >= j, pltpu.roll(xb, j, 0), pltpu.roll(prev, j, 0))
            xc = xc + cw_ref[3 - j:4 - j, :] * sh
        prev_s[...] = xb
        xc_ref[...] = xc
        _, r, ig, a, mult = _rg_gates(xc, wa_ref, ba_ref, wx_ref, bx_ref, sp_ref[...])
        a_cum, h_loc = _scan_fwd(a, mult * ig * xc, tm)
        hc = hc_s[...]
        h = h_loc + a_cum * hc
        h_ref[...] = h
        hp_ref[...] = jnp.where(row >= 1, pltpu.roll(h, 1, 0), hc)
        hc_s[...] = h[tm - 1:tm, :]
        out_ref[...] = (h * _gelu(gate_ref[...])).astype(bf16)

    out, xc, h, hp = _rt(
        "rg_fwd_" + tag, body, rows, tm,
        [(proj, BRANCH, U_COL + 5), (proj, BRANCH, U_COL + 6)],
        [cst["rg_cw"], cst["rg_cb"], cst["rg_wa"].astype(bf16), cst["rg_ba"], cst["rg_wx"].astype(bf16),
         cst["rg_bx"], cst["rg_sp8"]],
        [(BRANCH, bf16), (BRANCH, f32), (BRANCH, f32), (BRANCH, f32)],
        scratch=[pltpu.VMEM((tm, BRANCH), f32), pltpu.VMEM((1, BRANCH), f32)])
    return out, (xc, h, hp)


def _rg_bwd(tag, saved, proj, cst, d_out, rows):
    xc_saved, h_saved, hp_saved = saved
    tm = min(ROW_TILE, rows)

    def body(i, do_ref, xb_ref, gate_ref, xc_ref, h_ref, hp_ref, cw_ref, wa_ref, ba_ref, wx_ref, bx_ref, sp_ref,
             dxb_ref, dgate_ref, dcw_ref, dcb_ref, dwa_ref, dba_ref, dwx_ref, dbx_ref, dsp_ref,
             nxt_s, ec_s):
        @pl.when(i == 0)
        def _():
            nxt_s[...] = jnp.zeros_like(nxt_s)
            ec_s[...] = jnp.zeros_like(ec_s)

        row = _rows((tm, BRANCH))
        xc = xc_ref[...]
        sp8 = sp_ref[...]
        xcb, r, ig, a, mult = _rg_gates(xc, wa_ref, ba_ref, wx_ref, bx_ref, sp8)
        gate = gate_ref[...]
        dov = do_ref[...]
        dh = dov * _gelu(gate)
        dgate_ref[...] = (dov * h_ref[...] * _gelu_grad(gate)).astype(bf16)
        a_cum, e_loc = _scan_bwd(a, a * dh, tm)
        ec = ec_s[...]
        e = e_loc + a_cum * ec
        g_tot = dh + jnp.where(row == tm - 1, ec, pltpu.roll(e, tm - 1, 0))
        ec_s[...] = e[0:1, :]
        d_a = g_tot * hp_ref[...]
        d_mult = g_tot * ig * xc
        d_ix = g_tot * mult
        d_ig = d_ix * xc
        d_xc = d_ix * ig
        d_la = d_a * a - d_mult * (a * a) / mult
        d_r = -d_la * sp8
        dsp_ref[...] += jnp.sum(-d_la * r, axis=0, keepdims=True)
        dzr = d_r * r * (1.0 - r)
        dzi = d_ig * ig * (1.0 - ig)
        dzrb = dzr.astype(bf16)
        dzib = dzi.astype(bf16)
        d_xc = d_xc + _dot_nt(dzrb, wa_ref[...]) + _dot_nt(dzib, wx_ref[...])
        dwa_ref[...] += _dot_tn(xcb, dzrb)
        dwx_ref[...] += _dot_tn(xcb, dzib)
        dba_ref[...] += jnp.sum(dzr, axis=0, keepdims=True)
        dbx_ref[...] += jnp.sum(dzi, axis=0, keepdims=True)
        dcb_ref[...] += jnp.sum(d_xc, axis=0, keepdims=True)
        nxt = nxt_s[...]
        xb = xb_ref[...]
        dxb = cw_ref[3:4, :] * d_xc
        dcw_ref[3:4, :] += jnp.sum(d_xc * xb, axis=0, keepdims=True)
        for j in range(1, 4):
            sh = jnp.where(row < tm - j, pltpu.roll(d_xc, tm - j, 0), pltpu.roll(nxt, tm - j, 0))
            dxb = dxb + cw_ref[3 - j:4 - j, :] * sh
            dcw_ref[3 - j:4 - j, :] += jnp.sum(sh * xb, axis=0, keepdims=True)
        nxt_s[...] = d_xc
        dxb_ref[...] = dxb.astype(bf16)

    wa = cst["rg_wa"].astype(bf16)
    wx = cst["rg_wx"].astype(bf16)
    dxb, dgate, d_cw, d_cb, d_wa, d_ba, d_wx, d_bx, d_sp = _rt(
        "rg_bwd_" + tag, body, rows, tm,
        [(d_out, BRANCH, 0), (proj, BRANCH, U_COL + 5), (proj, BRANCH, U_COL + 6), (xc_saved, BRANCH, 0),
         (h_saved, BRANCH, 0), (hp_saved, BRANCH, 0)],
        [cst["rg_cw"], wa, cst["rg_ba"], wx, cst["rg_bx"], cst["rg_sp8"]],
        [(BRANCH, bf16), (BRANCH, bf16)],
        acc_outs=[(4, BRANCH), (1, BRANCH), (BRANCH, BRANCH), (1, BRANCH), (BRANCH, BRANCH), (1, BRANCH), (1, BRANCH)],
        scratch=[pltpu.VMEM((tm, BRANCH), f32), pltpu.VMEM((1, BRANCH), f32)],
        reverse=True)
    dcst = {"rg_cw": d_cw, "rg_cb": d_cb, "rg_wa": d_wa, "rg_ba": d_ba, "rg_wx": d_wx, "rg_bx": d_bx, "rg_sp8": d_sp}
    return dxb, dgate, dcst


def _merge_fwd(tag, proj, outs, bp, rows):
    def body(i, ya_ref, yb_ref, yc_ref, gm_ref, p_ref, m_ref):
        acc = None
        for n, y_ref in enumerate((ya_ref, yb_ref, yc_ref)):
            up = _dot(y_ref[...], p_ref[n])
            term = _sigmoid(gm_ref[:, n * D_MODEL:(n + 1) * D_MODEL]) * up
            acc = term if acc is None else acc + term
        m_ref[...] = acc.astype(bf16)
    return _rt("merge_fwd_" + tag, body, rows, ROW_TILE,
               [(outs[0], BRANCH, 0), (outs[1], BRANCH, 0), (outs[2], BRANCH, 0), (proj, GM_WIDTH, 0)],
               [bp], [(D_MODEL, bf16)])[0]


def _merge_bwd(tag, proj, outs, bp, dmerged, rows):
    def body(i, dm_ref, ya_ref, yb_ref, yc_ref, gm_ref, p_ref, da_ref, db_ref, dc_ref, dgm_ref, dp_ref):
        dm = dm_ref[...]
        for n, (y_ref, dy_ref) in enumerate(((ya_ref, da_ref), (yb_ref, db_ref), (yc_ref, dc_ref))):
            yv = y_ref[...]
            up = _dot(yv, p_ref[n])
            gt = _sigmoid(gm_ref[:, n * D_MODEL:(n + 1) * D_MODEL])
            dup = (dm * gt).astype(bf16)
            dgm_ref[:, n * D_MODEL:(n + 1) * D_MODEL] = (dm * up * gt * (1.0 - gt)).astype(bf16)
            dy_ref[...] = _dot_nt(dup, p_ref[n])
            dp_ref[n] += _dot_tn(yv, dup)
    return _rt("merge_bwd_" + tag, body, rows, ROW_TILE,
               [(dmerged, D_MODEL, 0), (outs[0], BRANCH, 0), (outs[1], BRANCH, 0), (outs[2], BRANCH, 0),
                (proj, GM_WIDTH, 0)],
               [bp], [(BRANCH, f32), (BRANCH, f32), (BRANCH, f32), (GM_WIDTH, bf16)],
               acc_outs=[(N_BRANCH, BRANCH, D_MODEL)])


def _block_diag(blocks):
    g, r, c = blocks.shape
    on_diag = (lax.broadcasted_iota(jnp.int32, (g * r, g * c), 0) // r
               == lax.broadcasted_iota(jnp.int32, (g * r, g * c), 1) // c)
    tiled = jnp.broadcast_to(blocks.reshape(g * r, 1, c), (g * r, g, c)).reshape(g * r, g * c)
    return jnp.where(on_diag, tiled, 0.0)


def _prep_consts(sp):
    p = jax.nn.softmax(sp["hg_lb_logits"], axis=0)
    lower = jnp.cumsum(p, axis=0) - p[0]
    out = []
    for l in range(DEPTH):
        lr = jnp.minimum(sp["s5_lambda_re"][l], S5_EIG_MAX)
        li = sp["s5_lambda_im"][l]
        dt = jnp.exp(sp["s5_log_dt"][l])[:, None]
        mag = jnp.exp(lr * dt)
        ar = mag * jnp.cos(li * dt)
        ai = mag * jnp.sin(li * dt)
        den = lr * lr + li * li
        fr = ((ar - 1.0) * lr + ai * li) / den
        fi = (ai * lr - (ar - 1.0) * li) / den
        br, bi = sp["s5_b_re"][l], sp["s5_b_im"][l]
        bbr = fr[..., None] * br - fi[..., None] * bi
        bbi = fr[..., None] * bi + fi[..., None] * br
        c = {
            "a_re": ar.reshape(1, S5_LANES), "a_im": ai.reshape(1, S5_LANES),
            "b_re": _block_diag(bbr.transpose(0, 2, 1)), "b_im": _block_diag(bbi.transpose(0, 2, 1)),
            "c_re": _block_diag(sp["s5_c_re"][l].transpose(0, 2, 1)),
            "c_im": -_block_diag(sp["s5_c_im"][l].transpose(0, 2, 1)),
            "s5_d": sp["s5_d"][l][None], "glu_b": sp["s5_glu_b"][l][None],
            "hg_lb": lower[l][None], "hg_nw": sp["hg_norm_w"][l][None],
            "rg_cw": sp["rg_conv_w"][l], "rg_cb": sp["rg_conv_b"][l][None],
            "rg_wa": _block_diag(sp["rg_wa"][l]), "rg_ba": sp["rg_ba"][l][None],
            "rg_wx": _block_diag(sp["rg_wx"][l]), "rg_bx": sp["rg_bx"][l][None],
            "rg_sp8": (RG_C * jax.nn.softplus(-sp["rg_lambda"][l]))[None],
        }
        out.append(c)
    return out


def _mixer_fwd(tag, x, nw, w_in, bp, w_out, cst, rows):
    hb = _rms_fwd("mix_norm_" + tag, x, nw, rows)
    proj = _mm1("mix_proj_" + tag, hb, w_in, "nn", rows, IN_TOTAL, D_MODEL, 512, 512, D_MODEL)
    cst = dict(cst)
    out_a, sv_a = _s5_fwd(tag, proj, cst, rows)
    out_b, sv_b = _hg_fwd(tag, proj, cst, rows)
    out_c, sv_c = _rg_fwd(tag, proj, cst, rows)
    merged = _merge_fwd(tag, proj, (out_a, out_b, out_c), bp, rows)
    x_out = _mm("mix_out_" + tag, [merged], [w_out], [(0, 0, 0)], 1, "nn", rows, D_MODEL, D_MODEL,
                512, D_MODEL, D_MODEL, [f32], lambda accs, extras: [extras[0] + accs[0]], extras=[x])[0]
    return x_out, (x, hb, proj, (out_a, out_b, out_c), merged, sv_a, sv_b, sv_c)


def _mixer_bwd(tag, saved, nw, w_in, bp, w_out, cst, dx, dxb, rows):
    x, hb, proj, outs, merged, sv_a, sv_b, sv_c = saved
    d_wout = _mm1("mix_dwout_" + tag, merged, dxb, "tn", D_MODEL, D_MODEL, rows, D_MODEL, D_MODEL, 512)
    dmerged = _mm1("mix_dmerged_" + tag, dxb, w_out, "nt", rows, D_MODEL, D_MODEL, 512, D_MODEL, D_MODEL)
    d_a, d_b, d_c, dgm, d_bp = _merge_bwd(tag, proj, outs, bp, dmerged, rows)
    dxbc, dgatec, dcst_c = _rg_bwd(tag, sv_c, proj, cst, d_c, rows)
    dq, dz, dv, dg, dcst_b = _hg_bwd(tag, sv_b, proj, cst, d_b, rows)
    du, dcst_a, d_glu_w = _s5_bwd(tag, sv_a, proj, cst, d_a, rows)
    dproj = jnp.concatenate([dgm, du, dq, dz, dv, dg, dxbc, dgatec], axis=1)
    d_win = _mm1("mix_dwin_" + tag, hb, dproj, "tn", D_MODEL, IN_TOTAL, rows, D_MODEL, 512, 512)
    dh = _mm1("mix_dh_" + tag, dproj, w_in, "nt", rows, D_MODEL, IN_TOTAL, 512, D_MODEL, 512)
    dx_in, dxb_in, d_nw = _rms_bwd("mix_norm_bwd_" + tag, x, dh, nw, dx, rows)
    dcst = {**dcst_a, **dcst_b, **dcst_c}
    return dx_in, dxb_in, d_nw, d_win, d_bp, d_wout, d_glu_w, dcst


def _local_step(x, target, big, small):
    rows = x.shape[0]
    consts, consts_vjp = jax.vjp(_prep_consts, small)
    norm_w = small["norm_w"]
    saved = []
    h = x
    for l in range(DEPTH):
        t = str(l)
        cst = dict(consts[l])
        cst["glu_w"] = big["glu_w"][l]
        h, sv0 = _ffn_fwd(t + "a", h, norm_w[l, 0][None], big["gate"][l, 0], big["up"][l, 0], big["down"][l, 0], rows)
        h, sv1 = _mixer_fwd(t, h, norm_w[l, 1][None], big["w_in"][l], big["bp"][l], big["w_out"][l], cst, rows)
        h, sv2 = _ffn_fwd(t + "b", h, norm_w[l, 2][None], big["gate"][l, 1], big["up"][l, 1], big["down"][l, 1], rows)
        saved.append((sv0, sv1, sv2, cst))
    dx, dxb, loss, d_fnw = _loss_head(h, small["final_norm_w"][None], target, rows)
    g_big = {k: [None] * DEPTH for k in ("gate", "up", "down", "w_in", "bp", "w_out", "glu_w")}
    d_norm = [None] * DEPTH
    d_consts = [None] * DEPTH
    for l in reversed(range(DEPTH)):
        t = str(l)
        sv0, sv1, sv2, cst = saved[l]
        dx, dxb, dn2, dg1, du1, dd1 = _ffn_bwd(t + "b", sv2, norm_w[l, 2][None], big["gate"][l, 1], big["up"][l, 1],
                                               big["down"][l, 1], dx, dxb, rows)
        dx, dxb, dn1, d_win, d_bp, d_wout, d_glu_w, dcst = _mixer_bwd(
            t, sv1, norm_w[l, 1][None], big["w_in"][l], big["bp"][l], big["w_out"][l], cst, dx, dxb, rows)
        dx, dxb, dn0, dg0, du0, dd0 = _ffn_bwd(t + "a", sv0, norm_w[l, 0][None], big["gate"][l, 0], big["up"][l, 0],
                                               big["down"][l, 0], dx, dxb, rows)
        g_big["gate"][l] = jnp.stack([dg0, dg1])
        g_big["up"][l] = jnp.stack([du0, du1])
        g_big["down"][l] = jnp.stack([dd0, dd1])
        g_big["w_in"][l] = d_win
        g_big["bp"][l] = d_bp
        g_big["w_out"][l] = d_wout
        g_big["glu_w"][l] = d_glu_w
        d_norm[l] = jnp.concatenate([dn0, dn1, dn2], axis=0)
        d_consts[l] = dcst
    g_big = {k: jnp.stack(v) for k, v in g_big.items()}
    (g_small,) = consts_vjp(d_consts)
    g_small = dict(g_small)
    g_small["norm_w"] = g_small["norm_w"] + jnp.stack(d_norm)
    g_small["final_norm_w"] = g_small["final_norm_w"] + d_fnw[0]
    return loss[0, 0], dx, g_big, g_small


MESH_IDS = pl.DeviceIdType.MESH
ANY_SPEC = pl.BlockSpec(memory_space=pl.ANY)


def _place():
    return lax.axis_index("x"), lax.axis_index("y"), lax.axis_index("c")


def _all_gather(name, shards):
    n = len(shards)

    def body(*refs):
        x_refs, out_refs = refs[:n], refs[n:2 * n]
        send_sems, recv_sems, local_sems = refs[2 * n:]
        x, y, c = _place()
        me, sibling = (x, y, c), (x, y, 1 - c)
        chips = [(1 - x, y), (x, 1 - y), (1 - x, 1 - y)]

        def blk(i, px, py, pc):
            return out_refs[i].at[4 * px + 2 * py + pc]

        def copy(i, k, block, to, src=None):
            return pltpu.make_async_remote_copy(
                src_ref=blk(i, *block) if src is None else src, dst_ref=blk(i, *block),
                send_sem=send_sems.at[7 * i + k], recv_sem=recv_sems.at[7 * i + k], device_id=to,
                device_id_type=MESH_IDS)

        mine = [pltpu.make_async_copy(x_refs[i], blk(i, *me), local_sems.at[i]) for i in range(n)]
        for cp in mine:
            cp.start()
        first = []
        for i in range(n):
            first.append(copy(i, 0, me, sibling, src=x_refs[i]))
            first += [copy(i, 1 + j, me, (*chip, c), src=x_refs[i]) for j, chip in enumerate(chips)]
        for cp in first:
            cp.start()
        passed = []
        for j, chip in enumerate(chips):
            for i in range(n):
                copy(i, 1 + j, (*chip, c), me).wait_recv()
                fwd = copy(i, 4 + j, (*chip, c), sibling)
                fwd.start()
                passed.append(fwd)
        for i in range(n):
            copy(i, 0, sibling, me).wait_recv()
            for j, chip in enumerate(chips):
                copy(i, 4 + j, (*chip, 1 - c), me).wait_recv()
        for cp in first + passed:
            cp.wait_send()
        for cp in mine:
            cp.wait()

    return pl.pallas_call(
        body, name=name, out_shape=[jax.ShapeDtypeStruct((N_DEV,) + s.shape, s.dtype) for s in shards],
        in_specs=[ANY_SPEC] * n, out_specs=[ANY_SPEC] * n,
        scratch_shapes=[pltpu.SemaphoreType.DMA((7 * n,)), pltpu.SemaphoreType.DMA((7 * n,)),
                        pltpu.SemaphoreType.DMA((n,))],
    )(*shards)


def _row_tile(rows):
    return rows if rows <= 512 else next(t for t in range(512, 7, -8) if rows % t == 0)


def _reduce_scatter(parts):
    n = len(parts)
    _, _, c = _place()

    def body_pair(*refs):
        p_refs, got_refs = refs[:n], refs[n:2 * n]
        send_sems, recv_sems = refs[2 * n:]
        x, y, c = _place()
        cps = [pltpu.make_async_remote_copy(
            src_ref=p_refs[i].at[1 - c], dst_ref=got_refs[i], send_sem=send_sems.at[i], recv_sem=recv_sems.at[i],
            device_id=(x, y, 1 - c), device_id_type=MESH_IDS) for i in range(n)]
        for cp in cps:
            cp.start()
        for cp in cps:
            cp.wait()

    from_sibling = pl.pallas_call(
        body_pair, name="rs_pair", out_shape=[jax.ShapeDtypeStruct(p.shape[1:], f32) for p in parts],
        in_specs=[ANY_SPEC] * n, out_specs=[ANY_SPEC] * n,
        scratch_shapes=[pltpu.SemaphoreType.DMA((n,)), pltpu.SemaphoreType.DMA((n,))],
    )(*parts)

    chip_sums = []
    for i, (part, got) in enumerate(zip(parts, from_sibling)):
        _, _, r, cols = part.shape
        tr = _row_tile(r)

        def body_add(idx_ref, p_ref, g_ref, o_ref):
            o_ref[...] = p_ref[...] + g_ref[...]

        chip_sums.append(pl.pallas_call(
            body_add, name="rs_pair_sum_%d" % i, out_shape=jax.ShapeDtypeStruct((4, r, cols), f32),
            grid_spec=pltpu.PrefetchScalarGridSpec(
                num_scalar_prefetch=1, grid=(4, r // tr),
                in_specs=[pl.BlockSpec((None, None, tr, cols), lambda j, t, idx: (idx[0], j, t, 0)),
                          pl.BlockSpec((None, tr, cols), lambda j, t, idx: (j, t, 0))],
                out_specs=pl.BlockSpec((None, tr, cols), lambda j, t, idx: (j, t, 0))),
            compiler_params=_cparams(("parallel", "parallel")),
        )(jnp.stack([c]).astype(jnp.int32), part, got))

    def body_chips(*refs):
        t_refs, got_refs = refs[:n], refs[n:2 * n]
        send_sems, recv_sems = refs[2 * n:]
        x, y, c = _place()
        chips = [(1 - x, y), (x, 1 - y), (1 - x, 1 - y)]
        cps = [pltpu.make_async_remote_copy(
            src_ref=t_refs[i].at[2 * px + py], dst_ref=got_refs[i].at[k], send_sem=send_sems.at[3 * i + k],
            recv_sem=recv_sems.at[3 * i + k], device_id=(px, py, c), device_id_type=MESH_IDS)
            for i in range(n) for k, (px, py) in enumerate(chips)]
        for cp in cps:
            cp.start()
        for cp in cps:
            cp.wait()

    from_chips = pl.pallas_call(
        body_chips, name="rs_chips", out_shape=[jax.ShapeDtypeStruct((3,) + p.shape[2:], f32) for p in parts],
        in_specs=[ANY_SPEC] * n, out_specs=[ANY_SPEC] * n,
        scratch_shapes=[pltpu.SemaphoreType.DMA((3 * n,)), pltpu.SemaphoreType.DMA((3 * n,))],
    )(*chip_sums)
    return list(zip(chip_sums, from_chips))


def _own_index():
    x, y, _ = _place()
    return jnp.stack([2 * x + y]).astype(jnp.int32)


def _own_total(name, chip_sum, others):
    _, r, cols = chip_sum.shape
    tr = _row_tile(r)

    def body(idx_ref, t_ref, g_ref, o_ref):
        o_ref[...] = ((t_ref[...] + g_ref[0]) + g_ref[1]) + g_ref[2]

    return pl.pallas_call(
        body, name=name, out_shape=jax.ShapeDtypeStruct((r, cols), f32),
        grid_spec=pltpu.PrefetchScalarGridSpec(
            num_scalar_prefetch=1, grid=(r // tr,),
            in_specs=[pl.BlockSpec((None, tr, cols), lambda t, idx: (idx[0], t, 0)),
                      pl.BlockSpec((3, tr, cols), lambda t, idx: (0, t, 0))],
            out_specs=pl.BlockSpec((tr, cols), lambda t, idx: (t, 0))),
        compiler_params=_cparams(("parallel",)),
    )(_own_index(), chip_sum, others)


def _adam_update(w, gv, m, v):
    m_new = ADAM_B1 * m + (1.0 - ADAM_B1) * gv
    v_new = ADAM_B2 * v + (1.0 - ADAM_B2) * (gv * gv)
    m_hat = m_new / (1.0 - ADAM_B1 ** ADAM_STEP)
    v_hat = v_new / (1.0 - ADAM_B2 ** ADAM_STEP)
    return -ADAM_LR * (m_hat / (jnp.sqrt(v_hat) + ADAM_EPS) + ADAM_WD * w), m_new, v_new


def _adamw_reduced(name, w, chip_sum, others, m, v):
    rows, cols = w.shape
    tr = _row_tile(rows)

    def body(idx_ref, w_ref, t_ref, o_ref, m_ref, v_ref, g_ref, d_ref, nm_ref, nv_ref):
        gv = ((t_ref[...] + o_ref[0]) + o_ref[1]) + o_ref[2]
        g_ref[...] = gv
        d_ref[...], nm_ref[...], nv_ref[...] = _adam_update(w_ref[...], gv, m_ref[...], v_ref[...])

    spec = pl.BlockSpec((tr, cols), lambda t, idx: (t, 0))
    return pl.pallas_call(
        body, name=name, out_shape=[jax.ShapeDtypeStruct((rows, cols), f32)] * 4,
        grid_spec=pltpu.PrefetchScalarGridSpec(
            num_scalar_prefetch=1, grid=(rows // tr,),
            in_specs=[spec, pl.BlockSpec((None, tr, cols), lambda t, idx: (idx[0], t, 0)),
                      pl.BlockSpec((3, tr, cols), lambda t, idx: (0, t, 0)), spec, spec],
            out_specs=[spec] * 4),
        compiler_params=_cparams(("parallel",)),
    )(_own_index(), w, chip_sum, others, m, v)


def _adamw(name, w, g, m, v):
    rows, cols = w.shape
    tr = _row_tile(rows)

    def body(w_ref, g_ref, m_ref, v_ref, d_ref, nm_ref, nv_ref):
        d_ref[...], nm_ref[...], nv_ref[...] = _adam_update(w_ref[...], g_ref[...], m_ref[...], v_ref[...])

    spec = pl.BlockSpec((tr, cols), lambda i: (i, 0))
    return pl.pallas_call(
        body, name=name, grid=(rows // tr,), in_specs=[spec] * 4, out_specs=[spec] * 3,
        out_shape=[jax.ShapeDtypeStruct((rows, cols), f32)] * 3, compiler_params=_cparams(("parallel",)),
    )(w, g, m, v)


WEIGHT_NAMES = ["norm_w", "final_norm_w", "ffn_gate", "ffn_up", "ffn_down", "w_in", "branch_proj", "w_out",
                "s5_lambda_re", "s5_lambda_im", "s5_log_dt", "s5_b_re", "s5_b_im", "s5_c_re", "s5_c_im", "s5_d",
                "s5_glu_w", "s5_glu_b", "hg_lb_logits", "hg_norm_w", "rg_conv_w", "rg_conv_b", "rg_wa", "rg_ba",
                "rg_wx", "rg_bx", "rg_lambda"]
SHARDED = {"ffn_gate": (3, "gate"), "ffn_up": (3, "up"), "ffn_down": (2, "down"), "w_in": (2, "w_in"),
           "branch_proj": (3, "bp"), "w_out": (1, "w_out"), "s5_glu_w": (1, "glu_w"),
           "norm_w": (2, None), "rg_conv_w": (2, None)}
BIG = ["ffn_gate", "ffn_up", "ffn_down", "w_in", "branch_proj", "w_out", "s5_glu_w"]
SMALL_SHARDED = ["norm_w", "rg_conv_w"]
REPLICATED = [n for n in WEIGHT_NAMES if n not in SHARDED]
LANES = 128


PACK_ROWS = 512


def _pack_rows(arrays, names):
    pieces = []
    for n in names:
        flat = arrays[n].reshape(-1)
        pieces.append(jnp.pad(flat, (0, -flat.shape[0] % LANES)).reshape(-1, LANES))
    rows = jnp.concatenate(pieces, axis=0)
    return jnp.pad(rows, ((0, -rows.shape[0] % PACK_ROWS), (0, 0)))


def _unpack_rows(rows, names, like):
    out, r0 = {}, 0
    for n in names:
        size = math.prod(like[n].shape)
        nrows = -(-size // LANES)
        out[n] = rows[r0:r0 + nrows].reshape(-1)[:size].reshape(like[n].shape)
        r0 += nrows
    return out


def _unshard(gathered, axis):
    g = jnp.moveaxis(gathered, 0, axis)
    shp = g.shape
    return g.reshape(shp[:axis] + (shp[axis] * shp[axis + 1],) + shp[axis + 2:])


def _to_blocks(full, axis):
    shp = full.shape
    g = full.reshape(shp[:axis] + (4, 2, shp[axis] // N_DEV) + shp[axis + 1:])
    g = jnp.moveaxis(g, (axis, axis + 1), (1, 0))
    return g.reshape(2, 4, -1, g.shape[-1])


W_IN_SPLIT = IN_TOTAL - GM_WIDTH


def kernel(x, norm_w, final_norm_w, ffn_gate, ffn_up, ffn_down, w_in, branch_proj, w_out, s5_lambda_re, s5_lambda_im, s5_log_dt, s5_b_re, s5_b_im, s5_c_re, s5_c_im, s5_d, s5_glu_w, s5_glu_b, hg_lb_logits, hg_norm_w, rg_conv_w, rg_conv_b, rg_wa, rg_ba, rg_wx, rg_bx, rg_lambda, loss_target, m_norm_w, m_final_norm_w, m_ffn_gate, m_ffn_up, m_ffn_down, m_w_in, m_branch_proj, m_w_out, m_s5_lambda_re, m_s5_lambda_im, m_s5_log_dt, m_s5_b_re, m_s5_b_im, m_s5_c_re, m_s5_c_im, m_s5_d, m_s5_glu_w, m_s5_glu_b, m_hg_lb_logits, m_hg_norm_w, m_rg_conv_w, m_rg_conv_b, m_rg_wa, m_rg_ba, m_rg_wx, m_rg_bx, m_rg_lambda, v_norm_w, v_final_norm_w, v_ffn_gate, v_ffn_up, v_ffn_down, v_w_in, v_branch_proj, v_w_out, v_s5_lambda_re, v_s5_lambda_im, v_s5_log_dt, v_s5_b_re, v_s5_b_im, v_s5_c_re, v_s5_c_im, v_s5_d, v_s5_glu_w, v_s5_glu_b, v_hg_lb_logits, v_hg_norm_w, v_rg_conv_w, v_rg_conv_b, v_rg_wa, v_rg_ba, v_rg_wx, v_rg_bx, v_rg_lambda):
    w = dict(zip(WEIGHT_NAMES, (norm_w, final_norm_w, ffn_gate, ffn_up, ffn_down, w_in, branch_proj, w_out,
                                s5_lambda_re, s5_lambda_im, s5_log_dt, s5_b_re, s5_b_im, s5_c_re, s5_c_im, s5_d,
                                s5_glu_w, s5_glu_b, hg_lb_logits, hg_norm_w, rg_conv_w, rg_conv_b, rg_wa, rg_ba,
                                rg_wx, rg_bx, rg_lambda)))
    m = dict(zip(WEIGHT_NAMES, (m_norm_w, m_final_norm_w, m_ffn_gate, m_ffn_up, m_ffn_down, m_w_in, m_branch_proj,
                                m_w_out, m_s5_lambda_re, m_s5_lambda_im, m_s5_log_dt, m_s5_b_re, m_s5_b_im, m_s5_c_re,
                                m_s5_c_im, m_s5_d, m_s5_glu_w, m_s5_glu_b, m_hg_lb_logits, m_hg_norm_w, m_rg_conv_w,
                                m_rg_conv_b, m_rg_wa, m_rg_ba, m_rg_wx, m_rg_bx, m_rg_lambda)))
    v = dict(zip(WEIGHT_NAMES, (v_norm_w, v_final_norm_w, v_ffn_gate, v_ffn_up, v_ffn_down, v_w_in, v_branch_proj,
                                v_w_out, v_s5_lambda_re, v_s5_lambda_im, v_s5_log_dt, v_s5_b_re, v_s5_b_im, v_s5_c_re,
                                v_s5_c_im, v_s5_d, v_s5_glu_w, v_s5_glu_b, v_hg_lb_logits, v_hg_norm_w, v_rg_conv_w,
                                v_rg_conv_b, v_rg_wa, v_rg_ba, v_rg_wx, v_rg_bx, v_rg_lambda)))
    rows = x.shape[1]

    sharded = BIG + SMALL_SHARDED
    gathered = _all_gather("gather_weights", [w[n].astype(bf16) for n in BIG] + [w[n] for n in SMALL_SHARDED])
    full = {n: _unshard(g, SHARDED[n][0]) for n, g in zip(sharded, gathered)}
    big = {SHARDED[n][1]: full[n] for n in BIG}
    big["w_in"] = jnp.concatenate([big["w_in"][..., W_IN_SPLIT:], big["w_in"][..., :W_IN_SPLIT]], axis=-1)
    small = {n: w[n] for n in REPLICATED}
    small["norm_w"] = full["norm_w"]
    small["rg_conv_w"] = full["rg_conv_w"]

    loss_part, dx, g_big, g_small = _local_step(x[0], loss_target[0], big, small)
    g_big["w_in"] = jnp.concatenate([g_big["w_in"][..., GM_WIDTH:], g_big["w_in"][..., :GM_WIDTH]], axis=-1)
    loss = lax.psum(loss_part, ("x", "y", "c"))

    parts = [_to_blocks(g_big[SHARDED[n][1]], SHARDED[n][0]) for n in BIG]
    parts += [_to_blocks(g_small[n], SHARDED[n][0]) for n in SMALL_SHARDED]
    rep_rows = _pack_rows(g_small, REPLICATED)
    rep_slice = rep_rows.shape[0] // N_DEV
    parts.append(rep_rows.reshape(4, 2, rep_slice, LANES).transpose(1, 0, 2, 3))
    sums = _reduce_scatter(parts)

    grads, delta, new_m, new_v = {}, {}, {}, {}
    for n, (chip_sum, others) in zip(sharded, sums):
        shp = w[n].shape
        view = (-1, shp[-1])
        res = _adamw_reduced("adamw_" + n, w[n].reshape(view), chip_sum, others, m[n].reshape(view), v[n].reshape(view))
        grads[n], delta[n], new_m[n], new_v[n] = (r.reshape(shp) for r in res)
    rep_mine = _own_total("rs_total_small", *sums[-1])
    rep_grads = _all_gather("gather_small_grads", [rep_mine])[0].reshape(-1, LANES)
    res = _adamw("adamw_small", _pack_rows(w, REPLICATED), rep_grads, _pack_rows(m, REPLICATED), _pack_rows(v, REPLICATED))
    for dst, src in zip((grads, delta, new_m, new_v), (rep_grads,) + tuple(res)):
        dst.update(_unpack_rows(src, REPLICATED, w))

    return (loss, dx.reshape(x.shape), *[grads[n] for n in WEIGHT_NAMES], *[delta[n] for n in WEIGHT_NAMES],
            *[new_m[n] for n in WEIGHT_NAMES], *[new_v[n] for n in WEIGHT_NAMES])
```

```python
import functools
import math

import jax
import jax.numpy as jnp
from jax import lax
from jax.experimental import pallas as pl
from jax.experimental.pallas import tpu as pltpu

f32 = jnp.float32
bf16 = jnp.bfloat16

D_MODEL = 1024
DEPTH = 2
BRANCH = 512
N_BRANCH = 3
S5_GROUP = 16
S5_GROUPS = 32
S5_STATE = 64
S5_LANES = S5_GROUPS * S5_STATE
S5_EIG_MAX = -1e-4
HG_HEADS = 4
HG_DK = 128
HG_CHUNK = 32
RG_BLOCKS = 8
RG_BLOCK = 64
RG_C = 8.0
D_FF = 2816
EPS = 1e-6
IN_TOTAL = 6656
GM_WIDTH = N_BRANCH * D_MODEL
N_DEV = 8

ADAM_LR = 0.001
ADAM_B1 = 0.9
ADAM_B2 = 0.999
ADAM_EPS = 1e-08
ADAM_WD = 0.01
ADAM_STEP = 10

VMEM_LIMIT_V7X = 56 * 1024 * 1024
ROW_TILE = 256
FF_TILE = 1408


def _cparams(sem):
    return pltpu.CompilerParams(dimension_semantics=sem, vmem_limit_bytes=VMEM_LIMIT_V7X)


def _sigmoid(x):
    return 1.0 / (1.0 + jnp.exp(-x))


_GELU_C = math.sqrt(2.0 / math.pi)


def _gelu(x):
    t = jnp.tanh(_GELU_C * (x + 0.044715 * x * x * x))
    return 0.5 * x * (1.0 + t)


def _gelu_grad(x):
    t = jnp.tanh(_GELU_C * (x + 0.044715 * x * x * x))
    return 0.5 * (1.0 + t) + 0.5 * x * (1.0 - t * t) * _GELU_C * (1.0 + 3.0 * 0.044715 * x * x)


def _expm1(x):
    p = x * (1.0 + x * (0.5 + x * (1.0 / 6 + x * (1.0 / 24 + x * (1.0 / 120 + x * (1.0 / 720))))))
    return jnp.where(jnp.abs(x) < 0.3, p, jnp.exp(x) - 1.0)


def _dot(a, b):
    return jnp.dot(a, b, preferred_element_type=f32)


def _dot_nt(a, b):
    return lax.dot_general(a, b, (((1,), (1,)), ((), ())), preferred_element_type=f32)


def _dot_tn(a, b):
    return lax.dot_general(a, b, (((0,), (0,)), ((), ())), preferred_element_type=f32)


def _rows(shape):
    return lax.broadcasted_iota(jnp.int32, shape, 0)


def _scan_fwd(a, b, n):
    row = _rows(a.shape)
    s = 1
    while s < n:
        valid = row >= s
        sh_a = pltpu.roll(a, s, 0)
        sh_b = pltpu.roll(b, s, 0)
        b = b + a * jnp.where(valid, sh_b, 0.0)
        a = a * jnp.where(valid, sh_a, 1.0)
        s *= 2
    return a, b


def _scan_bwd(a, b, n):
    row = _rows(a.shape)
    s = 1
    while s < n:
        valid = row < n - s
        sh_a = pltpu.roll(a, n - s, 0)
        sh_b = pltpu.roll(b, n - s, 0)
        b = b + a * jnp.where(valid, sh_b, 0.0)
        a = a * jnp.where(valid, sh_a, 1.0)
        s *= 2
    return a, b


def _seg_cumsum(x, n, seg):
    pos = _rows(x.shape) % seg
    s = 1
    while s < seg:
        x = x + jnp.where(pos >= s, pltpu.roll(x, s, 0), 0.0)
        s *= 2
    return x


def _seg_rev_cumsum(x, n, seg):
    pos = _rows(x.shape) % seg
    s = 1
    while s < seg:
        x = x + jnp.where(pos < seg - s, pltpu.roll(x, n - s, 0), 0.0)
        s *= 2
    return x


def _head_mean(x):
    parts = []
    for h in range(HG_HEADS):
        m = jnp.mean(x[:, h * HG_DK:(h + 1) * HG_DK], axis=1, keepdims=True)
        parts.append(jnp.broadcast_to(m, (x.shape[0], HG_DK)))
    return jnp.concatenate(parts, axis=1)


def _mm(name, a_list, b_list, terms, n_acc, mode, m, n, k, tm, tn, tk, out_dtypes, epilogue, extras=()):
    tm, tn, tk = min(tm, m), min(tn, n), min(tk, k)
    assert m % tm == 0 and n % tn == 0 and k % tk == 0, (name, m, n, k, tm, tn, tk)
    gk = k // tk
    if mode == "tn":
        a_spec = pl.BlockSpec((tk, tm), lambda i, j, kk: (kk, i))
    else:
        a_spec = pl.BlockSpec((tm, tk), lambda i, j, kk: (i, kk))
    if mode == "nt":
        b_spec = pl.BlockSpec((tn, tk), lambda i, j, kk: (j, kk))
    else:
        b_spec = pl.BlockSpec((tk, tn), lambda i, j, kk: (kk, j))
    o_spec = pl.BlockSpec((tm, tn), lambda i, j, kk: (i, j))
    dot = {"nn": _dot, "nt": _dot_nt, "tn": _dot_tn}[mode]
    na, nb, ne, no = len(a_list), len(b_list), len(extras), len(out_dtypes)

    def kern(*refs):
        a_refs = refs[:na]
        b_refs = refs[na:na + nb]
        e_refs = refs[na + nb:na + nb + ne]
        o_refs = refs[na + nb + ne:na + nb + ne + no]
        acc = refs[na + nb + ne + no]
        kk = pl.program_id(2)

        @pl.when(kk == 0)
        def _():
            acc[...] = jnp.zeros_like(acc)

        for ai, bi, ci in terms:
            acc[ci] += dot(a_refs[ai][...].astype(bf16), b_refs[bi][...].astype(bf16))

        @pl.when(kk == gk - 1)
        def _():
            outs = epilogue([acc[c] for c in range(n_acc)], [e[...] for e in e_refs])
            for o, val in zip(o_refs, outs):
                o[...] = val.astype(o.dtype)

    res = pl.pallas_call(
        kern, name=name,
        grid=(m // tm, n // tn, gk),
        in_specs=[a_spec] * na + [b_spec] * nb + [o_spec] * ne,
        out_specs=[o_spec] * no,
        out_shape=[jax.ShapeDtypeStruct((m, n), dt) for dt in out_dtypes],
        scratch_shapes=[pltpu.VMEM((n_acc, tm, tn), f32)],
        compiler_params=_cparams(("parallel", "parallel", "arbitrary")),
    )(*a_list, *b_list, *extras)
    return res


def _mm1(name, a, b, mode, m, n, k, tm, tn, tk, out_dtype=f32, scale=None):
    def epi(accs, extras):
        return [accs[0] if scale is None else accs[0] * scale]
    return _mm(name, [a], [b], [(0, 0, 0)], 1, mode, m, n, k, tm, tn, tk, [out_dtype], epi)[0]


def _rt(name, body, rows, tm, row_ins, consts, row_outs, acc_outs=(), scratch=(), reverse=False):
    tm = min(tm, rows)
    assert rows % tm == 0
    nt = rows // tm

    def tile(i):
        return nt - 1 - i if reverse else i

    in_specs, args = [], []
    for spec in row_ins:
        arr = spec[0]
        if isinstance(spec[1], int):
            in_specs.append(pl.BlockSpec((tm, spec[1]), lambda i, cb=spec[2]: (tile(i), cb)))
        else:
            in_specs.append(pl.BlockSpec(spec[1], lambda i, fn=spec[2]: fn(tile(i))))
        args.append(arr)
    for c in consts:
        in_specs.append(pl.BlockSpec(c.shape, lambda i, nd=c.ndim: (0,) * nd))
        args.append(c)
    out_specs, out_shape = [], []
    for spec in row_outs:
        if isinstance(spec[0], int):
            out_specs.append(pl.BlockSpec((tm, spec[0]), lambda i: (tile(i), 0)))
            out_shape.append(jax.ShapeDtypeStruct((rows, spec[0]), spec[1]))
        else:
            out_specs.append(pl.BlockSpec(spec[1], lambda i, fn=spec[2]: fn(tile(i))))
            out_shape.append(jax.ShapeDtypeStruct(spec[0], spec[3]))
    for shp in acc_outs:
        out_specs.append(pl.BlockSpec(shp, lambda i, nd=len(shp): (0,) * nd))
        out_shape.append(jax.ShapeDtypeStruct(shp, f32))
    n_in = len(args)
    n_row_out = len(row_outs)
    n_acc = len(acc_outs)

    def kern(*refs):
        i = pl.program_id(0)
        acc_refs = refs[n_in + n_row_out:n_in + n_row_out + n_acc]

        @pl.when(i == 0)
        def _():
            for r in acc_refs:
                r[...] = jnp.zeros_like(r)

        body(i, *refs)

    return pl.pallas_call(
        kern, name=name, grid=(nt,), in_specs=in_specs, out_specs=out_specs, out_shape=out_shape,
        scratch_shapes=list(scratch), compiler_params=_cparams(("arbitrary",)),
    )(*args)


def _rms_fwd(name, x, w, rows):
    def body(i, x_ref, w_ref, h_ref):
        xv = x_ref[...]
        r = lax.rsqrt(jnp.mean(xv * xv, axis=1, keepdims=True) + EPS)
        h_ref[...] = (xv * r * w_ref[...]).astype(bf16)
    return _rt(name, body, rows, ROW_TILE, [(x, D_MODEL, 0)], [w], [(D_MODEL, bf16)])[0]


def _rms_bwd(name, x, dh, w, dres, rows):
    def body(i, x_ref, dh_ref, dres_ref, w_ref, dx_ref, dxb_ref, dw_ref):
        xv = x_ref[...]
        r = lax.rsqrt(jnp.mean(xv * xv, axis=1, keepdims=True) + EPS)
        xn = xv * r
        dhv = dh_ref[...]
        dxn = dhv * w_ref[...]
        dx = dres_ref[...] + r * (dxn - xn * jnp.mean(dxn * xn, axis=1, keepdims=True))
        dx_ref[...] = dx
        dxb_ref[...] = dx.astype(bf16)
        dw_ref[...] += jnp.sum(dhv * xn, axis=0, keepdims=True)
    return _rt(name, body, rows, ROW_TILE, [(x, D_MODEL, 0), (dh, D_MODEL, 0), (dres, D_MODEL, 0)], [w],
               [(D_MODEL, f32), (D_MODEL, bf16)], acc_outs=[(1, D_MODEL)])


def _loss_head(x, w, target, rows):
    def body(i, x_ref, t_ref, w_ref, dx_ref, dxb_ref, loss_ref, dw_ref):
        xv = x_ref[...]
        r = lax.rsqrt(jnp.mean(xv * xv, axis=1, keepdims=True) + EPS)
        xn = xv * r
        wv = w_ref[...]
        err = xn * wv - t_ref[...]
        part = 0.5 * jnp.sum(jnp.mean(err * err, axis=1, keepdims=True), axis=0, keepdims=True)
        loss_ref[...] += jnp.broadcast_to(part, (1, 128))
        dy = err * (1.0 / D_MODEL)
        dxn = dy * wv
        dx = r * (dxn - xn * jnp.mean(dxn * xn, axis=1, keepdims=True))
        dx_ref[...] = dx
        dxb_ref[...] = dx.astype(bf16)
        dw_ref[...] += jnp.sum(dy * xn, axis=0, keepdims=True)
    return _rt("loss_head", body, rows, ROW_TILE, [(x, D_MODEL, 0), (target, D_MODEL, 0)], [w],
               [(D_MODEL, f32), (D_MODEL, bf16)], acc_outs=[(1, 128), (1, D_MODEL)])


def _ffn_fwd(tag, x, nw, wg, wu, wd, rows):
    hb = _rms_fwd("ffn_norm_" + tag, x, nw, rows)

    def epi_up(accs, extras):
        a, b = accs
        return [a, b, a * _sigmoid(a) * b]
    a, b, s = _mm("ffn_up_" + tag, [hb], [wg, wu], [(0, 0, 0), (0, 1, 1)], 2, "nn", rows, D_FF, D_MODEL,
                  512, FF_TILE, D_MODEL, [bf16, bf16, bf16], epi_up)

    def epi_down(accs, extras):
        return [extras[0] + 0.5 * accs[0]]
    x_out = _mm("ffn_down_" + tag, [s], [wd], [(0, 0, 0)], 1, "nn", rows, D_MODEL, D_FF,
                512, D_MODEL, FF_TILE, [f32], epi_down, extras=[x])[0]
    return x_out, (x, hb, a, b, s)


def _ffn_bwd(tag, saved, nw, wg, wu, wd, dx, dxb, rows):
    x, hb, a, b, s = saved

    def epi_mid(accs, extras):
        ds = 0.5 * accs[0]
        av = extras[0].astype(f32)
        bv = extras[1].astype(f32)
        sg = _sigmoid(av)
        return [ds * bv * sg * (1.0 + av * (1.0 - sg)), ds * av * sg]
    da, db = _mm("ffn_bwd_mid_" + tag, [dxb], [wd], [(0, 0, 0)], 1, "nt", rows, D_FF, D_MODEL,
                 512, FF_TILE, D_MODEL, [bf16, bf16], epi_mid, extras=[a, b])
    d_wd = _mm1("ffn_dwd_" + tag, s, dxb, "tn", D_FF, D_MODEL, rows, FF_TILE, D_MODEL, 512, out_dtype=bf16, scale=0.5)
    d_wg = _mm1("ffn_dwg_" + tag, hb, da, "tn", D_MODEL, D_FF, rows, D_MODEL, FF_TILE, 512, out_dtype=bf16)
    d_wu = _mm1("ffn_dwu_" + tag, hb, db, "tn", D_MODEL, D_FF, rows, D_MODEL, FF_TILE, 512, out_dtype=bf16)
    dh = _mm("ffn_dh_" + tag, [da, db], [wg, wu], [(0, 0, 0), (1, 1, 0)], 1, "nt", rows, D_MODEL, D_FF,
             512, D_MODEL, FF_TILE, [f32], lambda accs, extras: [accs[0]])[0]
    dx_in, dxb_in, d_nw = _rms_bwd("ffn_norm_bwd_" + tag, x, dh, nw, dx, rows)
    return dx_in, dxb_in, d_nw, d_wg, d_wu, d_wd


S5_CB = 512
U_COL = GM_WIDTH // BRANCH


def _s5_scan_fwd(tag, proj, b_re, b_im, a_re, a_im, rows):
    tm = min(ROW_TILE, rows)
    nt = rows // tm
    nc = S5_LANES // S5_CB

    def kern(u_ref, bre_ref, bim_ref, ar_ref, ai_ref, xr_ref, xi_ref, pr_s, pi_s, cr_s, ci_s):
        t = pl.program_id(1)
        row = _rows((tm, S5_CB))

        @pl.when(t == 0)
        def _():
            pr = jnp.broadcast_to(ar_ref[...], (tm, S5_CB))
            pi = jnp.broadcast_to(ai_ref[...], (tm, S5_CB))
            s = 1
            while s < tm:
                sr = pltpu.roll(pr, s, 0)
                si = pltpu.roll(pi, s, 0)
                valid = row >= s
                pr, pi = jnp.where(valid, pr * sr - pi * si, pr), jnp.where(valid, pr * si + pi * sr, pi)
                s *= 2
            pr_s[...] = pr
            pi_s[...] = pi
            cr_s[...] = jnp.zeros_like(cr_s)
            ci_s[...] = jnp.zeros_like(ci_s)

        ub = u_ref[...].astype(bf16)
        br = _dot(ub, bre_ref[...])
        bi = _dot(ub, bim_ref[...])
        s = 1
        while s < tm:
            mr = pr_s[s - 1:s, :]
            mi = pi_s[s - 1:s, :]
            sr = pltpu.roll(br, s, 0)
            si = pltpu.roll(bi, s, 0)
            valid = row >= s
            br, bi = (br + jnp.where(valid, mr * sr - mi * si, 0.0),
                      bi + jnp.where(valid, mr * si + mi * sr, 0.0))
            s *= 2
        cr = cr_s[...]
        ci = ci_s[...]
        pr = pr_s[...]
        pi = pi_s[...]
        xr = br + pr * cr - pi * ci
        xi = bi + pr * ci + pi * cr
        xr_ref[...] = xr
        xi_ref[...] = xi
        cr_s[...] = xr[tm - 1:tm, :]
        ci_s[...] = xi[tm - 1:tm, :]

    return pl.pallas_call(
        kern, name="s5_scan_fwd_" + tag, grid=(nc, nt),
        in_specs=[pl.BlockSpec((tm, BRANCH), lambda c, t: (t, U_COL)),
                  pl.BlockSpec((BRANCH, S5_CB), lambda c, t: (0, c)),
                  pl.BlockSpec((BRANCH, S5_CB), lambda c, t: (0, c)),
                  pl.BlockSpec((1, S5_CB), lambda c, t: (0, c)),
                  pl.BlockSpec((1, S5_CB), lambda c, t: (0, c))],
        out_specs=[pl.BlockSpec((tm, S5_CB), lambda c, t: (t, c))] * 2,
        out_shape=[jax.ShapeDtypeStruct((rows, S5_LANES), f32)] * 2,
        scratch_shapes=[pltpu.VMEM((tm, S5_CB), f32), pltpu.VMEM((tm, S5_CB), f32),
                        pltpu.VMEM((1, S5_CB), f32), pltpu.VMEM((1, S5_CB), f32)],
        compiler_params=_cparams(("parallel", "arbitrary")),
    )(proj, b_re, b_im, a_re, a_im)


def _s5_scan_bwd(tag, dxr, dxi, xr, xi, a_re, a_im, rows):
    tm = min(ROW_TILE, rows)
    nt = rows // tm
    nc = S5_LANES // S5_CB

    def kern(dxr_ref, dxi_ref, xr_ref, xi_ref, ar_ref, ai_ref, gr_ref, gi_ref, dar_ref, dai_ref,
             qr_s, qi_s, cr_s, ci_s):
        t = pl.program_id(1)
        row = _rows((tm, S5_CB))

        @pl.when(t == 0)
        def _():
            qr = jnp.broadcast_to(ar_ref[...], (tm, S5_CB))
            qi = jnp.broadcast_to(-ai_ref[...], (tm, S5_CB))
            s = 1
            while s < tm:
                sr = pltpu.roll(qr, tm - s, 0)
                si = pltpu.roll(qi, tm - s, 0)
                valid = row < tm - s
                qr, qi = jnp.where(valid, qr * sr - qi * si, qr), jnp.where(valid, qr * si + qi * sr, qi)
                s *= 2
            qr_s[...] = qr
            qi_s[...] = qi
            cr_s[...] = jnp.zeros_like(cr_s)
            ci_s[...] = jnp.zeros_like(ci_s)
            dar_ref[...] = jnp.zeros_like(dar_ref)
            dai_ref[...] = jnp.zeros_like(dai_ref)

        br = dxr_ref[...]
        bi = dxi_ref[...]
        s = 1
        while s < tm:
            mr = qr_s[tm - s:tm - s + 1, :]
            mi = qi_s[tm - s:tm - s + 1, :]
            sr = pltpu.roll(br, tm - s, 0)
            si = pltpu.roll(bi, tm - s, 0)
            valid = row < tm - s
            br, bi = (br + jnp.where(valid, mr * sr - mi * si, 0.0),
                      bi + jnp.where(valid, mr * si + mi * sr, 0.0))
            s *= 2
        cr = cr_s[...]
        ci = ci_s[...]
        qr = qr_s[...]
        qi = qi_s[...]
        gr = br + qr * cr - qi * ci
        gi = bi + qr * ci + qi * cr
        gr_ref[...] = gr.astype(bf16)
        gi_ref[...] = gi.astype(bf16)
        last = row == tm - 1
        gnr = jnp.where(last, cr, pltpu.roll(gr, tm - 1, 0))
        gni = jnp.where(last, ci, pltpu.roll(gi, tm - 1, 0))
        xr_v = xr_ref[...]
        xi_v = xi_ref[...]
        dar_ref[...] += jnp.sum(gnr * xr_v + gni * xi_v, axis=0, keepdims=True)
        dai_ref[...] += jnp.sum(gni * xr_v - gnr * xi_v, axis=0, keepdims=True)
        cr_s[...] = gr[0:1, :]
        ci_s[...] = gi[0:1, :]

    rev = lambda c, t: (nt - 1 - t, c)
    return pl.pallas_call(
        kern, name="s5_scan_bwd_" + tag, grid=(nc, nt),
        in_specs=[pl.BlockSpec((tm, S5_CB), rev)] * 4 + [pl.BlockSpec((1, S5_CB), lambda c, t: (0, c))] * 2,
        out_specs=[pl.BlockSpec((tm, S5_CB), rev)] * 2 + [pl.BlockSpec((1, S5_CB), lambda c, t: (0, c))] * 2,
        out_shape=[jax.ShapeDtypeStruct((rows, S5_LANES), bf16)] * 2 + [jax.ShapeDtypeStruct((1, S5_LANES), f32)] * 2,
        scratch_shapes=[pltpu.VMEM((tm, S5_CB), f32), pltpu.VMEM((tm, S5_CB), f32),
                        pltpu.VMEM((1, S5_CB), f32), pltpu.VMEM((1, S5_CB), f32)],
        compiler_params=_cparams(("parallel", "arbitrary")),
    )(dxr, dxi, xr, xi, a_re, a_im)


def _s5_fwd(tag, proj, cst, rows):
    xr, xi = _s5_scan_fwd(tag, proj, cst["b_re"].astype(bf16), cst["b_im"].astype(bf16), cst["a_re"], cst["a_im"], rows)

    def body(i, xr_ref, xi_ref, u_ref, cre_ref, cim_ref, d_ref, gw_ref, gb_ref, y_ref, out_ref):
        y = (_dot(xr_ref[...].astype(bf16), cre_ref[...]) + _dot(xi_ref[...].astype(bf16), cim_ref[...])
             + d_ref[...] * u_ref[...])
        y_ref[...] = y
        z = _gelu(y)
        zg = _dot(z.astype(bf16), gw_ref[...]) + gb_ref[...]
        out_ref[...] = (z * _sigmoid(zg)).astype(bf16)

    y, out = _rt("s5_out_" + tag, body, rows, ROW_TILE,
                 [(xr, S5_LANES, 0), (xi, S5_LANES, 0), (proj, BRANCH, U_COL)],
                 [cst["c_re"].astype(bf16), cst["c_im"].astype(bf16), cst["s5_d"], cst["glu_w"], cst["glu_b"]],
                 [(BRANCH, f32), (BRANCH, bf16)])
    return out, (xr, xi, y)


def _s5_bwd(tag, saved, proj, cst, d_out, rows):
    xr, xi, y = saved
    c_re = cst["c_re"].astype(bf16)
    c_im = cst["c_im"].astype(bf16)

    def body(i, do_ref, y_ref, u_ref, xr_ref, xi_ref, cre_ref, cim_ref, gw_ref, gb_ref,
             dxr_ref, dxi_ref, dy_ref, dgw_ref, dgb_ref, dd_ref, dcre_ref, dcim_ref):
        yv = y_ref[...]
        z = _gelu(yv)
        zb = z.astype(bf16)
        gt = _sigmoid(_dot(zb, gw_ref[...]) + gb_ref[...])
        dov = do_ref[...]
        dzg = dov * z * gt * (1.0 - gt)
        dzgb = dzg.astype(bf16)
        dz = dov * gt + _dot_nt(dzgb, gw_ref[...])
        dgw_ref[...] += _dot_tn(zb, dzgb)
        dgb_ref[...] += jnp.sum(dzg, axis=0, keepdims=True)
        dy = dz * _gelu_grad(yv)
        dy_ref[...] = dy
        dd_ref[...] += jnp.sum(dy * u_ref[...], axis=0, keepdims=True)
        dyb = dy.astype(bf16)
        dxr_ref[...] = _dot_nt(dyb, cre_ref[...])
        dxi_ref[...] = _dot_nt(dyb, cim_ref[...])
        dcre_ref[...] += _dot_tn(xr_ref[...].astype(bf16), dyb)
        dcim_ref[...] += _dot_tn(xi_ref[...].astype(bf16), dyb)

    dxr, dxi, dy, d_gw, d_gb, d_d, d_cre, d_cim = _rt(
        "s5_out_bwd_" + tag, body, rows, ROW_TILE,
        [(d_out, BRANCH, 0), (y, BRANCH, 0), (proj, BRANCH, U_COL), (xr, S5_LANES, 0), (xi, S5_LANES, 0)],
        [c_re, c_im, cst["glu_w"], cst["glu_b"]],
        [(S5_LANES, f32), (S5_LANES, f32), (BRANCH, f32)],
        acc_outs=[(BRANCH, BRANCH), (1, BRANCH), (1, BRANCH), (S5_LANES, BRANCH), (S5_LANES, BRANCH)])

    gr, gi, d_ar, d_ai = _s5_scan_bwd(tag, dxr, dxi, xr, xi, cst["a_re"], cst["a_im"], rows)
    b_re = cst["b_re"].astype(bf16)
    b_im = cst["b_im"].astype(bf16)

    def body_in(i, gr_ref, gi_ref, dy_ref, u_ref, bre_ref, bim_ref, d_ref, du_ref, dbre_ref, dbim_ref):
        grv = gr_ref[...]
        giv = gi_ref[...]
        du = _dot_nt(grv, bre_ref[...]) + _dot_nt(giv, bim_ref[...]) + dy_ref[...] * d_ref[...]
        du_ref[...] = du.astype(bf16)
        ub = u_ref[...].astype(bf16)
        dbre_ref[...] += _dot_tn(ub, grv)
        dbim_ref[...] += _dot_tn(ub, giv)

    du, d_bre, d_bim = _rt("s5_in_bwd_" + tag, body_in, rows, ROW_TILE,
                           [(gr, S5_LANES, 0), (gi, S5_LANES, 0), (dy, BRANCH, 0), (proj, BRANCH, U_COL)],
                           [b_re, b_im, cst["s5_d"]], [(BRANCH, bf16)],
                           acc_outs=[(BRANCH, S5_LANES), (BRANCH, S5_LANES)])
    dcst = {"b_re": d_bre, "b_im": d_bim, "a_re": d_ar, "a_im": d_ai, "c_re": d_cre, "c_im": d_cim,
            "s5_d": d_d, "glu_b": d_gb}
    return du, dcst, d_gw


def _hg_prep(q, z, lb):
    qs = _sigmoid(q)
    qh = q * qs
    sg = _sigmoid(z)
    fg = lb + (1.0 - lb) * sg
    kk = (1.0 - lb) * (1.0 - sg)
    return qs, qh, sg, fg, kk


def _hg_fwd(tag, proj, cst, rows):
    tm = min(ROW_TILE, rows)
    c_sz = HG_CHUNK
    nch = tm // c_sz
    n_chunks = rows // c_sz

    def body(i, q_ref, z_ref, v_ref, g_ref, lb_ref, nw_ref, out_ref, o_ref, ss_ref, sn_ref,
             st_s, qh_s, kh_s, vb_s, b_s, k_s):
        @pl.when(i == 0)
        def _():
            st_s[...] = jnp.zeros_like(st_s)

        lb = lb_ref[...]
        _, qh, sg, fg, kk = _hg_prep(q_ref[...], z_ref[...], lb)
        b = _seg_cumsum(jnp.log(fg), tm, c_sz)
        b_s[...] = b
        k_s[...] = kk
        qh_s[...] = (qh * jnp.exp(b)).astype(bf16)
        kh_s[...] = (kk * jnp.exp(-b)).astype(bf16)
        vb_s[...] = v_ref[...].astype(bf16)
        tril = _rows((c_sz, c_sz)) >= lax.broadcasted_iota(jnp.int32, (c_sz, c_sz), 1)

        def chunk(ci, carry):
            sl = pl.ds(pl.multiple_of(ci * c_sz, c_sz), c_sz)
            for h in range(HG_HEADS):
                hl = slice(h * HG_DK, (h + 1) * HG_DK)
                qb = qh_s[sl, hl]
                kb = kh_s[sl, hl]
                vb = vb_s[sl, hl]
                a_mat = jnp.where(tril, _dot_nt(qb, kb), 0.0)
                st = st_s[hl, :]
                stb = st.astype(bf16)
                o_ref[sl, hl] = _dot_nt(qb, stb) + _dot(a_mat.astype(bf16), vb)
                ss_ref[ci, hl, :] = stb
                bb = b_s[sl, hl]
                bl = bb[c_sz - 1:c_sz, :]
                kd = (k_s[sl, hl] * jnp.exp(bl - bb)).astype(bf16)
                st_new = st * jnp.exp(bl) + _dot_tn(vb, kd)
                st_s[hl, :] = st_new
                sn_ref[ci, hl, :] = st_new.astype(bf16)
            return carry

        lax.fori_loop(0, nch, chunk, 0)
        o = o_ref[...]
        r = lax.rsqrt(_head_mean(o * o) + EPS)
        g = g_ref[...]
        out_ref[...] = (o * r * nw_ref[...] * (g * _sigmoid(g))).astype(bf16)

    out, o, ss, sn = _rt(
        "hg_fwd_" + tag, body, rows, tm,
        [(proj, BRANCH, U_COL + 1), (proj, BRANCH, U_COL + 2), (proj, BRANCH, U_COL + 3), (proj, BRANCH, U_COL + 4)],
        [cst["hg_lb"], cst["hg_nw"]],
        [(BRANCH, bf16), (BRANCH, f32),
         ((n_chunks, BRANCH, HG_DK), (nch, BRANCH, HG_DK), lambda t: (t, 0, 0), bf16),
         ((n_chunks, BRANCH, HG_DK), (nch, BRANCH, HG_DK), lambda t: (t, 0, 0), bf16)],
        scratch=[pltpu.VMEM((BRANCH, HG_DK), f32), pltpu.VMEM((tm, BRANCH), bf16), pltpu.VMEM((tm, BRANCH), bf16),
                 pltpu.VMEM((tm, BRANCH), bf16), pltpu.VMEM((tm, BRANCH), f32), pltpu.VMEM((tm, BRANCH), f32)])
    return out, (o, ss, sn)


def _hg_bwd(tag, saved, proj, cst, d_out, rows):
    o_saved, ss, sn = saved
    tm = min(ROW_TILE, rows)
    c_sz = HG_CHUNK
    nch = tm // c_sz

    def body(i, do_ref, q_ref, z_ref, v_ref, g_ref, o_ref, ss_ref, sn_ref, lb_ref, nw_ref,
             dq_ref, dz_ref, dv_ref, dg_ref, dlb_ref, dnw_ref,
             dst_s, flux_s, qh_s, kh_s, vb_s, b_s, k_s, dob_s, dqh_s, dk_s, db_s):
        @pl.when(i == 0)
        def _():
            dst_s[...] = jnp.zeros_like(dst_s)

        lb = lb_ref[...]
        q = q_ref[...]
        qs, qh, sg, fg, kk = _hg_prep(q, z_ref[...], lb)
        b = _seg_cumsum(jnp.log(fg), tm, c_sz)
        b_s[...] = b
        k_s[...] = kk
        qh_s[...] = (qh * jnp.exp(b)).astype(bf16)
        kh_s[...] = (kk * jnp.exp(-b)).astype(bf16)
        vb_s[...] = v_ref[...].astype(bf16)
        g = g_ref[...]
        gs = _sigmoid(g)
        o = o_ref[...]
        r = lax.rsqrt(_head_mean(o * o) + EPS)
        oh = o * r
        nw = nw_ref[...]
        dov = do_ref[...]
        don = dov * (g * gs)
        dg_ref[...] = (dov * oh * nw * (gs * (1.0 + g * (1.0 - gs)))).astype(bf16)
        dnw_ref[...] += jnp.sum(don * oh, axis=0, keepdims=True)
        doh = don * nw
        d_o = r * (doh - oh * _head_mean(doh * oh))
        dob_s[...] = d_o.astype(bf16)
        tril = _rows((c_sz, c_sz)) >= lax.broadcasted_iota(jnp.int32, (c_sz, c_sz), 1)

        def chunk(cj, carry):
            ci = nch - 1 - cj
            sl = pl.ds(pl.multiple_of(ci * c_sz, c_sz), c_sz)
            for h in range(HG_HEADS):
                hl = slice(h * HG_DK, (h + 1) * HG_DK)
                qb = qh_s[sl, hl]
                kb = kh_s[sl, hl]
                vb = vb_s[sl, hl]
                dob = dob_s[sl, hl]
                stb = ss_ref[ci, hl, :]
                dst = dst_s[hl, :]
                dstb = dst.astype(bf16)
                a_mat = jnp.where(tril, _dot_nt(qb, kb), 0.0).astype(bf16)
                da_mat = jnp.where(tril, _dot_nt(dob, vb), 0.0).astype(bf16)
                bb = b_s[sl, hl]
                bl = bb[c_sz - 1:c_sz, :]
                ebl = jnp.exp(bl - bb)
                dqhat = _dot(dob, stb) + _dot(da_mat, kb)
                dkhat = _dot_tn(da_mat, qb)
                dk_inter = _dot(vb, dstb) * ebl
                kv = k_s[sl, hl]
                dqh_s[sl, hl] = dqhat * jnp.exp(bb)
                dk_s[sl, hl] = dkhat * jnp.exp(-bb) + dk_inter
                db_s[sl, hl] = qb.astype(f32) * dqhat - kb.astype(f32) * dkhat - kv * dk_inter
                flux = jnp.sum(sn_ref[ci, hl, :].astype(f32) * dst, axis=0, keepdims=True)
                flux_s[sl, hl] = jnp.broadcast_to(flux, (c_sz, HG_DK))
                kd = (kv * ebl).astype(bf16)
                dv_ref[sl, hl] = (_dot_tn(a_mat, dob) + _dot_nt(kd, dstb)).astype(bf16)
                dst_s[hl, :] = dst * jnp.exp(bl) + _dot_tn(dob, qb)
            return carry

        lax.fori_loop(0, nch, chunk, 0)
        dqh = dqh_s[...]
        dk = dk_s[...]
        dlf = _seg_rev_cumsum(db_s[...], tm, c_sz) + flux_s[...]
        tt = (1.0 - lb) * sg * (1.0 - sg)
        dz_ref[...] = (dlf * tt / fg - dk * tt).astype(bf16)
        dlb_ref[...] += jnp.sum(dlf * (1.0 - sg) / fg - dk * (1.0 - sg), axis=0, keepdims=True)
        dq_ref[...] = (dqh * (qs * (1.0 + q * (1.0 - qs)))).astype(bf16)

    dq, dz, dv, dg, d_lb, d_nw = _rt(
        "hg_bwd_" + tag, body, rows, tm,
        [(d_out, BRANCH, 0), (proj, BRANCH, U_COL + 1), (proj, BRANCH, U_COL + 2), (proj, BRANCH, U_COL + 3),
         (proj, BRANCH, U_COL + 4), (o_saved, BRANCH, 0), (ss, (nch, BRANCH, HG_DK), lambda t: (t, 0, 0)),
         (sn, (nch, BRANCH, HG_DK), lambda t: (t, 0, 0))],
        [cst["hg_lb"], cst["hg_nw"]],
        [(BRANCH, bf16)] * 4, acc_outs=[(1, BRANCH), (1, BRANCH)],
        scratch=[pltpu.VMEM((BRANCH, HG_DK), f32), pltpu.VMEM((tm, BRANCH), f32),
                 pltpu.VMEM((tm, BRANCH), bf16), pltpu.VMEM((tm, BRANCH), bf16), pltpu.VMEM((tm, BRANCH), bf16),
                 pltpu.VMEM((tm, BRANCH), f32), pltpu.VMEM((tm, BRANCH), f32), pltpu.VMEM((tm, BRANCH), bf16),
                 pltpu.VMEM((tm, BRANCH), f32), pltpu.VMEM((tm, BRANCH), f32), pltpu.VMEM((tm, BRANCH), f32)],
        reverse=True)
    return dq, dz, dv, dg, {"hg_lb": d_lb, "hg_nw": d_nw}


def _rg_gates(xc, wa_ref, ba_ref, wx_ref, bx_ref, sp8):
    xcb = xc.astype(bf16)
    r = _sigmoid(_dot(xcb, wa_ref[...]) + ba_ref[...])
    ig = _sigmoid(_dot(xcb, wx_ref[...]) + bx_ref[...])
    la = -sp8 * r
    a = jnp.exp(la)
    mult = jnp.sqrt(-_expm1(2.0 * la))
    return xcb, r, ig, a, mult


def _rg_fwd(tag, proj, cst, rows):
    tm = min(ROW_TILE, rows)

    def body(i, xb_ref, gate_ref, cw_ref, cb_ref, wa_ref, ba_ref, wx_ref, bx_ref, sp_ref,
             out_ref, xc_ref, h_ref, hp_ref, prev_s, hc_s):
        @pl.when(i == 0)
        def _():
            prev_s[...] = jnp.zeros_like(prev_s)
            hc_s[...] = jnp.zeros_like(hc_s)

        row = _rows((tm, BRANCH))
        xb = xb_ref[...]
        prev = prev_s[...]
        xc = cb_ref[...] + cw_ref[3:4, :] * xb
        for j in range(1, 4):
            sh = jnp.where(row >= j, pltpu.roll(xb, j, 0), pltpu.roll(prev, j, 0))
            xc = xc + cw_ref[3 - j:4 - j, :] * sh
        prev_s[...] = xb
        xc_ref[...] = xc
        _, r, ig, a, mult = _rg_gates(xc, wa_ref, ba_ref, wx_ref, bx_ref, sp_ref[...])
        a_cum, h_loc = _scan_fwd(a, mult * ig * xc, tm)
        hc = hc_s[...]
        h = h_loc + a_cum * hc
        h_ref[...] = h
        hp_ref[...] = jnp.where(row >= 1, pltpu.roll(h, 1, 0), hc)
        hc_s[...] = h[tm - 1:tm, :]
        out_ref[...] = (h * _gelu(gate_ref[...])).astype(bf16)

    out, xc, h, hp = _rt(
        "rg_fwd_" + tag, body, rows, tm,
        [(proj, BRANCH, U_COL + 5), (proj, BRANCH, U_COL + 6)],
        [cst["rg_cw"], cst["rg_cb"], cst["rg_wa"].astype(bf16), cst["rg_ba"], cst["rg_wx"].astype(bf16),
         cst["rg_bx"], cst["rg_sp8"]],
        [(BRANCH, bf16), (BRANCH, f32), (BRANCH, f32), (BRANCH, f32)],
        scratch=[pltpu.VMEM((tm, BRANCH), f32), pltpu.VMEM((1, BRANCH), f32)])
    return out, (xc, h, hp)


def _rg_bwd(tag, saved, proj, cst, d_out, rows):
    xc_saved, h_saved, hp_saved = saved
    tm = min(ROW_TILE, rows)

    def body(i, do_ref, xb_ref, gate_ref, xc_ref, h_ref, hp_ref, cw_ref, wa_ref, ba_ref, wx_ref, bx_ref, sp_ref,
             dxb_ref, dgate_ref, dcw_ref, dcb_ref, dwa_ref, dba_ref, dwx_ref, dbx_ref, dsp_ref,
             nxt_s, ec_s):
        @pl.when(i == 0)
        def _():
            nxt_s[...] = jnp.zeros_like(nxt_s)
            ec_s[...] = jnp.zeros_like(ec_s)

        row = _rows((tm, BRANCH))
        xc = xc_ref[...]
        sp8 = sp_ref[...]
        xcb, r, ig, a, mult = _rg_gates(xc, wa_ref, ba_ref, wx_ref, bx_ref, sp8)
        gate = gate_ref[...]
        dov = do_ref[...]
        dh = dov * _gelu(gate)
        dgate_ref[...] = (dov * h_ref[...] * _gelu_grad(gate)).astype(bf16)
        a_cum, e_loc = _scan_bwd(a, a * dh, tm)
        ec = ec_s[...]
        e = e_loc + a_cum * ec
        g_tot = dh + jnp.where(row == tm - 1, ec, pltpu.roll(e, tm - 1, 0))
        ec_s[...] = e[0:1, :]
        d_a = g_tot * hp_ref[...]
        d_mult = g_tot * ig * xc
        d_ix = g_tot * mult
        d_ig = d_ix * xc
        d_xc = d_ix * ig
        d_la = d_a * a - d_mult * (a * a) / mult
        d_r = -d_la * sp8
        dsp_ref[...] += jnp.sum(-d_la * r, axis=0, keepdims=True)
        dzr = d_r * r * (1.0 - r)
        dzi = d_ig * ig * (1.0 - ig)
        dzrb = dzr.astype(bf16)
        dzib = dzi.astype(bf16)
        d_xc = d_xc + _dot_nt(dzrb, wa_ref[...]) + _dot_nt(dzib, wx_ref[...])
        dwa_ref[...] += _dot_tn(xcb, dzrb)
        dwx_ref[...] += _dot_tn(xcb, dzib)
        dba_ref[...] += jnp.sum(dzr, axis=0, keepdims=True)
        dbx_ref[...] += jnp.sum(dzi, axis=0, keepdims=True)
        dcb_ref[...] += jnp.sum(d_xc, axis=0, keepdims=True)
        nxt = nxt_s[...]
        xb = xb_ref[...]
        dxb = cw_ref[3:4, :] * d_xc
        dcw_ref[3:4, :] += jnp.sum(d_xc * xb, axis=0, keepdims=True)
        for j in range(1, 4):
            sh = jnp.where(row < tm - j, pltpu.roll(d_xc, tm - j, 0), pltpu.roll(nxt, tm - j, 0))
            dxb = dxb + cw_ref[3 - j:4 - j, :] * sh
            dcw_ref[3 - j:4 - j, :] += jnp.sum(sh * xb, axis=0, keepdims=True)
        nxt_s[...] = d_xc
        dxb_ref[...] = dxb.astype(bf16)

    wa = cst["rg_wa"].astype(bf16)
    wx = cst["rg_wx"].astype(bf16)
    dxb, dgate, d_cw, d_cb, d_wa, d_ba, d_wx, d_bx, d_sp = _rt(
        "rg_bwd_" + tag, body, rows, tm,
        [(d_out, BRANCH, 0), (proj, BRANCH, U_COL + 5), (proj, BRANCH, U_COL + 6), (xc_saved, BRANCH, 0),
         (h_saved, BRANCH, 0), (hp_saved, BRANCH, 0)],
        [cst["rg_cw"], wa, cst["rg_ba"], wx, cst["rg_bx"], cst["rg_sp8"]],
        [(BRANCH, bf16), (BRANCH, bf16)],
        acc_outs=[(4, BRANCH), (1, BRANCH), (BRANCH, BRANCH), (1, BRANCH), (BRANCH, BRANCH), (1, BRANCH), (1, BRANCH)],
        scratch=[pltpu.VMEM((tm, BRANCH), f32), pltpu.VMEM((1, BRANCH), f32)],
        reverse=True)
    dcst = {"rg_cw": d_cw, "rg_cb": d_cb, "rg_wa": d_wa, "rg_ba": d_ba, "rg_wx": d_wx, "rg_bx": d_bx, "rg_sp8": d_sp}
    return dxb, dgate, dcst


def _merge_fwd(tag, proj, outs, bp, rows):
    def body(i, ya_ref, yb_ref, yc_ref, gm_ref, p_ref, m_ref):
        acc = None
        for n, y_ref in enumerate((ya_ref, yb_ref, yc_ref)):
            up = _dot(y_ref[...], p_ref[n])
            term = _sigmoid(gm_ref[:, n * D_MODEL:(n + 1) * D_MODEL]) * up
            acc = term if acc is None else acc + term
        m_ref[...] = acc.astype(bf16)
    return _rt("merge_fwd_" + tag, body, rows, ROW_TILE,
               [(outs[0], BRANCH, 0), (outs[1], BRANCH, 0), (outs[2], BRANCH, 0), (proj, GM_WIDTH, 0)],
               [bp], [(D_MODEL, bf16)])[0]


def _merge_bwd(tag, proj, outs, bp, dmerged, rows):
    def body(i, dm_ref, ya_ref, yb_ref, yc_ref, gm_ref, p_ref, da_ref, db_ref, dc_ref, dgm_ref, dp_ref):
        dm = dm_ref[...]
        for n, (y_ref, dy_ref) in enumerate(((ya_ref, da_ref), (yb_ref, db_ref), (yc_ref, dc_ref))):
            yv = y_ref[...]
            up = _dot(yv, p_ref[n])
            gt = _sigmoid(gm_ref[:, n * D_MODEL:(n + 1) * D_MODEL])
            dup = (dm * gt).astype(bf16)
            dgm_ref[:, n * D_MODEL:(n + 1) * D_MODEL] = (dm * up * gt * (1.0 - gt)).astype(bf16)
            dy_ref[...] = _dot_nt(dup, p_ref[n])
            dp_ref[n] += _dot_tn(yv, dup)
    return _rt("merge_bwd_" + tag, body, rows, ROW_TILE,
               [(dmerged, D_MODEL, 0), (outs[0], BRANCH, 0), (outs[1], BRANCH, 0), (outs[2], BRANCH, 0),
                (proj, GM_WIDTH, 0)],
               [bp], [(BRANCH, f32), (BRANCH, f32), (BRANCH, f32), (GM_WIDTH, bf16)],
               acc_outs=[(N_BRANCH, BRANCH, D_MODEL)])


def _block_diag(blocks):
    g, r, c = blocks.shape
    on_diag = (lax.broadcasted_iota(jnp.int32, (g * r, g * c), 0) // r
               == lax.broadcasted_iota(jnp.int32, (g * r, g * c), 1) // c)
    tiled = jnp.broadcast_to(blocks.reshape(g * r, 1, c), (g * r, g, c)).reshape(g * r, g * c)
    return jnp.where(on_diag, tiled, 0.0)


def _prep_consts(sp):
    p = jax.nn.softmax(sp["hg_lb_logits"], axis=0)
    lower = jnp.cumsum(p, axis=0) - p[0]
    out = []
    for l in range(DEPTH):
        lr = jnp.minimum(sp["s5_lambda_re"][l], S5_EIG_MAX)
        li = sp["s5_lambda_im"][l]
        dt = jnp.exp(sp["s5_log_dt"][l])[:, None]
        mag = jnp.exp(lr * dt)
        ar = mag * jnp.cos(li * dt)
        ai = mag * jnp.sin(li * dt)
        den = lr * lr + li * li
        fr = ((ar - 1.0) * lr + ai * li) / den
        fi = (ai * lr - (ar - 1.0) * li) / den
        br, bi = sp["s5_b_re"][l], sp["s5_b_im"][l]
        bbr = fr[..., None] * br - fi[..., None] * bi
        bbi = fr[..., None] * bi + fi[..., None] * br
        c = {
            "a_re": ar.reshape(1, S5_LANES), "a_im": ai.reshape(1, S5_LANES),
            "b_re": _block_diag(bbr.transpose(0, 2, 1)), "b_im": _block_diag(bbi.transpose(0, 2, 1)),
            "c_re": _block_diag(sp["s5_c_re"][l].transpose(0, 2, 1)),
            "c_im": -_block_diag(sp["s5_c_im"][l].transpose(0, 2, 1)),
            "s5_d": sp["s5_d"][l][None], "glu_b": sp["s5_glu_b"][l][None],
            "hg_lb": lower[l][None], "hg_nw": sp["hg_norm_w"][l][None],
            "rg_cw": sp["rg_conv_w"][l], "rg_cb": sp["rg_conv_b"][l][None],
            "rg_wa": _block_diag(sp["rg_wa"][l]), "rg_ba": sp["rg_ba"][l][None],
            "rg_wx": _block_diag(sp["rg_wx"][l]), "rg_bx": sp["rg_bx"][l][None],
            "rg_sp8": (RG_C * jax.nn.softplus(-sp["rg_lambda"][l]))[None],
        }
        out.append(c)
    return out


def _mixer_fwd(tag, x, nw, w_in, bp, w_out, cst, rows):
    hb = _rms_fwd("mix_norm_" + tag, x, nw, rows)
    proj = _mm1("mix_proj_" + tag, hb, w_in, "nn", rows, IN_TOTAL, D_MODEL, 512, 512, D_MODEL)
    cst = dict(cst)
    out_a, sv_a = _s5_fwd(tag, proj, cst, rows)
    out_b, sv_b = _hg_fwd(tag, proj, cst, rows)
    out_c, sv_c = _rg_fwd(tag, proj, cst, rows)
    merged = _merge_fwd(tag, proj, (out_a, out_b, out_c), bp, rows)
    x_out = _mm("mix_out_" + tag, [merged], [w_out], [(0, 0, 0)], 1, "nn", rows, D_MODEL, D_MODEL,
                512, D_MODEL, D_MODEL, [f32], lambda accs, extras: [extras[0] + accs[0]], extras=[x])[0]
    return x_out, (x, hb, proj, (out_a, out_b, out_c), merged, sv_a, sv_b, sv_c)


def _mixer_bwd(tag, saved, nw, w_in, bp, w_out, cst, dx, dxb, rows):
    x, hb, proj, outs, merged, sv_a, sv_b, sv_c = saved
    d_wout = _mm1("mix_dwout_" + tag, merged, dxb, "tn", D_MODEL, D_MODEL, rows, D_MODEL, D_MODEL, 512, out_dtype=bf16)
    dmerged = _mm1("mix_dmerged_" + tag, dxb, w_out, "nt", rows, D_MODEL, D_MODEL, 512, D_MODEL, D_MODEL)
    d_a, d_b, d_c, dgm, d_bp = _merge_bwd(tag, proj, outs, bp, dmerged, rows)
    dxbc, dgatec, dcst_c = _rg_bwd(tag, sv_c, proj, cst, d_c, rows)
    dq, dz, dv, dg, dcst_b = _hg_bwd(tag, sv_b, proj, cst, d_b, rows)
    du, dcst_a, d_glu_w = _s5_bwd(tag, sv_a, proj, cst, d_a, rows)
    dproj = jnp.concatenate([dgm, du, dq, dz, dv, dg, dxbc, dgatec], axis=1)
    d_win = _mm1("mix_dwin_" + tag, hb, dproj, "tn", D_MODEL, IN_TOTAL, rows, D_MODEL, 512, 512, out_dtype=bf16)
    dh = _mm1("mix_dh_" + tag, dproj, w_in, "nt", rows, D_MODEL, IN_TOTAL, 512, D_MODEL, 512)
    dx_in, dxb_in, d_nw = _rms_bwd("mix_norm_bwd_" + tag, x, dh, nw, dx, rows)
    dcst = {**dcst_a, **dcst_b, **dcst_c}
    return dx_in, dxb_in, d_nw, d_win, d_bp, d_wout, d_glu_w, dcst


def _local_step(x, target, big, small):
    rows = x.shape[0]
    consts, consts_vjp = jax.vjp(_prep_consts, small)
    norm_w = small["norm_w"]
    saved = []
    h = x
    for l in range(DEPTH):
        t = str(l)
        cst = dict(consts[l])
        cst["glu_w"] = big["glu_w"][l]
        h, sv0 = _ffn_fwd(t + "a", h, norm_w[l, 0][None], big["gate"][l, 0], big["up"][l, 0], big["down"][l, 0], rows)
        h, sv1 = _mixer_fwd(t, h, norm_w[l, 1][None], big["w_in"][l], big["bp"][l], big["w_out"][l], cst, rows)
        h, sv2 = _ffn_fwd(t + "b", h, norm_w[l, 2][None], big["gate"][l, 1], big["up"][l, 1], big["down"][l, 1], rows)
        saved.append((sv0, sv1, sv2, cst))
    dx, dxb, loss, d_fnw = _loss_head(h, small["final_norm_w"][None], target, rows)
    g_big = {k: [None] * DEPTH for k in ("gate", "up", "down", "w_in", "bp", "w_out", "glu_w")}
    d_norm = [None] * DEPTH
    d_consts = [None] * DEPTH
    for l in reversed(range(DEPTH)):
        t = str(l)
        sv0, sv1, sv2, cst = saved[l]
        dx, dxb, dn2, dg1, du1, dd1 = _ffn_bwd(t + "b", sv2, norm_w[l, 2][None], big["gate"][l, 1], big["up"][l, 1],
                                               big["down"][l, 1], dx, dxb, rows)
        dx, dxb, dn1, d_win, d_bp, d_wout, d_glu_w, dcst = _mixer_bwd(
            t, sv1, norm_w[l, 1][None], big["w_in"][l], big["bp"][l], big["w_out"][l], cst, dx, dxb, rows)
        dx, dxb, dn0, dg0, du0, dd0 = _ffn_bwd(t + "a", sv0, norm_w[l, 0][None], big["gate"][l, 0], big["up"][l, 0],
                                               big["down"][l, 0], dx, dxb, rows)
        g_big["gate"][l] = jnp.stack([dg0, dg1])
        g_big["up"][l] = jnp.stack([du0, du1])
        g_big["down"][l] = jnp.stack([dd0, dd1])
        g_big["w_in"][l] = d_win
        g_big["bp"][l] = d_bp
        g_big["w_out"][l] = d_wout
        g_big["glu_w"][l] = d_glu_w
        d_norm[l] = jnp.concatenate([dn0, dn1, dn2], axis=0)
        d_consts[l] = dcst
    g_big = {k: jnp.stack(v) for k, v in g_big.items()}
    (g_small,) = consts_vjp(d_consts)
    g_small = dict(g_small)
    g_small["norm_w"] = g_small["norm_w"] + jnp.stack(d_norm)
    g_small["final_norm_w"] = g_small["final_norm_w"] + d_fnw[0]
    return loss[0, 0], dx, g_big, g_small


MESH_IDS = pl.DeviceIdType.MESH
ANY_SPEC = pl.BlockSpec(memory_space=pl.ANY)


def _place():
    return lax.axis_index("x"), lax.axis_index("y"), lax.axis_index("c")


def _all_gather(name, shards):
    n = len(shards)

    def body(*refs):
        x_refs, out_refs = refs[:n], refs[n:2 * n]
        send_sems, recv_sems, local_sems = refs[2 * n:]
        x, y, c = _place()
        me, sibling = (x, y, c), (x, y, 1 - c)
        chips = [(1 - x, y), (x, 1 - y), (1 - x, 1 - y)]

        def blk(i, px, py, pc):
            return out_refs[i].at[4 * px + 2 * py + pc]

        def copy(i, k, block, to, src=None):
            return pltpu.make_async_remote_copy(
                src_ref=blk(i, *block) if src is None else src, dst_ref=blk(i, *block),
                send_sem=send_sems.at[7 * i + k], recv_sem=recv_sems.at[7 * i + k], device_id=to,
                device_id_type=MESH_IDS)

        mine = [pltpu.make_async_copy(x_refs[i], blk(i, *me), local_sems.at[i]) for i in range(n)]
        for cp in mine:
            cp.start()
        first = []
        for i in range(n):
            first.append(copy(i, 0, me, sibling, src=x_refs[i]))
            first += [copy(i, 1 + j, me, (*chip, c), src=x_refs[i]) for j, chip in enumerate(chips)]
        for cp in first:
            cp.start()
        passed = []
        for j, chip in enumerate(chips):
            for i in range(n):
                copy(i, 1 + j, (*chip, c), me).wait_recv()
                fwd = copy(i, 4 + j, (*chip, c), sibling)
                fwd.start()
                passed.append(fwd)
        for i in range(n):
            copy(i, 0, sibling, me).wait_recv()
            for j, chip in enumerate(chips):
                copy(i, 4 + j, (*chip, 1 - c), me).wait_recv()
        for cp in first + passed:
            cp.wait_send()
        for cp in mine:
            cp.wait()

    return pl.pallas_call(
        body, name=name, out_shape=[jax.ShapeDtypeStruct((N_DEV,) + s.shape, s.dtype) for s in shards],
        in_specs=[ANY_SPEC] * n, out_specs=[ANY_SPEC] * n,
        scratch_shapes=[pltpu.SemaphoreType.DMA((7 * n,)), pltpu.SemaphoreType.DMA((7 * n,)),
                        pltpu.SemaphoreType.DMA((n,))],
    )(*shards)


def _row_tile(rows):
    return rows if rows <= 512 else next(t for t in range(512, 7, -8) if rows % t == 0)


def _reduce_scatter(parts):
    n = len(parts)
    _, _, c = _place()

    def body_pair(*refs):
        p_refs, got_refs = refs[:n], refs[n:2 * n]
        send_sems, recv_sems = refs[2 * n:]
        x, y, c = _place()
        cps = [pltpu.make_async_remote_copy(
            src_ref=p_refs[i].at[1 - c], dst_ref=got_refs[i], send_sem=send_sems.at[i], recv_sem=recv_sems.at[i],
            device_id=(x, y, 1 - c), device_id_type=MESH_IDS) for i in range(n)]
        for cp in cps:
            cp.start()
        for cp in cps:
            cp.wait()

    from_sibling = pl.pallas_call(
        body_pair, name="rs_pair", out_shape=[jax.ShapeDtypeStruct(p.shape[1:], p.dtype) for p in parts],
        in_specs=[ANY_SPEC] * n, out_specs=[ANY_SPEC] * n,
        scratch_shapes=[pltpu.SemaphoreType.DMA((n,)), pltpu.SemaphoreType.DMA((n,))],
    )(*parts)

    chip_sums = []
    for i, (part, got) in enumerate(zip(parts, from_sibling)):
        _, _, r, cols = part.shape
        tr = _row_tile(r)

        def body_add(idx_ref, p_ref, g_ref, o_ref):
            o_ref[...] = (p_ref[...].astype(f32) + g_ref[...].astype(f32)).astype(o_ref.dtype)

        chip_sums.append(pl.pallas_call(
            body_add, name="rs_pair_sum_%d" % i, out_shape=jax.ShapeDtypeStruct((4, r, cols), part.dtype),
            grid_spec=pltpu.PrefetchScalarGridSpec(
                num_scalar_prefetch=1, grid=(4, r // tr),
                in_specs=[pl.BlockSpec((None, None, tr, cols), lambda j, t, idx: (idx[0], j, t, 0)),
                          pl.BlockSpec((None, tr, cols), lambda j, t, idx: (j, t, 0))],
                out_specs=pl.BlockSpec((None, tr, cols), lambda j, t, idx: (j, t, 0))),
            compiler_params=_cparams(("parallel", "parallel")),
        )(jnp.stack([c]).astype(jnp.int32), part, got))

    def body_chips(*refs):
        t_refs, got_refs = refs[:n], refs[n:2 * n]
        send_sems, recv_sems = refs[2 * n:]
        x, y, c = _place()
        chips = [(1 - x, y), (x, 1 - y), (1 - x, 1 - y)]
        cps = [pltpu.make_async_remote_copy(
            src_ref=t_refs[i].at[2 * px + py], dst_ref=got_refs[i].at[k], send_sem=send_sems.at[3 * i + k],
            recv_sem=recv_sems.at[3 * i + k], device_id=(px, py, c), device_id_type=MESH_IDS)
            for i in range(n) for k, (px, py) in enumerate(chips)]
        for cp in cps:
            cp.start()
        for cp in cps:
            cp.wait()

    from_chips = pl.pallas_call(
        body_chips, name="rs_chips", out_shape=[jax.ShapeDtypeStruct((3,) + p.shape[2:], p.dtype) for p in parts],
        in_specs=[ANY_SPEC] * n, out_specs=[ANY_SPEC] * n,
        scratch_shapes=[pltpu.SemaphoreType.DMA((3 * n,)), pltpu.SemaphoreType.DMA((3 * n,))],
    )(*chip_sums)
    return list(zip(chip_sums, from_chips))


def _own_index():
    x, y, _ = _place()
    return jnp.stack([2 * x + y]).astype(jnp.int32)


def _own_total(name, chip_sum, others):
    _, r, cols = chip_sum.shape
    tr = _row_tile(r)

    def body(idx_ref, t_ref, g_ref, o_ref):
        o_ref[...] = ((t_ref[...].astype(f32) + g_ref[0].astype(f32)) + g_ref[1].astype(f32)) + g_ref[2].astype(f32)

    return pl.pallas_call(
        body, name=name, out_shape=jax.ShapeDtypeStruct((r, cols), f32),
        grid_spec=pltpu.PrefetchScalarGridSpec(
            num_scalar_prefetch=1, grid=(r // tr,),
            in_specs=[pl.BlockSpec((None, tr, cols), lambda t, idx: (idx[0], t, 0)),
                      pl.BlockSpec((3, tr, cols), lambda t, idx: (0, t, 0))],
            out_specs=pl.BlockSpec((tr, cols), lambda t, idx: (t, 0))),
        compiler_params=_cparams(("parallel",)),
    )(_own_index(), chip_sum, others)


def _adam_update(w, gv, m, v):
    m_new = ADAM_B1 * m + (1.0 - ADAM_B1) * gv
    v_new = ADAM_B2 * v + (1.0 - ADAM_B2) * (gv * gv)
    m_hat = m_new / (1.0 - ADAM_B1 ** ADAM_STEP)
    v_hat = v_new / (1.0 - ADAM_B2 ** ADAM_STEP)
    return -ADAM_LR * (m_hat / (jnp.sqrt(v_hat) + ADAM_EPS) + ADAM_WD * w), m_new, v_new


def _adamw_reduced(name, w, chip_sum, others, m, v):
    rows, cols = w.shape
    tr = _row_tile(rows)

    def body(idx_ref, w_ref, t_ref, o_ref, m_ref, v_ref, g_ref, d_ref, nm_ref, nv_ref):
        gv = ((t_ref[...].astype(f32) + o_ref[0].astype(f32)) + o_ref[1].astype(f32)) + o_ref[2].astype(f32)
        g_ref[...] = gv
        d_ref[...], nm_ref[...], nv_ref[...] = _adam_update(w_ref[...], gv, m_ref[...], v_ref[...])

    spec = pl.BlockSpec((tr, cols), lambda t, idx: (t, 0))
    return pl.pallas_call(
        body, name=name, out_shape=[jax.ShapeDtypeStruct((rows, cols), f32)] * 4,
        grid_spec=pltpu.PrefetchScalarGridSpec(
            num_scalar_prefetch=1, grid=(rows // tr,),
            in_specs=[spec, pl.BlockSpec((None, tr, cols), lambda t, idx: (idx[0], t, 0)),
                      pl.BlockSpec((3, tr, cols), lambda t, idx: (0, t, 0)), spec, spec],
            out_specs=[spec] * 4),
        compiler_params=_cparams(("parallel",)),
    )(_own_index(), w, chip_sum, others, m, v)


def _adamw(name, w, g, m, v):
    rows, cols = w.shape
    tr = _row_tile(rows)

    def body(w_ref, g_ref, m_ref, v_ref, d_ref, nm_ref, nv_ref):
        d_ref[...], nm_ref[...], nv_ref[...] = _adam_update(w_ref[...], g_ref[...], m_ref[...], v_ref[...])

    spec = pl.BlockSpec((tr, cols), lambda i: (i, 0))
    return pl.pallas_call(
        body, name=name, grid=(rows // tr,), in_specs=[spec] * 4, out_specs=[spec] * 3,
        out_shape=[jax.ShapeDtypeStruct((rows, cols), f32)] * 3, compiler_params=_cparams(("parallel",)),
    )(w, g, m, v)


WEIGHT_NAMES = ["norm_w", "final_norm_w", "ffn_gate", "ffn_up", "ffn_down", "w_in", "branch_proj", "w_out",
                "s5_lambda_re", "s5_lambda_im", "s5_log_dt", "s5_b_re", "s5_b_im", "s5_c_re", "s5_c_im", "s5_d",
                "s5_glu_w", "s5_glu_b", "hg_lb_logits", "hg_norm_w", "rg_conv_w", "rg_conv_b", "rg_wa", "rg_ba",
                "rg_wx", "rg_bx", "rg_lambda"]
SHARDED = {"ffn_gate": (3, "gate"), "ffn_up": (3, "up"), "ffn_down": (2, "down"), "w_in": (2, "w_in"),
           "branch_proj": (3, "bp"), "w_out": (1, "w_out"), "s5_glu_w": (1, "glu_w"),
           "norm_w": (2, None), "rg_conv_w": (2, None)}
BIG = ["ffn_gate", "ffn_up", "ffn_down", "w_in", "branch_proj", "w_out", "s5_glu_w"]
SMALL_SHARDED = ["norm_w", "rg_conv_w"]
REPLICATED = [n for n in WEIGHT_NAMES if n not in SHARDED]
LANES = 128


PACK_ROWS = 512


def _pack_rows(arrays, names):
    pieces = []
    for n in names:
        flat = arrays[n].reshape(-1)
        pieces.append(jnp.pad(flat, (0, -flat.shape[0] % LANES)).reshape(-1, LANES))
    rows = jnp.concatenate(pieces, axis=0)
    return jnp.pad(rows, ((0, -rows.shape[0] % PACK_ROWS), (0, 0)))


def _unpack_rows(rows, names, like):
    out, r0 = {}, 0
    for n in names:
        size = math.prod(like[n].shape)
        nrows = -(-size // LANES)
        out[n] = rows[r0:r0 + nrows].reshape(-1)[:size].reshape(like[n].shape)
        r0 += nrows
    return out


def _unshard(gathered, axis):
    g = jnp.moveaxis(gathered, 0, axis)
    shp = g.shape
    return g.reshape(shp[:axis] + (shp[axis] * shp[axis + 1],) + shp[axis + 2:])


def _to_blocks(full, axis):
    shp = full.shape
    g = full.reshape(shp[:axis] + (4, 2, shp[axis] // N_DEV) + shp[axis + 1:])
    g = jnp.moveaxis(g, (axis, axis + 1), (1, 0))
    return g.reshape(2, 4, -1, g.shape[-1])


W_IN_SPLIT = IN_TOTAL - GM_WIDTH


def kernel(x, norm_w, final_norm_w, ffn_gate, ffn_up, ffn_down, w_in, branch_proj, w_out, s5_lambda_re, s5_lambda_im, s5_log_dt, s5_b_re, s5_b_im, s5_c_re, s5_c_im, s5_d, s5_glu_w, s5_glu_b, hg_lb_logits, hg_norm_w, rg_conv_w, rg_conv_b, rg_wa, rg_ba, rg_wx, rg_bx, rg_lambda, loss_target, m_norm_w, m_final_norm_w, m_ffn_gate, m_ffn_up, m_ffn_down, m_w_in, m_branch_proj, m_w_out, m_s5_lambda_re, m_s5_lambda_im, m_s5_log_dt, m_s5_b_re, m_s5_b_im, m_s5_c_re, m_s5_c_im, m_s5_d, m_s5_glu_w, m_s5_glu_b, m_hg_lb_logits, m_hg_norm_w, m_rg_conv_w, m_rg_conv_b, m_rg_wa, m_rg_ba, m_rg_wx, m_rg_bx, m_rg_lambda, v_norm_w, v_final_norm_w, v_ffn_gate, v_ffn_up, v_ffn_down, v_w_in, v_branch_proj, v_w_out, v_s5_lambda_re, v_s5_lambda_im, v_s5_log_dt, v_s5_b_re, v_s5_b_im, v_s5_c_re, v_s5_c_im, v_s5_d, v_s5_glu_w, v_s5_glu_b, v_hg_lb_logits, v_hg_norm_w, v_rg_conv_w, v_rg_conv_b, v_rg_wa, v_rg_ba, v_rg_wx, v_rg_bx, v_rg_lambda):
    w = dict(zip(WEIGHT_NAMES, (norm_w, final_norm_w, ffn_gate, ffn_up, ffn_down, w_in, branch_proj, w_out,
                                s5_lambda_re, s5_lambda_im, s5_log_dt, s5_b_re, s5_b_im, s5_c_re, s5_c_im, s5_d,
                                s5_glu_w, s5_glu_b, hg_lb_logits, hg_norm_w, rg_conv_w, rg_conv_b, rg_wa, rg_ba,
                                rg_wx, rg_bx, rg_lambda)))
    m = dict(zip(WEIGHT_NAMES, (m_norm_w, m_final_norm_w, m_ffn_gate, m_ffn_up, m_ffn_down, m_w_in, m_branch_proj,
                                m_w_out, m_s5_lambda_re, m_s5_lambda_im, m_s5_log_dt, m_s5_b_re, m_s5_b_im, m_s5_c_re,
                                m_s5_c_im, m_s5_d, m_s5_glu_w, m_s5_glu_b, m_hg_lb_logits, m_hg_norm_w, m_rg_conv_w,
                                m_rg_conv_b, m_rg_wa, m_rg_ba, m_rg_wx, m_rg_bx, m_rg_lambda)))
    v = dict(zip(WEIGHT_NAMES, (v_norm_w, v_final_norm_w, v_ffn_gate, v_ffn_up, v_ffn_down, v_w_in, v_branch_proj,
                                v_w_out, v_s5_lambda_re, v_s5_lambda_im, v_s5_log_dt, v_s5_b_re, v_s5_b_im, v_s5_c_re,
                                v_s5_c_im, v_s5_d, v_s5_glu_w, v_s5_glu_b, v_hg_lb_logits, v_hg_norm_w, v_rg_conv_w,
                                v_rg_conv_b, v_rg_wa, v_rg_ba, v_rg_wx, v_rg_bx, v_rg_lambda)))
    rows = x.shape[1]

    sharded = BIG + SMALL_SHARDED
    gathered = _all_gather("gather_weights", [w[n].astype(bf16) for n in BIG] + [w[n] for n in SMALL_SHARDED])
    full = {n: _unshard(g, SHARDED[n][0]) for n, g in zip(sharded, gathered)}
    big = {SHARDED[n][1]: full[n] for n in BIG}
    big["w_in"] = jnp.concatenate([big["w_in"][..., W_IN_SPLIT:], big["w_in"][..., :W_IN_SPLIT]], axis=-1)
    small = {n: w[n] for n in REPLICATED}
    small["norm_w"] = full["norm_w"]
    small["rg_conv_w"] = full["rg_conv_w"]

    loss_part, dx, g_big, g_small = _local_step(x[0], loss_target[0], big, small)
    g_big["w_in"] = jnp.concatenate([g_big["w_in"][..., GM_WIDTH:], g_big["w_in"][..., :GM_WIDTH]], axis=-1)
    loss = lax.psum(loss_part, ("x", "y", "c"))

    parts = [_to_blocks(g_big[SHARDED[n][1]], SHARDED[n][0]).astype(bf16) for n in BIG]
    parts += [_to_blocks(g_small[n], SHARDED[n][0]) for n in SMALL_SHARDED]
    rep_rows = _pack_rows(g_small, REPLICATED)
    rep_slice = rep_rows.shape[0] // N_DEV
    parts.append(rep_rows.reshape(4, 2, rep_slice, LANES).transpose(1, 0, 2, 3))
    sums = _reduce_scatter(parts)

    grads, delta, new_m, new_v = {}, {}, {}, {}
    for n, (chip_sum, others) in zip(sharded, sums):
        shp = w[n].shape
        view = (-1, shp[-1])
        res = _adamw_reduced("adamw_" + n, w[n].reshape(view), chip_sum, others, m[n].reshape(view), v[n].reshape(view))
        grads[n], delta[n], new_m[n], new_v[n] = (r.reshape(shp) for r in res)
    rep_mine = _own_total("rs_total_small", *sums[-1])
    rep_grads = _all_gather("gather_small_grads", [rep_mine])[0].reshape(-1, LANES)
    res = _adamw("adamw_small", _pack_rows(w, REPLICATED), rep_grads, _pack_rows(m, REPLICATED), _pack_rows(v, REPLICATED))
    for dst, src in zip((grads, delta, new_m, new_v), (rep_grads,) + tuple(res)):
        dst.update(_unpack_rows(src, REPLICATED, w))

    return (loss, dx.reshape(x.shape), *[grads[n] for n in WEIGHT_NAMES], *[delta[n] for n in WEIGHT_NAMES],
            *[new_m[n] for n in WEIGHT_NAMES], *[new_v[n] for n in WEIGHT_NAMES])
```

```python
import functools
import math

import jax
import jax.numpy as jnp
from jax import lax
from jax.experimental import pallas as pl
from jax.experimental.pallas import tpu as pltpu

f32 = jnp.float32
bf16 = jnp.bfloat16

D_MODEL = 1024
DEPTH = 2
BRANCH = 512
N_BRANCH = 3
S5_GROUP = 16
S5_GROUPS = 32
S5_STATE = 64
S5_LANES = S5_GROUPS * S5_STATE
S5_EIG_MAX = -1e-4
HG_HEADS = 4
HG_DK = 128
HG_CHUNK = 32
RG_BLOCKS = 8
RG_BLOCK = 64
RG_C = 8.0
D_FF = 2816
EPS = 1e-6
IN_TOTAL = 6656
GM_WIDTH = N_BRANCH * D_MODEL
N_DEV = 8

ADAM_LR = 0.001
ADAM_B1 = 0.9
ADAM_B2 = 0.999
ADAM_EPS = 1e-08
ADAM_WD = 0.01
ADAM_STEP = 10

VMEM_LIMIT_V7X = 56 * 1024 * 1024
ROW_TILE = 256
FF_TILE = 1408
TOKEN_K = 2048


def _cparams(sem):
    return pltpu.CompilerParams(dimension_semantics=sem, vmem_limit_bytes=VMEM_LIMIT_V7X)


def _sigmoid(x):
    return 1.0 / (1.0 + jnp.exp(-x))


_GELU_C = math.sqrt(2.0 / math.pi)


def _gelu(x):
    t = jnp.tanh(_GELU_C * (x + 0.044715 * x * x * x))
    return 0.5 * x * (1.0 + t)


def _gelu_grad(x):
    t = jnp.tanh(_GELU_C * (x + 0.044715 * x * x * x))
    return 0.5 * (1.0 + t) + 0.5 * x * (1.0 - t * t) * _GELU_C * (1.0 + 3.0 * 0.044715 * x * x)


def _expm1(x):
    p = x * (1.0 + x * (0.5 + x * (1.0 / 6 + x * (1.0 / 24 + x * (1.0 / 120 + x * (1.0 / 720))))))
    return jnp.where(jnp.abs(x) < 0.3, p, jnp.exp(x) - 1.0)


def _dot(a, b):
    return jnp.dot(a, b, preferred_element_type=f32)


def _dot_nt(a, b):
    return lax.dot_general(a, b, (((1,), (1,)), ((), ())), preferred_element_type=f32)


def _dot_tn(a, b):
    return lax.dot_general(a, b, (((0,), (0,)), ((), ())), preferred_element_type=f32)


def _rows(shape):
    return lax.broadcasted_iota(jnp.int32, shape, 0)


def _scan_fwd(a, b, n):
    row = _rows(a.shape)
    s = 1
    while s < n:
        valid = row >= s
        sh_a = pltpu.roll(a, s, 0)
        sh_b = pltpu.roll(b, s, 0)
        b = b + a * jnp.where(valid, sh_b, 0.0)
        a = a * jnp.where(valid, sh_a, 1.0)
        s *= 2
    return a, b


def _scan_bwd(a, b, n):
    row = _rows(a.shape)
    s = 1
    while s < n:
        valid = row < n - s
        sh_a = pltpu.roll(a, n - s, 0)
        sh_b = pltpu.roll(b, n - s, 0)
        b = b + a * jnp.where(valid, sh_b, 0.0)
        a = a * jnp.where(valid, sh_a, 1.0)
        s *= 2
    return a, b


def _seg_cumsum(x, n, seg):
    pos = _rows(x.shape) % seg
    s = 1
    while s < seg:
        x = x + jnp.where(pos >= s, pltpu.roll(x, s, 0), 0.0)
        s *= 2
    return x


def _seg_rev_cumsum(x, n, seg):
    pos = _rows(x.shape) % seg
    s = 1
    while s < seg:
        x = x + jnp.where(pos < seg - s, pltpu.roll(x, n - s, 0), 0.0)
        s *= 2
    return x


def _head_mean(x):
    parts = []
    for h in range(HG_HEADS):
        m = jnp.mean(x[:, h * HG_DK:(h + 1) * HG_DK], axis=1, keepdims=True)
        parts.append(jnp.broadcast_to(m, (x.shape[0], HG_DK)))
    return jnp.concatenate(parts, axis=1)


def _mm(name, a_list, b_list, terms, n_acc, mode, m, n, k, tm, tn, tk, out_dtypes, epilogue, extras=()):
    tm, tn, tk = min(tm, m), min(tn, n), min(tk, k)
    assert m % tm == 0 and n % tn == 0 and k % tk == 0, (name, m, n, k, tm, tn, tk)
    gk = k // tk
    if mode == "tn":
        a_spec = pl.BlockSpec((tk, tm), lambda i, j, kk: (kk, i))
    else:
        a_spec = pl.BlockSpec((tm, tk), lambda i, j, kk: (i, kk))
    if mode == "nt":
        b_spec = pl.BlockSpec((tn, tk), lambda i, j, kk: (j, kk))
    else:
        b_spec = pl.BlockSpec((tk, tn), lambda i, j, kk: (kk, j))
    o_spec = pl.BlockSpec((tm, tn), lambda i, j, kk: (i, j))
    dot = {"nn": _dot, "nt": _dot_nt, "tn": _dot_tn}[mode]
    na, nb, ne, no = len(a_list), len(b_list), len(extras), len(out_dtypes)

    def kern(*refs):
        a_refs = refs[:na]
        b_refs = refs[na:na + nb]
        e_refs = refs[na + nb:na + nb + ne]
        o_refs = refs[na + nb + ne:na + nb + ne + no]
        acc = refs[na + nb + ne + no]
        kk = pl.program_id(2)

        @pl.when(kk == 0)
        def _():
            acc[...] = jnp.zeros_like(acc)

        for ai, bi, ci in terms:
            acc[ci] += dot(a_refs[ai][...].astype(bf16), b_refs[bi][...].astype(bf16))

        @pl.when(kk == gk - 1)
        def _():
            outs = epilogue([acc[c] for c in range(n_acc)], [e[...] for e in e_refs])
            for o, val in zip(o_refs, outs):
                o[...] = val.astype(o.dtype)

    res = pl.pallas_call(
        kern, name=name,
        grid=(m // tm, n // tn, gk),
        in_specs=[a_spec] * na + [b_spec] * nb + [o_spec] * ne,
        out_specs=[o_spec] * no,
        out_shape=[jax.ShapeDtypeStruct((m, n), dt) for dt in out_dtypes],
        scratch_shapes=[pltpu.VMEM((n_acc, tm, tn), f32)],
        compiler_params=_cparams(("parallel", "parallel", "arbitrary")),
    )(*a_list, *b_list, *extras)
    return res


def _mm1(name, a, b, mode, m, n, k, tm, tn, tk, out_dtype=f32, scale=None):
    def epi(accs, extras):
        return [accs[0] if scale is None else accs[0] * scale]
    return _mm(name, [a], [b], [(0, 0, 0)], 1, mode, m, n, k, tm, tn, tk, [out_dtype], epi)[0]


def _rt(name, body, rows, tm, row_ins, consts, row_outs, acc_outs=(), scratch=(), reverse=False):
    tm = min(tm, rows)
    assert rows % tm == 0
    nt = rows // tm

    def tile(i):
        return nt - 1 - i if reverse else i

    in_specs, args = [], []
    for spec in row_ins:
        arr = spec[0]
        if isinstance(spec[1], int):
            in_specs.append(pl.BlockSpec((tm, spec[1]), lambda i, cb=spec[2]: (tile(i), cb)))
        else:
            in_specs.append(pl.BlockSpec(spec[1], lambda i, fn=spec[2]: fn(tile(i))))
        args.append(arr)
    for c in consts:
        in_specs.append(pl.BlockSpec(c.shape, lambda i, nd=c.ndim: (0,) * nd))
        args.append(c)
    out_specs, out_shape = [], []
    for spec in row_outs:
        if isinstance(spec[0], int):
            out_specs.append(pl.BlockSpec((tm, spec[0]), lambda i: (tile(i), 0)))
            out_shape.append(jax.ShapeDtypeStruct((rows, spec[0]), spec[1]))
        else:
            out_specs.append(pl.BlockSpec(spec[1], lambda i, fn=spec[2]: fn(tile(i))))
            out_shape.append(jax.ShapeDtypeStruct(spec[0], spec[3]))
    for shp in acc_outs:
        out_specs.append(pl.BlockSpec(shp, lambda i, nd=len(shp): (0,) * nd))
        out_shape.append(jax.ShapeDtypeStruct(shp, f32))
    n_in = len(args)
    n_row_out = len(row_outs)
    n_acc = len(acc_outs)

    def kern(*refs):
        i = pl.program_id(0)
        acc_refs = refs[n_in + n_row_out:n_in + n_row_out + n_acc]

        @pl.when(i == 0)
        def _():
            for r in acc_refs:
                r[...] = jnp.zeros_like(r)

        body(i, *refs)

    return pl.pallas_call(
        kern, name=name, grid=(nt,), in_specs=in_specs, out_specs=out_specs, out_shape=out_shape,
        scratch_shapes=list(scratch), compiler_params=_cparams(("arbitrary",)),
    )(*args)


def _rms_fwd(name, x, w, rows):
    def body(i, x_ref, w_ref, h_ref):
        xv = x_ref[...]
        r = lax.rsqrt(jnp.mean(xv * xv, axis=1, keepdims=True) + EPS)
        h_ref[...] = (xv * r * w_ref[...]).astype(bf16)
    return _rt(name, body, rows, ROW_TILE, [(x, D_MODEL, 0)], [w], [(D_MODEL, bf16)])[0]


def _rms_bwd(name, x, dh, w, dres, rows):
    def body(i, x_ref, dh_ref, dres_ref, w_ref, dx_ref, dxb_ref, dw_ref):
        xv = x_ref[...]
        r = lax.rsqrt(jnp.mean(xv * xv, axis=1, keepdims=True) + EPS)
        xn = xv * r
        dhv = dh_ref[...]
        dxn = dhv * w_ref[...]
        dx = dres_ref[...] + r * (dxn - xn * jnp.mean(dxn * xn, axis=1, keepdims=True))
        dx_ref[...] = dx
        dxb_ref[...] = dx.astype(bf16)
        dw_ref[...] += jnp.sum(dhv * xn, axis=0, keepdims=True)
    return _rt(name, body, rows, ROW_TILE, [(x, D_MODEL, 0), (dh, D_MODEL, 0), (dres, D_MODEL, 0)], [w],
               [(D_MODEL, f32), (D_MODEL, bf16)], acc_outs=[(1, D_MODEL)])


def _loss_head(x, w, target, rows):
    def body(i, x_ref, t_ref, w_ref, dx_ref, dxb_ref, loss_ref, dw_ref):
        xv = x_ref[...]
        r = lax.rsqrt(jnp.mean(xv * xv, axis=1, keepdims=True) + EPS)
        xn = xv * r
        wv = w_ref[...]
        err = xn * wv - t_ref[...]
        part = 0.5 * jnp.sum(jnp.mean(err * err, axis=1, keepdims=True), axis=0, keepdims=True)
        loss_ref[...] += jnp.broadcast_to(part, (1, 128))
        dy = err * (1.0 / D_MODEL)
        dxn = dy * wv
        dx = r * (dxn - xn * jnp.mean(dxn * xn, axis=1, keepdims=True))
        dx_ref[...] = dx
        dxb_ref[...] = dx.astype(bf16)
        dw_ref[...] += jnp.sum(dy * xn, axis=0, keepdims=True)
    return _rt("loss_head", body, rows, ROW_TILE, [(x, D_MODEL, 0), (target, D_MODEL, 0)], [w],
               [(D_MODEL, f32), (D_MODEL, bf16)], acc_outs=[(1, 128), (1, D_MODEL)])


def _ffn_fwd(tag, x, nw, wg, wu, wd, rows):
    hb = _rms_fwd("ffn_norm_" + tag, x, nw, rows)

    def epi_up(accs, extras):
        a, b = accs
        return [a, b, a * _sigmoid(a) * b]
    a, b, s = _mm("ffn_up_" + tag, [hb], [wg, wu], [(0, 0, 0), (0, 1, 1)], 2, "nn", rows, D_FF, D_MODEL,
                  512, FF_TILE, D_MODEL, [bf16, bf16, bf16], epi_up)

    def epi_down(accs, extras):
        return [extras[0] + 0.5 * accs[0]]
    x_out = _mm("ffn_down_" + tag, [s], [wd], [(0, 0, 0)], 1, "nn", rows, D_MODEL, D_FF,
                512, D_MODEL, FF_TILE, [f32], epi_down, extras=[x])[0]
    return x_out, (x, hb, a, b, s)


def _ffn_bwd(tag, saved, nw, wg, wu, wd, dx, dxb, rows):
    x, hb, a, b, s = saved

    def epi_mid(accs, extras):
        ds = 0.5 * accs[0]
        av = extras[0].astype(f32)
        bv = extras[1].astype(f32)
        sg = _sigmoid(av)
        return [ds * bv * sg * (1.0 + av * (1.0 - sg)), ds * av * sg]
    da, db = _mm("ffn_bwd_mid_" + tag, [dxb], [wd], [(0, 0, 0)], 1, "nt", rows, D_FF, D_MODEL,
                 512, FF_TILE, D_MODEL, [bf16, bf16], epi_mid, extras=[a, b])
    d_wd = _mm1("ffn_dwd_" + tag, s, dxb, "tn", D_FF, D_MODEL, rows, FF_TILE, D_MODEL, TOKEN_K, out_dtype=bf16,
                scale=0.5)
    d_wg = _mm1("ffn_dwg_" + tag, hb, da, "tn", D_MODEL, D_FF, rows, D_MODEL, FF_TILE, TOKEN_K, out_dtype=bf16)
    d_wu = _mm1("ffn_dwu_" + tag, hb, db, "tn", D_MODEL, D_FF, rows, D_MODEL, FF_TILE, TOKEN_K, out_dtype=bf16)
    dh = _mm("ffn_dh_" + tag, [da, db], [wg, wu], [(0, 0, 0), (1, 1, 0)], 1, "nt", rows, D_MODEL, D_FF,
             512, D_MODEL, FF_TILE, [f32], lambda accs, extras: [accs[0]])[0]
    dx_in, dxb_in, d_nw = _rms_bwd("ffn_norm_bwd_" + tag, x, dh, nw, dx, rows)
    return dx_in, dxb_in, d_nw, d_wg, d_wu, d_wd


S5_CB = 512
SUBLANES = 8
U_COL = GM_WIDTH // BRANCH


def _s5_scan_fwd(tag, proj, b_re, b_im, a_re, a_im, rows):
    tm = min(ROW_TILE, rows)
    nt = rows // tm
    nc = S5_LANES // S5_CB

    def kern(u_ref, bre_ref, bim_ref, ar_ref, ai_ref, xr_ref, xi_ref, pr_s, pi_s, cr_s, ci_s):
        t = pl.program_id(1)

        @pl.when(t == 0)
        def _():
            row8 = _rows((SUBLANES, S5_CB))
            pr = jnp.broadcast_to(ar_ref[...], (SUBLANES, S5_CB))
            pi = jnp.broadcast_to(ai_ref[...], (SUBLANES, S5_CB))
            s = 1
            while s < SUBLANES:
                sr = pltpu.roll(pr, s, 0)
                si = pltpu.roll(pi, s, 0)
                valid = row8 >= s
                pr, pi = jnp.where(valid, pr * sr - pi * si, pr), jnp.where(valid, pr * si + pi * sr, pi)
                s *= 2
            pr_s[...] = pr
            pi_s[...] = pi
            cr_s[...] = jnp.zeros_like(cr_s)
            ci_s[...] = jnp.zeros_like(ci_s)

        ub = u_ref[...].astype(bf16)
        br = _dot(ub, bre_ref[...])
        bi = _dot(ub, bim_ref[...])
        pos = _rows((tm, S5_CB)) % SUBLANES
        s = 1
        while s < SUBLANES:
            mr = pr_s[s - 1:s, :]
            mi = pi_s[s - 1:s, :]
            sr = pltpu.roll(br, s, 0)
            si = pltpu.roll(bi, s, 0)
            valid = pos >= s
            br, bi = (br + jnp.where(valid, mr * sr - mi * si, 0.0),
                      bi + jnp.where(valid, mr * si + mi * sr, 0.0))
            s *= 2
        cr = cr_s[...]
        ci = ci_s[...]
        pr = pr_s[...]
        pi = pi_s[...]
        for g in range(tm // SUBLANES):
            sl = slice(g * SUBLANES, (g + 1) * SUBLANES)
            xr = br[sl] + pr * cr - pi * ci
            xi = bi[sl] + pr * ci + pi * cr
            xr_ref[sl, :] = xr
            xi_ref[sl, :] = xi
            cr = xr[SUBLANES - 1:SUBLANES, :]
            ci = xi[SUBLANES - 1:SUBLANES, :]
        cr_s[...] = cr
        ci_s[...] = ci

    return pl.pallas_call(
        kern, name="s5_scan_fwd_" + tag, grid=(nc, nt),
        in_specs=[pl.BlockSpec((tm, BRANCH), lambda c, t: (t, U_COL)),
                  pl.BlockSpec((BRANCH, S5_CB), lambda c, t: (0, c)),
                  pl.BlockSpec((BRANCH, S5_CB), lambda c, t: (0, c)),
                  pl.BlockSpec((1, S5_CB), lambda c, t: (0, c)),
                  pl.BlockSpec((1, S5_CB), lambda c, t: (0, c))],
        out_specs=[pl.BlockSpec((tm, S5_CB), lambda c, t: (t, c))] * 2,
        out_shape=[jax.ShapeDtypeStruct((rows, S5_LANES), f32)] * 2,
        scratch_shapes=[pltpu.VMEM((SUBLANES, S5_CB), f32), pltpu.VMEM((SUBLANES, S5_CB), f32),
                        pltpu.VMEM((1, S5_CB), f32), pltpu.VMEM((1, S5_CB), f32)],
        compiler_params=_cparams(("parallel", "arbitrary")),
    )(proj, b_re, b_im, a_re, a_im)


def _s5_scan_bwd(tag, dxr, dxi, xr, xi, a_re, a_im, rows):
    tm = min(ROW_TILE, rows)
    nt = rows // tm
    nc = S5_LANES // S5_CB

    def kern(dxr_ref, dxi_ref, xr_ref, xi_ref, ar_ref, ai_ref, gr_ref, gi_ref, dar_ref, dai_ref,
             qr_s, qi_s, cr_s, ci_s, gr_s, gi_s):
        t = pl.program_id(1)
        row = _rows((tm, S5_CB))

        @pl.when(t == 0)
        def _():
            row8 = _rows((SUBLANES, S5_CB))
            qr = jnp.broadcast_to(ar_ref[...], (SUBLANES, S5_CB))
            qi = jnp.broadcast_to(-ai_ref[...], (SUBLANES, S5_CB))
            s = 1
            while s < SUBLANES:
                sr = pltpu.roll(qr, SUBLANES - s, 0)
                si = pltpu.roll(qi, SUBLANES - s, 0)
                valid = row8 < SUBLANES - s
                qr, qi = jnp.where(valid, qr * sr - qi * si, qr), jnp.where(valid, qr * si + qi * sr, qi)
                s *= 2
            qr_s[...] = qr
            qi_s[...] = qi
            cr_s[...] = jnp.zeros_like(cr_s)
            ci_s[...] = jnp.zeros_like(ci_s)
            dar_ref[...] = jnp.zeros_like(dar_ref)
            dai_ref[...] = jnp.zeros_like(dai_ref)

        br = dxr_ref[...]
        bi = dxi_ref[...]
        pos = row % SUBLANES
        s = 1
        while s < SUBLANES:
            mr = qr_s[SUBLANES - s:SUBLANES - s + 1, :]
            mi = qi_s[SUBLANES - s:SUBLANES - s + 1, :]
            sr = pltpu.roll(br, tm - s, 0)
            si = pltpu.roll(bi, tm - s, 0)
            valid = pos < SUBLANES - s
            br, bi = (br + jnp.where(valid, mr * sr - mi * si, 0.0),
                      bi + jnp.where(valid, mr * si + mi * sr, 0.0))
            s *= 2
        cin_r = cr_s[...]
        cin_i = ci_s[...]
        cr, ci = cin_r, cin_i
        qr = qr_s[...]
        qi = qi_s[...]
        for g in reversed(range(tm // SUBLANES)):
            sl = slice(g * SUBLANES, (g + 1) * SUBLANES)
            gr = br[sl] + qr * cr - qi * ci
            gi = bi[sl] + qr * ci + qi * cr
            gr_s[sl, :] = gr
            gi_s[sl, :] = gi
            cr = gr[0:1, :]
            ci = gi[0:1, :]
        cr_s[...] = cr
        ci_s[...] = ci
        gr = gr_s[...]
        gi = gi_s[...]
        gr_ref[...] = gr.astype(bf16)
        gi_ref[...] = gi.astype(bf16)
        last = row == tm - 1
        gnr = jnp.where(last, cin_r, pltpu.roll(gr, tm - 1, 0))
        gni = jnp.where(last, cin_i, pltpu.roll(gi, tm - 1, 0))
        xr_v = xr_ref[...]
        xi_v = xi_ref[...]
        dar_ref[...] += jnp.sum(gnr * xr_v + gni * xi_v, axis=0, keepdims=True)
        dai_ref[...] += jnp.sum(gni * xr_v - gnr * xi_v, axis=0, keepdims=True)

    rev = lambda c, t: (nt - 1 - t, c)
    return pl.pallas_call(
        kern, name="s5_scan_bwd_" + tag, grid=(nc, nt),
        in_specs=[pl.BlockSpec((tm, S5_CB), rev)] * 4 + [pl.BlockSpec((1, S5_CB), lambda c, t: (0, c))] * 2,
        out_specs=[pl.BlockSpec((tm, S5_CB), rev)] * 2 + [pl.BlockSpec((1, S5_CB), lambda c, t: (0, c))] * 2,
        out_shape=[jax.ShapeDtypeStruct((rows, S5_LANES), bf16)] * 2 + [jax.ShapeDtypeStruct((1, S5_LANES), f32)] * 2,
        scratch_shapes=[pltpu.VMEM((SUBLANES, S5_CB), f32), pltpu.VMEM((SUBLANES, S5_CB), f32),
                        pltpu.VMEM((1, S5_CB), f32), pltpu.VMEM((1, S5_CB), f32),
                        pltpu.VMEM((tm, S5_CB), f32), pltpu.VMEM((tm, S5_CB), f32)],
        compiler_params=_cparams(("parallel", "arbitrary")),
    )(dxr, dxi, xr, xi, a_re, a_im)


def _s5_fwd(tag, proj, cst, rows):
    xr, xi = _s5_scan_fwd(tag, proj, cst["b_re"].astype(bf16), cst["b_im"].astype(bf16), cst["a_re"], cst["a_im"], rows)

    def body(i, xr_ref, xi_ref, u_ref, cre_ref, cim_ref, d_ref, gw_ref, gb_ref, y_ref, out_ref):
        y = (_dot(xr_ref[...].astype(bf16), cre_ref[...]) + _dot(xi_ref[...].astype(bf16), cim_ref[...])
             + d_ref[...] * u_ref[...])
        y_ref[...] = y
        z = _gelu(y)
        zg = _dot(z.astype(bf16), gw_ref[...]) + gb_ref[...]
        out_ref[...] = (z * _sigmoid(zg)).astype(bf16)

    y, out = _rt("s5_out_" + tag, body, rows, ROW_TILE,
                 [(xr, S5_LANES, 0), (xi, S5_LANES, 0), (proj, BRANCH, U_COL)],
                 [cst["c_re"].astype(bf16), cst["c_im"].astype(bf16), cst["s5_d"], cst["glu_w"], cst["glu_b"]],
                 [(BRANCH, f32), (BRANCH, bf16)])
    return out, (xr, xi, y)


def _s5_bwd(tag, saved, proj, cst, d_out, rows):
    xr, xi, y = saved
    c_re = cst["c_re"].astype(bf16)
    c_im = cst["c_im"].astype(bf16)

    def body(i, do_ref, y_ref, u_ref, xr_ref, xi_ref, cre_ref, cim_ref, gw_ref, gb_ref,
             dxr_ref, dxi_ref, dy_ref, dgw_ref, dgb_ref, dd_ref, dcre_ref, dcim_ref):
        yv = y_ref[...]
        z = _gelu(yv)
        zb = z.astype(bf16)
        gt = _sigmoid(_dot(zb, gw_ref[...]) + gb_ref[...])
        dov = do_ref[...]
        dzg = dov * z * gt * (1.0 - gt)
        dzgb = dzg.astype(bf16)
        dz = dov * gt + _dot_nt(dzgb, gw_ref[...])
        dgw_ref[...] += _dot_tn(zb, dzgb)
        dgb_ref[...] += jnp.sum(dzg, axis=0, keepdims=True)
        dy = dz * _gelu_grad(yv)
        dy_ref[...] = dy
        dd_ref[...] += jnp.sum(dy * u_ref[...], axis=0, keepdims=True)
        dyb = dy.astype(bf16)
        dxr_ref[...] = _dot_nt(dyb, cre_ref[...])
        dxi_ref[...] = _dot_nt(dyb, cim_ref[...])
        dcre_ref[...] += _dot_tn(xr_ref[...].astype(bf16), dyb)
        dcim_ref[...] += _dot_tn(xi_ref[...].astype(bf16), dyb)

    dxr, dxi, dy, d_gw, d_gb, d_d, d_cre, d_cim = _rt(
        "s5_out_bwd_" + tag, body, rows, ROW_TILE,
        [(d_out, BRANCH, 0), (y, BRANCH, 0), (proj, BRANCH, U_COL), (xr, S5_LANES, 0), (xi, S5_LANES, 0)],
        [c_re, c_im, cst["glu_w"], cst["glu_b"]],
        [(S5_LANES, f32), (S5_LANES, f32), (BRANCH, f32)],
        acc_outs=[(BRANCH, BRANCH), (1, BRANCH), (1, BRANCH), (S5_LANES, BRANCH), (S5_LANES, BRANCH)])

    gr, gi, d_ar, d_ai = _s5_scan_bwd(tag, dxr, dxi, xr, xi, cst["a_re"], cst["a_im"], rows)
    b_re = cst["b_re"].astype(bf16)
    b_im = cst["b_im"].astype(bf16)

    def body_in(i, gr_ref, gi_ref, dy_ref, u_ref, bre_ref, bim_ref, d_ref, du_ref, dbre_ref, dbim_ref):
        grv = gr_ref[...]
        giv = gi_ref[...]
        du = _dot_nt(grv, bre_ref[...]) + _dot_nt(giv, bim_ref[...]) + dy_ref[...] * d_ref[...]
        du_ref[...] = du.astype(bf16)
        ub = u_ref[...].astype(bf16)
        dbre_ref[...] += _dot_tn(ub, grv)
        dbim_ref[...] += _dot_tn(ub, giv)

    du, d_bre, d_bim = _rt("s5_in_bwd_" + tag, body_in, rows, ROW_TILE,
                           [(gr, S5_LANES, 0), (gi, S5_LANES, 0), (dy, BRANCH, 0), (proj, BRANCH, U_COL)],
                           [b_re, b_im, cst["s5_d"]], [(BRANCH, bf16)],
                           acc_outs=[(BRANCH, S5_LANES), (BRANCH, S5_LANES)])
    dcst = {"b_re": d_bre, "b_im": d_bim, "a_re": d_ar, "a_im": d_ai, "c_re": d_cre, "c_im": d_cim,
            "s5_d": d_d, "glu_b": d_gb}
    return du, dcst, d_gw


def _hg_prep(q, z, lb):
    qs = _sigmoid(q)
    qh = q * qs
    sg = _sigmoid(z)
    fg = lb + (1.0 - lb) * sg
    kk = (1.0 - lb) * (1.0 - sg)
    return qs, qh, sg, fg, kk


def _hg_fwd(tag, proj, cst, rows):
    tm = min(ROW_TILE, rows)
    c_sz = HG_CHUNK
    nch = tm // c_sz
    n_chunks = rows // c_sz

    def body(i, q_ref, z_ref, v_ref, g_ref, lb_ref, nw_ref, out_ref, o_ref, ss_ref, sn_ref,
             st_s, qh_s, kh_s, vb_s, b_s, k_s):
        @pl.when(i == 0)
        def _():
            st_s[...] = jnp.zeros_like(st_s)

        lb = lb_ref[...]
        _, qh, sg, fg, kk = _hg_prep(q_ref[...], z_ref[...], lb)
        b = _seg_cumsum(jnp.log(fg), tm, c_sz)
        b_s[...] = b
        k_s[...] = kk
        qh_s[...] = (qh * jnp.exp(b)).astype(bf16)
        kh_s[...] = (kk * jnp.exp(-b)).astype(bf16)
        vb_s[...] = v_ref[...].astype(bf16)
        tril = _rows((c_sz, c_sz)) >= lax.broadcasted_iota(jnp.int32, (c_sz, c_sz), 1)

        def chunk(ci, carry):
            sl = pl.ds(pl.multiple_of(ci * c_sz, c_sz), c_sz)
            for h in range(HG_HEADS):
                hl = slice(h * HG_DK, (h + 1) * HG_DK)
                qb = qh_s[sl, hl]
                kb = kh_s[sl, hl]
                vb = vb_s[sl, hl]
                a_mat = jnp.where(tril, _dot_nt(qb, kb), 0.0)
                st = st_s[hl, :]
                stb = st.astype(bf16)
                o_ref[sl, hl] = _dot_nt(qb, stb) + _dot(a_mat.astype(bf16), vb)
                ss_ref[ci, hl, :] = stb
                bb = b_s[sl, hl]
                bl = bb[c_sz - 1:c_sz, :]
                kd = (k_s[sl, hl] * jnp.exp(bl - bb)).astype(bf16)
                st_new = st * jnp.exp(bl) + _dot_tn(vb, kd)
                st_s[hl, :] = st_new
                sn_ref[ci, hl, :] = st_new.astype(bf16)
            return carry

        lax.fori_loop(0, nch, chunk, 0)
        o = o_ref[...]
        r = lax.rsqrt(_head_mean(o * o) + EPS)
        g = g_ref[...]
        out_ref[...] = (o * r * nw_ref[...] * (g * _sigmoid(g))).astype(bf16)

    out, o, ss, sn = _rt(
        "hg_fwd_" + tag, body, rows, tm,
        [(proj, BRANCH, U_COL + 1), (proj, BRANCH, U_COL + 2), (proj, BRANCH, U_COL + 3), (proj, BRANCH, U_COL + 4)],
        [cst["hg_lb"], cst["hg_nw"]],
        [(BRANCH, bf16), (BRANCH, f32),
         ((n_chunks, BRANCH, HG_DK), (nch, BRANCH, HG_DK), lambda t: (t, 0, 0), bf16),
         ((n_chunks, BRANCH, HG_DK), (nch, BRANCH, HG_DK), lambda t: (t, 0, 0), bf16)],
        scratch=[pltpu.VMEM((BRANCH, HG_DK), f32), pltpu.VMEM((tm, BRANCH), bf16), pltpu.VMEM((tm, BRANCH), bf16),
                 pltpu.VMEM((tm, BRANCH), bf16), pltpu.VMEM((tm, BRANCH), f32), pltpu.VMEM((tm, BRANCH), f32)])
    return out, (o, ss, sn)


def _hg_bwd(tag, saved, proj, cst, d_out, rows):
    o_saved, ss, sn = saved
    tm = min(ROW_TILE, rows)
    c_sz = HG_CHUNK
    nch = tm // c_sz

    def body(i, do_ref, q_ref, z_ref, v_ref, g_ref, o_ref, ss_ref, sn_ref, lb_ref, nw_ref,
             dq_ref, dz_ref, dv_ref, dg_ref, dlb_ref, dnw_ref,
             dst_s, flux_s, qh_s, kh_s, vb_s, b_s, k_s, dob_s, dqh_s, dk_s, db_s):
        @pl.when(i == 0)
        def _():
            dst_s[...] = jnp.zeros_like(dst_s)

        lb = lb_ref[...]
        q = q_ref[...]
        qs, qh, sg, fg, kk = _hg_prep(q, z_ref[...], lb)
        b = _seg_cumsum(jnp.log(fg), tm, c_sz)
        b_s[...] = b
        k_s[...] = kk
        qh_s[...] = (qh * jnp.exp(b)).astype(bf16)
        kh_s[...] = (kk * jnp.exp(-b)).astype(bf16)
        vb_s[...] = v_ref[...].astype(bf16)
        g = g_ref[...]
        gs = _sigmoid(g)
        o = o_ref[...]
        r = lax.rsqrt(_head_mean(o * o) + EPS)
        oh = o * r
        nw = nw_ref[...]
        dov = do_ref[...]
        don = dov * (g * gs)
        dg_ref[...] = (dov * oh * nw * (gs * (1.0 + g * (1.0 - gs)))).astype(bf16)
        dnw_ref[...] += jnp.sum(don * oh, axis=0, keepdims=True)
        doh = don * nw
        d_o = r * (doh - oh * _head_mean(doh * oh))
        dob_s[...] = d_o.astype(bf16)
        tril = _rows((c_sz, c_sz)) >= lax.broadcasted_iota(jnp.int32, (c_sz, c_sz), 1)

        def chunk(cj, carry):
            ci = nch - 1 - cj
            sl = pl.ds(pl.multiple_of(ci * c_sz, c_sz), c_sz)
            for h in range(HG_HEADS):
                hl = slice(h * HG_DK, (h + 1) * HG_DK)
                qb = qh_s[sl, hl]
                kb = kh_s[sl, hl]
                vb = vb_s[sl, hl]
                dob = dob_s[sl, hl]
                stb = ss_ref[ci, hl, :]
                dst = dst_s[hl, :]
                dstb = dst.astype(bf16)
                a_mat = jnp.where(tril, _dot_nt(qb, kb), 0.0).astype(bf16)
                da_mat = jnp.where(tril, _dot_nt(dob, vb), 0.0).astype(bf16)
                bb = b_s[sl, hl]
                bl = bb[c_sz - 1:c_sz, :]
                ebl = jnp.exp(bl - bb)
                dqhat = _dot(dob, stb) + _dot(da_mat, kb)
                dkhat = _dot_tn(da_mat, qb)
                dk_inter = _dot(vb, dstb) * ebl
                kv = k_s[sl, hl]
                dqh_s[sl, hl] = dqhat * jnp.exp(bb)
                dk_s[sl, hl] = dkhat * jnp.exp(-bb) + dk_inter
                db_s[sl, hl] = qb.astype(f32) * dqhat - kb.astype(f32) * dkhat - kv * dk_inter
                flux = jnp.sum(sn_ref[ci, hl, :].astype(f32) * dst, axis=0, keepdims=True)
                flux_s[sl, hl] = jnp.broadcast_to(flux, (c_sz, HG_DK))
                kd = (kv * ebl).astype(bf16)
                dv_ref[sl, hl] = (_dot_tn(a_mat, dob) + _dot_nt(kd, dstb)).astype(bf16)
                dst_s[hl, :] = dst * jnp.exp(bl) + _dot_tn(dob, qb)
            return carry

        lax.fori_loop(0, nch, chunk, 0)
        dqh = dqh_s[...]
        dk = dk_s[...]
        dlf = _seg_rev_cumsum(db_s[...], tm, c_sz) + flux_s[...]
        tt = (1.0 - lb) * sg * (1.0 - sg)
        dz_ref[...] = (dlf * tt / fg - dk * tt).astype(bf16)
        dlb_ref[...] += jnp.sum(dlf * (1.0 - sg) / fg - dk * (1.0 - sg), axis=0, keepdims=True)
        dq_ref[...] = (dqh * (qs * (1.0 + q * (1.0 - qs)))).astype(bf16)

    dq, dz, dv, dg, d_lb, d_nw = _rt(
        "hg_bwd_" + tag, body, rows, tm,
        [(d_out, BRANCH, 0), (proj, BRANCH, U_COL + 1), (proj, BRANCH, U_COL + 2), (proj, BRANCH, U_COL + 3),
         (proj, BRANCH, U_COL + 4), (o_saved, BRANCH, 0), (ss, (nch, BRANCH, HG_DK), lambda t: (t, 0, 0)),
         (sn, (nch, BRANCH, HG_DK), lambda t: (t, 0, 0))],
        [cst["hg_lb"], cst["hg_nw"]],
        [(BRANCH, bf16)] * 4, acc_outs=[(1, BRANCH), (1, BRANCH)],
        scratch=[pltpu.VMEM((BRANCH, HG_DK), f32), pltpu.VMEM((tm, BRANCH), f32),
                 pltpu.VMEM((tm, BRANCH), bf16), pltpu.VMEM((tm, BRANCH), bf16), pltpu.VMEM((tm, BRANCH), bf16),
                 pltpu.VMEM((tm, BRANCH), f32), pltpu.VMEM((tm, BRANCH), f32), pltpu.VMEM((tm, BRANCH), bf16),
                 pltpu.VMEM((tm, BRANCH), f32), pltpu.VMEM((tm, BRANCH), f32), pltpu.VMEM((tm, BRANCH), f32)],
        reverse=True)
    return dq, dz, dv, dg, {"hg_lb": d_lb, "hg_nw": d_nw}


def _rg_gates(xc, wa_ref, ba_ref, wx_ref, bx_ref, sp8):
    xcb = xc.astype(bf16)
    r = _sigmoid(_dot(xcb, wa_ref[...]) + ba_ref[...])
    ig = _sigmoid(_dot(xcb, wx_ref[...]) + bx_ref[...])
    la = -sp8 * r
    a = jnp.exp(la)
    mult = jnp.sqrt(-_expm1(2.0 * la))
    return xcb, r, ig, a, mult


def _rg_fwd(tag, proj, cst, rows):
    tm = min(ROW_TILE, rows)

    def body(i, xb_ref, gate_ref, cw_ref, cb_ref, wa_ref, ba_ref, wx_ref, bx_ref, sp_ref,
             out_ref, xc_ref, h_ref, hp_ref, prev_s, hc_s):
        @pl.when(i == 0)
        def _():
            prev_s[...] = jnp.zeros_like(prev_s)
            hc_s[...] = jnp.zeros_like(hc_s)

        row = _rows((tm, BRANCH))
        xb = xb_ref[...]
        prev = prev_s[...]
        xc = cb_ref[...] + cw_ref[3:4, :] * xb
        for j in range(1, 4):
            sh = jnp.where(row >= j, pltpu.roll(xb, j, 0), pltpu.roll(prev, j, 0))
            xc = xc + cw_ref[3 - j:4 - j, :] * sh
        prev_s[...] = xb
        xc_ref[...] = xc
        _, r, ig, a, mult = _rg_gates(xc, wa_ref, ba_ref, wx_ref, bx_ref, sp_ref[...])
        a_cum, h_loc = _scan_fwd(a, mult * ig * xc, tm)
        hc = hc_s[...]
        h = h_loc + a_cum * hc
        h_ref[...] = h
        hp_ref[...] = jnp.where(row >= 1, pltpu.roll(h, 1, 0), hc)
        hc_s[...] = h[tm - 1:tm, :]
        out_ref[...] = (h * _gelu(gate_ref[...])).astype(bf16)

    out, xc, h, hp = _rt(
        "rg_fwd_" + tag, body, rows, tm,
        [(proj, BRANCH, U_COL + 5), (proj, BRANCH, U_COL + 6)],
        [cst["rg_cw"], cst["rg_cb"], cst["rg_wa"].astype(bf16), cst["rg_ba"], cst["rg_wx"].astype(bf16),
         cst["rg_bx"], cst["rg_sp8"]],
        [(BRANCH, bf16), (BRANCH, f32), (BRANCH, f32), (BRANCH, f32)],
        scratch=[pltpu.VMEM((tm, BRANCH), f32), pltpu.VMEM((1, BRANCH), f32)])
    return out, (xc, h, hp)


def _rg_bwd(tag, saved, proj, cst, d_out, rows):
    xc_saved, h_saved, hp_saved = saved
    tm = min(ROW_TILE, rows)

    def body(i, do_ref, xb_ref, gate_ref, xc_ref, h_ref, hp_ref, cw_ref, wa_ref, ba_ref, wx_ref, bx_ref, sp_ref,
             dxb_ref, dgate_ref, dcw_ref, dcb_ref, dwa_ref, dba_ref, dwx_ref, dbx_ref, dsp_ref,
             nxt_s, ec_s):
        @pl.when(i == 0)
        def _():
            nxt_s[...] = jnp.zeros_like(nxt_s)
            ec_s[...] = jnp.zeros_like(ec_s)

        row = _rows((tm, BRANCH))
        xc = xc_ref[...]
        sp8 = sp_ref[...]
        xcb, r, ig, a, mult = _rg_gates(xc, wa_ref, ba_ref, wx_ref, bx_ref, sp8)
        gate = gate_ref[...]
        dov = do_ref[...]
        dh = dov * _gelu(gate)
        dgate_ref[...] = (dov * h_ref[...] * _gelu_grad(gate)).astype(bf16)
        a_cum, e_loc = _scan_bwd(a, a * dh, tm)
        ec = ec_s[...]
        e = e_loc + a_cum * ec
        g_tot = dh + jnp.where(row == tm - 1, ec, pltpu.roll(e, tm - 1, 0))
        ec_s[...] = e[0:1, :]
        d_a = g_tot * hp_ref[...]
        d_mult = g_tot * ig * xc
        d_ix = g_tot * mult
        d_ig = d_ix * xc
        d_xc = d_ix * ig
        d_la = d_a * a - d_mult * (a * a) / mult
        d_r = -d_la * sp8
        dsp_ref[...] += jnp.sum(-d_la * r, axis=0, keepdims=True)
        dzr = d_r * r * (1.0 - r)
        dzi = d_ig * ig * (1.0 - ig)
        dzrb = dzr.astype(bf16)
        dzib = dzi.astype(bf16)
        d_xc = d_xc + _dot_nt(dzrb, wa_ref[...]) + _dot_nt(dzib, wx_ref[...])
        dwa_ref[...] += _dot_tn(xcb, dzrb)
        dwx_ref[...] += _dot_tn(xcb, dzib)
        dba_ref[...] += jnp.sum(dzr, axis=0, keepdims=True)
        dbx_ref[...] += jnp.sum(dzi, axis=0, keepdims=True)
        dcb_ref[...] += jnp.sum(d_xc, axis=0, keepdims=True)
        nxt = nxt_s[...]
        xb = xb_ref[...]
        dxb = cw_ref[3:4, :] * d_xc
        dcw_ref[3:4, :] += jnp.sum(d_xc * xb, axis=0, keepdims=True)
        for j in range(1, 4):
            sh = jnp.where(row < tm - j, pltpu.roll(d_xc, tm - j, 0), pltpu.roll(nxt, tm - j, 0))
            dxb = dxb + cw_ref[3 - j:4 - j, :] * sh
            dcw_ref[3 - j:4 - j, :] += jnp.sum(sh * xb, axis=0, keepdims=True)
        nxt_s[...] = d_xc
        dxb_ref[...] = dxb.astype(bf16)

    wa = cst["rg_wa"].astype(bf16)
    wx = cst["rg_wx"].astype(bf16)
    dxb, dgate, d_cw, d_cb, d_wa, d_ba, d_wx, d_bx, d_sp = _rt(
        "rg_bwd_" + tag, body, rows, tm,
        [(d_out, BRANCH, 0), (proj, BRANCH, U_COL + 5), (proj, BRANCH, U_COL + 6), (xc_saved, BRANCH, 0),
         (h_saved, BRANCH, 0), (hp_saved, BRANCH, 0)],
        [cst["rg_cw"], wa, cst["rg_ba"], wx, cst["rg_bx"], cst["rg_sp8"]],
        [(BRANCH, bf16), (BRANCH, bf16)],
        acc_outs=[(4, BRANCH), (1, BRANCH), (BRANCH, BRANCH), (1, BRANCH), (BRANCH, BRANCH), (1, BRANCH), (1, BRANCH)],
        scratch=[pltpu.VMEM((tm, BRANCH), f32), pltpu.VMEM((1, BRANCH), f32)],
        reverse=True)
    dcst = {"rg_cw": d_cw, "rg_cb": d_cb, "rg_wa": d_wa, "rg_ba": d_ba, "rg_wx": d_wx, "rg_bx": d_bx, "rg_sp8": d_sp}
    return dxb, dgate, dcst


def _merge_fwd(tag, proj, outs, bp, rows):
    def body(i, ya_ref, yb_ref, yc_ref, gm_ref, p_ref, m_ref):
        acc = None
        for n, y_ref in enumerate((ya_ref, yb_ref, yc_ref)):
            up = _dot(y_ref[...], p_ref[n])
            term = _sigmoid(gm_ref[:, n * D_MODEL:(n + 1) * D_MODEL]) * up
            acc = term if acc is None else acc + term
        m_ref[...] = acc.astype(bf16)
    return _rt("merge_fwd_" + tag, body, rows, ROW_TILE,
               [(outs[0], BRANCH, 0), (outs[1], BRANCH, 0), (outs[2], BRANCH, 0), (proj, GM_WIDTH, 0)],
               [bp], [(D_MODEL, bf16)])[0]


def _merge_bwd(tag, proj, outs, bp, dmerged, rows):
    def body(i, dm_ref, ya_ref, yb_ref, yc_ref, gm_ref, p_ref, da_ref, db_ref, dc_ref, dgm_ref, dp_ref):
        dm = dm_ref[...]
        for n, (y_ref, dy_ref) in enumerate(((ya_ref, da_ref), (yb_ref, db_ref), (yc_ref, dc_ref))):
            yv = y_ref[...]
            up = _dot(yv, p_ref[n])
            gt = _sigmoid(gm_ref[:, n * D_MODEL:(n + 1) * D_MODEL])
            dup = (dm * gt).astype(bf16)
            dgm_ref[:, n * D_MODEL:(n + 1) * D_MODEL] = (dm * up * gt * (1.0 - gt)).astype(bf16)
            dy_ref[...] = _dot_nt(dup, p_ref[n])
            dp_ref[n] += _dot_tn(yv, dup)
    return _rt("merge_bwd_" + tag, body, rows, ROW_TILE,
               [(dmerged, D_MODEL, 0), (outs[0], BRANCH, 0), (outs[1], BRANCH, 0), (outs[2], BRANCH, 0),
                (proj, GM_WIDTH, 0)],
               [bp], [(BRANCH, f32), (BRANCH, f32), (BRANCH, f32), (GM_WIDTH, bf16)],
               acc_outs=[(N_BRANCH, BRANCH, D_MODEL)])


def _block_diag(blocks):
    g, r, c = blocks.shape
    on_diag = (lax.broadcasted_iota(jnp.int32, (g * r, g * c), 0) // r
               == lax.broadcasted_iota(jnp.int32, (g * r, g * c), 1) // c)
    tiled = jnp.broadcast_to(blocks.reshape(g * r, 1, c), (g * r, g, c)).reshape(g * r, g * c)
    return jnp.where(on_diag, tiled, 0.0)


def _prep_consts(sp):
    p = jax.nn.softmax(sp["hg_lb_logits"], axis=0)
    lower = jnp.cumsum(p, axis=0) - p[0]
    out = []
    for l in range(DEPTH):
        lr = jnp.minimum(sp["s5_lambda_re"][l], S5_EIG_MAX)
        li = sp["s5_lambda_im"][l]
        dt = jnp.exp(sp["s5_log_dt"][l])[:, None]
        mag = jnp.exp(lr * dt)
        ar = mag * jnp.cos(li * dt)
        ai = mag * jnp.sin(li * dt)
        den = lr * lr + li * li
        fr = ((ar - 1.0) * lr + ai * li) / den
        fi = (ai * lr - (ar - 1.0) * li) / den
        br, bi = sp["s5_b_re"][l], sp["s5_b_im"][l]
        bbr = fr[..., None] * br - fi[..., None] * bi
        bbi = fr[..., None] * bi + fi[..., None] * br
        c = {
            "a_re": ar.reshape(1, S5_LANES), "a_im": ai.reshape(1, S5_LANES),
            "b_re": _block_diag(bbr.transpose(0, 2, 1)), "b_im": _block_diag(bbi.transpose(0, 2, 1)),
            "c_re": _block_diag(sp["s5_c_re"][l].transpose(0, 2, 1)),
            "c_im": -_block_diag(sp["s5_c_im"][l].transpose(0, 2, 1)),
            "s5_d": sp["s5_d"][l][None], "glu_b": sp["s5_glu_b"][l][None],
            "hg_lb": lower[l][None], "hg_nw": sp["hg_norm_w"][l][None],
            "rg_cw": sp["rg_conv_w"][l], "rg_cb": sp["rg_conv_b"][l][None],
            "rg_wa": _block_diag(sp["rg_wa"][l]), "rg_ba": sp["rg_ba"][l][None],
            "rg_wx": _block_diag(sp["rg_wx"][l]), "rg_bx": sp["rg_bx"][l][None],
            "rg_sp8": (RG_C * jax.nn.softplus(-sp["rg_lambda"][l]))[None],
        }
        out.append(c)
    return out


def _mixer_fwd(tag, x, nw, w_in, bp, w_out, cst, rows):
    hb = _rms_fwd("mix_norm_" + tag, x, nw, rows)
    proj = _mm1("mix_proj_" + tag, hb, w_in, "nn", rows, IN_TOTAL, D_MODEL, 512, 512, D_MODEL)
    cst = dict(cst)
    out_a, sv_a = _s5_fwd(tag, proj, cst, rows)
    out_b, sv_b = _hg_fwd(tag, proj, cst, rows)
    out_c, sv_c = _rg_fwd(tag, proj, cst, rows)
    merged = _merge_fwd(tag, proj, (out_a, out_b, out_c), bp, rows)
    x_out = _mm("mix_out_" + tag, [merged], [w_out], [(0, 0, 0)], 1, "nn", rows, D_MODEL, D_MODEL,
                512, D_MODEL, D_MODEL, [f32], lambda accs, extras: [extras[0] + accs[0]], extras=[x])[0]
    return x_out, (x, hb, proj, (out_a, out_b, out_c), merged, sv_a, sv_b, sv_c)


def _mixer_bwd(tag, saved, nw, w_in, bp, w_out, cst, dx, dxb, rows):
    x, hb, proj, outs, merged, sv_a, sv_b, sv_c = saved
    d_wout = _mm1("mix_dwout_" + tag, merged, dxb, "tn", D_MODEL, D_MODEL, rows, D_MODEL, D_MODEL, TOKEN_K,
                  out_dtype=bf16)
    dmerged = _mm1("mix_dmerged_" + tag, dxb, w_out, "nt", rows, D_MODEL, D_MODEL, 512, D_MODEL, D_MODEL)
    d_a, d_b, d_c, dgm, d_bp = _merge_bwd(tag, proj, outs, bp, dmerged, rows)
    dxbc, dgatec, dcst_c = _rg_bwd(tag, sv_c, proj, cst, d_c, rows)
    dq, dz, dv, dg, dcst_b = _hg_bwd(tag, sv_b, proj, cst, d_b, rows)
    du, dcst_a, d_glu_w = _s5_bwd(tag, sv_a, proj, cst, d_a, rows)
    dproj = jnp.concatenate([dgm, du, dq, dz, dv, dg, dxbc, dgatec], axis=1)
    d_win = _mm1("mix_dwin_" + tag, hb, dproj, "tn", D_MODEL, IN_TOTAL, rows, D_MODEL, IN_TOTAL // 4, TOKEN_K,
                 out_dtype=bf16)
    dh = _mm1("mix_dh_" + tag, dproj, w_in, "nt", rows, D_MODEL, IN_TOTAL, 512, D_MODEL, IN_TOTAL // 2)
    dx_in, dxb_in, d_nw = _rms_bwd("mix_norm_bwd_" + tag, x, dh, nw, dx, rows)
    dcst = {**dcst_a, **dcst_b, **dcst_c}
    return dx_in, dxb_in, d_nw, d_win, d_bp, d_wout, d_glu_w, dcst


def _local_step(x, target, big, small):
    rows = x.shape[0]
    consts, consts_vjp = jax.vjp(_prep_consts, small)
    norm_w = small["norm_w"]
    saved = []
    h = x
    for l in range(DEPTH):
        t = str(l)
        cst = dict(consts[l])
        cst["glu_w"] = big["glu_w"][l]
        h, sv0 = _ffn_fwd(t + "a", h, norm_w[l, 0][None], big["gate"][l, 0], big["up"][l, 0], big["down"][l, 0], rows)
        h, sv1 = _mixer_fwd(t, h, norm_w[l, 1][None], big["w_in"][l], big["bp"][l], big["w_out"][l], cst, rows)
        h, sv2 = _ffn_fwd(t + "b", h, norm_w[l, 2][None], big["gate"][l, 1], big["up"][l, 1], big["down"][l, 1], rows)
        saved.append((sv0, sv1, sv2, cst))
    dx, dxb, loss, d_fnw = _loss_head(h, small["final_norm_w"][None], target, rows)
    g_big = {k: [None] * DEPTH for k in ("gate", "up", "down", "w_in", "bp", "w_out", "glu_w")}
    d_norm = [None] * DEPTH
    d_consts = [None] * DEPTH
    for l in reversed(range(DEPTH)):
        t = str(l)
        sv0, sv1, sv2, cst = saved[l]
        dx, dxb, dn2, dg1, du1, dd1 = _ffn_bwd(t + "b", sv2, norm_w[l, 2][None], big["gate"][l, 1], big["up"][l, 1],
                                               big["down"][l, 1], dx, dxb, rows)
        dx, dxb, dn1, d_win, d_bp, d_wout, d_glu_w, dcst = _mixer_bwd(
            t, sv1, norm_w[l, 1][None], big["w_in"][l], big["bp"][l], big["w_out"][l], cst, dx, dxb, rows)
        dx, dxb, dn0, dg0, du0, dd0 = _ffn_bwd(t + "a", sv0, norm_w[l, 0][None], big["gate"][l, 0], big["up"][l, 0],
                                               big["down"][l, 0], dx, dxb, rows)
        g_big["gate"][l] = jnp.stack([dg0, dg1])
        g_big["up"][l] = jnp.stack([du0, du1])
        g_big["down"][l] = jnp.stack([dd0, dd1])
        g_big["w_in"][l] = d_win
        g_big["bp"][l] = d_bp
        g_big["w_out"][l] = d_wout
        g_big["glu_w"][l] = d_glu_w
        d_norm[l] = jnp.concatenate([dn0, dn1, dn2], axis=0)
        d_consts[l] = dcst
    g_big = {k: jnp.stack(v) for k, v in g_big.items()}
    (g_small,) = consts_vjp(d_consts)
    g_small = dict(g_small)
    g_small["norm_w"] = g_small["norm_w"] + jnp.stack(d_norm)
    g_small["final_norm_w"] = g_small["final_norm_w"] + d_fnw[0]
    return loss[0, 0], dx, g_big, g_small


MESH_IDS = pl.DeviceIdType.MESH
ANY_SPEC = pl.BlockSpec(memory_space=pl.ANY)


def _place():
    return lax.axis_index("x"), lax.axis_index("y"), lax.axis_index("c")


def _all_gather(name, shards):
    n = len(shards)

    def body(*refs):
        x_refs, out_refs = refs[:n], refs[n:2 * n]
        send_sems, recv_sems, local_sems = refs[2 * n:]
        x, y, c = _place()
        me, sibling = (x, y, c), (x, y, 1 - c)
        chips = [(1 - x, y), (x, 1 - y), (1 - x, 1 - y)]

        def blk(i, px, py, pc):
            return out_refs[i].at[4 * px + 2 * py + pc]

        def copy(i, k, block, to, src=None):
            return pltpu.make_async_remote_copy(
                src_ref=blk(i, *block) if src is None else src, dst_ref=blk(i, *block),
                send_sem=send_sems.at[7 * i + k], recv_sem=recv_sems.at[7 * i + k], device_id=to,
                device_id_type=MESH_IDS)

        mine = [pltpu.make_async_copy(x_refs[i], blk(i, *me), local_sems.at[i]) for i in range(n)]
        for cp in mine:
            cp.start()
        first = []
        for i in range(n):
            first.append(copy(i, 0, me, sibling, src=x_refs[i]))
            first += [copy(i, 1 + j, me, (*chip, c), src=x_refs[i]) for j, chip in enumerate(chips)]
        for cp in first:
            cp.start()
        passed = []
        for j, chip in enumerate(chips):
            for i in range(n):
                copy(i, 1 + j, (*chip, c), me).wait_recv()
                fwd = copy(i, 4 + j, (*chip, c), sibling)
                fwd.start()
                passed.append(fwd)
        for i in range(n):
            copy(i, 0, sibling, me).wait_recv()
            for j, chip in enumerate(chips):
                copy(i, 4 + j, (*chip, 1 - c), me).wait_recv()
        for cp in first + passed:
            cp.wait_send()
        for cp in mine:
            cp.wait()

    return pl.pallas_call(
        body, name=name, out_shape=[jax.ShapeDtypeStruct((N_DEV,) + s.shape, s.dtype) for s in shards],
        in_specs=[ANY_SPEC] * n, out_specs=[ANY_SPEC] * n,
        scratch_shapes=[pltpu.SemaphoreType.DMA((7 * n,)), pltpu.SemaphoreType.DMA((7 * n,)),
                        pltpu.SemaphoreType.DMA((n,))],
    )(*shards)


def _row_tile(rows):
    return rows if rows <= 512 else next(t for t in range(512, 7, -8) if rows % t == 0)


def _reduce_scatter(parts):
    n = len(parts)
    _, _, c = _place()

    def body_pair(*refs):
        p_refs, got_refs = refs[:n], refs[n:2 * n]
        send_sems, recv_sems = refs[2 * n:]
        x, y, c = _place()
        cps = [pltpu.make_async_remote_copy(
            src_ref=p_refs[i].at[1 - c], dst_ref=got_refs[i], send_sem=send_sems.at[i], recv_sem=recv_sems.at[i],
            device_id=(x, y, 1 - c), device_id_type=MESH_IDS) for i in range(n)]
        for cp in cps:
            cp.start()
        for cp in cps:
            cp.wait()

    from_sibling = pl.pallas_call(
        body_pair, name="rs_pair", out_shape=[jax.ShapeDtypeStruct(p.shape[1:], p.dtype) for p in parts],
        in_specs=[ANY_SPEC] * n, out_specs=[ANY_SPEC] * n,
        scratch_shapes=[pltpu.SemaphoreType.DMA((n,)), pltpu.SemaphoreType.DMA((n,))],
    )(*parts)

    chip_sums = []
    for i, (part, got) in enumerate(zip(parts, from_sibling)):
        _, _, r, cols = part.shape
        tr = _row_tile(r)

        def body_add(idx_ref, p_ref, g_ref, o_ref):
            o_ref[...] = (p_ref[...].astype(f32) + g_ref[...].astype(f32)).astype(o_ref.dtype)

        chip_sums.append(pl.pallas_call(
            body_add, name="rs_pair_sum_%d" % i, out_shape=jax.ShapeDtypeStruct((4, r, cols), part.dtype),
            grid_spec=pltpu.PrefetchScalarGridSpec(
                num_scalar_prefetch=1, grid=(4, r // tr),
                in_specs=[pl.BlockSpec((None, None, tr, cols), lambda j, t, idx: (idx[0], j, t, 0)),
                          pl.BlockSpec((None, tr, cols), lambda j, t, idx: (j, t, 0))],
                out_specs=pl.BlockSpec((None, tr, cols), lambda j, t, idx: (j, t, 0))),
            compiler_params=_cparams(("parallel", "parallel")),
        )(jnp.stack([c]).astype(jnp.int32), part, got))

    def body_chips(*refs):
        t_refs, got_refs = refs[:n], refs[n:2 * n]
        send_sems, recv_sems = refs[2 * n:]
        x, y, c = _place()
        chips = [(1 - x, y), (x, 1 - y), (1 - x, 1 - y)]
        cps = [pltpu.make_async_remote_copy(
            src_ref=t_refs[i].at[2 * px + py], dst_ref=got_refs[i].at[k], send_sem=send_sems.at[3 * i + k],
            recv_sem=recv_sems.at[3 * i + k], device_id=(px, py, c), device_id_type=MESH_IDS)
            for i in range(n) for k, (px, py) in enumerate(chips)]
        for cp in cps:
            cp.start()
        for cp in cps:
            cp.wait()

    from_chips = pl.pallas_call(
        body_chips, name="rs_chips", out_shape=[jax.ShapeDtypeStruct((3,) + p.shape[2:], p.dtype) for p in parts],
        in_specs=[ANY_SPEC] * n, out_specs=[ANY_SPEC] * n,
        scratch_shapes=[pltpu.SemaphoreType.DMA((3 * n,)), pltpu.SemaphoreType.DMA((3 * n,))],
    )(*chip_sums)
    return list(zip(chip_sums, from_chips))


def _own_index():
    x, y, _ = _place()
    return jnp.stack([2 * x + y]).astype(jnp.int32)


def _own_total(name, chip_sum, others):
    _, r, cols = chip_sum.shape
    tr = _row_tile(r)

    def body(idx_ref, t_ref, g_ref, o_ref):
        o_ref[...] = ((t_ref[...].astype(f32) + g_ref[0].astype(f32)) + g_ref[1].astype(f32)) + g_ref[2].astype(f32)

    return pl.pallas_call(
        body, name=name, out_shape=jax.ShapeDtypeStruct((r, cols), f32),
        grid_spec=pltpu.PrefetchScalarGridSpec(
            num_scalar_prefetch=1, grid=(r // tr,),
            in_specs=[pl.BlockSpec((None, tr, cols), lambda t, idx: (idx[0], t, 0)),
                      pl.BlockSpec((3, tr, cols), lambda t, idx: (0, t, 0))],
            out_specs=pl.BlockSpec((tr, cols), lambda t, idx: (t, 0))),
        compiler_params=_cparams(("parallel",)),
    )(_own_index(), chip_sum, others)


def _adam_update(w, gv, m, v):
    m_new = ADAM_B1 * m + (1.0 - ADAM_B1) * gv
    v_new = ADAM_B2 * v + (1.0 - ADAM_B2) * (gv * gv)
    m_hat = m_new / (1.0 - ADAM_B1 ** ADAM_STEP)
    v_hat = v_new / (1.0 - ADAM_B2 ** ADAM_STEP)
    return -ADAM_LR * (m_hat / (jnp.sqrt(v_hat) + ADAM_EPS) + ADAM_WD * w), m_new, v_new


def _adamw_reduced(name, w, chip_sum, others, m, v):
    rows, cols = w.shape
    tr = _row_tile(rows)

    def body(idx_ref, w_ref, t_ref, o_ref, m_ref, v_ref, g_ref, d_ref, nm_ref, nv_ref):
        gv = ((t_ref[...].astype(f32) + o_ref[0].astype(f32)) + o_ref[1].astype(f32)) + o_ref[2].astype(f32)
        g_ref[...] = gv
        d_ref[...], nm_ref[...], nv_ref[...] = _adam_update(w_ref[...], gv, m_ref[...], v_ref[...])

    spec = pl.BlockSpec((tr, cols), lambda t, idx: (t, 0))
    return pl.pallas_call(
        body, name=name, out_shape=[jax.ShapeDtypeStruct((rows, cols), f32)] * 4,
        grid_spec=pltpu.PrefetchScalarGridSpec(
            num_scalar_prefetch=1, grid=(rows // tr,),
            in_specs=[spec, pl.BlockSpec((None, tr, cols), lambda t, idx: (idx[0], t, 0)),
                      pl.BlockSpec((3, tr, cols), lambda t, idx: (0, t, 0)), spec, spec],
            out_specs=[spec] * 4),
        compiler_params=_cparams(("parallel",)),
    )(_own_index(), w, chip_sum, others, m, v)


def _adamw(name, w, g, m, v):
    rows, cols = w.shape
    tr = _row_tile(rows)

    def body(w_ref, g_ref, m_ref, v_ref, d_ref, nm_ref, nv_ref):
        d_ref[...], nm_ref[...], nv_ref[...] = _adam_update(w_ref[...], g_ref[...], m_ref[...], v_ref[...])

    spec = pl.BlockSpec((tr, cols), lambda i: (i, 0))
    return pl.pallas_call(
        body, name=name, grid=(rows // tr,), in_specs=[spec] * 4, out_specs=[spec] * 3,
        out_shape=[jax.ShapeDtypeStruct((rows, cols), f32)] * 3, compiler_params=_cparams(("parallel",)),
    )(w, g, m, v)


WEIGHT_NAMES = ["norm_w", "final_norm_w", "ffn_gate", "ffn_up", "ffn_down", "w_in", "branch_proj", "w_out",
                "s5_lambda_re", "s5_lambda_im", "s5_log_dt", "s5_b_re", "s5_b_im", "s5_c_re", "s5_c_im", "s5_d",
                "s5_glu_w", "s5_glu_b", "hg_lb_logits", "hg_norm_w", "rg_conv_w", "rg_conv_b", "rg_wa", "rg_ba",
                "rg_wx", "rg_bx", "rg_lambda"]
SHARDED = {"ffn_gate": (3, "gate"), "ffn_up": (3, "up"), "ffn_down": (2, "down"), "w_in": (2, "w_in"),
           "branch_proj": (3, "bp"), "w_out": (1, "w_out"), "s5_glu_w": (1, "glu_w"),
           "norm_w": (2, None), "rg_conv_w": (2, None)}
BIG = ["ffn_gate", "ffn_up", "ffn_down", "w_in", "branch_proj", "w_out", "s5_glu_w"]
SMALL_SHARDED = ["norm_w", "rg_conv_w"]
REPLICATED = [n for n in WEIGHT_NAMES if n not in SHARDED]
LANES = 128


PACK_ROWS = 512


def _pack_rows(arrays, names):
    pieces = []
    for n in names:
        flat = arrays[n].reshape(-1)
        pieces.append(jnp.pad(flat, (0, -flat.shape[0] % LANES)).reshape(-1, LANES))
    rows = jnp.concatenate(pieces, axis=0)
    return jnp.pad(rows, ((0, -rows.shape[0] % PACK_ROWS), (0, 0)))


def _unpack_rows(rows, names, like):
    out, r0 = {}, 0
    for n in names:
        size = math.prod(like[n].shape)
        nrows = -(-size // LANES)
        out[n] = rows[r0:r0 + nrows].reshape(-1)[:size].reshape(like[n].shape)
        r0 += nrows
    return out


def _unshard(gathered, axis):
    g = jnp.moveaxis(gathered, 0, axis)
    shp = g.shape
    return g.reshape(shp[:axis] + (shp[axis] * shp[axis + 1],) + shp[axis + 2:])


def _to_blocks(full, axis):
    shp = full.shape
    g = full.reshape(shp[:axis] + (4, 2, shp[axis] // N_DEV) + shp[axis + 1:])
    g = jnp.moveaxis(g, (axis, axis + 1), (1, 0))
    return g.reshape(2, 4, -1, g.shape[-1])


W_IN_SPLIT = IN_TOTAL - GM_WIDTH


def kernel(x, norm_w, final_norm_w, ffn_gate, ffn_up, ffn_down, w_in, branch_proj, w_out, s5_lambda_re, s5_lambda_im, s5_log_dt, s5_b_re, s5_b_im, s5_c_re, s5_c_im, s5_d, s5_glu_w, s5_glu_b, hg_lb_logits, hg_norm_w, rg_conv_w, rg_conv_b, rg_wa, rg_ba, rg_wx, rg_bx, rg_lambda, loss_target, m_norm_w, m_final_norm_w, m_ffn_gate, m_ffn_up, m_ffn_down, m_w_in, m_branch_proj, m_w_out, m_s5_lambda_re, m_s5_lambda_im, m_s5_log_dt, m_s5_b_re, m_s5_b_im, m_s5_c_re, m_s5_c_im, m_s5_d, m_s5_glu_w, m_s5_glu_b, m_hg_lb_logits, m_hg_norm_w, m_rg_conv_w, m_rg_conv_b, m_rg_wa, m_rg_ba, m_rg_wx, m_rg_bx, m_rg_lambda, v_norm_w, v_final_norm_w, v_ffn_gate, v_ffn_up, v_ffn_down, v_w_in, v_branch_proj, v_w_out, v_s5_lambda_re, v_s5_lambda_im, v_s5_log_dt, v_s5_b_re, v_s5_b_im, v_s5_c_re, v_s5_c_im, v_s5_d, v_s5_glu_w, v_s5_glu_b, v_hg_lb_logits, v_hg_norm_w, v_rg_conv_w, v_rg_conv_b, v_rg_wa, v_rg_ba, v_rg_wx, v_rg_bx, v_rg_lambda):
    w = dict(zip(WEIGHT_NAMES, (norm_w, final_norm_w, ffn_gate, ffn_up, ffn_down, w_in, branch_proj, w_out,
                                s5_lambda_re, s5_lambda_im, s5_log_dt, s5_b_re, s5_b_im, s5_c_re, s5_c_im, s5_d,
                                s5_glu_w, s5_glu_b, hg_lb_logits, hg_norm_w, rg_conv_w, rg_conv_b, rg_wa, rg_ba,
                                rg_wx, rg_bx, rg_lambda)))
    m = dict(zip(WEIGHT_NAMES, (m_norm_w, m_final_norm_w, m_ffn_gate, m_ffn_up, m_ffn_down, m_w_in, m_branch_proj,
                                m_w_out, m_s5_lambda_re, m_s5_lambda_im, m_s5_log_dt, m_s5_b_re, m_s5_b_im, m_s5_c_re,
                                m_s5_c_im, m_s5_d, m_s5_glu_w, m_s5_glu_b, m_hg_lb_logits, m_hg_norm_w, m_rg_conv_w,
                                m_rg_conv_b, m_rg_wa, m_rg_ba, m_rg_wx, m_rg_bx, m_rg_lambda)))
    v = dict(zip(WEIGHT_NAMES, (v_norm_w, v_final_norm_w, v_ffn_gate, v_ffn_up, v_ffn_down, v_w_in, v_branch_proj,
                                v_w_out, v_s5_lambda_re, v_s5_lambda_im, v_s5_log_dt, v_s5_b_re, v_s5_b_im, v_s5_c_re,
                                v_s5_c_im, v_s5_d, v_s5_glu_w, v_s5_glu_b, v_hg_lb_logits, v_hg_norm_w, v_rg_conv_w,
                                v_rg_conv_b, v_rg_wa, v_rg_ba, v_rg_wx, v_rg_bx, v_rg_lambda)))
    rows = x.shape[1]

    sharded = BIG + SMALL_SHARDED
    gathered = _all_gather("gather_weights", [w[n].astype(bf16) for n in BIG] + [w[n] for n in SMALL_SHARDED])
    full = {n: _unshard(g, SHARDED[n][0]) for n, g in zip(sharded, gathered)}
    big = {SHARDED[n][1]: full[n] for n in BIG}
    big["w_in"] = jnp.concatenate([big["w_in"][..., W_IN_SPLIT:], big["w_in"][..., :W_IN_SPLIT]], axis=-1)
    small = {n: w[n] for n in REPLICATED}
    small["norm_w"] = full["norm_w"]
    small["rg_conv_w"] = full["rg_conv_w"]

    loss_part, dx, g_big, g_small = _local_step(x[0], loss_target[0], big, small)
    g_big["w_in"] = jnp.concatenate([g_big["w_in"][..., GM_WIDTH:], g_big["w_in"][..., :GM_WIDTH]], axis=-1)
    loss = lax.psum(loss_part, ("x", "y", "c"))

    parts = [_to_blocks(g_big[SHARDED[n][1]], SHARDED[n][0]).astype(bf16) for n in BIG]
    parts += [_to_blocks(g_small[n], SHARDED[n][0]) for n in SMALL_SHARDED]
    rep_rows = _pack_rows(g_small, REPLICATED)
    rep_slice = rep_rows.shape[0] // N_DEV
    parts.append(rep_rows.reshape(4, 2, rep_slice, LANES).transpose(1, 0, 2, 3))
    sums = _reduce_scatter(parts)

    grads, delta, new_m, new_v = {}, {}, {}, {}
    for n, (chip_sum, others) in zip(sharded, sums):
        shp = w[n].shape
        view = (-1, shp[-1])
        res = _adamw_reduced("adamw_" + n, w[n].reshape(view), chip_sum, others, m[n].reshape(view), v[n].reshape(view))
        grads[n], delta[n], new_m[n], new_v[n] = (r.reshape(shp) for r in res)
    rep_mine = _own_total("rs_total_small", *sums[-1])
    rep_grads = _all_gather("gather_small_grads", [rep_mine])[0].reshape(-1, LANES)
    res = _adamw("adamw_small", _pack_rows(w, REPLICATED), rep_grads, _pack_rows(m, REPLICATED), _pack_rows(v, REPLICATED))
    for dst, src in zip((grads, delta, new_m, new_v), (rep_grads,) + tuple(res)):
        dst.update(_unpack_rows(src, REPLICATED, w))

    return (loss, dx.reshape(x.shape), *[grads[n] for n in WEIGHT_NAMES], *[delta[n] for n in WEIGHT_NAMES],
            *[new_m[n] for n in WEIGHT_NAMES], *[new_v[n] for n in WEIGHT_NAMES])
```

```python
import functools
import math

import jax
import jax.numpy as jnp
from jax import lax
from jax.experimental import pallas as pl
from jax.experimental.pallas import tpu as pltpu

f32 = jnp.float32
bf16 = jnp.bfloat16

D_MODEL = 1024
DEPTH = 2
BRANCH = 512
N_BRANCH = 3
S5_GROUP = 16
S5_GROUPS = 32
S5_STATE = 64
S5_LANES = S5_GROUPS * S5_STATE
S5_EIG_MAX = -1e-4
HG_HEADS = 4
HG_DK = 128
HG_CHUNK = 32
RG_BLOCKS = 8
RG_BLOCK = 64
RG_C = 8.0
D_FF = 2816
EPS = 1e-6
IN_TOTAL = 6656
GM_WIDTH = N_BRANCH * D_MODEL
N_DEV = 8

ADAM_LR = 0.001
ADAM_B1 = 0.9
ADAM_B2 = 0.999
ADAM_EPS = 1e-08
ADAM_WD = 0.01
ADAM_STEP = 10

VMEM_LIMIT_V7X = 56 * 1024 * 1024
ROW_TILE = 256
FF_TILE = 1408
TOKEN_K = 2048


def _cparams(sem):
    return pltpu.CompilerParams(dimension_semantics=sem, vmem_limit_bytes=VMEM_LIMIT_V7X)


def _sigmoid(x):
    return 0.5 * jnp.tanh(0.5 * x) + 0.5


def _sigmoid_small(x):
    return 1.0 / (1.0 + jnp.exp(-x))


_GELU_C = math.sqrt(2.0 / math.pi)


def _gelu(x):
    t = jnp.tanh(_GELU_C * (x + 0.044715 * x * x * x))
    return 0.5 * x * (1.0 + t)


def _gelu_grad(x):
    t = jnp.tanh(_GELU_C * (x + 0.044715 * x * x * x))
    return 0.5 * (1.0 + t) + 0.5 * x * (1.0 - t * t) * _GELU_C * (1.0 + 3.0 * 0.044715 * x * x)


def _expm1(x):
    p = x * (1.0 + x * (0.5 + x * (1.0 / 6 + x * (1.0 / 24 + x * (1.0 / 120 + x * (1.0 / 720))))))
    return jnp.where(jnp.abs(x) < 0.3, p, jnp.exp(x) - 1.0)


def _dot(a, b):
    return jnp.dot(a, b, preferred_element_type=f32)


def _dot_nt(a, b):
    return lax.dot_general(a, b, (((1,), (1,)), ((), ())), preferred_element_type=f32)


def _dot_tn(a, b):
    return lax.dot_general(a, b, (((0,), (0,)), ((), ())), preferred_element_type=f32)


def _rows(shape):
    return lax.broadcasted_iota(jnp.int32, shape, 0)


def _scan_fwd(a, b, n):
    row = _rows(a.shape)
    s = 1
    while s < n:
        valid = row >= s
        sh_a = pltpu.roll(a, s, 0)
        sh_b = pltpu.roll(b, s, 0)
        b = b + a * jnp.where(valid, sh_b, 0.0)
        a = a * jnp.where(valid, sh_a, 1.0)
        s *= 2
    return a, b


def _scan_bwd(a, b, n):
    row = _rows(a.shape)
    s = 1
    while s < n:
        valid = row < n - s
        sh_a = pltpu.roll(a, n - s, 0)
        sh_b = pltpu.roll(b, n - s, 0)
        b = b + a * jnp.where(valid, sh_b, 0.0)
        a = a * jnp.where(valid, sh_a, 1.0)
        s *= 2
    return a, b


def _seg_cumsum(x, n, seg):
    pos = _rows(x.shape) % seg
    s = 1
    while s < seg:
        x = x + jnp.where(pos >= s, pltpu.roll(x, s, 0), 0.0)
        s *= 2
    return x


def _seg_rev_cumsum(x, n, seg):
    pos = _rows(x.shape) % seg
    s = 1
    while s < seg:
        x = x + jnp.where(pos < seg - s, pltpu.roll(x, n - s, 0), 0.0)
        s *= 2
    return x


def _head_mean(x):
    parts = []
    for h in range(HG_HEADS):
        m = jnp.mean(x[:, h * HG_DK:(h + 1) * HG_DK], axis=1, keepdims=True)
        parts.append(jnp.broadcast_to(m, (x.shape[0], HG_DK)))
    return jnp.concatenate(parts, axis=1)


def _mm(name, a_list, b_list, terms, n_acc, mode, m, n, k, tm, tn, tk, out_dtypes, epilogue, extras=(), vecs=(),
        n_part=0):
    tm, tn, tk = min(tm, m), min(tn, n), min(tk, k)
    assert m % tm == 0 and n % tn == 0 and k % tk == 0, (name, m, n, k, tm, tn, tk)
    gk = k // tk
    if mode == "tn":
        a_spec = pl.BlockSpec((tk, tm), lambda i, j, kk: (kk, i))
    else:
        a_spec = pl.BlockSpec((tm, tk), lambda i, j, kk: (i, kk))
    if mode == "nt":
        b_spec = pl.BlockSpec((tn, tk), lambda i, j, kk: (j, kk))
    else:
        b_spec = pl.BlockSpec((tk, tn), lambda i, j, kk: (kk, j))
    o_spec = pl.BlockSpec((tm, tn), lambda i, j, kk: (i, j))
    v_spec = pl.BlockSpec((1, tn), lambda i, j, kk: (0, j))
    p_spec = pl.BlockSpec((None, 1, tn), lambda i, j, kk: (i, 0, j))
    dot = {"nn": _dot, "nt": _dot_nt, "tn": _dot_tn}[mode]
    na, nb, ne, nv, no = len(a_list), len(b_list), len(extras), len(vecs), len(out_dtypes)

    def kern(*refs):
        a_refs = refs[:na]
        b_refs = refs[na:na + nb]
        e_refs = refs[na + nb:na + nb + ne]
        v_refs = refs[na + nb + ne:na + nb + ne + nv]
        o_refs = refs[na + nb + ne + nv:na + nb + ne + nv + no + n_part]

        def finish(accs):
            outs = epilogue(accs, [e[...] for e in e_refs], [r[...] for r in v_refs])
            for o, val in zip(o_refs, outs):
                o[...] = val.astype(o.dtype)

        def partial_sums():
            sums = [None] * n_acc
            for ai, bi, ci in terms:
                d = dot(a_refs[ai][...].astype(bf16), b_refs[bi][...].astype(bf16))
                sums[ci] = d if sums[ci] is None else sums[ci] + d
            return sums

        if gk == 1:
            finish(partial_sums())
            return
        acc = refs[na + nb + ne + nv + no + n_part]
        kk = pl.program_id(2)

        @pl.when(kk == 0)
        def _():
            acc[...] = jnp.zeros_like(acc)

        for ci, d in enumerate(partial_sums()):
            acc[ci] += d

        @pl.when(kk == gk - 1)
        def _():
            finish([acc[c] for c in range(n_acc)])

    return pl.pallas_call(
        kern, name=name,
        grid=(m // tm, n // tn, gk),
        in_specs=[a_spec] * na + [b_spec] * nb + [o_spec] * ne + [v_spec] * nv,
        out_specs=[o_spec] * no + [p_spec] * n_part,
        out_shape=([jax.ShapeDtypeStruct((m, n), dt) for dt in out_dtypes]
                   + [jax.ShapeDtypeStruct((m // tm, 1, n), f32)] * n_part),
        scratch_shapes=[pltpu.VMEM((n_acc, tm, tn), f32)] if gk > 1 else [],
        compiler_params=_cparams(("parallel", "parallel", "arbitrary")),
    )(*a_list, *b_list, *extras, *vecs)


def _mm1(name, a, b, mode, m, n, k, tm, tn, tk, out_dtype=f32, scale=None):
    def epi(accs, extras, vecs):
        return [accs[0] if scale is None else accs[0] * scale]
    return _mm(name, [a], [b], [(0, 0, 0)], 1, mode, m, n, k, tm, tn, tk, [out_dtype], epi)[0]


def _rt(name, body, rows, tm, row_ins, consts, row_outs, acc_outs=(), scratch=(), reverse=False):
    tm = min(tm, rows)
    assert rows % tm == 0
    nt = rows // tm

    def tile(i):
        return nt - 1 - i if reverse else i

    in_specs, args = [], []
    for spec in row_ins:
        arr = spec[0]
        if isinstance(spec[1], int):
            in_specs.append(pl.BlockSpec((tm, spec[1]), lambda i, cb=spec[2]: (tile(i), cb)))
        else:
            in_specs.append(pl.BlockSpec(spec[1], lambda i, fn=spec[2]: fn(tile(i))))
        args.append(arr)
    for c in consts:
        in_specs.append(pl.BlockSpec(c.shape, lambda i, nd=c.ndim: (0,) * nd))
        args.append(c)
    out_specs, out_shape = [], []
    for spec in row_outs:
        if isinstance(spec[0], int):
            out_specs.append(pl.BlockSpec((tm, spec[0]), lambda i: (tile(i), 0)))
            out_shape.append(jax.ShapeDtypeStruct((rows, spec[0]), spec[1]))
        else:
            out_specs.append(pl.BlockSpec(spec[1], lambda i, fn=spec[2]: fn(tile(i))))
            out_shape.append(jax.ShapeDtypeStruct(spec[0], spec[3]))
    for shp in acc_outs:
        out_specs.append(pl.BlockSpec(shp, lambda i, nd=len(shp): (0,) * nd))
        out_shape.append(jax.ShapeDtypeStruct(shp, f32))
    n_in = len(args)
    n_row_out = len(row_outs)
    n_acc = len(acc_outs)

    def kern(*refs):
        i = pl.program_id(0)
        acc_refs = refs[n_in + n_row_out:n_in + n_row_out + n_acc]

        @pl.when(i == 0)
        def _():
            for r in acc_refs:
                r[...] = jnp.zeros_like(r)

        body(i, *refs)

    return pl.pallas_call(
        kern, name=name, grid=(nt,), in_specs=in_specs, out_specs=out_specs, out_shape=out_shape,
        scratch_shapes=list(scratch), compiler_params=_cparams(("arbitrary",)),
    )(*args)


def _rms_rows(xv, wv):
    r = lax.rsqrt(jnp.mean(xv * xv, axis=1, keepdims=True) + EPS)
    return (xv * r * wv).astype(bf16)


def _rms_bwd_rows(xv, dhv, wv, dres):
    r = lax.rsqrt(jnp.mean(xv * xv, axis=1, keepdims=True) + EPS)
    xn = xv * r
    dxn = dhv * wv
    dx = dres + r * (dxn - xn * jnp.mean(dxn * xn, axis=1, keepdims=True))
    return [dx, dx.astype(bf16), jnp.sum(dhv * xn, axis=0, keepdims=True)]


def _rms_fwd(name, x, w, rows):
    def body(i, x_ref, w_ref, h_ref):
        h_ref[...] = _rms_rows(x_ref[...], w_ref[...])
    return _rt(name, body, rows, ROW_TILE, [(x, D_MODEL, 0)], [w], [(D_MODEL, bf16)])[0]


def _residual_then_norm(scale):
    def epi(accs, extras, vecs):
        x_out = extras[0] + scale * accs[0]
        return [x_out] + [_rms_rows(x_out, v) for v in vecs]
    return epi


def _norm_bwd_epilogue(accs, extras, vecs):
    return _rms_bwd_rows(extras[0], accs[0], vecs[0], extras[1])


def _loss_head(x, w, target, rows):
    def body(i, x_ref, t_ref, w_ref, dx_ref, dxb_ref, loss_ref, dw_ref):
        xv = x_ref[...]
        r = lax.rsqrt(jnp.mean(xv * xv, axis=1, keepdims=True) + EPS)
        xn = xv * r
        wv = w_ref[...]
        err = xn * wv - t_ref[...]
        part = 0.5 * jnp.sum(jnp.mean(err * err, axis=1, keepdims=True), axis=0, keepdims=True)
        loss_ref[...] += jnp.broadcast_to(part, (1, 128))
        dy = err * (1.0 / D_MODEL)
        dxn = dy * wv
        dx = r * (dxn - xn * jnp.mean(dxn * xn, axis=1, keepdims=True))
        dx_ref[...] = dx
        dxb_ref[...] = dx.astype(bf16)
        dw_ref[...] += jnp.sum(dy * xn, axis=0, keepdims=True)
    return _rt("loss_head", body, rows, ROW_TILE, [(x, D_MODEL, 0), (target, D_MODEL, 0)], [w],
               [(D_MODEL, f32), (D_MODEL, bf16)], acc_outs=[(1, 128), (1, D_MODEL)])


def _ffn_fwd(tag, x, hb, wg, wu, wd, next_nw, rows):
    def epi_up(accs, extras, vecs):
        a, b = accs
        return [a, b, a * _sigmoid(a) * b]
    a, b, s = _mm("ffn_up_" + tag, [hb], [wg, wu], [(0, 0, 0), (0, 1, 1)], 2, "nn", rows, D_FF, D_MODEL,
                  512, FF_TILE, D_MODEL, [bf16, bf16, bf16], epi_up)
    outs = _mm("ffn_down_" + tag, [s], [wd], [(0, 0, 0)], 1, "nn", rows, D_MODEL, D_FF,
               512, D_MODEL, FF_TILE, [f32] + [bf16] * len(next_nw), _residual_then_norm(0.5), extras=[x],
               vecs=next_nw)
    return outs[0], (outs[1] if next_nw else None), (x, hb, a, b, s)


def _ffn_bwd(tag, saved, nw, wg, wu, wd, dx, dxb, rows):
    x, hb, a, b, s = saved

    def epi_mid(accs, extras, vecs):
        ds = 0.5 * accs[0]
        av = extras[0].astype(f32)
        bv = extras[1].astype(f32)
        sg = _sigmoid(av)
        return [ds * bv * sg * (1.0 + av * (1.0 - sg)), ds * av * sg]
    da, db = _mm("ffn_bwd_mid_" + tag, [dxb], [wd], [(0, 0, 0)], 1, "nt", rows, D_FF, D_MODEL,
                 512, FF_TILE, D_MODEL, [bf16, bf16], epi_mid, extras=[a, b])
    d_wd = _mm1("ffn_dwd_" + tag, s, dxb, "tn", D_FF, D_MODEL, rows, FF_TILE, D_MODEL, TOKEN_K, out_dtype=bf16,
                scale=0.5)
    d_wg = _mm1("ffn_dwg_" + tag, hb, da, "tn", D_MODEL, D_FF, rows, D_MODEL, FF_TILE, TOKEN_K, out_dtype=bf16)
    d_wu = _mm1("ffn_dwu_" + tag, hb, db, "tn", D_MODEL, D_FF, rows, D_MODEL, FF_TILE, TOKEN_K, out_dtype=bf16)
    dx_in, dxb_in, d_nw = _mm("ffn_dh_" + tag, [da, db], [wg, wu], [(0, 0, 0), (1, 1, 0)], 1, "nt", rows, D_MODEL,
                              D_FF, 512, D_MODEL, FF_TILE, [f32, bf16], _norm_bwd_epilogue, extras=[x, dx], vecs=[nw],
                              n_part=1)
    return dx_in, dxb_in, jnp.sum(d_nw, axis=0), d_wg, d_wu, d_wd


S5_CB = 512
SUBLANES = 8
U_COL = GM_WIDTH // BRANCH


def _s5_scan_fwd(tag, proj, b_re, b_im, a_re, a_im, rows):
    tm = min(ROW_TILE, rows)
    nt = rows // tm
    nc = S5_LANES // S5_CB

    def kern(u_ref, bre_ref, bim_ref, ar_ref, ai_ref, xr_ref, xi_ref, pr_s, pi_s, cr_s, ci_s):
        t = pl.program_id(1)

        @pl.when(t == 0)
        def _():
            row8 = _rows((SUBLANES, S5_CB))
            pr = jnp.broadcast_to(ar_ref[...], (SUBLANES, S5_CB))
            pi = jnp.broadcast_to(ai_ref[...], (SUBLANES, S5_CB))
            s = 1
            while s < SUBLANES:
                sr = pltpu.roll(pr, s, 0)
                si = pltpu.roll(pi, s, 0)
                valid = row8 >= s
                pr, pi = jnp.where(valid, pr * sr - pi * si, pr), jnp.where(valid, pr * si + pi * sr, pi)
                s *= 2
            pr_s[...] = pr
            pi_s[...] = pi
            cr_s[...] = jnp.zeros_like(cr_s)
            ci_s[...] = jnp.zeros_like(ci_s)

        ub = u_ref[...].astype(bf16)
        br = _dot(ub, bre_ref[...])
        bi = _dot(ub, bim_ref[...])
        pos = _rows((tm, S5_CB)) % SUBLANES
        s = 1
        while s < SUBLANES:
            mr = pr_s[s - 1:s, :]
            mi = pi_s[s - 1:s, :]
            sr = pltpu.roll(br, s, 0)
            si = pltpu.roll(bi, s, 0)
            valid = pos >= s
            br, bi = (br + jnp.where(valid, mr * sr - mi * si, 0.0),
                      bi + jnp.where(valid, mr * si + mi * sr, 0.0))
            s *= 2
        cr = cr_s[...]
        ci = ci_s[...]
        pr = pr_s[...]
        pi = pi_s[...]
        for g in range(tm // SUBLANES):
            sl = slice(g * SUBLANES, (g + 1) * SUBLANES)
            xr = br[sl] + pr * cr - pi * ci
            xi = bi[sl] + pr * ci + pi * cr
            xr_ref[sl, :] = xr
            xi_ref[sl, :] = xi
            cr = xr[SUBLANES - 1:SUBLANES, :]
            ci = xi[SUBLANES - 1:SUBLANES, :]
        cr_s[...] = cr
        ci_s[...] = ci

    return pl.pallas_call(
        kern, name="s5_scan_fwd_" + tag, grid=(nc, nt),
        in_specs=[pl.BlockSpec((tm, BRANCH), lambda c, t: (t, U_COL)),
                  pl.BlockSpec((BRANCH, S5_CB), lambda c, t: (0, c)),
                  pl.BlockSpec((BRANCH, S5_CB), lambda c, t: (0, c)),
                  pl.BlockSpec((1, S5_CB), lambda c, t: (0, c)),
                  pl.BlockSpec((1, S5_CB), lambda c, t: (0, c))],
        out_specs=[pl.BlockSpec((tm, S5_CB), lambda c, t: (t, c))] * 2,
        out_shape=[jax.ShapeDtypeStruct((rows, S5_LANES), f32)] * 2,
        scratch_shapes=[pltpu.VMEM((SUBLANES, S5_CB), f32), pltpu.VMEM((SUBLANES, S5_CB), f32),
                        pltpu.VMEM((1, S5_CB), f32), pltpu.VMEM((1, S5_CB), f32)],
        compiler_params=_cparams(("parallel", "arbitrary")),
    )(proj, b_re, b_im, a_re, a_im)


def _s5_scan_bwd(tag, dxr, dxi, xr, xi, a_re, a_im, rows):
    tm = min(ROW_TILE, rows)
    nt = rows // tm
    nc = S5_LANES // S5_CB

    def kern(dxr_ref, dxi_ref, xr_ref, xi_ref, ar_ref, ai_ref, gr_ref, gi_ref, dar_ref, dai_ref,
             qr_s, qi_s, cr_s, ci_s, gr_s, gi_s):
        t = pl.program_id(1)
        row = _rows((tm, S5_CB))

        @pl.when(t == 0)
        def _():
            row8 = _rows((SUBLANES, S5_CB))
            qr = jnp.broadcast_to(ar_ref[...], (SUBLANES, S5_CB))
            qi = jnp.broadcast_to(-ai_ref[...], (SUBLANES, S5_CB))
            s = 1
            while s < SUBLANES:
                sr = pltpu.roll(qr, SUBLANES - s, 0)
                si = pltpu.roll(qi, SUBLANES - s, 0)
                valid = row8 < SUBLANES - s
                qr, qi = jnp.where(valid, qr * sr - qi * si, qr), jnp.where(valid, qr * si + qi * sr, qi)
                s *= 2
            qr_s[...] = qr
            qi_s[...] = qi
            cr_s[...] = jnp.zeros_like(cr_s)
            ci_s[...] = jnp.zeros_like(ci_s)
            dar_ref[...] = jnp.zeros_like(dar_ref)
            dai_ref[...] = jnp.zeros_like(dai_ref)

        br = dxr_ref[...]
        bi = dxi_ref[...]
        pos = row % SUBLANES
        s = 1
        while s < SUBLANES:
            mr = qr_s[SUBLANES - s:SUBLANES - s + 1, :]
            mi = qi_s[SUBLANES - s:SUBLANES - s + 1, :]
            sr = pltpu.roll(br, tm - s, 0)
            si = pltpu.roll(bi, tm - s, 0)
            valid = pos < SUBLANES - s
            br, bi = (br + jnp.where(valid, mr * sr - mi * si, 0.0),
                      bi + jnp.where(valid, mr * si + mi * sr, 0.0))
            s *= 2
        cin_r = cr_s[...]
        cin_i = ci_s[...]
        cr, ci = cin_r, cin_i
        qr = qr_s[...]
        qi = qi_s[...]
        for g in reversed(range(tm // SUBLANES)):
            sl = slice(g * SUBLANES, (g + 1) * SUBLANES)
            gr = br[sl] + qr * cr - qi * ci
            gi = bi[sl] + qr * ci + qi * cr
            gr_s[sl, :] = gr
            gi_s[sl, :] = gi
            cr = gr[0:1, :]
            ci = gi[0:1, :]
        cr_s[...] = cr
        ci_s[...] = ci
        gr = gr_s[...]
        gi = gi_s[...]
        gr_ref[...] = gr.astype(bf16)
        gi_ref[...] = gi.astype(bf16)
        last = row == tm - 1
        gnr = jnp.where(last, cin_r, pltpu.roll(gr, tm - 1, 0))
        gni = jnp.where(last, cin_i, pltpu.roll(gi, tm - 1, 0))
        xr_v = xr_ref[...]
        xi_v = xi_ref[...]
        dar_ref[...] += jnp.sum(gnr * xr_v + gni * xi_v, axis=0, keepdims=True)
        dai_ref[...] += jnp.sum(gni * xr_v - gnr * xi_v, axis=0, keepdims=True)

    rev = lambda c, t: (nt - 1 - t, c)
    return pl.pallas_call(
        kern, name="s5_scan_bwd_" + tag, grid=(nc, nt),
        in_specs=[pl.BlockSpec((tm, S5_CB), rev)] * 4 + [pl.BlockSpec((1, S5_CB), lambda c, t: (0, c))] * 2,
        out_specs=[pl.BlockSpec((tm, S5_CB), rev)] * 2 + [pl.BlockSpec((1, S5_CB), lambda c, t: (0, c))] * 2,
        out_shape=[jax.ShapeDtypeStruct((rows, S5_LANES), bf16)] * 2 + [jax.ShapeDtypeStruct((1, S5_LANES), f32)] * 2,
        scratch_shapes=[pltpu.VMEM((SUBLANES, S5_CB), f32), pltpu.VMEM((SUBLANES, S5_CB), f32),
                        pltpu.VMEM((1, S5_CB), f32), pltpu.VMEM((1, S5_CB), f32),
                        pltpu.VMEM((tm, S5_CB), f32), pltpu.VMEM((tm, S5_CB), f32)],
        compiler_params=_cparams(("parallel", "arbitrary")),
    )(dxr, dxi, xr, xi, a_re, a_im)


def _s5_fwd(tag, proj, cst, rows):
    xr, xi = _s5_scan_fwd(tag, proj, cst["b_re"].astype(bf16), cst["b_im"].astype(bf16), cst["a_re"], cst["a_im"], rows)

    def body(i, xr_ref, xi_ref, u_ref, cre_ref, cim_ref, d_ref, gw_ref, gb_ref, y_ref, out_ref):
        y = (_dot(xr_ref[...].astype(bf16), cre_ref[...]) + _dot(xi_ref[...].astype(bf16), cim_ref[...])
             + d_ref[...] * u_ref[...])
        y_ref[...] = y
        z = _gelu(y)
        zg = _dot(z.astype(bf16), gw_ref[...]) + gb_ref[...]
        out_ref[...] = (z * _sigmoid(zg)).astype(bf16)

    y, out = _rt("s5_out_" + tag, body, rows, ROW_TILE,
                 [(xr, S5_LANES, 0), (xi, S5_LANES, 0), (proj, BRANCH, U_COL)],
                 [cst["c_re"].astype(bf16), cst["c_im"].astype(bf16), cst["s5_d"], cst["glu_w"], cst["glu_b"]],
                 [(BRANCH, f32), (BRANCH, bf16)])
    return out, (xr, xi, y)


def _s5_bwd(tag, saved, proj, cst, d_out, rows):
    xr, xi, y = saved
    c_re = cst["c_re"].astype(bf16)
    c_im = cst["c_im"].astype(bf16)

    def body(i, do_ref, y_ref, u_ref, xr_ref, xi_ref, cre_ref, cim_ref, gw_ref, gb_ref,
             dxr_ref, dxi_ref, dy_ref, dgw_ref, dgb_ref, dd_ref, dcre_ref, dcim_ref):
        yv = y_ref[...]
        z = _gelu(yv)
        zb = z.astype(bf16)
        gt = _sigmoid(_dot(zb, gw_ref[...]) + gb_ref[...])
        dov = do_ref[...]
        dzg = dov * z * gt * (1.0 - gt)
        dzgb = dzg.astype(bf16)
        dz = dov * gt + _dot_nt(dzgb, gw_ref[...])
        dgw_ref[...] += _dot_tn(zb, dzgb)
        dgb_ref[...] += jnp.sum(dzg, axis=0, keepdims=True)
        dy = dz * _gelu_grad(yv)
        dy_ref[...] = dy
        dd_ref[...] += jnp.sum(dy * u_ref[...], axis=0, keepdims=True)
        dyb = dy.astype(bf16)
        dxr_ref[...] = _dot_nt(dyb, cre_ref[...])
        dxi_ref[...] = _dot_nt(dyb, cim_ref[...])
        dcre_ref[...] += _dot_tn(xr_ref[...].astype(bf16), dyb)
        dcim_ref[...] += _dot_tn(xi_ref[...].astype(bf16), dyb)

    dxr, dxi, dy, d_gw, d_gb, d_d, d_cre, d_cim = _rt(
        "s5_out_bwd_" + tag, body, rows, ROW_TILE,
        [(d_out, BRANCH, 0), (y, BRANCH, 0), (proj, BRANCH, U_COL), (xr, S5_LANES, 0), (xi, S5_LANES, 0)],
        [c_re, c_im, cst["glu_w"], cst["glu_b"]],
        [(S5_LANES, f32), (S5_LANES, f32), (BRANCH, f32)],
        acc_outs=[(BRANCH, BRANCH), (1, BRANCH), (1, BRANCH), (S5_LANES, BRANCH), (S5_LANES, BRANCH)])

    gr, gi, d_ar, d_ai = _s5_scan_bwd(tag, dxr, dxi, xr, xi, cst["a_re"], cst["a_im"], rows)
    b_re = cst["b_re"].astype(bf16)
    b_im = cst["b_im"].astype(bf16)

    def body_in(i, gr_ref, gi_ref, dy_ref, u_ref, bre_ref, bim_ref, d_ref, du_ref, dbre_ref, dbim_ref):
        grv = gr_ref[...]
        giv = gi_ref[...]
        du = _dot_nt(grv, bre_ref[...]) + _dot_nt(giv, bim_ref[...]) + dy_ref[...] * d_ref[...]
        du_ref[...] = du.astype(bf16)
        ub = u_ref[...].astype(bf16)
        dbre_ref[...] += _dot_tn(ub, grv)
        dbim_ref[...] += _dot_tn(ub, giv)

    du, d_bre, d_bim = _rt("s5_in_bwd_" + tag, body_in, rows, ROW_TILE,
                           [(gr, S5_LANES, 0), (gi, S5_LANES, 0), (dy, BRANCH, 0), (proj, BRANCH, U_COL)],
                           [b_re, b_im, cst["s5_d"]], [(BRANCH, bf16)],
                           acc_outs=[(BRANCH, S5_LANES), (BRANCH, S5_LANES)])
    dcst = {"b_re": d_bre, "b_im": d_bim, "a_re": d_ar, "a_im": d_ai, "c_re": d_cre, "c_im": d_cim,
            "s5_d": d_d, "glu_b": d_gb}
    return du, dcst, d_gw


def _hg_prep(q, z, lb):
    qs = _sigmoid(q)
    qh = q * qs
    sg = _sigmoid_small(z)
    fg = lb + (1.0 - lb) * sg
    kk = (1.0 - lb) * (1.0 - sg)
    return qs, qh, sg, fg, kk


def _hg_fwd(tag, proj, cst, rows):
    tm = min(ROW_TILE, rows)
    c_sz = HG_CHUNK
    nch = tm // c_sz
    n_chunks = rows // c_sz

    def body(i, q_ref, z_ref, v_ref, g_ref, lb_ref, nw_ref, out_ref, o_ref, ss_ref, sn_ref,
             st_s, qh_s, kh_s, vb_s, b_s, k_s):
        @pl.when(i == 0)
        def _():
            st_s[...] = jnp.zeros_like(st_s)

        lb = lb_ref[...]
        _, qh, sg, fg, kk = _hg_prep(q_ref[...], z_ref[...], lb)
        b = _seg_cumsum(jnp.log(fg), tm, c_sz)
        b_s[...] = b
        k_s[...] = kk
        qh_s[...] = (qh * jnp.exp(b)).astype(bf16)
        kh_s[...] = (kk * jnp.exp(-b)).astype(bf16)
        vb_s[...] = v_ref[...].astype(bf16)
        tril = _rows((c_sz, c_sz)) >= lax.broadcasted_iota(jnp.int32, (c_sz, c_sz), 1)

        def chunk(ci, carry):
            sl = pl.ds(pl.multiple_of(ci * c_sz, c_sz), c_sz)
            for h in range(HG_HEADS):
                hl = slice(h * HG_DK, (h + 1) * HG_DK)
                qb = qh_s[sl, hl]
                kb = kh_s[sl, hl]
                vb = vb_s[sl, hl]
                a_mat = jnp.where(tril, _dot_nt(qb, kb), 0.0)
                st = st_s[hl, :]
                stb = st.astype(bf16)
                o_ref[sl, hl] = _dot_nt(qb, stb) + _dot(a_mat.astype(bf16), vb)
                ss_ref[ci, hl, :] = stb
                bb = b_s[sl, hl]
                bl = bb[c_sz - 1:c_sz, :]
                kd = (k_s[sl, hl] * jnp.exp(bl - bb)).astype(bf16)
                st_new = st * jnp.exp(bl) + _dot_tn(vb, kd)
                st_s[hl, :] = st_new
                sn_ref[ci, hl, :] = st_new.astype(bf16)
            return carry

        lax.fori_loop(0, nch, chunk, 0)
        o = o_ref[...]
        r = lax.rsqrt(_head_mean(o * o) + EPS)
        g = g_ref[...]
        out_ref[...] = (o * r * nw_ref[...] * (g * _sigmoid(g))).astype(bf16)

    out, o, ss, sn = _rt(
        "hg_fwd_" + tag, body, rows, tm,
        [(proj, BRANCH, U_COL + 1), (proj, BRANCH, U_COL + 2), (proj, BRANCH, U_COL + 3), (proj, BRANCH, U_COL + 4)],
        [cst["hg_lb"], cst["hg_nw"]],
        [(BRANCH, bf16), (BRANCH, f32),
         ((n_chunks, BRANCH, HG_DK), (nch, BRANCH, HG_DK), lambda t: (t, 0, 0), bf16),
         ((n_chunks, BRANCH, HG_DK), (nch, BRANCH, HG_DK), lambda t: (t, 0, 0), bf16)],
        scratch=[pltpu.VMEM((BRANCH, HG_DK), f32), pltpu.VMEM((tm, BRANCH), bf16), pltpu.VMEM((tm, BRANCH), bf16),
                 pltpu.VMEM((tm, BRANCH), bf16), pltpu.VMEM((tm, BRANCH), f32), pltpu.VMEM((tm, BRANCH), f32)])
    return out, (o, ss, sn)


def _hg_bwd(tag, saved, proj, cst, d_out, rows):
    o_saved, ss, sn = saved
    tm = min(ROW_TILE, rows)
    c_sz = HG_CHUNK
    nch = tm // c_sz

    def body(i, do_ref, q_ref, z_ref, v_ref, g_ref, o_ref, ss_ref, sn_ref, lb_ref, nw_ref,
             dq_ref, dz_ref, dv_ref, dg_ref, dlb_ref, dnw_ref,
             dst_s, flux_s, qh_s, kh_s, vb_s, b_s, k_s, dob_s, dqh_s, dk_s, db_s):
        @pl.when(i == 0)
        def _():
            dst_s[...] = jnp.zeros_like(dst_s)

        lb = lb_ref[...]
        q = q_ref[...]
        qs, qh, sg, fg, kk = _hg_prep(q, z_ref[...], lb)
        b = _seg_cumsum(jnp.log(fg), tm, c_sz)
        b_s[...] = b
        k_s[...] = kk
        qh_s[...] = (qh * jnp.exp(b)).astype(bf16)
        kh_s[...] = (kk * jnp.exp(-b)).astype(bf16)
        vb_s[...] = v_ref[...].astype(bf16)
        g = g_ref[...]
        gs = _sigmoid(g)
        o = o_ref[...]
        r = lax.rsqrt(_head_mean(o * o) + EPS)
        oh = o * r
        nw = nw_ref[...]
        dov = do_ref[...]
        don = dov * (g * gs)
        dg_ref[...] = (dov * oh * nw * (gs * (1.0 + g * (1.0 - gs)))).astype(bf16)
        dnw_ref[...] += jnp.sum(don * oh, axis=0, keepdims=True)
        doh = don * nw
        d_o = r * (doh - oh * _head_mean(doh * oh))
        dob_s[...] = d_o.astype(bf16)
        tril = _rows((c_sz, c_sz)) >= lax.broadcasted_iota(jnp.int32, (c_sz, c_sz), 1)

        def chunk(cj, carry):
            ci = nch - 1 - cj
            sl = pl.ds(pl.multiple_of(ci * c_sz, c_sz), c_sz)
            for h in range(HG_HEADS):
                hl = slice(h * HG_DK, (h + 1) * HG_DK)
                qb = qh_s[sl, hl]
                kb = kh_s[sl, hl]
                vb = vb_s[sl, hl]
                dob = dob_s[sl, hl]
                stb = ss_ref[ci, hl, :]
                dst = dst_s[hl, :]
                dstb = dst.astype(bf16)
                a_mat = jnp.where(tril, _dot_nt(qb, kb), 0.0).astype(bf16)
                da_mat = jnp.where(tril, _dot_nt(dob, vb), 0.0).astype(bf16)
                bb = b_s[sl, hl]
                bl = bb[c_sz - 1:c_sz, :]
                ebl = jnp.exp(bl - bb)
                dqhat = _dot(dob, stb) + _dot(da_mat, kb)
                dkhat = _dot_tn(da_mat, qb)
                dk_inter = _dot(vb, dstb) * ebl
                kv = k_s[sl, hl]
                dqh_s[sl, hl] = dqhat * jnp.exp(bb)
                dk_s[sl, hl] = dkhat * jnp.exp(-bb) + dk_inter
                db_s[sl, hl] = qb.astype(f32) * dqhat - kb.astype(f32) * dkhat - kv * dk_inter
                flux = jnp.sum(sn_ref[ci, hl, :].astype(f32) * dst, axis=0, keepdims=True)
                flux_s[sl, hl] = jnp.broadcast_to(flux, (c_sz, HG_DK))
                kd = (kv * ebl).astype(bf16)
                dv_ref[sl, hl] = (_dot_tn(a_mat, dob) + _dot_nt(kd, dstb)).astype(bf16)
                dst_s[hl, :] = dst * jnp.exp(bl) + _dot_tn(dob, qb)
            return carry

        lax.fori_loop(0, nch, chunk, 0)
        dqh = dqh_s[...]
        dk = dk_s[...]
        dlf = _seg_rev_cumsum(db_s[...], tm, c_sz) + flux_s[...]
        tt = (1.0 - lb) * sg * (1.0 - sg)
        dz_ref[...] = (dlf * tt / fg - dk * tt).astype(bf16)
        dlb_ref[...] += jnp.sum(dlf * (1.0 - sg) / fg - dk * (1.0 - sg), axis=0, keepdims=True)
        dq_ref[...] = (dqh * (qs * (1.0 + q * (1.0 - qs)))).astype(bf16)

    dq, dz, dv, dg, d_lb, d_nw = _rt(
        "hg_bwd_" + tag, body, rows, tm,
        [(d_out, BRANCH, 0), (proj, BRANCH, U_COL + 1), (proj, BRANCH, U_COL + 2), (proj, BRANCH, U_COL + 3),
         (proj, BRANCH, U_COL + 4), (o_saved, BRANCH, 0), (ss, (nch, BRANCH, HG_DK), lambda t: (t, 0, 0)),
         (sn, (nch, BRANCH, HG_DK), lambda t: (t, 0, 0))],
        [cst["hg_lb"], cst["hg_nw"]],
        [(BRANCH, bf16)] * 4, acc_outs=[(1, BRANCH), (1, BRANCH)],
        scratch=[pltpu.VMEM((BRANCH, HG_DK), f32), pltpu.VMEM((tm, BRANCH), f32),
                 pltpu.VMEM((tm, BRANCH), bf16), pltpu.VMEM((tm, BRANCH), bf16), pltpu.VMEM((tm, BRANCH), bf16),
                 pltpu.VMEM((tm, BRANCH), f32), pltpu.VMEM((tm, BRANCH), f32), pltpu.VMEM((tm, BRANCH), bf16),
                 pltpu.VMEM((tm, BRANCH), f32), pltpu.VMEM((tm, BRANCH), f32), pltpu.VMEM((tm, BRANCH), f32)],
        reverse=True)
    return dq, dz, dv, dg, {"hg_lb": d_lb, "hg_nw": d_nw}


def _rg_gates(xc, wa_ref, ba_ref, wx_ref, bx_ref, sp8):
    xcb = xc.astype(bf16)
    r = _sigmoid(_dot(xcb, wa_ref[...]) + ba_ref[...])
    ig = _sigmoid(_dot(xcb, wx_ref[...]) + bx_ref[...])
    la = -sp8 * r
    a = jnp.exp(la)
    mult = jnp.sqrt(-_expm1(2.0 * la))
    return xcb, r, ig, a, mult


def _rg_fwd(tag, proj, cst, rows):
    tm = min(ROW_TILE, rows)

    def body(i, xb_ref, gate_ref, cw_ref, cb_ref, wa_ref, ba_ref, wx_ref, bx_ref, sp_ref,
             out_ref, xc_ref, h_ref, hp_ref, prev_s, hc_s):
        @pl.when(i == 0)
        def _():
            prev_s[...] = jnp.zeros_like(prev_s)
            hc_s[...] = jnp.zeros_like(hc_s)

        row = _rows((tm, BRANCH))
        xb = xb_ref[...]
        prev = prev_s[...]
        xc = cb_ref[...] + cw_ref[3:4, :] * xb
        for j in range(1, 4):
            sh = jnp.where(row >= j, pltpu.roll(xb, j, 0), pltpu.roll(prev, j, 0))
            xc = xc + cw_ref[3 - j:4 - j, :] * sh
        prev_s[...] = xb
        xc_ref[...] = xc
        _, r, ig, a, mult = _rg_gates(xc, wa_ref, ba_ref, wx_ref, bx_ref, sp_ref[...])
        a_cum, h_loc = _scan_fwd(a, mult * ig * xc, tm)
        hc = hc_s[...]
        h = h_loc + a_cum * hc
        h_ref[...] = h
        hp_ref[...] = jnp.where(row >= 1, pltpu.roll(h, 1, 0), hc)
        hc_s[...] = h[tm - 1:tm, :]
        out_ref[...] = (h * _gelu(gate_ref[...])).astype(bf16)

    out, xc, h, hp = _rt(
        "rg_fwd_" + tag, body, rows, tm,
        [(proj, BRANCH, U_COL + 5), (proj, BRANCH, U_COL + 6)],
        [cst["rg_cw"], cst["rg_cb"], cst["rg_wa"].astype(bf16), cst["rg_ba"], cst["rg_wx"].astype(bf16),
         cst["rg_bx"], cst["rg_sp8"]],
        [(BRANCH, bf16), (BRANCH, f32), (BRANCH, f32), (BRANCH, f32)],
        scratch=[pltpu.VMEM((tm, BRANCH), f32), pltpu.VMEM((1, BRANCH), f32)])
    return out, (xc, h, hp)


def _rg_bwd(tag, saved, proj, cst, d_out, rows):
    xc_saved, h_saved, hp_saved = saved
    tm = min(ROW_TILE, rows)

    def body(i, do_ref, xb_ref, gate_ref, xc_ref, h_ref, hp_ref, cw_ref, wa_ref, ba_ref, wx_ref, bx_ref, sp_ref,
             dxb_ref, dgate_ref, dcw_ref, dcb_ref, dwa_ref, dba_ref, dwx_ref, dbx_ref, dsp_ref,
             nxt_s, ec_s):
        @pl.when(i == 0)
        def _():
            nxt_s[...] = jnp.zeros_like(nxt_s)
            ec_s[...] = jnp.zeros_like(ec_s)

        row = _rows((tm, BRANCH))
        xc = xc_ref[...]
        sp8 = sp_ref[...]
        xcb, r, ig, a, mult = _rg_gates(xc, wa_ref, ba_ref, wx_ref, bx_ref, sp8)
        gate = gate_ref[...]
        dov = do_ref[...]
        dh = dov * _gelu(gate)
        dgate_ref[...] = (dov * h_ref[...] * _gelu_grad(gate)).astype(bf16)
        a_cum, e_loc = _scan_bwd(a, a * dh, tm)
        ec = ec_s[...]
        e = e_loc + a_cum * ec
        g_tot = dh + jnp.where(row == tm - 1, ec, pltpu.roll(e, tm - 1, 0))
        ec_s[...] = e[0:1, :]
        d_a = g_tot * hp_ref[...]
        d_mult = g_tot * ig * xc
        d_ix = g_tot * mult
        d_ig = d_ix * xc
        d_xc = d_ix * ig
        d_la = d_a * a - d_mult * (a * a) / mult
        d_r = -d_la * sp8
        dsp_ref[...] += jnp.sum(-d_la * r, axis=0, keepdims=True)
        dzr = d_r * r * (1.0 - r)
        dzi = d_ig * ig * (1.0 - ig)
        dzrb = dzr.astype(bf16)
        dzib = dzi.astype(bf16)
        d_xc = d_xc + _dot_nt(dzrb, wa_ref[...]) + _dot_nt(dzib, wx_ref[...])
        dwa_ref[...] += _dot_tn(xcb, dzrb)
        dwx_ref[...] += _dot_tn(xcb, dzib)
        dba_ref[...] += jnp.sum(dzr, axis=0, keepdims=True)
        dbx_ref[...] += jnp.sum(dzi, axis=0, keepdims=True)
        dcb_ref[...] += jnp.sum(d_xc, axis=0, keepdims=True)
        nxt = nxt_s[...]
        xb = xb_ref[...]
        dxb = cw_ref[3:4, :] * d_xc
        dcw_ref[3:4, :] += jnp.sum(d_xc * xb, axis=0, keepdims=True)
        for j in range(1, 4):
            sh = jnp.where(row < tm - j, pltpu.roll(d_xc, tm - j, 0), pltpu.roll(nxt, tm - j, 0))
            dxb = dxb + cw_ref[3 - j:4 - j, :] * sh
            dcw_ref[3 - j:4 - j, :] += jnp.sum(sh * xb, axis=0, keepdims=True)
        nxt_s[...] = d_xc
        dxb_ref[...] = dxb.astype(bf16)

    wa = cst["rg_wa"].astype(bf16)
    wx = cst["rg_wx"].astype(bf16)
    dxb, dgate, d_cw, d_cb, d_wa, d_ba, d_wx, d_bx, d_sp = _rt(
        "rg_bwd_" + tag, body, rows, tm,
        [(d_out, BRANCH, 0), (proj, BRANCH, U_COL + 5), (proj, BRANCH, U_COL + 6), (xc_saved, BRANCH, 0),
         (h_saved, BRANCH, 0), (hp_saved, BRANCH, 0)],
        [cst["rg_cw"], wa, cst["rg_ba"], wx, cst["rg_bx"], cst["rg_sp8"]],
        [(BRANCH, bf16), (BRANCH, bf16)],
        acc_outs=[(4, BRANCH), (1, BRANCH), (BRANCH, BRANCH), (1, BRANCH), (BRANCH, BRANCH), (1, BRANCH), (1, BRANCH)],
        scratch=[pltpu.VMEM((tm, BRANCH), f32), pltpu.VMEM((1, BRANCH), f32)],
        reverse=True)
    dcst = {"rg_cw": d_cw, "rg_cb": d_cb, "rg_wa": d_wa, "rg_ba": d_ba, "rg_wx": d_wx, "rg_bx": d_bx, "rg_sp8": d_sp}
    return dxb, dgate, dcst


def _merge_fwd(tag, proj, outs, bp, rows):
    def body(i, ya_ref, yb_ref, yc_ref, gm_ref, p_ref, m_ref):
        acc = None
        for n, y_ref in enumerate((ya_ref, yb_ref, yc_ref)):
            up = _dot(y_ref[...], p_ref[n])
            term = _sigmoid(gm_ref[:, n * D_MODEL:(n + 1) * D_MODEL]) * up
            acc = term if acc is None else acc + term
        m_ref[...] = acc.astype(bf16)
    return _rt("merge_fwd_" + tag, body, rows, ROW_TILE,
               [(outs[0], BRANCH, 0), (outs[1], BRANCH, 0), (outs[2], BRANCH, 0), (proj, GM_WIDTH, 0)],
               [bp], [(D_MODEL, bf16)])[0]


def _merge_bwd(tag, proj, outs, bp, dmerged, rows):
    def body(i, dm_ref, ya_ref, yb_ref, yc_ref, gm_ref, p_ref, da_ref, db_ref, dc_ref, dgm_ref, dp_ref):
        dm = dm_ref[...]
        for n, (y_ref, dy_ref) in enumerate(((ya_ref, da_ref), (yb_ref, db_ref), (yc_ref, dc_ref))):
            yv = y_ref[...]
            up = _dot(yv, p_ref[n])
            gt = _sigmoid(gm_ref[:, n * D_MODEL:(n + 1) * D_MODEL])
            dup = (dm * gt).astype(bf16)
            dgm_ref[:, n * D_MODEL:(n + 1) * D_MODEL] = (dm * up * gt * (1.0 - gt)).astype(bf16)
            dy_ref[...] = _dot_nt(dup, p_ref[n])
            dp_ref[n] += _dot_tn(yv, dup)
    return _rt("merge_bwd_" + tag, body, rows, ROW_TILE,
               [(dmerged, D_MODEL, 0), (outs[0], BRANCH, 0), (outs[1], BRANCH, 0), (outs[2], BRANCH, 0),
                (proj, GM_WIDTH, 0)],
               [bp], [(BRANCH, f32), (BRANCH, f32), (BRANCH, f32), (GM_WIDTH, bf16)],
               acc_outs=[(N_BRANCH, BRANCH, D_MODEL)])


def _block_diag(blocks):
    g, r, c = blocks.shape
    on_diag = (lax.broadcasted_iota(jnp.int32, (g * r, g * c), 0) // r
               == lax.broadcasted_iota(jnp.int32, (g * r, g * c), 1) // c)
    tiled = jnp.broadcast_to(blocks.reshape(g * r, 1, c), (g * r, g, c)).reshape(g * r, g * c)
    return jnp.where(on_diag, tiled, 0.0)


def _prep_consts(sp):
    p = jax.nn.softmax(sp["hg_lb_logits"], axis=0)
    lower = jnp.cumsum(p, axis=0) - p[0]
    out = []
    for l in range(DEPTH):
        lr = jnp.minimum(sp["s5_lambda_re"][l], S5_EIG_MAX)
        li = sp["s5_lambda_im"][l]
        dt = jnp.exp(sp["s5_log_dt"][l])[:, None]
        mag = jnp.exp(lr * dt)
        ar = mag * jnp.cos(li * dt)
        ai = mag * jnp.sin(li * dt)
        den = lr * lr + li * li
        fr = ((ar - 1.0) * lr + ai * li) / den
        fi = (ai * lr - (ar - 1.0) * li) / den
        br, bi = sp["s5_b_re"][l], sp["s5_b_im"][l]
        bbr = fr[..., None] * br - fi[..., None] * bi
        bbi = fr[..., None] * bi + fi[..., None] * br
        c = {
            "a_re": ar.reshape(1, S5_LANES), "a_im": ai.reshape(1, S5_LANES),
            "b_re": _block_diag(bbr.transpose(0, 2, 1)), "b_im": _block_diag(bbi.transpose(0, 2, 1)),
            "c_re": _block_diag(sp["s5_c_re"][l].transpose(0, 2, 1)),
            "c_im": -_block_diag(sp["s5_c_im"][l].transpose(0, 2, 1)),
            "s5_d": sp["s5_d"][l][None], "glu_b": sp["s5_glu_b"][l][None],
            "hg_lb": lower[l][None], "hg_nw": sp["hg_norm_w"][l][None],
            "rg_cw": sp["rg_conv_w"][l], "rg_cb": sp["rg_conv_b"][l][None],
            "rg_wa": _block_diag(sp["rg_wa"][l]), "rg_ba": sp["rg_ba"][l][None],
            "rg_wx": _block_diag(sp["rg_wx"][l]), "rg_bx": sp["rg_bx"][l][None],
            "rg_sp8": (RG_C * jax.nn.softplus(-sp["rg_lambda"][l]))[None],
        }
        out.append(c)
    return out


def _mixer_fwd(tag, x, hb, w_in, bp, w_out, cst, next_nw, rows):
    proj = _mm1("mix_proj_" + tag, hb, w_in, "nn", rows, IN_TOTAL, D_MODEL, 512, 512, D_MODEL)
    cst = dict(cst)
    out_a, sv_a = _s5_fwd(tag, proj, cst, rows)
    out_b, sv_b = _hg_fwd(tag, proj, cst, rows)
    out_c, sv_c = _rg_fwd(tag, proj, cst, rows)
    merged = _merge_fwd(tag, proj, (out_a, out_b, out_c), bp, rows)
    x_out, hb_out = _mm("mix_out_" + tag, [merged], [w_out], [(0, 0, 0)], 1, "nn", rows, D_MODEL, D_MODEL,
                        512, D_MODEL, D_MODEL, [f32, bf16], _residual_then_norm(1.0), extras=[x], vecs=[next_nw])
    return x_out, hb_out, (x, hb, proj, (out_a, out_b, out_c), merged, sv_a, sv_b, sv_c)


def _mixer_bwd(tag, saved, nw, w_in, bp, w_out, cst, dx, dxb, rows):
    x, hb, proj, outs, merged, sv_a, sv_b, sv_c = saved
    d_wout = _mm1("mix_dwout_" + tag, merged, dxb, "tn", D_MODEL, D_MODEL, rows, D_MODEL, D_MODEL, TOKEN_K,
                  out_dtype=bf16)
    dmerged = _mm1("mix_dmerged_" + tag, dxb, w_out, "nt", rows, D_MODEL, D_MODEL, 512, D_MODEL, D_MODEL)
    d_a, d_b, d_c, dgm, d_bp = _merge_bwd(tag, proj, outs, bp, dmerged, rows)
    dxbc, dgatec, dcst_c = _rg_bwd(tag, sv_c, proj, cst, d_c, rows)
    dq, dz, dv, dg, dcst_b = _hg_bwd(tag, sv_b, proj, cst, d_b, rows)
    du, dcst_a, d_glu_w = _s5_bwd(tag, sv_a, proj, cst, d_a, rows)
    dproj = jnp.concatenate([dgm, du, dq, dz, dv, dg, dxbc, dgatec], axis=1)
    d_win = _mm1("mix_dwin_" + tag, hb, dproj, "tn", D_MODEL, IN_TOTAL, rows, D_MODEL, IN_TOTAL // 4, TOKEN_K,
                 out_dtype=bf16)
    dx_in, dxb_in, d_nw = _mm("mix_dh_" + tag, [dproj], [w_in], [(0, 0, 0)], 1, "nt", rows, D_MODEL, IN_TOTAL,
                              512, D_MODEL, IN_TOTAL // 2, [f32, bf16], _norm_bwd_epilogue, extras=[x, dx], vecs=[nw],
                              n_part=1)
    dcst = {**dcst_a, **dcst_b, **dcst_c}
    return dx_in, dxb_in, jnp.sum(d_nw, axis=0), d_win, d_bp, d_wout, d_glu_w, dcst


def _local_step(x, target, big, small):
    rows = x.shape[0]
    consts, consts_vjp = jax.vjp(_prep_consts, small)
    norm_w = small["norm_w"]
    saved = []
    h = x
    hb = _rms_fwd("first_norm", x, norm_w[0, 0][None], rows)
    for l in range(DEPTH):
        t = str(l)
        cst = dict(consts[l])
        cst["glu_w"] = big["glu_w"][l]
        after = [norm_w[l + 1, 0][None]] if l + 1 < DEPTH else []
        h, hb, sv0 = _ffn_fwd(t + "a", h, hb, big["gate"][l, 0], big["up"][l, 0], big["down"][l, 0],
                              [norm_w[l, 1][None]], rows)
        h, hb, sv1 = _mixer_fwd(t, h, hb, big["w_in"][l], big["bp"][l], big["w_out"][l], cst, norm_w[l, 2][None], rows)
        h, hb, sv2 = _ffn_fwd(t + "b", h, hb, big["gate"][l, 1], big["up"][l, 1], big["down"][l, 1], after, rows)
        saved.append((sv0, sv1, sv2, cst))
    dx, dxb, loss, d_fnw = _loss_head(h, small["final_norm_w"][None], target, rows)
    g_big = {k: [None] * DEPTH for k in ("gate", "up", "down", "w_in", "bp", "w_out", "glu_w")}
    d_norm = [None] * DEPTH
    d_consts = [None] * DEPTH
    for l in reversed(range(DEPTH)):
        t = str(l)
        sv0, sv1, sv2, cst = saved[l]
        dx, dxb, dn2, dg1, du1, dd1 = _ffn_bwd(t + "b", sv2, norm_w[l, 2][None], big["gate"][l, 1], big["up"][l, 1],
                                               big["down"][l, 1], dx, dxb, rows)
        dx, dxb, dn1, d_win, d_bp, d_wout, d_glu_w, dcst = _mixer_bwd(
            t, sv1, norm_w[l, 1][None], big["w_in"][l], big["bp"][l], big["w_out"][l], cst, dx, dxb, rows)
        dx, dxb, dn0, dg0, du0, dd0 = _ffn_bwd(t + "a", sv0, norm_w[l, 0][None], big["gate"][l, 0], big["up"][l, 0],
                                               big["down"][l, 0], dx, dxb, rows)
        g_big["gate"][l] = jnp.stack([dg0, dg1])
        g_big["up"][l] = jnp.stack([du0, du1])
        g_big["down"][l] = jnp.stack([dd0, dd1])
        g_big["w_in"][l] = d_win
        g_big["bp"][l] = d_bp
        g_big["w_out"][l] = d_wout
        g_big["glu_w"][l] = d_glu_w
        d_norm[l] = jnp.concatenate([dn0, dn1, dn2], axis=0)
        d_consts[l] = dcst
    g_big = {k: jnp.stack(v) for k, v in g_big.items()}
    (g_small,) = consts_vjp(d_consts)
    g_small = dict(g_small)
    g_small["norm_w"] = g_small["norm_w"] + jnp.stack(d_norm)
    g_small["final_norm_w"] = g_small["final_norm_w"] + d_fnw[0]
    return loss[0, 0], dx, g_big, g_small


MESH_IDS = pl.DeviceIdType.MESH
ANY_SPEC = pl.BlockSpec(memory_space=pl.ANY)


def _place():
    return lax.axis_index("x"), lax.axis_index("y"), lax.axis_index("c")


def _all_gather(name, shards):
    n = len(shards)

    def body(*refs):
        x_refs, out_refs = refs[:n], refs[n:2 * n]
        send_sems, recv_sems, local_sems = refs[2 * n:]
        x, y, c = _place()
        me, sibling = (x, y, c), (x, y, 1 - c)
        chips = [(1 - x, y), (x, 1 - y), (1 - x, 1 - y)]

        def blk(i, px, py, pc):
            return out_refs[i].at[4 * px + 2 * py + pc]

        def copy(i, k, block, to, src=None):
            return pltpu.make_async_remote_copy(
                src_ref=blk(i, *block) if src is None else src, dst_ref=blk(i, *block),
                send_sem=send_sems.at[7 * i + k], recv_sem=recv_sems.at[7 * i + k], device_id=to,
                device_id_type=MESH_IDS)

        mine = [pltpu.make_async_copy(x_refs[i], blk(i, *me), local_sems.at[i]) for i in range(n)]
        for cp in mine:
            cp.start()
        first = []
        for i in range(n):
            first.append(copy(i, 0, me, sibling, src=x_refs[i]))
            first += [copy(i, 1 + j, me, (*chip, c), src=x_refs[i]) for j, chip in enumerate(chips)]
        for cp in first:
            cp.start()
        passed = []
        for j, chip in enumerate(chips):
            for i in range(n):
                copy(i, 1 + j, (*chip, c), me).wait_recv()
                fwd = copy(i, 4 + j, (*chip, c), sibling)
                fwd.start()
                passed.append(fwd)
        for i in range(n):
            copy(i, 0, sibling, me).wait_recv()
            for j, chip in enumerate(chips):
                copy(i, 4 + j, (*chip, 1 - c), me).wait_recv()
        for cp in first + passed:
            cp.wait_send()
        for cp in mine:
            cp.wait()

    return pl.pallas_call(
        body, name=name, out_shape=[jax.ShapeDtypeStruct((N_DEV,) + s.shape, s.dtype) for s in shards],
        in_specs=[ANY_SPEC] * n, out_specs=[ANY_SPEC] * n,
        scratch_shapes=[pltpu.SemaphoreType.DMA((7 * n,)), pltpu.SemaphoreType.DMA((7 * n,)),
                        pltpu.SemaphoreType.DMA((n,))],
    )(*shards)


def _row_tile(rows):
    return rows if rows <= 512 else next(t for t in range(512, 7, -8) if rows % t == 0)


def _reduce_scatter(parts):
    n = len(parts)
    _, _, c = _place()

    def body_pair(*refs):
        p_refs, got_refs = refs[:n], refs[n:2 * n]
        send_sems, recv_sems = refs[2 * n:]
        x, y, c = _place()
        cps = [pltpu.make_async_remote_copy(
            src_ref=p_refs[i].at[1 - c], dst_ref=got_refs[i], send_sem=send_sems.at[i], recv_sem=recv_sems.at[i],
            device_id=(x, y, 1 - c), device_id_type=MESH_IDS) for i in range(n)]
        for cp in cps:
            cp.start()
        for cp in cps:
            cp.wait()

    from_sibling = pl.pallas_call(
        body_pair, name="rs_pair", out_shape=[jax.ShapeDtypeStruct(p.shape[1:], p.dtype) for p in parts],
        in_specs=[ANY_SPEC] * n, out_specs=[ANY_SPEC] * n,
        scratch_shapes=[pltpu.SemaphoreType.DMA((n,)), pltpu.SemaphoreType.DMA((n,))],
    )(*parts)

    chip_sums = []
    for i, (part, got) in enumerate(zip(parts, from_sibling)):
        _, _, r, cols = part.shape
        tr = _row_tile(r)

        def body_add(idx_ref, p_ref, g_ref, o_ref):
            o_ref[...] = (p_ref[...].astype(f32) + g_ref[...].astype(f32)).astype(o_ref.dtype)

        chip_sums.append(pl.pallas_call(
            body_add, name="rs_pair_sum_%d" % i, out_shape=jax.ShapeDtypeStruct((4, r, cols), part.dtype),
            grid_spec=pltpu.PrefetchScalarGridSpec(
                num_scalar_prefetch=1, grid=(4, r // tr),
                in_specs=[pl.BlockSpec((None, None, tr, cols), lambda j, t, idx: (idx[0], j, t, 0)),
                          pl.BlockSpec((None, tr, cols), lambda j, t, idx: (j, t, 0))],
                out_specs=pl.BlockSpec((None, tr, cols), lambda j, t, idx: (j, t, 0))),
            compiler_params=_cparams(("parallel", "parallel")),
        )(jnp.stack([c]).astype(jnp.int32), part, got))

    def body_chips(*refs):
        t_refs, got_refs = refs[:n], refs[n:2 * n]
        send_sems, recv_sems = refs[2 * n:]
        x, y, c = _place()
        chips = [(1 - x, y), (x, 1 - y), (1 - x, 1 - y)]
        cps = [pltpu.make_async_remote_copy(
            src_ref=t_refs[i].at[2 * px + py], dst_ref=got_refs[i].at[k], send_sem=send_sems.at[3 * i + k],
            recv_sem=recv_sems.at[3 * i + k], device_id=(px, py, c), device_id_type=MESH_IDS)
            for i in range(n) for k, (px, py) in enumerate(chips)]
        for cp in cps:
            cp.start()
        for cp in cps:
            cp.wait()

    from_chips = pl.pallas_call(
        body_chips, name="rs_chips", out_shape=[jax.ShapeDtypeStruct((3,) + p.shape[2:], p.dtype) for p in parts],
        in_specs=[ANY_SPEC] * n, out_specs=[ANY_SPEC] * n,
        scratch_shapes=[pltpu.SemaphoreType.DMA((3 * n,)), pltpu.SemaphoreType.DMA((3 * n,))],
    )(*chip_sums)
    return list(zip(chip_sums, from_chips))


def _own_index():
    x, y, _ = _place()
    return jnp.stack([2 * x + y]).astype(jnp.int32)


def _own_total(name, chip_sum, others):
    _, r, cols = chip_sum.shape
    tr = _row_tile(r)

    def body(idx_ref, t_ref, g_ref, o_ref):
        o_ref[...] = ((t_ref[...].astype(f32) + g_ref[0].astype(f32)) + g_ref[1].astype(f32)) + g_ref[2].astype(f32)

    return pl.pallas_call(
        body, name=name, out_shape=jax.ShapeDtypeStruct((r, cols), f32),
        grid_spec=pltpu.PrefetchScalarGridSpec(
            num_scalar_prefetch=1, grid=(r // tr,),
            in_specs=[pl.BlockSpec((None, tr, cols), lambda t, idx: (idx[0], t, 0)),
                      pl.BlockSpec((3, tr, cols), lambda t, idx: (0, t, 0))],
            out_specs=pl.BlockSpec((tr, cols), lambda t, idx: (t, 0))),
        compiler_params=_cparams(("parallel",)),
    )(_own_index(), chip_sum, others)


def _adam_update(w, gv, m, v):
    m_new = ADAM_B1 * m + (1.0 - ADAM_B1) * gv
    v_new = ADAM_B2 * v + (1.0 - ADAM_B2) * (gv * gv)
    m_hat = m_new / (1.0 - ADAM_B1 ** ADAM_STEP)
    v_hat = v_new / (1.0 - ADAM_B2 ** ADAM_STEP)
    return -ADAM_LR * (m_hat / (jnp.sqrt(v_hat) + ADAM_EPS) + ADAM_WD * w), m_new, v_new


def _adamw_reduced(name, w, chip_sum, others, m, v):
    rows, cols = w.shape
    tr = _row_tile(rows)

    def body(idx_ref, w_ref, t_ref, o_ref, m_ref, v_ref, g_ref, d_ref, nm_ref, nv_ref):
        gv = ((t_ref[...].astype(f32) + o_ref[0].astype(f32)) + o_ref[1].astype(f32)) + o_ref[2].astype(f32)
        g_ref[...] = gv
        d_ref[...], nm_ref[...], nv_ref[...] = _adam_update(w_ref[...], gv, m_ref[...], v_ref[...])

    spec = pl.BlockSpec((tr, cols), lambda t, idx: (t, 0))
    return pl.pallas_call(
        body, name=name, out_shape=[jax.ShapeDtypeStruct((rows, cols), f32)] * 4,
        grid_spec=pltpu.PrefetchScalarGridSpec(
            num_scalar_prefetch=1, grid=(rows // tr,),
            in_specs=[spec, pl.BlockSpec((None, tr, cols), lambda t, idx: (idx[0], t, 0)),
                      pl.BlockSpec((3, tr, cols), lambda t, idx: (0, t, 0)), spec, spec],
            out_specs=[spec] * 4),
        compiler_params=_cparams(("parallel",)),
    )(_own_index(), w, chip_sum, others, m, v)


def _adamw(name, w, g, m, v):
    rows, cols = w.shape
    tr = _row_tile(rows)

    def body(w_ref, g_ref, m_ref, v_ref, d_ref, nm_ref, nv_ref):
        d_ref[...], nm_ref[...], nv_ref[...] = _adam_update(w_ref[...], g_ref[...], m_ref[...], v_ref[...])

    spec = pl.BlockSpec((tr, cols), lambda i: (i, 0))
    return pl.pallas_call(
        body, name=name, grid=(rows // tr,), in_specs=[spec] * 4, out_specs=[spec] * 3,
        out_shape=[jax.ShapeDtypeStruct((rows, cols), f32)] * 3, compiler_params=_cparams(("parallel",)),
    )(w, g, m, v)


WEIGHT_NAMES = ["norm_w", "final_norm_w", "ffn_gate", "ffn_up", "ffn_down", "w_in", "branch_proj", "w_out",
                "s5_lambda_re", "s5_lambda_im", "s5_log_dt", "s5_b_re", "s5_b_im", "s5_c_re", "s5_c_im", "s5_d",
                "s5_glu_w", "s5_glu_b", "hg_lb_logits", "hg_norm_w", "rg_conv_w", "rg_conv_b", "rg_wa", "rg_ba",
                "rg_wx", "rg_bx", "rg_lambda"]
SHARDED = {"ffn_gate": (3, "gate"), "ffn_up": (3, "up"), "ffn_down": (2, "down"), "w_in": (2, "w_in"),
           "branch_proj": (3, "bp"), "w_out": (1, "w_out"), "s5_glu_w": (1, "glu_w"),
           "norm_w": (2, None), "rg_conv_w": (2, None)}
BIG = ["ffn_gate", "ffn_up", "ffn_down", "w_in", "branch_proj", "w_out", "s5_glu_w"]
SMALL_SHARDED = ["norm_w", "rg_conv_w"]
REPLICATED = [n for n in WEIGHT_NAMES if n not in SHARDED]
LANES = 128


PACK_ROWS = 512


def _pack_rows(arrays, names):
    pieces = []
    for n in names:
        flat = arrays[n].reshape(-1)
        pieces.append(jnp.pad(flat, (0, -flat.shape[0] % LANES)).reshape(-1, LANES))
    rows = jnp.concatenate(pieces, axis=0)
    return jnp.pad(rows, ((0, -rows.shape[0] % PACK_ROWS), (0, 0)))


def _unpack_rows(rows, names, like):
    out, r0 = {}, 0
    for n in names:
        size = math.prod(like[n].shape)
        nrows = -(-size // LANES)
        out[n] = rows[r0:r0 + nrows].reshape(-1)[:size].reshape(like[n].shape)
        r0 += nrows
    return out


def _unshard(gathered, axis):
    g = jnp.moveaxis(gathered, 0, axis)
    shp = g.shape
    return g.reshape(shp[:axis] + (shp[axis] * shp[axis + 1],) + shp[axis + 2:])


def _to_blocks(full, axis):
    shp = full.shape
    g = full.reshape(shp[:axis] + (4, 2, shp[axis] // N_DEV) + shp[axis + 1:])
    g = jnp.moveaxis(g, (axis, axis + 1), (1, 0))
    return g.reshape(2, 4, -1, g.shape[-1])


W_IN_SPLIT = IN_TOTAL - GM_WIDTH


def kernel(x, norm_w, final_norm_w, ffn_gate, ffn_up, ffn_down, w_in, branch_proj, w_out, s5_lambda_re, s5_lambda_im, s5_log_dt, s5_b_re, s5_b_im, s5_c_re, s5_c_im, s5_d, s5_glu_w, s5_glu_b, hg_lb_logits, hg_norm_w, rg_conv_w, rg_conv_b, rg_wa, rg_ba, rg_wx, rg_bx, rg_lambda, loss_target, m_norm_w, m_final_norm_w, m_ffn_gate, m_ffn_up, m_ffn_down, m_w_in, m_branch_proj, m_w_out, m_s5_lambda_re, m_s5_lambda_im, m_s5_log_dt, m_s5_b_re, m_s5_b_im, m_s5_c_re, m_s5_c_im, m_s5_d, m_s5_glu_w, m_s5_glu_b, m_hg_lb_logits, m_hg_norm_w, m_rg_conv_w, m_rg_conv_b, m_rg_wa, m_rg_ba, m_rg_wx, m_rg_bx, m_rg_lambda, v_norm_w, v_final_norm_w, v_ffn_gate, v_ffn_up, v_ffn_down, v_w_in, v_branch_proj, v_w_out, v_s5_lambda_re, v_s5_lambda_im, v_s5_log_dt, v_s5_b_re, v_s5_b_im, v_s5_c_re, v_s5_c_im, v_s5_d, v_s5_glu_w, v_s5_glu_b, v_hg_lb_logits, v_hg_norm_w, v_rg_conv_w, v_rg_conv_b, v_rg_wa, v_rg_ba, v_rg_wx, v_rg_bx, v_rg_lambda):
    w = dict(zip(WEIGHT_NAMES, (norm_w, final_norm_w, ffn_gate, ffn_up, ffn_down, w_in, branch_proj, w_out,
                                s5_lambda_re, s5_lambda_im, s5_log_dt, s5_b_re, s5_b_im, s5_c_re, s5_c_im, s5_d,
                                s5_glu_w, s5_glu_b, hg_lb_logits, hg_norm_w, rg_conv_w, rg_conv_b, rg_wa, rg_ba,
                                rg_wx, rg_bx, rg_lambda)))
    m = dict(zip(WEIGHT_NAMES, (m_norm_w, m_final_norm_w, m_ffn_gate, m_ffn_up, m_ffn_down, m_w_in, m_branch_proj,
                                m_w_out, m_s5_lambda_re, m_s5_lambda_im, m_s5_log_dt, m_s5_b_re, m_s5_b_im, m_s5_c_re,
                                m_s5_c_im, m_s5_d, m_s5_glu_w, m_s5_glu_b, m_hg_lb_logits, m_hg_norm_w, m_rg_conv_w,
                                m_rg_conv_b, m_rg_wa, m_rg_ba, m_rg_wx, m_rg_bx, m_rg_lambda)))
    v = dict(zip(WEIGHT_NAMES, (v_norm_w, v_final_norm_w, v_ffn_gate, v_ffn_up, v_ffn_down, v_w_in, v_branch_proj,
                                v_w_out, v_s5_lambda_re, v_s5_lambda_im, v_s5_log_dt, v_s5_b_re, v_s5_b_im, v_s5_c_re,
                                v_s5_c_im, v_s5_d, v_s5_glu_w, v_s5_glu_b, v_hg_lb_logits, v_hg_norm_w, v_rg_conv_w,
                                v_rg_conv_b, v_rg_wa, v_rg_ba, v_rg_wx, v_rg_bx, v_rg_lambda)))
    rows = x.shape[1]

    sharded = BIG + SMALL_SHARDED
    gathered = _all_gather("gather_weights", [w[n].astype(bf16) for n in BIG] + [w[n] for n in SMALL_SHARDED])
    full = {n: _unshard(g, SHARDED[n][0]) for n, g in zip(sharded, gathered)}
    big = {SHARDED[n][1]: full[n] for n in BIG}
    big["w_in"] = jnp.concatenate([big["w_in"][..., W_IN_SPLIT:], big["w_in"][..., :W_IN_SPLIT]], axis=-1)
    small = {n: w[n] for n in REPLICATED}
    small["norm_w"] = full["norm_w"]
    small["rg_conv_w"] = full["rg_conv_w"]

    loss_part, dx, g_big, g_small = _local_step(x[0], loss_target[0], big, small)
    g_big["w_in"] = jnp.concatenate([g_big["w_in"][..., GM_WIDTH:], g_big["w_in"][..., :GM_WIDTH]], axis=-1)
    loss = lax.psum(loss_part, ("x", "y", "c"))

    parts = [_to_blocks(g_big[SHARDED[n][1]], SHARDED[n][0]).astype(bf16) for n in BIG]
    parts += [_to_blocks(g_small[n], SHARDED[n][0]) for n in SMALL_SHARDED]
    rep_rows = _pack_rows(g_small, REPLICATED)
    rep_slice = rep_rows.shape[0] // N_DEV
    parts.append(rep_rows.reshape(4, 2, rep_slice, LANES).transpose(1, 0, 2, 3))
    sums = _reduce_scatter(parts)

    grads, delta, new_m, new_v = {}, {}, {}, {}
    for n, (chip_sum, others) in zip(sharded, sums):
        shp = w[n].shape
        view = (-1, shp[-1])
        res = _adamw_reduced("adamw_" + n, w[n].reshape(view), chip_sum, others, m[n].reshape(view), v[n].reshape(view))
        grads[n], delta[n], new_m[n], new_v[n] = (r.reshape(shp) for r in res)
    rep_mine = _own_total("rs_total_small", *sums[-1])
    rep_grads = _all_gather("gather_small_grads", [rep_mine])[0].reshape(-1, LANES)
    res = _adamw("adamw_small", _pack_rows(w, REPLICATED), rep_grads, _pack_rows(m, REPLICATED), _pack_rows(v, REPLICATED))
    for dst, src in zip((grads, delta, new_m, new_v), (rep_grads,) + tuple(res)):
        dst.update(_unpack_rows(src, REPLICATED, w))

    return (loss, dx.reshape(x.shape), *[grads[n] for n in WEIGHT_NAMES], *[delta[n] for n in WEIGHT_NAMES],
            *[new_m[n] for n in WEIGHT_NAMES], *[new_v[n] for n in WEIGHT_NAMES])
```

```python
import functools
import math

import jax
import jax.numpy as jnp
from jax import lax
from jax.experimental import pallas as pl
from jax.experimental.pallas import tpu as pltpu

f32 = jnp.float32
bf16 = jnp.bfloat16

D_MODEL = 1024
DEPTH = 2
BRANCH = 512
N_BRANCH = 3
S5_GROUP = 16
S5_GROUPS = 32
S5_STATE = 64
S5_LANES = S5_GROUPS * S5_STATE
S5_EIG_MAX = -1e-4
HG_HEADS = 4
HG_DK = 128
HG_CHUNK = 32
RG_BLOCKS = 8
RG_BLOCK = 64
RG_C = 8.0
D_FF = 2816
EPS = 1e-6
IN_TOTAL = 6656
GM_WIDTH = N_BRANCH * D_MODEL
N_DEV = 8

ADAM_LR = 0.001
ADAM_B1 = 0.9
ADAM_B2 = 0.999
ADAM_EPS = 1e-08
ADAM_WD = 0.01
ADAM_STEP = 10

VMEM_LIMIT_V7X = 56 * 1024 * 1024
ROW_TILE = 256
FF_TILE = 1408
TOKEN_K = 2048
MXU_COLS = 256


def _cparams(sem):
    return pltpu.CompilerParams(dimension_semantics=sem, vmem_limit_bytes=VMEM_LIMIT_V7X)


def _sigmoid(x):
    return 0.5 * jnp.tanh(0.5 * x) + 0.5


def _sigmoid_small(x):
    return 1.0 / (1.0 + jnp.exp(-x))


_GELU_C = math.sqrt(2.0 / math.pi)


def _gelu(x):
    t = jnp.tanh(_GELU_C * (x + 0.044715 * x * x * x))
    return 0.5 * x * (1.0 + t)


def _gelu_grad(x):
    t = jnp.tanh(_GELU_C * (x + 0.044715 * x * x * x))
    return 0.5 * (1.0 + t) + 0.5 * x * (1.0 - t * t) * _GELU_C * (1.0 + 3.0 * 0.044715 * x * x)


def _expm1(x):
    p = x * (1.0 + x * (0.5 + x * (1.0 / 6 + x * (1.0 / 24 + x * (1.0 / 120 + x * (1.0 / 720))))))
    return jnp.where(jnp.abs(x) < 0.3, p, jnp.exp(x) - 1.0)


def _dot(a, b):
    return jnp.dot(a, b, preferred_element_type=f32)


def _dot_nt(a, b):
    return lax.dot_general(a, b, (((1,), (1,)), ((), ())), preferred_element_type=f32)


def _dot_tn(a, b):
    return lax.dot_general(a, b, (((0,), (0,)), ((), ())), preferred_element_type=f32)


def _bdot(a, b):
    return lax.dot_general(a, b, (((2,), (1,)), ((0,), (0,))), preferred_element_type=f32)


def _bdot_nt(a, b):
    return lax.dot_general(a, b, (((2,), (2,)), ((0,), (0,))), preferred_element_type=f32)


def _rows(shape):
    return lax.broadcasted_iota(jnp.int32, shape, 0)


def _scan_fwd(a, b, n):
    row = _rows(a.shape)
    s = 1
    while s < n:
        valid = row >= s
        sh_a = pltpu.roll(a, s, 0)
        sh_b = pltpu.roll(b, s, 0)
        b = b + a * jnp.where(valid, sh_b, 0.0)
        a = a * jnp.where(valid, sh_a, 1.0)
        s *= 2
    return a, b


def _scan_bwd(a, b, n):
    row = _rows(a.shape)
    s = 1
    while s < n:
        valid = row < n - s
        sh_a = pltpu.roll(a, n - s, 0)
        sh_b = pltpu.roll(b, n - s, 0)
        b = b + a * jnp.where(valid, sh_b, 0.0)
        a = a * jnp.where(valid, sh_a, 1.0)
        s *= 2
    return a, b


def _seg_cumsum(x, n, seg):
    pos = _rows(x.shape) % seg
    s = 1
    while s < seg:
        x = x + jnp.where(pos >= s, pltpu.roll(x, s, 0), 0.0)
        s *= 2
    return x


def _seg_rev_cumsum(x, n, seg):
    pos = _rows(x.shape) % seg
    s = 1
    while s < seg:
        x = x + jnp.where(pos < seg - s, pltpu.roll(x, n - s, 0), 0.0)
        s *= 2
    return x


def _head_mean(x):
    parts = []
    for h in range(HG_HEADS):
        m = jnp.mean(x[:, h * HG_DK:(h + 1) * HG_DK], axis=1, keepdims=True)
        parts.append(jnp.broadcast_to(m, (x.shape[0], HG_DK)))
    return jnp.concatenate(parts, axis=1)


def _mm(name, a_list, b_list, terms, n_acc, mode, m, n, k, tm, tn, tk, out_dtypes, epilogue, extras=(), vecs=(),
        n_part=0, chunk=0):
    tm, tn, tk = min(tm, m), min(tn, n), min(tk, k)
    assert m % tm == 0 and n % tn == 0 and k % tk == 0, (name, m, n, k, tm, tn, tk)
    gk = k // tk
    if mode == "tn":
        a_spec = pl.BlockSpec((tk, tm), lambda i, j, kk: (kk, i))
    else:
        a_spec = pl.BlockSpec((tm, tk), lambda i, j, kk: (i, kk))
    if mode == "nt":
        b_spec = pl.BlockSpec((tn, tk), lambda i, j, kk: (j, kk))
    else:
        b_spec = pl.BlockSpec((tk, tn), lambda i, j, kk: (kk, j))
    o_spec = pl.BlockSpec((tm, tn), lambda i, j, kk: (i, j))
    v_spec = pl.BlockSpec((1, tn), lambda i, j, kk: (0, j))
    p_spec = pl.BlockSpec((None, 1, tn), lambda i, j, kk: (i, 0, j))
    dot = {"nn": _dot, "nt": _dot_nt, "tn": _dot_tn}[mode]
    na, nb, ne, nv, no = len(a_list), len(b_list), len(extras), len(vecs), len(out_dtypes)

    def kern(*refs):
        a_refs = refs[:na]
        b_refs = refs[na:na + nb]
        e_refs = refs[na + nb:na + nb + ne]
        v_refs = refs[na + nb + ne:na + nb + ne + nv]
        o_refs = refs[na + nb + ne + nv:na + nb + ne + nv + no + n_part]

        def finish(accs):
            outs = epilogue(accs, [e[...] for e in e_refs], [r[...] for r in v_refs])
            for o, val in zip(o_refs, outs):
                o[...] = val.astype(o.dtype)

        def partial_sums():
            sums = [None] * n_acc
            for ai, bi, ci in terms:
                d = dot(a_refs[ai][...].astype(bf16), b_refs[bi][...].astype(bf16))
                sums[ci] = d if sums[ci] is None else sums[ci] + d
            return sums

        if gk == 1 and chunk:
            assert mode in ("nn", "nt") and tn % chunk == 0
            for c0 in range(0, tn, chunk):
                cols = slice(c0, c0 + chunk)
                sums = [None] * n_acc
                for ai, bi, ci in terms:
                    b_part = b_refs[bi][:, cols] if mode == "nn" else b_refs[bi][cols, :]
                    d = dot(a_refs[ai][...].astype(bf16), b_part.astype(bf16))
                    sums[ci] = d if sums[ci] is None else sums[ci] + d
                outs = epilogue(sums, [e[:, cols] for e in e_refs], [r[:, cols] for r in v_refs])
                for o, val in zip(o_refs, outs):
                    o[:, cols] = val.astype(o.dtype)
            return
        if gk == 1:
            finish(partial_sums())
            return
        acc = refs[na + nb + ne + nv + no + n_part]
        kk = pl.program_id(2)

        @pl.when(kk == 0)
        def _():
            acc[...] = jnp.zeros_like(acc)

        for ci, d in enumerate(partial_sums()):
            acc[ci] += d

        @pl.when(kk == gk - 1)
        def _():
            finish([acc[c] for c in range(n_acc)])

    return pl.pallas_call(
        kern, name=name,
        grid=(m // tm, n // tn, gk),
        in_specs=[a_spec] * na + [b_spec] * nb + [o_spec] * ne + [v_spec] * nv,
        out_specs=[o_spec] * no + [p_spec] * n_part,
        out_shape=([jax.ShapeDtypeStruct((m, n), dt) for dt in out_dtypes]
                   + [jax.ShapeDtypeStruct((m // tm, 1, n), f32)] * n_part),
        scratch_shapes=[pltpu.VMEM((n_acc, tm, tn), f32)] if gk > 1 else [],
        compiler_params=_cparams(("parallel", "parallel", "arbitrary")),
    )(*a_list, *b_list, *extras, *vecs)


def _mm1(name, a, b, mode, m, n, k, tm, tn, tk, out_dtype=f32, scale=None):
    def epi(accs, extras, vecs):
        return [accs[0] if scale is None else accs[0] * scale]
    return _mm(name, [a], [b], [(0, 0, 0)], 1, mode, m, n, k, tm, tn, tk, [out_dtype], epi)[0]


def _rt(name, body, rows, tm, row_ins, consts, row_outs, acc_outs=(), scratch=(), reverse=False):
    tm = min(tm, rows)
    assert rows % tm == 0
    nt = rows // tm

    def tile(i):
        return nt - 1 - i if reverse else i

    in_specs, args = [], []
    for spec in row_ins:
        arr = spec[0]
        if isinstance(spec[1], int):
            in_specs.append(pl.BlockSpec((tm, spec[1]), lambda i, cb=spec[2]: (tile(i), cb)))
        else:
            in_specs.append(pl.BlockSpec(spec[1], lambda i, fn=spec[2]: fn(tile(i))))
        args.append(arr)
    for c in consts:
        in_specs.append(pl.BlockSpec(c.shape, lambda i, nd=c.ndim: (0,) * nd))
        args.append(c)
    out_specs, out_shape = [], []
    for spec in row_outs:
        if isinstance(spec[0], int):
            out_specs.append(pl.BlockSpec((tm, spec[0]), lambda i: (tile(i), 0)))
            out_shape.append(jax.ShapeDtypeStruct((rows, spec[0]), spec[1]))
        else:
            out_specs.append(pl.BlockSpec(spec[1], lambda i, fn=spec[2]: fn(tile(i))))
            out_shape.append(jax.ShapeDtypeStruct(spec[0], spec[3]))
    for shp in acc_outs:
        out_specs.append(pl.BlockSpec(shp, lambda i, nd=len(shp): (0,) * nd))
        out_shape.append(jax.ShapeDtypeStruct(shp, f32))
    n_in = len(args)
    n_row_out = len(row_outs)
    n_acc = len(acc_outs)

    def kern(*refs):
        i = pl.program_id(0)
        acc_refs = refs[n_in + n_row_out:n_in + n_row_out + n_acc]

        @pl.when(i == 0)
        def _():
            for r in acc_refs:
                r[...] = jnp.zeros_like(r)

        body(i, *refs)

    return pl.pallas_call(
        kern, name=name, grid=(nt,), in_specs=in_specs, out_specs=out_specs, out_shape=out_shape,
        scratch_shapes=list(scratch), compiler_params=_cparams(("arbitrary",)),
    )(*args)


def _rms_rows(xv, wv):
    r = lax.rsqrt(jnp.mean(xv * xv, axis=1, keepdims=True) + EPS)
    return (xv * r * wv).astype(bf16)


def _rms_bwd_rows(xv, dhv, wv, dres):
    r = lax.rsqrt(jnp.mean(xv * xv, axis=1, keepdims=True) + EPS)
    xn = xv * r
    dxn = dhv * wv
    dx = dres + r * (dxn - xn * jnp.mean(dxn * xn, axis=1, keepdims=True))
    return [dx, dx.astype(bf16), jnp.sum(dhv * xn, axis=0, keepdims=True)]


def _rms_fwd(name, x, w, rows):
    def body(i, x_ref, w_ref, h_ref):
        h_ref[...] = _rms_rows(x_ref[...], w_ref[...])
    return _rt(name, body, rows, ROW_TILE, [(x, D_MODEL, 0)], [w], [(D_MODEL, bf16)])[0]


def _residual_then_norm(scale):
    def epi(accs, extras, vecs):
        x_out = extras[0] + scale * accs[0]
        return [x_out] + [_rms_rows(x_out, v) for v in vecs]
    return epi


def _norm_bwd_epilogue(accs, extras, vecs):
    return _rms_bwd_rows(extras[0], accs[0], vecs[0], extras[1])


def _loss_head(x, w, target, rows):
    def body(i, x_ref, t_ref, w_ref, dx_ref, dxb_ref, loss_ref, dw_ref):
        xv = x_ref[...]
        r = lax.rsqrt(jnp.mean(xv * xv, axis=1, keepdims=True) + EPS)
        xn = xv * r
        wv = w_ref[...]
        err = xn * wv - t_ref[...]
        part = 0.5 * jnp.sum(jnp.mean(err * err, axis=1, keepdims=True), axis=0, keepdims=True)
        loss_ref[...] += jnp.broadcast_to(part, (1, 128))
        dy = err * (1.0 / D_MODEL)
        dxn = dy * wv
        dx = r * (dxn - xn * jnp.mean(dxn * xn, axis=1, keepdims=True))
        dx_ref[...] = dx
        dxb_ref[...] = dx.astype(bf16)
        dw_ref[...] += jnp.sum(dy * xn, axis=0, keepdims=True)
    return _rt("loss_head", body, rows, ROW_TILE, [(x, D_MODEL, 0), (target, D_MODEL, 0)], [w],
               [(D_MODEL, f32), (D_MODEL, bf16)], acc_outs=[(1, 128), (1, D_MODEL)])


def _ffn_fwd(tag, x, hb, wg, wu, wd, next_nw, rows):
    def epi_up(accs, extras, vecs):
        a, b = accs
        return [a, b, a * _sigmoid(a) * b]
    a, b, s = _mm("ffn_up_" + tag, [hb], [wg, wu], [(0, 0, 0), (0, 1, 1)], 2, "nn", rows, D_FF, D_MODEL,
                  512, D_FF, D_MODEL, [bf16, bf16, bf16], epi_up, chunk=MXU_COLS)
    outs = _mm("ffn_down_" + tag, [s], [wd], [(0, 0, 0)], 1, "nn", rows, D_MODEL, D_FF,
               512, D_MODEL, D_FF, [f32] + [bf16] * len(next_nw), _residual_then_norm(0.5), extras=[x],
               vecs=next_nw)
    return outs[0], (outs[1] if next_nw else None), (x, hb, a, b, s)


def _ffn_bwd(tag, saved, nw, wg, wu, wd, dx, dxb, rows):
    x, hb, a, b, s = saved

    def epi_mid(accs, extras, vecs):
        ds = 0.5 * accs[0]
        av = extras[0].astype(f32)
        bv = extras[1].astype(f32)
        sg = _sigmoid(av)
        return [ds * bv * sg * (1.0 + av * (1.0 - sg)), ds * av * sg]
    da, db = _mm("ffn_bwd_mid_" + tag, [dxb], [wd], [(0, 0, 0)], 1, "nt", rows, D_FF, D_MODEL,
                 512, D_FF, D_MODEL, [bf16, bf16], epi_mid, extras=[a, b], chunk=MXU_COLS)
    d_wd = _mm1("ffn_dwd_" + tag, s, dxb, "tn", D_FF, D_MODEL, rows, FF_TILE, D_MODEL, TOKEN_K, out_dtype=bf16,
                scale=0.5)
    d_wg = _mm1("ffn_dwg_" + tag, hb, da, "tn", D_MODEL, D_FF, rows, D_MODEL, FF_TILE, TOKEN_K, out_dtype=bf16)
    d_wu = _mm1("ffn_dwu_" + tag, hb, db, "tn", D_MODEL, D_FF, rows, D_MODEL, FF_TILE, TOKEN_K, out_dtype=bf16)
    dx_in, dxb_in, d_nw = _mm("ffn_dh_" + tag, [da, db], [wg, wu], [(0, 0, 0), (1, 1, 0)], 1, "nt", rows, D_MODEL,
                              D_FF, 512, D_MODEL, FF_TILE, [f32, bf16], _norm_bwd_epilogue, extras=[x, dx], vecs=[nw],
                              n_part=1)
    return dx_in, dxb_in, jnp.sum(d_nw, axis=0), d_wg, d_wu, d_wd


S5_CB = 512
SUBLANES = 8
U_COL = GM_WIDTH // BRANCH


def _s5_scan_fwd(tag, proj, b_re, b_im, a_re, a_im, rows):
    tm = min(ROW_TILE, rows)
    nt = rows // tm
    nc = S5_LANES // S5_CB

    def kern(u_ref, bre_ref, bim_ref, ar_ref, ai_ref, xr_ref, xi_ref, pr_s, pi_s, cr_s, ci_s):
        t = pl.program_id(1)

        @pl.when(t == 0)
        def _():
            row8 = _rows((SUBLANES, S5_CB))
            pr = jnp.broadcast_to(ar_ref[...], (SUBLANES, S5_CB))
            pi = jnp.broadcast_to(ai_ref[...], (SUBLANES, S5_CB))
            s = 1
            while s < SUBLANES:
                sr = pltpu.roll(pr, s, 0)
                si = pltpu.roll(pi, s, 0)
                valid = row8 >= s
                pr, pi = jnp.where(valid, pr * sr - pi * si, pr), jnp.where(valid, pr * si + pi * sr, pi)
                s *= 2
            pr_s[...] = pr
            pi_s[...] = pi
            cr_s[...] = jnp.zeros_like(cr_s)
            ci_s[...] = jnp.zeros_like(ci_s)

        ub = u_ref[...].astype(bf16)
        br = _dot(ub, bre_ref[...])
        bi = _dot(ub, bim_ref[...])
        pos = _rows((tm, S5_CB)) % SUBLANES
        s = 1
        while s < SUBLANES:
            mr = pr_s[s - 1:s, :]
            mi = pi_s[s - 1:s, :]
            sr = pltpu.roll(br, s, 0)
            si = pltpu.roll(bi, s, 0)
            valid = pos >= s
            br, bi = (br + jnp.where(valid, mr * sr - mi * si, 0.0),
                      bi + jnp.where(valid, mr * si + mi * sr, 0.0))
            s *= 2
        cr = cr_s[...]
        ci = ci_s[...]
        pr = pr_s[...]
        pi = pi_s[...]
        for g in range(tm // SUBLANES):
            sl = slice(g * SUBLANES, (g + 1) * SUBLANES)
            xr = br[sl] + pr * cr - pi * ci
            xi = bi[sl] + pr * ci + pi * cr
            xr_ref[sl, :] = xr
            xi_ref[sl, :] = xi
            cr = xr[SUBLANES - 1:SUBLANES, :]
            ci = xi[SUBLANES - 1:SUBLANES, :]
        cr_s[...] = cr
        ci_s[...] = ci

    return pl.pallas_call(
        kern, name="s5_scan_fwd_" + tag, grid=(nc, nt),
        in_specs=[pl.BlockSpec((tm, BRANCH), lambda c, t: (t, U_COL)),
                  pl.BlockSpec((BRANCH, S5_CB), lambda c, t: (0, c)),
                  pl.BlockSpec((BRANCH, S5_CB), lambda c, t: (0, c)),
                  pl.BlockSpec((1, S5_CB), lambda c, t: (0, c)),
                  pl.BlockSpec((1, S5_CB), lambda c, t: (0, c))],
        out_specs=[pl.BlockSpec((tm, S5_CB), lambda c, t: (t, c))] * 2,
        out_shape=[jax.ShapeDtypeStruct((rows, S5_LANES), f32)] * 2,
        scratch_shapes=[pltpu.VMEM((SUBLANES, S5_CB), f32), pltpu.VMEM((SUBLANES, S5_CB), f32),
                        pltpu.VMEM((1, S5_CB), f32), pltpu.VMEM((1, S5_CB), f32)],
        compiler_params=_cparams(("parallel", "arbitrary")),
    )(proj, b_re, b_im, a_re, a_im)


def _s5_scan_bwd(tag, dxr, dxi, xr, xi, a_re, a_im, rows):
    tm = min(ROW_TILE, rows)
    nt = rows // tm
    nc = S5_LANES // S5_CB

    def kern(dxr_ref, dxi_ref, xr_ref, xi_ref, ar_ref, ai_ref, gr_ref, gi_ref, dar_ref, dai_ref,
             qr_s, qi_s, cr_s, ci_s, gr_s, gi_s):
        t = pl.program_id(1)
        row = _rows((tm, S5_CB))

        @pl.when(t == 0)
        def _():
            row8 = _rows((SUBLANES, S5_CB))
            qr = jnp.broadcast_to(ar_ref[...], (SUBLANES, S5_CB))
            qi = jnp.broadcast_to(-ai_ref[...], (SUBLANES, S5_CB))
            s = 1
            while s < SUBLANES:
                sr = pltpu.roll(qr, SUBLANES - s, 0)
                si = pltpu.roll(qi, SUBLANES - s, 0)
                valid = row8 < SUBLANES - s
                qr, qi = jnp.where(valid, qr * sr - qi * si, qr), jnp.where(valid, qr * si + qi * sr, qi)
                s *= 2
            qr_s[...] = qr
            qi_s[...] = qi
            cr_s[...] = jnp.zeros_like(cr_s)
            ci_s[...] = jnp.zeros_like(ci_s)
            dar_ref[...] = jnp.zeros_like(dar_ref)
            dai_ref[...] = jnp.zeros_like(dai_ref)

        br = dxr_ref[...]
        bi = dxi_ref[...]
        pos = row % SUBLANES
        s = 1
        while s < SUBLANES:
            mr = qr_s[SUBLANES - s:SUBLANES - s + 1, :]
            mi = qi_s[SUBLANES - s:SUBLANES - s + 1, :]
            sr = pltpu.roll(br, tm - s, 0)
            si = pltpu.roll(bi, tm - s, 0)
            valid = pos < SUBLANES - s
            br, bi = (br + jnp.where(valid, mr * sr - mi * si, 0.0),
                      bi + jnp.where(valid, mr * si + mi * sr, 0.0))
            s *= 2
        cin_r = cr_s[...]
        cin_i = ci_s[...]
        cr, ci = cin_r, cin_i
        qr = qr_s[...]
        qi = qi_s[...]
        for g in reversed(range(tm // SUBLANES)):
            sl = slice(g * SUBLANES, (g + 1) * SUBLANES)
            gr = br[sl] + qr * cr - qi * ci
            gi = bi[sl] + qr * ci + qi * cr
            gr_s[sl, :] = gr
            gi_s[sl, :] = gi
            cr = gr[0:1, :]
            ci = gi[0:1, :]
        cr_s[...] = cr
        ci_s[...] = ci
        gr = gr_s[...]
        gi = gi_s[...]
        gr_ref[...] = gr.astype(bf16)
        gi_ref[...] = gi.astype(bf16)
        last = row == tm - 1
        gnr = jnp.where(last, cin_r, pltpu.roll(gr, tm - 1, 0))
        gni = jnp.where(last, cin_i, pltpu.roll(gi, tm - 1, 0))
        xr_v = xr_ref[...]
        xi_v = xi_ref[...]
        dar_ref[...] += jnp.sum(gnr * xr_v + gni * xi_v, axis=0, keepdims=True)
        dai_ref[...] += jnp.sum(gni * xr_v - gnr * xi_v, axis=0, keepdims=True)

    rev = lambda c, t: (nt - 1 - t, c)
    return pl.pallas_call(
        kern, name="s5_scan_bwd_" + tag, grid=(nc, nt),
        in_specs=[pl.BlockSpec((tm, S5_CB), rev)] * 4 + [pl.BlockSpec((1, S5_CB), lambda c, t: (0, c))] * 2,
        out_specs=[pl.BlockSpec((tm, S5_CB), rev)] * 2 + [pl.BlockSpec((1, S5_CB), lambda c, t: (0, c))] * 2,
        out_shape=[jax.ShapeDtypeStruct((rows, S5_LANES), bf16)] * 2 + [jax.ShapeDtypeStruct((1, S5_LANES), f32)] * 2,
        scratch_shapes=[pltpu.VMEM((SUBLANES, S5_CB), f32), pltpu.VMEM((SUBLANES, S5_CB), f32),
                        pltpu.VMEM((1, S5_CB), f32), pltpu.VMEM((1, S5_CB), f32),
                        pltpu.VMEM((tm, S5_CB), f32), pltpu.VMEM((tm, S5_CB), f32)],
        compiler_params=_cparams(("parallel", "arbitrary")),
    )(dxr, dxi, xr, xi, a_re, a_im)


def _s5_fwd(tag, proj, cst, rows):
    xr, xi = _s5_scan_fwd(tag, proj, cst["b_re"].astype(bf16), cst["b_im"].astype(bf16), cst["a_re"], cst["a_im"], rows)

    def body(i, xr_ref, xi_ref, u_ref, cre_ref, cim_ref, d_ref, gw_ref, gb_ref, y_ref, out_ref):
        y = (_dot(xr_ref[...].astype(bf16), cre_ref[...]) + _dot(xi_ref[...].astype(bf16), cim_ref[...])
             + d_ref[...] * u_ref[...])
        y_ref[...] = y
        z = _gelu(y)
        zg = _dot(z.astype(bf16), gw_ref[...]) + gb_ref[...]
        out_ref[...] = (z * _sigmoid(zg)).astype(bf16)

    y, out = _rt("s5_out_" + tag, body, rows, ROW_TILE,
                 [(xr, S5_LANES, 0), (xi, S5_LANES, 0), (proj, BRANCH, U_COL)],
                 [cst["c_re"].astype(bf16), cst["c_im"].astype(bf16), cst["s5_d"], cst["glu_w"], cst["glu_b"]],
                 [(BRANCH, f32), (BRANCH, bf16)])
    return out, (xr, xi, y)


def _s5_bwd(tag, saved, proj, cst, d_out, rows):
    xr, xi, y = saved
    c_re = cst["c_re"].astype(bf16)
    c_im = cst["c_im"].astype(bf16)

    def body(i, do_ref, y_ref, u_ref, xr_ref, xi_ref, cre_ref, cim_ref, gw_ref, gb_ref,
             dxr_ref, dxi_ref, dy_ref, dgw_ref, dgb_ref, dd_ref, dcre_ref, dcim_ref):
        yv = y_ref[...]
        z = _gelu(yv)
        zb = z.astype(bf16)
        gt = _sigmoid(_dot(zb, gw_ref[...]) + gb_ref[...])
        dov = do_ref[...]
        dzg = dov * z * gt * (1.0 - gt)
        dzgb = dzg.astype(bf16)
        dz = dov * gt + _dot_nt(dzgb, gw_ref[...])
        dgw_ref[...] += _dot_tn(zb, dzgb)
        dgb_ref[...] += jnp.sum(dzg, axis=0, keepdims=True)
        dy = dz * _gelu_grad(yv)
        dy_ref[...] = dy
        dd_ref[...] += jnp.sum(dy * u_ref[...], axis=0, keepdims=True)
        dyb = dy.astype(bf16)
        dxr_ref[...] = _dot_nt(dyb, cre_ref[...])
        dxi_ref[...] = _dot_nt(dyb, cim_ref[...])
        dcre_ref[...] += _dot_tn(xr_ref[...].astype(bf16), dyb)
        dcim_ref[...] += _dot_tn(xi_ref[...].astype(bf16), dyb)

    dxr, dxi, dy, d_gw, d_gb, d_d, d_cre, d_cim = _rt(
        "s5_out_bwd_" + tag, body, rows, ROW_TILE,
        [(d_out, BRANCH, 0), (y, BRANCH, 0), (proj, BRANCH, U_COL), (xr, S5_LANES, 0), (xi, S5_LANES, 0)],
        [c_re, c_im, cst["glu_w"], cst["glu_b"]],
        [(S5_LANES, f32), (S5_LANES, f32), (BRANCH, f32)],
        acc_outs=[(BRANCH, BRANCH), (1, BRANCH), (1, BRANCH), (S5_LANES, BRANCH), (S5_LANES, BRANCH)])

    gr, gi, d_ar, d_ai = _s5_scan_bwd(tag, dxr, dxi, xr, xi, cst["a_re"], cst["a_im"], rows)
    b_re = cst["b_re"].astype(bf16)
    b_im = cst["b_im"].astype(bf16)

    def body_in(i, gr_ref, gi_ref, dy_ref, u_ref, bre_ref, bim_ref, d_ref, du_ref, dbre_ref, dbim_ref):
        grv = gr_ref[...]
        giv = gi_ref[...]
        du = _dot_nt(grv, bre_ref[...]) + _dot_nt(giv, bim_ref[...]) + dy_ref[...] * d_ref[...]
        du_ref[...] = du.astype(bf16)
        ub = u_ref[...].astype(bf16)
        dbre_ref[...] += _dot_tn(ub, grv)
        dbim_ref[...] += _dot_tn(ub, giv)

    du, d_bre, d_bim = _rt("s5_in_bwd_" + tag, body_in, rows, ROW_TILE,
                           [(gr, S5_LANES, 0), (gi, S5_LANES, 0), (dy, BRANCH, 0), (proj, BRANCH, U_COL)],
                           [b_re, b_im, cst["s5_d"]], [(BRANCH, bf16)],
                           acc_outs=[(BRANCH, S5_LANES), (BRANCH, S5_LANES)])
    dcst = {"b_re": d_bre, "b_im": d_bim, "a_re": d_ar, "a_im": d_ai, "c_re": d_cre, "c_im": d_cim,
            "s5_d": d_d, "glu_b": d_gb}
    return du, dcst, d_gw


def _hg_prep(q, z, lb):
    qs = _sigmoid(q)
    qh = q * qs
    sg = _sigmoid_small(z)
    fg = lb + (1.0 - lb) * sg
    kk = (1.0 - lb) * (1.0 - sg)
    return qs, qh, sg, fg, kk


def _hg_fwd(tag, proj, cst, rows):
    tm = min(ROW_TILE, rows)
    c_sz = HG_CHUNK
    nch = tm // c_sz
    n_chunks = rows // c_sz

    def body(i, q_ref, z_ref, v_ref, g_ref, lb_ref, nw_ref, out_ref, o_ref, ss_ref, sn_ref, st_s):
        @pl.when(i == 0)
        def _():
            st_s[...] = jnp.zeros_like(st_s)

        lb = lb_ref[...]
        _, qh, sg, fg, kk = _hg_prep(q_ref[...], z_ref[...], lb)
        b = _seg_cumsum(jnp.log(fg), tm, c_sz)
        qhat = (qh * jnp.exp(b)).astype(bf16)
        khat = (kk * jnp.exp(-b)).astype(bf16)
        vb = v_ref[...].astype(bf16)
        b3 = b.reshape(nch, c_sz, BRANCH)
        bl3 = b3[:, c_sz - 1:c_sz, :]
        kdec = (kk.reshape(nch, c_sz, BRANCH) * jnp.exp(bl3 - b3)).astype(bf16)
        ebl = jnp.exp(bl3)
        tril = (lax.broadcasted_iota(jnp.int32, (nch, c_sz, c_sz), 1)
                >= lax.broadcasted_iota(jnp.int32, (nch, c_sz, c_sz), 2))
        o_heads = []
        for h in range(HG_HEADS):
            hl = slice(h * HG_DK, (h + 1) * HG_DK)
            q3 = qhat[:, hl].reshape(nch, c_sz, HG_DK)
            k3 = khat[:, hl].reshape(nch, c_sz, HG_DK)
            v3 = vb[:, hl].reshape(nch, c_sz, HG_DK)
            a_mat = jnp.where(tril, _bdot_nt(q3, k3), 0.0).astype(bf16)
            o3 = _bdot(a_mat, v3)
            st = st_s[hl, :]
            before = []
            for ci in range(nch):
                before.append(st.astype(bf16))
                st = st * ebl[ci][:, hl] + _dot_tn(v3[ci], kdec[ci][:, hl])
                sn_ref[ci, hl, :] = st.astype(bf16)
            st_s[hl, :] = st
            s3 = jnp.stack(before)
            ss_ref[:, hl, :] = s3
            o3 = o3 + _bdot_nt(q3, s3)
            o_heads.append(o3.reshape(tm, HG_DK))
        o = jnp.concatenate(o_heads, axis=1)
        o_ref[...] = o
        r = lax.rsqrt(_head_mean(o * o) + EPS)
        g = g_ref[...]
        out_ref[...] = (o * r * nw_ref[...] * (g * _sigmoid(g))).astype(bf16)

    out, o, ss, sn = _rt(
        "hg_fwd_" + tag, body, rows, tm,
        [(proj, BRANCH, U_COL + 1), (proj, BRANCH, U_COL + 2), (proj, BRANCH, U_COL + 3), (proj, BRANCH, U_COL + 4)],
        [cst["hg_lb"], cst["hg_nw"]],
        [(BRANCH, bf16), (BRANCH, f32),
         ((n_chunks, BRANCH, HG_DK), (nch, BRANCH, HG_DK), lambda t: (t, 0, 0), bf16),
         ((n_chunks, BRANCH, HG_DK), (nch, BRANCH, HG_DK), lambda t: (t, 0, 0), bf16)],
        scratch=[pltpu.VMEM((BRANCH, HG_DK), f32)])
    return out, (o, ss, sn)


def _hg_bwd(tag, saved, proj, cst, d_out, rows):
    o_saved, ss, sn = saved
    tm = min(ROW_TILE, rows)
    c_sz = HG_CHUNK
    nch = tm // c_sz

    def body(i, do_ref, q_ref, z_ref, v_ref, g_ref, o_ref, ss_ref, sn_ref, lb_ref, nw_ref,
             dq_ref, dz_ref, dv_ref, dg_ref, dlb_ref, dnw_ref, dst_s):
        @pl.when(i == 0)
        def _():
            dst_s[...] = jnp.zeros_like(dst_s)

        lb = lb_ref[...]
        q = q_ref[...]
        qs, qh, sg, fg, kk = _hg_prep(q, z_ref[...], lb)
        b = _seg_cumsum(jnp.log(fg), tm, c_sz)
        eb = jnp.exp(b)
        enb = jnp.exp(-b)
        qhat = (qh * eb).astype(bf16)
        khat = (kk * enb).astype(bf16)
        vb = v_ref[...].astype(bf16)
        b3 = b.reshape(nch, c_sz, BRANCH)
        bl3 = b3[:, c_sz - 1:c_sz, :]
        dec3 = jnp.exp(bl3 - b3)
        kdec = (kk.reshape(nch, c_sz, BRANCH) * dec3).astype(bf16)
        ebl = jnp.exp(bl3)
        g = g_ref[...]
        gs = _sigmoid(g)
        o = o_ref[...]
        r = lax.rsqrt(_head_mean(o * o) + EPS)
        oh = o * r
        nw = nw_ref[...]
        dov = do_ref[...]
        don = dov * (g * gs)
        dg_ref[...] = (dov * oh * nw * (gs * (1.0 + g * (1.0 - gs)))).astype(bf16)
        dnw_ref[...] += jnp.sum(don * oh, axis=0, keepdims=True)
        doh = don * nw
        d_o = r * (doh - oh * _head_mean(doh * oh))
        dob = d_o.astype(bf16)
        t_idx = lax.broadcasted_iota(jnp.int32, (nch, c_sz, c_sz), 1)
        s_idx = lax.broadcasted_iota(jnp.int32, (nch, c_sz, c_sz), 2)
        heads = []
        for h in range(HG_HEADS):
            hl = slice(h * HG_DK, (h + 1) * HG_DK)
            q3 = qhat[:, hl].reshape(nch, c_sz, HG_DK)
            k3 = khat[:, hl].reshape(nch, c_sz, HG_DK)
            v3 = vb[:, hl].reshape(nch, c_sz, HG_DK)
            do3 = dob[:, hl].reshape(nch, c_sz, HG_DK)
            s3 = ss_ref[:, hl, :]
            da_mat = jnp.where(t_idx >= s_idx, _bdot_nt(do3, v3), 0.0).astype(bf16)
            a_t = jnp.where(t_idx <= s_idx, _bdot_nt(k3, q3), 0.0).astype(bf16)
            da_t = jnp.where(t_idx <= s_idx, _bdot_nt(v3, do3), 0.0).astype(bf16)
            dqhat = _bdot(do3, s3) + _bdot(da_mat, k3)
            dkhat = _bdot(da_t, q3)
            dst = dst_s[hl, :]
            after = [None] * nch
            for ci in reversed(range(nch)):
                after[ci] = dst
                dst = dst * ebl[ci][:, hl] + _dot_tn(do3[ci], q3[ci])
            dst_s[hl, :] = dst
            ds3 = jnp.stack(after)
            ds3b = ds3.astype(bf16)
            dk_inter = _bdot(v3, ds3b) * dec3[:, :, hl]
            dv3 = _bdot(a_t, do3) + _bdot_nt(kdec[:, :, hl], ds3b)
            flux = jnp.sum(sn_ref[:, hl, :].astype(f32) * ds3, axis=1, keepdims=True)
            heads.append((dqhat.reshape(tm, HG_DK), dkhat.reshape(tm, HG_DK), dk_inter.reshape(tm, HG_DK),
                          dv3.reshape(tm, HG_DK), jnp.broadcast_to(flux, (nch, c_sz, HG_DK)).reshape(tm, HG_DK)))
        dqhat, dkhat, dk_inter, dv, flux = (jnp.concatenate(parts, axis=1) for parts in zip(*heads))
        dv_ref[...] = dv.astype(bf16)
        dqh = dqhat * eb
        dk = dkhat * enb + dk_inter
        db = qhat.astype(f32) * dqhat - khat.astype(f32) * dkhat - kk * dk_inter
        dlf = _seg_rev_cumsum(db, tm, c_sz) + flux
        tt = (1.0 - lb) * sg * (1.0 - sg)
        dz_ref[...] = (dlf * tt / fg - dk * tt).astype(bf16)
        dlb_ref[...] += jnp.sum(dlf * (1.0 - sg) / fg - dk * (1.0 - sg), axis=0, keepdims=True)
        dq_ref[...] = (dqh * (qs * (1.0 + q * (1.0 - qs)))).astype(bf16)

    dq, dz, dv, dg, d_lb, d_nw = _rt(
        "hg_bwd_" + tag, body, rows, tm,
        [(d_out, BRANCH, 0), (proj, BRANCH, U_COL + 1), (proj, BRANCH, U_COL + 2), (proj, BRANCH, U_COL + 3),
         (proj, BRANCH, U_COL + 4), (o_saved, BRANCH, 0), (ss, (nch, BRANCH, HG_DK), lambda t: (t, 0, 0)),
         (sn, (nch, BRANCH, HG_DK), lambda t: (t, 0, 0))],
        [cst["hg_lb"], cst["hg_nw"]],
        [(BRANCH, bf16)] * 4, acc_outs=[(1, BRANCH), (1, BRANCH)],
        scratch=[pltpu.VMEM((BRANCH, HG_DK), f32)],
        reverse=True)
    return dq, dz, dv, dg, {"hg_lb": d_lb, "hg_nw": d_nw}


def _rg_gates(xc, wa_ref, ba_ref, wx_ref, bx_ref, sp8):
    xcb = xc.astype(bf16)
    r = _sigmoid(_dot(xcb, wa_ref[...]) + ba_ref[...])
    ig = _sigmoid(_dot(xcb, wx_ref[...]) + bx_ref[...])
    la = -sp8 * r
    a = jnp.exp(la)
    mult = jnp.sqrt(-_expm1(2.0 * la))
    return xcb, r, ig, a, mult


def _rg_fwd(tag, proj, cst, rows):
    tm = min(ROW_TILE, rows)

    def body(i, xb_ref, gate_ref, cw_ref, cb_ref, wa_ref, ba_ref, wx_ref, bx_ref, sp_ref,
             out_ref, xc_ref, h_ref, hp_ref, prev_s, hc_s):
        @pl.when(i == 0)
        def _():
            prev_s[...] = jnp.zeros_like(prev_s)
            hc_s[...] = jnp.zeros_like(hc_s)

        row = _rows((tm, BRANCH))
        xb = xb_ref[...]
        prev = prev_s[...]
        xc = cb_ref[...] + cw_ref[3:4, :] * xb
        for j in range(1, 4):
            sh = jnp.where(row >= j, pltpu.roll(xb, j, 0), pltpu.roll(prev, j, 0))
            xc = xc + cw_ref[3 - j:4 - j, :] * sh
        prev_s[...] = xb
        xc_ref[...] = xc
        _, r, ig, a, mult = _rg_gates(xc, wa_ref, ba_ref, wx_ref, bx_ref, sp_ref[...])
        a_cum, h_loc = _scan_fwd(a, mult * ig * xc, tm)
        hc = hc_s[...]
        h = h_loc + a_cum * hc
        h_ref[...] = h
        hp_ref[...] = jnp.where(row >= 1, pltpu.roll(h, 1, 0), hc)
        hc_s[...] = h[tm - 1:tm, :]
        out_ref[...] = (h * _gelu(gate_ref[...])).astype(bf16)

    out, xc, h, hp = _rt(
        "rg_fwd_" + tag, body, rows, tm,
        [(proj, BRANCH, U_COL + 5), (proj, BRANCH, U_COL + 6)],
        [cst["rg_cw"], cst["rg_cb"], cst["rg_wa"].astype(bf16), cst["rg_ba"], cst["rg_wx"].astype(bf16),
         cst["rg_bx"], cst["rg_sp8"]],
        [(BRANCH, bf16), (BRANCH, f32), (BRANCH, f32), (BRANCH, f32)],
        scratch=[pltpu.VMEM((tm, BRANCH), f32), pltpu.VMEM((1, BRANCH), f32)])
    return out, (xc, h, hp)


def _rg_bwd(tag, saved, proj, cst, d_out, rows):
    xc_saved, h_saved, hp_saved = saved
    tm = min(ROW_TILE, rows)

    def body(i, do_ref, xb_ref, gate_ref, xc_ref, h_ref, hp_ref, cw_ref, wa_ref, ba_ref, wx_ref, bx_ref, sp_ref,
             dxb_ref, dgate_ref, dcw_ref, dcb_ref, dwa_ref, dba_ref, dwx_ref, dbx_ref, dsp_ref,
             nxt_s, ec_s):
        @pl.when(i == 0)
        def _():
            nxt_s[...] = jnp.zeros_like(nxt_s)
            ec_s[...] = jnp.zeros_like(ec_s)

        row = _rows((tm, BRANCH))
        xc = xc_ref[...]
        sp8 = sp_ref[...]
        xcb, r, ig, a, mult = _rg_gates(xc, wa_ref, ba_ref, wx_ref, bx_ref, sp8)
        gate = gate_ref[...]
        dov = do_ref[...]
        dh = dov * _gelu(gate)
        dgate_ref[...] = (dov * h_ref[...] * _gelu_grad(gate)).astype(bf16)
        a_cum, e_loc = _scan_bwd(a, a * dh, tm)
        ec = ec_s[...]
        e = e_loc + a_cum * ec
        g_tot = dh + jnp.where(row == tm - 1, ec, pltpu.roll(e, tm - 1, 0))
        ec_s[...] = e[0:1, :]
        d_a = g_tot * hp_ref[...]
        d_mult = g_tot * ig * xc
        d_ix = g_tot * mult
        d_ig = d_ix * xc
        d_xc = d_ix * ig
        d_la = d_a * a - d_mult * (a * a) / mult
        d_r = -d_la * sp8
        dsp_ref[...] += jnp.sum(-d_la * r, axis=0, keepdims=True)
        dzr = d_r * r * (1.0 - r)
        dzi = d_ig * ig * (1.0 - ig)
        dzrb = dzr.astype(bf16)
        dzib = dzi.astype(bf16)
        d_xc = d_xc + _dot_nt(dzrb, wa_ref[...]) + _dot_nt(dzib, wx_ref[...])
        dwa_ref[...] += _dot_tn(xcb, dzrb)
        dwx_ref[...] += _dot_tn(xcb, dzib)
        dba_ref[...] += jnp.sum(dzr, axis=0, keepdims=True)
        dbx_ref[...] += jnp.sum(dzi, axis=0, keepdims=True)
        dcb_ref[...] += jnp.sum(d_xc, axis=0, keepdims=True)
        nxt = nxt_s[...]
        xb = xb_ref[...]
        dxb = cw_ref[3:4, :] * d_xc
        dcw_ref[3:4, :] += jnp.sum(d_xc * xb, axis=0, keepdims=True)
        for j in range(1, 4):
            sh = jnp.where(row < tm - j, pltpu.roll(d_xc, tm - j, 0), pltpu.roll(nxt, tm - j, 0))
            dxb = dxb + cw_ref[3 - j:4 - j, :] * sh
            dcw_ref[3 - j:4 - j, :] += jnp.sum(sh * xb, axis=0, keepdims=True)
        nxt_s[...] = d_xc
        dxb_ref[...] = dxb.astype(bf16)

    wa = cst["rg_wa"].astype(bf16)
    wx = cst["rg_wx"].astype(bf16)
    dxb, dgate, d_cw, d_cb, d_wa, d_ba, d_wx, d_bx, d_sp = _rt(
        "rg_bwd_" + tag, body, rows, tm,
        [(d_out, BRANCH, 0), (proj, BRANCH, U_COL + 5), (proj, BRANCH, U_COL + 6), (xc_saved, BRANCH, 0),
         (h_saved, BRANCH, 0), (hp_saved, BRANCH, 0)],
        [cst["rg_cw"], wa, cst["rg_ba"], wx, cst["rg_bx"], cst["rg_sp8"]],
        [(BRANCH, bf16), (BRANCH, bf16)],
        acc_outs=[(4, BRANCH), (1, BRANCH), (BRANCH, BRANCH), (1, BRANCH), (BRANCH, BRANCH), (1, BRANCH), (1, BRANCH)],
        scratch=[pltpu.VMEM((tm, BRANCH), f32), pltpu.VMEM((1, BRANCH), f32)],
        reverse=True)
    dcst = {"rg_cw": d_cw, "rg_cb": d_cb, "rg_wa": d_wa, "rg_ba": d_ba, "rg_wx": d_wx, "rg_bx": d_bx, "rg_sp8": d_sp}
    return dxb, dgate, dcst


def _merge_fwd(tag, proj, outs, bp, rows):
    def body(i, ya_ref, yb_ref, yc_ref, gm_ref, p_ref, m_ref):
        acc = None
        for n, y_ref in enumerate((ya_ref, yb_ref, yc_ref)):
            up = _dot(y_ref[...], p_ref[n])
            term = _sigmoid(gm_ref[:, n * D_MODEL:(n + 1) * D_MODEL]) * up
            acc = term if acc is None else acc + term
        m_ref[...] = acc.astype(bf16)
    return _rt("merge_fwd_" + tag, body, rows, ROW_TILE,
               [(outs[0], BRANCH, 0), (outs[1], BRANCH, 0), (outs[2], BRANCH, 0), (proj, GM_WIDTH, 0)],
               [bp], [(D_MODEL, bf16)])[0]


def _merge_bwd(tag, proj, outs, bp, dmerged, rows):
    def body(i, dm_ref, ya_ref, yb_ref, yc_ref, gm_ref, p_ref, da_ref, db_ref, dc_ref, dgm_ref, dp_ref):
        dm = dm_ref[...]
        for n, (y_ref, dy_ref) in enumerate(((ya_ref, da_ref), (yb_ref, db_ref), (yc_ref, dc_ref))):
            yv = y_ref[...]
            up = _dot(yv, p_ref[n])
            gt = _sigmoid(gm_ref[:, n * D_MODEL:(n + 1) * D_MODEL])
            dup = (dm * gt).astype(bf16)
            dgm_ref[:, n * D_MODEL:(n + 1) * D_MODEL] = (dm * up * gt * (1.0 - gt)).astype(bf16)
            dy_ref[...] = _dot_nt(dup, p_ref[n])
            dp_ref[n] += _dot_tn(yv, dup)
    return _rt("merge_bwd_" + tag, body, rows, ROW_TILE,
               [(dmerged, D_MODEL, 0), (outs[0], BRANCH, 0), (outs[1], BRANCH, 0), (outs[2], BRANCH, 0),
                (proj, GM_WIDTH, 0)],
               [bp], [(BRANCH, f32), (BRANCH, f32), (BRANCH, f32), (GM_WIDTH, bf16)],
               acc_outs=[(N_BRANCH, BRANCH, D_MODEL)])


def _block_diag(blocks):
    g, r, c = blocks.shape
    on_diag = (lax.broadcasted_iota(jnp.int32, (g * r, g * c), 0) // r
               == lax.broadcasted_iota(jnp.int32, (g * r, g * c), 1) // c)
    tiled = jnp.broadcast_to(blocks.reshape(g * r, 1, c), (g * r, g, c)).reshape(g * r, g * c)
    return jnp.where(on_diag, tiled, 0.0)


def _prep_consts(sp):
    p = jax.nn.softmax(sp["hg_lb_logits"], axis=0)
    lower = jnp.cumsum(p, axis=0) - p[0]
    out = []
    for l in range(DEPTH):
        lr = jnp.minimum(sp["s5_lambda_re"][l], S5_EIG_MAX)
        li = sp["s5_lambda_im"][l]
        dt = jnp.exp(sp["s5_log_dt"][l])[:, None]
        mag = jnp.exp(lr * dt)
        ar = mag * jnp.cos(li * dt)
        ai = mag * jnp.sin(li * dt)
        den = lr * lr + li * li
        fr = ((ar - 1.0) * lr + ai * li) / den
        fi = (ai * lr - (ar - 1.0) * li) / den
        br, bi = sp["s5_b_re"][l], sp["s5_b_im"][l]
        bbr = fr[..., None] * br - fi[..., None] * bi
        bbi = fr[..., None] * bi + fi[..., None] * br
        c = {
            "a_re": ar.reshape(1, S5_LANES), "a_im": ai.reshape(1, S5_LANES),
            "b_re": _block_diag(bbr.transpose(0, 2, 1)), "b_im": _block_diag(bbi.transpose(0, 2, 1)),
            "c_re": _block_diag(sp["s5_c_re"][l].transpose(0, 2, 1)),
            "c_im": -_block_diag(sp["s5_c_im"][l].transpose(0, 2, 1)),
            "s5_d": sp["s5_d"][l][None], "glu_b": sp["s5_glu_b"][l][None],
            "hg_lb": lower[l][None], "hg_nw": sp["hg_norm_w"][l][None],
            "rg_cw": sp["rg_conv_w"][l], "rg_cb": sp["rg_conv_b"][l][None],
            "rg_wa": _block_diag(sp["rg_wa"][l]), "rg_ba": sp["rg_ba"][l][None],
            "rg_wx": _block_diag(sp["rg_wx"][l]), "rg_bx": sp["rg_bx"][l][None],
            "rg_sp8": (RG_C * jax.nn.softplus(-sp["rg_lambda"][l]))[None],
        }
        out.append(c)
    return out


def _mixer_fwd(tag, x, hb, w_in, bp, w_out, cst, next_nw, rows):
    proj = _mm1("mix_proj_" + tag, hb, w_in, "nn", rows, IN_TOTAL, D_MODEL, 512, IN_TOTAL // 4, D_MODEL)
    cst = dict(cst)
    out_a, sv_a = _s5_fwd(tag, proj, cst, rows)
    out_b, sv_b = _hg_fwd(tag, proj, cst, rows)
    out_c, sv_c = _rg_fwd(tag, proj, cst, rows)
    merged = _merge_fwd(tag, proj, (out_a, out_b, out_c), bp, rows)
    x_out, hb_out = _mm("mix_out_" + tag, [merged], [w_out], [(0, 0, 0)], 1, "nn", rows, D_MODEL, D_MODEL,
                        512, D_MODEL, D_MODEL, [f32, bf16], _residual_then_norm(1.0), extras=[x], vecs=[next_nw])
    return x_out, hb_out, (x, hb, proj, (out_a, out_b, out_c), merged, sv_a, sv_b, sv_c)


def _mixer_bwd(tag, saved, nw, w_in, bp, w_out, cst, dx, dxb, rows):
    x, hb, proj, outs, merged, sv_a, sv_b, sv_c = saved
    d_wout = _mm1("mix_dwout_" + tag, merged, dxb, "tn", D_MODEL, D_MODEL, rows, D_MODEL, D_MODEL, TOKEN_K,
                  out_dtype=bf16)
    dmerged = _mm1("mix_dmerged_" + tag, dxb, w_out, "nt", rows, D_MODEL, D_MODEL, 512, D_MODEL, D_MODEL)
    d_a, d_b, d_c, dgm, d_bp = _merge_bwd(tag, proj, outs, bp, dmerged, rows)
    dxbc, dgatec, dcst_c = _rg_bwd(tag, sv_c, proj, cst, d_c, rows)
    dq, dz, dv, dg, dcst_b = _hg_bwd(tag, sv_b, proj, cst, d_b, rows)
    du, dcst_a, d_glu_w = _s5_bwd(tag, sv_a, proj, cst, d_a, rows)
    dproj = jnp.concatenate([dgm, du, dq, dz, dv, dg, dxbc, dgatec], axis=1)
    d_win = _mm1("mix_dwin_" + tag, hb, dproj, "tn", D_MODEL, IN_TOTAL, rows, D_MODEL, IN_TOTAL // 4, TOKEN_K,
                 out_dtype=bf16)
    dx_in, dxb_in, d_nw = _mm("mix_dh_" + tag, [dproj], [w_in], [(0, 0, 0)], 1, "nt", rows, D_MODEL, IN_TOTAL,
                              512, D_MODEL, IN_TOTAL // 2, [f32, bf16], _norm_bwd_epilogue, extras=[x, dx], vecs=[nw],
                              n_part=1)
    dcst = {**dcst_a, **dcst_b, **dcst_c}
    return dx_in, dxb_in, jnp.sum(d_nw, axis=0), d_win, d_bp, d_wout, d_glu_w, dcst


def _local_step(x, target, big, small):
    rows = x.shape[0]
    consts, consts_vjp = jax.vjp(_prep_consts, small)
    norm_w = small["norm_w"]
    saved = []
    h = x
    hb = _rms_fwd("first_norm", x, norm_w[0, 0][None], rows)
    for l in range(DEPTH):
        t = str(l)
        cst = dict(consts[l])
        cst["glu_w"] = big["glu_w"][l]
        after = [norm_w[l + 1, 0][None]] if l + 1 < DEPTH else []
        h, hb, sv0 = _ffn_fwd(t + "a", h, hb, big["gate"][l, 0], big["up"][l, 0], big["down"][l, 0],
                              [norm_w[l, 1][None]], rows)
        h, hb, sv1 = _mixer_fwd(t, h, hb, big["w_in"][l], big["bp"][l], big["w_out"][l], cst, norm_w[l, 2][None], rows)
        h, hb, sv2 = _ffn_fwd(t + "b", h, hb, big["gate"][l, 1], big["up"][l, 1], big["down"][l, 1], after, rows)
        saved.append((sv0, sv1, sv2, cst))
    dx, dxb, loss, d_fnw = _loss_head(h, small["final_norm_w"][None], target, rows)
    g_big = {k: [None] * DEPTH for k in ("gate", "up", "down", "w_in", "bp", "w_out", "glu_w")}
    d_norm = [None] * DEPTH
    d_consts = [None] * DEPTH
    for l in reversed(range(DEPTH)):
        t = str(l)
        sv0, sv1, sv2, cst = saved[l]
        dx, dxb, dn2, dg1, du1, dd1 = _ffn_bwd(t + "b", sv2, norm_w[l, 2][None], big["gate"][l, 1], big["up"][l, 1],
                                               big["down"][l, 1], dx, dxb, rows)
        dx, dxb, dn1, d_win, d_bp, d_wout, d_glu_w, dcst = _mixer_bwd(
            t, sv1, norm_w[l, 1][None], big["w_in"][l], big["bp"][l], big["w_out"][l], cst, dx, dxb, rows)
        dx, dxb, dn0, dg0, du0, dd0 = _ffn_bwd(t + "a", sv0, norm_w[l, 0][None], big["gate"][l, 0], big["up"][l, 0],
                                               big["down"][l, 0], dx, dxb, rows)
        g_big["gate"][l] = jnp.stack([dg0, dg1])
        g_big["up"][l] = jnp.stack([du0, du1])
        g_big["down"][l] = jnp.stack([dd0, dd1])
        g_big["w_in"][l] = d_win
        g_big["bp"][l] = d_bp
        g_big["w_out"][l] = d_wout
        g_big["glu_w"][l] = d_glu_w
        d_norm[l] = jnp.concatenate([dn0, dn1, dn2], axis=0)
        d_consts[l] = dcst
    g_big = {k: jnp.stack(v) for k, v in g_big.items()}
    (g_small,) = consts_vjp(d_consts)
    g_small = dict(g_small)
    g_small["norm_w"] = g_small["norm_w"] + jnp.stack(d_norm)
    g_small["final_norm_w"] = g_small["final_norm_w"] + d_fnw[0]
    return loss[0, 0], dx, g_big, g_small


MESH_IDS = pl.DeviceIdType.MESH
ANY_SPEC = pl.BlockSpec(memory_space=pl.ANY)


def _place():
    return lax.axis_index("x"), lax.axis_index("y"), lax.axis_index("c")


def _all_gather(name, shards):
    n = len(shards)

    def body(*refs):
        x_refs, out_refs = refs[:n], refs[n:2 * n]
        send_sems, recv_sems, local_sems = refs[2 * n:]
        x, y, c = _place()
        me, sibling = (x, y, c), (x, y, 1 - c)
        chips = [(1 - x, y), (x, 1 - y), (1 - x, 1 - y)]

        def blk(i, px, py, pc):
            return out_refs[i].at[4 * px + 2 * py + pc]

        def copy(i, k, block, to, src=None):
            return pltpu.make_async_remote_copy(
                src_ref=blk(i, *block) if src is None else src, dst_ref=blk(i, *block),
                send_sem=send_sems.at[7 * i + k], recv_sem=recv_sems.at[7 * i + k], device_id=to,
                device_id_type=MESH_IDS)

        mine = [pltpu.make_async_copy(x_refs[i], blk(i, *me), local_sems.at[i]) for i in range(n)]
        for cp in mine:
            cp.start()
        first = []
        for i in range(n):
            first.append(copy(i, 0, me, sibling, src=x_refs[i]))
            first += [copy(i, 1 + j, me, (*chip, c), src=x_refs[i]) for j, chip in enumerate(chips)]
        for cp in first:
            cp.start()
        passed = []
        for j, chip in enumerate(chips):
            for i in range(n):
                copy(i, 1 + j, (*chip, c), me).wait_recv()
                fwd = copy(i, 4 + j, (*chip, c), sibling)
                fwd.start()
                passed.append(fwd)
        for i in range(n):
            copy(i, 0, sibling, me).wait_recv()
            for j, chip in enumerate(chips):
                copy(i, 4 + j, (*chip, 1 - c), me).wait_recv()
        for cp in first + passed:
            cp.wait_send()
        for cp in mine:
            cp.wait()

    return pl.pallas_call(
        body, name=name, out_shape=[jax.ShapeDtypeStruct((N_DEV,) + s.shape, s.dtype) for s in shards],
        in_specs=[ANY_SPEC] * n, out_specs=[ANY_SPEC] * n,
        scratch_shapes=[pltpu.SemaphoreType.DMA((7 * n,)), pltpu.SemaphoreType.DMA((7 * n,)),
                        pltpu.SemaphoreType.DMA((n,))],
    )(*shards)


def _row_tile(rows):
    return rows if rows <= 512 else next(t for t in range(512, 7, -8) if rows % t == 0)


def _reduce_scatter(parts):
    n = len(parts)
    _, _, c = _place()

    def body_pair(*refs):
        p_refs, got_refs = refs[:n], refs[n:2 * n]
        send_sems, recv_sems = refs[2 * n:]
        x, y, c = _place()
        cps = [pltpu.make_async_remote_copy(
            src_ref=p_refs[i].at[1 - c], dst_ref=got_refs[i], send_sem=send_sems.at[i], recv_sem=recv_sems.at[i],
            device_id=(x, y, 1 - c), device_id_type=MESH_IDS) for i in range(n)]
        for cp in cps:
            cp.start()
        for cp in cps:
            cp.wait()

    from_sibling = pl.pallas_call(
        body_pair, name="rs_pair", out_shape=[jax.ShapeDtypeStruct(p.shape[1:], p.dtype) for p in parts],
        in_specs=[ANY_SPEC] * n, out_specs=[ANY_SPEC] * n,
        scratch_shapes=[pltpu.SemaphoreType.DMA((n,)), pltpu.SemaphoreType.DMA((n,))],
    )(*parts)

    chip_sums = []
    for i, (part, got) in enumerate(zip(parts, from_sibling)):
        _, _, r, cols = part.shape
        tr = _row_tile(r)

        def body_add(idx_ref, p_ref, g_ref, o_ref):
            o_ref[...] = (p_ref[...].astype(f32) + g_ref[...].astype(f32)).astype(o_ref.dtype)

        chip_sums.append(pl.pallas_call(
            body_add, name="rs_pair_sum_%d" % i, out_shape=jax.ShapeDtypeStruct((4, r, cols), part.dtype),
            grid_spec=pltpu.PrefetchScalarGridSpec(
                num_scalar_prefetch=1, grid=(4, r // tr),
                in_specs=[pl.BlockSpec((None, None, tr, cols), lambda j, t, idx: (idx[0], j, t, 0)),
                          pl.BlockSpec((None, tr, cols), lambda j, t, idx: (j, t, 0))],
                out_specs=pl.BlockSpec((None, tr, cols), lambda j, t, idx: (j, t, 0))),
            compiler_params=_cparams(("parallel", "parallel")),
        )(jnp.stack([c]).astype(jnp.int32), part, got))

    def body_chips(*refs):
        t_refs, got_refs = refs[:n], refs[n:2 * n]
        send_sems, recv_sems = refs[2 * n:]
        x, y, c = _place()
        chips = [(1 - x, y), (x, 1 - y), (1 - x, 1 - y)]
        cps = [pltpu.make_async_remote_copy(
            src_ref=t_refs[i].at[2 * px + py], dst_ref=got_refs[i].at[k], send_sem=send_sems.at[3 * i + k],
            recv_sem=recv_sems.at[3 * i + k], device_id=(px, py, c), device_id_type=MESH_IDS)
            for i in range(n) for k, (px, py) in enumerate(chips)]
        for cp in cps:
            cp.start()
        for cp in cps:
            cp.wait()

    from_chips = pl.pallas_call(
        body_chips, name="rs_chips", out_shape=[jax.ShapeDtypeStruct((3,) + p.shape[2:], p.dtype) for p in parts],
        in_specs=[ANY_SPEC] * n, out_specs=[ANY_SPEC] * n,
        scratch_shapes=[pltpu.SemaphoreType.DMA((3 * n,)), pltpu.SemaphoreType.DMA((3 * n,))],
    )(*chip_sums)
    return list(zip(chip_sums, from_chips))


def _own_index():
    x, y, _ = _place()
    return jnp.stack([2 * x + y]).astype(jnp.int32)


def _own_total(name, chip_sum, others):
    _, r, cols = chip_sum.shape
    tr = _row_tile(r)

    def body(idx_ref, t_ref, g_ref, o_ref):
        o_ref[...] = ((t_ref[...].astype(f32) + g_ref[0].astype(f32)) + g_ref[1].astype(f32)) + g_ref[2].astype(f32)

    return pl.pallas_call(
        body, name=name, out_shape=jax.ShapeDtypeStruct((r, cols), f32),
        grid_spec=pltpu.PrefetchScalarGridSpec(
            num_scalar_prefetch=1, grid=(r // tr,),
            in_specs=[pl.BlockSpec((None, tr, cols), lambda t, idx: (idx[0], t, 0)),
                      pl.BlockSpec((3, tr, cols), lambda t, idx: (0, t, 0))],
            out_specs=pl.BlockSpec((tr, cols), lambda t, idx: (t, 0))),
        compiler_params=_cparams(("parallel",)),
    )(_own_index(), chip_sum, others)


def _adam_update(w, gv, m, v):
    m_new = ADAM_B1 * m + (1.0 - ADAM_B1) * gv
    v_new = ADAM_B2 * v + (1.0 - ADAM_B2) * (gv * gv)
    m_hat = m_new / (1.0 - ADAM_B1 ** ADAM_STEP)
    v_hat = v_new / (1.0 - ADAM_B2 ** ADAM_STEP)
    return -ADAM_LR * (m_hat / (jnp.sqrt(v_hat) + ADAM_EPS) + ADAM_WD * w), m_new, v_new


def _adamw_reduced(name, w, chip_sum, others, m, v):
    rows, cols = w.shape
    tr = _row_tile(rows)

    def body(idx_ref, w_ref, t_ref, o_ref, m_ref, v_ref, g_ref, d_ref, nm_ref, nv_ref):
        gv = ((t_ref[...].astype(f32) + o_ref[0].astype(f32)) + o_ref[1].astype(f32)) + o_ref[2].astype(f32)
        g_ref[...] = gv
        d_ref[...], nm_ref[...], nv_ref[...] = _adam_update(w_ref[...], gv, m_ref[...], v_ref[...])

    spec = pl.BlockSpec((tr, cols), lambda t, idx: (t, 0))
    return pl.pallas_call(
        body, name=name, out_shape=[jax.ShapeDtypeStruct((rows, cols), f32)] * 4,
        grid_spec=pltpu.PrefetchScalarGridSpec(
            num_scalar_prefetch=1, grid=(rows // tr,),
            in_specs=[spec, pl.BlockSpec((None, tr, cols), lambda t, idx: (idx[0], t, 0)),
                      pl.BlockSpec((3, tr, cols), lambda t, idx: (0, t, 0)), spec, spec],
            out_specs=[spec] * 4),
        compiler_params=_cparams(("parallel",)),
    )(_own_index(), w, chip_sum, others, m, v)


def _adamw(name, w, g, m, v):
    rows, cols = w.shape
    tr = _row_tile(rows)

    def body(w_ref, g_ref, m_ref, v_ref, d_ref, nm_ref, nv_ref):
        d_ref[...], nm_ref[...], nv_ref[...] = _adam_update(w_ref[...], g_ref[...], m_ref[...], v_ref[...])

    spec = pl.BlockSpec((tr, cols), lambda i: (i, 0))
    return pl.pallas_call(
        body, name=name, grid=(rows // tr,), in_specs=[spec] * 4, out_specs=[spec] * 3,
        out_shape=[jax.ShapeDtypeStruct((rows, cols), f32)] * 3, compiler_params=_cparams(("parallel",)),
    )(w, g, m, v)


WEIGHT_NAMES = ["norm_w", "final_norm_w", "ffn_gate", "ffn_up", "ffn_down", "w_in", "branch_proj", "w_out",
                "s5_lambda_re", "s5_lambda_im", "s5_log_dt", "s5_b_re", "s5_b_im", "s5_c_re", "s5_c_im", "s5_d",
                "s5_glu_w", "s5_glu_b", "hg_lb_logits", "hg_norm_w", "rg_conv_w", "rg_conv_b", "rg_wa", "rg_ba",
                "rg_wx", "rg_bx", "rg_lambda"]
SHARDED = {"ffn_gate": (3, "gate"), "ffn_up": (3, "up"), "ffn_down": (2, "down"), "w_in": (2, "w_in"),
           "branch_proj": (3, "bp"), "w_out": (1, "w_out"), "s5_glu_w": (1, "glu_w"),
           "norm_w": (2, None), "rg_conv_w": (2, None)}
BIG = ["ffn_gate", "ffn_up", "ffn_down", "w_in", "branch_proj", "w_out", "s5_glu_w"]
SMALL_SHARDED = ["norm_w", "rg_conv_w"]
REPLICATED = [n for n in WEIGHT_NAMES if n not in SHARDED]
LANES = 128


PACK_ROWS = 512


def _pack_rows(arrays, names):
    pieces = []
    for n in names:
        flat = arrays[n].reshape(-1)
        pieces.append(jnp.pad(flat, (0, -flat.shape[0] % LANES)).reshape(-1, LANES))
    rows = jnp.concatenate(pieces, axis=0)
    return jnp.pad(rows, ((0, -rows.shape[0] % PACK_ROWS), (0, 0)))


def _unpack_rows(rows, names, like):
    out, r0 = {}, 0
    for n in names:
        size = math.prod(like[n].shape)
        nrows = -(-size // LANES)
        out[n] = rows[r0:r0 + nrows].reshape(-1)[:size].reshape(like[n].shape)
        r0 += nrows
    return out


def _unshard(gathered, axis):
    g = jnp.moveaxis(gathered, 0, axis)
    shp = g.shape
    return g.reshape(shp[:axis] + (shp[axis] * shp[axis + 1],) + shp[axis + 2:])


def _to_blocks(full, axis):
    shp = full.shape
    g = full.reshape(shp[:axis] + (4, 2, shp[axis] // N_DEV) + shp[axis + 1:])
    g = jnp.moveaxis(g, (axis, axis + 1), (1, 0))
    return g.reshape(2, 4, -1, g.shape[-1])


W_IN_SPLIT = IN_TOTAL - GM_WIDTH


def kernel(x, norm_w, final_norm_w, ffn_gate, ffn_up, ffn_down, w_in, branch_proj, w_out, s5_lambda_re, s5_lambda_im, s5_log_dt, s5_b_re, s5_b_im, s5_c_re, s5_c_im, s5_d, s5_glu_w, s5_glu_b, hg_lb_logits, hg_norm_w, rg_conv_w, rg_conv_b, rg_wa, rg_ba, rg_wx, rg_bx, rg_lambda, loss_target, m_norm_w, m_final_norm_w, m_ffn_gate, m_ffn_up, m_ffn_down, m_w_in, m_branch_proj, m_w_out, m_s5_lambda_re, m_s5_lambda_im, m_s5_log_dt, m_s5_b_re, m_s5_b_im, m_s5_c_re, m_s5_c_im, m_s5_d, m_s5_glu_w, m_s5_glu_b, m_hg_lb_logits, m_hg_norm_w, m_rg_conv_w, m_rg_conv_b, m_rg_wa, m_rg_ba, m_rg_wx, m_rg_bx, m_rg_lambda, v_norm_w, v_final_norm_w, v_ffn_gate, v_ffn_up, v_ffn_down, v_w_in, v_branch_proj, v_w_out, v_s5_lambda_re, v_s5_lambda_im, v_s5_log_dt, v_s5_b_re, v_s5_b_im, v_s5_c_re, v_s5_c_im, v_s5_d, v_s5_glu_w, v_s5_glu_b, v_hg_lb_logits, v_hg_norm_w, v_rg_conv_w, v_rg_conv_b, v_rg_wa, v_rg_ba, v_rg_wx, v_rg_bx, v_rg_lambda):
    w = dict(zip(WEIGHT_NAMES, (norm_w, final_norm_w, ffn_gate, ffn_up, ffn_down, w_in, branch_proj, w_out,
                                s5_lambda_re, s5_lambda_im, s5_log_dt, s5_b_re, s5_b_im, s5_c_re, s5_c_im, s5_d,
                                s5_glu_w, s5_glu_b, hg_lb_logits, hg_norm_w, rg_conv_w, rg_conv_b, rg_wa, rg_ba,
                                rg_wx, rg_bx, rg_lambda)))
    m = dict(zip(WEIGHT_NAMES, (m_norm_w, m_final_norm_w, m_ffn_gate, m_ffn_up, m_ffn_down, m_w_in, m_branch_proj,
                                m_w_out, m_s5_lambda_re, m_s5_lambda_im, m_s5_log_dt, m_s5_b_re, m_s5_b_im, m_s5_c_re,
                                m_s5_c_im, m_s5_d, m_s5_glu_w, m_s5_glu_b, m_hg_lb_logits, m_hg_norm_w, m_rg_conv_w,
                                m_rg_conv_b, m_rg_wa, m_rg_ba, m_rg_wx, m_rg_bx, m_rg_lambda)))
    v = dict(zip(WEIGHT_NAMES, (v_norm_w, v_final_norm_w, v_ffn_gate, v_ffn_up, v_ffn_down, v_w_in, v_branch_proj,
                                v_w_out, v_s5_lambda_re, v_s5_lambda_im, v_s5_log_dt, v_s5_b_re, v_s5_b_im, v_s5_c_re,
                                v_s5_c_im, v_s5_d, v_s5_glu_w, v_s5_glu_b, v_hg_lb_logits, v_hg_norm_w, v_rg_conv_w,
                                v_rg_conv_b, v_rg_wa, v_rg_ba, v_rg_wx, v_rg_bx, v_rg_lambda)))
    rows = x.shape[1]

    sharded = BIG + SMALL_SHARDED
    gathered = _all_gather("gather_weights", [w[n].astype(bf16) for n in BIG] + [w[n] for n in SMALL_SHARDED])
    full = {n: _unshard(g, SHARDED[n][0]) for n, g in zip(sharded, gathered)}
    big = {SHARDED[n][1]: full[n] for n in BIG}
    big["w_in"] = jnp.concatenate([big["w_in"][..., W_IN_SPLIT:], big["w_in"][..., :W_IN_SPLIT]], axis=-1)
    small = {n: w[n] for n in REPLICATED}
    small["norm_w"] = full["norm_w"]
    small["rg_conv_w"] = full["rg_conv_w"]

    loss_part, dx, g_big, g_small = _local_step(x[0], loss_target[0], big, small)
    g_big["w_in"] = jnp.concatenate([g_big["w_in"][..., GM_WIDTH:], g_big["w_in"][..., :GM_WIDTH]], axis=-1)
    loss = lax.psum(loss_part, ("x", "y", "c"))

    parts = [_to_blocks(g_big[SHARDED[n][1]], SHARDED[n][0]).astype(bf16) for n in BIG]
    parts += [_to_blocks(g_small[n], SHARDED[n][0]) for n in SMALL_SHARDED]
    rep_rows = _pack_rows(g_small, REPLICATED)
    rep_slice = rep_rows.shape[0] // N_DEV
    parts.append(rep_rows.reshape(4, 2, rep_slice, LANES).transpose(1, 0, 2, 3))
    sums = _reduce_scatter(parts)

    grads, delta, new_m, new_v = {}, {}, {}, {}
    for n, (chip_sum, others) in zip(sharded, sums):
        shp = w[n].shape
        view = (-1, shp[-1])
        res = _adamw_reduced("adamw_" + n, w[n].reshape(view), chip_sum, others, m[n].reshape(view), v[n].reshape(view))
        grads[n], delta[n], new_m[n], new_v[n] = (r.reshape(shp) for r in res)
    rep_mine = _own_total("rs_total_small", *sums[-1])
    rep_grads = _all_gather("gather_small_grads", [rep_mine])[0].reshape(-1, LANES)
    res = _adamw("adamw_small", _pack_rows(w, REPLICATED), rep_grads, _pack_rows(m, REPLICATED), _pack_rows(v, REPLICATED))
    for dst, src in zip((grads, delta, new_m, new_v), (rep_grads,) + tuple(res)):
        dst.update(_unpack_rows(src, REPLICATED, w))

    return (loss, dx.reshape(x.shape), *[grads[n] for n in WEIGHT_NAMES], *[delta[n] for n in WEIGHT_NAMES],
            *[new_m[n] for n in WEIGHT_NAMES], *[new_v[n] for n in WEIGHT_NAMES])
```

```python
import functools
import math

import jax
import jax.numpy as jnp
from jax import lax
from jax.experimental import pallas as pl
from jax.experimental.pallas import tpu as pltpu

f32 = jnp.float32
bf16 = jnp.bfloat16

D_MODEL = 1024
DEPTH = 2
BRANCH = 512
N_BRANCH = 3
S5_GROUP = 16
S5_GROUPS = 32
S5_STATE = 64
S5_LANES = S5_GROUPS * S5_STATE
S5_EIG_MAX = -1e-4
HG_HEADS = 4
HG_DK = 128
HG_CHUNK = 32
RG_BLOCKS = 8
RG_BLOCK = 64
RG_C = 8.0
D_FF = 2816
EPS = 1e-6
IN_TOTAL = 6656
GM_WIDTH = N_BRANCH * D_MODEL
N_DEV = 8

ADAM_LR = 0.001
ADAM_B1 = 0.9
ADAM_B2 = 0.999
ADAM_EPS = 1e-08
ADAM_WD = 0.01
ADAM_STEP = 10

VMEM_LIMIT_V7X = 56 * 1024 * 1024
ROW_TILE = 256
FF_TILE = 1408
TOKEN_K = 2048
MXU_COLS = 256


def _cparams(sem):
    return pltpu.CompilerParams(dimension_semantics=sem, vmem_limit_bytes=VMEM_LIMIT_V7X)


MESH_IDS = pl.DeviceIdType.MESH
ANY_SPEC = pl.BlockSpec(memory_space=pl.ANY)


def _place():
    return lax.axis_index("x"), lax.axis_index("y"), lax.axis_index("c")


class _Carry:
    def __init__(self, ins, out_shapes, n_sems, copies):
        self.ins, self.out_shapes, self.n_sems, self.copies = list(ins), list(out_shapes), n_sems, copies
        self.outs = None

    def sems(self):
        return [pltpu.SemaphoreType.DMA((self.n_sems,)), pltpu.SemaphoreType.DMA((self.n_sems,))]

    def start(self, when, *riders):
        @pl.when(when)
        def _():
            for cp in self.copies(*riders):
                cp.start()

    def finish(self, when, *riders):
        @pl.when(when)
        def _():
            for cp in self.copies(*riders):
                cp.wait()


_CARRIED = {}


def _sigmoid(x):
    return 0.5 * jnp.tanh(0.5 * x) + 0.5


def _sigmoid_small(x):
    return 1.0 / (1.0 + jnp.exp(-x))


_GELU_C = math.sqrt(2.0 / math.pi)


def _gelu(x):
    t = jnp.tanh(_GELU_C * (x + 0.044715 * x * x * x))
    return 0.5 * x * (1.0 + t)


def _gelu_grad(x):
    t = jnp.tanh(_GELU_C * (x + 0.044715 * x * x * x))
    return 0.5 * (1.0 + t) + 0.5 * x * (1.0 - t * t) * _GELU_C * (1.0 + 3.0 * 0.044715 * x * x)


def _expm1(x):
    p = x * (1.0 + x * (0.5 + x * (1.0 / 6 + x * (1.0 / 24 + x * (1.0 / 120 + x * (1.0 / 720))))))
    return jnp.where(jnp.abs(x) < 0.3, p, jnp.exp(x) - 1.0)


def _dot(a, b):
    return jnp.dot(a, b, preferred_element_type=f32)


def _dot_nt(a, b):
    return lax.dot_general(a, b, (((1,), (1,)), ((), ())), preferred_element_type=f32)


def _dot_tn(a, b):
    return lax.dot_general(a, b, (((0,), (0,)), ((), ())), preferred_element_type=f32)


def _bdot(a, b):
    return lax.dot_general(a, b, (((2,), (1,)), ((0,), (0,))), preferred_element_type=f32)


def _bdot_nt(a, b):
    return lax.dot_general(a, b, (((2,), (2,)), ((0,), (0,))), preferred_element_type=f32)


def _rows(shape):
    return lax.broadcasted_iota(jnp.int32, shape, 0)


def _scan_fwd(a, b, n):
    row = _rows(a.shape)
    s = 1
    while s < n:
        valid = row >= s
        sh_a = pltpu.roll(a, s, 0)
        sh_b = pltpu.roll(b, s, 0)
        b = b + a * jnp.where(valid, sh_b, 0.0)
        a = a * jnp.where(valid, sh_a, 1.0)
        s *= 2
    return a, b


def _scan_bwd(a, b, n):
    row = _rows(a.shape)
    s = 1
    while s < n:
        valid = row < n - s
        sh_a = pltpu.roll(a, n - s, 0)
        sh_b = pltpu.roll(b, n - s, 0)
        b = b + a * jnp.where(valid, sh_b, 0.0)
        a = a * jnp.where(valid, sh_a, 1.0)
        s *= 2
    return a, b


def _seg_cumsum(x, n, seg):
    pos = _rows(x.shape) % seg
    s = 1
    while s < seg:
        x = x + jnp.where(pos >= s, pltpu.roll(x, s, 0), 0.0)
        s *= 2
    return x


def _seg_rev_cumsum(x, n, seg):
    pos = _rows(x.shape) % seg
    s = 1
    while s < seg:
        x = x + jnp.where(pos < seg - s, pltpu.roll(x, n - s, 0), 0.0)
        s *= 2
    return x


def _head_mean(x):
    parts = []
    for h in range(HG_HEADS):
        m = jnp.mean(x[:, h * HG_DK:(h + 1) * HG_DK], axis=1, keepdims=True)
        parts.append(jnp.broadcast_to(m, (x.shape[0], HG_DK)))
    return jnp.concatenate(parts, axis=1)


def _mm(name, a_list, b_list, terms, n_acc, mode, m, n, k, tm, tn, tk, out_dtypes, epilogue, extras=(), vecs=(),
        n_part=0, chunk=0):
    tm, tn, tk = min(tm, m), min(tn, n), min(tk, k)
    assert m % tm == 0 and n % tn == 0 and k % tk == 0, (name, m, n, k, tm, tn, tk)
    gk = k // tk
    if mode == "tn":
        a_spec = pl.BlockSpec((tk, tm), lambda i, j, kk: (kk, i))
    else:
        a_spec = pl.BlockSpec((tm, tk), lambda i, j, kk: (i, kk))
    if mode == "nt":
        b_spec = pl.BlockSpec((tn, tk), lambda i, j, kk: (j, kk))
    else:
        b_spec = pl.BlockSpec((tk, tn), lambda i, j, kk: (kk, j))
    o_spec = pl.BlockSpec((tm, tn), lambda i, j, kk: (i, j))
    v_spec = pl.BlockSpec((1, tn), lambda i, j, kk: (0, j))
    p_spec = pl.BlockSpec((None, 1, tn), lambda i, j, kk: (i, 0, j))
    dot = {"nn": _dot, "nt": _dot_nt, "tn": _dot_tn}[mode]
    na, nb, ne, nv, no = len(a_list), len(b_list), len(extras), len(vecs), len(out_dtypes)
    carry = _CARRIED.pop(name, None)
    nci, nco = (len(carry.ins), len(carry.out_shapes)) if carry else (0, 0)
    n_in = na + nb + ne + nv + nci
    grid = (m // tm, n // tn, gk)

    def kern(*refs):
        if carry:
            ids = [pl.program_id(d) for d in range(3)]
            riders = (refs[n_in - nci:n_in], refs[n_in + no + n_part:n_in + no + n_part + nco]) + tuple(refs[-2:])
            carry.start(functools.reduce(jnp.logical_and, [p == 0 for p in ids]), *riders)
        compute(*refs)
        if carry:
            carry.finish(functools.reduce(jnp.logical_and, [p == g - 1 for p, g in zip(ids, grid)]), *riders)

    def compute(*refs):
        a_refs = refs[:na]
        b_refs = refs[na:na + nb]
        e_refs = refs[na + nb:na + nb + ne]
        v_refs = refs[na + nb + ne:na + nb + ne + nv]
        o_refs = refs[n_in:n_in + no + n_part]

        def finish(accs):
            outs = epilogue(accs, [e[...] for e in e_refs], [r[...] for r in v_refs])
            for o, val in zip(o_refs, outs):
                o[...] = val.astype(o.dtype)

        def partial_sums():
            sums = [None] * n_acc
            for ai, bi, ci in terms:
                d = dot(a_refs[ai][...].astype(bf16), b_refs[bi][...].astype(bf16))
                sums[ci] = d if sums[ci] is None else sums[ci] + d
            return sums

        if gk == 1 and chunk:
            assert mode in ("nn", "nt") and tn % chunk == 0
            for c0 in range(0, tn, chunk):
                cols = slice(c0, c0 + chunk)
                sums = [None] * n_acc
                for ai, bi, ci in terms:
                    b_part = b_refs[bi][:, cols] if mode == "nn" else b_refs[bi][cols, :]
                    d = dot(a_refs[ai][...].astype(bf16), b_part.astype(bf16))
                    sums[ci] = d if sums[ci] is None else sums[ci] + d
                outs = epilogue(sums, [e[:, cols] for e in e_refs], [r[:, cols] for r in v_refs])
                for o, val in zip(o_refs, outs):
                    o[:, cols] = val.astype(o.dtype)
            return
        if gk == 1:
            finish(partial_sums())
            return
        acc = refs[n_in + no + n_part + nco]
        kk = pl.program_id(2)

        @pl.when(kk == 0)
        def _():
            acc[...] = jnp.zeros_like(acc)

        for ci, d in enumerate(partial_sums()):
            acc[ci] += d

        @pl.when(kk == gk - 1)
        def _():
            finish([acc[c] for c in range(n_acc)])

    res = pl.pallas_call(
        kern, name=name,
        grid=grid,
        in_specs=[a_spec] * na + [b_spec] * nb + [o_spec] * ne + [v_spec] * nv + [ANY_SPEC] * nci,
        out_specs=[o_spec] * no + [p_spec] * n_part + [ANY_SPEC] * nco,
        out_shape=([jax.ShapeDtypeStruct((m, n), dt) for dt in out_dtypes]
                   + [jax.ShapeDtypeStruct((m // tm, 1, n), f32)] * n_part + (carry.out_shapes if carry else [])),
        scratch_shapes=([pltpu.VMEM((n_acc, tm, tn), f32)] if gk > 1 else []) + (carry.sems() if carry else []),
        compiler_params=_cparams(("arbitrary",) * 3 if carry else ("parallel", "parallel", "arbitrary")),
    )(*a_list, *b_list, *extras, *vecs, *(carry.ins if carry else []))
    if carry:
        carry.outs = res[no + n_part:]
        res = res[:no + n_part]
    return res


def _mm1(name, a, b, mode, m, n, k, tm, tn, tk, out_dtype=f32, scale=None):
    def epi(accs, extras, vecs):
        return [accs[0] if scale is None else accs[0] * scale]
    return _mm(name, [a], [b], [(0, 0, 0)], 1, mode, m, n, k, tm, tn, tk, [out_dtype], epi)[0]


def _rt(name, body, rows, tm, row_ins, consts, row_outs, acc_outs=(), scratch=(), reverse=False):
    tm = min(tm, rows)
    assert rows % tm == 0
    nt = rows // tm

    def tile(i):
        return nt - 1 - i if reverse else i

    in_specs, args = [], []
    for spec in row_ins:
        arr = spec[0]
        if isinstance(spec[1], int):
            in_specs.append(pl.BlockSpec((tm, spec[1]), lambda i, cb=spec[2]: (tile(i), cb)))
        else:
            in_specs.append(pl.BlockSpec(spec[1], lambda i, fn=spec[2]: fn(tile(i))))
        args.append(arr)
    for c in consts:
        in_specs.append(pl.BlockSpec(c.shape, lambda i, nd=c.ndim: (0,) * nd))
        args.append(c)
    out_specs, out_shape = [], []
    for spec in row_outs:
        if isinstance(spec[0], int):
            out_specs.append(pl.BlockSpec((tm, spec[0]), lambda i: (tile(i), 0)))
            out_shape.append(jax.ShapeDtypeStruct((rows, spec[0]), spec[1]))
        else:
            out_specs.append(pl.BlockSpec(spec[1], lambda i, fn=spec[2]: fn(tile(i))))
            out_shape.append(jax.ShapeDtypeStruct(spec[0], spec[3]))
    for shp in acc_outs:
        out_specs.append(pl.BlockSpec(shp, lambda i, nd=len(shp): (0,) * nd))
        out_shape.append(jax.ShapeDtypeStruct(shp, f32))
    n_in = len(args)
    n_row_out = len(row_outs)
    n_acc = len(acc_outs)
    n_out = n_row_out + n_acc
    carry = _CARRIED.pop(name, None)
    nci, nco = (len(carry.ins), len(carry.out_shapes)) if carry else (0, 0)

    def kern(*refs):
        i = pl.program_id(0)
        if carry:
            own = refs[:n_in] + refs[n_in + nci:n_in + nci + n_out] + refs[n_in + nci + n_out + nco:-2]
            riders = (refs[n_in:n_in + nci], refs[n_in + nci + n_out:n_in + nci + n_out + nco]) + tuple(refs[-2:])
            carry.start(i == 0, *riders)
        else:
            own = refs
        acc_refs = own[n_in + n_row_out:n_in + n_out]

        @pl.when(i == 0)
        def _():
            for r in acc_refs:
                r[...] = jnp.zeros_like(r)

        body(i, *own)
        if carry:
            carry.finish(i == nt - 1, *riders)

    res = pl.pallas_call(
        kern, name=name, grid=(nt,), in_specs=in_specs + [ANY_SPEC] * nci, out_specs=out_specs + [ANY_SPEC] * nco,
        out_shape=out_shape + (carry.out_shapes if carry else []),
        scratch_shapes=list(scratch) + (carry.sems() if carry else []), compiler_params=_cparams(("arbitrary",)),
    )(*args, *(carry.ins if carry else []))
    if carry:
        carry.outs = res[n_out:]
        res = res[:n_out]
    return res


def _rms_rows(xv, wv):
    r = lax.rsqrt(jnp.mean(xv * xv, axis=1, keepdims=True) + EPS)
    return (xv * r * wv).astype(bf16)


def _rms_bwd_rows(xv, dhv, wv, dres):
    r = lax.rsqrt(jnp.mean(xv * xv, axis=1, keepdims=True) + EPS)
    xn = xv * r
    dxn = dhv * wv
    dx = dres + r * (dxn - xn * jnp.mean(dxn * xn, axis=1, keepdims=True))
    return [dx, dx.astype(bf16), jnp.sum(dhv * xn, axis=0, keepdims=True)]


def _rms_fwd(name, x, w, rows):
    def body(i, x_ref, w_ref, h_ref):
        h_ref[...] = _rms_rows(x_ref[...], w_ref[...])
    return _rt(name, body, rows, ROW_TILE, [(x, D_MODEL, 0)], [w], [(D_MODEL, bf16)])[0]


def _residual_then_norm(scale):
    def epi(accs, extras, vecs):
        x_out = extras[0] + scale * accs[0]
        return [x_out] + [_rms_rows(x_out, v) for v in vecs]
    return epi


def _norm_bwd_epilogue(accs, extras, vecs):
    return _rms_bwd_rows(extras[0], accs[0], vecs[0], extras[1])


def _loss_head(x, w, target, rows):
    def body(i, x_ref, t_ref, w_ref, dx_ref, dxb_ref, loss_ref, dw_ref):
        xv = x_ref[...]
        r = lax.rsqrt(jnp.mean(xv * xv, axis=1, keepdims=True) + EPS)
        xn = xv * r
        wv = w_ref[...]
        err = xn * wv - t_ref[...]
        part = 0.5 * jnp.sum(jnp.mean(err * err, axis=1, keepdims=True), axis=0, keepdims=True)
        loss_ref[...] += jnp.broadcast_to(part, (1, 128))
        dy = err * (1.0 / D_MODEL)
        dxn = dy * wv
        dx = r * (dxn - xn * jnp.mean(dxn * xn, axis=1, keepdims=True))
        dx_ref[...] = dx
        dxb_ref[...] = dx.astype(bf16)
        dw_ref[...] += jnp.sum(dy * xn, axis=0, keepdims=True)
    return _rt("loss_head", body, rows, ROW_TILE, [(x, D_MODEL, 0), (target, D_MODEL, 0)], [w],
               [(D_MODEL, f32), (D_MODEL, bf16)], acc_outs=[(1, 128), (1, D_MODEL)])


def _ffn_fwd(tag, x, hb, wg, wu, wd, next_nw, rows):
    def epi_up(accs, extras, vecs):
        a, b = accs
        return [a, b, a * _sigmoid(a) * b]
    a, b, s = _mm("ffn_up_" + tag, [hb], [wg, wu], [(0, 0, 0), (0, 1, 1)], 2, "nn", rows, D_FF, D_MODEL,
                  512, D_FF, D_MODEL, [bf16, bf16, bf16], epi_up, chunk=MXU_COLS)
    outs = _mm("ffn_down_" + tag, [s], [wd], [(0, 0, 0)], 1, "nn", rows, D_MODEL, D_FF,
               512, D_MODEL, D_FF, [f32] + [bf16] * len(next_nw), _residual_then_norm(0.5), extras=[x],
               vecs=next_nw)
    return outs[0], (outs[1] if next_nw else None), (x, hb, a, b, s)


def _ffn_bwd(tag, saved, nw, wg, wu, wd, dx, dxb, rows):
    x, hb, a, b, s = saved

    def epi_mid(accs, extras, vecs):
        ds = 0.5 * accs[0]
        av = extras[0].astype(f32)
        bv = extras[1].astype(f32)
        sg = _sigmoid(av)
        return [ds * bv * sg * (1.0 + av * (1.0 - sg)), ds * av * sg]
    da, db = _mm("ffn_bwd_mid_" + tag, [dxb], [wd], [(0, 0, 0)], 1, "nt", rows, D_FF, D_MODEL,
                 512, D_FF, D_MODEL, [bf16, bf16], epi_mid, extras=[a, b], chunk=MXU_COLS)
    d_wd = _mm1("ffn_dwd_" + tag, s, dxb, "tn", D_FF, D_MODEL, rows, FF_TILE, D_MODEL, TOKEN_K, out_dtype=bf16,
                scale=0.5)
    d_wg = _mm1("ffn_dwg_" + tag, hb, da, "tn", D_MODEL, D_FF, rows, D_MODEL, FF_TILE, TOKEN_K, out_dtype=bf16)
    d_wu = _mm1("ffn_dwu_" + tag, hb, db, "tn", D_MODEL, D_FF, rows, D_MODEL, FF_TILE, TOKEN_K, out_dtype=bf16)
    dx_in, dxb_in, d_nw = _mm("ffn_dh_" + tag, [da, db], [wg, wu], [(0, 0, 0), (1, 1, 0)], 1, "nt", rows, D_MODEL,
                              D_FF, 512, D_MODEL, FF_TILE, [f32, bf16], _norm_bwd_epilogue, extras=[x, dx], vecs=[nw],
                              n_part=1)
    return dx_in, dxb_in, jnp.sum(d_nw, axis=0), d_wg, d_wu, d_wd


S5_CB = 512
SUBLANES = 8
U_COL = GM_WIDTH // BRANCH


def _s5_scan_fwd(tag, proj, b_re, b_im, a_re, a_im, rows):
    tm = min(ROW_TILE, rows)
    nt = rows // tm
    nc = S5_LANES // S5_CB

    def kern(u_ref, bre_ref, bim_ref, ar_ref, ai_ref, xr_ref, xi_ref, pr_s, pi_s, cr_s, ci_s):
        t = pl.program_id(1)

        @pl.when(t == 0)
        def _():
            row8 = _rows((SUBLANES, S5_CB))
            pr = jnp.broadcast_to(ar_ref[...], (SUBLANES, S5_CB))
            pi = jnp.broadcast_to(ai_ref[...], (SUBLANES, S5_CB))
            s = 1
            while s < SUBLANES:
                sr = pltpu.roll(pr, s, 0)
                si = pltpu.roll(pi, s, 0)
                valid = row8 >= s
                pr, pi = jnp.where(valid, pr * sr - pi * si, pr), jnp.where(valid, pr * si + pi * sr, pi)
                s *= 2
            pr_s[...] = pr
            pi_s[...] = pi
            cr_s[...] = jnp.zeros_like(cr_s)
            ci_s[...] = jnp.zeros_like(ci_s)

        ub = u_ref[...].astype(bf16)
        br = _dot(ub, bre_ref[...])
        bi = _dot(ub, bim_ref[...])
        pos = _rows((tm, S5_CB)) % SUBLANES
        s = 1
        while s < SUBLANES:
            mr = pr_s[s - 1:s, :]
            mi = pi_s[s - 1:s, :]
            sr = pltpu.roll(br, s, 0)
            si = pltpu.roll(bi, s, 0)
            valid = pos >= s
            br, bi = (br + jnp.where(valid, mr * sr - mi * si, 0.0),
                      bi + jnp.where(valid, mr * si + mi * sr, 0.0))
            s *= 2
        cr = cr_s[...]
        ci = ci_s[...]
        pr = pr_s[...]
        pi = pi_s[...]
        for g in range(tm // SUBLANES):
            sl = slice(g * SUBLANES, (g + 1) * SUBLANES)
            xr = br[sl] + pr * cr - pi * ci
            xi = bi[sl] + pr * ci + pi * cr
            xr_ref[sl, :] = xr
            xi_ref[sl, :] = xi
            cr = xr[SUBLANES - 1:SUBLANES, :]
            ci = xi[SUBLANES - 1:SUBLANES, :]
        cr_s[...] = cr
        ci_s[...] = ci

    return pl.pallas_call(
        kern, name="s5_scan_fwd_" + tag, grid=(nc, nt),
        in_specs=[pl.BlockSpec((tm, BRANCH), lambda c, t: (t, U_COL)),
                  pl.BlockSpec((BRANCH, S5_CB), lambda c, t: (0, c)),
                  pl.BlockSpec((BRANCH, S5_CB), lambda c, t: (0, c)),
                  pl.BlockSpec((1, S5_CB), lambda c, t: (0, c)),
                  pl.BlockSpec((1, S5_CB), lambda c, t: (0, c))],
        out_specs=[pl.BlockSpec((tm, S5_CB), lambda c, t: (t, c))] * 2,
        out_shape=[jax.ShapeDtypeStruct((rows, S5_LANES), f32)] * 2,
        scratch_shapes=[pltpu.VMEM((SUBLANES, S5_CB), f32), pltpu.VMEM((SUBLANES, S5_CB), f32),
                        pltpu.VMEM((1, S5_CB), f32), pltpu.VMEM((1, S5_CB), f32)],
        compiler_params=_cparams(("parallel", "arbitrary")),
    )(proj, b_re, b_im, a_re, a_im)


def _s5_scan_bwd(tag, dxr, dxi, xr, xi, a_re, a_im, rows):
    tm = min(ROW_TILE, rows)
    nt = rows // tm
    nc = S5_LANES // S5_CB

    def kern(dxr_ref, dxi_ref, xr_ref, xi_ref, ar_ref, ai_ref, gr_ref, gi_ref, dar_ref, dai_ref,
             qr_s, qi_s, cr_s, ci_s, gr_s, gi_s):
        t = pl.program_id(1)
        row = _rows((tm, S5_CB))

        @pl.when(t == 0)
        def _():
            row8 = _rows((SUBLANES, S5_CB))
            qr = jnp.broadcast_to(ar_ref[...], (SUBLANES, S5_CB))
            qi = jnp.broadcast_to(-ai_ref[...], (SUBLANES, S5_CB))
            s = 1
            while s < SUBLANES:
                sr = pltpu.roll(qr, SUBLANES - s, 0)
                si = pltpu.roll(qi, SUBLANES - s, 0)
                valid = row8 < SUBLANES - s
                qr, qi = jnp.where(valid, qr * sr - qi * si, qr), jnp.where(valid, qr * si + qi * sr, qi)
                s *= 2
            qr_s[...] = qr
            qi_s[...] = qi
            cr_s[...] = jnp.zeros_like(cr_s)
            ci_s[...] = jnp.zeros_like(ci_s)
            dar_ref[...] = jnp.zeros_like(dar_ref)
            dai_ref[...] = jnp.zeros_like(dai_ref)

        br = dxr_ref[...]
        bi = dxi_ref[...]
        pos = row % SUBLANES
        s = 1
        while s < SUBLANES:
            mr = qr_s[SUBLANES - s:SUBLANES - s + 1, :]
            mi = qi_s[SUBLANES - s:SUBLANES - s + 1, :]
            sr = pltpu.roll(br, tm - s, 0)
            si = pltpu.roll(bi, tm - s, 0)
            valid = pos < SUBLANES - s
            br, bi = (br + jnp.where(valid, mr * sr - mi * si, 0.0),
                      bi + jnp.where(valid, mr * si + mi * sr, 0.0))
            s *= 2
        cin_r = cr_s[...]
        cin_i = ci_s[...]
        cr, ci = cin_r, cin_i
        qr = qr_s[...]
        qi = qi_s[...]
        for g in reversed(range(tm // SUBLANES)):
            sl = slice(g * SUBLANES, (g + 1) * SUBLANES)
            gr = br[sl] + qr * cr - qi * ci
            gi = bi[sl] + qr * ci + qi * cr
            gr_s[sl, :] = gr
            gi_s[sl, :] = gi
            cr = gr[0:1, :]
            ci = gi[0:1, :]
        cr_s[...] = cr
        ci_s[...] = ci
        gr = gr_s[...]
        gi = gi_s[...]
        gr_ref[...] = gr.astype(bf16)
        gi_ref[...] = gi.astype(bf16)
        last = row == tm - 1
        gnr = jnp.where(last, cin_r, pltpu.roll(gr, tm - 1, 0))
        gni = jnp.where(last, cin_i, pltpu.roll(gi, tm - 1, 0))
        xr_v = xr_ref[...]
        xi_v = xi_ref[...]
        dar_ref[...] += jnp.sum(gnr * xr_v + gni * xi_v, axis=0, keepdims=True)
        dai_ref[...] += jnp.sum(gni * xr_v - gnr * xi_v, axis=0, keepdims=True)

    rev = lambda c, t: (nt - 1 - t, c)
    return pl.pallas_call(
        kern, name="s5_scan_bwd_" + tag, grid=(nc, nt),
        in_specs=[pl.BlockSpec((tm, S5_CB), rev)] * 4 + [pl.BlockSpec((1, S5_CB), lambda c, t: (0, c))] * 2,
        out_specs=[pl.BlockSpec((tm, S5_CB), rev)] * 2 + [pl.BlockSpec((1, S5_CB), lambda c, t: (0, c))] * 2,
        out_shape=[jax.ShapeDtypeStruct((rows, S5_LANES), bf16)] * 2 + [jax.ShapeDtypeStruct((1, S5_LANES), f32)] * 2,
        scratch_shapes=[pltpu.VMEM((SUBLANES, S5_CB), f32), pltpu.VMEM((SUBLANES, S5_CB), f32),
                        pltpu.VMEM((1, S5_CB), f32), pltpu.VMEM((1, S5_CB), f32),
                        pltpu.VMEM((tm, S5_CB), f32), pltpu.VMEM((tm, S5_CB), f32)],
        compiler_params=_cparams(("parallel", "arbitrary")),
    )(dxr, dxi, xr, xi, a_re, a_im)


def _s5_fwd(tag, proj, cst, rows):
    xr, xi = _s5_scan_fwd(tag, proj, cst["b_re"].astype(bf16), cst["b_im"].astype(bf16), cst["a_re"], cst["a_im"], rows)

    def body(i, xr_ref, xi_ref, u_ref, cre_ref, cim_ref, d_ref, gw_ref, gb_ref, y_ref, out_ref):
        y = (_dot(xr_ref[...].astype(bf16), cre_ref[...]) + _dot(xi_ref[...].astype(bf16), cim_ref[...])
             + d_ref[...] * u_ref[...])
        y_ref[...] = y
        z = _gelu(y)
        zg = _dot(z.astype(bf16), gw_ref[...]) + gb_ref[...]
        out_ref[...] = (z * _sigmoid(zg)).astype(bf16)

    y, out = _rt("s5_out_" + tag, body, rows, ROW_TILE,
                 [(xr, S5_LANES, 0), (xi, S5_LANES, 0), (proj, BRANCH, U_COL)],
                 [cst["c_re"].astype(bf16), cst["c_im"].astype(bf16), cst["s5_d"], cst["glu_w"], cst["glu_b"]],
                 [(BRANCH, f32), (BRANCH, bf16)])
    return out, (xr, xi, y)


def _s5_bwd(tag, saved, proj, cst, d_out, rows):
    xr, xi, y = saved
    c_re = cst["c_re"].astype(bf16)
    c_im = cst["c_im"].astype(bf16)

    def body(i, do_ref, y_ref, u_ref, xr_ref, xi_ref, cre_ref, cim_ref, gw_ref, gb_ref,
             dxr_ref, dxi_ref, dy_ref, dgw_ref, dgb_ref, dd_ref, dcre_ref, dcim_ref):
        yv = y_ref[...]
        z = _gelu(yv)
        zb = z.astype(bf16)
        gt = _sigmoid(_dot(zb, gw_ref[...]) + gb_ref[...])
        dov = do_ref[...]
        dzg = dov * z * gt * (1.0 - gt)
        dzgb = dzg.astype(bf16)
        dz = dov * gt + _dot_nt(dzgb, gw_ref[...])
        dgw_ref[...] += _dot_tn(zb, dzgb)
        dgb_ref[...] += jnp.sum(dzg, axis=0, keepdims=True)
        dy = dz * _gelu_grad(yv)
        dy_ref[...] = dy
        dd_ref[...] += jnp.sum(dy * u_ref[...], axis=0, keepdims=True)
        dyb = dy.astype(bf16)
        dxr_ref[...] = _dot_nt(dyb, cre_ref[...])
        dxi_ref[...] = _dot_nt(dyb, cim_ref[...])
        dcre_ref[...] += _dot_tn(xr_ref[...].astype(bf16), dyb)
        dcim_ref[...] += _dot_tn(xi_ref[...].astype(bf16), dyb)

    dxr, dxi, dy, d_gw, d_gb, d_d, d_cre, d_cim = _rt(
        "s5_out_bwd_" + tag, body, rows, ROW_TILE,
        [(d_out, BRANCH, 0), (y, BRANCH, 0), (proj, BRANCH, U_COL), (xr, S5_LANES, 0), (xi, S5_LANES, 0)],
        [c_re, c_im, cst["glu_w"], cst["glu_b"]],
        [(S5_LANES, f32), (S5_LANES, f32), (BRANCH, f32)],
        acc_outs=[(BRANCH, BRANCH), (1, BRANCH), (1, BRANCH), (S5_LANES, BRANCH), (S5_LANES, BRANCH)])

    gr, gi, d_ar, d_ai = _s5_scan_bwd(tag, dxr, dxi, xr, xi, cst["a_re"], cst["a_im"], rows)
    b_re = cst["b_re"].astype(bf16)
    b_im = cst["b_im"].astype(bf16)

    def body_in(i, gr_ref, gi_ref, dy_ref, u_ref, bre_ref, bim_ref, d_ref, du_ref, dbre_ref, dbim_ref):
        grv = gr_ref[...]
        giv = gi_ref[...]
        du = _dot_nt(grv, bre_ref[...]) + _dot_nt(giv, bim_ref[...]) + dy_ref[...] * d_ref[...]
        du_ref[...] = du.astype(bf16)
        ub = u_ref[...].astype(bf16)
        dbre_ref[...] += _dot_tn(ub, grv)
        dbim_ref[...] += _dot_tn(ub, giv)

    du, d_bre, d_bim = _rt("s5_in_bwd_" + tag, body_in, rows, ROW_TILE,
                           [(gr, S5_LANES, 0), (gi, S5_LANES, 0), (dy, BRANCH, 0), (proj, BRANCH, U_COL)],
                           [b_re, b_im, cst["s5_d"]], [(BRANCH, bf16)],
                           acc_outs=[(BRANCH, S5_LANES), (BRANCH, S5_LANES)])
    dcst = {"b_re": d_bre, "b_im": d_bim, "a_re": d_ar, "a_im": d_ai, "c_re": d_cre, "c_im": d_cim,
            "s5_d": d_d, "glu_b": d_gb}
    return du, dcst, d_gw


def _hg_prep(q, z, lb):
    qs = _sigmoid(q)
    qh = q * qs
    sg = _sigmoid_small(z)
    fg = lb + (1.0 - lb) * sg
    kk = (1.0 - lb) * (1.0 - sg)
    return qs, qh, sg, fg, kk


def _hg_fwd(tag, proj, cst, rows):
    tm = min(ROW_TILE, rows)
    c_sz = HG_CHUNK
    nch = tm // c_sz
    n_chunks = rows // c_sz

    def body(i, q_ref, z_ref, v_ref, g_ref, lb_ref, nw_ref, out_ref, o_ref, ss_ref, sn_ref, st_s):
        @pl.when(i == 0)
        def _():
            st_s[...] = jnp.zeros_like(st_s)

        lb = lb_ref[...]
        _, qh, sg, fg, kk = _hg_prep(q_ref[...], z_ref[...], lb)
        b = _seg_cumsum(jnp.log(fg), tm, c_sz)
        qhat = (qh * jnp.exp(b)).astype(bf16)
        khat = (kk * jnp.exp(-b)).astype(bf16)
        vb = v_ref[...].astype(bf16)
        b3 = b.reshape(nch, c_sz, BRANCH)
        bl3 = b3[:, c_sz - 1:c_sz, :]
        kdec = (kk.reshape(nch, c_sz, BRANCH) * jnp.exp(bl3 - b3)).astype(bf16)
        ebl = jnp.exp(bl3)
        tril = (lax.broadcasted_iota(jnp.int32, (nch, c_sz, c_sz), 1)
                >= lax.broadcasted_iota(jnp.int32, (nch, c_sz, c_sz), 2))
        o_heads = []
        for h in range(HG_HEADS):
            hl = slice(h * HG_DK, (h + 1) * HG_DK)
            q3 = qhat[:, hl].reshape(nch, c_sz, HG_DK)
            k3 = khat[:, hl].reshape(nch, c_sz, HG_DK)
            v3 = vb[:, hl].reshape(nch, c_sz, HG_DK)
            a_mat = jnp.where(tril, _bdot_nt(q3, k3), 0.0).astype(bf16)
            o3 = _bdot(a_mat, v3)
            st = st_s[hl, :]
            before = []
            for ci in range(nch):
                before.append(st.astype(bf16))
                st = st * ebl[ci][:, hl] + _dot_tn(v3[ci], kdec[ci][:, hl])
                sn_ref[ci, hl, :] = st.astype(bf16)
            st_s[hl, :] = st
            s3 = jnp.stack(before)
            ss_ref[:, hl, :] = s3
            o3 = o3 + _bdot_nt(q3, s3)
            o_heads.append(o3.reshape(tm, HG_DK))
        o = jnp.concatenate(o_heads, axis=1)
        o_ref[...] = o
        r = lax.rsqrt(_head_mean(o * o) + EPS)
        g = g_ref[...]
        out_ref[...] = (o * r * nw_ref[...] * (g * _sigmoid(g))).astype(bf16)

    out, o, ss, sn = _rt(
        "hg_fwd_" + tag, body, rows, tm,
        [(proj, BRANCH, U_COL + 1), (proj, BRANCH, U_COL + 2), (proj, BRANCH, U_COL + 3), (proj, BRANCH, U_COL + 4)],
        [cst["hg_lb"], cst["hg_nw"]],
        [(BRANCH, bf16), (BRANCH, f32),
         ((n_chunks, BRANCH, HG_DK), (nch, BRANCH, HG_DK), lambda t: (t, 0, 0), bf16),
         ((n_chunks, BRANCH, HG_DK), (nch, BRANCH, HG_DK), lambda t: (t, 0, 0), bf16)],
        scratch=[pltpu.VMEM((BRANCH, HG_DK), f32)])
    return out, (o, ss, sn)


def _hg_bwd(tag, saved, proj, cst, d_out, rows):
    o_saved, ss, sn = saved
    tm = min(ROW_TILE, rows)
    c_sz = HG_CHUNK
    nch = tm // c_sz

    def body(i, do_ref, q_ref, z_ref, v_ref, g_ref, o_ref, ss_ref, sn_ref, lb_ref, nw_ref,
             dq_ref, dz_ref, dv_ref, dg_ref, dlb_ref, dnw_ref, dst_s):
        @pl.when(i == 0)
        def _():
            dst_s[...] = jnp.zeros_like(dst_s)

        lb = lb_ref[...]
        q = q_ref[...]
        qs, qh, sg, fg, kk = _hg_prep(q, z_ref[...], lb)
        b = _seg_cumsum(jnp.log(fg), tm, c_sz)
        eb = jnp.exp(b)
        enb = jnp.exp(-b)
        qhat = (qh * eb).astype(bf16)
        khat = (kk * enb).astype(bf16)
        vb = v_ref[...].astype(bf16)
        b3 = b.reshape(nch, c_sz, BRANCH)
        bl3 = b3[:, c_sz - 1:c_sz, :]
        dec3 = jnp.exp(bl3 - b3)
        kdec = (kk.reshape(nch, c_sz, BRANCH) * dec3).astype(bf16)
        ebl = jnp.exp(bl3)
        g = g_ref[...]
        gs = _sigmoid(g)
        o = o_ref[...]
        r = lax.rsqrt(_head_mean(o * o) + EPS)
        oh = o * r
        nw = nw_ref[...]
        dov = do_ref[...]
        don = dov * (g * gs)
        dg_ref[...] = (dov * oh * nw * (gs * (1.0 + g * (1.0 - gs)))).astype(bf16)
        dnw_ref[...] += jnp.sum(don * oh, axis=0, keepdims=True)
        doh = don * nw
        d_o = r * (doh - oh * _head_mean(doh * oh))
        dob = d_o.astype(bf16)
        t_idx = lax.broadcasted_iota(jnp.int32, (nch, c_sz, c_sz), 1)
        s_idx = lax.broadcasted_iota(jnp.int32, (nch, c_sz, c_sz), 2)
        heads = []
        for h in range(HG_HEADS):
            hl = slice(h * HG_DK, (h + 1) * HG_DK)
            q3 = qhat[:, hl].reshape(nch, c_sz, HG_DK)
            k3 = khat[:, hl].reshape(nch, c_sz, HG_DK)
            v3 = vb[:, hl].reshape(nch, c_sz, HG_DK)
            do3 = dob[:, hl].reshape(nch, c_sz, HG_DK)
            s3 = ss_ref[:, hl, :]
            da_mat = jnp.where(t_idx >= s_idx, _bdot_nt(do3, v3), 0.0).astype(bf16)
            a_t = jnp.where(t_idx <= s_idx, _bdot_nt(k3, q3), 0.0).astype(bf16)
            da_t = jnp.where(t_idx <= s_idx, _bdot_nt(v3, do3), 0.0).astype(bf16)
            dqhat = _bdot(do3, s3) + _bdot(da_mat, k3)
            dkhat = _bdot(da_t, q3)
            dst = dst_s[hl, :]
            after = [None] * nch
            for ci in reversed(range(nch)):
                after[ci] = dst
                dst = dst * ebl[ci][:, hl] + _dot_tn(do3[ci], q3[ci])
            dst_s[hl, :] = dst
            ds3 = jnp.stack(after)
            ds3b = ds3.astype(bf16)
            dk_inter = _bdot(v3, ds3b) * dec3[:, :, hl]
            dv3 = _bdot(a_t, do3) + _bdot_nt(kdec[:, :, hl], ds3b)
            flux = jnp.sum(sn_ref[:, hl, :].astype(f32) * ds3, axis=1, keepdims=True)
            heads.append((dqhat.reshape(tm, HG_DK), dkhat.reshape(tm, HG_DK), dk_inter.reshape(tm, HG_DK),
                          dv3.reshape(tm, HG_DK), jnp.broadcast_to(flux, (nch, c_sz, HG_DK)).reshape(tm, HG_DK)))
        dqhat, dkhat, dk_inter, dv, flux = (jnp.concatenate(parts, axis=1) for parts in zip(*heads))
        dv_ref[...] = dv.astype(bf16)
        dqh = dqhat * eb
        dk = dkhat * enb + dk_inter
        db = qhat.astype(f32) * dqhat - khat.astype(f32) * dkhat - kk * dk_inter
        dlf = _seg_rev_cumsum(db, tm, c_sz) + flux
        tt = (1.0 - lb) * sg * (1.0 - sg)
        dz_ref[...] = (dlf * tt / fg - dk * tt).astype(bf16)
        dlb_ref[...] += jnp.sum(dlf * (1.0 - sg) / fg - dk * (1.0 - sg), axis=0, keepdims=True)
        dq_ref[...] = (dqh * (qs * (1.0 + q * (1.0 - qs)))).astype(bf16)

    dq, dz, dv, dg, d_lb, d_nw = _rt(
        "hg_bwd_" + tag, body, rows, tm,
        [(d_out, BRANCH, 0), (proj, BRANCH, U_COL + 1), (proj, BRANCH, U_COL + 2), (proj, BRANCH, U_COL + 3),
         (proj, BRANCH, U_COL + 4), (o_saved, BRANCH, 0), (ss, (nch, BRANCH, HG_DK), lambda t: (t, 0, 0)),
         (sn, (nch, BRANCH, HG_DK), lambda t: (t, 0, 0))],
        [cst["hg_lb"], cst["hg_nw"]],
        [(BRANCH, bf16)] * 4, acc_outs=[(1, BRANCH), (1, BRANCH)],
        scratch=[pltpu.VMEM((BRANCH, HG_DK), f32)],
        reverse=True)
    return dq, dz, dv, dg, {"hg_lb": d_lb, "hg_nw": d_nw}


def _rg_gates(xc, wa_ref, ba_ref, wx_ref, bx_ref, sp8):
    xcb = xc.astype(bf16)
    r = _sigmoid(_dot(xcb, wa_ref[...]) + ba_ref[...])
    ig = _sigmoid(_dot(xcb, wx_ref[...]) + bx_ref[...])
    la = -sp8 * r
    a = jnp.exp(la)
    mult = jnp.sqrt(-_expm1(2.0 * la))
    return xcb, r, ig, a, mult


def _rg_fwd(tag, proj, cst, rows):
    tm = min(ROW_TILE, rows)

    def body(i, xb_ref, gate_ref, cw_ref, cb_ref, wa_ref, ba_ref, wx_ref, bx_ref, sp_ref,
             out_ref, xc_ref, h_ref, hp_ref, prev_s, hc_s):
        @pl.when(i == 0)
        def _():
            prev_s[...] = jnp.zeros_like(prev_s)
            hc_s[...] = jnp.zeros_like(hc_s)

        row = _rows((tm, BRANCH))
        xb = xb_ref[...]
        prev = prev_s[...]
        xc = cb_ref[...] + cw_ref[3:4, :] * xb
        for j in range(1, 4):
            sh = jnp.where(row >= j, pltpu.roll(xb, j, 0), pltpu.roll(prev, j, 0))
            xc = xc + cw_ref[3 - j:4 - j, :] * sh
        prev_s[...] = xb
        xc_ref[...] = xc
        _, r, ig, a, mult = _rg_gates(xc, wa_ref, ba_ref, wx_ref, bx_ref, sp_ref[...])
        a_cum, h_loc = _scan_fwd(a, mult * ig * xc, tm)
        hc = hc_s[...]
        h = h_loc + a_cum * hc
        h_ref[...] = h
        hp_ref[...] = jnp.where(row >= 1, pltpu.roll(h, 1, 0), hc)
        hc_s[...] = h[tm - 1:tm, :]
        out_ref[...] = (h * _gelu(gate_ref[...])).astype(bf16)

    out, xc, h, hp = _rt(
        "rg_fwd_" + tag, body, rows, tm,
        [(proj, BRANCH, U_COL + 5), (proj, BRANCH, U_COL + 6)],
        [cst["rg_cw"], cst["rg_cb"], cst["rg_wa"].astype(bf16), cst["rg_ba"], cst["rg_wx"].astype(bf16),
         cst["rg_bx"], cst["rg_sp8"]],
        [(BRANCH, bf16), (BRANCH, f32), (BRANCH, f32), (BRANCH, f32)],
        scratch=[pltpu.VMEM((tm, BRANCH), f32), pltpu.VMEM((1, BRANCH), f32)])
    return out, (xc, h, hp)


def _rg_bwd(tag, saved, proj, cst, d_out, rows):
    xc_saved, h_saved, hp_saved = saved
    tm = min(ROW_TILE, rows)

    def body(i, do_ref, xb_ref, gate_ref, xc_ref, h_ref, hp_ref, cw_ref, wa_ref, ba_ref, wx_ref, bx_ref, sp_ref,
             dxb_ref, dgate_ref, dcw_ref, dcb_ref, dwa_ref, dba_ref, dwx_ref, dbx_ref, dsp_ref,
             nxt_s, ec_s):
        @pl.when(i == 0)
        def _():
            nxt_s[...] = jnp.zeros_like(nxt_s)
            ec_s[...] = jnp.zeros_like(ec_s)

        row = _rows((tm, BRANCH))
        xc = xc_ref[...]
        sp8 = sp_ref[...]
        xcb, r, ig, a, mult = _rg_gates(xc, wa_ref, ba_ref, wx_ref, bx_ref, sp8)
        gate = gate_ref[...]
        dov = do_ref[...]
        dh = dov * _gelu(gate)
        dgate_ref[...] = (dov * h_ref[...] * _gelu_grad(gate)).astype(bf16)
        a_cum, e_loc = _scan_bwd(a, a * dh, tm)
        ec = ec_s[...]
        e = e_loc + a_cum * ec
        g_tot = dh + jnp.where(row == tm - 1, ec, pltpu.roll(e, tm - 1, 0))
        ec_s[...] = e[0:1, :]
        d_a = g_tot * hp_ref[...]
        d_mult = g_tot * ig * xc
        d_ix = g_tot * mult
        d_ig = d_ix * xc
        d_xc = d_ix * ig
        d_la = d_a * a - d_mult * (a * a) / mult
        d_r = -d_la * sp8
        dsp_ref[...] += jnp.sum(-d_la * r, axis=0, keepdims=True)
        dzr = d_r * r * (1.0 - r)
        dzi = d_ig * ig * (1.0 - ig)
        dzrb = dzr.astype(bf16)
        dzib = dzi.astype(bf16)
        d_xc = d_xc + _dot_nt(dzrb, wa_ref[...]) + _dot_nt(dzib, wx_ref[...])
        dwa_ref[...] += _dot_tn(xcb, dzrb)
        dwx_ref[...] += _dot_tn(xcb, dzib)
        dba_ref[...] += jnp.sum(dzr, axis=0, keepdims=True)
        dbx_ref[...] += jnp.sum(dzi, axis=0, keepdims=True)
        dcb_ref[...] += jnp.sum(d_xc, axis=0, keepdims=True)
        nxt = nxt_s[...]
        xb = xb_ref[...]
        dxb = cw_ref[3:4, :] * d_xc
        dcw_ref[3:4, :] += jnp.sum(d_xc * xb, axis=0, keepdims=True)
        for j in range(1, 4):
            sh = jnp.where(row < tm - j, pltpu.roll(d_xc, tm - j, 0), pltpu.roll(nxt, tm - j, 0))
            dxb = dxb + cw_ref[3 - j:4 - j, :] * sh
            dcw_ref[3 - j:4 - j, :] += jnp.sum(sh * xb, axis=0, keepdims=True)
        nxt_s[...] = d_xc
        dxb_ref[...] = dxb.astype(bf16)

    wa = cst["rg_wa"].astype(bf16)
    wx = cst["rg_wx"].astype(bf16)
    dxb, dgate, d_cw, d_cb, d_wa, d_ba, d_wx, d_bx, d_sp = _rt(
        "rg_bwd_" + tag, body, rows, tm,
        [(d_out, BRANCH, 0), (proj, BRANCH, U_COL + 5), (proj, BRANCH, U_COL + 6), (xc_saved, BRANCH, 0),
         (h_saved, BRANCH, 0), (hp_saved, BRANCH, 0)],
        [cst["rg_cw"], wa, cst["rg_ba"], wx, cst["rg_bx"], cst["rg_sp8"]],
        [(BRANCH, bf16), (BRANCH, bf16)],
        acc_outs=[(4, BRANCH), (1, BRANCH), (BRANCH, BRANCH), (1, BRANCH), (BRANCH, BRANCH), (1, BRANCH), (1, BRANCH)],
        scratch=[pltpu.VMEM((tm, BRANCH), f32), pltpu.VMEM((1, BRANCH), f32)],
        reverse=True)
    dcst = {"rg_cw": d_cw, "rg_cb": d_cb, "rg_wa": d_wa, "rg_ba": d_ba, "rg_wx": d_wx, "rg_bx": d_bx, "rg_sp8": d_sp}
    return dxb, dgate, dcst


def _merge_fwd(tag, proj, outs, bp, rows):
    def body(i, ya_ref, yb_ref, yc_ref, gm_ref, p_ref, m_ref):
        acc = None
        for n, y_ref in enumerate((ya_ref, yb_ref, yc_ref)):
            up = _dot(y_ref[...], p_ref[n])
            term = _sigmoid(gm_ref[:, n * D_MODEL:(n + 1) * D_MODEL]) * up
            acc = term if acc is None else acc + term
        m_ref[...] = acc.astype(bf16)
    return _rt("merge_fwd_" + tag, body, rows, ROW_TILE,
               [(outs[0], BRANCH, 0), (outs[1], BRANCH, 0), (outs[2], BRANCH, 0), (proj, GM_WIDTH, 0)],
               [bp], [(D_MODEL, bf16)])[0]


def _merge_bwd(tag, proj, outs, bp, dmerged, rows):
    def body(i, dm_ref, ya_ref, yb_ref, yc_ref, gm_ref, p_ref, da_ref, db_ref, dc_ref, dgm_ref, dp_ref):
        dm = dm_ref[...]
        for n, (y_ref, dy_ref) in enumerate(((ya_ref, da_ref), (yb_ref, db_ref), (yc_ref, dc_ref))):
            yv = y_ref[...]
            up = _dot(yv, p_ref[n])
            gt = _sigmoid(gm_ref[:, n * D_MODEL:(n + 1) * D_MODEL])
            dup = (dm * gt).astype(bf16)
            dgm_ref[:, n * D_MODEL:(n + 1) * D_MODEL] = (dm * up * gt * (1.0 - gt)).astype(bf16)
            dy_ref[...] = _dot_nt(dup, p_ref[n])
            dp_ref[n] += _dot_tn(yv, dup)
    return _rt("merge_bwd_" + tag, body, rows, ROW_TILE,
               [(dmerged, D_MODEL, 0), (outs[0], BRANCH, 0), (outs[1], BRANCH, 0), (outs[2], BRANCH, 0),
                (proj, GM_WIDTH, 0)],
               [bp], [(BRANCH, f32), (BRANCH, f32), (BRANCH, f32), (GM_WIDTH, bf16)],
               acc_outs=[(N_BRANCH, BRANCH, D_MODEL)])


def _block_diag(blocks):
    g, r, c = blocks.shape
    on_diag = (lax.broadcasted_iota(jnp.int32, (g * r, g * c), 0) // r
               == lax.broadcasted_iota(jnp.int32, (g * r, g * c), 1) // c)
    tiled = jnp.broadcast_to(blocks.reshape(g * r, 1, c), (g * r, g, c)).reshape(g * r, g * c)
    return jnp.where(on_diag, tiled, 0.0)


def _prep_consts(sp):
    p = jax.nn.softmax(sp["hg_lb_logits"], axis=0)
    lower = jnp.cumsum(p, axis=0) - p[0]
    out = []
    for l in range(DEPTH):
        lr = jnp.minimum(sp["s5_lambda_re"][l], S5_EIG_MAX)
        li = sp["s5_lambda_im"][l]
        dt = jnp.exp(sp["s5_log_dt"][l])[:, None]
        mag = jnp.exp(lr * dt)
        ar = mag * jnp.cos(li * dt)
        ai = mag * jnp.sin(li * dt)
        den = lr * lr + li * li
        fr = ((ar - 1.0) * lr + ai * li) / den
        fi = (ai * lr - (ar - 1.0) * li) / den
        br, bi = sp["s5_b_re"][l], sp["s5_b_im"][l]
        bbr = fr[..., None] * br - fi[..., None] * bi
        bbi = fr[..., None] * bi + fi[..., None] * br
        c = {
            "a_re": ar.reshape(1, S5_LANES), "a_im": ai.reshape(1, S5_LANES),
            "b_re": _block_diag(bbr.transpose(0, 2, 1)), "b_im": _block_diag(bbi.transpose(0, 2, 1)),
            "c_re": _block_diag(sp["s5_c_re"][l].transpose(0, 2, 1)),
            "c_im": -_block_diag(sp["s5_c_im"][l].transpose(0, 2, 1)),
            "s5_d": sp["s5_d"][l][None], "glu_b": sp["s5_glu_b"][l][None],
            "hg_lb": lower[l][None], "hg_nw": sp["hg_norm_w"][l][None],
            "rg_cw": sp["rg_conv_w"][l], "rg_cb": sp["rg_conv_b"][l][None],
            "rg_wa": _block_diag(sp["rg_wa"][l]), "rg_ba": sp["rg_ba"][l][None],
            "rg_wx": _block_diag(sp["rg_wx"][l]), "rg_bx": sp["rg_bx"][l][None],
            "rg_sp8": (RG_C * jax.nn.softplus(-sp["rg_lambda"][l]))[None],
        }
        out.append(c)
    return out


def _mixer_fwd(tag, x, hb, w_in, bp, w_out, cst, next_nw, rows):
    proj = _mm1("mix_proj_" + tag, hb, w_in, "nn", rows, IN_TOTAL, D_MODEL, 512, IN_TOTAL // 4, D_MODEL)
    cst = dict(cst)
    out_a, sv_a = _s5_fwd(tag, proj, cst, rows)
    out_b, sv_b = _hg_fwd(tag, proj, cst, rows)
    out_c, sv_c = _rg_fwd(tag, proj, cst, rows)
    merged = _merge_fwd(tag, proj, (out_a, out_b, out_c), bp, rows)
    x_out, hb_out = _mm("mix_out_" + tag, [merged], [w_out], [(0, 0, 0)], 1, "nn", rows, D_MODEL, D_MODEL,
                        512, D_MODEL, D_MODEL, [f32, bf16], _residual_then_norm(1.0), extras=[x], vecs=[next_nw])
    return x_out, hb_out, (x, hb, proj, (out_a, out_b, out_c), merged, sv_a, sv_b, sv_c)


def _mixer_bwd(tag, saved, nw, w_in, bp, w_out, cst, dx, dxb, rows):
    x, hb, proj, outs, merged, sv_a, sv_b, sv_c = saved
    d_wout = _mm1("mix_dwout_" + tag, merged, dxb, "tn", D_MODEL, D_MODEL, rows, D_MODEL, D_MODEL, TOKEN_K,
                  out_dtype=bf16)
    dmerged = _mm1("mix_dmerged_" + tag, dxb, w_out, "nt", rows, D_MODEL, D_MODEL, 512, D_MODEL, D_MODEL)
    d_a, d_b, d_c, dgm, d_bp = _merge_bwd(tag, proj, outs, bp, dmerged, rows)
    dxbc, dgatec, dcst_c = _rg_bwd(tag, sv_c, proj, cst, d_c, rows)
    dq, dz, dv, dg, dcst_b = _hg_bwd(tag, sv_b, proj, cst, d_b, rows)
    du, dcst_a, d_glu_w = _s5_bwd(tag, sv_a, proj, cst, d_a, rows)
    dproj = jnp.concatenate([dgm, du, dq, dz, dv, dg, dxbc, dgatec], axis=1)
    d_win = _mm1("mix_dwin_" + tag, hb, dproj, "tn", D_MODEL, IN_TOTAL, rows, D_MODEL, IN_TOTAL // 4, TOKEN_K,
                 out_dtype=bf16)
    dx_in, dxb_in, d_nw = _mm("mix_dh_" + tag, [dproj], [w_in], [(0, 0, 0)], 1, "nt", rows, D_MODEL, IN_TOTAL,
                              512, D_MODEL, IN_TOTAL // 2, [f32, bf16], _norm_bwd_epilogue, extras=[x, dx], vecs=[nw],
                              n_part=1)
    dcst = {**dcst_a, **dcst_b, **dcst_c}
    return dx_in, dxb_in, jnp.sum(d_nw, axis=0), d_win, d_bp, d_wout, d_glu_w, dcst


def _local_step(x, target, weights_of, small, layer_done):
    rows = x.shape[0]
    consts, consts_vjp = jax.vjp(_prep_consts, small)
    norm_w = small["norm_w"]
    saved = []
    h = x
    hb = _rms_fwd("first_norm", x, norm_w[0, 0][None], rows)
    for l in range(DEPTH):
        t = str(l)
        big = weights_of(l)
        cst = dict(consts[l])
        cst["glu_w"] = big["glu_w"]
        after = [norm_w[l + 1, 0][None]] if l + 1 < DEPTH else []
        h, hb, sv0 = _ffn_fwd(t + "a", h, hb, big["gate"][0], big["up"][0], big["down"][0], [norm_w[l, 1][None]], rows)
        h, hb, sv1 = _mixer_fwd(t, h, hb, big["w_in"], big["bp"], big["w_out"], cst, norm_w[l, 2][None], rows)
        h, hb, sv2 = _ffn_fwd(t + "b", h, hb, big["gate"][1], big["up"][1], big["down"][1], after, rows)
        saved.append((sv0, sv1, sv2, cst, big))
    dx, dxb, loss, d_fnw = _loss_head(h, small["final_norm_w"][None], target, rows)
    d_norm = [None] * DEPTH
    d_consts = [None] * DEPTH
    for l in reversed(range(DEPTH)):
        t = str(l)
        sv0, sv1, sv2, cst, big = saved[l]
        dx, dxb, dn2, dg1, du1, dd1 = _ffn_bwd(t + "b", sv2, norm_w[l, 2][None], big["gate"][1], big["up"][1],
                                               big["down"][1], dx, dxb, rows)
        dx, dxb, dn1, d_win, d_bp, d_wout, d_glu_w, dcst = _mixer_bwd(
            t, sv1, norm_w[l, 1][None], big["w_in"], big["bp"], big["w_out"], cst, dx, dxb, rows)
        dx, dxb, dn0, dg0, du0, dd0 = _ffn_bwd(t + "a", sv0, norm_w[l, 0][None], big["gate"][0], big["up"][0],
                                               big["down"][0], dx, dxb, rows)
        layer_done(l, {"gate": [dg0, dg1], "up": [du0, du1], "down": [dd0, dd1], "w_in": d_win, "bp": d_bp,
                       "w_out": d_wout, "glu_w": d_glu_w})
        d_norm[l] = jnp.concatenate([dn0, dn1, dn2], axis=0)
        d_consts[l] = dcst
    (g_small,) = consts_vjp(d_consts)
    g_small = dict(g_small)
    g_small["norm_w"] = g_small["norm_w"] + jnp.stack(d_norm)
    g_small["final_norm_w"] = g_small["final_norm_w"] + d_fnw[0]
    return loss[0, 0], dx, g_small


def _other_chips(x, y):
    return [(1 - x, y), (x, 1 - y), (1 - x, 1 - y)]


def _gather_over_ici(shards):
    n = len(shards)

    def copies(in_refs, out_refs, send_sems, recv_sems):
        x, y, c = _place()
        cps = []
        for i in range(n):
            mine = out_refs[i].at[4 * x + 2 * y + c]
            cps.append(pltpu.make_async_copy(in_refs[i], mine, send_sems.at[5 * i + 4]))
            for k, to in enumerate([(x, y, 1 - c)] + [(px, py, c) for px, py in _other_chips(x, y)]):
                cps.append(pltpu.make_async_remote_copy(
                    src_ref=in_refs[i], dst_ref=mine, send_sem=send_sems.at[5 * i + k],
                    recv_sem=recv_sems.at[5 * i + k], device_id=to, device_id_type=MESH_IDS))
        return cps

    return _Carry(shards, [jax.ShapeDtypeStruct((N_DEV,) + s.shape, s.dtype) for s in shards], 5 * n, copies)


def _gather_forward(name, landings):
    n = len(landings)

    def body(*refs):
        in_refs, out_refs = refs[:n], refs[n:2 * n]
        send_sems, recv_sems = refs[2 * n:]
        x, y, c = _place()
        cps = []
        for i in range(n):
            for j, (px, py) in enumerate(_other_chips(x, y)):
                block = 4 * px + 2 * py + c
                cps.append(pltpu.make_async_remote_copy(
                    src_ref=in_refs[i].at[block], dst_ref=out_refs[i].at[block], send_sem=send_sems.at[3 * i + j],
                    recv_sem=recv_sems.at[3 * i + j], device_id=(x, y, 1 - c), device_id_type=MESH_IDS))
        for cp in cps:
            cp.start()
        for cp in cps:
            cp.wait()

    return pl.pallas_call(
        body, name=name, out_shape=[jax.ShapeDtypeStruct(a.shape, a.dtype) for a in landings],
        in_specs=[ANY_SPEC] * n, out_specs=[ANY_SPEC] * n, input_output_aliases={i: i for i in range(n)},
        scratch_shapes=[pltpu.SemaphoreType.DMA((3 * n,)), pltpu.SemaphoreType.DMA((3 * n,))],
    )(*landings)


def _all_gather(name, shards):
    n = len(shards)

    def body(*refs):
        x_refs, out_refs = refs[:n], refs[n:2 * n]
        send_sems, recv_sems, local_sems = refs[2 * n:]
        x, y, c = _place()
        me, sibling = (x, y, c), (x, y, 1 - c)
        chips = [(1 - x, y), (x, 1 - y), (1 - x, 1 - y)]

        def blk(i, px, py, pc):
            return out_refs[i].at[4 * px + 2 * py + pc]

        def copy(i, k, block, to, src=None):
            return pltpu.make_async_remote_copy(
                src_ref=blk(i, *block) if src is None else src, dst_ref=blk(i, *block),
                send_sem=send_sems.at[7 * i + k], recv_sem=recv_sems.at[7 * i + k], device_id=to,
                device_id_type=MESH_IDS)

        mine = [pltpu.make_async_copy(x_refs[i], blk(i, *me), local_sems.at[i]) for i in range(n)]
        for cp in mine:
            cp.start()
        first = []
        for i in range(n):
            first.append(copy(i, 0, me, sibling, src=x_refs[i]))
            first += [copy(i, 1 + j, me, (*chip, c), src=x_refs[i]) for j, chip in enumerate(chips)]
        for cp in first:
            cp.start()
        passed = []
        for j, chip in enumerate(chips):
            for i in range(n):
                copy(i, 1 + j, (*chip, c), me).wait_recv()
                fwd = copy(i, 4 + j, (*chip, c), sibling)
                fwd.start()
                passed.append(fwd)
        for i in range(n):
            copy(i, 0, sibling, me).wait_recv()
            for j, chip in enumerate(chips):
                copy(i, 4 + j, (*chip, 1 - c), me).wait_recv()
        for cp in first + passed:
            cp.wait_send()
        for cp in mine:
            cp.wait()

    return pl.pallas_call(
        body, name=name, out_shape=[jax.ShapeDtypeStruct((N_DEV,) + s.shape, s.dtype) for s in shards],
        in_specs=[ANY_SPEC] * n, out_specs=[ANY_SPEC] * n,
        scratch_shapes=[pltpu.SemaphoreType.DMA((7 * n,)), pltpu.SemaphoreType.DMA((7 * n,)),
                        pltpu.SemaphoreType.DMA((n,))],
    )(*shards)


def _row_tile(rows):
    return rows if rows <= 512 else next(t for t in range(512, 7, -8) if rows % t == 0)


def _sums_over_ici(chip_sums):
    n = len(chip_sums)

    def copies(in_refs, out_refs, send_sems, recv_sems):
        x, y, c = _place()
        return [pltpu.make_async_remote_copy(
            src_ref=in_refs[i].at[2 * px + py], dst_ref=out_refs[i].at[k], send_sem=send_sems.at[3 * i + k],
            recv_sem=recv_sems.at[3 * i + k], device_id=(px, py, c), device_id_type=MESH_IDS)
            for i in range(n) for k, (px, py) in enumerate(_other_chips(x, y))]

    return _Carry(chip_sums, [jax.ShapeDtypeStruct((3,) + t.shape[1:], t.dtype) for t in chip_sums], 3 * n, copies)


def _reduce_scatter(tag, parts, hosts=None):
    n = len(parts)
    _, _, c = _place()

    def body_pair(*refs):
        p_refs, got_refs = refs[:n], refs[n:2 * n]
        send_sems, recv_sems = refs[2 * n:]
        x, y, c = _place()
        cps = [pltpu.make_async_remote_copy(
            src_ref=p_refs[i].at[1 - c], dst_ref=got_refs[i], send_sem=send_sems.at[i], recv_sem=recv_sems.at[i],
            device_id=(x, y, 1 - c), device_id_type=MESH_IDS) for i in range(n)]
        for cp in cps:
            cp.start()
        for cp in cps:
            cp.wait()

    from_sibling = pl.pallas_call(
        body_pair, name="rs_pair_" + tag, out_shape=[jax.ShapeDtypeStruct(p.shape[1:], p.dtype) for p in parts],
        in_specs=[ANY_SPEC] * n, out_specs=[ANY_SPEC] * n,
        scratch_shapes=[pltpu.SemaphoreType.DMA((n,)), pltpu.SemaphoreType.DMA((n,))],
    )(*parts)

    chip_sums = []
    for i, (part, got) in enumerate(zip(parts, from_sibling)):
        _, _, r, cols = part.shape
        tr = _row_tile(r)

        def body_add(idx_ref, p_ref, g_ref, o_ref):
            o_ref[...] = (p_ref[...].astype(f32) + g_ref[...].astype(f32)).astype(o_ref.dtype)

        chip_sums.append(pl.pallas_call(
            body_add, name="rs_pair_sum_%s_%d" % (tag, i), out_shape=jax.ShapeDtypeStruct((4, r, cols), part.dtype),
            grid_spec=pltpu.PrefetchScalarGridSpec(
                num_scalar_prefetch=1, grid=(4, r // tr),
                in_specs=[pl.BlockSpec((None, None, tr, cols), lambda j, t, idx: (idx[0], j, t, 0)),
                          pl.BlockSpec((None, tr, cols), lambda j, t, idx: (j, t, 0))],
                out_specs=pl.BlockSpec((None, tr, cols), lambda j, t, idx: (j, t, 0))),
            compiler_params=_cparams(("parallel", "parallel")),
        )(jnp.stack([c]).astype(jnp.int32), part, got))

    others = [None] * n
    riding = set()
    for host, which in (hosts or {}).items():
        rider = _sums_over_ici([chip_sums[i] for i in which])
        _CARRIED[host] = rider
        for pos, i in enumerate(which):
            others[i] = functools.partial(lambda r, p: r.outs[p], rider, pos)
        riding.update(which)
    rest = [i for i in range(n) if i not in riding]
    if rest:
        alone = _sums_over_ici([chip_sums[i] for i in rest])

        def body_chips(*refs):
            k = len(rest)
            cps = alone.copies(refs[:k], refs[k:2 * k], *refs[2 * k:])
            for cp in cps:
                cp.start()
            for cp in cps:
                cp.wait()

        from_chips = pl.pallas_call(
            body_chips, name="rs_chips_" + tag, out_shape=alone.out_shapes,
            in_specs=[ANY_SPEC] * len(rest), out_specs=[ANY_SPEC] * len(rest), scratch_shapes=alone.sems(),
        )(*alone.ins)
        for pos, i in enumerate(rest):
            others[i] = functools.partial(lambda got: got, from_chips[pos])
    return list(zip(chip_sums, others))


def _own_index():
    x, y, _ = _place()
    return jnp.stack([2 * x + y]).astype(jnp.int32)


def _own_total(name, chip_sum, others):
    _, r, cols = chip_sum.shape
    tr = _row_tile(r)

    def body(idx_ref, t_ref, g_ref, o_ref):
        o_ref[...] = ((t_ref[...].astype(f32) + g_ref[0].astype(f32)) + g_ref[1].astype(f32)) + g_ref[2].astype(f32)

    return pl.pallas_call(
        body, name=name, out_shape=jax.ShapeDtypeStruct((r, cols), f32),
        grid_spec=pltpu.PrefetchScalarGridSpec(
            num_scalar_prefetch=1, grid=(r // tr,),
            in_specs=[pl.BlockSpec((None, tr, cols), lambda t, idx: (idx[0], t, 0)),
                      pl.BlockSpec((3, tr, cols), lambda t, idx: (0, t, 0))],
            out_specs=pl.BlockSpec((tr, cols), lambda t, idx: (t, 0))),
        compiler_params=_cparams(("parallel",)),
    )(_own_index(), chip_sum, others)


def _adam_update(w, gv, m, v):
    m_new = ADAM_B1 * m + (1.0 - ADAM_B1) * gv
    v_new = ADAM_B2 * v + (1.0 - ADAM_B2) * (gv * gv)
    m_hat = m_new / (1.0 - ADAM_B1 ** ADAM_STEP)
    v_hat = v_new / (1.0 - ADAM_B2 ** ADAM_STEP)
    return -ADAM_LR * (m_hat / (jnp.sqrt(v_hat) + ADAM_EPS) + ADAM_WD * w), m_new, v_new


def _adamw_reduced(name, w, pieces, m, v):
    n_p, rows, cols = w.shape
    tr = _row_tile(rows)

    def body(idx_ref, w_ref, *refs):
        red = refs[:2 * n_p]
        m_ref, v_ref, g_ref, d_ref, nm_ref, nv_ref = refs[2 * n_p:]
        p = pl.program_id(0)
        for q in range(n_p):
            @pl.when(p == q)
            def _(t_ref=red[2 * q], o_ref=red[2 * q + 1]):
                gv = ((t_ref[...].astype(f32) + o_ref[0].astype(f32)) + o_ref[1].astype(f32)) + o_ref[2].astype(f32)
                g_ref[...] = gv
                d_ref[...], nm_ref[...], nv_ref[...] = _adam_update(w_ref[...], gv, m_ref[...], v_ref[...])

    spec = pl.BlockSpec((None, tr, cols), lambda p, t, idx: (p, t, 0))
    red_specs, red_args = [], []
    for q, (chip_sum, others) in enumerate(pieces):
        red_specs.append(pl.BlockSpec((None, tr, cols), lambda p, t, idx, q=q: (idx[0], jnp.where(p == q, t, 0), 0)))
        red_specs.append(pl.BlockSpec((3, tr, cols), lambda p, t, idx, q=q: (0, jnp.where(p == q, t, 0), 0)))
        red_args += [chip_sum, others]
    return pl.pallas_call(
        body, name=name, out_shape=[jax.ShapeDtypeStruct((n_p, rows, cols), f32)] * 4,
        grid_spec=pltpu.PrefetchScalarGridSpec(
            num_scalar_prefetch=1, grid=(n_p, rows // tr),
            in_specs=[spec] + red_specs + [spec, spec], out_specs=[spec] * 4),
        compiler_params=_cparams(("parallel", "parallel")),
    )(_own_index(), w, *red_args, m, v)


def _adamw(name, w, g, m, v):
    rows, cols = w.shape
    tr = _row_tile(rows)

    def body(w_ref, g_ref, m_ref, v_ref, d_ref, nm_ref, nv_ref):
        d_ref[...], nm_ref[...], nv_ref[...] = _adam_update(w_ref[...], g_ref[...], m_ref[...], v_ref[...])

    spec = pl.BlockSpec((tr, cols), lambda i: (i, 0))
    return pl.pallas_call(
        body, name=name, grid=(rows // tr,), in_specs=[spec] * 4, out_specs=[spec] * 3,
        out_shape=[jax.ShapeDtypeStruct((rows, cols), f32)] * 3, compiler_params=_cparams(("parallel",)),
    )(w, g, m, v)


WEIGHT_NAMES = ["norm_w", "final_norm_w", "ffn_gate", "ffn_up", "ffn_down", "w_in", "branch_proj", "w_out",
                "s5_lambda_re", "s5_lambda_im", "s5_log_dt", "s5_b_re", "s5_b_im", "s5_c_re", "s5_c_im", "s5_d",
                "s5_glu_w", "s5_glu_b", "hg_lb_logits", "hg_norm_w", "rg_conv_w", "rg_conv_b", "rg_wa", "rg_ba",
                "rg_wx", "rg_bx", "rg_lambda"]
SHARDED = {"ffn_gate": (3, "gate"), "ffn_up": (3, "up"), "ffn_down": (2, "down"), "w_in": (2, "w_in"),
           "branch_proj": (3, "bp"), "w_out": (1, "w_out"), "s5_glu_w": (1, "glu_w"),
           "norm_w": (2, None), "rg_conv_w": (2, None)}
BIG = ["ffn_gate", "ffn_up", "ffn_down", "w_in", "branch_proj", "w_out", "s5_glu_w"]
PIECES = [("ffn_gate", 0, 1), ("ffn_gate", 1, 1), ("ffn_up", 0, 1), ("ffn_up", 1, 1), ("ffn_down", 0, 0),
          ("ffn_down", 1, 0), ("w_in", None, 1), ("branch_proj", None, 2), ("w_out", None, 0), ("s5_glu_w", None, 0)]
AG_HOSTS = {"ffn_up_0a": [0, 1], "mix_proj_0": [6, 7], "hg_fwd_0": [2, 8], "rg_fwd_0": [3, 9], "ffn_up_0b": [4, 5]}
RS_HOSTS = {"ffn_dh_0b": [0, 1, 7], "merge_bwd_0": [2, 3], "rg_bwd_0": [4, 5], "mix_dh_0": [6, 8, 9]}
SMALL_SHARDED = ["norm_w", "rg_conv_w"]
REPLICATED = [n for n in WEIGHT_NAMES if n not in SHARDED]
LANES = 128


PACK_ROWS = 512


def _pack_rows(arrays, names):
    pieces = []
    for n in names:
        flat = arrays[n].reshape(-1)
        pieces.append(jnp.pad(flat, (0, -flat.shape[0] % LANES)).reshape(-1, LANES))
    rows = jnp.concatenate(pieces, axis=0)
    return jnp.pad(rows, ((0, -rows.shape[0] % PACK_ROWS), (0, 0)))


def _unpack_rows(rows, names, like):
    out, r0 = {}, 0
    for n in names:
        size = math.prod(like[n].shape)
        nrows = -(-size // LANES)
        out[n] = rows[r0:r0 + nrows].reshape(-1)[:size].reshape(like[n].shape)
        r0 += nrows
    return out


def _unshard(gathered, axis):
    g = jnp.moveaxis(gathered, 0, axis)
    shp = g.shape
    return g.reshape(shp[:axis] + (shp[axis] * shp[axis + 1],) + shp[axis + 2:])


def _to_blocks(full, axis):
    shp = full.shape
    g = full.reshape(shp[:axis] + (4, 2, shp[axis] // N_DEV) + shp[axis + 1:])
    g = jnp.moveaxis(g, (axis, axis + 1), (1, 0))
    return g.reshape(2, 4, -1, g.shape[-1])


W_IN_SPLIT = IN_TOTAL - GM_WIDTH


def kernel(x, norm_w, final_norm_w, ffn_gate, ffn_up, ffn_down, w_in, branch_proj, w_out, s5_lambda_re, s5_lambda_im, s5_log_dt, s5_b_re, s5_b_im, s5_c_re, s5_c_im, s5_d, s5_glu_w, s5_glu_b, hg_lb_logits, hg_norm_w, rg_conv_w, rg_conv_b, rg_wa, rg_ba, rg_wx, rg_bx, rg_lambda, loss_target, m_norm_w, m_final_norm_w, m_ffn_gate, m_ffn_up, m_ffn_down, m_w_in, m_branch_proj, m_w_out, m_s5_lambda_re, m_s5_lambda_im, m_s5_log_dt, m_s5_b_re, m_s5_b_im, m_s5_c_re, m_s5_c_im, m_s5_d, m_s5_glu_w, m_s5_glu_b, m_hg_lb_logits, m_hg_norm_w, m_rg_conv_w, m_rg_conv_b, m_rg_wa, m_rg_ba, m_rg_wx, m_rg_bx, m_rg_lambda, v_norm_w, v_final_norm_w, v_ffn_gate, v_ffn_up, v_ffn_down, v_w_in, v_branch_proj, v_w_out, v_s5_lambda_re, v_s5_lambda_im, v_s5_log_dt, v_s5_b_re, v_s5_b_im, v_s5_c_re, v_s5_c_im, v_s5_d, v_s5_glu_w, v_s5_glu_b, v_hg_lb_logits, v_hg_norm_w, v_rg_conv_w, v_rg_conv_b, v_rg_wa, v_rg_ba, v_rg_wx, v_rg_bx, v_rg_lambda):
    w = dict(zip(WEIGHT_NAMES, (norm_w, final_norm_w, ffn_gate, ffn_up, ffn_down, w_in, branch_proj, w_out,
                                s5_lambda_re, s5_lambda_im, s5_log_dt, s5_b_re, s5_b_im, s5_c_re, s5_c_im, s5_d,
                                s5_glu_w, s5_glu_b, hg_lb_logits, hg_norm_w, rg_conv_w, rg_conv_b, rg_wa, rg_ba,
                                rg_wx, rg_bx, rg_lambda)))
    m = dict(zip(WEIGHT_NAMES, (m_norm_w, m_final_norm_w, m_ffn_gate, m_ffn_up, m_ffn_down, m_w_in, m_branch_proj,
                                m_w_out, m_s5_lambda_re, m_s5_lambda_im, m_s5_log_dt, m_s5_b_re, m_s5_b_im, m_s5_c_re,
                                m_s5_c_im, m_s5_d, m_s5_glu_w, m_s5_glu_b, m_hg_lb_logits, m_hg_norm_w, m_rg_conv_w,
                                m_rg_conv_b, m_rg_wa, m_rg_ba, m_rg_wx, m_rg_bx, m_rg_lambda)))
    v = dict(zip(WEIGHT_NAMES, (v_norm_w, v_final_norm_w, v_ffn_gate, v_ffn_up, v_ffn_down, v_w_in, v_branch_proj,
                                v_w_out, v_s5_lambda_re, v_s5_lambda_im, v_s5_log_dt, v_s5_b_re, v_s5_b_im, v_s5_c_re,
                                v_s5_c_im, v_s5_d, v_s5_glu_w, v_s5_glu_b, v_hg_lb_logits, v_hg_norm_w, v_rg_conv_w,
                                v_rg_conv_b, v_rg_wa, v_rg_ba, v_rg_wx, v_rg_bx, v_rg_lambda)))
    rows = x.shape[1]

    _CARRIED.clear()

    def shard_of(piece, l):
        n, k, _ = piece
        return (w[n][l] if k is None else w[n][l, k]).astype(bf16)

    def assemble(gathered):
        full = [_unshard(g, piece[2]) for piece, g in zip(PIECES, gathered)]
        w_in = full[6]
        return {"gate": full[0:2], "up": full[2:4], "down": full[4:6],
                "w_in": jnp.concatenate([w_in[:, W_IN_SPLIT:], w_in[:, :W_IN_SPLIT]], axis=1),
                "bp": full[7], "w_out": full[8], "glu_w": full[9]}

    first = _all_gather("gather_weights", [shard_of(p, 0) for p in PIECES] + [w[n] for n in SMALL_SHARDED])
    small = {n: w[n] for n in REPLICATED}
    for n, g in zip(SMALL_SHARDED, first[len(PIECES):]):
        small[n] = _unshard(g, SHARDED[n][0])
    riders = []
    for host, which in AG_HOSTS.items():
        rider = _gather_over_ici([shard_of(PIECES[i], 1) for i in which])
        _CARRIED[host] = rider
        riders.append((which, rider))

    def weights_of(l):
        if l == 0:
            return assemble(first[:len(PIECES)])
        landed = [None] * len(PIECES)
        for which, rider in riders:
            for i, buf in zip(which, rider.outs):
                landed[i] = buf
        return assemble(_gather_forward("gather_forward", landed))

    sums = {}

    def blocks_of(grads):
        flat = grads["gate"] + grads["up"] + grads["down"] + [
            jnp.concatenate([grads["w_in"][:, GM_WIDTH:], grads["w_in"][:, :GM_WIDTH]], axis=1),
            grads["bp"], grads["w_out"], grads["glu_w"]]
        return [_to_blocks(g, piece[2]).astype(bf16) for piece, g in zip(PIECES, flat)]

    layer0 = []

    def layer_done(l, grads):
        if l == 1:
            sums[1] = _reduce_scatter("l1", blocks_of(grads), hosts=RS_HOSTS)
        else:
            layer0.append(grads)

    loss_part, dx, g_small = _local_step(x[0], loss_target[0], weights_of, small, layer_done)
    loss = lax.psum(loss_part, ("x", "y", "c"))

    parts = blocks_of(layer0[0]) + [_to_blocks(g_small[n], SHARDED[n][0]) for n in SMALL_SHARDED]
    rep_rows = _pack_rows(g_small, REPLICATED)
    rep_slice = rep_rows.shape[0] // N_DEV
    parts.append(rep_rows.reshape(4, 2, rep_slice, LANES).transpose(1, 0, 2, 3))
    last = _reduce_scatter("l0", parts)
    sums[0] = last[:len(PIECES)]

    grads, delta, new_m, new_v = {}, {}, {}, {}

    def update(n, pieces):
        shp = w[n].shape
        view = (len(pieces), -1, shp[-1])
        res = _adamw_reduced("adamw_" + n, w[n].reshape(view), [(t, others()) for t, others in pieces],
                             m[n].reshape(view), v[n].reshape(view))
        grads[n], delta[n], new_m[n], new_v[n] = (r.reshape(shp) for r in res)

    for n in BIG:
        update(n, [sums[l][i] for l in range(DEPTH) for i, piece in enumerate(PIECES) if piece[0] == n])
    for j, n in enumerate(SMALL_SHARDED):
        update(n, [last[len(PIECES) + j]])
    rep_mine = _own_total("rs_total_small", last[-1][0], last[-1][1]())
    rep_grads = _all_gather("gather_small_grads", [rep_mine])[0].reshape(-1, LANES)
    res = _adamw("adamw_small", _pack_rows(w, REPLICATED), rep_grads, _pack_rows(m, REPLICATED), _pack_rows(v, REPLICATED))
    for dst, src in zip((grads, delta, new_m, new_v), (rep_grads,) + tuple(res)):
        dst.update(_unpack_rows(src, REPLICATED, w))

    return (loss, dx.reshape(x.shape), *[grads[n] for n in WEIGHT_NAMES], *[delta[n] for n in WEIGHT_NAMES],
            *[new_m[n] for n in WEIGHT_NAMES], *[new_v[n] for n in WEIGHT_NAMES])
```

```python
import functools
import math

import jax
import jax.numpy as jnp
from jax import lax
from jax.experimental import pallas as pl
from jax.experimental.pallas import tpu as pltpu

f32 = jnp.float32
bf16 = jnp.bfloat16

D_MODEL = 1024
DEPTH = 2
BRANCH = 512
N_BRANCH = 3
S5_GROUP = 16
S5_GROUPS = 32
S5_STATE = 64
S5_LANES = S5_GROUPS * S5_STATE
S5_EIG_MAX = -1e-4
HG_HEADS = 4
HG_DK = 128
HG_CHUNK = 32
RG_BLOCKS = 8
RG_BLOCK = 64
RG_C = 8.0
D_FF = 2816
EPS = 1e-6
IN_TOTAL = 6656
GM_WIDTH = N_BRANCH * D_MODEL
N_DEV = 8

ADAM_LR = 0.001
ADAM_B1 = 0.9
ADAM_B2 = 0.999
ADAM_EPS = 1e-08
ADAM_WD = 0.01
ADAM_STEP = 10

VMEM_LIMIT_V7X = 56 * 1024 * 1024
ROW_TILE = 256
FF_TILE = 1408
TOKEN_K = 2048
MXU_COLS = 256


def _cparams(sem):
    return pltpu.CompilerParams(dimension_semantics=sem, vmem_limit_bytes=VMEM_LIMIT_V7X)


MESH_IDS = pl.DeviceIdType.MESH
ANY_SPEC = pl.BlockSpec(memory_space=pl.ANY)


def _place():
    return lax.axis_index("x"), lax.axis_index("y"), lax.axis_index("c")


class _Carry:
    def __init__(self, ins, out_shapes, n_sems, copies):
        self.ins, self.out_shapes, self.n_sems, self.copies = list(ins), list(out_shapes), n_sems, copies
        self.outs = None

    def sems(self):
        return [pltpu.SemaphoreType.DMA((self.n_sems,)), pltpu.SemaphoreType.DMA((self.n_sems,))]

    def start(self, when, *riders):
        @pl.when(when)
        def _():
            for cp in self.copies(*riders):
                cp.start()

    def finish(self, when, *riders):
        @pl.when(when)
        def _():
            for cp in self.copies(*riders):
                cp.wait()


_CARRIED = {}


def _sigmoid(x):
    return 0.5 * jnp.tanh(0.5 * x) + 0.5


def _sigmoid_small(x):
    return 1.0 / (1.0 + jnp.exp(-x))


_GELU_C = math.sqrt(2.0 / math.pi)


def _gelu(x):
    t = jnp.tanh(_GELU_C * (x + 0.044715 * x * x * x))
    return 0.5 * x * (1.0 + t)


def _gelu_grad(x):
    t = jnp.tanh(_GELU_C * (x + 0.044715 * x * x * x))
    return 0.5 * (1.0 + t) + 0.5 * x * (1.0 - t * t) * _GELU_C * (1.0 + 3.0 * 0.044715 * x * x)


def _expm1(x):
    p = x * (1.0 + x * (0.5 + x * (1.0 / 6 + x * (1.0 / 24 + x * (1.0 / 120 + x * (1.0 / 720))))))
    return jnp.where(jnp.abs(x) < 0.3, p, jnp.exp(x) - 1.0)


def _dot(a, b):
    return jnp.dot(a, b, preferred_element_type=f32)


def _dot_nt(a, b):
    return lax.dot_general(a, b, (((1,), (1,)), ((), ())), preferred_element_type=f32)


def _dot_tn(a, b):
    return lax.dot_general(a, b, (((0,), (0,)), ((), ())), preferred_element_type=f32)


def _bdot(a, b):
    return lax.dot_general(a, b, (((2,), (1,)), ((0,), (0,))), preferred_element_type=f32)


def _bdot_nt(a, b):
    return lax.dot_general(a, b, (((2,), (2,)), ((0,), (0,))), preferred_element_type=f32)


def _rows(shape):
    return lax.broadcasted_iota(jnp.int32, shape, 0)


def _scan_fwd(a, b, n):
    row = _rows(a.shape)
    s = 1
    while s < n:
        valid = row >= s
        sh_a = pltpu.roll(a, s, 0)
        sh_b = pltpu.roll(b, s, 0)
        b = b + a * jnp.where(valid, sh_b, 0.0)
        a = a * jnp.where(valid, sh_a, 1.0)
        s *= 2
    return a, b


def _scan_bwd(a, b, n):
    row = _rows(a.shape)
    s = 1
    while s < n:
        valid = row < n - s
        sh_a = pltpu.roll(a, n - s, 0)
        sh_b = pltpu.roll(b, n - s, 0)
        b = b + a * jnp.where(valid, sh_b, 0.0)
        a = a * jnp.where(valid, sh_a, 1.0)
        s *= 2
    return a, b


def _seg_cumsum(x, n, seg):
    pos = _rows(x.shape) % seg
    s = 1
    while s < seg:
        x = x + jnp.where(pos >= s, pltpu.roll(x, s, 0), 0.0)
        s *= 2
    return x


def _seg_rev_cumsum(x, n, seg):
    pos = _rows(x.shape) % seg
    s = 1
    while s < seg:
        x = x + jnp.where(pos < seg - s, pltpu.roll(x, n - s, 0), 0.0)
        s *= 2
    return x


def _head_mean(x):
    parts = []
    for h in range(HG_HEADS):
        m = jnp.mean(x[:, h * HG_DK:(h + 1) * HG_DK], axis=1, keepdims=True)
        parts.append(jnp.broadcast_to(m, (x.shape[0], HG_DK)))
    return jnp.concatenate(parts, axis=1)


def _mm(name, a_list, b_list, terms, n_acc, mode, m, n, k, tm, tn, tk, out_dtypes, epilogue, extras=(), vecs=(),
        n_part=0, chunk=0):
    tm, tn, tk = min(tm, m), min(tn, n), min(tk, k)
    assert m % tm == 0 and n % tn == 0 and k % tk == 0, (name, m, n, k, tm, tn, tk)
    gk = k // tk
    if mode == "tn":
        a_spec = pl.BlockSpec((tk, tm), lambda i, j, kk: (kk, i))
    else:
        a_spec = pl.BlockSpec((tm, tk), lambda i, j, kk: (i, kk))
    if mode == "nt":
        b_spec = pl.BlockSpec((tn, tk), lambda i, j, kk: (j, kk))
    else:
        b_spec = pl.BlockSpec((tk, tn), lambda i, j, kk: (kk, j))
    o_spec = pl.BlockSpec((tm, tn), lambda i, j, kk: (i, j))
    v_spec = pl.BlockSpec((1, tn), lambda i, j, kk: (0, j))
    p_spec = pl.BlockSpec((None, 1, tn), lambda i, j, kk: (i, 0, j))
    dot = {"nn": _dot, "nt": _dot_nt, "tn": _dot_tn}[mode]
    na, nb, ne, nv, no = len(a_list), len(b_list), len(extras), len(vecs), len(out_dtypes)
    carry = _CARRIED.pop(name, None)
    nci, nco = (len(carry.ins), len(carry.out_shapes)) if carry else (0, 0)
    n_in = na + nb + ne + nv + nci
    grid = (m // tm, n // tn, gk)

    def kern(*refs):
        if carry:
            ids = [pl.program_id(d) for d in range(3)]
            riders = (refs[n_in - nci:n_in], refs[n_in + no + n_part:n_in + no + n_part + nco]) + tuple(refs[-2:])
            carry.start(functools.reduce(jnp.logical_and, [p == 0 for p in ids]), *riders)
        compute(*refs)
        if carry:
            carry.finish(functools.reduce(jnp.logical_and, [p == g - 1 for p, g in zip(ids, grid)]), *riders)

    def compute(*refs):
        a_refs = refs[:na]
        b_refs = refs[na:na + nb]
        e_refs = refs[na + nb:na + nb + ne]
        v_refs = refs[na + nb + ne:na + nb + ne + nv]
        o_refs = refs[n_in:n_in + no + n_part]

        def finish(accs):
            outs = epilogue(accs, [e[...] for e in e_refs], [r[...] for r in v_refs])
            for o, val in zip(o_refs, outs):
                o[...] = val.astype(o.dtype)

        def partial_sums():
            sums = [None] * n_acc
            for ai, bi, ci in terms:
                d = dot(a_refs[ai][...].astype(bf16), b_refs[bi][...].astype(bf16))
                sums[ci] = d if sums[ci] is None else sums[ci] + d
            return sums

        if gk == 1 and chunk:
            assert mode in ("nn", "nt") and tn % chunk == 0
            for c0 in range(0, tn, chunk):
                cols = slice(c0, c0 + chunk)
                sums = [None] * n_acc
                for ai, bi, ci in terms:
                    b_part = b_refs[bi][:, cols] if mode == "nn" else b_refs[bi][cols, :]
                    d = dot(a_refs[ai][...].astype(bf16), b_part.astype(bf16))
                    sums[ci] = d if sums[ci] is None else sums[ci] + d
                outs = epilogue(sums, [e[:, cols] for e in e_refs], [r[:, cols] for r in v_refs])
                for o, val in zip(o_refs, outs):
                    o[:, cols] = val.astype(o.dtype)
            return
        if gk == 1:
            finish(partial_sums())
            return
        acc = refs[n_in + no + n_part + nco]
        kk = pl.program_id(2)

        @pl.when(kk == 0)
        def _():
            acc[...] = jnp.zeros_like(acc)

        for ci, d in enumerate(partial_sums()):
            acc[ci] += d

        @pl.when(kk == gk - 1)
        def _():
            finish([acc[c] for c in range(n_acc)])

    res = pl.pallas_call(
        kern, name=name,
        grid=grid,
        in_specs=[a_spec] * na + [b_spec] * nb + [o_spec] * ne + [v_spec] * nv + [ANY_SPEC] * nci,
        out_specs=[o_spec] * no + [p_spec] * n_part + [ANY_SPEC] * nco,
        out_shape=([jax.ShapeDtypeStruct((m, n), dt) for dt in out_dtypes]
                   + [jax.ShapeDtypeStruct((m // tm, 1, n), f32)] * n_part + (carry.out_shapes if carry else [])),
        scratch_shapes=([pltpu.VMEM((n_acc, tm, tn), f32)] if gk > 1 else []) + (carry.sems() if carry else []),
        compiler_params=_cparams(("arbitrary",) * 3 if carry else ("parallel", "parallel", "arbitrary")),
    )(*a_list, *b_list, *extras, *vecs, *(carry.ins if carry else []))
    if carry:
        carry.outs = res[no + n_part:]
        res = res[:no + n_part]
    return res


def _mm1(name, a, b, mode, m, n, k, tm, tn, tk, out_dtype=f32, scale=None):
    def epi(accs, extras, vecs):
        return [accs[0] if scale is None else accs[0] * scale]
    return _mm(name, [a], [b], [(0, 0, 0)], 1, mode, m, n, k, tm, tn, tk, [out_dtype], epi)[0]


def _rt(name, body, rows, tm, row_ins, consts, row_outs, acc_outs=(), scratch=(), reverse=False):
    tm = min(tm, rows)
    assert rows % tm == 0
    nt = rows // tm

    def tile(i):
        return nt - 1 - i if reverse else i

    in_specs, args = [], []
    for spec in row_ins:
        arr = spec[0]
        if isinstance(spec[1], int):
            in_specs.append(pl.BlockSpec((tm, spec[1]), lambda i, cb=spec[2]: (tile(i), cb)))
        else:
            in_specs.append(pl.BlockSpec(spec[1], lambda i, fn=spec[2]: fn(tile(i))))
        args.append(arr)
    for c in consts:
        in_specs.append(pl.BlockSpec(c.shape, lambda i, nd=c.ndim: (0,) * nd))
        args.append(c)
    out_specs, out_shape = [], []
    for spec in row_outs:
        if isinstance(spec[0], int):
            out_specs.append(pl.BlockSpec((tm, spec[0]), lambda i: (tile(i), 0)))
            out_shape.append(jax.ShapeDtypeStruct((rows, spec[0]), spec[1]))
        else:
            out_specs.append(pl.BlockSpec(spec[1], lambda i, fn=spec[2]: fn(tile(i))))
            out_shape.append(jax.ShapeDtypeStruct(spec[0], spec[3]))
    for shp in acc_outs:
        out_specs.append(pl.BlockSpec(shp, lambda i, nd=len(shp): (0,) * nd))
        out_shape.append(jax.ShapeDtypeStruct(shp, f32))
    n_in = len(args)
    n_row_out = len(row_outs)
    n_acc = len(acc_outs)
    n_out = n_row_out + n_acc
    carry = _CARRIED.pop(name, None)
    nci, nco = (len(carry.ins), len(carry.out_shapes)) if carry else (0, 0)

    def kern(*refs):
        i = pl.program_id(0)
        if carry:
            own = refs[:n_in] + refs[n_in + nci:n_in + nci + n_out] + refs[n_in + nci + n_out + nco:-2]
            riders = (refs[n_in:n_in + nci], refs[n_in + nci + n_out:n_in + nci + n_out + nco]) + tuple(refs[-2:])
            carry.start(i == 0, *riders)
        else:
            own = refs
        acc_refs = own[n_in + n_row_out:n_in + n_out]

        @pl.when(i == 0)
        def _():
            for r in acc_refs:
                r[...] = jnp.zeros_like(r)

        body(i, *own)
        if carry:
            carry.finish(i == nt - 1, *riders)

    res = pl.pallas_call(
        kern, name=name, grid=(nt,), in_specs=in_specs + [ANY_SPEC] * nci, out_specs=out_specs + [ANY_SPEC] * nco,
        out_shape=out_shape + (carry.out_shapes if carry else []),
        scratch_shapes=list(scratch) + (carry.sems() if carry else []), compiler_params=_cparams(("arbitrary",)),
    )(*args, *(carry.ins if carry else []))
    if carry:
        carry.outs = res[n_out:]
        res = res[:n_out]
    return res


def _rms_rows(xv, wv):
    r = lax.rsqrt(jnp.mean(xv * xv, axis=1, keepdims=True) + EPS)
    return (xv * r * wv).astype(bf16)


def _rms_bwd_rows(xv, dhv, wv, dres):
    r = lax.rsqrt(jnp.mean(xv * xv, axis=1, keepdims=True) + EPS)
    xn = xv * r
    dxn = dhv * wv
    dx = dres + r * (dxn - xn * jnp.mean(dxn * xn, axis=1, keepdims=True))
    return [dx, dx.astype(bf16), jnp.sum(dhv * xn, axis=0, keepdims=True)]


def _rms_fwd(name, x, w, rows):
    def body(i, x_ref, w_ref, h_ref):
        h_ref[...] = _rms_rows(x_ref[...], w_ref[...])
    return _rt(name, body, rows, ROW_TILE, [(x, D_MODEL, 0)], [w], [(D_MODEL, bf16)])[0]


def _residual_then_norm(scale):
    def epi(accs, extras, vecs):
        x_out = extras[0] + scale * accs[0]
        return [x_out] + [_rms_rows(x_out, v) for v in vecs]
    return epi


def _norm_bwd_epilogue(accs, extras, vecs):
    return _rms_bwd_rows(extras[0], accs[0], vecs[0], extras[1])


def _loss_head(x, w, target, rows):
    def body(i, x_ref, t_ref, w_ref, dx_ref, dxb_ref, loss_ref, dw_ref):
        xv = x_ref[...]
        r = lax.rsqrt(jnp.mean(xv * xv, axis=1, keepdims=True) + EPS)
        xn = xv * r
        wv = w_ref[...]
        err = xn * wv - t_ref[...]
        part = 0.5 * jnp.sum(jnp.mean(err * err, axis=1, keepdims=True), axis=0, keepdims=True)
        loss_ref[...] += jnp.broadcast_to(part, (1, 128))
        dy = err * (1.0 / D_MODEL)
        dxn = dy * wv
        dx = r * (dxn - xn * jnp.mean(dxn * xn, axis=1, keepdims=True))
        dx_ref[...] = dx
        dxb_ref[...] = dx.astype(bf16)
        dw_ref[...] += jnp.sum(dy * xn, axis=0, keepdims=True)
    return _rt("loss_head", body, rows, ROW_TILE, [(x, D_MODEL, 0), (target, D_MODEL, 0)], [w],
               [(D_MODEL, f32), (D_MODEL, bf16)], acc_outs=[(1, 128), (1, D_MODEL)])


def _ffn_fwd(tag, x, hb, wg, wu, wd, next_nw, rows):
    def epi_up(accs, extras, vecs):
        a, b = accs
        return [a, b, a * _sigmoid(a) * b]
    a, b, s = _mm("ffn_up_" + tag, [hb], [wg, wu], [(0, 0, 0), (0, 1, 1)], 2, "nn", rows, D_FF, D_MODEL,
                  512, D_FF, D_MODEL, [bf16, bf16, bf16], epi_up, chunk=MXU_COLS)
    outs = _mm("ffn_down_" + tag, [s], [wd], [(0, 0, 0)], 1, "nn", rows, D_MODEL, D_FF,
               512, D_MODEL, D_FF, [f32] + [bf16] * len(next_nw), _residual_then_norm(0.5), extras=[x],
               vecs=next_nw)
    return outs[0], (outs[1] if next_nw else None), (x, hb, a, b, s)


def _ffn_bwd(tag, saved, nw, wg, wu, wd, dx, dxb, rows):
    x, hb, a, b, s = saved

    def epi_mid(accs, extras, vecs):
        ds = 0.5 * accs[0]
        av = extras[0].astype(f32)
        bv = extras[1].astype(f32)
        sg = _sigmoid(av)
        return [ds * bv * sg * (1.0 + av * (1.0 - sg)), ds * av * sg]
    da, db = _mm("ffn_bwd_mid_" + tag, [dxb], [wd], [(0, 0, 0)], 1, "nt", rows, D_FF, D_MODEL,
                 512, D_FF, D_MODEL, [bf16, bf16], epi_mid, extras=[a, b], chunk=MXU_COLS)
    d_wd = _mm1("ffn_dwd_" + tag, s, dxb, "tn", D_FF, D_MODEL, rows, FF_TILE, D_MODEL, TOKEN_K, out_dtype=bf16,
                scale=0.5)
    d_wg = _mm1("ffn_dwg_" + tag, hb, da, "tn", D_MODEL, D_FF, rows, D_MODEL, FF_TILE, TOKEN_K, out_dtype=bf16)
    d_wu = _mm1("ffn_dwu_" + tag, hb, db, "tn", D_MODEL, D_FF, rows, D_MODEL, FF_TILE, TOKEN_K, out_dtype=bf16)
    dx_in, dxb_in, d_nw = _mm("ffn_dh_" + tag, [da, db], [wg, wu], [(0, 0, 0), (1, 1, 0)], 1, "nt", rows, D_MODEL,
                              D_FF, 512, D_MODEL, FF_TILE, [f32, bf16], _norm_bwd_epilogue, extras=[x, dx], vecs=[nw],
                              n_part=1)
    return dx_in, dxb_in, jnp.sum(d_nw, axis=0), d_wg, d_wu, d_wd


S5_CB = 512
SUBLANES = 8
U_COL = GM_WIDTH // BRANCH


def _s5_scan_fwd(tag, proj, b_re, b_im, a_re, a_im, rows):
    tm = min(ROW_TILE, rows)
    nt = rows // tm
    nc = S5_LANES // S5_CB

    def kern(u_ref, bre_ref, bim_ref, ar_ref, ai_ref, xr_ref, xi_ref, pr_s, pi_s, cr_s, ci_s):
        t = pl.program_id(1)

        @pl.when(t == 0)
        def _():
            row8 = _rows((SUBLANES, S5_CB))
            pr = jnp.broadcast_to(ar_ref[...], (SUBLANES, S5_CB))
            pi = jnp.broadcast_to(ai_ref[...], (SUBLANES, S5_CB))
            s = 1
            while s < SUBLANES:
                sr = pltpu.roll(pr, s, 0)
                si = pltpu.roll(pi, s, 0)
                valid = row8 >= s
                pr, pi = jnp.where(valid, pr * sr - pi * si, pr), jnp.where(valid, pr * si + pi * sr, pi)
                s *= 2
            pr_s[...] = pr
            pi_s[...] = pi
            cr_s[...] = jnp.zeros_like(cr_s)
            ci_s[...] = jnp.zeros_like(ci_s)

        ub = u_ref[...].astype(bf16)
        br = _dot(ub, bre_ref[...])
        bi = _dot(ub, bim_ref[...])
        pos = _rows((tm, S5_CB)) % SUBLANES
        s = 1
        while s < SUBLANES:
            mr = pr_s[s - 1:s, :]
            mi = pi_s[s - 1:s, :]
            sr = pltpu.roll(br, s, 0)
            si = pltpu.roll(bi, s, 0)
            valid = pos >= s
            br, bi = (br + jnp.where(valid, mr * sr - mi * si, 0.0),
                      bi + jnp.where(valid, mr * si + mi * sr, 0.0))
            s *= 2
        cr = cr_s[...]
        ci = ci_s[...]
        pr = pr_s[...]
        pi = pi_s[...]
        for g in range(tm // SUBLANES):
            sl = slice(g * SUBLANES, (g + 1) * SUBLANES)
            xr = br[sl] + pr * cr - pi * ci
            xi = bi[sl] + pr * ci + pi * cr
            xr_ref[sl, :] = xr
            xi_ref[sl, :] = xi
            cr = xr[SUBLANES - 1:SUBLANES, :]
            ci = xi[SUBLANES - 1:SUBLANES, :]
        cr_s[...] = cr
        ci_s[...] = ci

    return pl.pallas_call(
        kern, name="s5_scan_fwd_" + tag, grid=(nc, nt),
        in_specs=[pl.BlockSpec((tm, BRANCH), lambda c, t: (t, U_COL)),
                  pl.BlockSpec((BRANCH, S5_CB), lambda c, t: (0, c)),
                  pl.BlockSpec((BRANCH, S5_CB), lambda c, t: (0, c)),
                  pl.BlockSpec((1, S5_CB), lambda c, t: (0, c)),
                  pl.BlockSpec((1, S5_CB), lambda c, t: (0, c))],
        out_specs=[pl.BlockSpec((tm, S5_CB), lambda c, t: (t, c))] * 2,
        out_shape=[jax.ShapeDtypeStruct((rows, S5_LANES), f32)] * 2,
        scratch_shapes=[pltpu.VMEM((SUBLANES, S5_CB), f32), pltpu.VMEM((SUBLANES, S5_CB), f32),
                        pltpu.VMEM((1, S5_CB), f32), pltpu.VMEM((1, S5_CB), f32)],
        compiler_params=_cparams(("parallel", "arbitrary")),
    )(proj, b_re, b_im, a_re, a_im)


def _s5_scan_bwd(tag, dxr, dxi, xr, xi, a_re, a_im, rows):
    tm = min(ROW_TILE, rows)
    nt = rows // tm
    nc = S5_LANES // S5_CB

    def kern(dxr_ref, dxi_ref, xr_ref, xi_ref, ar_ref, ai_ref, gr_ref, gi_ref, dar_ref, dai_ref,
             qr_s, qi_s, cr_s, ci_s, gr_s, gi_s):
        t = pl.program_id(1)
        row = _rows((tm, S5_CB))

        @pl.when(t == 0)
        def _():
            row8 = _rows((SUBLANES, S5_CB))
            qr = jnp.broadcast_to(ar_ref[...], (SUBLANES, S5_CB))
            qi = jnp.broadcast_to(-ai_ref[...], (SUBLANES, S5_CB))
            s = 1
            while s < SUBLANES:
                sr = pltpu.roll(qr, SUBLANES - s, 0)
                si = pltpu.roll(qi, SUBLANES - s, 0)
                valid = row8 < SUBLANES - s
                qr, qi = jnp.where(valid, qr * sr - qi * si, qr), jnp.where(valid, qr * si + qi * sr, qi)
                s *= 2
            qr_s[...] = qr
            qi_s[...] = qi
            cr_s[...] = jnp.zeros_like(cr_s)
            ci_s[...] = jnp.zeros_like(ci_s)
            dar_ref[...] = jnp.zeros_like(dar_ref)
            dai_ref[...] = jnp.zeros_like(dai_ref)

        br = dxr_ref[...]
        bi = dxi_ref[...]
        pos = row % SUBLANES
        s = 1
        while s < SUBLANES:
            mr = qr_s[SUBLANES - s:SUBLANES - s + 1, :]
            mi = qi_s[SUBLANES - s:SUBLANES - s + 1, :]
            sr = pltpu.roll(br, tm - s, 0)
            si = pltpu.roll(bi, tm - s, 0)
            valid = pos < SUBLANES - s
            br, bi = (br + jnp.where(valid, mr * sr - mi * si, 0.0),
                      bi + jnp.where(valid, mr * si + mi * sr, 0.0))
            s *= 2
        cin_r = cr_s[...]
        cin_i = ci_s[...]
        cr, ci = cin_r, cin_i
        qr = qr_s[...]
        qi = qi_s[...]
        for g in reversed(range(tm // SUBLANES)):
            sl = slice(g * SUBLANES, (g + 1) * SUBLANES)
            gr = br[sl] + qr * cr - qi * ci
            gi = bi[sl] + qr * ci + qi * cr
            gr_s[sl, :] = gr
            gi_s[sl, :] = gi
            cr = gr[0:1, :]
            ci = gi[0:1, :]
        cr_s[...] = cr
        ci_s[...] = ci
        gr = gr_s[...]
        gi = gi_s[...]
        gr_ref[...] = gr.astype(bf16)
        gi_ref[...] = gi.astype(bf16)
        last = row == tm - 1
        gnr = jnp.where(last, cin_r, pltpu.roll(gr, tm - 1, 0))
        gni = jnp.where(last, cin_i, pltpu.roll(gi, tm - 1, 0))
        xr_v = xr_ref[...]
        xi_v = xi_ref[...]
        dar_ref[...] += jnp.sum(gnr * xr_v + gni * xi_v, axis=0, keepdims=True)
        dai_ref[...] += jnp.sum(gni * xr_v - gnr * xi_v, axis=0, keepdims=True)

    rev = lambda c, t: (nt - 1 - t, c)
    return pl.pallas_call(
        kern, name="s5_scan_bwd_" + tag, grid=(nc, nt),
        in_specs=[pl.BlockSpec((tm, S5_CB), rev)] * 4 + [pl.BlockSpec((1, S5_CB), lambda c, t: (0, c))] * 2,
        out_specs=[pl.BlockSpec((tm, S5_CB), rev)] * 2 + [pl.BlockSpec((1, S5_CB), lambda c, t: (0, c))] * 2,
        out_shape=[jax.ShapeDtypeStruct((rows, S5_LANES), bf16)] * 2 + [jax.ShapeDtypeStruct((1, S5_LANES), f32)] * 2,
        scratch_shapes=[pltpu.VMEM((SUBLANES, S5_CB), f32), pltpu.VMEM((SUBLANES, S5_CB), f32),
                        pltpu.VMEM((1, S5_CB), f32), pltpu.VMEM((1, S5_CB), f32),
                        pltpu.VMEM((tm, S5_CB), f32), pltpu.VMEM((tm, S5_CB), f32)],
        compiler_params=_cparams(("parallel", "arbitrary")),
    )(dxr, dxi, xr, xi, a_re, a_im)


def _s5_fwd(tag, proj, cst, rows):
    xr, xi = _s5_scan_fwd(tag, proj, cst["b_re"].astype(bf16), cst["b_im"].astype(bf16), cst["a_re"], cst["a_im"], rows)

    def body(i, xr_ref, xi_ref, u_ref, cre_ref, cim_ref, d_ref, gw_ref, gb_ref, y_ref, out_ref):
        y = (_dot(xr_ref[...].astype(bf16), cre_ref[...]) + _dot(xi_ref[...].astype(bf16), cim_ref[...])
             + d_ref[...] * u_ref[...])
        y_ref[...] = y
        z = _gelu(y)
        zg = _dot(z.astype(bf16), gw_ref[...]) + gb_ref[...]
        out_ref[...] = (z * _sigmoid(zg)).astype(bf16)

    y, out = _rt("s5_out_" + tag, body, rows, ROW_TILE,
                 [(xr, S5_LANES, 0), (xi, S5_LANES, 0), (proj, BRANCH, U_COL)],
                 [cst["c_re"].astype(bf16), cst["c_im"].astype(bf16), cst["s5_d"], cst["glu_w"], cst["glu_b"]],
                 [(BRANCH, f32), (BRANCH, bf16)])
    return out, (xr, xi, y)


def _s5_bwd(tag, saved, proj, cst, d_out, rows):
    xr, xi, y = saved
    c_re = cst["c_re"].astype(bf16)
    c_im = cst["c_im"].astype(bf16)

    def body(i, do_ref, y_ref, u_ref, xr_ref, xi_ref, cre_ref, cim_ref, gw_ref, gb_ref,
             dxr_ref, dxi_ref, dy_ref, dgw_ref, dgb_ref, dd_ref, dcre_ref, dcim_ref):
        yv = y_ref[...]
        z = _gelu(yv)
        zb = z.astype(bf16)
        gt = _sigmoid(_dot(zb, gw_ref[...]) + gb_ref[...])
        dov = do_ref[...]
        dzg = dov * z * gt * (1.0 - gt)
        dzgb = dzg.astype(bf16)
        dz = dov * gt + _dot_nt(dzgb, gw_ref[...])
        dgw_ref[...] += _dot_tn(zb, dzgb)
        dgb_ref[...] += jnp.sum(dzg, axis=0, keepdims=True)
        dy = dz * _gelu_grad(yv)
        dy_ref[...] = dy
        dd_ref[...] += jnp.sum(dy * u_ref[...], axis=0, keepdims=True)
        dyb = dy.astype(bf16)
        dxr_ref[...] = _dot_nt(dyb, cre_ref[...])
        dxi_ref[...] = _dot_nt(dyb, cim_ref[...])
        dcre_ref[...] += _dot_tn(xr_ref[...].astype(bf16), dyb)
        dcim_ref[...] += _dot_tn(xi_ref[...].astype(bf16), dyb)

    dxr, dxi, dy, d_gw, d_gb, d_d, d_cre, d_cim = _rt(
        "s5_out_bwd_" + tag, body, rows, ROW_TILE,
        [(d_out, BRANCH, 0), (y, BRANCH, 0), (proj, BRANCH, U_COL), (xr, S5_LANES, 0), (xi, S5_LANES, 0)],
        [c_re, c_im, cst["glu_w"], cst["glu_b"]],
        [(S5_LANES, f32), (S5_LANES, f32), (BRANCH, f32)],
        acc_outs=[(BRANCH, BRANCH), (1, BRANCH), (1, BRANCH), (S5_LANES, BRANCH), (S5_LANES, BRANCH)])

    gr, gi, d_ar, d_ai = _s5_scan_bwd(tag, dxr, dxi, xr, xi, cst["a_re"], cst["a_im"], rows)
    b_re = cst["b_re"].astype(bf16)
    b_im = cst["b_im"].astype(bf16)

    def body_in(i, gr_ref, gi_ref, dy_ref, u_ref, bre_ref, bim_ref, d_ref, du_ref, dbre_ref, dbim_ref):
        grv = gr_ref[...]
        giv = gi_ref[...]
        du = _dot_nt(grv, bre_ref[...]) + _dot_nt(giv, bim_ref[...]) + dy_ref[...] * d_ref[...]
        du_ref[...] = du.astype(bf16)
        ub = u_ref[...].astype(bf16)
        dbre_ref[...] += _dot_tn(ub, grv)
        dbim_ref[...] += _dot_tn(ub, giv)

    du, d_bre, d_bim = _rt("s5_in_bwd_" + tag, body_in, rows, ROW_TILE,
                           [(gr, S5_LANES, 0), (gi, S5_LANES, 0), (dy, BRANCH, 0), (proj, BRANCH, U_COL)],
                           [b_re, b_im, cst["s5_d"]], [(BRANCH, bf16)],
                           acc_outs=[(BRANCH, S5_LANES), (BRANCH, S5_LANES)])
    dcst = {"b_re": d_bre, "b_im": d_bim, "a_re": d_ar, "a_im": d_ai, "c_re": d_cre, "c_im": d_cim,
            "s5_d": d_d, "glu_b": d_gb}
    return du, dcst, d_gw


def _hg_prep(q, z, lb):
    qs = _sigmoid(q)
    qh = q * qs
    sg = _sigmoid_small(z)
    fg = lb + (1.0 - lb) * sg
    kk = (1.0 - lb) * (1.0 - sg)
    return qs, qh, sg, fg, kk


def _hg_fwd(tag, proj, cst, rows):
    tm = min(ROW_TILE, rows)
    c_sz = HG_CHUNK
    nch = tm // c_sz
    n_chunks = rows // c_sz

    def body(i, q_ref, z_ref, v_ref, g_ref, lb_ref, nw_ref, out_ref, o_ref, ss_ref, sn_ref, st_s):
        @pl.when(i == 0)
        def _():
            st_s[...] = jnp.zeros_like(st_s)

        lb = lb_ref[...]
        _, qh, sg, fg, kk = _hg_prep(q_ref[...], z_ref[...], lb)
        b = _seg_cumsum(jnp.log(fg), tm, c_sz)
        qhat = (qh * jnp.exp(b)).astype(bf16)
        khat = (kk * jnp.exp(-b)).astype(bf16)
        vb = v_ref[...].astype(bf16)
        b3 = b.reshape(nch, c_sz, BRANCH)
        bl3 = b3[:, c_sz - 1:c_sz, :]
        kdec = (kk.reshape(nch, c_sz, BRANCH) * jnp.exp(bl3 - b3)).astype(bf16)
        ebl = jnp.exp(bl3)
        tril = (lax.broadcasted_iota(jnp.int32, (nch, c_sz, c_sz), 1)
                >= lax.broadcasted_iota(jnp.int32, (nch, c_sz, c_sz), 2))
        o_heads = []
        for h in range(HG_HEADS):
            hl = slice(h * HG_DK, (h + 1) * HG_DK)
            q3 = qhat[:, hl].reshape(nch, c_sz, HG_DK)
            k3 = khat[:, hl].reshape(nch, c_sz, HG_DK)
            v3 = vb[:, hl].reshape(nch, c_sz, HG_DK)
            a_mat = jnp.where(tril, _bdot_nt(q3, k3), 0.0).astype(bf16)
            o3 = _bdot(a_mat, v3)
            st = st_s[hl, :]
            before = []
            for ci in range(nch):
                before.append(st.astype(bf16))
                st = st * ebl[ci][:, hl] + _dot_tn(v3[ci], kdec[ci][:, hl])
                sn_ref[ci, hl, :] = st.astype(bf16)
            st_s[hl, :] = st
            s3 = jnp.stack(before)
            ss_ref[:, hl, :] = s3
            o3 = o3 + _bdot_nt(q3, s3)
            o_heads.append(o3.reshape(tm, HG_DK))
        o = jnp.concatenate(o_heads, axis=1)
        o_ref[...] = o
        r = lax.rsqrt(_head_mean(o * o) + EPS)
        g = g_ref[...]
        out_ref[...] = (o * r * nw_ref[...] * (g * _sigmoid(g))).astype(bf16)

    out, o, ss, sn = _rt(
        "hg_fwd_" + tag, body, rows, tm,
        [(proj, BRANCH, U_COL + 1), (proj, BRANCH, U_COL + 2), (proj, BRANCH, U_COL + 3), (proj, BRANCH, U_COL + 4)],
        [cst["hg_lb"], cst["hg_nw"]],
        [(BRANCH, bf16), (BRANCH, f32),
         ((n_chunks, BRANCH, HG_DK), (nch, BRANCH, HG_DK), lambda t: (t, 0, 0), bf16),
         ((n_chunks, BRANCH, HG_DK), (nch, BRANCH, HG_DK), lambda t: (t, 0, 0), bf16)],
        scratch=[pltpu.VMEM((BRANCH, HG_DK), f32)])
    return out, (o, ss, sn)


def _hg_bwd(tag, saved, proj, cst, d_out, rows):
    o_saved, ss, sn = saved
    tm = min(ROW_TILE, rows)
    c_sz = HG_CHUNK
    nch = tm // c_sz

    def body(i, do_ref, q_ref, z_ref, v_ref, g_ref, o_ref, ss_ref, sn_ref, lb_ref, nw_ref,
             dq_ref, dz_ref, dv_ref, dg_ref, dlb_ref, dnw_ref, dst_s):
        @pl.when(i == 0)
        def _():
            dst_s[...] = jnp.zeros_like(dst_s)

        lb = lb_ref[...]
        q = q_ref[...]
        qs, qh, sg, fg, kk = _hg_prep(q, z_ref[...], lb)
        b = _seg_cumsum(jnp.log(fg), tm, c_sz)
        eb = jnp.exp(b)
        enb = jnp.exp(-b)
        qhat = (qh * eb).astype(bf16)
        khat = (kk * enb).astype(bf16)
        vb = v_ref[...].astype(bf16)
        b3 = b.reshape(nch, c_sz, BRANCH)
        bl3 = b3[:, c_sz - 1:c_sz, :]
        dec3 = jnp.exp(bl3 - b3)
        kdec = (kk.reshape(nch, c_sz, BRANCH) * dec3).astype(bf16)
        ebl = jnp.exp(bl3)
        g = g_ref[...]
        gs = _sigmoid(g)
        o = o_ref[...]
        r = lax.rsqrt(_head_mean(o * o) + EPS)
        oh = o * r
        nw = nw_ref[...]
        dov = do_ref[...]
        don = dov * (g * gs)
        dg_ref[...] = (dov * oh * nw * (gs * (1.0 + g * (1.0 - gs)))).astype(bf16)
        dnw_ref[...] += jnp.sum(don * oh, axis=0, keepdims=True)
        doh = don * nw
        d_o = r * (doh - oh * _head_mean(doh * oh))
        dob = d_o.astype(bf16)
        t_idx = lax.broadcasted_iota(jnp.int32, (nch, c_sz, c_sz), 1)
        s_idx = lax.broadcasted_iota(jnp.int32, (nch, c_sz, c_sz), 2)
        heads = []
        for h in range(HG_HEADS):
            hl = slice(h * HG_DK, (h + 1) * HG_DK)
            q3 = qhat[:, hl].reshape(nch, c_sz, HG_DK)
            k3 = khat[:, hl].reshape(nch, c_sz, HG_DK)
            v3 = vb[:, hl].reshape(nch, c_sz, HG_DK)
            do3 = dob[:, hl].reshape(nch, c_sz, HG_DK)
            s3 = ss_ref[:, hl, :]
            da_mat = jnp.where(t_idx >= s_idx, _bdot_nt(do3, v3), 0.0).astype(bf16)
            a_t = jnp.where(t_idx <= s_idx, _bdot_nt(k3, q3), 0.0).astype(bf16)
            da_t = jnp.where(t_idx <= s_idx, _bdot_nt(v3, do3), 0.0).astype(bf16)
            dqhat = _bdot(do3, s3) + _bdot(da_mat, k3)
            dkhat = _bdot(da_t, q3)
            dst = dst_s[hl, :]
            after = [None] * nch
            for ci in reversed(range(nch)):
                after[ci] = dst
                dst = dst * ebl[ci][:, hl] + _dot_tn(do3[ci], q3[ci])
            dst_s[hl, :] = dst
            ds3 = jnp.stack(after)
            ds3b = ds3.astype(bf16)
            dk_inter = _bdot(v3, ds3b) * dec3[:, :, hl]
            dv3 = _bdot(a_t, do3) + _bdot_nt(kdec[:, :, hl], ds3b)
            flux = jnp.sum(sn_ref[:, hl, :].astype(f32) * ds3, axis=1, keepdims=True)
            heads.append((dqhat.reshape(tm, HG_DK), dkhat.reshape(tm, HG_DK), dk_inter.reshape(tm, HG_DK),
                          dv3.reshape(tm, HG_DK), jnp.broadcast_to(flux, (nch, c_sz, HG_DK)).reshape(tm, HG_DK)))
        dqhat, dkhat, dk_inter, dv, flux = (jnp.concatenate(parts, axis=1) for parts in zip(*heads))
        dv_ref[...] = dv.astype(bf16)
        dqh = dqhat * eb
        dk = dkhat * enb + dk_inter
        db = qhat.astype(f32) * dqhat - khat.astype(f32) * dkhat - kk * dk_inter
        dlf = _seg_rev_cumsum(db, tm, c_sz) + flux
        tt = (1.0 - lb) * sg * (1.0 - sg)
        dz_ref[...] = (dlf * tt / fg - dk * tt).astype(bf16)
        dlb_ref[...] += jnp.sum(dlf * (1.0 - sg) / fg - dk * (1.0 - sg), axis=0, keepdims=True)
        dq_ref[...] = (dqh * (qs * (1.0 + q * (1.0 - qs)))).astype(bf16)

    dq, dz, dv, dg, d_lb, d_nw = _rt(
        "hg_bwd_" + tag, body, rows, tm,
        [(d_out, BRANCH, 0), (proj, BRANCH, U_COL + 1), (proj, BRANCH, U_COL + 2), (proj, BRANCH, U_COL + 3),
         (proj, BRANCH, U_COL + 4), (o_saved, BRANCH, 0), (ss, (nch, BRANCH, HG_DK), lambda t: (t, 0, 0)),
         (sn, (nch, BRANCH, HG_DK), lambda t: (t, 0, 0))],
        [cst["hg_lb"], cst["hg_nw"]],
        [(BRANCH, bf16)] * 4, acc_outs=[(1, BRANCH), (1, BRANCH)],
        scratch=[pltpu.VMEM((BRANCH, HG_DK), f32)],
        reverse=True)
    return dq, dz, dv, dg, {"hg_lb": d_lb, "hg_nw": d_nw}


def _rg_gates(xc, wa_ref, ba_ref, wx_ref, bx_ref, sp8):
    xcb = xc.astype(bf16)
    r = _sigmoid(_dot(xcb, wa_ref[...]) + ba_ref[...])
    ig = _sigmoid(_dot(xcb, wx_ref[...]) + bx_ref[...])
    la = -sp8 * r
    a = jnp.exp(la)
    mult = jnp.sqrt(-_expm1(2.0 * la))
    return xcb, r, ig, a, mult


def _rg_fwd(tag, proj, cst, rows):
    tm = min(ROW_TILE, rows)

    def body(i, xb_ref, gate_ref, cw_ref, cb_ref, wa_ref, ba_ref, wx_ref, bx_ref, sp_ref,
             out_ref, xc_ref, h_ref, hp_ref, prev_s, hc_s):
        @pl.when(i == 0)
        def _():
            prev_s[...] = jnp.zeros_like(prev_s)
            hc_s[...] = jnp.zeros_like(hc_s)

        row = _rows((tm, BRANCH))
        xb = xb_ref[...]
        prev = prev_s[...]
        xc = cb_ref[...] + cw_ref[3:4, :] * xb
        for j in range(1, 4):
            sh = jnp.where(row >= j, pltpu.roll(xb, j, 0), pltpu.roll(prev, j, 0))
            xc = xc + cw_ref[3 - j:4 - j, :] * sh
        prev_s[...] = xb
        xc_ref[...] = xc
        _, r, ig, a, mult = _rg_gates(xc, wa_ref, ba_ref, wx_ref, bx_ref, sp_ref[...])
        a_cum, h_loc = _scan_fwd(a, mult * ig * xc, tm)
        hc = hc_s[...]
        h = h_loc + a_cum * hc
        h_ref[...] = h
        hp_ref[...] = jnp.where(row >= 1, pltpu.roll(h, 1, 0), hc)
        hc_s[...] = h[tm - 1:tm, :]
        out_ref[...] = (h * _gelu(gate_ref[...])).astype(bf16)

    out, xc, h, hp = _rt(
        "rg_fwd_" + tag, body, rows, tm,
        [(proj, BRANCH, U_COL + 5), (proj, BRANCH, U_COL + 6)],
        [cst["rg_cw"], cst["rg_cb"], cst["rg_wa"].astype(bf16), cst["rg_ba"], cst["rg_wx"].astype(bf16),
         cst["rg_bx"], cst["rg_sp8"]],
        [(BRANCH, bf16), (BRANCH, f32), (BRANCH, f32), (BRANCH, f32)],
        scratch=[pltpu.VMEM((tm, BRANCH), f32), pltpu.VMEM((1, BRANCH), f32)])
    return out, (xc, h, hp)


def _rg_bwd(tag, saved, proj, cst, d_out, rows):
    xc_saved, h_saved, hp_saved = saved
    tm = min(ROW_TILE, rows)

    def body(i, do_ref, xb_ref, gate_ref, xc_ref, h_ref, hp_ref, cw_ref, wa_ref, ba_ref, wx_ref, bx_ref, sp_ref,
             dxb_ref, dgate_ref, dcw_ref, dcb_ref, dwa_ref, dba_ref, dwx_ref, dbx_ref, dsp_ref,
             nxt_s, ec_s):
        @pl.when(i == 0)
        def _():
            nxt_s[...] = jnp.zeros_like(nxt_s)
            ec_s[...] = jnp.zeros_like(ec_s)

        row = _rows((tm, BRANCH))
        xc = xc_ref[...]
        sp8 = sp_ref[...]
        xcb, r, ig, a, mult = _rg_gates(xc, wa_ref, ba_ref, wx_ref, bx_ref, sp8)
        gate = gate_ref[...]
        dov = do_ref[...]
        dh = dov * _gelu(gate)
        dgate_ref[...] = (dov * h_ref[...] * _gelu_grad(gate)).astype(bf16)
        a_cum, e_loc = _scan_bwd(a, a * dh, tm)
        ec = ec_s[...]
        e = e_loc + a_cum * ec
        g_tot = dh + jnp.where(row == tm - 1, ec, pltpu.roll(e, tm - 1, 0))
        ec_s[...] = e[0:1, :]
        d_a = g_tot * hp_ref[...]
        d_mult = g_tot * ig * xc
        d_ix = g_tot * mult
        d_ig = d_ix * xc
        d_xc = d_ix * ig
        d_la = d_a * a - d_mult * (a * a) / mult
        d_r = -d_la * sp8
        dsp_ref[...] += jnp.sum(-d_la * r, axis=0, keepdims=True)
        dzr = d_r * r * (1.0 - r)
        dzi = d_ig * ig * (1.0 - ig)
        dzrb = dzr.astype(bf16)
        dzib = dzi.astype(bf16)
        d_xc = d_xc + _dot_nt(dzrb, wa_ref[...]) + _dot_nt(dzib, wx_ref[...])
        dwa_ref[...] += _dot_tn(xcb, dzrb)
        dwx_ref[...] += _dot_tn(xcb, dzib)
        dba_ref[...] += jnp.sum(dzr, axis=0, keepdims=True)
        dbx_ref[...] += jnp.sum(dzi, axis=0, keepdims=True)
        dcb_ref[...] += jnp.sum(d_xc, axis=0, keepdims=True)
        nxt = nxt_s[...]
        xb = xb_ref[...]
        dxb = cw_ref[3:4, :] * d_xc
        dcw_ref[3:4, :] += jnp.sum(d_xc * xb, axis=0, keepdims=True)
        for j in range(1, 4):
            sh = jnp.where(row < tm - j, pltpu.roll(d_xc, tm - j, 0), pltpu.roll(nxt, tm - j, 0))
            dxb = dxb + cw_ref[3 - j:4 - j, :] * sh
            dcw_ref[3 - j:4 - j, :] += jnp.sum(sh * xb, axis=0, keepdims=True)
        nxt_s[...] = d_xc
        dxb_ref[...] = dxb.astype(bf16)

    wa = cst["rg_wa"].astype(bf16)
    wx = cst["rg_wx"].astype(bf16)
    dxb, dgate, d_cw, d_cb, d_wa, d_ba, d_wx, d_bx, d_sp = _rt(
        "rg_bwd_" + tag, body, rows, tm,
        [(d_out, BRANCH, 0), (proj, BRANCH, U_COL + 5), (proj, BRANCH, U_COL + 6), (xc_saved, BRANCH, 0),
         (h_saved, BRANCH, 0), (hp_saved, BRANCH, 0)],
        [cst["rg_cw"], wa, cst["rg_ba"], wx, cst["rg_bx"], cst["rg_sp8"]],
        [(BRANCH, bf16), (BRANCH, bf16)],
        acc_outs=[(4, BRANCH), (1, BRANCH), (BRANCH, BRANCH), (1, BRANCH), (BRANCH, BRANCH), (1, BRANCH), (1, BRANCH)],
        scratch=[pltpu.VMEM((tm, BRANCH), f32), pltpu.VMEM((1, BRANCH), f32)],
        reverse=True)
    dcst = {"rg_cw": d_cw, "rg_cb": d_cb, "rg_wa": d_wa, "rg_ba": d_ba, "rg_wx": d_wx, "rg_bx": d_bx, "rg_sp8": d_sp}
    return dxb, dgate, dcst


def _merge_fwd(tag, proj, outs, bp, rows):
    def body(i, ya_ref, yb_ref, yc_ref, gm_ref, p_ref, m_ref):
        acc = None
        for n, y_ref in enumerate((ya_ref, yb_ref, yc_ref)):
            up = _dot(y_ref[...], p_ref[n])
            term = _sigmoid(gm_ref[:, n * D_MODEL:(n + 1) * D_MODEL]) * up
            acc = term if acc is None else acc + term
        m_ref[...] = acc.astype(bf16)
    return _rt("merge_fwd_" + tag, body, rows, ROW_TILE,
               [(outs[0], BRANCH, 0), (outs[1], BRANCH, 0), (outs[2], BRANCH, 0), (proj, GM_WIDTH, 0)],
               [bp], [(D_MODEL, bf16)])[0]


def _merge_bwd(tag, proj, outs, bp, dmerged, rows):
    def body(i, dm_ref, ya_ref, yb_ref, yc_ref, gm_ref, p_ref, da_ref, db_ref, dc_ref, dgm_ref, dp_ref):
        dm = dm_ref[...]
        for n, (y_ref, dy_ref) in enumerate(((ya_ref, da_ref), (yb_ref, db_ref), (yc_ref, dc_ref))):
            yv = y_ref[...]
            up = _dot(yv, p_ref[n])
            gt = _sigmoid(gm_ref[:, n * D_MODEL:(n + 1) * D_MODEL])
            dup = (dm * gt).astype(bf16)
            dgm_ref[:, n * D_MODEL:(n + 1) * D_MODEL] = (dm * up * gt * (1.0 - gt)).astype(bf16)
            dy_ref[...] = _dot_nt(dup, p_ref[n])
            dp_ref[n] += _dot_tn(yv, dup)
    return _rt("merge_bwd_" + tag, body, rows, ROW_TILE,
               [(dmerged, D_MODEL, 0), (outs[0], BRANCH, 0), (outs[1], BRANCH, 0), (outs[2], BRANCH, 0),
                (proj, GM_WIDTH, 0)],
               [bp], [(BRANCH, f32), (BRANCH, f32), (BRANCH, f32), (GM_WIDTH, bf16)],
               acc_outs=[(N_BRANCH, BRANCH, D_MODEL)])


def _block_diag(blocks):
    g, r, c = blocks.shape
    on_diag = (lax.broadcasted_iota(jnp.int32, (g * r, g * c), 0) // r
               == lax.broadcasted_iota(jnp.int32, (g * r, g * c), 1) // c)
    tiled = jnp.broadcast_to(blocks.reshape(g * r, 1, c), (g * r, g, c)).reshape(g * r, g * c)
    return jnp.where(on_diag, tiled, 0.0)


def _prep_consts(sp):
    p = jax.nn.softmax(sp["hg_lb_logits"], axis=0)
    lower = jnp.cumsum(p, axis=0) - p[0]
    out = []
    for l in range(DEPTH):
        lr = jnp.minimum(sp["s5_lambda_re"][l], S5_EIG_MAX)
        li = sp["s5_lambda_im"][l]
        dt = jnp.exp(sp["s5_log_dt"][l])[:, None]
        mag = jnp.exp(lr * dt)
        ar = mag * jnp.cos(li * dt)
        ai = mag * jnp.sin(li * dt)
        den = lr * lr + li * li
        fr = ((ar - 1.0) * lr + ai * li) / den
        fi = (ai * lr - (ar - 1.0) * li) / den
        br, bi = sp["s5_b_re"][l], sp["s5_b_im"][l]
        bbr = fr[..., None] * br - fi[..., None] * bi
        bbi = fr[..., None] * bi + fi[..., None] * br
        c = {
            "a_re": ar.reshape(1, S5_LANES), "a_im": ai.reshape(1, S5_LANES),
            "b_re": _block_diag(bbr.transpose(0, 2, 1)), "b_im": _block_diag(bbi.transpose(0, 2, 1)),
            "c_re": _block_diag(sp["s5_c_re"][l].transpose(0, 2, 1)),
            "c_im": -_block_diag(sp["s5_c_im"][l].transpose(0, 2, 1)),
            "s5_d": sp["s5_d"][l][None], "glu_b": sp["s5_glu_b"][l][None],
            "hg_lb": lower[l][None], "hg_nw": sp["hg_norm_w"][l][None],
            "rg_cw": sp["rg_conv_w"][l], "rg_cb": sp["rg_conv_b"][l][None],
            "rg_wa": _block_diag(sp["rg_wa"][l]), "rg_ba": sp["rg_ba"][l][None],
            "rg_wx": _block_diag(sp["rg_wx"][l]), "rg_bx": sp["rg_bx"][l][None],
            "rg_sp8": (RG_C * jax.nn.softplus(-sp["rg_lambda"][l]))[None],
        }
        out.append(c)
    return out


def _mixer_fwd(tag, x, hb, w_in, bp, w_out, cst, next_nw, rows):
    proj = _mm1("mix_proj_" + tag, hb, w_in, "nn", rows, IN_TOTAL, D_MODEL, 512, IN_TOTAL // 4, D_MODEL)
    cst = dict(cst)
    out_a, sv_a = _s5_fwd(tag, proj, cst, rows)
    out_b, sv_b = _hg_fwd(tag, proj, cst, rows)
    out_c, sv_c = _rg_fwd(tag, proj, cst, rows)
    merged = _merge_fwd(tag, proj, (out_a, out_b, out_c), bp, rows)
    x_out, hb_out = _mm("mix_out_" + tag, [merged], [w_out], [(0, 0, 0)], 1, "nn", rows, D_MODEL, D_MODEL,
                        512, D_MODEL, D_MODEL, [f32, bf16], _residual_then_norm(1.0), extras=[x], vecs=[next_nw])
    return x_out, hb_out, (x, hb, proj, (out_a, out_b, out_c), merged, sv_a, sv_b, sv_c)


def _mixer_bwd(tag, saved, nw, w_in, bp, w_out, cst, dx, dxb, rows):
    x, hb, proj, outs, merged, sv_a, sv_b, sv_c = saved
    d_wout = _mm1("mix_dwout_" + tag, merged, dxb, "tn", D_MODEL, D_MODEL, rows, D_MODEL, D_MODEL, TOKEN_K,
                  out_dtype=bf16)
    dmerged = _mm1("mix_dmerged_" + tag, dxb, w_out, "nt", rows, D_MODEL, D_MODEL, 512, D_MODEL, D_MODEL)
    d_a, d_b, d_c, dgm, d_bp = _merge_bwd(tag, proj, outs, bp, dmerged, rows)
    dxbc, dgatec, dcst_c = _rg_bwd(tag, sv_c, proj, cst, d_c, rows)
    dq, dz, dv, dg, dcst_b = _hg_bwd(tag, sv_b, proj, cst, d_b, rows)
    du, dcst_a, d_glu_w = _s5_bwd(tag, sv_a, proj, cst, d_a, rows)
    dproj = jnp.concatenate([dgm, du, dq, dz, dv, dg, dxbc, dgatec], axis=1)
    d_win = _mm1("mix_dwin_" + tag, hb, dproj, "tn", D_MODEL, IN_TOTAL, rows, D_MODEL, IN_TOTAL // 4, TOKEN_K,
                 out_dtype=bf16)
    dx_in, dxb_in, d_nw = _mm("mix_dh_" + tag, [dproj], [w_in], [(0, 0, 0)], 1, "nt", rows, D_MODEL, IN_TOTAL,
                              512, D_MODEL, IN_TOTAL // 2, [f32, bf16], _norm_bwd_epilogue, extras=[x, dx], vecs=[nw],
                              n_part=1)
    dcst = {**dcst_a, **dcst_b, **dcst_c}
    return dx_in, dxb_in, jnp.sum(d_nw, axis=0), d_win, d_bp, d_wout, d_glu_w, dcst


def _local_step(x, target, weights_of, small, grads_done):
    rows = x.shape[0]
    consts, consts_vjp = jax.vjp(_prep_consts, small)
    norm_w = small["norm_w"]
    saved = []
    h = x
    hb = _rms_fwd("first_norm", x, norm_w[0, 0][None], rows)
    for l in range(DEPTH):
        t = str(l)
        after = [norm_w[l + 1, 0][None]] if l + 1 < DEPTH else []
        wa = weights_of(l, "a")
        h, hb, sv0 = _ffn_fwd(t + "a", h, hb, *wa, [norm_w[l, 1][None]], rows)
        wm = weights_of(l, "mix")
        cst = dict(consts[l])
        cst["glu_w"] = wm[3]
        h, hb, sv1 = _mixer_fwd(t, h, hb, *wm[:3], cst, norm_w[l, 2][None], rows)
        wb = weights_of(l, "b")
        h, hb, sv2 = _ffn_fwd(t + "b", h, hb, *wb, after, rows)
        saved.append((sv0, sv1, sv2, cst, wa, wm, wb))
    dx, dxb, loss, d_fnw = _loss_head(h, small["final_norm_w"][None], target, rows)
    d_norm = [None] * DEPTH
    d_consts = [None] * DEPTH
    for l in reversed(range(DEPTH)):
        t = str(l)
        sv0, sv1, sv2, cst, wa, wm, wb = saved[l]
        dx, dxb, dn2, dg1, du1, dd1 = _ffn_bwd(t + "b", sv2, norm_w[l, 2][None], *wb, dx, dxb, rows)
        grads_done(l, "b", [dg1, du1, dd1])
        dx, dxb, dn1, d_win, d_bp, d_wout, d_glu_w, dcst = _mixer_bwd(
            t, sv1, norm_w[l, 1][None], *wm[:3], cst, dx, dxb, rows)
        grads_done(l, "mix", [d_win, d_bp, d_wout, d_glu_w])
        dx, dxb, dn0, dg0, du0, dd0 = _ffn_bwd(t + "a", sv0, norm_w[l, 0][None], *wa, dx, dxb, rows)
        grads_done(l, "a", [dg0, du0, dd0])
        d_norm[l] = jnp.concatenate([dn0, dn1, dn2], axis=0)
        d_consts[l] = dcst
    (g_small,) = consts_vjp(d_consts)
    g_small = dict(g_small)
    g_small["norm_w"] = g_small["norm_w"] + jnp.stack(d_norm)
    g_small["final_norm_w"] = g_small["final_norm_w"] + d_fnw[0]
    return loss[0, 0], dx, g_small


def _other_chips(x, y):
    return [(1 - x, y), (x, 1 - y), (1 - x, 1 - y)]


def _gather_over_ici(shards):
    n = len(shards)

    def copies(in_refs, out_refs, send_sems, recv_sems):
        x, y, c = _place()
        cps = []
        for i in range(n):
            mine = out_refs[i].at[4 * x + 2 * y + c]
            cps.append(pltpu.make_async_copy(in_refs[i], mine, send_sems.at[5 * i + 4]))
            for k, to in enumerate([(x, y, 1 - c)] + [(px, py, c) for px, py in _other_chips(x, y)]):
                cps.append(pltpu.make_async_remote_copy(
                    src_ref=in_refs[i], dst_ref=mine, send_sem=send_sems.at[5 * i + k],
                    recv_sem=recv_sems.at[5 * i + k], device_id=to, device_id_type=MESH_IDS))
        return cps

    return _Carry(shards, [jax.ShapeDtypeStruct((N_DEV,) + s.shape, s.dtype) for s in shards], 5 * n, copies)


def _gather_forward(name, landings):
    n = len(landings)

    def body(*refs):
        in_refs, out_refs = refs[:n], refs[n:2 * n]
        send_sems, recv_sems = refs[2 * n:]
        x, y, c = _place()
        cps = []
        for i in range(n):
            for j, (px, py) in enumerate(_other_chips(x, y)):
                block = 4 * px + 2 * py + c
                cps.append(pltpu.make_async_remote_copy(
                    src_ref=in_refs[i].at[block], dst_ref=out_refs[i].at[block], send_sem=send_sems.at[3 * i + j],
                    recv_sem=recv_sems.at[3 * i + j], device_id=(x, y, 1 - c), device_id_type=MESH_IDS))
        for cp in cps:
            cp.start()
        for cp in cps:
            cp.wait()

    return pl.pallas_call(
        body, name=name, out_shape=[jax.ShapeDtypeStruct(a.shape, a.dtype) for a in landings],
        in_specs=[ANY_SPEC] * n, out_specs=[ANY_SPEC] * n, input_output_aliases={i: i for i in range(n)},
        scratch_shapes=[pltpu.SemaphoreType.DMA((3 * n,)), pltpu.SemaphoreType.DMA((3 * n,))],
    )(*landings)


def _all_gather(name, shards):
    n = len(shards)

    def body(*refs):
        x_refs, out_refs = refs[:n], refs[n:2 * n]
        send_sems, recv_sems, local_sems = refs[2 * n:]
        x, y, c = _place()
        me, sibling = (x, y, c), (x, y, 1 - c)
        chips = [(1 - x, y), (x, 1 - y), (1 - x, 1 - y)]

        def blk(i, px, py, pc):
            return out_refs[i].at[4 * px + 2 * py + pc]

        def copy(i, k, block, to, src=None):
            return pltpu.make_async_remote_copy(
                src_ref=blk(i, *block) if src is None else src, dst_ref=blk(i, *block),
                send_sem=send_sems.at[7 * i + k], recv_sem=recv_sems.at[7 * i + k], device_id=to,
                device_id_type=MESH_IDS)

        mine = [pltpu.make_async_copy(x_refs[i], blk(i, *me), local_sems.at[i]) for i in range(n)]
        for cp in mine:
            cp.start()
        first = []
        for i in range(n):
            first.append(copy(i, 0, me, sibling, src=x_refs[i]))
            first += [copy(i, 1 + j, me, (*chip, c), src=x_refs[i]) for j, chip in enumerate(chips)]
        for cp in first:
            cp.start()
        passed = []
        for j, chip in enumerate(chips):
            for i in range(n):
                copy(i, 1 + j, (*chip, c), me).wait_recv()
                fwd = copy(i, 4 + j, (*chip, c), sibling)
                fwd.start()
                passed.append(fwd)
        for i in range(n):
            copy(i, 0, sibling, me).wait_recv()
            for j, chip in enumerate(chips):
                copy(i, 4 + j, (*chip, 1 - c), me).wait_recv()
        for cp in first + passed:
            cp.wait_send()
        for cp in mine:
            cp.wait()

    return pl.pallas_call(
        body, name=name, out_shape=[jax.ShapeDtypeStruct((N_DEV,) + s.shape, s.dtype) for s in shards],
        in_specs=[ANY_SPEC] * n, out_specs=[ANY_SPEC] * n,
        scratch_shapes=[pltpu.SemaphoreType.DMA((7 * n,)), pltpu.SemaphoreType.DMA((7 * n,)),
                        pltpu.SemaphoreType.DMA((n,))],
    )(*shards)


def _row_tile(rows):
    return rows if rows <= 512 else next(t for t in range(512, 7, -8) if rows % t == 0)


def _sums_over_ici(chip_sums):
    n = len(chip_sums)

    def copies(in_refs, out_refs, send_sems, recv_sems):
        x, y, c = _place()
        return [pltpu.make_async_remote_copy(
            src_ref=in_refs[i].at[2 * px + py], dst_ref=out_refs[i].at[k], send_sem=send_sems.at[3 * i + k],
            recv_sem=recv_sems.at[3 * i + k], device_id=(px, py, c), device_id_type=MESH_IDS)
            for i in range(n) for k, (px, py) in enumerate(_other_chips(x, y))]

    return _Carry(chip_sums, [jax.ShapeDtypeStruct((3,) + t.shape[1:], t.dtype) for t in chip_sums], 3 * n, copies)


def _reduce_scatter(tag, parts, hosts=None):
    n = len(parts)
    _, _, c = _place()

    def body_pair(*refs):
        p_refs, got_refs = refs[:n], refs[n:2 * n]
        send_sems, recv_sems = refs[2 * n:]
        x, y, c = _place()
        cps = [pltpu.make_async_remote_copy(
            src_ref=p_refs[i].at[1 - c], dst_ref=got_refs[i], send_sem=send_sems.at[i], recv_sem=recv_sems.at[i],
            device_id=(x, y, 1 - c), device_id_type=MESH_IDS) for i in range(n)]
        for cp in cps:
            cp.start()
        for cp in cps:
            cp.wait()

    from_sibling = pl.pallas_call(
        body_pair, name="rs_pair_" + tag, out_shape=[jax.ShapeDtypeStruct(p.shape[1:], p.dtype) for p in parts],
        in_specs=[ANY_SPEC] * n, out_specs=[ANY_SPEC] * n,
        scratch_shapes=[pltpu.SemaphoreType.DMA((n,)), pltpu.SemaphoreType.DMA((n,))],
    )(*parts)

    chip_sums = []
    for i, (part, got) in enumerate(zip(parts, from_sibling)):
        _, _, r, cols = part.shape
        tr = _row_tile(r)

        def body_add(idx_ref, p_ref, g_ref, o_ref):
            o_ref[...] = (p_ref[...].astype(f32) + g_ref[...].astype(f32)).astype(o_ref.dtype)

        chip_sums.append(pl.pallas_call(
            body_add, name="rs_pair_sum_%s_%d" % (tag, i), out_shape=jax.ShapeDtypeStruct((4, r, cols), part.dtype),
            grid_spec=pltpu.PrefetchScalarGridSpec(
                num_scalar_prefetch=1, grid=(4, r // tr),
                in_specs=[pl.BlockSpec((None, None, tr, cols), lambda j, t, idx: (idx[0], j, t, 0)),
                          pl.BlockSpec((None, tr, cols), lambda j, t, idx: (j, t, 0))],
                out_specs=pl.BlockSpec((None, tr, cols), lambda j, t, idx: (j, t, 0))),
            compiler_params=_cparams(("parallel", "parallel")),
        )(jnp.stack([c]).astype(jnp.int32), part, got))

    others = [None] * n
    riding = set()
    for host, which in (hosts or {}).items():
        rider = _sums_over_ici([chip_sums[i] for i in which])
        _CARRIED[host] = rider
        for pos, i in enumerate(which):
            others[i] = functools.partial(lambda r, p: r.outs[p], rider, pos)
        riding.update(which)
    rest = [i for i in range(n) if i not in riding]
    if rest:
        alone = _sums_over_ici([chip_sums[i] for i in rest])

        def body_chips(*refs):
            k = len(rest)
            cps = alone.copies(refs[:k], refs[k:2 * k], *refs[2 * k:])
            for cp in cps:
                cp.start()
            for cp in cps:
                cp.wait()

        from_chips = pl.pallas_call(
            body_chips, name="rs_chips_" + tag, out_shape=alone.out_shapes,
            in_specs=[ANY_SPEC] * len(rest), out_specs=[ANY_SPEC] * len(rest), scratch_shapes=alone.sems(),
        )(*alone.ins)
        for pos, i in enumerate(rest):
            others[i] = functools.partial(lambda got: got, from_chips[pos])
    return list(zip(chip_sums, others))


def _own_index():
    x, y, _ = _place()
    return jnp.stack([2 * x + y]).astype(jnp.int32)


def _own_total(name, chip_sum, others):
    _, r, cols = chip_sum.shape
    tr = _row_tile(r)

    def body(idx_ref, t_ref, g_ref, o_ref):
        o_ref[...] = ((t_ref[...].astype(f32) + g_ref[0].astype(f32)) + g_ref[1].astype(f32)) + g_ref[2].astype(f32)

    return pl.pallas_call(
        body, name=name, out_shape=jax.ShapeDtypeStruct((r, cols), f32),
        grid_spec=pltpu.PrefetchScalarGridSpec(
            num_scalar_prefetch=1, grid=(r // tr,),
            in_specs=[pl.BlockSpec((None, tr, cols), lambda t, idx: (idx[0], t, 0)),
                      pl.BlockSpec((3, tr, cols), lambda t, idx: (0, t, 0))],
            out_specs=pl.BlockSpec((tr, cols), lambda t, idx: (t, 0))),
        compiler_params=_cparams(("parallel",)),
    )(_own_index(), chip_sum, others)


def _adam_update(w, gv, m, v):
    m_new = ADAM_B1 * m + (1.0 - ADAM_B1) * gv
    v_new = ADAM_B2 * v + (1.0 - ADAM_B2) * (gv * gv)
    m_hat = m_new / (1.0 - ADAM_B1 ** ADAM_STEP)
    v_hat = v_new / (1.0 - ADAM_B2 ** ADAM_STEP)
    return -ADAM_LR * (m_hat / (jnp.sqrt(v_hat) + ADAM_EPS) + ADAM_WD * w), m_new, v_new


def _adamw_reduced(name, w, pieces, m, v):
    n_p, rows, cols = w.shape
    tr = _row_tile(rows)

    def body(idx_ref, w_ref, *refs):
        red = refs[:2 * n_p]
        m_ref, v_ref, g_ref, d_ref, nm_ref, nv_ref = refs[2 * n_p:]
        p = pl.program_id(0)
        for q in range(n_p):
            @pl.when(p == q)
            def _(t_ref=red[2 * q], o_ref=red[2 * q + 1]):
                gv = ((t_ref[...].astype(f32) + o_ref[0].astype(f32)) + o_ref[1].astype(f32)) + o_ref[2].astype(f32)
                g_ref[...] = gv
                d_ref[...], nm_ref[...], nv_ref[...] = _adam_update(w_ref[...], gv, m_ref[...], v_ref[...])

    spec = pl.BlockSpec((None, tr, cols), lambda p, t, idx: (p, t, 0))
    red_specs, red_args = [], []
    for q, (chip_sum, others) in enumerate(pieces):
        red_specs.append(pl.BlockSpec((None, tr, cols), lambda p, t, idx, q=q: (idx[0], jnp.where(p == q, t, 0), 0)))
        red_specs.append(pl.BlockSpec((3, tr, cols), lambda p, t, idx, q=q: (0, jnp.where(p == q, t, 0), 0)))
        red_args += [chip_sum, others]
    return pl.pallas_call(
        body, name=name, out_shape=[jax.ShapeDtypeStruct((n_p, rows, cols), f32)] * 4,
        grid_spec=pltpu.PrefetchScalarGridSpec(
            num_scalar_prefetch=1, grid=(n_p, rows // tr),
            in_specs=[spec] + red_specs + [spec, spec], out_specs=[spec] * 4),
        compiler_params=_cparams(("parallel", "parallel")),
    )(_own_index(), w, *red_args, m, v)


def _adamw(name, w, g, m, v):
    rows, cols = w.shape
    tr = _row_tile(rows)

    def body(w_ref, g_ref, m_ref, v_ref, d_ref, nm_ref, nv_ref):
        d_ref[...], nm_ref[...], nv_ref[...] = _adam_update(w_ref[...], g_ref[...], m_ref[...], v_ref[...])

    spec = pl.BlockSpec((tr, cols), lambda i: (i, 0))
    return pl.pallas_call(
        body, name=name, grid=(rows // tr,), in_specs=[spec] * 4, out_specs=[spec] * 3,
        out_shape=[jax.ShapeDtypeStruct((rows, cols), f32)] * 3, compiler_params=_cparams(("parallel",)),
    )(w, g, m, v)


WEIGHT_NAMES = ["norm_w", "final_norm_w", "ffn_gate", "ffn_up", "ffn_down", "w_in", "branch_proj", "w_out",
                "s5_lambda_re", "s5_lambda_im", "s5_log_dt", "s5_b_re", "s5_b_im", "s5_c_re", "s5_c_im", "s5_d",
                "s5_glu_w", "s5_glu_b", "hg_lb_logits", "hg_norm_w", "rg_conv_w", "rg_conv_b", "rg_wa", "rg_ba",
                "rg_wx", "rg_bx", "rg_lambda"]
SHARDED = {"ffn_gate": (3, "gate"), "ffn_up": (3, "up"), "ffn_down": (2, "down"), "w_in": (2, "w_in"),
           "branch_proj": (3, "bp"), "w_out": (1, "w_out"), "s5_glu_w": (1, "glu_w"),
           "norm_w": (2, None), "rg_conv_w": (2, None)}
BIG = ["ffn_gate", "ffn_up", "ffn_down", "w_in", "branch_proj", "w_out", "s5_glu_w"]
PARTS = {"a": [("ffn_gate", 0, 1), ("ffn_up", 0, 1), ("ffn_down", 0, 0)],
         "b": [("ffn_gate", 1, 1), ("ffn_up", 1, 1), ("ffn_down", 1, 0)],
         "mix": [("w_in", None, 1), ("branch_proj", None, 2), ("w_out", None, 0), ("s5_glu_w", None, 0)]}
AG_HOSTS = {"ffn_up_0a": (0, "mix", [0]), "ffn_down_0a": (0, "mix", [1, 2, 3]), "mix_proj_0": (0, "b", [0, 1, 2]),
            "s5_out_0": (1, "a", [0]), "hg_fwd_0": (1, "a", [1]), "rg_fwd_0": (1, "a", [2]),
            "merge_fwd_0": (1, "b", [0]), "ffn_up_0b": (1, "b", [1, 2]),
            "mix_out_0": (1, "mix", [2, 3]), "ffn_down_0b": (1, "mix", [1]), "ffn_up_1a": (1, "mix", [0])}
RS_HOSTS = {(1, "b"): {"merge_bwd_1": [0, 1], "rg_bwd_1": [2]},
            (1, "mix"): {"ffn_bwd_mid_1a": [1, 2, 3], "ffn_dh_1a": [0]},
            (1, "a"): {"ffn_dh_0b": [0, 1], "merge_bwd_0": [2]},
            (0, "b"): {"rg_bwd_0": [0, 1], "hg_bwd_0": [2]},
            (0, "mix"): {"ffn_bwd_mid_0a": [1, 2, 3], "ffn_dh_0a": [0]}}
SMALL_SHARDED = ["norm_w", "rg_conv_w"]
REPLICATED = [n for n in WEIGHT_NAMES if n not in SHARDED]
LANES = 128


PACK_ROWS = 512


def _pack_rows(arrays, names):
    pieces = []
    for n in names:
        flat = arrays[n].reshape(-1)
        pieces.append(jnp.pad(flat, (0, -flat.shape[0] % LANES)).reshape(-1, LANES))
    rows = jnp.concatenate(pieces, axis=0)
    return jnp.pad(rows, ((0, -rows.shape[0] % PACK_ROWS), (0, 0)))


def _unpack_rows(rows, names, like):
    out, r0 = {}, 0
    for n in names:
        size = math.prod(like[n].shape)
        nrows = -(-size // LANES)
        out[n] = rows[r0:r0 + nrows].reshape(-1)[:size].reshape(like[n].shape)
        r0 += nrows
    return out


def _unshard(gathered, axis):
    g = jnp.moveaxis(gathered, 0, axis)
    shp = g.shape
    return g.reshape(shp[:axis] + (shp[axis] * shp[axis + 1],) + shp[axis + 2:])


def _to_blocks(full, axis):
    shp = full.shape
    g = full.reshape(shp[:axis] + (4, 2, shp[axis] // N_DEV) + shp[axis + 1:])
    g = jnp.moveaxis(g, (axis, axis + 1), (1, 0))
    return g.reshape(2, 4, -1, g.shape[-1])


W_IN_SPLIT = IN_TOTAL - GM_WIDTH


def kernel(x, norm_w, final_norm_w, ffn_gate, ffn_up, ffn_down, w_in, branch_proj, w_out, s5_lambda_re, s5_lambda_im, s5_log_dt, s5_b_re, s5_b_im, s5_c_re, s5_c_im, s5_d, s5_glu_w, s5_glu_b, hg_lb_logits, hg_norm_w, rg_conv_w, rg_conv_b, rg_wa, rg_ba, rg_wx, rg_bx, rg_lambda, loss_target, m_norm_w, m_final_norm_w, m_ffn_gate, m_ffn_up, m_ffn_down, m_w_in, m_branch_proj, m_w_out, m_s5_lambda_re, m_s5_lambda_im, m_s5_log_dt, m_s5_b_re, m_s5_b_im, m_s5_c_re, m_s5_c_im, m_s5_d, m_s5_glu_w, m_s5_glu_b, m_hg_lb_logits, m_hg_norm_w, m_rg_conv_w, m_rg_conv_b, m_rg_wa, m_rg_ba, m_rg_wx, m_rg_bx, m_rg_lambda, v_norm_w, v_final_norm_w, v_ffn_gate, v_ffn_up, v_ffn_down, v_w_in, v_branch_proj, v_w_out, v_s5_lambda_re, v_s5_lambda_im, v_s5_log_dt, v_s5_b_re, v_s5_b_im, v_s5_c_re, v_s5_c_im, v_s5_d, v_s5_glu_w, v_s5_glu_b, v_hg_lb_logits, v_hg_norm_w, v_rg_conv_w, v_rg_conv_b, v_rg_wa, v_rg_ba, v_rg_wx, v_rg_bx, v_rg_lambda):
    w = dict(zip(WEIGHT_NAMES, (norm_w, final_norm_w, ffn_gate, ffn_up, ffn_down, w_in, branch_proj, w_out,
                                s5_lambda_re, s5_lambda_im, s5_log_dt, s5_b_re, s5_b_im, s5_c_re, s5_c_im, s5_d,
                                s5_glu_w, s5_glu_b, hg_lb_logits, hg_norm_w, rg_conv_w, rg_conv_b, rg_wa, rg_ba,
                                rg_wx, rg_bx, rg_lambda)))
    m = dict(zip(WEIGHT_NAMES, (m_norm_w, m_final_norm_w, m_ffn_gate, m_ffn_up, m_ffn_down, m_w_in, m_branch_proj,
                                m_w_out, m_s5_lambda_re, m_s5_lambda_im, m_s5_log_dt, m_s5_b_re, m_s5_b_im, m_s5_c_re,
                                m_s5_c_im, m_s5_d, m_s5_glu_w, m_s5_glu_b, m_hg_lb_logits, m_hg_norm_w, m_rg_conv_w,
                                m_rg_conv_b, m_rg_wa, m_rg_ba, m_rg_wx, m_rg_bx, m_rg_lambda)))
    v = dict(zip(WEIGHT_NAMES, (v_norm_w, v_final_norm_w, v_ffn_gate, v_ffn_up, v_ffn_down, v_w_in, v_branch_proj,
                                v_w_out, v_s5_lambda_re, v_s5_lambda_im, v_s5_log_dt, v_s5_b_re, v_s5_b_im, v_s5_c_re,
                                v_s5_c_im, v_s5_d, v_s5_glu_w, v_s5_glu_b, v_hg_lb_logits, v_hg_norm_w, v_rg_conv_w,
                                v_rg_conv_b, v_rg_wa, v_rg_ba, v_rg_wx, v_rg_bx, v_rg_lambda)))
    rows = x.shape[1]

    _CARRIED.clear()

    def shard_of(piece, l):
        n, k, _ = piece
        return (w[n][l] if k is None else w[n][l, k]).astype(bf16)

    def assemble(part, gathered):
        full = [_unshard(g, piece[2]) for piece, g in zip(PARTS[part], gathered)]
        if part == "mix":
            full[0] = jnp.concatenate([full[0][:, W_IN_SPLIT:], full[0][:, :W_IN_SPLIT]], axis=1)
        return full

    n_a = len(PARTS["a"])
    first = _all_gather("gather_weights", [shard_of(p, 0) for p in PARTS["a"]] + [w[n] for n in SMALL_SHARDED])
    small = {n: w[n] for n in REPLICATED}
    for n, g in zip(SMALL_SHARDED, first[n_a:]):
        small[n] = _unshard(g, SHARDED[n][0])
    riders = {}
    for host, (l, part, which) in AG_HOSTS.items():
        rider = _gather_over_ici([shard_of(PARTS[part][j], l) for j in which])
        _CARRIED[host] = rider
        riders.setdefault((l, part), []).append((which, rider))

    def weights_of(l, part):
        if (l, part) == (0, "a"):
            return assemble(part, first[:n_a])
        landed = [None] * len(PARTS[part])
        for which, rider in riders[l, part]:
            for j, buf in zip(which, rider.outs):
                landed[j] = buf
        return assemble(part, _gather_forward("gather_forward_%d%s" % (l, part), landed))

    sums = {}

    def blocks_of(part, grads):
        if part == "mix":
            grads = [jnp.concatenate([grads[0][:, GM_WIDTH:], grads[0][:, :GM_WIDTH]], axis=1)] + grads[1:]
        return [_to_blocks(g, piece[2]).astype(bf16) for piece, g in zip(PARTS[part], grads)]

    last_grads = []

    def grads_done(l, part, grads):
        if (l, part) in RS_HOSTS:
            sums[l, part] = _reduce_scatter("%d%s" % (l, part), blocks_of(part, grads), hosts=RS_HOSTS[l, part])
        else:
            last_grads.extend(blocks_of(part, grads))

    loss_part, dx, g_small = _local_step(x[0], loss_target[0], weights_of, small, grads_done)
    loss = lax.psum(loss_part, ("x", "y", "c"))

    parts = last_grads + [_to_blocks(g_small[n], SHARDED[n][0]) for n in SMALL_SHARDED]
    rep_rows = _pack_rows(g_small, REPLICATED)
    rep_slice = rep_rows.shape[0] // N_DEV
    parts.append(rep_rows.reshape(4, 2, rep_slice, LANES).transpose(1, 0, 2, 3))
    last = _reduce_scatter("last", parts)
    sums[0, "a"] = last[:n_a]

    grads, delta, new_m, new_v = {}, {}, {}, {}

    def update(n, pieces):
        shp = w[n].shape
        view = (len(pieces), -1, shp[-1])
        res = _adamw_reduced("adamw_" + n, w[n].reshape(view), [(t, others()) for t, others in pieces],
                             m[n].reshape(view), v[n].reshape(view))
        grads[n], delta[n], new_m[n], new_v[n] = (r.reshape(shp) for r in res)

    for n in BIG:
        update(n, [sums[l, part][j] for l in range(DEPTH) for part in ("a", "b", "mix")
                   for j, piece in enumerate(PARTS[part]) if piece[0] == n])
    for j, n in enumerate(SMALL_SHARDED):
        update(n, [last[n_a + j]])
    rep_mine = _own_total("rs_total_small", last[-1][0], last[-1][1]())
    rep_grads = _all_gather("gather_small_grads", [rep_mine])[0].reshape(-1, LANES)
    res = _adamw("adamw_small", _pack_rows(w, REPLICATED), rep_grads, _pack_rows(m, REPLICATED), _pack_rows(v, REPLICATED))
    for dst, src in zip((grads, delta, new_m, new_v), (rep_grads,) + tuple(res)):
        dst.update(_unpack_rows(src, REPLICATED, w))

    return (loss, dx.reshape(x.shape), *[grads[n] for n in WEIGHT_NAMES], *[delta[n] for n in WEIGHT_NAMES],
            *[new_m[n] for n in WEIGHT_NAMES], *[new_v[n] for n in WEIGHT_NAMES])
```

```python
import functools
import math

import jax
import jax.numpy as jnp
from jax import lax
from jax.experimental import pallas as pl
from jax.experimental.pallas import tpu as pltpu

f32 = jnp.float32
bf16 = jnp.bfloat16

D_MODEL = 1024
DEPTH = 2
BRANCH = 512
N_BRANCH = 3
S5_GROUP = 16
S5_GROUPS = 32
S5_STATE = 64
S5_LANES = S5_GROUPS * S5_STATE
S5_EIG_MAX = -1e-4
HG_HEADS = 4
HG_DK = 128
HG_CHUNK = 32
RG_BLOCKS = 8
RG_BLOCK = 64
RG_C = 8.0
D_FF = 2816
EPS = 1e-6
IN_TOTAL = 6656
GM_WIDTH = N_BRANCH * D_MODEL
N_DEV = 8

ADAM_LR = 0.001
ADAM_B1 = 0.9
ADAM_B2 = 0.999
ADAM_EPS = 1e-08
ADAM_WD = 0.01
ADAM_STEP = 10

VMEM_LIMIT_V7X = 56 * 1024 * 1024
ROW_TILE = 256
FF_TILE = 1408
TOKEN_K = 2048
MXU_COLS = 256


def _cparams(sem):
    return pltpu.CompilerParams(dimension_semantics=sem, vmem_limit_bytes=VMEM_LIMIT_V7X)


MESH_IDS = pl.DeviceIdType.MESH
ANY_SPEC = pl.BlockSpec(memory_space=pl.ANY)


def _place():
    return lax.axis_index("x"), lax.axis_index("y"), lax.axis_index("c")


class _Carry:
    def __init__(self, ins, out_shapes, n_sems, copies):
        self.ins, self.out_shapes, self.n_sems, self.copies = list(ins), list(out_shapes), n_sems, copies
        self.outs = None

    def sems(self):
        return [pltpu.SemaphoreType.DMA((self.n_sems,)), pltpu.SemaphoreType.DMA((self.n_sems,))]

    def start(self, when, *riders):
        @pl.when(when)
        def _():
            for cp in self.copies(*riders):
                cp.start()

    def finish(self, when, *riders):
        @pl.when(when)
        def _():
            for cp in self.copies(*riders):
                cp.wait()


_CARRIED = {}


def _call_with_rider(name, body, grid, in_specs, out_specs, out_shape, scratch, semantics, args):
    carry = _CARRIED.pop(name, None)
    if carry is None:
        return pl.pallas_call(body, name=name, grid=grid, in_specs=in_specs, out_specs=out_specs,
                              out_shape=out_shape, scratch_shapes=scratch, compiler_params=_cparams(semantics))(*args)
    n_in, n_out, nci, nco = len(in_specs), len(out_specs), len(carry.ins), len(carry.out_shapes)

    def kern(*refs):
        ids = [pl.program_id(d) for d in range(len(grid))]
        own = refs[:n_in] + refs[n_in + nci:n_in + nci + n_out] + refs[n_in + nci + n_out + nco:-2]
        riders = (refs[n_in:n_in + nci], refs[n_in + nci + n_out:n_in + nci + n_out + nco]) + tuple(refs[-2:])
        carry.start(functools.reduce(jnp.logical_and, [p == 0 for p in ids]), *riders)
        body(*own)
        carry.finish(functools.reduce(jnp.logical_and, [p == g - 1 for p, g in zip(ids, grid)]), *riders)

    res = pl.pallas_call(
        kern, name=name, grid=grid, in_specs=list(in_specs) + [ANY_SPEC] * nci,
        out_specs=list(out_specs) + [ANY_SPEC] * nco, out_shape=list(out_shape) + carry.out_shapes,
        scratch_shapes=list(scratch) + carry.sems(), compiler_params=_cparams(("arbitrary",) * len(grid)),
    )(*args, *carry.ins)
    carry.outs = res[n_out:]
    return res[:n_out]


def _sigmoid(x):
    return 0.5 * jnp.tanh(0.5 * x) + 0.5


def _sigmoid_small(x):
    return 1.0 / (1.0 + jnp.exp(-x))


_GELU_C = math.sqrt(2.0 / math.pi)


def _gelu(x):
    t = jnp.tanh(_GELU_C * (x + 0.044715 * x * x * x))
    return 0.5 * x * (1.0 + t)


def _gelu_grad(x):
    t = jnp.tanh(_GELU_C * (x + 0.044715 * x * x * x))
    return 0.5 * (1.0 + t) + 0.5 * x * (1.0 - t * t) * _GELU_C * (1.0 + 3.0 * 0.044715 * x * x)


def _expm1(x):
    p = x * (1.0 + x * (0.5 + x * (1.0 / 6 + x * (1.0 / 24 + x * (1.0 / 120 + x * (1.0 / 720))))))
    return jnp.where(jnp.abs(x) < 0.3, p, jnp.exp(x) - 1.0)


def _dot(a, b):
    return jnp.dot(a, b, preferred_element_type=f32)


def _dot_nt(a, b):
    return lax.dot_general(a, b, (((1,), (1,)), ((), ())), preferred_element_type=f32)


def _dot_tn(a, b):
    return lax.dot_general(a, b, (((0,), (0,)), ((), ())), preferred_element_type=f32)


def _bdot(a, b):
    return lax.dot_general(a, b, (((2,), (1,)), ((0,), (0,))), preferred_element_type=f32)


def _bdot_nt(a, b):
    return lax.dot_general(a, b, (((2,), (2,)), ((0,), (0,))), preferred_element_type=f32)


def _rows(shape):
    return lax.broadcasted_iota(jnp.int32, shape, 0)


def _scan_fwd(a, b, n):
    row = _rows(a.shape)
    s = 1
    while s < n:
        valid = row >= s
        sh_a = pltpu.roll(a, s, 0)
        sh_b = pltpu.roll(b, s, 0)
        b = b + a * jnp.where(valid, sh_b, 0.0)
        a = a * jnp.where(valid, sh_a, 1.0)
        s *= 2
    return a, b


def _scan_bwd(a, b, n):
    row = _rows(a.shape)
    s = 1
    while s < n:
        valid = row < n - s
        sh_a = pltpu.roll(a, n - s, 0)
        sh_b = pltpu.roll(b, n - s, 0)
        b = b + a * jnp.where(valid, sh_b, 0.0)
        a = a * jnp.where(valid, sh_a, 1.0)
        s *= 2
    return a, b


def _seg_cumsum(x, n, seg):
    pos = _rows(x.shape) % seg
    s = 1
    while s < seg:
        x = x + jnp.where(pos >= s, pltpu.roll(x, s, 0), 0.0)
        s *= 2
    return x


def _seg_rev_cumsum(x, n, seg):
    pos = _rows(x.shape) % seg
    s = 1
    while s < seg:
        x = x + jnp.where(pos < seg - s, pltpu.roll(x, n - s, 0), 0.0)
        s *= 2
    return x


def _head_mean(x):
    parts = []
    for h in range(HG_HEADS):
        m = jnp.mean(x[:, h * HG_DK:(h + 1) * HG_DK], axis=1, keepdims=True)
        parts.append(jnp.broadcast_to(m, (x.shape[0], HG_DK)))
    return jnp.concatenate(parts, axis=1)


def _mm(name, a_list, b_list, terms, n_acc, mode, m, n, k, tm, tn, tk, out_dtypes, epilogue, extras=(), vecs=(),
        n_part=0, chunk=0):
    tm, tn, tk = min(tm, m), min(tn, n), min(tk, k)
    assert m % tm == 0 and n % tn == 0 and k % tk == 0, (name, m, n, k, tm, tn, tk)
    gk = k // tk
    if mode == "tn":
        a_spec = pl.BlockSpec((tk, tm), lambda i, j, kk: (kk, i))
    else:
        a_spec = pl.BlockSpec((tm, tk), lambda i, j, kk: (i, kk))
    if mode == "nt":
        b_spec = pl.BlockSpec((tn, tk), lambda i, j, kk: (j, kk))
    else:
        b_spec = pl.BlockSpec((tk, tn), lambda i, j, kk: (kk, j))
    o_spec = pl.BlockSpec((tm, tn), lambda i, j, kk: (i, j))
    v_spec = pl.BlockSpec((1, tn), lambda i, j, kk: (0, j))
    p_spec = pl.BlockSpec((None, 1, tn), lambda i, j, kk: (i, 0, j))
    dot = {"nn": _dot, "nt": _dot_nt, "tn": _dot_tn}[mode]
    na, nb, ne, nv, no = len(a_list), len(b_list), len(extras), len(vecs), len(out_dtypes)
    carry = _CARRIED.pop(name, None)
    nci, nco = (len(carry.ins), len(carry.out_shapes)) if carry else (0, 0)
    n_in = na + nb + ne + nv + nci
    grid = (m // tm, n // tn, gk)

    def kern(*refs):
        if carry:
            ids = [pl.program_id(d) for d in range(3)]
            riders = (refs[n_in - nci:n_in], refs[n_in + no + n_part:n_in + no + n_part + nco]) + tuple(refs[-2:])
            carry.start(functools.reduce(jnp.logical_and, [p == 0 for p in ids]), *riders)
        compute(*refs)
        if carry:
            carry.finish(functools.reduce(jnp.logical_and, [p == g - 1 for p, g in zip(ids, grid)]), *riders)

    def compute(*refs):
        a_refs = refs[:na]
        b_refs = refs[na:na + nb]
        e_refs = refs[na + nb:na + nb + ne]
        v_refs = refs[na + nb + ne:na + nb + ne + nv]
        o_refs = refs[n_in:n_in + no + n_part]

        def finish(accs):
            outs = epilogue(accs, [e[...] for e in e_refs], [r[...] for r in v_refs])
            for o, val in zip(o_refs, outs):
                o[...] = val.astype(o.dtype)

        def partial_sums():
            sums = [None] * n_acc
            for ai, bi, ci in terms:
                d = dot(a_refs[ai][...].astype(bf16), b_refs[bi][...].astype(bf16))
                sums[ci] = d if sums[ci] is None else sums[ci] + d
            return sums

        if gk == 1 and chunk:
            assert mode in ("nn", "nt") and tn % chunk == 0
            for c0 in range(0, tn, chunk):
                cols = slice(c0, c0 + chunk)
                sums = [None] * n_acc
                for ai, bi, ci in terms:
                    b_part = b_refs[bi][:, cols] if mode == "nn" else b_refs[bi][cols, :]
                    d = dot(a_refs[ai][...].astype(bf16), b_part.astype(bf16))
                    sums[ci] = d if sums[ci] is None else sums[ci] + d
                outs = epilogue(sums, [e[:, cols] for e in e_refs], [r[:, cols] for r in v_refs])
                for o, val in zip(o_refs, outs):
                    o[:, cols] = val.astype(o.dtype)
            return
        if gk == 1:
            finish(partial_sums())
            return
        acc = refs[n_in + no + n_part + nco]
        kk = pl.program_id(2)

        @pl.when(kk == 0)
        def _():
            acc[...] = jnp.zeros_like(acc)

        for ci, d in enumerate(partial_sums()):
            acc[ci] += d

        @pl.when(kk == gk - 1)
        def _():
            finish([acc[c] for c in range(n_acc)])

    res = pl.pallas_call(
        kern, name=name,
        grid=grid,
        in_specs=[a_spec] * na + [b_spec] * nb + [o_spec] * ne + [v_spec] * nv + [ANY_SPEC] * nci,
        out_specs=[o_spec] * no + [p_spec] * n_part + [ANY_SPEC] * nco,
        out_shape=([jax.ShapeDtypeStruct((m, n), dt) for dt in out_dtypes]
                   + [jax.ShapeDtypeStruct((m // tm, 1, n), f32)] * n_part + (carry.out_shapes if carry else [])),
        scratch_shapes=([pltpu.VMEM((n_acc, tm, tn), f32)] if gk > 1 else []) + (carry.sems() if carry else []),
        compiler_params=_cparams(("arbitrary",) * 3 if carry else ("parallel", "parallel", "arbitrary")),
    )(*a_list, *b_list, *extras, *vecs, *(carry.ins if carry else []))
    if carry:
        carry.outs = res[no + n_part:]
        res = res[:no + n_part]
    return res


def _mm1(name, a, b, mode, m, n, k, tm, tn, tk, out_dtype=f32, scale=None):
    def epi(accs, extras, vecs):
        return [accs[0] if scale is None else accs[0] * scale]
    return _mm(name, [a], [b], [(0, 0, 0)], 1, mode, m, n, k, tm, tn, tk, [out_dtype], epi)[0]


def _rt(name, body, rows, tm, row_ins, consts, row_outs, acc_outs=(), scratch=(), reverse=False):
    tm = min(tm, rows)
    assert rows % tm == 0
    nt = rows // tm

    def tile(i):
        return nt - 1 - i if reverse else i

    in_specs, args = [], []
    for spec in row_ins:
        arr = spec[0]
        if isinstance(spec[1], int):
            in_specs.append(pl.BlockSpec((tm, spec[1]), lambda i, cb=spec[2]: (tile(i), cb)))
        else:
            in_specs.append(pl.BlockSpec(spec[1], lambda i, fn=spec[2]: fn(tile(i))))
        args.append(arr)
    for c in consts:
        in_specs.append(pl.BlockSpec(c.shape, lambda i, nd=c.ndim: (0,) * nd))
        args.append(c)
    out_specs, out_shape = [], []
    for spec in row_outs:
        if isinstance(spec[0], int):
            out_specs.append(pl.BlockSpec((tm, spec[0]), lambda i: (tile(i), 0)))
            out_shape.append(jax.ShapeDtypeStruct((rows, spec[0]), spec[1]))
        else:
            out_specs.append(pl.BlockSpec(spec[1], lambda i, fn=spec[2]: fn(tile(i))))
            out_shape.append(jax.ShapeDtypeStruct(spec[0], spec[3]))
    for shp in acc_outs:
        out_specs.append(pl.BlockSpec(shp, lambda i, nd=len(shp): (0,) * nd))
        out_shape.append(jax.ShapeDtypeStruct(shp, f32))
    n_in = len(args)
    n_row_out = len(row_outs)
    n_acc = len(acc_outs)
    n_out = n_row_out + n_acc
    carry = _CARRIED.pop(name, None)
    nci, nco = (len(carry.ins), len(carry.out_shapes)) if carry else (0, 0)

    def kern(*refs):
        i = pl.program_id(0)
        if carry:
            own = refs[:n_in] + refs[n_in + nci:n_in + nci + n_out] + refs[n_in + nci + n_out + nco:-2]
            riders = (refs[n_in:n_in + nci], refs[n_in + nci + n_out:n_in + nci + n_out + nco]) + tuple(refs[-2:])
            carry.start(i == 0, *riders)
        else:
            own = refs
        acc_refs = own[n_in + n_row_out:n_in + n_out]

        @pl.when(i == 0)
        def _():
            for r in acc_refs:
                r[...] = jnp.zeros_like(r)

        body(i, *own)
        if carry:
            carry.finish(i == nt - 1, *riders)

    res = pl.pallas_call(
        kern, name=name, grid=(nt,), in_specs=in_specs + [ANY_SPEC] * nci, out_specs=out_specs + [ANY_SPEC] * nco,
        out_shape=out_shape + (carry.out_shapes if carry else []),
        scratch_shapes=list(scratch) + (carry.sems() if carry else []), compiler_params=_cparams(("arbitrary",)),
    )(*args, *(carry.ins if carry else []))
    if carry:
        carry.outs = res[n_out:]
        res = res[:n_out]
    return res


def _rms_rows(xv, wv):
    r = lax.rsqrt(jnp.mean(xv * xv, axis=1, keepdims=True) + EPS)
    return (xv * r * wv).astype(bf16)


def _rms_bwd_rows(xv, dhv, wv, dres):
    r = lax.rsqrt(jnp.mean(xv * xv, axis=1, keepdims=True) + EPS)
    xn = xv * r
    dxn = dhv * wv
    dx = dres + r * (dxn - xn * jnp.mean(dxn * xn, axis=1, keepdims=True))
    return [dx, dx.astype(bf16), jnp.sum(dhv * xn, axis=0, keepdims=True)]


def _rms_fwd(name, x, w, rows):
    def body(i, x_ref, w_ref, h_ref):
        h_ref[...] = _rms_rows(x_ref[...], w_ref[...])
    return _rt(name, body, rows, ROW_TILE, [(x, D_MODEL, 0)], [w], [(D_MODEL, bf16)])[0]


def _residual_then_norm(scale):
    def epi(accs, extras, vecs):
        x_out = extras[0] + scale * accs[0]
        return [x_out] + [_rms_rows(x_out, v) for v in vecs]
    return epi


def _norm_bwd_epilogue(accs, extras, vecs):
    return _rms_bwd_rows(extras[0], accs[0], vecs[0], extras[1])


def _loss_head(x, w, target, rows):
    def body(i, x_ref, t_ref, w_ref, dx_ref, dxb_ref, loss_ref, dw_ref):
        xv = x_ref[...]
        r = lax.rsqrt(jnp.mean(xv * xv, axis=1, keepdims=True) + EPS)
        xn = xv * r
        wv = w_ref[...]
        err = xn * wv - t_ref[...]
        part = 0.5 * jnp.sum(jnp.mean(err * err, axis=1, keepdims=True), axis=0, keepdims=True)
        loss_ref[...] += jnp.broadcast_to(part, (1, 128))
        dy = err * (1.0 / D_MODEL)
        dxn = dy * wv
        dx = r * (dxn - xn * jnp.mean(dxn * xn, axis=1, keepdims=True))
        dx_ref[...] = dx
        dxb_ref[...] = dx.astype(bf16)
        dw_ref[...] += jnp.sum(dy * xn, axis=0, keepdims=True)
    return _rt("loss_head", body, rows, ROW_TILE, [(x, D_MODEL, 0), (target, D_MODEL, 0)], [w],
               [(D_MODEL, f32), (D_MODEL, bf16)], acc_outs=[(1, 128), (1, D_MODEL)])


def _ffn_fwd(tag, x, hb, wg, wu, wd, next_nw, rows):
    def epi_up(accs, extras, vecs):
        a, b = accs
        return [a, b, a * _sigmoid(a) * b]
    a, b, s = _mm("ffn_up_" + tag, [hb], [wg, wu], [(0, 0, 0), (0, 1, 1)], 2, "nn", rows, D_FF, D_MODEL,
                  512, D_FF, D_MODEL, [bf16, bf16, bf16], epi_up, chunk=MXU_COLS)
    outs = _mm("ffn_down_" + tag, [s], [wd], [(0, 0, 0)], 1, "nn", rows, D_MODEL, D_FF,
               512, D_MODEL, D_FF, [f32] + [bf16] * len(next_nw), _residual_then_norm(0.5), extras=[x],
               vecs=next_nw)
    return outs[0], (outs[1] if next_nw else None), (x, hb, a, b, s)


def _ffn_bwd(tag, saved, nw, wg, wu, wd, dx, dxb, rows):
    x, hb, a, b, s = saved

    def epi_mid(accs, extras, vecs):
        ds = 0.5 * accs[0]
        av = extras[0].astype(f32)
        bv = extras[1].astype(f32)
        sg = _sigmoid(av)
        return [ds * bv * sg * (1.0 + av * (1.0 - sg)), ds * av * sg]
    da, db = _mm("ffn_bwd_mid_" + tag, [dxb], [wd], [(0, 0, 0)], 1, "nt", rows, D_FF, D_MODEL,
                 512, D_FF, D_MODEL, [bf16, bf16], epi_mid, extras=[a, b], chunk=MXU_COLS)
    d_wd = _mm1("ffn_dwd_" + tag, s, dxb, "tn", D_FF, D_MODEL, rows, FF_TILE, D_MODEL, TOKEN_K, out_dtype=bf16,
                scale=0.5)
    d_wg = _mm1("ffn_dwg_" + tag, hb, da, "tn", D_MODEL, D_FF, rows, D_MODEL, FF_TILE, TOKEN_K, out_dtype=bf16)
    d_wu = _mm1("ffn_dwu_" + tag, hb, db, "tn", D_MODEL, D_FF, rows, D_MODEL, FF_TILE, TOKEN_K, out_dtype=bf16)
    dx_in, dxb_in, d_nw = _mm("ffn_dh_" + tag, [da, db], [wg, wu], [(0, 0, 0), (1, 1, 0)], 1, "nt", rows, D_MODEL,
                              D_FF, 512, D_MODEL, D_FF, [f32, bf16], _norm_bwd_epilogue, extras=[x, dx], vecs=[nw],
                              n_part=1)
    return dx_in, dxb_in, jnp.sum(d_nw, axis=0), d_wg, d_wu, d_wd


S5_CB = 512
SUBLANES = 8
U_COL = GM_WIDTH // BRANCH


def _s5_scan_fwd(tag, proj, b_re, b_im, a_re, a_im, rows):
    tm = min(ROW_TILE, rows)
    nt = rows // tm
    nc = S5_LANES // S5_CB

    def kern(u_ref, bre_ref, bim_ref, ar_ref, ai_ref, xr_ref, xi_ref, pr_s, pi_s, cr_s, ci_s, mr_s, mi_s):
        t = pl.program_id(1)

        @pl.when(t == 0)
        def _():
            row8 = _rows((SUBLANES, S5_CB))
            pr = jnp.broadcast_to(ar_ref[...], (SUBLANES, S5_CB))
            pi = jnp.broadcast_to(ai_ref[...], (SUBLANES, S5_CB))
            s = 1
            while s < SUBLANES:
                sr = pltpu.roll(pr, s, 0)
                si = pltpu.roll(pi, s, 0)
                valid = row8 >= s
                pr, pi = jnp.where(valid, pr * sr - pi * si, pr), jnp.where(valid, pr * si + pi * sr, pi)
                s *= 2
            pr_s[...] = pr
            pi_s[...] = pi
            for k in range(3):
                s = 1 << k
                mr_s[k] = jnp.where(row8 >= s, pr[s - 1:s, :], 0.0)
                mi_s[k] = jnp.where(row8 >= s, pi[s - 1:s, :], 0.0)
            cr_s[...] = jnp.zeros_like(cr_s)
            ci_s[...] = jnp.zeros_like(ci_s)

        ng = tm // SUBLANES
        ub = u_ref[...].astype(bf16)
        br = _dot(ub, bre_ref[...]).reshape(ng, SUBLANES, S5_CB)
        bi = _dot(ub, bim_ref[...]).reshape(ng, SUBLANES, S5_CB)
        for k in range(3):
            mr = mr_s[k]
            mi = mi_s[k]
            sr = pltpu.roll(br, 1 << k, 1)
            si = pltpu.roll(bi, 1 << k, 1)
            br, bi = br + (mr * sr - mi * si), bi + (mr * si + mi * sr)
        cr = cr_s[...]
        ci = ci_s[...]
        pr = pr_s[...]
        pi = pi_s[...]
        for g in range(ng):
            sl = slice(g * SUBLANES, (g + 1) * SUBLANES)
            xr = br[g] + pr * cr - pi * ci
            xi = bi[g] + pr * ci + pi * cr
            xr_ref[sl, :] = xr
            xi_ref[sl, :] = xi
            cr = xr[SUBLANES - 1:SUBLANES, :]
            ci = xi[SUBLANES - 1:SUBLANES, :]
        cr_s[...] = cr
        ci_s[...] = ci

    return _call_with_rider(
        "s5_scan_fwd_" + tag, kern, (nc, nt),
        [pl.BlockSpec((tm, BRANCH), lambda c, t: (t, U_COL)),
         pl.BlockSpec((BRANCH, S5_CB), lambda c, t: (0, c)),
         pl.BlockSpec((BRANCH, S5_CB), lambda c, t: (0, c)),
         pl.BlockSpec((1, S5_CB), lambda c, t: (0, c)),
         pl.BlockSpec((1, S5_CB), lambda c, t: (0, c))],
        [pl.BlockSpec((tm, S5_CB), lambda c, t: (t, c))] * 2,
        [jax.ShapeDtypeStruct((rows, S5_LANES), f32)] * 2,
        [pltpu.VMEM((SUBLANES, S5_CB), f32), pltpu.VMEM((SUBLANES, S5_CB), f32),
         pltpu.VMEM((1, S5_CB), f32), pltpu.VMEM((1, S5_CB), f32),
         pltpu.VMEM((3, SUBLANES, S5_CB), f32), pltpu.VMEM((3, SUBLANES, S5_CB), f32)],
        ("parallel", "arbitrary"), (proj, b_re, b_im, a_re, a_im))


def _s5_scan_bwd(tag, dxr, dxi, xr, xi, a_re, a_im, rows):
    tm = min(ROW_TILE, rows)
    nt = rows // tm
    nc = S5_LANES // S5_CB

    def kern(dxr_ref, dxi_ref, xr_ref, xi_ref, ar_ref, ai_ref, gr_ref, gi_ref, dar_ref, dai_ref,
             qr_s, qi_s, cr_s, ci_s, gr_s, gi_s, mr_s, mi_s):
        t = pl.program_id(1)
        row = _rows((tm, S5_CB))
        ng = tm // SUBLANES

        @pl.when(t == 0)
        def _():
            row8 = _rows((SUBLANES, S5_CB))
            qr = jnp.broadcast_to(ar_ref[...], (SUBLANES, S5_CB))
            qi = jnp.broadcast_to(-ai_ref[...], (SUBLANES, S5_CB))
            s = 1
            while s < SUBLANES:
                sr = pltpu.roll(qr, SUBLANES - s, 0)
                si = pltpu.roll(qi, SUBLANES - s, 0)
                valid = row8 < SUBLANES - s
                qr, qi = jnp.where(valid, qr * sr - qi * si, qr), jnp.where(valid, qr * si + qi * sr, qi)
                s *= 2
            qr_s[...] = qr
            qi_s[...] = qi
            for k in range(3):
                s = 1 << k
                mr_s[k] = jnp.where(row8 < SUBLANES - s, qr[SUBLANES - s:SUBLANES - s + 1, :], 0.0)
                mi_s[k] = jnp.where(row8 < SUBLANES - s, qi[SUBLANES - s:SUBLANES - s + 1, :], 0.0)
            cr_s[...] = jnp.zeros_like(cr_s)
            ci_s[...] = jnp.zeros_like(ci_s)
            dar_ref[...] = jnp.zeros_like(dar_ref)
            dai_ref[...] = jnp.zeros_like(dai_ref)

        br = dxr_ref[...].reshape(ng, SUBLANES, S5_CB)
        bi = dxi_ref[...].reshape(ng, SUBLANES, S5_CB)
        for k in range(3):
            mr = mr_s[k]
            mi = mi_s[k]
            sr = pltpu.roll(br, SUBLANES - (1 << k), 1)
            si = pltpu.roll(bi, SUBLANES - (1 << k), 1)
            br, bi = br + (mr * sr - mi * si), bi + (mr * si + mi * sr)
        cin_r = cr_s[...]
        cin_i = ci_s[...]
        cr, ci = cin_r, cin_i
        qr = qr_s[...]
        qi = qi_s[...]
        for g in reversed(range(ng)):
            sl = slice(g * SUBLANES, (g + 1) * SUBLANES)
            gr = br[g] + qr * cr - qi * ci
            gi = bi[g] + qr * ci + qi * cr
            gr_s[sl, :] = gr
            gi_s[sl, :] = gi
            cr = gr[0:1, :]
            ci = gi[0:1, :]
        cr_s[...] = cr
        ci_s[...] = ci
        gr = gr_s[...]
        gi = gi_s[...]
        gr_ref[...] = gr.astype(bf16)
        gi_ref[...] = gi.astype(bf16)
        last = row == tm - 1
        gnr = jnp.where(last, cin_r, pltpu.roll(gr, tm - 1, 0))
        gni = jnp.where(last, cin_i, pltpu.roll(gi, tm - 1, 0))
        xr_v = xr_ref[...]
        xi_v = xi_ref[...]
        dar_ref[...] += jnp.sum(gnr * xr_v + gni * xi_v, axis=0, keepdims=True)
        dai_ref[...] += jnp.sum(gni * xr_v - gnr * xi_v, axis=0, keepdims=True)

    rev = lambda c, t: (nt - 1 - t, c)
    return _call_with_rider(
        "s5_scan_bwd_" + tag, kern, (nc, nt),
        [pl.BlockSpec((tm, S5_CB), rev)] * 4 + [pl.BlockSpec((1, S5_CB), lambda c, t: (0, c))] * 2,
        [pl.BlockSpec((tm, S5_CB), rev)] * 2 + [pl.BlockSpec((1, S5_CB), lambda c, t: (0, c))] * 2,
        [jax.ShapeDtypeStruct((rows, S5_LANES), bf16)] * 2 + [jax.ShapeDtypeStruct((1, S5_LANES), f32)] * 2,
        [pltpu.VMEM((SUBLANES, S5_CB), f32), pltpu.VMEM((SUBLANES, S5_CB), f32),
         pltpu.VMEM((1, S5_CB), f32), pltpu.VMEM((1, S5_CB), f32),
         pltpu.VMEM((tm, S5_CB), f32), pltpu.VMEM((tm, S5_CB), f32),
         pltpu.VMEM((3, SUBLANES, S5_CB), f32), pltpu.VMEM((3, SUBLANES, S5_CB), f32)],
        ("parallel", "arbitrary"), (dxr, dxi, xr, xi, a_re, a_im))


def _s5_fwd(tag, proj, cst, rows):
    xr, xi = _s5_scan_fwd(tag, proj, cst["b_re"].astype(bf16), cst["b_im"].astype(bf16), cst["a_re"], cst["a_im"], rows)

    def body(i, xr_ref, xi_ref, u_ref, cre_ref, cim_ref, d_ref, gw_ref, gb_ref, y_ref, out_ref):
        y = (_dot(xr_ref[...].astype(bf16), cre_ref[...]) + _dot(xi_ref[...].astype(bf16), cim_ref[...])
             + d_ref[...] * u_ref[...])
        y_ref[...] = y
        z = _gelu(y)
        zg = _dot(z.astype(bf16), gw_ref[...]) + gb_ref[...]
        out_ref[...] = (z * _sigmoid(zg)).astype(bf16)

    y, out = _rt("s5_out_" + tag, body, rows, ROW_TILE,
                 [(xr, S5_LANES, 0), (xi, S5_LANES, 0), (proj, BRANCH, U_COL)],
                 [cst["c_re"].astype(bf16), cst["c_im"].astype(bf16), cst["s5_d"], cst["glu_w"], cst["glu_b"]],
                 [(BRANCH, f32), (BRANCH, bf16)])
    return out, (xr, xi, y)


def _s5_bwd(tag, saved, proj, cst, d_out, rows):
    xr, xi, y = saved
    c_re = cst["c_re"].astype(bf16)
    c_im = cst["c_im"].astype(bf16)

    def body(i, do_ref, y_ref, u_ref, xr_ref, xi_ref, cre_ref, cim_ref, gw_ref, gb_ref,
             dxr_ref, dxi_ref, dy_ref, dgw_ref, dgb_ref, dd_ref, dcre_ref, dcim_ref):
        yv = y_ref[...]
        z = _gelu(yv)
        zb = z.astype(bf16)
        gt = _sigmoid(_dot(zb, gw_ref[...]) + gb_ref[...])
        dov = do_ref[...]
        dzg = dov * z * gt * (1.0 - gt)
        dzgb = dzg.astype(bf16)
        dz = dov * gt + _dot_nt(dzgb, gw_ref[...])
        dgw_ref[...] += _dot_tn(zb, dzgb)
        dgb_ref[...] += jnp.sum(dzg, axis=0, keepdims=True)
        dy = dz * _gelu_grad(yv)
        dy_ref[...] = dy
        dd_ref[...] += jnp.sum(dy * u_ref[...], axis=0, keepdims=True)
        dyb = dy.astype(bf16)
        dxr_ref[...] = _dot_nt(dyb, cre_ref[...])
        dxi_ref[...] = _dot_nt(dyb, cim_ref[...])
        dcre_ref[...] += _dot_tn(xr_ref[...].astype(bf16), dyb)
        dcim_ref[...] += _dot_tn(xi_ref[...].astype(bf16), dyb)

    dxr, dxi, dy, d_gw, d_gb, d_d, d_cre, d_cim = _rt(
        "s5_out_bwd_" + tag, body, rows, ROW_TILE,
        [(d_out, BRANCH, 0), (y, BRANCH, 0), (proj, BRANCH, U_COL), (xr, S5_LANES, 0), (xi, S5_LANES, 0)],
        [c_re, c_im, cst["glu_w"], cst["glu_b"]],
        [(S5_LANES, f32), (S5_LANES, f32), (BRANCH, f32)],
        acc_outs=[(BRANCH, BRANCH), (1, BRANCH), (1, BRANCH), (S5_LANES, BRANCH), (S5_LANES, BRANCH)])

    gr, gi, d_ar, d_ai = _s5_scan_bwd(tag, dxr, dxi, xr, xi, cst["a_re"], cst["a_im"], rows)
    b_re = cst["b_re"].astype(bf16)
    b_im = cst["b_im"].astype(bf16)

    def body_in(i, gr_ref, gi_ref, dy_ref, u_ref, bre_ref, bim_ref, d_ref, du_ref, dbre_ref, dbim_ref):
        grv = gr_ref[...]
        giv = gi_ref[...]
        du = _dot_nt(grv, bre_ref[...]) + _dot_nt(giv, bim_ref[...]) + dy_ref[...] * d_ref[...]
        du_ref[...] = du.astype(bf16)
        ub = u_ref[...].astype(bf16)
        dbre_ref[...] += _dot_tn(ub, grv)
        dbim_ref[...] += _dot_tn(ub, giv)

    du, d_bre, d_bim = _rt("s5_in_bwd_" + tag, body_in, rows, ROW_TILE,
                           [(gr, S5_LANES, 0), (gi, S5_LANES, 0), (dy, BRANCH, 0), (proj, BRANCH, U_COL)],
                           [b_re, b_im, cst["s5_d"]], [(BRANCH, bf16)],
                           acc_outs=[(BRANCH, S5_LANES), (BRANCH, S5_LANES)])
    dcst = {"b_re": d_bre, "b_im": d_bim, "a_re": d_ar, "a_im": d_ai, "c_re": d_cre, "c_im": d_cim,
            "s5_d": d_d, "glu_b": d_gb}
    return du, dcst, d_gw


def _hg_prep(q, z, lb):
    qs = _sigmoid(q)
    qh = q * qs
    sg = _sigmoid_small(z)
    fg = lb + (1.0 - lb) * sg
    kk = (1.0 - lb) * (1.0 - sg)
    return qs, qh, sg, fg, kk


def _hg_fwd(tag, proj, cst, rows):
    tm = min(ROW_TILE, rows)
    c_sz = HG_CHUNK
    nch = tm // c_sz
    n_chunks = rows // c_sz

    def body(i, q_ref, z_ref, v_ref, g_ref, lb_ref, nw_ref, out_ref, o_ref, ss_ref, sn_ref, st_s):
        @pl.when(i == 0)
        def _():
            st_s[...] = jnp.zeros_like(st_s)

        lb = lb_ref[...]
        _, qh, sg, fg, kk = _hg_prep(q_ref[...], z_ref[...], lb)
        b = _seg_cumsum(jnp.log(fg), tm, c_sz)
        qhat = (qh * jnp.exp(b)).astype(bf16)
        khat = (kk * jnp.exp(-b)).astype(bf16)
        vb = v_ref[...].astype(bf16)
        b3 = b.reshape(nch, c_sz, BRANCH)
        bl3 = b3[:, c_sz - 1:c_sz, :]
        kdec = (kk.reshape(nch, c_sz, BRANCH) * jnp.exp(bl3 - b3)).astype(bf16)
        ebl = jnp.exp(bl3)
        tril = (lax.broadcasted_iota(jnp.int32, (nch, c_sz, c_sz), 1)
                >= lax.broadcasted_iota(jnp.int32, (nch, c_sz, c_sz), 2))
        o_heads = []
        for h in range(HG_HEADS):
            hl = slice(h * HG_DK, (h + 1) * HG_DK)
            q3 = qhat[:, hl].reshape(nch, c_sz, HG_DK)
            k3 = khat[:, hl].reshape(nch, c_sz, HG_DK)
            v3 = vb[:, hl].reshape(nch, c_sz, HG_DK)
            a_mat = jnp.where(tril, _bdot_nt(q3, k3), 0.0).astype(bf16)
            o3 = _bdot(a_mat, v3)
            st = st_s[hl, :]
            before = []
            for ci in range(nch):
                before.append(st.astype(bf16))
                st = st * ebl[ci][:, hl] + _dot_tn(v3[ci], kdec[ci][:, hl])
                sn_ref[ci, hl, :] = st.astype(bf16)
            st_s[hl, :] = st
            s3 = jnp.stack(before)
            ss_ref[:, hl, :] = s3
            o3 = o3 + _bdot_nt(q3, s3)
            o_heads.append(o3.reshape(tm, HG_DK))
        o = jnp.concatenate(o_heads, axis=1)
        o_ref[...] = o
        r = lax.rsqrt(_head_mean(o * o) + EPS)
        g = g_ref[...]
        out_ref[...] = (o * r * nw_ref[...] * (g * _sigmoid(g))).astype(bf16)

    out, o, ss, sn = _rt(
        "hg_fwd_" + tag, body, rows, tm,
        [(proj, BRANCH, U_COL + 1), (proj, BRANCH, U_COL + 2), (proj, BRANCH, U_COL + 3), (proj, BRANCH, U_COL + 4)],
        [cst["hg_lb"], cst["hg_nw"]],
        [(BRANCH, bf16), (BRANCH, f32),
         ((n_chunks, BRANCH, HG_DK), (nch, BRANCH, HG_DK), lambda t: (t, 0, 0), bf16),
         ((n_chunks, BRANCH, HG_DK), (nch, BRANCH, HG_DK), lambda t: (t, 0, 0), bf16)],
        scratch=[pltpu.VMEM((BRANCH, HG_DK), f32)])
    return out, (o, ss, sn)


def _hg_bwd(tag, saved, proj, cst, d_out, rows):
    o_saved, ss, sn = saved
    tm = min(ROW_TILE, rows)
    c_sz = HG_CHUNK
    nch = tm // c_sz

    def body(i, do_ref, q_ref, z_ref, v_ref, g_ref, o_ref, ss_ref, sn_ref, lb_ref, nw_ref,
             dq_ref, dz_ref, dv_ref, dg_ref, dlb_ref, dnw_ref, dst_s):
        @pl.when(i == 0)
        def _():
            dst_s[...] = jnp.zeros_like(dst_s)

        lb = lb_ref[...]
        q = q_ref[...]
        qs, qh, sg, fg, kk = _hg_prep(q, z_ref[...], lb)
        b = _seg_cumsum(jnp.log(fg), tm, c_sz)
        eb = jnp.exp(b)
        enb = jnp.exp(-b)
        qhat = (qh * eb).astype(bf16)
        khat = (kk * enb).astype(bf16)
        vb = v_ref[...].astype(bf16)
        b3 = b.reshape(nch, c_sz, BRANCH)
        bl3 = b3[:, c_sz - 1:c_sz, :]
        dec3 = jnp.exp(bl3 - b3)
        kdec = (kk.reshape(nch, c_sz, BRANCH) * dec3).astype(bf16)
        ebl = jnp.exp(bl3)
        g = g_ref[...]
        gs = _sigmoid(g)
        o = o_ref[...]
        r = lax.rsqrt(_head_mean(o * o) + EPS)
        oh = o * r
        nw = nw_ref[...]
        dov = do_ref[...]
        don = dov * (g * gs)
        dg_ref[...] = (dov * oh * nw * (gs * (1.0 + g * (1.0 - gs)))).astype(bf16)
        dnw_ref[...] += jnp.sum(don * oh, axis=0, keepdims=True)
        doh = don * nw
        d_o = r * (doh - oh * _head_mean(doh * oh))
        dob = d_o.astype(bf16)
        t_idx = lax.broadcasted_iota(jnp.int32, (nch, c_sz, c_sz), 1)
        s_idx = lax.broadcasted_iota(jnp.int32, (nch, c_sz, c_sz), 2)
        heads = []
        for h in range(HG_HEADS):
            hl = slice(h * HG_DK, (h + 1) * HG_DK)
            q3 = qhat[:, hl].reshape(nch, c_sz, HG_DK)
            k3 = khat[:, hl].reshape(nch, c_sz, HG_DK)
            v3 = vb[:, hl].reshape(nch, c_sz, HG_DK)
            do3 = dob[:, hl].reshape(nch, c_sz, HG_DK)
            s3 = ss_ref[:, hl, :]
            da_mat = jnp.where(t_idx >= s_idx, _bdot_nt(do3, v3), 0.0).astype(bf16)
            a_t = jnp.where(t_idx <= s_idx, _bdot_nt(k3, q3), 0.0).astype(bf16)
            da_t = jnp.where(t_idx <= s_idx, _bdot_nt(v3, do3), 0.0).astype(bf16)
            dqhat = _bdot(do3, s3) + _bdot(da_mat, k3)
            dkhat = _bdot(da_t, q3)
            dst = dst_s[hl, :]
            after = [None] * nch
            for ci in reversed(range(nch)):
                after[ci] = dst
                dst = dst * ebl[ci][:, hl] + _dot_tn(do3[ci], q3[ci])
            dst_s[hl, :] = dst
            ds3 = jnp.stack(after)
            ds3b = ds3.astype(bf16)
            dk_inter = _bdot(v3, ds3b) * dec3[:, :, hl]
            dv3 = _bdot(a_t, do3) + _bdot_nt(kdec[:, :, hl], ds3b)
            flux = jnp.sum(sn_ref[:, hl, :].astype(f32) * ds3, axis=1, keepdims=True)
            heads.append((dqhat.reshape(tm, HG_DK), dkhat.reshape(tm, HG_DK), dk_inter.reshape(tm, HG_DK),
                          dv3.reshape(tm, HG_DK), jnp.broadcast_to(flux, (nch, c_sz, HG_DK)).reshape(tm, HG_DK)))
        dqhat, dkhat, dk_inter, dv, flux = (jnp.concatenate(parts, axis=1) for parts in zip(*heads))
        dv_ref[...] = dv.astype(bf16)
        dqh = dqhat * eb
        dk = dkhat * enb + dk_inter
        db = qhat.astype(f32) * dqhat - khat.astype(f32) * dkhat - kk * dk_inter
        dlf = _seg_rev_cumsum(db, tm, c_sz) + flux
        tt = (1.0 - lb) * sg * (1.0 - sg)
        dz_ref[...] = (dlf * tt / fg - dk * tt).astype(bf16)
        dlb_ref[...] += jnp.sum(dlf * (1.0 - sg) / fg - dk * (1.0 - sg), axis=0, keepdims=True)
        dq_ref[...] = (dqh * (qs * (1.0 + q * (1.0 - qs)))).astype(bf16)

    dq, dz, dv, dg, d_lb, d_nw = _rt(
        "hg_bwd_" + tag, body, rows, tm,
        [(d_out, BRANCH, 0), (proj, BRANCH, U_COL + 1), (proj, BRANCH, U_COL + 2), (proj, BRANCH, U_COL + 3),
         (proj, BRANCH, U_COL + 4), (o_saved, BRANCH, 0), (ss, (nch, BRANCH, HG_DK), lambda t: (t, 0, 0)),
         (sn, (nch, BRANCH, HG_DK), lambda t: (t, 0, 0))],
        [cst["hg_lb"], cst["hg_nw"]],
        [(BRANCH, bf16)] * 4, acc_outs=[(1, BRANCH), (1, BRANCH)],
        scratch=[pltpu.VMEM((BRANCH, HG_DK), f32)],
        reverse=True)
    return dq, dz, dv, dg, {"hg_lb": d_lb, "hg_nw": d_nw}


def _rg_gates(xc, wa_ref, ba_ref, wx_ref, bx_ref, sp8):
    xcb = xc.astype(bf16)
    r = _sigmoid(_dot(xcb, wa_ref[...]) + ba_ref[...])
    ig = _sigmoid(_dot(xcb, wx_ref[...]) + bx_ref[...])
    la = -sp8 * r
    a = jnp.exp(la)
    mult = jnp.sqrt(-_expm1(2.0 * la))
    return xcb, r, ig, a, mult


def _rg_fwd(tag, proj, cst, rows):
    tm = min(ROW_TILE, rows)

    def body(i, xb_ref, gate_ref, cw_ref, cb_ref, wa_ref, ba_ref, wx_ref, bx_ref, sp_ref,
             out_ref, xc_ref, h_ref, hp_ref, prev_s, hc_s):
        @pl.when(i == 0)
        def _():
            prev_s[...] = jnp.zeros_like(prev_s)
            hc_s[...] = jnp.zeros_like(hc_s)

        row = _rows((tm, BRANCH))
        xb = xb_ref[...]
        prev = prev_s[...]
        xc = cb_ref[...] + cw_ref[3:4, :] * xb
        for j in range(1, 4):
            sh = jnp.where(row >= j, pltpu.roll(xb, j, 0), pltpu.roll(prev, j, 0))
            xc = xc + cw_ref[3 - j:4 - j, :] * sh
        prev_s[...] = xb
        xc_ref[...] = xc
        _, r, ig, a, mult = _rg_gates(xc, wa_ref, ba_ref, wx_ref, bx_ref, sp_ref[...])
        a_cum, h_loc = _scan_fwd(a, mult * ig * xc, tm)
        hc = hc_s[...]
        h = h_loc + a_cum * hc
        h_ref[...] = h
        hp_ref[...] = jnp.where(row >= 1, pltpu.roll(h, 1, 0), hc)
        hc_s[...] = h[tm - 1:tm, :]
        out_ref[...] = (h * _gelu(gate_ref[...])).astype(bf16)

    out, xc, h, hp = _rt(
        "rg_fwd_" + tag, body, rows, tm,
        [(proj, BRANCH, U_COL + 5), (proj, BRANCH, U_COL + 6)],
        [cst["rg_cw"], cst["rg_cb"], cst["rg_wa"].astype(bf16), cst["rg_ba"], cst["rg_wx"].astype(bf16),
         cst["rg_bx"], cst["rg_sp8"]],
        [(BRANCH, bf16), (BRANCH, f32), (BRANCH, f32), (BRANCH, f32)],
        scratch=[pltpu.VMEM((tm, BRANCH), f32), pltpu.VMEM((1, BRANCH), f32)])
    return out, (xc, h, hp)


def _rg_bwd(tag, saved, proj, cst, d_out, rows):
    xc_saved, h_saved, hp_saved = saved
    tm = min(ROW_TILE, rows)

    def body(i, do_ref, xb_ref, gate_ref, xc_ref, h_ref, hp_ref, cw_ref, wa_ref, ba_ref, wx_ref, bx_ref, sp_ref,
             dxb_ref, dgate_ref, dcw_ref, dcb_ref, dwa_ref, dba_ref, dwx_ref, dbx_ref, dsp_ref,
             nxt_s, ec_s):
        @pl.when(i == 0)
        def _():
            nxt_s[...] = jnp.zeros_like(nxt_s)
            ec_s[...] = jnp.zeros_like(ec_s)

        row = _rows((tm, BRANCH))
        xc = xc_ref[...]
        sp8 = sp_ref[...]
        xcb, r, ig, a, mult = _rg_gates(xc, wa_ref, ba_ref, wx_ref, bx_ref, sp8)
        gate = gate_ref[...]
        dov = do_ref[...]
        dh = dov * _gelu(gate)
        dgate_ref[...] = (dov * h_ref[...] * _gelu_grad(gate)).astype(bf16)
        a_cum, e_loc = _scan_bwd(a, a * dh, tm)
        ec = ec_s[...]
        e = e_loc + a_cum * ec
        g_tot = dh + jnp.where(row == tm - 1, ec, pltpu.roll(e, tm - 1, 0))
        ec_s[...] = e[0:1, :]
        d_a = g_tot * hp_ref[...]
        d_mult = g_tot * ig * xc
        d_ix = g_tot * mult
        d_ig = d_ix * xc
        d_xc = d_ix * ig
        d_la = d_a * a - d_mult * (a * a) / mult
        d_r = -d_la * sp8
        dsp_ref[...] += jnp.sum(-d_la * r, axis=0, keepdims=True)
        dzr = d_r * r * (1.0 - r)
        dzi = d_ig * ig * (1.0 - ig)
        dzrb = dzr.astype(bf16)
        dzib = dzi.astype(bf16)
        d_xc = d_xc + _dot_nt(dzrb, wa_ref[...]) + _dot_nt(dzib, wx_ref[...])
        dwa_ref[...] += _dot_tn(xcb, dzrb)
        dwx_ref[...] += _dot_tn(xcb, dzib)
        dba_ref[...] += jnp.sum(dzr, axis=0, keepdims=True)
        dbx_ref[...] += jnp.sum(dzi, axis=0, keepdims=True)
        dcb_ref[...] += jnp.sum(d_xc, axis=0, keepdims=True)
        nxt = nxt_s[...]
        xb = xb_ref[...]
        dxb = cw_ref[3:4, :] * d_xc
        dcw_ref[3:4, :] += jnp.sum(d_xc * xb, axis=0, keepdims=True)
        for j in range(1, 4):
            sh = jnp.where(row < tm - j, pltpu.roll(d_xc, tm - j, 0), pltpu.roll(nxt, tm - j, 0))
            dxb = dxb + cw_ref[3 - j:4 - j, :] * sh
            dcw_ref[3 - j:4 - j, :] += jnp.sum(sh * xb, axis=0, keepdims=True)
        nxt_s[...] = d_xc
        dxb_ref[...] = dxb.astype(bf16)

    wa = cst["rg_wa"].astype(bf16)
    wx = cst["rg_wx"].astype(bf16)
    dxb, dgate, d_cw, d_cb, d_wa, d_ba, d_wx, d_bx, d_sp = _rt(
        "rg_bwd_" + tag, body, rows, tm,
        [(d_out, BRANCH, 0), (proj, BRANCH, U_COL + 5), (proj, BRANCH, U_COL + 6), (xc_saved, BRANCH, 0),
         (h_saved, BRANCH, 0), (hp_saved, BRANCH, 0)],
        [cst["rg_cw"], wa, cst["rg_ba"], wx, cst["rg_bx"], cst["rg_sp8"]],
        [(BRANCH, bf16), (BRANCH, bf16)],
        acc_outs=[(4, BRANCH), (1, BRANCH), (BRANCH, BRANCH), (1, BRANCH), (BRANCH, BRANCH), (1, BRANCH), (1, BRANCH)],
        scratch=[pltpu.VMEM((tm, BRANCH), f32), pltpu.VMEM((1, BRANCH), f32)],
        reverse=True)
    dcst = {"rg_cw": d_cw, "rg_cb": d_cb, "rg_wa": d_wa, "rg_ba": d_ba, "rg_wx": d_wx, "rg_bx": d_bx, "rg_sp8": d_sp}
    return dxb, dgate, dcst


def _merge_fwd(tag, proj, outs, bp, rows):
    def body(i, ya_ref, yb_ref, yc_ref, gm_ref, p_ref, m_ref):
        acc = None
        for n, y_ref in enumerate((ya_ref, yb_ref, yc_ref)):
            up = _dot(y_ref[...], p_ref[n])
            term = _sigmoid(gm_ref[:, n * D_MODEL:(n + 1) * D_MODEL]) * up
            acc = term if acc is None else acc + term
        m_ref[...] = acc.astype(bf16)
    return _rt("merge_fwd_" + tag, body, rows, ROW_TILE,
               [(outs[0], BRANCH, 0), (outs[1], BRANCH, 0), (outs[2], BRANCH, 0), (proj, GM_WIDTH, 0)],
               [bp], [(D_MODEL, bf16)])[0]


def _merge_bwd(tag, proj, outs, bp, dmerged, rows):
    def body(i, dm_ref, ya_ref, yb_ref, yc_ref, gm_ref, p_ref, da_ref, db_ref, dc_ref, dgm_ref, dp_ref):
        dm = dm_ref[...]
        for n, (y_ref, dy_ref) in enumerate(((ya_ref, da_ref), (yb_ref, db_ref), (yc_ref, dc_ref))):
            yv = y_ref[...]
            up = _dot(yv, p_ref[n])
            gt = _sigmoid(gm_ref[:, n * D_MODEL:(n + 1) * D_MODEL])
            dup = (dm * gt).astype(bf16)
            dgm_ref[:, n * D_MODEL:(n + 1) * D_MODEL] = (dm * up * gt * (1.0 - gt)).astype(bf16)
            dy_ref[...] = _dot_nt(dup, p_ref[n])
            dp_ref[n] += _dot_tn(yv, dup)
    return _rt("merge_bwd_" + tag, body, rows, ROW_TILE,
               [(dmerged, D_MODEL, 0), (outs[0], BRANCH, 0), (outs[1], BRANCH, 0), (outs[2], BRANCH, 0),
                (proj, GM_WIDTH, 0)],
               [bp], [(BRANCH, f32), (BRANCH, f32), (BRANCH, f32), (GM_WIDTH, bf16)],
               acc_outs=[(N_BRANCH, BRANCH, D_MODEL)])


def _block_diag(blocks):
    g, r, c = blocks.shape
    on_diag = (lax.broadcasted_iota(jnp.int32, (g * r, g * c), 0) // r
               == lax.broadcasted_iota(jnp.int32, (g * r, g * c), 1) // c)
    tiled = jnp.broadcast_to(blocks.reshape(g * r, 1, c), (g * r, g, c)).reshape(g * r, g * c)
    return jnp.where(on_diag, tiled, 0.0)


def _prep_consts(sp):
    p = jax.nn.softmax(sp["hg_lb_logits"], axis=0)
    lower = jnp.cumsum(p, axis=0) - p[0]
    out = []
    for l in range(DEPTH):
        lr = jnp.minimum(sp["s5_lambda_re"][l], S5_EIG_MAX)
        li = sp["s5_lambda_im"][l]
        dt = jnp.exp(sp["s5_log_dt"][l])[:, None]
        mag = jnp.exp(lr * dt)
        ar = mag * jnp.cos(li * dt)
        ai = mag * jnp.sin(li * dt)
        den = lr * lr + li * li
        fr = ((ar - 1.0) * lr + ai * li) / den
        fi = (ai * lr - (ar - 1.0) * li) / den
        br, bi = sp["s5_b_re"][l], sp["s5_b_im"][l]
        bbr = fr[..., None] * br - fi[..., None] * bi
        bbi = fr[..., None] * bi + fi[..., None] * br
        c = {
            "a_re": ar.reshape(1, S5_LANES), "a_im": ai.reshape(1, S5_LANES),
            "b_re": _block_diag(bbr.transpose(0, 2, 1)), "b_im": _block_diag(bbi.transpose(0, 2, 1)),
            "c_re": _block_diag(sp["s5_c_re"][l].transpose(0, 2, 1)),
            "c_im": -_block_diag(sp["s5_c_im"][l].transpose(0, 2, 1)),
            "s5_d": sp["s5_d"][l][None], "glu_b": sp["s5_glu_b"][l][None],
            "hg_lb": lower[l][None], "hg_nw": sp["hg_norm_w"][l][None],
            "rg_cw": sp["rg_conv_w"][l], "rg_cb": sp["rg_conv_b"][l][None],
            "rg_wa": _block_diag(sp["rg_wa"][l]), "rg_ba": sp["rg_ba"][l][None],
            "rg_wx": _block_diag(sp["rg_wx"][l]), "rg_bx": sp["rg_bx"][l][None],
            "rg_sp8": (RG_C * jax.nn.softplus(-sp["rg_lambda"][l]))[None],
        }
        out.append(c)
    return out


def _mixer_fwd(tag, x, hb, w_in, bp, w_out, cst, next_nw, rows):
    proj = _mm1("mix_proj_" + tag, hb, w_in, "nn", rows, IN_TOTAL, D_MODEL, 512, IN_TOTAL // 4, D_MODEL)
    cst = dict(cst)
    out_a, sv_a = _s5_fwd(tag, proj, cst, rows)
    out_b, sv_b = _hg_fwd(tag, proj, cst, rows)
    out_c, sv_c = _rg_fwd(tag, proj, cst, rows)
    merged = _merge_fwd(tag, proj, (out_a, out_b, out_c), bp, rows)
    x_out, hb_out = _mm("mix_out_" + tag, [merged], [w_out], [(0, 0, 0)], 1, "nn", rows, D_MODEL, D_MODEL,
                        512, D_MODEL, D_MODEL, [f32, bf16], _residual_then_norm(1.0), extras=[x], vecs=[next_nw])
    return x_out, hb_out, (x, hb, proj, (out_a, out_b, out_c), merged, sv_a, sv_b, sv_c)


def _mixer_bwd(tag, saved, nw, w_in, bp, w_out, cst, dx, dxb, rows):
    x, hb, proj, outs, merged, sv_a, sv_b, sv_c = saved
    d_wout = _mm1("mix_dwout_" + tag, merged, dxb, "tn", D_MODEL, D_MODEL, rows, D_MODEL, D_MODEL, TOKEN_K,
                  out_dtype=bf16)
    dmerged = _mm1("mix_dmerged_" + tag, dxb, w_out, "nt", rows, D_MODEL, D_MODEL, 512, D_MODEL, D_MODEL)
    d_a, d_b, d_c, dgm, d_bp = _merge_bwd(tag, proj, outs, bp, dmerged, rows)
    dxbc, dgatec, dcst_c = _rg_bwd(tag, sv_c, proj, cst, d_c, rows)
    dq, dz, dv, dg, dcst_b = _hg_bwd(tag, sv_b, proj, cst, d_b, rows)
    du, dcst_a, d_glu_w = _s5_bwd(tag, sv_a, proj, cst, d_a, rows)
    dproj = jnp.concatenate([dgm, du, dq, dz, dv, dg, dxbc, dgatec], axis=1)
    d_win = _mm1("mix_dwin_" + tag, hb, dproj, "tn", D_MODEL, IN_TOTAL, rows, D_MODEL, IN_TOTAL // 4, TOKEN_K,
                 out_dtype=bf16)
    dx_in, dxb_in, d_nw = _mm("mix_dh_" + tag, [dproj], [w_in], [(0, 0, 0)], 1, "nt", rows, D_MODEL, IN_TOTAL,
                              512, D_MODEL, IN_TOTAL // 2, [f32, bf16], _norm_bwd_epilogue, extras=[x, dx], vecs=[nw],
                              n_part=1)
    dcst = {**dcst_a, **dcst_b, **dcst_c}
    return dx_in, dxb_in, jnp.sum(d_nw, axis=0), d_win, d_bp, d_wout, d_glu_w, dcst


def _local_step(x, target, weights_of, small, grads_done):
    rows = x.shape[0]
    consts, consts_vjp = jax.vjp(_prep_consts, small)
    norm_w = small["norm_w"]
    saved = []
    h = x
    hb = _rms_fwd("first_norm", x, norm_w[0, 0][None], rows)
    for l in range(DEPTH):
        t = str(l)
        after = [norm_w[l + 1, 0][None]] if l + 1 < DEPTH else []
        wa = weights_of(l, "a")
        h, hb, sv0 = _ffn_fwd(t + "a", h, hb, *wa, [norm_w[l, 1][None]], rows)
        wm = weights_of(l, "mix")
        cst = dict(consts[l])
        cst["glu_w"] = wm[3]
        h, hb, sv1 = _mixer_fwd(t, h, hb, *wm[:3], cst, norm_w[l, 2][None], rows)
        wb = weights_of(l, "b")
        h, hb, sv2 = _ffn_fwd(t + "b", h, hb, *wb, after, rows)
        saved.append((sv0, sv1, sv2, cst, wa, wm, wb))
    dx, dxb, loss, d_fnw = _loss_head(h, small["final_norm_w"][None], target, rows)
    d_norm = [None] * DEPTH
    d_consts = [None] * DEPTH
    for l in reversed(range(DEPTH)):
        t = str(l)
        sv0, sv1, sv2, cst, wa, wm, wb = saved[l]
        dx, dxb, dn2, dg1, du1, dd1 = _ffn_bwd(t + "b", sv2, norm_w[l, 2][None], *wb, dx, dxb, rows)
        grads_done(l, "b", [dg1, du1, dd1])
        dx, dxb, dn1, d_win, d_bp, d_wout, d_glu_w, dcst = _mixer_bwd(
            t, sv1, norm_w[l, 1][None], *wm[:3], cst, dx, dxb, rows)
        grads_done(l, "mix", [d_win, d_bp, d_wout, d_glu_w])
        dx, dxb, dn0, dg0, du0, dd0 = _ffn_bwd(t + "a", sv0, norm_w[l, 0][None], *wa, dx, dxb, rows)
        grads_done(l, "a", [dg0, du0, dd0])
        d_norm[l] = jnp.concatenate([dn0, dn1, dn2], axis=0)
        d_consts[l] = dcst
    (g_small,) = consts_vjp(d_consts)
    g_small = dict(g_small)
    g_small["norm_w"] = g_small["norm_w"] + jnp.stack(d_norm)
    g_small["final_norm_w"] = g_small["final_norm_w"] + d_fnw[0]
    return loss[0, 0], dx, g_small


def _other_chips(x, y):
    return [(1 - x, y), (x, 1 - y), (1 - x, 1 - y)]


def _gather_over_ici(shards):
    n = len(shards)

    def copies(in_refs, out_refs, send_sems, recv_sems):
        x, y, c = _place()
        cps = []
        for i in range(n):
            mine = out_refs[i].at[4 * x + 2 * y + c]
            cps.append(pltpu.make_async_copy(in_refs[i], mine, send_sems.at[5 * i + 4]))
            for k, to in enumerate([(x, y, 1 - c)] + [(px, py, c) for px, py in _other_chips(x, y)]):
                cps.append(pltpu.make_async_remote_copy(
                    src_ref=in_refs[i], dst_ref=mine, send_sem=send_sems.at[5 * i + k],
                    recv_sem=recv_sems.at[5 * i + k], device_id=to, device_id_type=MESH_IDS))
        return cps

    return _Carry(shards, [jax.ShapeDtypeStruct((N_DEV,) + s.shape, s.dtype) for s in shards], 5 * n, copies)


def _gather_forward(name, landings):
    n = len(landings)

    def body(*refs):
        in_refs, out_refs = refs[:n], refs[n:2 * n]
        send_sems, recv_sems = refs[2 * n:]
        x, y, c = _place()
        cps = []
        for i in range(n):
            for j, (px, py) in enumerate(_other_chips(x, y)):
                block = 4 * px + 2 * py + c
                cps.append(pltpu.make_async_remote_copy(
                    src_ref=in_refs[i].at[block], dst_ref=out_refs[i].at[block], send_sem=send_sems.at[3 * i + j],
                    recv_sem=recv_sems.at[3 * i + j], device_id=(x, y, 1 - c), device_id_type=MESH_IDS))
        for cp in cps:
            cp.start()
        for cp in cps:
            cp.wait()

    return pl.pallas_call(
        body, name=name, out_shape=[jax.ShapeDtypeStruct(a.shape, a.dtype) for a in landings],
        in_specs=[ANY_SPEC] * n, out_specs=[ANY_SPEC] * n, input_output_aliases={i: i for i in range(n)},
        scratch_shapes=[pltpu.SemaphoreType.DMA((3 * n,)), pltpu.SemaphoreType.DMA((3 * n,))],
    )(*landings)


def _all_gather(name, shards):
    n = len(shards)

    def body(*refs):
        x_refs, out_refs = refs[:n], refs[n:2 * n]
        send_sems, recv_sems, local_sems = refs[2 * n:]
        x, y, c = _place()
        me, sibling = (x, y, c), (x, y, 1 - c)
        chips = [(1 - x, y), (x, 1 - y), (1 - x, 1 - y)]

        def blk(i, px, py, pc):
            return out_refs[i].at[4 * px + 2 * py + pc]

        def copy(i, k, block, to, src=None):
            return pltpu.make_async_remote_copy(
                src_ref=blk(i, *block) if src is None else src, dst_ref=blk(i, *block),
                send_sem=send_sems.at[7 * i + k], recv_sem=recv_sems.at[7 * i + k], device_id=to,
                device_id_type=MESH_IDS)

        mine = [pltpu.make_async_copy(x_refs[i], blk(i, *me), local_sems.at[i]) for i in range(n)]
        for cp in mine:
            cp.start()
        first = []
        for i in range(n):
            first.append(copy(i, 0, me, sibling, src=x_refs[i]))
            first += [copy(i, 1 + j, me, (*chip, c), src=x_refs[i]) for j, chip in enumerate(chips)]
        for cp in first:
            cp.start()
        passed = []
        for j, chip in enumerate(chips):
            for i in range(n):
                copy(i, 1 + j, (*chip, c), me).wait_recv()
                fwd = copy(i, 4 + j, (*chip, c), sibling)
                fwd.start()
                passed.append(fwd)
        for i in range(n):
            copy(i, 0, sibling, me).wait_recv()
            for j, chip in enumerate(chips):
                copy(i, 4 + j, (*chip, 1 - c), me).wait_recv()
        for cp in first + passed:
            cp.wait_send()
        for cp in mine:
            cp.wait()

    return pl.pallas_call(
        body, name=name, out_shape=[jax.ShapeDtypeStruct((N_DEV,) + s.shape, s.dtype) for s in shards],
        in_specs=[ANY_SPEC] * n, out_specs=[ANY_SPEC] * n,
        scratch_shapes=[pltpu.SemaphoreType.DMA((7 * n,)), pltpu.SemaphoreType.DMA((7 * n,)),
                        pltpu.SemaphoreType.DMA((n,))],
    )(*shards)


def _row_tile(rows):
    return rows if rows <= 512 else next(t for t in range(512, 7, -8) if rows % t == 0)


def _sums_over_ici(chip_sums):
    n = len(chip_sums)

    def copies(in_refs, out_refs, send_sems, recv_sems):
        x, y, c = _place()
        return [pltpu.make_async_remote_copy(
            src_ref=in_refs[i].at[2 * px + py], dst_ref=out_refs[i].at[k], send_sem=send_sems.at[3 * i + k],
            recv_sem=recv_sems.at[3 * i + k], device_id=(px, py, c), device_id_type=MESH_IDS)
            for i in range(n) for k, (px, py) in enumerate(_other_chips(x, y))]

    return _Carry(chip_sums, [jax.ShapeDtypeStruct((3,) + t.shape[1:], t.dtype) for t in chip_sums], 3 * n, copies)


def _reduce_scatter(tag, parts, hosts=None):
    n = len(parts)
    _, _, c = _place()

    def body_pair(*refs):
        p_refs, got_refs = refs[:n], refs[n:2 * n]
        send_sems, recv_sems = refs[2 * n:]
        x, y, c = _place()
        cps = [pltpu.make_async_remote_copy(
            src_ref=p_refs[i].at[1 - c], dst_ref=got_refs[i], send_sem=send_sems.at[i], recv_sem=recv_sems.at[i],
            device_id=(x, y, 1 - c), device_id_type=MESH_IDS) for i in range(n)]
        for cp in cps:
            cp.start()
        for cp in cps:
            cp.wait()

    from_sibling = pl.pallas_call(
        body_pair, name="rs_pair_" + tag, out_shape=[jax.ShapeDtypeStruct(p.shape[1:], p.dtype) for p in parts],
        in_specs=[ANY_SPEC] * n, out_specs=[ANY_SPEC] * n,
        scratch_shapes=[pltpu.SemaphoreType.DMA((n,)), pltpu.SemaphoreType.DMA((n,))],
    )(*parts)

    chip_sums = []
    for i, (part, got) in enumerate(zip(parts, from_sibling)):
        _, _, r, cols = part.shape
        tr = _row_tile(r)

        def body_add(idx_ref, p_ref, g_ref, o_ref):
            o_ref[...] = (p_ref[...].astype(f32) + g_ref[...].astype(f32)).astype(o_ref.dtype)

        chip_sums.append(pl.pallas_call(
            body_add, name="rs_pair_sum_%s_%d" % (tag, i), out_shape=jax.ShapeDtypeStruct((4, r, cols), part.dtype),
            grid_spec=pltpu.PrefetchScalarGridSpec(
                num_scalar_prefetch=1, grid=(4, r // tr),
                in_specs=[pl.BlockSpec((None, None, tr, cols), lambda j, t, idx: (idx[0], j, t, 0)),
                          pl.BlockSpec((None, tr, cols), lambda j, t, idx: (j, t, 0))],
                out_specs=pl.BlockSpec((None, tr, cols), lambda j, t, idx: (j, t, 0))),
            compiler_params=_cparams(("parallel", "parallel")),
        )(jnp.stack([c]).astype(jnp.int32), part, got))

    others = [None] * n
    riding = set()
    for host, which in (hosts or {}).items():
        rider = _sums_over_ici([chip_sums[i] for i in which])
        _CARRIED[host] = rider
        for pos, i in enumerate(which):
            others[i] = functools.partial(lambda r, p: r.outs[p], rider, pos)
        riding.update(which)
    rest = [i for i in range(n) if i not in riding]
    if rest:
        alone = _sums_over_ici([chip_sums[i] for i in rest])

        def body_chips(*refs):
            k = len(rest)
            cps = alone.copies(refs[:k], refs[k:2 * k], *refs[2 * k:])
            for cp in cps:
                cp.start()
            for cp in cps:
                cp.wait()

        from_chips = pl.pallas_call(
            body_chips, name="rs_chips_" + tag, out_shape=alone.out_shapes,
            in_specs=[ANY_SPEC] * len(rest), out_specs=[ANY_SPEC] * len(rest), scratch_shapes=alone.sems(),
        )(*alone.ins)
        for pos, i in enumerate(rest):
            others[i] = functools.partial(lambda got: got, from_chips[pos])
    return list(zip(chip_sums, others))


def _own_index():
    x, y, _ = _place()
    return jnp.stack([2 * x + y]).astype(jnp.int32)


def _own_total(name, chip_sum, others):
    _, r, cols = chip_sum.shape
    tr = _row_tile(r)

    def body(idx_ref, t_ref, g_ref, o_ref):
        o_ref[...] = ((t_ref[...].astype(f32) + g_ref[0].astype(f32)) + g_ref[1].astype(f32)) + g_ref[2].astype(f32)

    return pl.pallas_call(
        body, name=name, out_shape=jax.ShapeDtypeStruct((r, cols), f32),
        grid_spec=pltpu.PrefetchScalarGridSpec(
            num_scalar_prefetch=1, grid=(r // tr,),
            in_specs=[pl.BlockSpec((None, tr, cols), lambda t, idx: (idx[0], t, 0)),
                      pl.BlockSpec((3, tr, cols), lambda t, idx: (0, t, 0))],
            out_specs=pl.BlockSpec((tr, cols), lambda t, idx: (t, 0))),
        compiler_params=_cparams(("parallel",)),
    )(_own_index(), chip_sum, others)


def _adam_update(w, gv, m, v):
    m_new = ADAM_B1 * m + (1.0 - ADAM_B1) * gv
    v_new = ADAM_B2 * v + (1.0 - ADAM_B2) * (gv * gv)
    m_hat = m_new / (1.0 - ADAM_B1 ** ADAM_STEP)
    v_hat = v_new / (1.0 - ADAM_B2 ** ADAM_STEP)
    return -ADAM_LR * (m_hat / (jnp.sqrt(v_hat) + ADAM_EPS) + ADAM_WD * w), m_new, v_new


def _adamw_reduced(name, w, pieces, m, v):
    n_p, rows, cols = w.shape
    tr = _row_tile(rows)

    def body(idx_ref, w_ref, *refs):
        red = refs[:2 * n_p]
        m_ref, v_ref, g_ref, d_ref, nm_ref, nv_ref = refs[2 * n_p:]
        p = pl.program_id(0)
        for q in range(n_p):
            @pl.when(p == q)
            def _(t_ref=red[2 * q], o_ref=red[2 * q + 1]):
                gv = ((t_ref[...].astype(f32) + o_ref[0].astype(f32)) + o_ref[1].astype(f32)) + o_ref[2].astype(f32)
                g_ref[...] = gv
                d_ref[...], nm_ref[...], nv_ref[...] = _adam_update(w_ref[...], gv, m_ref[...], v_ref[...])

    spec = pl.BlockSpec((None, tr, cols), lambda p, t, idx: (p, t, 0))
    red_specs, red_args = [], []
    for q, (chip_sum, others) in enumerate(pieces):
        red_specs.append(pl.BlockSpec((None, tr, cols), lambda p, t, idx, q=q: (idx[0], jnp.where(p == q, t, 0), 0)))
        red_specs.append(pl.BlockSpec((3, tr, cols), lambda p, t, idx, q=q: (0, jnp.where(p == q, t, 0), 0)))
        red_args += [chip_sum, others]
    return pl.pallas_call(
        body, name=name, out_shape=[jax.ShapeDtypeStruct((n_p, rows, cols), f32)] * 4,
        grid_spec=pltpu.PrefetchScalarGridSpec(
            num_scalar_prefetch=1, grid=(n_p, rows // tr),
            in_specs=[spec] + red_specs + [spec, spec], out_specs=[spec] * 4),
        compiler_params=_cparams(("parallel", "parallel")),
    )(_own_index(), w, *red_args, m, v)


def _adamw(name, w, g, m, v):
    rows, cols = w.shape
    tr = _row_tile(rows)

    def body(w_ref, g_ref, m_ref, v_ref, d_ref, nm_ref, nv_ref):
        d_ref[...], nm_ref[...], nv_ref[...] = _adam_update(w_ref[...], g_ref[...], m_ref[...], v_ref[...])

    spec = pl.BlockSpec((tr, cols), lambda i: (i, 0))
    return pl.pallas_call(
        body, name=name, grid=(rows // tr,), in_specs=[spec] * 4, out_specs=[spec] * 3,
        out_shape=[jax.ShapeDtypeStruct((rows, cols), f32)] * 3, compiler_params=_cparams(("parallel",)),
    )(w, g, m, v)


WEIGHT_NAMES = ["norm_w", "final_norm_w", "ffn_gate", "ffn_up", "ffn_down", "w_in", "branch_proj", "w_out",
                "s5_lambda_re", "s5_lambda_im", "s5_log_dt", "s5_b_re", "s5_b_im", "s5_c_re", "s5_c_im", "s5_d",
                "s5_glu_w", "s5_glu_b", "hg_lb_logits", "hg_norm_w", "rg_conv_w", "rg_conv_b", "rg_wa", "rg_ba",
                "rg_wx", "rg_bx", "rg_lambda"]
SHARDED = {"ffn_gate": (3, "gate"), "ffn_up": (3, "up"), "ffn_down": (2, "down"), "w_in": (2, "w_in"),
           "branch_proj": (3, "bp"), "w_out": (1, "w_out"), "s5_glu_w": (1, "glu_w"),
           "norm_w": (2, None), "rg_conv_w": (2, None)}
BIG = ["ffn_gate", "ffn_up", "ffn_down", "w_in", "branch_proj", "w_out", "s5_glu_w"]
PARTS = {"a": [("ffn_gate", 0, 1), ("ffn_up", 0, 1), ("ffn_down", 0, 0)],
         "b": [("ffn_gate", 1, 1), ("ffn_up", 1, 1), ("ffn_down", 1, 0)],
         "mix": [("w_in", None, 1), ("branch_proj", None, 2), ("w_out", None, 0), ("s5_glu_w", None, 0)]}
AG_HOSTS = {"ffn_up_0a": (0, "mix", [0]), "ffn_down_0a": (0, "mix", [1, 2, 3]), "mix_proj_0": (0, "b", [0, 1, 2]),
            "s5_out_0": (1, "a", [0]), "hg_fwd_0": (1, "a", [1]), "rg_fwd_0": (1, "a", [2]),
            "merge_fwd_0": (1, "b", [0]), "ffn_up_0b": (1, "b", [1]), "ffn_down_0b": (1, "b", [2]),
            "s5_scan_fwd_0": (1, "mix", [0, 1]), "mix_out_0": (1, "mix", [2, 3])}
RS_HOSTS = {(1, "b"): {"s5_scan_bwd_1": [0, 1, 2]},
            (1, "mix"): {"ffn_bwd_mid_1a": [1, 2, 3], "ffn_dh_1a": [0]},
            (1, "a"): {"ffn_dh_0b": [0, 1], "merge_bwd_0": [2]},
            (0, "b"): {"s5_scan_bwd_0": [0, 1, 2]},
            (0, "mix"): {"ffn_bwd_mid_0a": [1, 2, 3], "ffn_dh_0a": [0]}}
SMALL_SHARDED = ["norm_w", "rg_conv_w"]
REPLICATED = [n for n in WEIGHT_NAMES if n not in SHARDED]
LANES = 128


PACK_ROWS = 512


def _pack_rows(arrays, names):
    pieces = []
    for n in names:
        flat = arrays[n].reshape(-1)
        pieces.append(jnp.pad(flat, (0, -flat.shape[0] % LANES)).reshape(-1, LANES))
    rows = jnp.concatenate(pieces, axis=0)
    return jnp.pad(rows, ((0, -rows.shape[0] % PACK_ROWS), (0, 0)))


def _unpack_rows(rows, names, like):
    out, r0 = {}, 0
    for n in names:
        size = math.prod(like[n].shape)
        nrows = -(-size // LANES)
        out[n] = rows[r0:r0 + nrows].reshape(-1)[:size].reshape(like[n].shape)
        r0 += nrows
    return out


def _unshard(gathered, axis):
    g = jnp.moveaxis(gathered, 0, axis)
    shp = g.shape
    return g.reshape(shp[:axis] + (shp[axis] * shp[axis + 1],) + shp[axis + 2:])


def _to_blocks(full, axis):
    shp = full.shape
    g = full.reshape(shp[:axis] + (4, 2, shp[axis] // N_DEV) + shp[axis + 1:])
    g = jnp.moveaxis(g, (axis, axis + 1), (1, 0))
    return g.reshape(2, 4, -1, g.shape[-1])


W_IN_SPLIT = IN_TOTAL - GM_WIDTH


def kernel(x, norm_w, final_norm_w, ffn_gate, ffn_up, ffn_down, w_in, branch_proj, w_out, s5_lambda_re, s5_lambda_im, s5_log_dt, s5_b_re, s5_b_im, s5_c_re, s5_c_im, s5_d, s5_glu_w, s5_glu_b, hg_lb_logits, hg_norm_w, rg_conv_w, rg_conv_b, rg_wa, rg_ba, rg_wx, rg_bx, rg_lambda, loss_target, m_norm_w, m_final_norm_w, m_ffn_gate, m_ffn_up, m_ffn_down, m_w_in, m_branch_proj, m_w_out, m_s5_lambda_re, m_s5_lambda_im, m_s5_log_dt, m_s5_b_re, m_s5_b_im, m_s5_c_re, m_s5_c_im, m_s5_d, m_s5_glu_w, m_s5_glu_b, m_hg_lb_logits, m_hg_norm_w, m_rg_conv_w, m_rg_conv_b, m_rg_wa, m_rg_ba, m_rg_wx, m_rg_bx, m_rg_lambda, v_norm_w, v_final_norm_w, v_ffn_gate, v_ffn_up, v_ffn_down, v_w_in, v_branch_proj, v_w_out, v_s5_lambda_re, v_s5_lambda_im, v_s5_log_dt, v_s5_b_re, v_s5_b_im, v_s5_c_re, v_s5_c_im, v_s5_d, v_s5_glu_w, v_s5_glu_b, v_hg_lb_logits, v_hg_norm_w, v_rg_conv_w, v_rg_conv_b, v_rg_wa, v_rg_ba, v_rg_wx, v_rg_bx, v_rg_lambda):
    w = dict(zip(WEIGHT_NAMES, (norm_w, final_norm_w, ffn_gate, ffn_up, ffn_down, w_in, branch_proj, w_out,
                                s5_lambda_re, s5_lambda_im, s5_log_dt, s5_b_re, s5_b_im, s5_c_re, s5_c_im, s5_d,
                                s5_glu_w, s5_glu_b, hg_lb_logits, hg_norm_w, rg_conv_w, rg_conv_b, rg_wa, rg_ba,
                                rg_wx, rg_bx, rg_lambda)))
    m = dict(zip(WEIGHT_NAMES, (m_norm_w, m_final_norm_w, m_ffn_gate, m_ffn_up, m_ffn_down, m_w_in, m_branch_proj,
                                m_w_out, m_s5_lambda_re, m_s5_lambda_im, m_s5_log_dt, m_s5_b_re, m_s5_b_im, m_s5_c_re,
                                m_s5_c_im, m_s5_d, m_s5_glu_w, m_s5_glu_b, m_hg_lb_logits, m_hg_norm_w, m_rg_conv_w,
                                m_rg_conv_b, m_rg_wa, m_rg_ba, m_rg_wx, m_rg_bx, m_rg_lambda)))
    v = dict(zip(WEIGHT_NAMES, (v_norm_w, v_final_norm_w, v_ffn_gate, v_ffn_up, v_ffn_down, v_w_in, v_branch_proj,
                                v_w_out, v_s5_lambda_re, v_s5_lambda_im, v_s5_log_dt, v_s5_b_re, v_s5_b_im, v_s5_c_re,
                                v_s5_c_im, v_s5_d, v_s5_glu_w, v_s5_glu_b, v_hg_lb_logits, v_hg_norm_w, v_rg_conv_w,
                                v_rg_conv_b, v_rg_wa, v_rg_ba, v_rg_wx, v_rg_bx, v_rg_lambda)))
    rows = x.shape[1]

    _CARRIED.clear()

    def shard_of(piece, l):
        n, k, _ = piece
        return (w[n][l] if k is None else w[n][l, k]).astype(bf16)

    def assemble(part, gathered):
        full = [_unshard(g, piece[2]) for piece, g in zip(PARTS[part], gathered)]
        if part == "mix":
            full[0] = jnp.concatenate([full[0][:, W_IN_SPLIT:], full[0][:, :W_IN_SPLIT]], axis=1)
        return full

    n_a = len(PARTS["a"])
    first = _all_gather("gather_weights", [shard_of(p, 0) for p in PARTS["a"]] + [w[n] for n in SMALL_SHARDED])
    small = {n: w[n] for n in REPLICATED}
    for n, g in zip(SMALL_SHARDED, first[n_a:]):
        small[n] = _unshard(g, SHARDED[n][0])
    riders = {}
    for host, (l, part, which) in AG_HOSTS.items():
        rider = _gather_over_ici([shard_of(PARTS[part][j], l) for j in which])
        _CARRIED[host] = rider
        riders.setdefault((l, part), []).append((which, rider))

    def weights_of(l, part):
        if (l, part) == (0, "a"):
            return assemble(part, first[:n_a])
        landed = [None] * len(PARTS[part])
        for which, rider in riders[l, part]:
            for j, buf in zip(which, rider.outs):
                landed[j] = buf
        return assemble(part, _gather_forward("gather_forward_%d%s" % (l, part), landed))

    sums = {}

    def blocks_of(part, grads):
        if part == "mix":
            grads = [jnp.concatenate([grads[0][:, GM_WIDTH:], grads[0][:, :GM_WIDTH]], axis=1)] + grads[1:]
        return [_to_blocks(g, piece[2]).astype(bf16) for piece, g in zip(PARTS[part], grads)]

    last_grads = []

    def grads_done(l, part, grads):
        if (l, part) in RS_HOSTS:
            sums[l, part] = _reduce_scatter("%d%s" % (l, part), blocks_of(part, grads), hosts=RS_HOSTS[l, part])
        else:
            last_grads.extend(blocks_of(part, grads))

    loss_part, dx, g_small = _local_step(x[0], loss_target[0], weights_of, small, grads_done)
    loss = lax.psum(loss_part, ("x", "y", "c"))

    parts = last_grads + [_to_blocks(g_small[n], SHARDED[n][0]) for n in SMALL_SHARDED]
    rep_rows = _pack_rows(g_small, REPLICATED)
    rep_slice = rep_rows.shape[0] // N_DEV
    parts.append(rep_rows.reshape(4, 2, rep_slice, LANES).transpose(1, 0, 2, 3))
    last = _reduce_scatter("last", parts)
    sums[0, "a"] = last[:n_a]

    grads, delta, new_m, new_v = {}, {}, {}, {}

    def update(n, pieces):
        shp = w[n].shape
        view = (len(pieces), -1, shp[-1])
        res = _adamw_reduced("adamw_" + n, w[n].reshape(view), [(t, others()) for t, others in pieces],
                             m[n].reshape(view), v[n].reshape(view))
        grads[n], delta[n], new_m[n], new_v[n] = (r.reshape(shp) for r in res)

    for n in BIG:
        update(n, [sums[l, part][j] for l in range(DEPTH) for part in ("a", "b", "mix")
                   for j, piece in enumerate(PARTS[part]) if piece[0] == n])
    for j, n in enumerate(SMALL_SHARDED):
        update(n, [last[n_a + j]])
    rep_mine = _own_total("rs_total_small", last[-1][0], last[-1][1]())
    rep_grads = _all_gather("gather_small_grads", [rep_mine])[0].reshape(-1, LANES)
    res = _adamw("adamw_small", _pack_rows(w, REPLICATED), rep_grads, _pack_rows(m, REPLICATED), _pack_rows(v, REPLICATED))
    for dst, src in zip((grads, delta, new_m, new_v), (rep_grads,) + tuple(res)):
        dst.update(_unpack_rows(src, REPLICATED, w))

    return (loss, dx.reshape(x.shape), *[grads[n] for n in WEIGHT_NAMES], *[delta[n] for n in WEIGHT_NAMES],
            *[new_m[n] for n in WEIGHT_NAMES], *[new_v[n] for n in WEIGHT_NAMES])
```

```python
import functools
import math

import jax
import jax.numpy as jnp
from jax import lax
from jax.experimental import pallas as pl
from jax.experimental.pallas import tpu as pltpu

f32 = jnp.float32
bf16 = jnp.bfloat16

D_MODEL = 1024
DEPTH = 2
BRANCH = 512
N_BRANCH = 3
S5_GROUP = 16
S5_GROUPS = 32
S5_STATE = 64
S5_LANES = S5_GROUPS * S5_STATE
S5_EIG_MAX = -1e-4
HG_HEADS = 4
HG_DK = 128
HG_CHUNK = 32
RG_BLOCKS = 8
RG_BLOCK = 64
RG_C = 8.0
D_FF = 2816
EPS = 1e-6
IN_TOTAL = 6656
GM_WIDTH = N_BRANCH * D_MODEL
N_DEV = 8

ADAM_LR = 0.001
ADAM_B1 = 0.9
ADAM_B2 = 0.999
ADAM_EPS = 1e-08
ADAM_WD = 0.01
ADAM_STEP = 10

VMEM_LIMIT_V7X = 56 * 1024 * 1024
ROW_TILE = 256
FF_TILE = 1408
TOKEN_K = 2048
MXU_COLS = 256


def _cparams(sem):
    return pltpu.CompilerParams(dimension_semantics=sem, vmem_limit_bytes=VMEM_LIMIT_V7X)


MESH_IDS = pl.DeviceIdType.MESH
ANY_SPEC = pl.BlockSpec(memory_space=pl.ANY)


def _place():
    return lax.axis_index("x"), lax.axis_index("y"), lax.axis_index("c")


class _Carry:
    def __init__(self, ins, out_shapes, n_sems, copies):
        self.ins, self.out_shapes, self.n_sems, self.copies = list(ins), list(out_shapes), n_sems, copies
        self.outs = None

    def sems(self):
        return [pltpu.SemaphoreType.DMA((self.n_sems,)), pltpu.SemaphoreType.DMA((self.n_sems,))]

    def start(self, when, *riders):
        @pl.when(when)
        def _():
            for cp in self.copies(*riders):
                cp.start()

    def finish(self, when, *riders):
        @pl.when(when)
        def _():
            for cp in self.copies(*riders):
                cp.wait()


_CARRIED = {}


def _call_with_rider(name, body, grid, in_specs, out_specs, out_shape, scratch, semantics, args):
    carry = _CARRIED.pop(name, None)
    if carry is None:
        return pl.pallas_call(body, name=name, grid=grid, in_specs=in_specs, out_specs=out_specs,
                              out_shape=out_shape, scratch_shapes=scratch, compiler_params=_cparams(semantics))(*args)
    n_in, n_out, nci, nco = len(in_specs), len(out_specs), len(carry.ins), len(carry.out_shapes)

    def kern(*refs):
        ids = [pl.program_id(d) for d in range(len(grid))]
        own = refs[:n_in] + refs[n_in + nci:n_in + nci + n_out] + refs[n_in + nci + n_out + nco:-2]
        riders = (refs[n_in:n_in + nci], refs[n_in + nci + n_out:n_in + nci + n_out + nco]) + tuple(refs[-2:])
        carry.start(functools.reduce(jnp.logical_and, [p == 0 for p in ids]), *riders)
        body(*own)
        carry.finish(functools.reduce(jnp.logical_and, [p == g - 1 for p, g in zip(ids, grid)]), *riders)

    res = pl.pallas_call(
        kern, name=name, grid=grid, in_specs=list(in_specs) + [ANY_SPEC] * nci,
        out_specs=list(out_specs) + [ANY_SPEC] * nco, out_shape=list(out_shape) + carry.out_shapes,
        scratch_shapes=list(scratch) + carry.sems(), compiler_params=_cparams(("arbitrary",) * len(grid)),
    )(*args, *carry.ins)
    carry.outs = res[n_out:]
    return res[:n_out]


def _sigmoid(x):
    return 0.5 * jnp.tanh(0.5 * x) + 0.5


def _sigmoid_small(x):
    return 1.0 / (1.0 + jnp.exp(-x))


_GELU_C = math.sqrt(2.0 / math.pi)


def _gelu(x):
    t = jnp.tanh(_GELU_C * (x + 0.044715 * x * x * x))
    return 0.5 * x * (1.0 + t)


def _gelu_grad(x):
    t = jnp.tanh(_GELU_C * (x + 0.044715 * x * x * x))
    return 0.5 * (1.0 + t) + 0.5 * x * (1.0 - t * t) * _GELU_C * (1.0 + 3.0 * 0.044715 * x * x)


def _expm1(x):
    p = x * (1.0 + x * (0.5 + x * (1.0 / 6 + x * (1.0 / 24 + x * (1.0 / 120 + x * (1.0 / 720))))))
    return jnp.where(jnp.abs(x) < 0.3, p, jnp.exp(x) - 1.0)


def _dot(a, b):
    return jnp.dot(a, b, preferred_element_type=f32)


def _dot_nt(a, b):
    return lax.dot_general(a, b, (((1,), (1,)), ((), ())), preferred_element_type=f32)


def _dot_tn(a, b):
    return lax.dot_general(a, b, (((0,), (0,)), ((), ())), preferred_element_type=f32)


def _bdot(a, b):
    return lax.dot_general(a, b, (((2,), (1,)), ((0,), (0,))), preferred_element_type=f32)


def _bdot_nt(a, b):
    return lax.dot_general(a, b, (((2,), (2,)), ((0,), (0,))), preferred_element_type=f32)


def _rows(shape):
    return lax.broadcasted_iota(jnp.int32, shape, 0)


def _scan_fwd(a, b, n):
    row = _rows(a.shape)
    s = 1
    while s < n:
        valid = row >= s
        sh_a = pltpu.roll(a, s, 0)
        sh_b = pltpu.roll(b, s, 0)
        b = b + a * jnp.where(valid, sh_b, 0.0)
        a = a * jnp.where(valid, sh_a, 1.0)
        s *= 2
    return a, b


def _scan_bwd(a, b, n):
    row = _rows(a.shape)
    s = 1
    while s < n:
        valid = row < n - s
        sh_a = pltpu.roll(a, n - s, 0)
        sh_b = pltpu.roll(b, n - s, 0)
        b = b + a * jnp.where(valid, sh_b, 0.0)
        a = a * jnp.where(valid, sh_a, 1.0)
        s *= 2
    return a, b


def _seg_cumsum(x, n, seg):
    pos = _rows(x.shape) % seg
    s = 1
    while s < seg:
        x = x + jnp.where(pos >= s, pltpu.roll(x, s, 0), 0.0)
        s *= 2
    return x


def _seg_rev_cumsum(x, n, seg):
    pos = _rows(x.shape) % seg
    s = 1
    while s < seg:
        x = x + jnp.where(pos < seg - s, pltpu.roll(x, n - s, 0), 0.0)
        s *= 2
    return x


def _head_mean(x):
    parts = []
    for h in range(HG_HEADS):
        m = jnp.mean(x[:, h * HG_DK:(h + 1) * HG_DK], axis=1, keepdims=True)
        parts.append(jnp.broadcast_to(m, (x.shape[0], HG_DK)))
    return jnp.concatenate(parts, axis=1)


def _mm(name, a_list, b_list, terms, n_acc, mode, m, n, k, tm, tn, tk, out_dtypes, epilogue, extras=(), vecs=(),
        n_part=0, chunk=0):
    tm, tn, tk = min(tm, m), min(tn, n), min(tk, k)
    assert m % tm == 0 and n % tn == 0 and k % tk == 0, (name, m, n, k, tm, tn, tk)
    gk = k // tk
    if mode == "tn":
        a_spec = pl.BlockSpec((tk, tm), lambda i, j, kk: (kk, i))
    else:
        a_spec = pl.BlockSpec((tm, tk), lambda i, j, kk: (i, kk))
    if mode == "nt":
        b_spec = pl.BlockSpec((tn, tk), lambda i, j, kk: (j, kk))
    else:
        b_spec = pl.BlockSpec((tk, tn), lambda i, j, kk: (kk, j))
    o_spec = pl.BlockSpec((tm, tn), lambda i, j, kk: (i, j))
    v_spec = pl.BlockSpec((1, tn), lambda i, j, kk: (0, j))
    p_spec = pl.BlockSpec((None, 1, tn), lambda i, j, kk: (i, 0, j))
    dot = {"nn": _dot, "nt": _dot_nt, "tn": _dot_tn}[mode]
    na, nb, ne, nv, no = len(a_list), len(b_list), len(extras), len(vecs), len(out_dtypes)
    carry = _CARRIED.pop(name, None)
    nci, nco = (len(carry.ins), len(carry.out_shapes)) if carry else (0, 0)
    n_in = na + nb + ne + nv + nci
    grid = (m // tm, n // tn, gk)

    def kern(*refs):
        if carry:
            ids = [pl.program_id(d) for d in range(3)]
            riders = (refs[n_in - nci:n_in], refs[n_in + no + n_part:n_in + no + n_part + nco]) + tuple(refs[-2:])
            carry.start(functools.reduce(jnp.logical_and, [p == 0 for p in ids]), *riders)
        compute(*refs)
        if carry:
            carry.finish(functools.reduce(jnp.logical_and, [p == g - 1 for p, g in zip(ids, grid)]), *riders)

    def compute(*refs):
        a_refs = refs[:na]
        b_refs = refs[na:na + nb]
        e_refs = refs[na + nb:na + nb + ne]
        v_refs = refs[na + nb + ne:na + nb + ne + nv]
        o_refs = refs[n_in:n_in + no + n_part]

        def finish(accs):
            outs = epilogue(accs, [e[...] for e in e_refs], [r[...] for r in v_refs])
            for o, val in zip(o_refs, outs):
                o[...] = val.astype(o.dtype)

        def partial_sums():
            sums = [None] * n_acc
            for ai, bi, ci in terms:
                d = dot(a_refs[ai][...].astype(bf16), b_refs[bi][...].astype(bf16))
                sums[ci] = d if sums[ci] is None else sums[ci] + d
            return sums

        if gk == 1 and chunk:
            assert mode in ("nn", "nt") and tn % chunk == 0
            for c0 in range(0, tn, chunk):
                cols = slice(c0, c0 + chunk)
                sums = [None] * n_acc
                for ai, bi, ci in terms:
                    b_part = b_refs[bi][:, cols] if mode == "nn" else b_refs[bi][cols, :]
                    d = dot(a_refs[ai][...].astype(bf16), b_part.astype(bf16))
                    sums[ci] = d if sums[ci] is None else sums[ci] + d
                outs = epilogue(sums, [e[:, cols] for e in e_refs], [r[:, cols] for r in v_refs])
                for o, val in zip(o_refs, outs):
                    o[:, cols] = val.astype(o.dtype)
            return
        if gk == 1:
            finish(partial_sums())
            return
        acc = refs[n_in + no + n_part + nco]
        kk = pl.program_id(2)

        @pl.when(kk == 0)
        def _():
            acc[...] = jnp.zeros_like(acc)

        for ci, d in enumerate(partial_sums()):
            acc[ci] += d

        @pl.when(kk == gk - 1)
        def _():
            finish([acc[c] for c in range(n_acc)])

    res = pl.pallas_call(
        kern, name=name,
        grid=grid,
        in_specs=[a_spec] * na + [b_spec] * nb + [o_spec] * ne + [v_spec] * nv + [ANY_SPEC] * nci,
        out_specs=[o_spec] * no + [p_spec] * n_part + [ANY_SPEC] * nco,
        out_shape=([jax.ShapeDtypeStruct((m, n), dt) for dt in out_dtypes]
                   + [jax.ShapeDtypeStruct((m // tm, 1, n), f32)] * n_part + (carry.out_shapes if carry else [])),
        scratch_shapes=([pltpu.VMEM((n_acc, tm, tn), f32)] if gk > 1 else []) + (carry.sems() if carry else []),
        compiler_params=_cparams(("arbitrary",) * 3 if carry else ("parallel", "parallel", "arbitrary")),
    )(*a_list, *b_list, *extras, *vecs, *(carry.ins if carry else []))
    if carry:
        carry.outs = res[no + n_part:]
        res = res[:no + n_part]
    return res


def _mm1(name, a, b, mode, m, n, k, tm, tn, tk, out_dtype=f32, scale=None):
    def epi(accs, extras, vecs):
        return [accs[0] if scale is None else accs[0] * scale]
    return _mm(name, [a], [b], [(0, 0, 0)], 1, mode, m, n, k, tm, tn, tk, [out_dtype], epi)[0]


def _rt(name, body, rows, tm, row_ins, consts, row_outs, acc_outs=(), scratch=(), reverse=False):
    tm = min(tm, rows)
    assert rows % tm == 0
    nt = rows // tm

    def tile(i):
        return nt - 1 - i if reverse else i

    in_specs, args = [], []
    for spec in row_ins:
        arr = spec[0]
        if isinstance(spec[1], int):
            in_specs.append(pl.BlockSpec((tm, spec[1]), lambda i, cb=spec[2]: (tile(i), cb)))
        else:
            in_specs.append(pl.BlockSpec(spec[1], lambda i, fn=spec[2]: fn(tile(i))))
        args.append(arr)
    for c in consts:
        in_specs.append(pl.BlockSpec(c.shape, lambda i, nd=c.ndim: (0,) * nd))
        args.append(c)
    out_specs, out_shape = [], []
    for spec in row_outs:
        if isinstance(spec[0], int):
            out_specs.append(pl.BlockSpec((tm, spec[0]), lambda i: (tile(i), 0)))
            out_shape.append(jax.ShapeDtypeStruct((rows, spec[0]), spec[1]))
        else:
            out_specs.append(pl.BlockSpec(spec[1], lambda i, fn=spec[2]: fn(tile(i))))
            out_shape.append(jax.ShapeDtypeStruct(spec[0], spec[3]))
    for shp in acc_outs:
        out_specs.append(pl.BlockSpec(shp, lambda i, nd=len(shp): (0,) * nd))
        out_shape.append(jax.ShapeDtypeStruct(shp, f32))
    n_in = len(args)
    n_row_out = len(row_outs)
    n_acc = len(acc_outs)
    n_out = n_row_out + n_acc
    carry = _CARRIED.pop(name, None)
    nci, nco = (len(carry.ins), len(carry.out_shapes)) if carry else (0, 0)

    def kern(*refs):
        i = pl.program_id(0)
        if carry:
            own = refs[:n_in] + refs[n_in + nci:n_in + nci + n_out] + refs[n_in + nci + n_out + nco:-2]
            riders = (refs[n_in:n_in + nci], refs[n_in + nci + n_out:n_in + nci + n_out + nco]) + tuple(refs[-2:])
            carry.start(i == 0, *riders)
        else:
            own = refs
        acc_refs = own[n_in + n_row_out:n_in + n_out]

        @pl.when(i == 0)
        def _():
            for r in acc_refs:
                r[...] = jnp.zeros_like(r)

        body(i, *own)
        if carry:
            carry.finish(i == nt - 1, *riders)

    res = pl.pallas_call(
        kern, name=name, grid=(nt,), in_specs=in_specs + [ANY_SPEC] * nci, out_specs=out_specs + [ANY_SPEC] * nco,
        out_shape=out_shape + (carry.out_shapes if carry else []),
        scratch_shapes=list(scratch) + (carry.sems() if carry else []), compiler_params=_cparams(("arbitrary",)),
    )(*args, *(carry.ins if carry else []))
    if carry:
        carry.outs = res[n_out:]
        res = res[:n_out]
    return res


def _rms_rows(xv, wv):
    r = lax.rsqrt(jnp.mean(xv * xv, axis=1, keepdims=True) + EPS)
    return (xv * r * wv).astype(bf16)


def _rms_bwd_rows(xv, dhv, wv, dres):
    r = lax.rsqrt(jnp.mean(xv * xv, axis=1, keepdims=True) + EPS)
    xn = xv * r
    dxn = dhv * wv
    dx = dres + r * (dxn - xn * jnp.mean(dxn * xn, axis=1, keepdims=True))
    return [dx, dx.astype(bf16), jnp.sum(dhv * xn, axis=0, keepdims=True)]


def _rms_fwd(name, x, w, rows):
    def body(i, x_ref, w_ref, h_ref):
        h_ref[...] = _rms_rows(x_ref[...], w_ref[...])
    return _rt(name, body, rows, ROW_TILE, [(x, D_MODEL, 0)], [w], [(D_MODEL, bf16)])[0]


def _residual_then_norm(scale):
    def epi(accs, extras, vecs):
        x_out = extras[0] + scale * accs[0]
        return [x_out] + [_rms_rows(x_out, v) for v in vecs]
    return epi


def _norm_bwd_epilogue(accs, extras, vecs):
    return _rms_bwd_rows(extras[0], accs[0], vecs[0], extras[1])


def _loss_head(x, w, target, rows):
    def body(i, x_ref, t_ref, w_ref, dx_ref, dxb_ref, loss_ref, dw_ref):
        xv = x_ref[...]
        r = lax.rsqrt(jnp.mean(xv * xv, axis=1, keepdims=True) + EPS)
        xn = xv * r
        wv = w_ref[...]
        err = xn * wv - t_ref[...]
        part = 0.5 * jnp.sum(jnp.mean(err * err, axis=1, keepdims=True), axis=0, keepdims=True)
        loss_ref[...] += jnp.broadcast_to(part, (1, 128))
        dy = err * (1.0 / D_MODEL)
        dxn = dy * wv
        dx = r * (dxn - xn * jnp.mean(dxn * xn, axis=1, keepdims=True))
        dx_ref[...] = dx
        dxb_ref[...] = dx.astype(bf16)
        dw_ref[...] += jnp.sum(dy * xn, axis=0, keepdims=True)
    return _rt("loss_head", body, rows, ROW_TILE, [(x, D_MODEL, 0), (target, D_MODEL, 0)], [w],
               [(D_MODEL, f32), (D_MODEL, bf16)], acc_outs=[(1, 128), (1, D_MODEL)])


def _ffn_fwd(tag, x, hb, wg, wu, wd, next_nw, rows):
    def epi_up(accs, extras, vecs):
        a, b = accs
        return [a, b, a * _sigmoid(a) * b]
    a, b, s = _mm("ffn_up_" + tag, [hb], [wg, wu], [(0, 0, 0), (0, 1, 1)], 2, "nn", rows, D_FF, D_MODEL,
                  512, D_FF, D_MODEL, [bf16, bf16, bf16], epi_up, chunk=MXU_COLS)
    outs = _mm("ffn_down_" + tag, [s], [wd], [(0, 0, 0)], 1, "nn", rows, D_MODEL, D_FF,
               512, D_MODEL, D_FF, [f32] + [bf16] * len(next_nw), _residual_then_norm(0.5), extras=[x],
               vecs=next_nw)
    return outs[0], (outs[1] if next_nw else None), (x, hb, a, b, s)


def _ffn_bwd(tag, saved, nw, wg, wu, wd, dx, dxb, rows):
    x, hb, a, b, s = saved

    def epi_mid(accs, extras, vecs):
        ds = 0.5 * accs[0]
        av = extras[0].astype(f32)
        bv = extras[1].astype(f32)
        sg = _sigmoid(av)
        return [ds * bv * sg * (1.0 + av * (1.0 - sg)), ds * av * sg]
    da, db = _mm("ffn_bwd_mid_" + tag, [dxb], [wd], [(0, 0, 0)], 1, "nt", rows, D_FF, D_MODEL,
                 512, D_FF, D_MODEL, [bf16, bf16], epi_mid, extras=[a, b], chunk=MXU_COLS)
    d_wd = _mm1("ffn_dwd_" + tag, s, dxb, "tn", D_FF, D_MODEL, rows, FF_TILE, D_MODEL, TOKEN_K, out_dtype=bf16,
                scale=0.5)
    d_wg = _mm1("ffn_dwg_" + tag, hb, da, "tn", D_MODEL, D_FF, rows, D_MODEL, FF_TILE, TOKEN_K, out_dtype=bf16)
    d_wu = _mm1("ffn_dwu_" + tag, hb, db, "tn", D_MODEL, D_FF, rows, D_MODEL, FF_TILE, TOKEN_K, out_dtype=bf16)
    dx_in, dxb_in, d_nw = _mm("ffn_dh_" + tag, [da, db], [wg, wu], [(0, 0, 0), (1, 1, 0)], 1, "nt", rows, D_MODEL,
                              D_FF, 512, D_MODEL, D_FF, [f32, bf16], _norm_bwd_epilogue, extras=[x, dx], vecs=[nw],
                              n_part=1)
    return dx_in, dxb_in, jnp.sum(d_nw, axis=0), d_wg, d_wu, d_wd


S5_CB = 512
SUBLANES = 8
U_COL = GM_WIDTH // BRANCH


def _s5_scan_fwd(tag, proj, b_re, b_im, a_re, a_im, rows):
    tm = min(ROW_TILE, rows)
    nt = rows // tm
    nc = S5_LANES // S5_CB

    def kern(u_ref, bre_ref, bim_ref, ar_ref, ai_ref, xr_ref, xi_ref, pr_s, pi_s, cr_s, ci_s, mr_s, mi_s):
        t = pl.program_id(1)

        @pl.when(t == 0)
        def _():
            row8 = _rows((SUBLANES, S5_CB))
            pr = jnp.broadcast_to(ar_ref[...], (SUBLANES, S5_CB))
            pi = jnp.broadcast_to(ai_ref[...], (SUBLANES, S5_CB))
            s = 1
            while s < SUBLANES:
                sr = pltpu.roll(pr, s, 0)
                si = pltpu.roll(pi, s, 0)
                valid = row8 >= s
                pr, pi = jnp.where(valid, pr * sr - pi * si, pr), jnp.where(valid, pr * si + pi * sr, pi)
                s *= 2
            pr_s[...] = pr
            pi_s[...] = pi
            for k in range(3):
                s = 1 << k
                mr_s[k] = jnp.where(row8 >= s, pr[s - 1:s, :], 0.0)
                mi_s[k] = jnp.where(row8 >= s, pi[s - 1:s, :], 0.0)
            cr_s[...] = jnp.zeros_like(cr_s)
            ci_s[...] = jnp.zeros_like(ci_s)

        ub = u_ref[...].astype(bf16)
        br = _dot(ub, bre_ref[...])
        bi = _dot(ub, bim_ref[...])
        steps = [(mr_s[k], mi_s[k]) for k in range(3)]
        cr = cr_s[...]
        ci = ci_s[...]
        pr = pr_s[...]
        pi = pi_s[...]
        for g in range(tm // SUBLANES):
            sl = slice(g * SUBLANES, (g + 1) * SUBLANES)
            xr = br[sl]
            xi = bi[sl]
            for k, (mr, mi) in enumerate(steps):
                sr = pltpu.roll(xr, 1 << k, 0)
                si = pltpu.roll(xi, 1 << k, 0)
                xr, xi = xr + (mr * sr - mi * si), xi + (mr * si + mi * sr)
            xr, xi = xr + (pr * cr - pi * ci), xi + (pr * ci + pi * cr)
            xr_ref[sl, :] = xr
            xi_ref[sl, :] = xi
            cr = xr[SUBLANES - 1:SUBLANES, :]
            ci = xi[SUBLANES - 1:SUBLANES, :]
        cr_s[...] = cr
        ci_s[...] = ci

    return _call_with_rider(
        "s5_scan_fwd_" + tag, kern, (nc, nt),
        [pl.BlockSpec((tm, BRANCH), lambda c, t: (t, U_COL)),
         pl.BlockSpec((BRANCH, S5_CB), lambda c, t: (0, c)),
         pl.BlockSpec((BRANCH, S5_CB), lambda c, t: (0, c)),
         pl.BlockSpec((1, S5_CB), lambda c, t: (0, c)),
         pl.BlockSpec((1, S5_CB), lambda c, t: (0, c))],
        [pl.BlockSpec((tm, S5_CB), lambda c, t: (t, c))] * 2,
        [jax.ShapeDtypeStruct((rows, S5_LANES), f32)] * 2,
        [pltpu.VMEM((SUBLANES, S5_CB), f32), pltpu.VMEM((SUBLANES, S5_CB), f32),
         pltpu.VMEM((1, S5_CB), f32), pltpu.VMEM((1, S5_CB), f32),
         pltpu.VMEM((3, SUBLANES, S5_CB), f32), pltpu.VMEM((3, SUBLANES, S5_CB), f32)],
        ("parallel", "arbitrary"), (proj, b_re, b_im, a_re, a_im))


def _s5_scan_bwd(tag, dxr, dxi, xr, xi, a_re, a_im, rows):
    tm = min(ROW_TILE, rows)
    nt = rows // tm
    nc = S5_LANES // S5_CB

    def kern(dxr_ref, dxi_ref, xr_ref, xi_ref, ar_ref, ai_ref, gr_ref, gi_ref, dar_ref, dai_ref,
             qr_s, qi_s, cr_s, ci_s, gr_s, gi_s, mr_s, mi_s):
        t = pl.program_id(1)
        row = _rows((tm, S5_CB))
        ng = tm // SUBLANES

        @pl.when(t == 0)
        def _():
            row8 = _rows((SUBLANES, S5_CB))
            qr = jnp.broadcast_to(ar_ref[...], (SUBLANES, S5_CB))
            qi = jnp.broadcast_to(-ai_ref[...], (SUBLANES, S5_CB))
            s = 1
            while s < SUBLANES:
                sr = pltpu.roll(qr, SUBLANES - s, 0)
                si = pltpu.roll(qi, SUBLANES - s, 0)
                valid = row8 < SUBLANES - s
                qr, qi = jnp.where(valid, qr * sr - qi * si, qr), jnp.where(valid, qr * si + qi * sr, qi)
                s *= 2
            qr_s[...] = qr
            qi_s[...] = qi
            for k in range(3):
                s = 1 << k
                mr_s[k] = jnp.where(row8 < SUBLANES - s, qr[SUBLANES - s:SUBLANES - s + 1, :], 0.0)
                mi_s[k] = jnp.where(row8 < SUBLANES - s, qi[SUBLANES - s:SUBLANES - s + 1, :], 0.0)
            cr_s[...] = jnp.zeros_like(cr_s)
            ci_s[...] = jnp.zeros_like(ci_s)
            dar_ref[...] = jnp.zeros_like(dar_ref)
            dai_ref[...] = jnp.zeros_like(dai_ref)

        steps = [(mr_s[k], mi_s[k]) for k in range(3)]
        cr = cr_s[...]
        ci = ci_s[...]
        qr = qr_s[...]
        qi = qi_s[...]
        last8 = _rows((SUBLANES, S5_CB)) == SUBLANES - 1
        acc_r = jnp.zeros((SUBLANES, S5_CB), f32)
        acc_i = jnp.zeros((SUBLANES, S5_CB), f32)
        for g in reversed(range(ng)):
            sl = slice(g * SUBLANES, (g + 1) * SUBLANES)
            gr = dxr_ref[sl, :]
            gi = dxi_ref[sl, :]
            for k, (mr, mi) in enumerate(steps):
                sr = pltpu.roll(gr, SUBLANES - (1 << k), 0)
                si = pltpu.roll(gi, SUBLANES - (1 << k), 0)
                gr, gi = gr + (mr * sr - mi * si), gi + (mr * si + mi * sr)
            gr, gi = gr + (qr * cr - qi * ci), gi + (qr * ci + qi * cr)
            gr_s[sl, :] = gr
            gi_s[sl, :] = gi
            gnr = jnp.where(last8, cr, pltpu.roll(gr, SUBLANES - 1, 0))
            gni = jnp.where(last8, ci, pltpu.roll(gi, SUBLANES - 1, 0))
            xr_v = xr_ref[sl, :]
            xi_v = xi_ref[sl, :]
            acc_r = acc_r + (gnr * xr_v + gni * xi_v)
            acc_i = acc_i + (gni * xr_v - gnr * xi_v)
            cr = gr[0:1, :]
            ci = gi[0:1, :]
        cr_s[...] = cr
        ci_s[...] = ci
        gr_ref[...] = gr_s[...].astype(bf16)
        gi_ref[...] = gi_s[...].astype(bf16)
        dar_ref[...] += jnp.sum(acc_r, axis=0, keepdims=True)
        dai_ref[...] += jnp.sum(acc_i, axis=0, keepdims=True)

    rev = lambda c, t: (nt - 1 - t, c)
    return _call_with_rider(
        "s5_scan_bwd_" + tag, kern, (nc, nt),
        [pl.BlockSpec((tm, S5_CB), rev)] * 4 + [pl.BlockSpec((1, S5_CB), lambda c, t: (0, c))] * 2,
        [pl.BlockSpec((tm, S5_CB), rev)] * 2 + [pl.BlockSpec((1, S5_CB), lambda c, t: (0, c))] * 2,
        [jax.ShapeDtypeStruct((rows, S5_LANES), bf16)] * 2 + [jax.ShapeDtypeStruct((1, S5_LANES), f32)] * 2,
        [pltpu.VMEM((SUBLANES, S5_CB), f32), pltpu.VMEM((SUBLANES, S5_CB), f32),
         pltpu.VMEM((1, S5_CB), f32), pltpu.VMEM((1, S5_CB), f32),
         pltpu.VMEM((tm, S5_CB), f32), pltpu.VMEM((tm, S5_CB), f32),
         pltpu.VMEM((3, SUBLANES, S5_CB), f32), pltpu.VMEM((3, SUBLANES, S5_CB), f32)],
        ("parallel", "arbitrary"), (dxr, dxi, xr, xi, a_re, a_im))


def _s5_fwd(tag, proj, cst, rows):
    xr, xi = _s5_scan_fwd(tag, proj, cst["b_re"].astype(bf16), cst["b_im"].astype(bf16), cst["a_re"], cst["a_im"], rows)

    def body(i, xr_ref, xi_ref, u_ref, cre_ref, cim_ref, d_ref, gw_ref, gb_ref, y_ref, out_ref):
        y = (_dot(xr_ref[...].astype(bf16), cre_ref[...]) + _dot(xi_ref[...].astype(bf16), cim_ref[...])
             + d_ref[...] * u_ref[...])
        y_ref[...] = y
        z = _gelu(y)
        zg = _dot(z.astype(bf16), gw_ref[...]) + gb_ref[...]
        out_ref[...] = (z * _sigmoid(zg)).astype(bf16)

    y, out = _rt("s5_out_" + tag, body, rows, ROW_TILE,
                 [(xr, S5_LANES, 0), (xi, S5_LANES, 0), (proj, BRANCH, U_COL)],
                 [cst["c_re"].astype(bf16), cst["c_im"].astype(bf16), cst["s5_d"], cst["glu_w"], cst["glu_b"]],
                 [(BRANCH, f32), (BRANCH, bf16)])
    return out, (xr, xi, y)


def _s5_bwd(tag, saved, proj, cst, d_out, rows):
    xr, xi, y = saved
    c_re = cst["c_re"].astype(bf16)
    c_im = cst["c_im"].astype(bf16)

    def body(i, do_ref, y_ref, u_ref, xr_ref, xi_ref, cre_ref, cim_ref, gw_ref, gb_ref,
             dxr_ref, dxi_ref, dy_ref, dgw_ref, dgb_ref, dd_ref, dcre_ref, dcim_ref):
        yv = y_ref[...]
        z = _gelu(yv)
        zb = z.astype(bf16)
        gt = _sigmoid(_dot(zb, gw_ref[...]) + gb_ref[...])
        dov = do_ref[...]
        dzg = dov * z * gt * (1.0 - gt)
        dzgb = dzg.astype(bf16)
        dz = dov * gt + _dot_nt(dzgb, gw_ref[...])
        dgw_ref[...] += _dot_tn(zb, dzgb)
        dgb_ref[...] += jnp.sum(dzg, axis=0, keepdims=True)
        dy = dz * _gelu_grad(yv)
        dy_ref[...] = dy
        dd_ref[...] += jnp.sum(dy * u_ref[...], axis=0, keepdims=True)
        dyb = dy.astype(bf16)
        dxr_ref[...] = _dot_nt(dyb, cre_ref[...])
        dxi_ref[...] = _dot_nt(dyb, cim_ref[...])
        dcre_ref[...] += _dot_tn(xr_ref[...].astype(bf16), dyb)
        dcim_ref[...] += _dot_tn(xi_ref[...].astype(bf16), dyb)

    dxr, dxi, dy, d_gw, d_gb, d_d, d_cre, d_cim = _rt(
        "s5_out_bwd_" + tag, body, rows, ROW_TILE,
        [(d_out, BRANCH, 0), (y, BRANCH, 0), (proj, BRANCH, U_COL), (xr, S5_LANES, 0), (xi, S5_LANES, 0)],
        [c_re, c_im, cst["glu_w"], cst["glu_b"]],
        [(S5_LANES, f32), (S5_LANES, f32), (BRANCH, f32)],
        acc_outs=[(BRANCH, BRANCH), (1, BRANCH), (1, BRANCH), (S5_LANES, BRANCH), (S5_LANES, BRANCH)])

    gr, gi, d_ar, d_ai = _s5_scan_bwd(tag, dxr, dxi, xr, xi, cst["a_re"], cst["a_im"], rows)
    b_re = cst["b_re"].astype(bf16)
    b_im = cst["b_im"].astype(bf16)

    def body_in(i, gr_ref, gi_ref, dy_ref, u_ref, bre_ref, bim_ref, d_ref, du_ref, dbre_ref, dbim_ref):
        grv = gr_ref[...]
        giv = gi_ref[...]
        du = _dot_nt(grv, bre_ref[...]) + _dot_nt(giv, bim_ref[...]) + dy_ref[...] * d_ref[...]
        du_ref[...] = du.astype(bf16)
        ub = u_ref[...].astype(bf16)
        dbre_ref[...] += _dot_tn(ub, grv)
        dbim_ref[...] += _dot_tn(ub, giv)

    du, d_bre, d_bim = _rt("s5_in_bwd_" + tag, body_in, rows, ROW_TILE,
                           [(gr, S5_LANES, 0), (gi, S5_LANES, 0), (dy, BRANCH, 0), (proj, BRANCH, U_COL)],
                           [b_re, b_im, cst["s5_d"]], [(BRANCH, bf16)],
                           acc_outs=[(BRANCH, S5_LANES), (BRANCH, S5_LANES)])
    dcst = {"b_re": d_bre, "b_im": d_bim, "a_re": d_ar, "a_im": d_ai, "c_re": d_cre, "c_im": d_cim,
            "s5_d": d_d, "glu_b": d_gb}
    return du, dcst, d_gw


def _hg_prep(q, z, lb):
    qs = _sigmoid(q)
    qh = q * qs
    sg = _sigmoid_small(z)
    fg = lb + (1.0 - lb) * sg
    kk = (1.0 - lb) * (1.0 - sg)
    return qs, qh, sg, fg, kk


def _hg_fwd(tag, proj, cst, rows):
    tm = min(ROW_TILE, rows)
    c_sz = HG_CHUNK
    nch = tm // c_sz
    n_chunks = rows // c_sz

    def body(i, q_ref, z_ref, v_ref, g_ref, lb_ref, nw_ref, out_ref, o_ref, ss_ref, sn_ref, st_s):
        @pl.when(i == 0)
        def _():
            st_s[...] = jnp.zeros_like(st_s)

        lb = lb_ref[...]
        _, qh, sg, fg, kk = _hg_prep(q_ref[...], z_ref[...], lb)
        b = _seg_cumsum(jnp.log(fg), tm, c_sz)
        qhat = (qh * jnp.exp(b)).astype(bf16)
        khat = (kk * jnp.exp(-b)).astype(bf16)
        vb = v_ref[...].astype(bf16)
        b3 = b.reshape(nch, c_sz, BRANCH)
        bl3 = b3[:, c_sz - 1:c_sz, :]
        kdec = (kk.reshape(nch, c_sz, BRANCH) * jnp.exp(bl3 - b3)).astype(bf16)
        ebl = jnp.exp(bl3)
        tril = (lax.broadcasted_iota(jnp.int32, (nch, c_sz, c_sz), 1)
                >= lax.broadcasted_iota(jnp.int32, (nch, c_sz, c_sz), 2))
        o_heads = []
        for h in range(HG_HEADS):
            hl = slice(h * HG_DK, (h + 1) * HG_DK)
            q3 = qhat[:, hl].reshape(nch, c_sz, HG_DK)
            k3 = khat[:, hl].reshape(nch, c_sz, HG_DK)
            v3 = vb[:, hl].reshape(nch, c_sz, HG_DK)
            a_mat = jnp.where(tril, _bdot_nt(q3, k3), 0.0).astype(bf16)
            o3 = _bdot(a_mat, v3)
            st = st_s[hl, :]
            before = []
            for ci in range(nch):
                before.append(st.astype(bf16))
                st = st * ebl[ci][:, hl] + _dot_tn(v3[ci], kdec[ci][:, hl])
                sn_ref[ci, hl, :] = st.astype(bf16)
            st_s[hl, :] = st
            s3 = jnp.stack(before)
            ss_ref[:, hl, :] = s3
            o3 = o3 + _bdot_nt(q3, s3)
            o_heads.append(o3.reshape(tm, HG_DK))
        o = jnp.concatenate(o_heads, axis=1)
        o_ref[...] = o
        r = lax.rsqrt(_head_mean(o * o) + EPS)
        g = g_ref[...]
        out_ref[...] = (o * r * nw_ref[...] * (g * _sigmoid(g))).astype(bf16)

    out, o, ss, sn = _rt(
        "hg_fwd_" + tag, body, rows, tm,
        [(proj, BRANCH, U_COL + 1), (proj, BRANCH, U_COL + 2), (proj, BRANCH, U_COL + 3), (proj, BRANCH, U_COL + 4)],
        [cst["hg_lb"], cst["hg_nw"]],
        [(BRANCH, bf16), (BRANCH, f32),
         ((n_chunks, BRANCH, HG_DK), (nch, BRANCH, HG_DK), lambda t: (t, 0, 0), bf16),
         ((n_chunks, BRANCH, HG_DK), (nch, BRANCH, HG_DK), lambda t: (t, 0, 0), bf16)],
        scratch=[pltpu.VMEM((BRANCH, HG_DK), f32)])
    return out, (o, ss, sn)


def _hg_bwd(tag, saved, proj, cst, d_out, rows):
    o_saved, ss, sn = saved
    tm = min(ROW_TILE, rows)
    c_sz = HG_CHUNK
    nch = tm // c_sz

    def body(i, do_ref, q_ref, z_ref, v_ref, g_ref, o_ref, ss_ref, sn_ref, lb_ref, nw_ref,
             dq_ref, dz_ref, dv_ref, dg_ref, dlb_ref, dnw_ref, dst_s):
        @pl.when(i == 0)
        def _():
            dst_s[...] = jnp.zeros_like(dst_s)

        lb = lb_ref[...]
        q = q_ref[...]
        qs, qh, sg, fg, kk = _hg_prep(q, z_ref[...], lb)
        b = _seg_cumsum(jnp.log(fg), tm, c_sz)
        eb = jnp.exp(b)
        enb = jnp.exp(-b)
        qhat = (qh * eb).astype(bf16)
        khat = (kk * enb).astype(bf16)
        vb = v_ref[...].astype(bf16)
        b3 = b.reshape(nch, c_sz, BRANCH)
        bl3 = b3[:, c_sz - 1:c_sz, :]
        dec3 = jnp.exp(bl3 - b3)
        kdec = (kk.reshape(nch, c_sz, BRANCH) * dec3).astype(bf16)
        ebl = jnp.exp(bl3)
        g = g_ref[...]
        gs = _sigmoid(g)
        o = o_ref[...]
        r = lax.rsqrt(_head_mean(o * o) + EPS)
        oh = o * r
        nw = nw_ref[...]
        dov = do_ref[...]
        don = dov * (g * gs)
        dg_ref[...] = (dov * oh * nw * (gs * (1.0 + g * (1.0 - gs)))).astype(bf16)
        dnw_ref[...] += jnp.sum(don * oh, axis=0, keepdims=True)
        doh = don * nw
        d_o = r * (doh - oh * _head_mean(doh * oh))
        dob = d_o.astype(bf16)
        t_idx = lax.broadcasted_iota(jnp.int32, (nch, c_sz, c_sz), 1)
        s_idx = lax.broadcasted_iota(jnp.int32, (nch, c_sz, c_sz), 2)
        heads = []
        for h in range(HG_HEADS):
            hl = slice(h * HG_DK, (h + 1) * HG_DK)
            q3 = qhat[:, hl].reshape(nch, c_sz, HG_DK)
            k3 = khat[:, hl].reshape(nch, c_sz, HG_DK)
            v3 = vb[:, hl].reshape(nch, c_sz, HG_DK)
            do3 = dob[:, hl].reshape(nch, c_sz, HG_DK)
            s3 = ss_ref[:, hl, :]
            da_mat = jnp.where(t_idx >= s_idx, _bdot_nt(do3, v3), 0.0).astype(bf16)
            a_t = jnp.where(t_idx <= s_idx, _bdot_nt(k3, q3), 0.0).astype(bf16)
            da_t = jnp.where(t_idx <= s_idx, _bdot_nt(v3, do3), 0.0).astype(bf16)
            dqhat = _bdot(do3, s3) + _bdot(da_mat, k3)
            dkhat = _bdot(da_t, q3)
            dst = dst_s[hl, :]
            after = [None] * nch
            for ci in reversed(range(nch)):
                after[ci] = dst
                dst = dst * ebl[ci][:, hl] + _dot_tn(do3[ci], q3[ci])
            dst_s[hl, :] = dst
            ds3 = jnp.stack(after)
            ds3b = ds3.astype(bf16)
            dk_inter = _bdot(v3, ds3b) * dec3[:, :, hl]
            dv3 = _bdot(a_t, do3) + _bdot_nt(kdec[:, :, hl], ds3b)
            flux = jnp.sum(sn_ref[:, hl, :].astype(f32) * ds3, axis=1, keepdims=True)
            heads.append((dqhat.reshape(tm, HG_DK), dkhat.reshape(tm, HG_DK), dk_inter.reshape(tm, HG_DK),
                          dv3.reshape(tm, HG_DK), jnp.broadcast_to(flux, (nch, c_sz, HG_DK)).reshape(tm, HG_DK)))
        dqhat, dkhat, dk_inter, dv, flux = (jnp.concatenate(parts, axis=1) for parts in zip(*heads))
        dv_ref[...] = dv.astype(bf16)
        dqh = dqhat * eb
        dk = dkhat * enb + dk_inter
        db = qhat.astype(f32) * dqhat - khat.astype(f32) * dkhat - kk * dk_inter
        dlf = _seg_rev_cumsum(db, tm, c_sz) + flux
        tt = (1.0 - lb) * sg * (1.0 - sg)
        dz_ref[...] = (dlf * tt / fg - dk * tt).astype(bf16)
        dlb_ref[...] += jnp.sum(dlf * (1.0 - sg) / fg - dk * (1.0 - sg), axis=0, keepdims=True)
        dq_ref[...] = (dqh * (qs * (1.0 + q * (1.0 - qs)))).astype(bf16)

    dq, dz, dv, dg, d_lb, d_nw = _rt(
        "hg_bwd_" + tag, body, rows, tm,
        [(d_out, BRANCH, 0), (proj, BRANCH, U_COL + 1), (proj, BRANCH, U_COL + 2), (proj, BRANCH, U_COL + 3),
         (proj, BRANCH, U_COL + 4), (o_saved, BRANCH, 0), (ss, (nch, BRANCH, HG_DK), lambda t: (t, 0, 0)),
         (sn, (nch, BRANCH, HG_DK), lambda t: (t, 0, 0))],
        [cst["hg_lb"], cst["hg_nw"]],
        [(BRANCH, bf16)] * 4, acc_outs=[(1, BRANCH), (1, BRANCH)],
        scratch=[pltpu.VMEM((BRANCH, HG_DK), f32)],
        reverse=True)
    return dq, dz, dv, dg, {"hg_lb": d_lb, "hg_nw": d_nw}


def _rg_gates(xc, wa_ref, ba_ref, wx_ref, bx_ref, sp8):
    xcb = xc.astype(bf16)
    r = _sigmoid(_dot(xcb, wa_ref[...]) + ba_ref[...])
    ig = _sigmoid(_dot(xcb, wx_ref[...]) + bx_ref[...])
    la = -sp8 * r
    a = jnp.exp(la)
    mult = jnp.sqrt(-_expm1(2.0 * la))
    return xcb, r, ig, a, mult


def _rg_fwd(tag, proj, cst, rows):
    tm = min(ROW_TILE, rows)

    def body(i, xb_ref, gate_ref, cw_ref, cb_ref, wa_ref, ba_ref, wx_ref, bx_ref, sp_ref,
             out_ref, xc_ref, h_ref, hp_ref, prev_s, hc_s):
        @pl.when(i == 0)
        def _():
            prev_s[...] = jnp.zeros_like(prev_s)
            hc_s[...] = jnp.zeros_like(hc_s)

        row = _rows((tm, BRANCH))
        xb = xb_ref[...]
        prev = prev_s[...]
        xc = cb_ref[...] + cw_ref[3:4, :] * xb
        for j in range(1, 4):
            sh = jnp.where(row >= j, pltpu.roll(xb, j, 0), pltpu.roll(prev, j, 0))
            xc = xc + cw_ref[3 - j:4 - j, :] * sh
        prev_s[...] = xb
        xc_ref[...] = xc
        _, r, ig, a, mult = _rg_gates(xc, wa_ref, ba_ref, wx_ref, bx_ref, sp_ref[...])
        bb = mult * ig * xc
        hc = hc_s[...]
        row8 = _rows((SUBLANES, BRANCH))
        for g in range(tm // SUBLANES):
            sl = slice(g * SUBLANES, (g + 1) * SUBLANES)
            a_cum, h_loc = _scan_fwd(a[sl], bb[sl], SUBLANES)
            h = h_loc + a_cum * hc
            h_ref[sl, :] = h
            hp_ref[sl, :] = jnp.where(row8 >= 1, pltpu.roll(h, 1, 0), hc)
            hc = h[SUBLANES - 1:SUBLANES, :]
        hc_s[...] = hc
        out_ref[...] = (h_ref[...] * _gelu(gate_ref[...])).astype(bf16)

    out, xc, h, hp = _rt(
        "rg_fwd_" + tag, body, rows, tm,
        [(proj, BRANCH, U_COL + 5), (proj, BRANCH, U_COL + 6)],
        [cst["rg_cw"], cst["rg_cb"], cst["rg_wa"].astype(bf16), cst["rg_ba"], cst["rg_wx"].astype(bf16),
         cst["rg_bx"], cst["rg_sp8"]],
        [(BRANCH, bf16), (BRANCH, f32), (BRANCH, f32), (BRANCH, f32)],
        scratch=[pltpu.VMEM((tm, BRANCH), f32), pltpu.VMEM((1, BRANCH), f32)])
    return out, (xc, h, hp)


def _rg_bwd(tag, saved, proj, cst, d_out, rows):
    xc_saved, h_saved, hp_saved = saved
    tm = min(ROW_TILE, rows)

    def body(i, do_ref, xb_ref, gate_ref, xc_ref, h_ref, hp_ref, cw_ref, wa_ref, ba_ref, wx_ref, bx_ref, sp_ref,
             dxb_ref, dgate_ref, dcw_ref, dcb_ref, dwa_ref, dba_ref, dwx_ref, dbx_ref, dsp_ref,
             nxt_s, ec_s, gt_s):
        @pl.when(i == 0)
        def _():
            nxt_s[...] = jnp.zeros_like(nxt_s)
            ec_s[...] = jnp.zeros_like(ec_s)

        row = _rows((tm, BRANCH))
        xc = xc_ref[...]
        sp8 = sp_ref[...]
        xcb, r, ig, a, mult = _rg_gates(xc, wa_ref, ba_ref, wx_ref, bx_ref, sp8)
        gate = gate_ref[...]
        dov = do_ref[...]
        dh = dov * _gelu(gate)
        dgate_ref[...] = (dov * h_ref[...] * _gelu_grad(gate)).astype(bf16)
        adh = a * dh
        ec = ec_s[...]
        last8 = _rows((SUBLANES, BRANCH)) == SUBLANES - 1
        for g in reversed(range(tm // SUBLANES)):
            sl = slice(g * SUBLANES, (g + 1) * SUBLANES)
            a_cum, e_loc = _scan_bwd(a[sl], adh[sl], SUBLANES)
            e = e_loc + a_cum * ec
            gt_s[sl, :] = dh[sl] + jnp.where(last8, ec, pltpu.roll(e, SUBLANES - 1, 0))
            ec = e[0:1, :]
        ec_s[...] = ec
        g_tot = gt_s[...]
        d_a = g_tot * hp_ref[...]
        d_mult = g_tot * ig * xc
        d_ix = g_tot * mult
        d_ig = d_ix * xc
        d_xc = d_ix * ig
        d_la = d_a * a - d_mult * (a * a) / mult
        d_r = -d_la * sp8
        dsp_ref[...] += jnp.sum(-d_la * r, axis=0, keepdims=True)
        dzr = d_r * r * (1.0 - r)
        dzi = d_ig * ig * (1.0 - ig)
        dzrb = dzr.astype(bf16)
        dzib = dzi.astype(bf16)
        d_xc = d_xc + _dot_nt(dzrb, wa_ref[...]) + _dot_nt(dzib, wx_ref[...])
        dwa_ref[...] += _dot_tn(xcb, dzrb)
        dwx_ref[...] += _dot_tn(xcb, dzib)
        dba_ref[...] += jnp.sum(dzr, axis=0, keepdims=True)
        dbx_ref[...] += jnp.sum(dzi, axis=0, keepdims=True)
        dcb_ref[...] += jnp.sum(d_xc, axis=0, keepdims=True)
        nxt = nxt_s[...]
        xb = xb_ref[...]
        dxb = cw_ref[3:4, :] * d_xc
        dcw_ref[3:4, :] += jnp.sum(d_xc * xb, axis=0, keepdims=True)
        for j in range(1, 4):
            sh = jnp.where(row < tm - j, pltpu.roll(d_xc, tm - j, 0), pltpu.roll(nxt, tm - j, 0))
            dxb = dxb + cw_ref[3 - j:4 - j, :] * sh
            dcw_ref[3 - j:4 - j, :] += jnp.sum(sh * xb, axis=0, keepdims=True)
        nxt_s[...] = d_xc
        dxb_ref[...] = dxb.astype(bf16)

    wa = cst["rg_wa"].astype(bf16)
    wx = cst["rg_wx"].astype(bf16)
    dxb, dgate, d_cw, d_cb, d_wa, d_ba, d_wx, d_bx, d_sp = _rt(
        "rg_bwd_" + tag, body, rows, tm,
        [(d_out, BRANCH, 0), (proj, BRANCH, U_COL + 5), (proj, BRANCH, U_COL + 6), (xc_saved, BRANCH, 0),
         (h_saved, BRANCH, 0), (hp_saved, BRANCH, 0)],
        [cst["rg_cw"], wa, cst["rg_ba"], wx, cst["rg_bx"], cst["rg_sp8"]],
        [(BRANCH, bf16), (BRANCH, bf16)],
        acc_outs=[(4, BRANCH), (1, BRANCH), (BRANCH, BRANCH), (1, BRANCH), (BRANCH, BRANCH), (1, BRANCH), (1, BRANCH)],
        scratch=[pltpu.VMEM((tm, BRANCH), f32), pltpu.VMEM((1, BRANCH), f32), pltpu.VMEM((tm, BRANCH), f32)],
        reverse=True)
    dcst = {"rg_cw": d_cw, "rg_cb": d_cb, "rg_wa": d_wa, "rg_ba": d_ba, "rg_wx": d_wx, "rg_bx": d_bx, "rg_sp8": d_sp}
    return dxb, dgate, dcst


def _merge_fwd(tag, proj, outs, bp, rows):
    def body(i, ya_ref, yb_ref, yc_ref, gm_ref, p_ref, m_ref):
        acc = None
        for n, y_ref in enumerate((ya_ref, yb_ref, yc_ref)):
            up = _dot(y_ref[...], p_ref[n])
            term = _sigmoid(gm_ref[:, n * D_MODEL:(n + 1) * D_MODEL]) * up
            acc = term if acc is None else acc + term
        m_ref[...] = acc.astype(bf16)
    return _rt("merge_fwd_" + tag, body, rows, ROW_TILE,
               [(outs[0], BRANCH, 0), (outs[1], BRANCH, 0), (outs[2], BRANCH, 0), (proj, GM_WIDTH, 0)],
               [bp], [(D_MODEL, bf16)])[0]


def _merge_bwd(tag, proj, outs, bp, dmerged, rows):
    def body(i, dm_ref, ya_ref, yb_ref, yc_ref, gm_ref, p_ref, da_ref, db_ref, dc_ref, dgm_ref, dp_ref):
        dm = dm_ref[...]
        for n, (y_ref, dy_ref) in enumerate(((ya_ref, da_ref), (yb_ref, db_ref), (yc_ref, dc_ref))):
            yv = y_ref[...]
            up = _dot(yv, p_ref[n])
            gt = _sigmoid(gm_ref[:, n * D_MODEL:(n + 1) * D_MODEL])
            dup = (dm * gt).astype(bf16)
            dgm_ref[:, n * D_MODEL:(n + 1) * D_MODEL] = (dm * up * gt * (1.0 - gt)).astype(bf16)
            dy_ref[...] = _dot_nt(dup, p_ref[n])
            dp_ref[n] += _dot_tn(yv, dup)
    return _rt("merge_bwd_" + tag, body, rows, ROW_TILE,
               [(dmerged, D_MODEL, 0), (outs[0], BRANCH, 0), (outs[1], BRANCH, 0), (outs[2], BRANCH, 0),
                (proj, GM_WIDTH, 0)],
               [bp], [(BRANCH, f32), (BRANCH, f32), (BRANCH, f32), (GM_WIDTH, bf16)],
               acc_outs=[(N_BRANCH, BRANCH, D_MODEL)])


def _block_diag(blocks):
    g, r, c = blocks.shape
    on_diag = (lax.broadcasted_iota(jnp.int32, (g * r, g * c), 0) // r
               == lax.broadcasted_iota(jnp.int32, (g * r, g * c), 1) // c)
    tiled = jnp.broadcast_to(blocks.reshape(g * r, 1, c), (g * r, g, c)).reshape(g * r, g * c)
    return jnp.where(on_diag, tiled, 0.0)


def _prep_consts(sp):
    p = jax.nn.softmax(sp["hg_lb_logits"], axis=0)
    lower = jnp.cumsum(p, axis=0) - p[0]
    out = []
    for l in range(DEPTH):
        lr = jnp.minimum(sp["s5_lambda_re"][l], S5_EIG_MAX)
        li = sp["s5_lambda_im"][l]
        dt = jnp.exp(sp["s5_log_dt"][l])[:, None]
        mag = jnp.exp(lr * dt)
        ar = mag * jnp.cos(li * dt)
        ai = mag * jnp.sin(li * dt)
        den = lr * lr + li * li
        fr = ((ar - 1.0) * lr + ai * li) / den
        fi = (ai * lr - (ar - 1.0) * li) / den
        br, bi = sp["s5_b_re"][l], sp["s5_b_im"][l]
        bbr = fr[..., None] * br - fi[..., None] * bi
        bbi = fr[..., None] * bi + fi[..., None] * br
        c = {
            "a_re": ar.reshape(1, S5_LANES), "a_im": ai.reshape(1, S5_LANES),
            "b_re": _block_diag(bbr.transpose(0, 2, 1)), "b_im": _block_diag(bbi.transpose(0, 2, 1)),
            "c_re": _block_diag(sp["s5_c_re"][l].transpose(0, 2, 1)),
            "c_im": -_block_diag(sp["s5_c_im"][l].transpose(0, 2, 1)),
            "s5_d": sp["s5_d"][l][None], "glu_b": sp["s5_glu_b"][l][None],
            "hg_lb": lower[l][None], "hg_nw": sp["hg_norm_w"][l][None],
            "rg_cw": sp["rg_conv_w"][l], "rg_cb": sp["rg_conv_b"][l][None],
            "rg_wa": _block_diag(sp["rg_wa"][l]), "rg_ba": sp["rg_ba"][l][None],
            "rg_wx": _block_diag(sp["rg_wx"][l]), "rg_bx": sp["rg_bx"][l][None],
            "rg_sp8": (RG_C * jax.nn.softplus(-sp["rg_lambda"][l]))[None],
        }
        out.append(c)
    return out


def _mixer_fwd(tag, x, hb, w_in, bp, w_out, cst, next_nw, rows):
    proj = _mm1("mix_proj_" + tag, hb, w_in, "nn", rows, IN_TOTAL, D_MODEL, 512, IN_TOTAL // 4, D_MODEL)
    cst = dict(cst)
    out_a, sv_a = _s5_fwd(tag, proj, cst, rows)
    out_b, sv_b = _hg_fwd(tag, proj, cst, rows)
    out_c, sv_c = _rg_fwd(tag, proj, cst, rows)
    merged = _merge_fwd(tag, proj, (out_a, out_b, out_c), bp, rows)
    x_out, hb_out = _mm("mix_out_" + tag, [merged], [w_out], [(0, 0, 0)], 1, "nn", rows, D_MODEL, D_MODEL,
                        512, D_MODEL, D_MODEL, [f32, bf16], _residual_then_norm(1.0), extras=[x], vecs=[next_nw])
    return x_out, hb_out, (x, hb, proj, (out_a, out_b, out_c), merged, sv_a, sv_b, sv_c)


def _mixer_bwd(tag, saved, nw, w_in, bp, w_out, cst, dx, dxb, rows):
    x, hb, proj, outs, merged, sv_a, sv_b, sv_c = saved
    d_wout = _mm1("mix_dwout_" + tag, merged, dxb, "tn", D_MODEL, D_MODEL, rows, D_MODEL, D_MODEL, TOKEN_K,
                  out_dtype=bf16)
    dmerged = _mm1("mix_dmerged_" + tag, dxb, w_out, "nt", rows, D_MODEL, D_MODEL, 512, D_MODEL, D_MODEL)
    d_a, d_b, d_c, dgm, d_bp = _merge_bwd(tag, proj, outs, bp, dmerged, rows)
    dxbc, dgatec, dcst_c = _rg_bwd(tag, sv_c, proj, cst, d_c, rows)
    dq, dz, dv, dg, dcst_b = _hg_bwd(tag, sv_b, proj, cst, d_b, rows)
    du, dcst_a, d_glu_w = _s5_bwd(tag, sv_a, proj, cst, d_a, rows)
    dproj = jnp.concatenate([dgm, du, dq, dz, dv, dg, dxbc, dgatec], axis=1)
    d_win = _mm1("mix_dwin_" + tag, hb, dproj, "tn", D_MODEL, IN_TOTAL, rows, D_MODEL, IN_TOTAL // 4, TOKEN_K,
                 out_dtype=bf16)
    dx_in, dxb_in, d_nw = _mm("mix_dh_" + tag, [dproj], [w_in], [(0, 0, 0)], 1, "nt", rows, D_MODEL, IN_TOTAL,
                              512, D_MODEL, IN_TOTAL // 2, [f32, bf16], _norm_bwd_epilogue, extras=[x, dx], vecs=[nw],
                              n_part=1)
    dcst = {**dcst_a, **dcst_b, **dcst_c}
    return dx_in, dxb_in, jnp.sum(d_nw, axis=0), d_win, d_bp, d_wout, d_glu_w, dcst


def _local_step(x, target, weights_of, small, grads_done):
    rows = x.shape[0]
    consts, consts_vjp = jax.vjp(_prep_consts, small)
    norm_w = small["norm_w"]
    saved = []
    h = x
    hb = _rms_fwd("first_norm", x, norm_w[0, 0][None], rows)
    for l in range(DEPTH):
        t = str(l)
        after = [norm_w[l + 1, 0][None]] if l + 1 < DEPTH else []
        wa = weights_of(l, "a")
        h, hb, sv0 = _ffn_fwd(t + "a", h, hb, *wa, [norm_w[l, 1][None]], rows)
        wm = weights_of(l, "mix")
        cst = dict(consts[l])
        cst["glu_w"] = wm[3]
        h, hb, sv1 = _mixer_fwd(t, h, hb, *wm[:3], cst, norm_w[l, 2][None], rows)
        wb = weights_of(l, "b")
        h, hb, sv2 = _ffn_fwd(t + "b", h, hb, *wb, after, rows)
        saved.append((sv0, sv1, sv2, cst, wa, wm, wb))
    dx, dxb, loss, d_fnw = _loss_head(h, small["final_norm_w"][None], target, rows)
    d_norm = [None] * DEPTH
    d_consts = [None] * DEPTH
    for l in reversed(range(DEPTH)):
        t = str(l)
        sv0, sv1, sv2, cst, wa, wm, wb = saved[l]
        dx, dxb, dn2, dg1, du1, dd1 = _ffn_bwd(t + "b", sv2, norm_w[l, 2][None], *wb, dx, dxb, rows)
        grads_done(l, "b", [dg1, du1, dd1])
        dx, dxb, dn1, d_win, d_bp, d_wout, d_glu_w, dcst = _mixer_bwd(
            t, sv1, norm_w[l, 1][None], *wm[:3], cst, dx, dxb, rows)
        grads_done(l, "mix", [d_win, d_bp, d_wout, d_glu_w])
        dx, dxb, dn0, dg0, du0, dd0 = _ffn_bwd(t + "a", sv0, norm_w[l, 0][None], *wa, dx, dxb, rows)
        grads_done(l, "a", [dg0, du0, dd0])
        d_norm[l] = jnp.concatenate([dn0, dn1, dn2], axis=0)
        d_consts[l] = dcst
    (g_small,) = consts_vjp(d_consts)
    g_small = dict(g_small)
    g_small["norm_w"] = g_small["norm_w"] + jnp.stack(d_norm)
    g_small["final_norm_w"] = g_small["final_norm_w"] + d_fnw[0]
    return loss[0, 0], dx, g_small


def _other_chips(x, y):
    return [(1 - x, y), (x, 1 - y), (1 - x, 1 - y)]


def _gather_over_ici(shards):
    n = len(shards)

    def copies(in_refs, out_refs, send_sems, recv_sems):
        x, y, c = _place()
        cps = []
        for i in range(n):
            mine = out_refs[i].at[4 * x + 2 * y + c]
            cps.append(pltpu.make_async_copy(in_refs[i], mine, send_sems.at[5 * i + 4]))
            for k, to in enumerate([(x, y, 1 - c)] + [(px, py, c) for px, py in _other_chips(x, y)]):
                cps.append(pltpu.make_async_remote_copy(
                    src_ref=in_refs[i], dst_ref=mine, send_sem=send_sems.at[5 * i + k],
                    recv_sem=recv_sems.at[5 * i + k], device_id=to, device_id_type=MESH_IDS))
        return cps

    return _Carry(shards, [jax.ShapeDtypeStruct((N_DEV,) + s.shape, s.dtype) for s in shards], 5 * n, copies)


def _gather_forward(name, landings):
    n = len(landings)

    def body(*refs):
        in_refs, out_refs = refs[:n], refs[n:2 * n]
        send_sems, recv_sems = refs[2 * n:]
        x, y, c = _place()
        cps = []
        for i in range(n):
            for j, (px, py) in enumerate(_other_chips(x, y)):
                block = 4 * px + 2 * py + c
                cps.append(pltpu.make_async_remote_copy(
                    src_ref=in_refs[i].at[block], dst_ref=out_refs[i].at[block], send_sem=send_sems.at[3 * i + j],
                    recv_sem=recv_sems.at[3 * i + j], device_id=(x, y, 1 - c), device_id_type=MESH_IDS))
        for cp in cps:
            cp.start()
        for cp in cps:
            cp.wait()

    return pl.pallas_call(
        body, name=name, out_shape=[jax.ShapeDtypeStruct(a.shape, a.dtype) for a in landings],
        in_specs=[ANY_SPEC] * n, out_specs=[ANY_SPEC] * n, input_output_aliases={i: i for i in range(n)},
        scratch_shapes=[pltpu.SemaphoreType.DMA((3 * n,)), pltpu.SemaphoreType.DMA((3 * n,))],
    )(*landings)


def _all_gather(name, shards):
    n = len(shards)

    def body(*refs):
        x_refs, out_refs = refs[:n], refs[n:2 * n]
        send_sems, recv_sems, local_sems = refs[2 * n:]
        x, y, c = _place()
        me, sibling = (x, y, c), (x, y, 1 - c)
        chips = [(1 - x, y), (x, 1 - y), (1 - x, 1 - y)]

        def blk(i, px, py, pc):
            return out_refs[i].at[4 * px + 2 * py + pc]

        def copy(i, k, block, to, src=None):
            return pltpu.make_async_remote_copy(
                src_ref=blk(i, *block) if src is None else src, dst_ref=blk(i, *block),
                send_sem=send_sems.at[7 * i + k], recv_sem=recv_sems.at[7 * i + k], device_id=to,
                device_id_type=MESH_IDS)

        mine = [pltpu.make_async_copy(x_refs[i], blk(i, *me), local_sems.at[i]) for i in range(n)]
        for cp in mine:
            cp.start()
        first = []
        for i in range(n):
            first.append(copy(i, 0, me, sibling, src=x_refs[i]))
            first += [copy(i, 1 + j, me, (*chip, c), src=x_refs[i]) for j, chip in enumerate(chips)]
        for cp in first:
            cp.start()
        passed = []
        for j, chip in enumerate(chips):
            for i in range(n):
                copy(i, 1 + j, (*chip, c), me).wait_recv()
                fwd = copy(i, 4 + j, (*chip, c), sibling)
                fwd.start()
                passed.append(fwd)
        for i in range(n):
            copy(i, 0, sibling, me).wait_recv()
            for j, chip in enumerate(chips):
                copy(i, 4 + j, (*chip, 1 - c), me).wait_recv()
        for cp in first + passed:
            cp.wait_send()
        for cp in mine:
            cp.wait()

    return pl.pallas_call(
        body, name=name, out_shape=[jax.ShapeDtypeStruct((N_DEV,) + s.shape, s.dtype) for s in shards],
        in_specs=[ANY_SPEC] * n, out_specs=[ANY_SPEC] * n,
        scratch_shapes=[pltpu.SemaphoreType.DMA((7 * n,)), pltpu.SemaphoreType.DMA((7 * n,)),
                        pltpu.SemaphoreType.DMA((n,))],
    )(*shards)


def _row_tile(rows):
    return rows if rows <= 512 else next(t for t in range(512, 7, -8) if rows % t == 0)


def _sums_over_ici(chip_sums):
    n = len(chip_sums)

    def copies(in_refs, out_refs, send_sems, recv_sems):
        x, y, c = _place()
        return [pltpu.make_async_remote_copy(
            src_ref=in_refs[i].at[2 * px + py], dst_ref=out_refs[i].at[k], send_sem=send_sems.at[3 * i + k],
            recv_sem=recv_sems.at[3 * i + k], device_id=(px, py, c), device_id_type=MESH_IDS)
            for i in range(n) for k, (px, py) in enumerate(_other_chips(x, y))]

    return _Carry(chip_sums, [jax.ShapeDtypeStruct((3,) + t.shape[1:], t.dtype) for t in chip_sums], 3 * n, copies)


def _reduce_scatter(tag, parts, hosts=None):
    n = len(parts)
    _, _, c = _place()

    def body_pair(*refs):
        p_refs, got_refs = refs[:n], refs[n:2 * n]
        send_sems, recv_sems = refs[2 * n:]
        x, y, c = _place()
        cps = [pltpu.make_async_remote_copy(
            src_ref=p_refs[i].at[1 - c], dst_ref=got_refs[i], send_sem=send_sems.at[i], recv_sem=recv_sems.at[i],
            device_id=(x, y, 1 - c), device_id_type=MESH_IDS) for i in range(n)]
        for cp in cps:
            cp.start()
        for cp in cps:
            cp.wait()

    from_sibling = pl.pallas_call(
        body_pair, name="rs_pair_" + tag, out_shape=[jax.ShapeDtypeStruct(p.shape[1:], p.dtype) for p in parts],
        in_specs=[ANY_SPEC] * n, out_specs=[ANY_SPEC] * n,
        scratch_shapes=[pltpu.SemaphoreType.DMA((n,)), pltpu.SemaphoreType.DMA((n,))],
    )(*parts)

    chip_sums = []
    for i, (part, got) in enumerate(zip(parts, from_sibling)):
        _, _, r, cols = part.shape
        tr = _row_tile(r)

        def body_add(idx_ref, p_ref, g_ref, o_ref):
            o_ref[...] = (p_ref[...].astype(f32) + g_ref[...].astype(f32)).astype(o_ref.dtype)

        chip_sums.append(pl.pallas_call(
            body_add, name="rs_pair_sum_%s_%d" % (tag, i), out_shape=jax.ShapeDtypeStruct((4, r, cols), part.dtype),
            grid_spec=pltpu.PrefetchScalarGridSpec(
                num_scalar_prefetch=1, grid=(4, r // tr),
                in_specs=[pl.BlockSpec((None, None, tr, cols), lambda j, t, idx: (idx[0], j, t, 0)),
                          pl.BlockSpec((None, tr, cols), lambda j, t, idx: (j, t, 0))],
                out_specs=pl.BlockSpec((None, tr, cols), lambda j, t, idx: (j, t, 0))),
            compiler_params=_cparams(("parallel", "parallel")),
        )(jnp.stack([c]).astype(jnp.int32), part, got))

    others = [None] * n
    riding = set()
    for host, which in (hosts or {}).items():
        rider = _sums_over_ici([chip_sums[i] for i in which])
        _CARRIED[host] = rider
        for pos, i in enumerate(which):
            others[i] = functools.partial(lambda r, p: r.outs[p], rider, pos)
        riding.update(which)
    rest = [i for i in range(n) if i not in riding]
    if rest:
        alone = _sums_over_ici([chip_sums[i] for i in rest])

        def body_chips(*refs):
            k = len(rest)
            cps = alone.copies(refs[:k], refs[k:2 * k], *refs[2 * k:])
            for cp in cps:
                cp.start()
            for cp in cps:
                cp.wait()

        from_chips = pl.pallas_call(
            body_chips, name="rs_chips_" + tag, out_shape=alone.out_shapes,
            in_specs=[ANY_SPEC] * len(rest), out_specs=[ANY_SPEC] * len(rest), scratch_shapes=alone.sems(),
        )(*alone.ins)
        for pos, i in enumerate(rest):
            others[i] = functools.partial(lambda got: got, from_chips[pos])
    return list(zip(chip_sums, others))


def _own_index():
    x, y, _ = _place()
    return jnp.stack([2 * x + y]).astype(jnp.int32)


def _own_total(name, chip_sum, others):
    _, r, cols = chip_sum.shape
    tr = _row_tile(r)

    def body(idx_ref, t_ref, g_ref, o_ref):
        o_ref[...] = ((t_ref[...].astype(f32) + g_ref[0].astype(f32)) + g_ref[1].astype(f32)) + g_ref[2].astype(f32)

    return pl.pallas_call(
        body, name=name, out_shape=jax.ShapeDtypeStruct((r, cols), f32),
        grid_spec=pltpu.PrefetchScalarGridSpec(
            num_scalar_prefetch=1, grid=(r // tr,),
            in_specs=[pl.BlockSpec((None, tr, cols), lambda t, idx: (idx[0], t, 0)),
                      pl.BlockSpec((3, tr, cols), lambda t, idx: (0, t, 0))],
            out_specs=pl.BlockSpec((tr, cols), lambda t, idx: (t, 0))),
        compiler_params=_cparams(("parallel",)),
    )(_own_index(), chip_sum, others)


def _adam_update(w, gv, m, v):
    m_new = ADAM_B1 * m + (1.0 - ADAM_B1) * gv
    v_new = ADAM_B2 * v + (1.0 - ADAM_B2) * (gv * gv)
    m_hat = m_new / (1.0 - ADAM_B1 ** ADAM_STEP)
    v_hat = v_new / (1.0 - ADAM_B2 ** ADAM_STEP)
    return -ADAM_LR * (m_hat / (jnp.sqrt(v_hat) + ADAM_EPS) + ADAM_WD * w), m_new, v_new


def _adamw_reduced(name, w, pieces, m, v):
    n_p, rows, cols = w.shape
    tr = _row_tile(rows)

    def body(idx_ref, w_ref, *refs):
        red = refs[:2 * n_p]
        m_ref, v_ref, g_ref, d_ref, nm_ref, nv_ref = refs[2 * n_p:]
        p = pl.program_id(0)
        for q in range(n_p):
            @pl.when(p == q)
            def _(t_ref=red[2 * q], o_ref=red[2 * q + 1]):
                gv = ((t_ref[...].astype(f32) + o_ref[0].astype(f32)) + o_ref[1].astype(f32)) + o_ref[2].astype(f32)
                g_ref[...] = gv
                d_ref[...], nm_ref[...], nv_ref[...] = _adam_update(w_ref[...], gv, m_ref[...], v_ref[...])

    spec = pl.BlockSpec((None, tr, cols), lambda p, t, idx: (p, t, 0))
    red_specs, red_args = [], []
    for q, (chip_sum, others) in enumerate(pieces):
        red_specs.append(pl.BlockSpec((None, tr, cols), lambda p, t, idx, q=q: (idx[0], jnp.where(p == q, t, 0), 0)))
        red_specs.append(pl.BlockSpec((3, tr, cols), lambda p, t, idx, q=q: (0, jnp.where(p == q, t, 0), 0)))
        red_args += [chip_sum, others]
    return pl.pallas_call(
        body, name=name, out_shape=[jax.ShapeDtypeStruct((n_p, rows, cols), f32)] * 4,
        grid_spec=pltpu.PrefetchScalarGridSpec(
            num_scalar_prefetch=1, grid=(n_p, rows // tr),
            in_specs=[spec] + red_specs + [spec, spec], out_specs=[spec] * 4),
        compiler_params=_cparams(("parallel", "parallel")),
    )(_own_index(), w, *red_args, m, v)


def _adamw(name, w, g, m, v):
    rows, cols = w.shape
    tr = _row_tile(rows)

    def body(w_ref, g_ref, m_ref, v_ref, d_ref, nm_ref, nv_ref):
        d_ref[...], nm_ref[...], nv_ref[...] = _adam_update(w_ref[...], g_ref[...], m_ref[...], v_ref[...])

    spec = pl.BlockSpec((tr, cols), lambda i: (i, 0))
    return pl.pallas_call(
        body, name=name, grid=(rows // tr,), in_specs=[spec] * 4, out_specs=[spec] * 3,
        out_shape=[jax.ShapeDtypeStruct((rows, cols), f32)] * 3, compiler_params=_cparams(("parallel",)),
    )(w, g, m, v)


WEIGHT_NAMES = ["norm_w", "final_norm_w", "ffn_gate", "ffn_up", "ffn_down", "w_in", "branch_proj", "w_out",
                "s5_lambda_re", "s5_lambda_im", "s5_log_dt", "s5_b_re", "s5_b_im", "s5_c_re", "s5_c_im", "s5_d",
                "s5_glu_w", "s5_glu_b", "hg_lb_logits", "hg_norm_w", "rg_conv_w", "rg_conv_b", "rg_wa", "rg_ba",
                "rg_wx", "rg_bx", "rg_lambda"]
SHARDED = {"ffn_gate": (3, "gate"), "ffn_up": (3, "up"), "ffn_down": (2, "down"), "w_in": (2, "w_in"),
           "branch_proj": (3, "bp"), "w_out": (1, "w_out"), "s5_glu_w": (1, "glu_w"),
           "norm_w": (2, None), "rg_conv_w": (2, None)}
BIG = ["ffn_gate", "ffn_up", "ffn_down", "w_in", "branch_proj", "w_out", "s5_glu_w"]
PARTS = {"a": [("ffn_gate", 0, 1), ("ffn_up", 0, 1), ("ffn_down", 0, 0)],
         "b": [("ffn_gate", 1, 1), ("ffn_up", 1, 1), ("ffn_down", 1, 0)],
         "mix": [("w_in", None, 1), ("branch_proj", None, 2), ("w_out", None, 0), ("s5_glu_w", None, 0)]}
AG_HOSTS = {"ffn_up_0a": (0, "mix", [0]), "ffn_down_0a": (0, "mix", [1, 2, 3]), "mix_proj_0": (0, "b", [0, 1, 2]),
            "s5_out_0": (1, "a", [0]), "hg_fwd_0": (1, "a", [1]), "rg_fwd_0": (1, "a", [2]),
            "merge_fwd_0": (1, "b", [0]), "ffn_up_0b": (1, "b", [1]), "ffn_down_0b": (1, "b", [2]),
            "s5_scan_fwd_0": (1, "mix", [0, 1]), "mix_out_0": (1, "mix", [2, 3])}
RS_HOSTS = {(1, "b"): {"s5_scan_bwd_1": [0, 1, 2]},
            (1, "mix"): {"ffn_bwd_mid_1a": [1, 2, 3], "ffn_dh_1a": [0]},
            (1, "a"): {"ffn_dh_0b": [0, 1], "merge_bwd_0": [2]},
            (0, "b"): {"s5_scan_bwd_0": [0, 1, 2]},
            (0, "mix"): {"ffn_bwd_mid_0a": [1, 2, 3], "ffn_dh_0a": [0]}}
SMALL_SHARDED = ["norm_w", "rg_conv_w"]
REPLICATED = [n for n in WEIGHT_NAMES if n not in SHARDED]
LANES = 128


PACK_ROWS = 512


def _pack_rows(arrays, names):
    pieces = []
    for n in names:
        flat = arrays[n].reshape(-1)
        pieces.append(jnp.pad(flat, (0, -flat.shape[0] % LANES)).reshape(-1, LANES))
    rows = jnp.concatenate(pieces, axis=0)
    return jnp.pad(rows, ((0, -rows.shape[0] % PACK_ROWS), (0, 0)))


def _unpack_rows(rows, names, like):
    out, r0 = {}, 0
    for n in names:
        size = math.prod(like[n].shape)
        nrows = -(-size // LANES)
        out[n] = rows[r0:r0 + nrows].reshape(-1)[:size].reshape(like[n].shape)
        r0 += nrows
    return out


def _unshard(gathered, axis):
    g = jnp.moveaxis(gathered, 0, axis)
    shp = g.shape
    return g.reshape(shp[:axis] + (shp[axis] * shp[axis + 1],) + shp[axis + 2:])


def _to_blocks(full, axis):
    shp = full.shape
    g = full.reshape(shp[:axis] + (4, 2, shp[axis] // N_DEV) + shp[axis + 1:])
    g = jnp.moveaxis(g, (axis, axis + 1), (1, 0))
    return g.reshape(2, 4, -1, g.shape[-1])


W_IN_SPLIT = IN_TOTAL - GM_WIDTH


def kernel(x, norm_w, final_norm_w, ffn_gate, ffn_up, ffn_down, w_in, branch_proj, w_out, s5_lambda_re, s5_lambda_im, s5_log_dt, s5_b_re, s5_b_im, s5_c_re, s5_c_im, s5_d, s5_glu_w, s5_glu_b, hg_lb_logits, hg_norm_w, rg_conv_w, rg_conv_b, rg_wa, rg_ba, rg_wx, rg_bx, rg_lambda, loss_target, m_norm_w, m_final_norm_w, m_ffn_gate, m_ffn_up, m_ffn_down, m_w_in, m_branch_proj, m_w_out, m_s5_lambda_re, m_s5_lambda_im, m_s5_log_dt, m_s5_b_re, m_s5_b_im, m_s5_c_re, m_s5_c_im, m_s5_d, m_s5_glu_w, m_s5_glu_b, m_hg_lb_logits, m_hg_norm_w, m_rg_conv_w, m_rg_conv_b, m_rg_wa, m_rg_ba, m_rg_wx, m_rg_bx, m_rg_lambda, v_norm_w, v_final_norm_w, v_ffn_gate, v_ffn_up, v_ffn_down, v_w_in, v_branch_proj, v_w_out, v_s5_lambda_re, v_s5_lambda_im, v_s5_log_dt, v_s5_b_re, v_s5_b_im, v_s5_c_re, v_s5_c_im, v_s5_d, v_s5_glu_w, v_s5_glu_b, v_hg_lb_logits, v_hg_norm_w, v_rg_conv_w, v_rg_conv_b, v_rg_wa, v_rg_ba, v_rg_wx, v_rg_bx, v_rg_lambda):
    w = dict(zip(WEIGHT_NAMES, (norm_w, final_norm_w, ffn_gate, ffn_up, ffn_down, w_in, branch_proj, w_out,
                                s5_lambda_re, s5_lambda_im, s5_log_dt, s5_b_re, s5_b_im, s5_c_re, s5_c_im, s5_d,
                                s5_glu_w, s5_glu_b, hg_lb_logits, hg_norm_w, rg_conv_w, rg_conv_b, rg_wa, rg_ba,
                                rg_wx, rg_bx, rg_lambda)))
    m = dict(zip(WEIGHT_NAMES, (m_norm_w, m_final_norm_w, m_ffn_gate, m_ffn_up, m_ffn_down, m_w_in, m_branch_proj,
                                m_w_out, m_s5_lambda_re, m_s5_lambda_im, m_s5_log_dt, m_s5_b_re, m_s5_b_im, m_s5_c_re,
                                m_s5_c_im, m_s5_d, m_s5_glu_w, m_s5_glu_b, m_hg_lb_logits, m_hg_norm_w, m_rg_conv_w,
                                m_rg_conv_b, m_rg_wa, m_rg_ba, m_rg_wx, m_rg_bx, m_rg_lambda)))
    v = dict(zip(WEIGHT_NAMES, (v_norm_w, v_final_norm_w, v_ffn_gate, v_ffn_up, v_ffn_down, v_w_in, v_branch_proj,
                                v_w_out, v_s5_lambda_re, v_s5_lambda_im, v_s5_log_dt, v_s5_b_re, v_s5_b_im, v_s5_c_re,
                                v_s5_c_im, v_s5_d, v_s5_glu_w, v_s5_glu_b, v_hg_lb_logits, v_hg_norm_w, v_rg_conv_w,
                                v_rg_conv_b, v_rg_wa, v_rg_ba, v_rg_wx, v_rg_bx, v_rg_lambda)))
    rows = x.shape[1]

    _CARRIED.clear()

    def shard_of(piece, l):
        n, k, _ = piece
        return (w[n][l] if k is None else w[n][l, k]).astype(bf16)

    def assemble(part, gathered):
        full = [_unshard(g, piece[2]) for piece, g in zip(PARTS[part], gathered)]
        if part == "mix":
            full[0] = jnp.concatenate([full[0][:, W_IN_SPLIT:], full[0][:, :W_IN_SPLIT]], axis=1)
        return full

    n_a = len(PARTS["a"])
    first = _all_gather("gather_weights", [shard_of(p, 0) for p in PARTS["a"]] + [w[n] for n in SMALL_SHARDED])
    small = {n: w[n] for n in REPLICATED}
    for n, g in zip(SMALL_SHARDED, first[n_a:]):
        small[n] = _unshard(g, SHARDED[n][0])
    riders = {}
    for host, (l, part, which) in AG_HOSTS.items():
        rider = _gather_over_ici([shard_of(PARTS[part][j], l) for j in which])
        _CARRIED[host] = rider
        riders.setdefault((l, part), []).append((which, rider))

    def weights_of(l, part):
        if (l, part) == (0, "a"):
            return assemble(part, first[:n_a])
        landed = [None] * len(PARTS[part])
        for which, rider in riders[l, part]:
            for j, buf in zip(which, rider.outs):
                landed[j] = buf
        return assemble(part, _gather_forward("gather_forward_%d%s" % (l, part), landed))

    sums = {}

    def blocks_of(part, grads):
        if part == "mix":
            grads = [jnp.concatenate([grads[0][:, GM_WIDTH:], grads[0][:, :GM_WIDTH]], axis=1)] + grads[1:]
        return [_to_blocks(g, piece[2]).astype(bf16) for piece, g in zip(PARTS[part], grads)]

    last_grads = []

    def grads_done(l, part, grads):
        if (l, part) in RS_HOSTS:
            sums[l, part] = _reduce_scatter("%d%s" % (l, part), blocks_of(part, grads), hosts=RS_HOSTS[l, part])
        else:
            last_grads.extend(blocks_of(part, grads))

    loss_part, dx, g_small = _local_step(x[0], loss_target[0], weights_of, small, grads_done)
    loss = lax.psum(loss_part, ("x", "y", "c"))

    parts = last_grads + [_to_blocks(g_small[n], SHARDED[n][0]) for n in SMALL_SHARDED]
    rep_rows = _pack_rows(g_small, REPLICATED)
    rep_slice = rep_rows.shape[0] // N_DEV
    parts.append(rep_rows.reshape(4, 2, rep_slice, LANES).transpose(1, 0, 2, 3))
    last = _reduce_scatter("last", parts)
    sums[0, "a"] = last[:n_a]

    grads, delta, new_m, new_v = {}, {}, {}, {}

    def update(n, pieces):
        shp = w[n].shape
        view = (len(pieces), -1, shp[-1])
        res = _adamw_reduced("adamw_" + n, w[n].reshape(view), [(t, others()) for t, others in pieces],
                             m[n].reshape(view), v[n].reshape(view))
        grads[n], delta[n], new_m[n], new_v[n] = (r.reshape(shp) for r in res)

    for n in BIG:
        update(n, [sums[l, part][j] for l in range(DEPTH) for part in ("a", "b", "mix")
                   for j, piece in enumerate(PARTS[part]) if piece[0] == n])
    for j, n in enumerate(SMALL_SHARDED):
        update(n, [last[n_a + j]])
    rep_mine = _own_total("rs_total_small", last[-1][0], last[-1][1]())
    rep_grads = _all_gather("gather_small_grads", [rep_mine])[0].reshape(-1, LANES)
    res = _adamw("adamw_small", _pack_rows(w, REPLICATED), rep_grads, _pack_rows(m, REPLICATED), _pack_rows(v, REPLICATED))
    for dst, src in zip((grads, delta, new_m, new_v), (rep_grads,) + tuple(res)):
        dst.update(_unpack_rows(src, REPLICATED, w))

    return (loss, dx.reshape(x.shape), *[grads[n] for n in WEIGHT_NAMES], *[delta[n] for n in WEIGHT_NAMES],
            *[new_m[n] for n in WEIGHT_NAMES], *[new_v[n] for n in WEIGHT_NAMES])
```

```python
import functools
import math

import jax
import jax.numpy as jnp
from jax import lax
from jax.experimental import pallas as pl
from jax.experimental.pallas import tpu as pltpu

f32 = jnp.float32
bf16 = jnp.bfloat16

D_MODEL = 1024
DEPTH = 2
BRANCH = 512
N_BRANCH = 3
S5_GROUP = 16
S5_GROUPS = 32
S5_STATE = 64
S5_LANES = S5_GROUPS * S5_STATE
S5_EIG_MAX = -1e-4
HG_HEADS = 4
HG_DK = 128
HG_CHUNK = 32
RG_BLOCKS = 8
RG_BLOCK = 64
RG_C = 8.0
D_FF = 2816
EPS = 1e-6
IN_TOTAL = 6656
GM_WIDTH = N_BRANCH * D_MODEL
N_DEV = 8

ADAM_LR = 0.001
ADAM_B1 = 0.9
ADAM_B2 = 0.999
ADAM_EPS = 1e-08
ADAM_WD = 0.01
ADAM_STEP = 10

VMEM_LIMIT_V7X = 56 * 1024 * 1024
ROW_TILE = 256
FF_TILE = 1408
TOKEN_K = 2048
MXU_COLS = 256


def _cparams(sem):
    return pltpu.CompilerParams(dimension_semantics=sem, vmem_limit_bytes=VMEM_LIMIT_V7X)


MESH_IDS = pl.DeviceIdType.MESH
ANY_SPEC = pl.BlockSpec(memory_space=pl.ANY)


def _place():
    return lax.axis_index("x"), lax.axis_index("y"), lax.axis_index("c")


class _Carry:
    def __init__(self, ins, out_shapes, n_sems, copies):
        self.ins, self.out_shapes, self.n_sems, self.copies = list(ins), list(out_shapes), n_sems, copies
        self.outs = None

    def sems(self):
        return [pltpu.SemaphoreType.DMA((self.n_sems,)), pltpu.SemaphoreType.DMA((self.n_sems,))]

    def start(self, when, *riders):
        @pl.when(when)
        def _():
            for cp in self.copies(*riders):
                cp.start()

    def finish(self, when, *riders):
        @pl.when(when)
        def _():
            for cp in self.copies(*riders):
                cp.wait()


_CARRIED = {}


def _call_with_rider(name, body, grid, in_specs, out_specs, out_shape, scratch, semantics, args):
    carry = _CARRIED.pop(name, None)
    if carry is None:
        return pl.pallas_call(body, name=name, grid=grid, in_specs=in_specs, out_specs=out_specs,
                              out_shape=out_shape, scratch_shapes=scratch, compiler_params=_cparams(semantics))(*args)
    n_in, n_out, nci, nco = len(in_specs), len(out_specs), len(carry.ins), len(carry.out_shapes)

    def kern(*refs):
        ids = [pl.program_id(d) for d in range(len(grid))]
        own = refs[:n_in] + refs[n_in + nci:n_in + nci + n_out] + refs[n_in + nci + n_out + nco:-2]
        riders = (refs[n_in:n_in + nci], refs[n_in + nci + n_out:n_in + nci + n_out + nco]) + tuple(refs[-2:])
        carry.start(functools.reduce(jnp.logical_and, [p == 0 for p in ids]), *riders)
        body(*own)
        carry.finish(functools.reduce(jnp.logical_and, [p == g - 1 for p, g in zip(ids, grid)]), *riders)

    res = pl.pallas_call(
        kern, name=name, grid=grid, in_specs=list(in_specs) + [ANY_SPEC] * nci,
        out_specs=list(out_specs) + [ANY_SPEC] * nco, out_shape=list(out_shape) + carry.out_shapes,
        scratch_shapes=list(scratch) + carry.sems(), compiler_params=_cparams(("arbitrary",) * len(grid)),
    )(*args, *carry.ins)
    carry.outs = res[n_out:]
    return res[:n_out]


def _sigmoid(x):
    return 0.5 * jnp.tanh(0.5 * x) + 0.5


def _sigmoid_small(x):
    return 1.0 / (1.0 + jnp.exp(-x))


_GELU_C = math.sqrt(2.0 / math.pi)


def _gelu(x):
    t = jnp.tanh(_GELU_C * (x + 0.044715 * x * x * x))
    return 0.5 * x * (1.0 + t)


def _gelu_grad(x):
    t = jnp.tanh(_GELU_C * (x + 0.044715 * x * x * x))
    return 0.5 * (1.0 + t) + 0.5 * x * (1.0 - t * t) * _GELU_C * (1.0 + 3.0 * 0.044715 * x * x)


def _expm1(x):
    p = x * (1.0 + x * (0.5 + x * (1.0 / 6 + x * (1.0 / 24 + x * (1.0 / 120 + x * (1.0 / 720))))))
    return jnp.where(jnp.abs(x) < 0.3, p, jnp.exp(x) - 1.0)


def _dot(a, b):
    return jnp.dot(a, b, preferred_element_type=f32)


def _dot_nt(a, b):
    return lax.dot_general(a, b, (((1,), (1,)), ((), ())), preferred_element_type=f32)


def _dot_tn(a, b):
    return lax.dot_general(a, b, (((0,), (0,)), ((), ())), preferred_element_type=f32)


def _bdot(a, b):
    return lax.dot_general(a, b, (((2,), (1,)), ((0,), (0,))), preferred_element_type=f32)


def _bdot_nt(a, b):
    return lax.dot_general(a, b, (((2,), (2,)), ((0,), (0,))), preferred_element_type=f32)


def _rows(shape):
    return lax.broadcasted_iota(jnp.int32, shape, 0)


def _scan_fwd(a, b, n):
    row = _rows(a.shape)
    s = 1
    while s < n:
        valid = row >= s
        sh_a = pltpu.roll(a, s, 0)
        sh_b = pltpu.roll(b, s, 0)
        b = b + a * jnp.where(valid, sh_b, 0.0)
        a = a * jnp.where(valid, sh_a, 1.0)
        s *= 2
    return a, b


def _scan_bwd(a, b, n):
    row = _rows(a.shape)
    s = 1
    while s < n:
        valid = row < n - s
        sh_a = pltpu.roll(a, n - s, 0)
        sh_b = pltpu.roll(b, n - s, 0)
        b = b + a * jnp.where(valid, sh_b, 0.0)
        a = a * jnp.where(valid, sh_a, 1.0)
        s *= 2
    return a, b


def _seg_cumsum(x, n, seg):
    pos = _rows(x.shape) % seg
    s = 1
    while s < seg:
        x = x + jnp.where(pos >= s, pltpu.roll(x, s, 0), 0.0)
        s *= 2
    return x


def _seg_rev_cumsum(x, n, seg):
    pos = _rows(x.shape) % seg
    s = 1
    while s < seg:
        x = x + jnp.where(pos < seg - s, pltpu.roll(x, n - s, 0), 0.0)
        s *= 2
    return x


def _head_mean(x):
    parts = []
    for h in range(HG_HEADS):
        m = jnp.mean(x[:, h * HG_DK:(h + 1) * HG_DK], axis=1, keepdims=True)
        parts.append(jnp.broadcast_to(m, (x.shape[0], HG_DK)))
    return jnp.concatenate(parts, axis=1)


def _mm(name, a_list, b_list, terms, n_acc, mode, m, n, k, tm, tn, tk, out_dtypes, epilogue, extras=(), vecs=(),
        n_part=0, chunk=0):
    tm, tn, tk = min(tm, m), min(tn, n), min(tk, k)
    assert m % tm == 0 and n % tn == 0 and k % tk == 0, (name, m, n, k, tm, tn, tk)
    gk = k // tk
    if mode == "tn":
        a_spec = pl.BlockSpec((tk, tm), lambda i, j, kk: (kk, i))
    else:
        a_spec = pl.BlockSpec((tm, tk), lambda i, j, kk: (i, kk))
    if mode == "nt":
        b_spec = pl.BlockSpec((tn, tk), lambda i, j, kk: (j, kk))
    else:
        b_spec = pl.BlockSpec((tk, tn), lambda i, j, kk: (kk, j))
    o_spec = pl.BlockSpec((tm, tn), lambda i, j, kk: (i, j))
    v_spec = pl.BlockSpec((1, tn), lambda i, j, kk: (0, j))
    p_spec = pl.BlockSpec((None, 1, tn), lambda i, j, kk: (i, 0, j))
    dot = {"nn": _dot, "nt": _dot_nt, "tn": _dot_tn}[mode]
    na, nb, ne, nv, no = len(a_list), len(b_list), len(extras), len(vecs), len(out_dtypes)
    carry = _CARRIED.pop(name, None)
    nci, nco = (len(carry.ins), len(carry.out_shapes)) if carry else (0, 0)
    n_in = na + nb + ne + nv + nci
    grid = (m // tm, n // tn, gk)

    def kern(*refs):
        if carry:
            ids = [pl.program_id(d) for d in range(3)]
            riders = (refs[n_in - nci:n_in], refs[n_in + no + n_part:n_in + no + n_part + nco]) + tuple(refs[-2:])
            carry.start(functools.reduce(jnp.logical_and, [p == 0 for p in ids]), *riders)
        compute(*refs)
        if carry:
            carry.finish(functools.reduce(jnp.logical_and, [p == g - 1 for p, g in zip(ids, grid)]), *riders)

    def compute(*refs):
        a_refs = refs[:na]
        b_refs = refs[na:na + nb]
        e_refs = refs[na + nb:na + nb + ne]
        v_refs = refs[na + nb + ne:na + nb + ne + nv]
        o_refs = refs[n_in:n_in + no + n_part]

        def finish(accs):
            outs = epilogue(accs, [e[...] for e in e_refs], [r[...] for r in v_refs])
            for o, val in zip(o_refs, outs):
                o[...] = val.astype(o.dtype)

        def partial_sums():
            sums = [None] * n_acc
            for ai, bi, ci in terms:
                d = dot(a_refs[ai][...].astype(bf16), b_refs[bi][...].astype(bf16))
                sums[ci] = d if sums[ci] is None else sums[ci] + d
            return sums

        if gk == 1 and chunk:
            assert mode in ("nn", "nt") and tn % chunk == 0
            for c0 in range(0, tn, chunk):
                cols = slice(c0, c0 + chunk)
                sums = [None] * n_acc
                for ai, bi, ci in terms:
                    b_part = b_refs[bi][:, cols] if mode == "nn" else b_refs[bi][cols, :]
                    d = dot(a_refs[ai][...].astype(bf16), b_part.astype(bf16))
                    sums[ci] = d if sums[ci] is None else sums[ci] + d
                outs = epilogue(sums, [e[:, cols] for e in e_refs], [r[:, cols] for r in v_refs])
                for o, val in zip(o_refs, outs):
                    o[:, cols] = val.astype(o.dtype)
            return
        if gk == 1:
            finish(partial_sums())
            return
        acc = refs[n_in + no + n_part + nco]
        kk = pl.program_id(2)

        @pl.when(kk == 0)
        def _():
            acc[...] = jnp.zeros_like(acc)

        for ci, d in enumerate(partial_sums()):
            acc[ci] += d

        @pl.when(kk == gk - 1)
        def _():
            finish([acc[c] for c in range(n_acc)])

    res = pl.pallas_call(
        kern, name=name,
        grid=grid,
        in_specs=[a_spec] * na + [b_spec] * nb + [o_spec] * ne + [v_spec] * nv + [ANY_SPEC] * nci,
        out_specs=[o_spec] * no + [p_spec] * n_part + [ANY_SPEC] * nco,
        out_shape=([jax.ShapeDtypeStruct((m, n), dt) for dt in out_dtypes]
                   + [jax.ShapeDtypeStruct((m // tm, 1, n), f32)] * n_part + (carry.out_shapes if carry else [])),
        scratch_shapes=([pltpu.VMEM((n_acc, tm, tn), f32)] if gk > 1 else []) + (carry.sems() if carry else []),
        compiler_params=_cparams(("arbitrary",) * 3 if carry else ("parallel", "parallel", "arbitrary")),
    )(*a_list, *b_list, *extras, *vecs, *(carry.ins if carry else []))
    if carry:
        carry.outs = res[no + n_part:]
        res = res[:no + n_part]
    return res


def _mm1(name, a, b, mode, m, n, k, tm, tn, tk, out_dtype=f32, scale=None):
    def epi(accs, extras, vecs):
        return [accs[0] if scale is None else accs[0] * scale]
    return _mm(name, [a], [b], [(0, 0, 0)], 1, mode, m, n, k, tm, tn, tk, [out_dtype], epi)[0]


def _rt(name, body, rows, tm, row_ins, consts, row_outs, acc_outs=(), scratch=(), reverse=False):
    tm = min(tm, rows)
    assert rows % tm == 0
    nt = rows // tm

    def tile(i):
        return nt - 1 - i if reverse else i

    in_specs, args = [], []
    for spec in row_ins:
        arr = spec[0]
        if isinstance(spec[1], int):
            in_specs.append(pl.BlockSpec((tm, spec[1]), lambda i, cb=spec[2]: (tile(i), cb)))
        else:
            in_specs.append(pl.BlockSpec(spec[1], lambda i, fn=spec[2]: fn(tile(i))))
        args.append(arr)
    for c in consts:
        in_specs.append(pl.BlockSpec(c.shape, lambda i, nd=c.ndim: (0,) * nd))
        args.append(c)
    out_specs, out_shape = [], []
    for spec in row_outs:
        if isinstance(spec[0], int):
            out_specs.append(pl.BlockSpec((tm, spec[0]), lambda i: (tile(i), 0)))
            out_shape.append(jax.ShapeDtypeStruct((rows, spec[0]), spec[1]))
        else:
            out_specs.append(pl.BlockSpec(spec[1], lambda i, fn=spec[2]: fn(tile(i))))
            out_shape.append(jax.ShapeDtypeStruct(spec[0], spec[3]))
    for shp in acc_outs:
        out_specs.append(pl.BlockSpec(shp, lambda i, nd=len(shp): (0,) * nd))
        out_shape.append(jax.ShapeDtypeStruct(shp, f32))
    n_in = len(args)
    n_row_out = len(row_outs)
    n_acc = len(acc_outs)
    n_out = n_row_out + n_acc
    carry = _CARRIED.pop(name, None)
    nci, nco = (len(carry.ins), len(carry.out_shapes)) if carry else (0, 0)

    def kern(*refs):
        i = pl.program_id(0)
        if carry:
            own = refs[:n_in] + refs[n_in + nci:n_in + nci + n_out] + refs[n_in + nci + n_out + nco:-2]
            riders = (refs[n_in:n_in + nci], refs[n_in + nci + n_out:n_in + nci + n_out + nco]) + tuple(refs[-2:])
            carry.start(i == 0, *riders)
        else:
            own = refs
        acc_refs = own[n_in + n_row_out:n_in + n_out]

        @pl.when(i == 0)
        def _():
            for r in acc_refs:
                r[...] = jnp.zeros_like(r)

        body(i, *own)
        if carry:
            carry.finish(i == nt - 1, *riders)

    res = pl.pallas_call(
        kern, name=name, grid=(nt,), in_specs=in_specs + [ANY_SPEC] * nci, out_specs=out_specs + [ANY_SPEC] * nco,
        out_shape=out_shape + (carry.out_shapes if carry else []),
        scratch_shapes=list(scratch) + (carry.sems() if carry else []), compiler_params=_cparams(("arbitrary",)),
    )(*args, *(carry.ins if carry else []))
    if carry:
        carry.outs = res[n_out:]
        res = res[:n_out]
    return res


def _rms_rows(xv, wv):
    r = lax.rsqrt(jnp.mean(xv * xv, axis=1, keepdims=True) + EPS)
    return (xv * r * wv).astype(bf16)


def _rms_bwd_rows(xv, dhv, wv, dres):
    r = lax.rsqrt(jnp.mean(xv * xv, axis=1, keepdims=True) + EPS)
    xn = xv * r
    dxn = dhv * wv
    dx = dres + r * (dxn - xn * jnp.mean(dxn * xn, axis=1, keepdims=True))
    return [dx, dx.astype(bf16), jnp.sum(dhv * xn, axis=0, keepdims=True)]


def _rms_fwd(name, x, w, rows):
    def body(i, x_ref, w_ref, h_ref):
        h_ref[...] = _rms_rows(x_ref[...], w_ref[...])
    return _rt(name, body, rows, ROW_TILE, [(x, D_MODEL, 0)], [w], [(D_MODEL, bf16)])[0]


def _residual_then_norm(scale):
    def epi(accs, extras, vecs):
        x_out = extras[0] + scale * accs[0]
        return [x_out] + [_rms_rows(x_out, v) for v in vecs]
    return epi


def _norm_bwd_epilogue(accs, extras, vecs):
    return _rms_bwd_rows(extras[0], accs[0], vecs[0], extras[1])


def _loss_head(x, w, target, rows):
    def body(i, x_ref, t_ref, w_ref, dx_ref, dxb_ref, loss_ref, dw_ref):
        xv = x_ref[...]
        r = lax.rsqrt(jnp.mean(xv * xv, axis=1, keepdims=True) + EPS)
        xn = xv * r
        wv = w_ref[...]
        err = xn * wv - t_ref[...]
        part = 0.5 * jnp.sum(jnp.mean(err * err, axis=1, keepdims=True), axis=0, keepdims=True)
        loss_ref[...] += jnp.broadcast_to(part, (1, 128))
        dy = err * (1.0 / D_MODEL)
        dxn = dy * wv
        dx = r * (dxn - xn * jnp.mean(dxn * xn, axis=1, keepdims=True))
        dx_ref[...] = dx
        dxb_ref[...] = dx.astype(bf16)
        dw_ref[...] += jnp.sum(dy * xn, axis=0, keepdims=True)
    return _rt("loss_head", body, rows, ROW_TILE, [(x, D_MODEL, 0), (target, D_MODEL, 0)], [w],
               [(D_MODEL, f32), (D_MODEL, bf16)], acc_outs=[(1, 128), (1, D_MODEL)])


def _ffn_fwd(tag, x, hb, wg, wu, wd, next_nw, rows):
    def epi_up(accs, extras, vecs):
        a, b = accs
        return [a, b, a * _sigmoid(a) * b]
    a, b, s = _mm("ffn_up_" + tag, [hb], [wg, wu], [(0, 0, 0), (0, 1, 1)], 2, "nn", rows, D_FF, D_MODEL,
                  512, D_FF, D_MODEL, [bf16, bf16, bf16], epi_up, chunk=MXU_COLS)
    outs = _mm("ffn_down_" + tag, [s], [wd], [(0, 0, 0)], 1, "nn", rows, D_MODEL, D_FF,
               512, D_MODEL, D_FF, [f32] + [bf16] * len(next_nw), _residual_then_norm(0.5), extras=[x],
               vecs=next_nw)
    return outs[0], (outs[1] if next_nw else None), (x, hb, a, b, s)


def _ffn_bwd(tag, saved, nw, wg, wu, wd, dx, dxb, rows):
    x, hb, a, b, s = saved

    def epi_mid(accs, extras, vecs):
        ds = 0.5 * accs[0]
        av = extras[0].astype(f32)
        bv = extras[1].astype(f32)
        sg = _sigmoid(av)
        return [ds * bv * sg * (1.0 + av * (1.0 - sg)), ds * av * sg]
    da, db = _mm("ffn_bwd_mid_" + tag, [dxb], [wd], [(0, 0, 0)], 1, "nt", rows, D_FF, D_MODEL,
                 512, D_FF, D_MODEL, [bf16, bf16], epi_mid, extras=[a, b], chunk=MXU_COLS)
    d_wd = _mm1("ffn_dwd_" + tag, s, dxb, "tn", D_FF, D_MODEL, rows, FF_TILE, D_MODEL, TOKEN_K, out_dtype=bf16,
                scale=0.5)
    d_wg = _mm1("ffn_dwg_" + tag, hb, da, "tn", D_MODEL, D_FF, rows, D_MODEL, FF_TILE, TOKEN_K, out_dtype=bf16)
    d_wu = _mm1("ffn_dwu_" + tag, hb, db, "tn", D_MODEL, D_FF, rows, D_MODEL, FF_TILE, TOKEN_K, out_dtype=bf16)
    dx_in, dxb_in, d_nw = _mm("ffn_dh_" + tag, [da, db], [wg, wu], [(0, 0, 0), (1, 1, 0)], 1, "nt", rows, D_MODEL,
                              D_FF, 512, D_MODEL, D_FF, [f32, bf16], _norm_bwd_epilogue, extras=[x, dx], vecs=[nw],
                              n_part=1)
    return dx_in, dxb_in, jnp.sum(d_nw, axis=0), d_wg, d_wu, d_wd


S5_CB = 512
SUBLANES = 8
U_COL = GM_WIDTH // BRANCH


def _s5_scan_fwd(tag, proj, b_re, b_im, a_re, a_im, rows):
    tm = min(ROW_TILE, rows)
    nt = rows // tm
    nc = S5_LANES // S5_CB

    def kern(u_ref, bre_ref, bim_ref, ar_ref, ai_ref, xr_ref, xi_ref, pr_s, pi_s, cr_s, ci_s, mr_s, mi_s):
        t = pl.program_id(1)

        @pl.when(t == 0)
        def _():
            row8 = _rows((SUBLANES, S5_CB))
            pr = jnp.broadcast_to(ar_ref[...], (SUBLANES, S5_CB))
            pi = jnp.broadcast_to(ai_ref[...], (SUBLANES, S5_CB))
            s = 1
            while s < SUBLANES:
                sr = pltpu.roll(pr, s, 0)
                si = pltpu.roll(pi, s, 0)
                valid = row8 >= s
                pr, pi = jnp.where(valid, pr * sr - pi * si, pr), jnp.where(valid, pr * si + pi * sr, pi)
                s *= 2
            pr_s[...] = pr
            pi_s[...] = pi
            for k in range(3):
                s = 1 << k
                mr_s[k] = jnp.where(row8 >= s, pr[s - 1:s, :], 0.0)
                mi_s[k] = jnp.where(row8 >= s, pi[s - 1:s, :], 0.0)
            cr_s[...] = jnp.zeros_like(cr_s)
            ci_s[...] = jnp.zeros_like(ci_s)

        ub = u_ref[...].astype(bf16)
        br = _dot(ub, bre_ref[...])
        bi = _dot(ub, bim_ref[...])
        steps = [(mr_s[k], mi_s[k]) for k in range(3)]
        cr = cr_s[...]
        ci = ci_s[...]
        pr = pr_s[...]
        pi = pi_s[...]
        for g in range(tm // SUBLANES):
            sl = slice(g * SUBLANES, (g + 1) * SUBLANES)
            xr = br[sl]
            xi = bi[sl]
            for k, (mr, mi) in enumerate(steps):
                sr = pltpu.roll(xr, 1 << k, 0)
                si = pltpu.roll(xi, 1 << k, 0)
                xr, xi = xr + (mr * sr - mi * si), xi + (mr * si + mi * sr)
            xr, xi = xr + (pr * cr - pi * ci), xi + (pr * ci + pi * cr)
            xr_ref[sl, :] = xr
            xi_ref[sl, :] = xi
            cr = xr[SUBLANES - 1:SUBLANES, :]
            ci = xi[SUBLANES - 1:SUBLANES, :]
        cr_s[...] = cr
        ci_s[...] = ci

    return _call_with_rider(
        "s5_scan_fwd_" + tag, kern, (nc, nt),
        [pl.BlockSpec((tm, BRANCH), lambda c, t: (t, U_COL)),
         pl.BlockSpec((BRANCH, S5_CB), lambda c, t: (0, c)),
         pl.BlockSpec((BRANCH, S5_CB), lambda c, t: (0, c)),
         pl.BlockSpec((1, S5_CB), lambda c, t: (0, c)),
         pl.BlockSpec((1, S5_CB), lambda c, t: (0, c))],
        [pl.BlockSpec((tm, S5_CB), lambda c, t: (t, c))] * 2,
        [jax.ShapeDtypeStruct((rows, S5_LANES), f32)] * 2,
        [pltpu.VMEM((SUBLANES, S5_CB), f32), pltpu.VMEM((SUBLANES, S5_CB), f32),
         pltpu.VMEM((1, S5_CB), f32), pltpu.VMEM((1, S5_CB), f32),
         pltpu.VMEM((3, SUBLANES, S5_CB), f32), pltpu.VMEM((3, SUBLANES, S5_CB), f32)],
        ("parallel", "arbitrary"), (proj, b_re, b_im, a_re, a_im))


def _s5_scan_bwd(tag, dxr, dxi, xr, xi, a_re, a_im, rows):
    tm = min(ROW_TILE, rows)
    nt = rows // tm
    nc = S5_LANES // S5_CB

    def kern(dxr_ref, dxi_ref, xr_ref, xi_ref, ar_ref, ai_ref, gr_ref, gi_ref, dar_ref, dai_ref,
             qr_s, qi_s, cr_s, ci_s, gr_s, gi_s, mr_s, mi_s):
        t = pl.program_id(1)
        row = _rows((tm, S5_CB))
        ng = tm // SUBLANES

        @pl.when(t == 0)
        def _():
            row8 = _rows((SUBLANES, S5_CB))
            qr = jnp.broadcast_to(ar_ref[...], (SUBLANES, S5_CB))
            qi = jnp.broadcast_to(-ai_ref[...], (SUBLANES, S5_CB))
            s = 1
            while s < SUBLANES:
                sr = pltpu.roll(qr, SUBLANES - s, 0)
                si = pltpu.roll(qi, SUBLANES - s, 0)
                valid = row8 < SUBLANES - s
                qr, qi = jnp.where(valid, qr * sr - qi * si, qr), jnp.where(valid, qr * si + qi * sr, qi)
                s *= 2
            qr_s[...] = qr
            qi_s[...] = qi
            for k in range(3):
                s = 1 << k
                mr_s[k] = jnp.where(row8 < SUBLANES - s, qr[SUBLANES - s:SUBLANES - s + 1, :], 0.0)
                mi_s[k] = jnp.where(row8 < SUBLANES - s, qi[SUBLANES - s:SUBLANES - s + 1, :], 0.0)
            cr_s[...] = jnp.zeros_like(cr_s)
            ci_s[...] = jnp.zeros_like(ci_s)
            dar_ref[...] = jnp.zeros_like(dar_ref)
            dai_ref[...] = jnp.zeros_like(dai_ref)

        steps = [(mr_s[k], mi_s[k]) for k in range(3)]
        cr = cr_s[...]
        ci = ci_s[...]
        qr = qr_s[...]
        qi = qi_s[...]
        last8 = _rows((SUBLANES, S5_CB)) == SUBLANES - 1
        acc_r = jnp.zeros((SUBLANES, S5_CB), f32)
        acc_i = jnp.zeros((SUBLANES, S5_CB), f32)
        for g in reversed(range(ng)):
            sl = slice(g * SUBLANES, (g + 1) * SUBLANES)
            gr = dxr_ref[sl, :]
            gi = dxi_ref[sl, :]
            for k, (mr, mi) in enumerate(steps):
                sr = pltpu.roll(gr, SUBLANES - (1 << k), 0)
                si = pltpu.roll(gi, SUBLANES - (1 << k), 0)
                gr, gi = gr + (mr * sr - mi * si), gi + (mr * si + mi * sr)
            gr, gi = gr + (qr * cr - qi * ci), gi + (qr * ci + qi * cr)
            gr_s[sl, :] = gr
            gi_s[sl, :] = gi
            gnr = jnp.where(last8, cr, pltpu.roll(gr, SUBLANES - 1, 0))
            gni = jnp.where(last8, ci, pltpu.roll(gi, SUBLANES - 1, 0))
            xr_v = xr_ref[sl, :]
            xi_v = xi_ref[sl, :]
            acc_r = acc_r + (gnr * xr_v + gni * xi_v)
            acc_i = acc_i + (gni * xr_v - gnr * xi_v)
            cr = gr[0:1, :]
            ci = gi[0:1, :]
        cr_s[...] = cr
        ci_s[...] = ci
        gr_ref[...] = gr_s[...].astype(bf16)
        gi_ref[...] = gi_s[...].astype(bf16)
        dar_ref[...] += jnp.sum(acc_r, axis=0, keepdims=True)
        dai_ref[...] += jnp.sum(acc_i, axis=0, keepdims=True)

    rev = lambda c, t: (nt - 1 - t, c)
    return _call_with_rider(
        "s5_scan_bwd_" + tag, kern, (nc, nt),
        [pl.BlockSpec((tm, S5_CB), rev)] * 4 + [pl.BlockSpec((1, S5_CB), lambda c, t: (0, c))] * 2,
        [pl.BlockSpec((tm, S5_CB), rev)] * 2 + [pl.BlockSpec((1, S5_CB), lambda c, t: (0, c))] * 2,
        [jax.ShapeDtypeStruct((rows, S5_LANES), bf16)] * 2 + [jax.ShapeDtypeStruct((1, S5_LANES), f32)] * 2,
        [pltpu.VMEM((SUBLANES, S5_CB), f32), pltpu.VMEM((SUBLANES, S5_CB), f32),
         pltpu.VMEM((1, S5_CB), f32), pltpu.VMEM((1, S5_CB), f32),
         pltpu.VMEM((tm, S5_CB), f32), pltpu.VMEM((tm, S5_CB), f32),
         pltpu.VMEM((3, SUBLANES, S5_CB), f32), pltpu.VMEM((3, SUBLANES, S5_CB), f32)],
        ("parallel", "arbitrary"), (dxr, dxi, xr, xi, a_re, a_im))


def _s5_fwd(tag, proj, cst, rows):
    xr, xi = _s5_scan_fwd(tag, proj, cst["b_re"].astype(bf16), cst["b_im"].astype(bf16), cst["a_re"], cst["a_im"], rows)

    def body(i, xr_ref, xi_ref, u_ref, cre_ref, cim_ref, d_ref, gw_ref, gb_ref, y_ref, out_ref):
        y = (_dot(xr_ref[...].astype(bf16), cre_ref[...]) + _dot(xi_ref[...].astype(bf16), cim_ref[...])
             + d_ref[...] * u_ref[...])
        y_ref[...] = y
        z = _gelu(y)
        zg = _dot(z.astype(bf16), gw_ref[...]) + gb_ref[...]
        out_ref[...] = (z * _sigmoid(zg)).astype(bf16)

    y, out = _rt("s5_out_" + tag, body, rows, ROW_TILE,
                 [(xr, S5_LANES, 0), (xi, S5_LANES, 0), (proj, BRANCH, U_COL)],
                 [cst["c_re"].astype(bf16), cst["c_im"].astype(bf16), cst["s5_d"], cst["glu_w"], cst["glu_b"]],
                 [(BRANCH, f32), (BRANCH, bf16)])
    return out, (xr, xi, y)


def _s5_bwd(tag, saved, proj, cst, d_out, rows):
    xr, xi, y = saved
    c_re = cst["c_re"].astype(bf16)
    c_im = cst["c_im"].astype(bf16)

    def body(i, do_ref, y_ref, u_ref, xr_ref, xi_ref, cre_ref, cim_ref, gw_ref, gb_ref,
             dxr_ref, dxi_ref, dy_ref, dgw_ref, dgb_ref, dd_ref, dcre_ref, dcim_ref):
        yv = y_ref[...]
        z = _gelu(yv)
        zb = z.astype(bf16)
        gt = _sigmoid(_dot(zb, gw_ref[...]) + gb_ref[...])
        dov = do_ref[...]
        dzg = dov * z * gt * (1.0 - gt)
        dzgb = dzg.astype(bf16)
        dz = dov * gt + _dot_nt(dzgb, gw_ref[...])
        dgw_ref[...] += _dot_tn(zb, dzgb)
        dgb_ref[...] += jnp.sum(dzg, axis=0, keepdims=True)
        dy = dz * _gelu_grad(yv)
        dy_ref[...] = dy
        dd_ref[...] += jnp.sum(dy * u_ref[...], axis=0, keepdims=True)
        dyb = dy.astype(bf16)
        dxr_ref[...] = _dot_nt(dyb, cre_ref[...])
        dxi_ref[...] = _dot_nt(dyb, cim_ref[...])
        dcre_ref[...] += _dot_tn(xr_ref[...].astype(bf16), dyb)
        dcim_ref[...] += _dot_tn(xi_ref[...].astype(bf16), dyb)

    dxr, dxi, dy, d_gw, d_gb, d_d, d_cre, d_cim = _rt(
        "s5_out_bwd_" + tag, body, rows, ROW_TILE,
        [(d_out, BRANCH, 0), (y, BRANCH, 0), (proj, BRANCH, U_COL), (xr, S5_LANES, 0), (xi, S5_LANES, 0)],
        [c_re, c_im, cst["glu_w"], cst["glu_b"]],
        [(S5_LANES, f32), (S5_LANES, f32), (BRANCH, f32)],
        acc_outs=[(BRANCH, BRANCH), (1, BRANCH), (1, BRANCH), (S5_LANES, BRANCH), (S5_LANES, BRANCH)])

    gr, gi, d_ar, d_ai = _s5_scan_bwd(tag, dxr, dxi, xr, xi, cst["a_re"], cst["a_im"], rows)
    b_re = cst["b_re"].astype(bf16)
    b_im = cst["b_im"].astype(bf16)

    def body_in(i, gr_ref, gi_ref, dy_ref, u_ref, bre_ref, bim_ref, d_ref, du_ref, dbre_ref, dbim_ref):
        grv = gr_ref[...]
        giv = gi_ref[...]
        du = _dot_nt(grv, bre_ref[...]) + _dot_nt(giv, bim_ref[...]) + dy_ref[...] * d_ref[...]
        du_ref[...] = du.astype(bf16)
        ub = u_ref[...].astype(bf16)
        dbre_ref[...] += _dot_tn(ub, grv)
        dbim_ref[...] += _dot_tn(ub, giv)

    du, d_bre, d_bim = _rt("s5_in_bwd_" + tag, body_in, rows, ROW_TILE,
                           [(gr, S5_LANES, 0), (gi, S5_LANES, 0), (dy, BRANCH, 0), (proj, BRANCH, U_COL)],
                           [b_re, b_im, cst["s5_d"]], [(BRANCH, bf16)],
                           acc_outs=[(BRANCH, S5_LANES), (BRANCH, S5_LANES)])
    dcst = {"b_re": d_bre, "b_im": d_bim, "a_re": d_ar, "a_im": d_ai, "c_re": d_cre, "c_im": d_cim,
            "s5_d": d_d, "glu_b": d_gb}
    return du, dcst, d_gw


def _hg_prep(q, z, lb):
    qs = _sigmoid(q)
    qh = q * qs
    sg = _sigmoid_small(z)
    fg = lb + (1.0 - lb) * sg
    kk = (1.0 - lb) * (1.0 - sg)
    return qs, qh, sg, fg, kk


def _hg_fwd(tag, proj, cst, rows):
    tm = min(ROW_TILE, rows)
    c_sz = HG_CHUNK
    nch = tm // c_sz
    n_chunks = rows // c_sz

    def body(i, q_ref, z_ref, v_ref, g_ref, lb_ref, nw_ref, out_ref, o_ref, ss_ref, sn_ref, st_s):
        @pl.when(i == 0)
        def _():
            st_s[...] = jnp.zeros_like(st_s)

        lb = lb_ref[...]
        _, qh, sg, fg, kk = _hg_prep(q_ref[...], z_ref[...], lb)
        b = _seg_cumsum(jnp.log(fg), tm, c_sz)
        qhat = (qh * jnp.exp(b)).astype(bf16)
        khat = (kk * jnp.exp(-b)).astype(bf16)
        vb = v_ref[...].astype(bf16)
        b3 = b.reshape(nch, c_sz, BRANCH)
        bl3 = b3[:, c_sz - 1:c_sz, :]
        kdec = (kk.reshape(nch, c_sz, BRANCH) * jnp.exp(bl3 - b3)).astype(bf16)
        ebl = jnp.exp(bl3)
        tril = (lax.broadcasted_iota(jnp.int32, (nch, c_sz, c_sz), 1)
                >= lax.broadcasted_iota(jnp.int32, (nch, c_sz, c_sz), 2))
        o_heads = []
        for h in range(HG_HEADS):
            hl = slice(h * HG_DK, (h + 1) * HG_DK)
            q3 = qhat[:, hl].reshape(nch, c_sz, HG_DK)
            k3 = khat[:, hl].reshape(nch, c_sz, HG_DK)
            v3 = vb[:, hl].reshape(nch, c_sz, HG_DK)
            a_mat = jnp.where(tril, _bdot_nt(q3, k3), 0.0).astype(bf16)
            o3 = _bdot(a_mat, v3)
            st = st_s[hl, :]
            before = []
            for ci in range(nch):
                before.append(st.astype(bf16))
                st = st * ebl[ci][:, hl] + _dot_tn(v3[ci], kdec[ci][:, hl])
                sn_ref[ci, hl, :] = st.astype(bf16)
            st_s[hl, :] = st
            s3 = jnp.stack(before)
            ss_ref[:, hl, :] = s3
            o3 = o3 + _bdot_nt(q3, s3)
            o_heads.append(o3.reshape(tm, HG_DK))
        o = jnp.concatenate(o_heads, axis=1)
        o_ref[...] = o
        r = lax.rsqrt(_head_mean(o * o) + EPS)
        g = g_ref[...]
        out_ref[...] = (o * r * nw_ref[...] * (g * _sigmoid(g))).astype(bf16)

    out, o, ss, sn = _rt(
        "hg_fwd_" + tag, body, rows, tm,
        [(proj, BRANCH, U_COL + 1), (proj, BRANCH, U_COL + 2), (proj, BRANCH, U_COL + 3), (proj, BRANCH, U_COL + 4)],
        [cst["hg_lb"], cst["hg_nw"]],
        [(BRANCH, bf16), (BRANCH, f32),
         ((n_chunks, BRANCH, HG_DK), (nch, BRANCH, HG_DK), lambda t: (t, 0, 0), bf16),
         ((n_chunks, BRANCH, HG_DK), (nch, BRANCH, HG_DK), lambda t: (t, 0, 0), bf16)],
        scratch=[pltpu.VMEM((BRANCH, HG_DK), f32)])
    return out, (o, ss, sn)


def _hg_bwd(tag, saved, proj, cst, d_out, rows):
    o_saved, ss, sn = saved
    tm = min(ROW_TILE, rows)
    c_sz = HG_CHUNK
    nch = tm // c_sz

    def body(i, do_ref, q_ref, z_ref, v_ref, g_ref, o_ref, ss_ref, sn_ref, lb_ref, nw_ref,
             dq_ref, dz_ref, dv_ref, dg_ref, dlb_ref, dnw_ref, dst_s):
        @pl.when(i == 0)
        def _():
            dst_s[...] = jnp.zeros_like(dst_s)

        lb = lb_ref[...]
        q = q_ref[...]
        qs, qh, sg, fg, kk = _hg_prep(q, z_ref[...], lb)
        b = _seg_cumsum(jnp.log(fg), tm, c_sz)
        eb = jnp.exp(b)
        enb = jnp.exp(-b)
        qhat = (qh * eb).astype(bf16)
        khat = (kk * enb).astype(bf16)
        vb = v_ref[...].astype(bf16)
        b3 = b.reshape(nch, c_sz, BRANCH)
        bl3 = b3[:, c_sz - 1:c_sz, :]
        dec3 = jnp.exp(bl3 - b3)
        kdec = (kk.reshape(nch, c_sz, BRANCH) * dec3).astype(bf16)
        ebl = jnp.exp(bl3)
        g = g_ref[...]
        gs = _sigmoid(g)
        o = o_ref[...]
        r = lax.rsqrt(_head_mean(o * o) + EPS)
        oh = o * r
        nw = nw_ref[...]
        dov = do_ref[...]
        don = dov * (g * gs)
        dg_ref[...] = (dov * oh * nw * (gs * (1.0 + g * (1.0 - gs)))).astype(bf16)
        dnw_ref[...] += jnp.sum(don * oh, axis=0, keepdims=True)
        doh = don * nw
        d_o = r * (doh - oh * _head_mean(doh * oh))
        dob = d_o.astype(bf16)
        t_idx = lax.broadcasted_iota(jnp.int32, (nch, c_sz, c_sz), 1)
        s_idx = lax.broadcasted_iota(jnp.int32, (nch, c_sz, c_sz), 2)
        heads = []
        for h in range(HG_HEADS):
            hl = slice(h * HG_DK, (h + 1) * HG_DK)
            q3 = qhat[:, hl].reshape(nch, c_sz, HG_DK)
            k3 = khat[:, hl].reshape(nch, c_sz, HG_DK)
            v3 = vb[:, hl].reshape(nch, c_sz, HG_DK)
            do3 = dob[:, hl].reshape(nch, c_sz, HG_DK)
            s3 = ss_ref[:, hl, :]
            da_mat = jnp.where(t_idx >= s_idx, _bdot_nt(do3, v3), 0.0).astype(bf16)
            a_t = jnp.where(t_idx <= s_idx, _bdot_nt(k3, q3), 0.0).astype(bf16)
            da_t = jnp.where(t_idx <= s_idx, _bdot_nt(v3, do3), 0.0).astype(bf16)
            dqhat = _bdot(do3, s3) + _bdot(da_mat, k3)
            dkhat = _bdot(da_t, q3)
            dst = dst_s[hl, :]
            after = [None] * nch
            for ci in reversed(range(nch)):
                after[ci] = dst
                dst = dst * ebl[ci][:, hl] + _dot_tn(do3[ci], q3[ci])
            dst_s[hl, :] = dst
            ds3 = jnp.stack(after)
            ds3b = ds3.astype(bf16)
            dk_inter = _bdot(v3, ds3b) * dec3[:, :, hl]
            dv3 = _bdot(a_t, do3) + _bdot_nt(kdec[:, :, hl], ds3b)
            flux = jnp.sum(sn_ref[:, hl, :].astype(f32) * ds3, axis=1, keepdims=True)
            heads.append((dqhat.reshape(tm, HG_DK), dkhat.reshape(tm, HG_DK), dk_inter.reshape(tm, HG_DK),
                          dv3.reshape(tm, HG_DK), jnp.broadcast_to(flux, (nch, c_sz, HG_DK)).reshape(tm, HG_DK)))
        dqhat, dkhat, dk_inter, dv, flux = (jnp.concatenate(parts, axis=1) for parts in zip(*heads))
        dv_ref[...] = dv.astype(bf16)
        dqh = dqhat * eb
        dk = dkhat * enb + dk_inter
        db = qhat.astype(f32) * dqhat - khat.astype(f32) * dkhat - kk * dk_inter
        dlf = _seg_rev_cumsum(db, tm, c_sz) + flux
        tt = (1.0 - lb) * sg * (1.0 - sg)
        dz_ref[...] = (dlf * tt / fg - dk * tt).astype(bf16)
        dlb_ref[...] += jnp.sum(dlf * (1.0 - sg) / fg - dk * (1.0 - sg), axis=0, keepdims=True)
        dq_ref[...] = (dqh * (qs * (1.0 + q * (1.0 - qs)))).astype(bf16)

    dq, dz, dv, dg, d_lb, d_nw = _rt(
        "hg_bwd_" + tag, body, rows, tm,
        [(d_out, BRANCH, 0), (proj, BRANCH, U_COL + 1), (proj, BRANCH, U_COL + 2), (proj, BRANCH, U_COL + 3),
         (proj, BRANCH, U_COL + 4), (o_saved, BRANCH, 0), (ss, (nch, BRANCH, HG_DK), lambda t: (t, 0, 0)),
         (sn, (nch, BRANCH, HG_DK), lambda t: (t, 0, 0))],
        [cst["hg_lb"], cst["hg_nw"]],
        [(BRANCH, bf16)] * 4, acc_outs=[(1, BRANCH), (1, BRANCH)],
        scratch=[pltpu.VMEM((BRANCH, HG_DK), f32)],
        reverse=True)
    return dq, dz, dv, dg, {"hg_lb": d_lb, "hg_nw": d_nw}


def _rg_gates(xc, wa_ref, ba_ref, wx_ref, bx_ref, sp8):
    xcb = xc.astype(bf16)
    r = _sigmoid(_dot(xcb, wa_ref[...]) + ba_ref[...])
    ig = _sigmoid(_dot(xcb, wx_ref[...]) + bx_ref[...])
    la = -sp8 * r
    a = jnp.exp(la)
    mult = jnp.sqrt(-_expm1(2.0 * la))
    return xcb, r, ig, a, mult


def _rg_fwd(tag, proj, cst, rows):
    tm = min(ROW_TILE, rows)

    def body(i, xb_ref, gate_ref, cw_ref, cb_ref, wa_ref, ba_ref, wx_ref, bx_ref, sp_ref,
             out_ref, xc_ref, h_ref, hp_ref, prev_s, hc_s):
        @pl.when(i == 0)
        def _():
            prev_s[...] = jnp.zeros_like(prev_s)
            hc_s[...] = jnp.zeros_like(hc_s)

        row = _rows((tm, BRANCH))
        xb = xb_ref[...]
        prev = prev_s[...]
        xc = cb_ref[...] + cw_ref[3:4, :] * xb
        for j in range(1, 4):
            sh = jnp.where(row >= j, pltpu.roll(xb, j, 0), pltpu.roll(prev, j, 0))
            xc = xc + cw_ref[3 - j:4 - j, :] * sh
        prev_s[...] = xb
        xc_ref[...] = xc
        _, r, ig, a, mult = _rg_gates(xc, wa_ref, ba_ref, wx_ref, bx_ref, sp_ref[...])
        bb = mult * ig * xc
        hc = hc_s[...]
        row8 = _rows((SUBLANES, BRANCH))
        for g in range(tm // SUBLANES):
            sl = slice(g * SUBLANES, (g + 1) * SUBLANES)
            a_cum, h_loc = _scan_fwd(a[sl], bb[sl], SUBLANES)
            h = h_loc + a_cum * hc
            h_ref[sl, :] = h
            hp_ref[sl, :] = jnp.where(row8 >= 1, pltpu.roll(h, 1, 0), hc)
            hc = h[SUBLANES - 1:SUBLANES, :]
        hc_s[...] = hc
        out_ref[...] = (h_ref[...] * _gelu(gate_ref[...])).astype(bf16)

    out, xc, h, hp = _rt(
        "rg_fwd_" + tag, body, rows, tm,
        [(proj, BRANCH, U_COL + 5), (proj, BRANCH, U_COL + 6)],
        [cst["rg_cw"], cst["rg_cb"], cst["rg_wa"].astype(bf16), cst["rg_ba"], cst["rg_wx"].astype(bf16),
         cst["rg_bx"], cst["rg_sp8"]],
        [(BRANCH, bf16), (BRANCH, f32), (BRANCH, f32), (BRANCH, f32)],
        scratch=[pltpu.VMEM((tm, BRANCH), f32), pltpu.VMEM((1, BRANCH), f32)])
    return out, (xc, h, hp)


def _rg_bwd(tag, saved, proj, cst, d_out, rows):
    xc_saved, h_saved, hp_saved = saved
    tm = min(ROW_TILE, rows)

    def body(i, do_ref, xb_ref, gate_ref, xc_ref, h_ref, hp_ref, cw_ref, wa_ref, ba_ref, wx_ref, bx_ref, sp_ref,
             dxb_ref, dgate_ref, dcw_ref, dcb_ref, dwa_ref, dba_ref, dwx_ref, dbx_ref, dsp_ref,
             nxt_s, ec_s, gt_s):
        @pl.when(i == 0)
        def _():
            nxt_s[...] = jnp.zeros_like(nxt_s)
            ec_s[...] = jnp.zeros_like(ec_s)

        row = _rows((tm, BRANCH))
        xc = xc_ref[...]
        sp8 = sp_ref[...]
        xcb, r, ig, a, mult = _rg_gates(xc, wa_ref, ba_ref, wx_ref, bx_ref, sp8)
        gate = gate_ref[...]
        dov = do_ref[...]
        dh = dov * _gelu(gate)
        dgate_ref[...] = (dov * h_ref[...] * _gelu_grad(gate)).astype(bf16)
        adh = a * dh
        ec = ec_s[...]
        last8 = _rows((SUBLANES, BRANCH)) == SUBLANES - 1
        for g in reversed(range(tm // SUBLANES)):
            sl = slice(g * SUBLANES, (g + 1) * SUBLANES)
            a_cum, e_loc = _scan_bwd(a[sl], adh[sl], SUBLANES)
            e = e_loc + a_cum * ec
            gt_s[sl, :] = dh[sl] + jnp.where(last8, ec, pltpu.roll(e, SUBLANES - 1, 0))
            ec = e[0:1, :]
        ec_s[...] = ec
        g_tot = gt_s[...]
        d_a = g_tot * hp_ref[...]
        d_mult = g_tot * ig * xc
        d_ix = g_tot * mult
        d_ig = d_ix * xc
        d_xc = d_ix * ig
        d_la = d_a * a - d_mult * (a * a) / mult
        d_r = -d_la * sp8
        dsp_ref[...] += jnp.sum(-d_la * r, axis=0, keepdims=True)
        dzr = d_r * r * (1.0 - r)
        dzi = d_ig * ig * (1.0 - ig)
        dzrb = dzr.astype(bf16)
        dzib = dzi.astype(bf16)
        d_xc = d_xc + _dot_nt(dzrb, wa_ref[...]) + _dot_nt(dzib, wx_ref[...])
        dwa_ref[...] += _dot_tn(xcb, dzrb)
        dwx_ref[...] += _dot_tn(xcb, dzib)
        dba_ref[...] += jnp.sum(dzr, axis=0, keepdims=True)
        dbx_ref[...] += jnp.sum(dzi, axis=0, keepdims=True)
        dcb_ref[...] += jnp.sum(d_xc, axis=0, keepdims=True)
        nxt = nxt_s[...]
        xb = xb_ref[...]
        dxb = cw_ref[3:4, :] * d_xc
        dcw_ref[3:4, :] += jnp.sum(d_xc * xb, axis=0, keepdims=True)
        for j in range(1, 4):
            sh = jnp.where(row < tm - j, pltpu.roll(d_xc, tm - j, 0), pltpu.roll(nxt, tm - j, 0))
            dxb = dxb + cw_ref[3 - j:4 - j, :] * sh
            dcw_ref[3 - j:4 - j, :] += jnp.sum(sh * xb, axis=0, keepdims=True)
        nxt_s[...] = d_xc
        dxb_ref[...] = dxb.astype(bf16)

    wa = cst["rg_wa"].astype(bf16)
    wx = cst["rg_wx"].astype(bf16)
    dxb, dgate, d_cw, d_cb, d_wa, d_ba, d_wx, d_bx, d_sp = _rt(
        "rg_bwd_" + tag, body, rows, tm,
        [(d_out, BRANCH, 0), (proj, BRANCH, U_COL + 5), (proj, BRANCH, U_COL + 6), (xc_saved, BRANCH, 0),
         (h_saved, BRANCH, 0), (hp_saved, BRANCH, 0)],
        [cst["rg_cw"], wa, cst["rg_ba"], wx, cst["rg_bx"], cst["rg_sp8"]],
        [(BRANCH, bf16), (BRANCH, bf16)],
        acc_outs=[(4, BRANCH), (1, BRANCH), (BRANCH, BRANCH), (1, BRANCH), (BRANCH, BRANCH), (1, BRANCH), (1, BRANCH)],
        scratch=[pltpu.VMEM((tm, BRANCH), f32), pltpu.VMEM((1, BRANCH), f32), pltpu.VMEM((tm, BRANCH), f32)],
        reverse=True)
    dcst = {"rg_cw": d_cw, "rg_cb": d_cb, "rg_wa": d_wa, "rg_ba": d_ba, "rg_wx": d_wx, "rg_bx": d_bx, "rg_sp8": d_sp}
    return dxb, dgate, dcst


def _merge_fwd(tag, proj, outs, bp, rows):
    def body(i, ya_ref, yb_ref, yc_ref, gm_ref, p_ref, m_ref):
        acc = None
        for n, y_ref in enumerate((ya_ref, yb_ref, yc_ref)):
            up = _dot(y_ref[...], p_ref[n])
            term = _sigmoid(gm_ref[:, n * D_MODEL:(n + 1) * D_MODEL]) * up
            acc = term if acc is None else acc + term
        m_ref[...] = acc.astype(bf16)
    return _rt("merge_fwd_" + tag, body, rows, ROW_TILE,
               [(outs[0], BRANCH, 0), (outs[1], BRANCH, 0), (outs[2], BRANCH, 0), (proj, GM_WIDTH, 0)],
               [bp], [(D_MODEL, bf16)])[0]


def _merge_bwd(tag, proj, outs, bp, dmerged, rows):
    def body(i, dm_ref, ya_ref, yb_ref, yc_ref, gm_ref, p_ref, da_ref, db_ref, dc_ref, dgm_ref, dp_ref):
        dm = dm_ref[...]
        for n, (y_ref, dy_ref) in enumerate(((ya_ref, da_ref), (yb_ref, db_ref), (yc_ref, dc_ref))):
            yv = y_ref[...]
            up = _dot(yv, p_ref[n])
            gt = _sigmoid(gm_ref[:, n * D_MODEL:(n + 1) * D_MODEL])
            dup = (dm * gt).astype(bf16)
            dgm_ref[:, n * D_MODEL:(n + 1) * D_MODEL] = (dm * up * gt * (1.0 - gt)).astype(bf16)
            dy_ref[...] = _dot_nt(dup, p_ref[n])
            dp_ref[n] += _dot_tn(yv, dup)
    return _rt("merge_bwd_" + tag, body, rows, ROW_TILE,
               [(dmerged, D_MODEL, 0), (outs[0], BRANCH, 0), (outs[1], BRANCH, 0), (outs[2], BRANCH, 0),
                (proj, GM_WIDTH, 0)],
               [bp], [(BRANCH, f32), (BRANCH, f32), (BRANCH, f32), (GM_WIDTH, bf16)],
               acc_outs=[(N_BRANCH, BRANCH, D_MODEL)])


def _block_diag(blocks):
    g, r, c = blocks.shape
    on_diag = (lax.broadcasted_iota(jnp.int32, (g * r, g * c), 0) // r
               == lax.broadcasted_iota(jnp.int32, (g * r, g * c), 1) // c)
    tiled = jnp.broadcast_to(blocks.reshape(g * r, 1, c), (g * r, g, c)).reshape(g * r, g * c)
    return jnp.where(on_diag, tiled, 0.0)


def _prep_consts(sp):
    p = jax.nn.softmax(sp["hg_lb_logits"], axis=0)
    lower = jnp.cumsum(p, axis=0) - p[0]
    out = []
    for l in range(DEPTH):
        lr = jnp.minimum(sp["s5_lambda_re"][l], S5_EIG_MAX)
        li = sp["s5_lambda_im"][l]
        dt = jnp.exp(sp["s5_log_dt"][l])[:, None]
        mag = jnp.exp(lr * dt)
        ar = mag * jnp.cos(li * dt)
        ai = mag * jnp.sin(li * dt)
        den = lr * lr + li * li
        fr = ((ar - 1.0) * lr + ai * li) / den
        fi = (ai * lr - (ar - 1.0) * li) / den
        br, bi = sp["s5_b_re"][l], sp["s5_b_im"][l]
        bbr = fr[..., None] * br - fi[..., None] * bi
        bbi = fr[..., None] * bi + fi[..., None] * br
        c = {
            "a_re": ar.reshape(1, S5_LANES), "a_im": ai.reshape(1, S5_LANES),
            "b_re": _block_diag(bbr.transpose(0, 2, 1)), "b_im": _block_diag(bbi.transpose(0, 2, 1)),
            "c_re": _block_diag(sp["s5_c_re"][l].transpose(0, 2, 1)),
            "c_im": -_block_diag(sp["s5_c_im"][l].transpose(0, 2, 1)),
            "s5_d": sp["s5_d"][l][None], "glu_b": sp["s5_glu_b"][l][None],
            "hg_lb": lower[l][None], "hg_nw": sp["hg_norm_w"][l][None],
            "rg_cw": sp["rg_conv_w"][l], "rg_cb": sp["rg_conv_b"][l][None],
            "rg_wa": _block_diag(sp["rg_wa"][l]), "rg_ba": sp["rg_ba"][l][None],
            "rg_wx": _block_diag(sp["rg_wx"][l]), "rg_bx": sp["rg_bx"][l][None],
            "rg_sp8": (RG_C * jax.nn.softplus(-sp["rg_lambda"][l]))[None],
        }
        out.append(c)
    return out


def _mixer_fwd(tag, x, hb, w_in, bp, w_out, cst, next_nw, rows):
    proj = _mm1("mix_proj_" + tag, hb, w_in, "nn", rows, IN_TOTAL, D_MODEL, 512, IN_TOTAL // 4, D_MODEL)
    cst = dict(cst)
    out_a, sv_a = _s5_fwd(tag, proj, cst, rows)
    out_b, sv_b = _hg_fwd(tag, proj, cst, rows)
    out_c, sv_c = _rg_fwd(tag, proj, cst, rows)
    merged = _merge_fwd(tag, proj, (out_a, out_b, out_c), bp, rows)
    x_out, hb_out = _mm("mix_out_" + tag, [merged], [w_out], [(0, 0, 0)], 1, "nn", rows, D_MODEL, D_MODEL,
                        512, D_MODEL, D_MODEL, [f32, bf16], _residual_then_norm(1.0), extras=[x], vecs=[next_nw])
    return x_out, hb_out, (x, hb, proj, (out_a, out_b, out_c), merged, sv_a, sv_b, sv_c)


def _mixer_bwd(tag, saved, nw, w_in, bp, w_out, cst, dx, dxb, rows):
    x, hb, proj, outs, merged, sv_a, sv_b, sv_c = saved
    d_wout = _mm1("mix_dwout_" + tag, merged, dxb, "tn", D_MODEL, D_MODEL, rows, D_MODEL, D_MODEL, TOKEN_K,
                  out_dtype=bf16)
    dmerged = _mm1("mix_dmerged_" + tag, dxb, w_out, "nt", rows, D_MODEL, D_MODEL, 512, D_MODEL, D_MODEL)
    d_a, d_b, d_c, dgm, d_bp = _merge_bwd(tag, proj, outs, bp, dmerged, rows)
    dxbc, dgatec, dcst_c = _rg_bwd(tag, sv_c, proj, cst, d_c, rows)
    dq, dz, dv, dg, dcst_b = _hg_bwd(tag, sv_b, proj, cst, d_b, rows)
    du, dcst_a, d_glu_w = _s5_bwd(tag, sv_a, proj, cst, d_a, rows)
    dproj = jnp.concatenate([dgm, du, dq, dz, dv, dg, dxbc, dgatec], axis=1)
    d_win = _mm1("mix_dwin_" + tag, hb, dproj, "tn", D_MODEL, IN_TOTAL, rows, D_MODEL, IN_TOTAL // 4, TOKEN_K,
                 out_dtype=bf16)
    dx_in, dxb_in, d_nw = _mm("mix_dh_" + tag, [dproj], [w_in], [(0, 0, 0)], 1, "nt", rows, D_MODEL, IN_TOTAL,
                              512, D_MODEL, IN_TOTAL // 2, [f32, bf16], _norm_bwd_epilogue, extras=[x, dx], vecs=[nw],
                              n_part=1)
    dcst = {**dcst_a, **dcst_b, **dcst_c}
    return dx_in, dxb_in, jnp.sum(d_nw, axis=0), d_win, d_bp, d_wout, d_glu_w, dcst


def _local_step(x, target, weights_of, small, grads_done):
    rows = x.shape[0]
    consts, consts_vjp = jax.vjp(_prep_consts, small)
    norm_w = small["norm_w"]
    saved = []
    h = x
    hb = _rms_fwd("first_norm", x, norm_w[0, 0][None], rows)
    for l in range(DEPTH):
        t = str(l)
        after = [norm_w[l + 1, 0][None]] if l + 1 < DEPTH else []
        wa = weights_of(l, "a")
        h, hb, sv0 = _ffn_fwd(t + "a", h, hb, *wa, [norm_w[l, 1][None]], rows)
        wm = weights_of(l, "mix")
        cst = dict(consts[l])
        cst["glu_w"] = wm[3]
        h, hb, sv1 = _mixer_fwd(t, h, hb, *wm[:3], cst, norm_w[l, 2][None], rows)
        wb = weights_of(l, "b")
        h, hb, sv2 = _ffn_fwd(t + "b", h, hb, *wb, after, rows)
        saved.append((sv0, sv1, sv2, cst, wa, wm, wb))
    dx, dxb, loss, d_fnw = _loss_head(h, small["final_norm_w"][None], target, rows)
    d_norm = [None] * DEPTH
    d_consts = [None] * DEPTH
    for l in reversed(range(DEPTH)):
        t = str(l)
        sv0, sv1, sv2, cst, wa, wm, wb = saved[l]
        dx, dxb, dn2, dg1, du1, dd1 = _ffn_bwd(t + "b", sv2, norm_w[l, 2][None], *wb, dx, dxb, rows)
        grads_done(l, "b", [dg1, du1, dd1])
        dx, dxb, dn1, d_win, d_bp, d_wout, d_glu_w, dcst = _mixer_bwd(
            t, sv1, norm_w[l, 1][None], *wm[:3], cst, dx, dxb, rows)
        grads_done(l, "mix", [d_win, d_bp, d_wout, d_glu_w])
        dx, dxb, dn0, dg0, du0, dd0 = _ffn_bwd(t + "a", sv0, norm_w[l, 0][None], *wa, dx, dxb, rows)
        grads_done(l, "a", [dg0, du0, dd0])
        d_norm[l] = jnp.concatenate([dn0, dn1, dn2], axis=0)
        d_consts[l] = dcst
    (g_small,) = consts_vjp(d_consts)
    g_small = dict(g_small)
    g_small["norm_w"] = g_small["norm_w"] + jnp.stack(d_norm)
    g_small["final_norm_w"] = g_small["final_norm_w"] + d_fnw[0]
    return loss[0, 0], dx, g_small


def _other_chips(x, y):
    return [(1 - x, y), (x, 1 - y), (1 - x, 1 - y)]


def _gather_over_ici(shards):
    n = len(shards)

    def copies(in_refs, out_refs, send_sems, recv_sems):
        x, y, c = _place()
        cps = []
        for i in range(n):
            mine = out_refs[i].at[4 * x + 2 * y + c]
            cps.append(pltpu.make_async_copy(in_refs[i], mine, send_sems.at[5 * i + 4]))
            for k, to in enumerate([(x, y, 1 - c)] + [(px, py, c) for px, py in _other_chips(x, y)]):
                cps.append(pltpu.make_async_remote_copy(
                    src_ref=in_refs[i], dst_ref=mine, send_sem=send_sems.at[5 * i + k],
                    recv_sem=recv_sems.at[5 * i + k], device_id=to, device_id_type=MESH_IDS))
        return cps

    return _Carry(shards, [jax.ShapeDtypeStruct((N_DEV,) + s.shape, s.dtype) for s in shards], 5 * n, copies)


def _gather_forward(name, landings):
    n = len(landings)

    def body(*refs):
        in_refs, out_refs = refs[:n], refs[n:2 * n]
        send_sems, recv_sems = refs[2 * n:]
        x, y, c = _place()
        cps = []
        for i in range(n):
            for j, (px, py) in enumerate(_other_chips(x, y)):
                block = 4 * px + 2 * py + c
                cps.append(pltpu.make_async_remote_copy(
                    src_ref=in_refs[i].at[block], dst_ref=out_refs[i].at[block], send_sem=send_sems.at[3 * i + j],
                    recv_sem=recv_sems.at[3 * i + j], device_id=(x, y, 1 - c), device_id_type=MESH_IDS))
        for cp in cps:
            cp.start()
        for cp in cps:
            cp.wait()

    return pl.pallas_call(
        body, name=name, out_shape=[jax.ShapeDtypeStruct(a.shape, a.dtype) for a in landings],
        in_specs=[ANY_SPEC] * n, out_specs=[ANY_SPEC] * n, input_output_aliases={i: i for i in range(n)},
        scratch_shapes=[pltpu.SemaphoreType.DMA((3 * n,)), pltpu.SemaphoreType.DMA((3 * n,))],
    )(*landings)


def _all_gather(name, shards):
    n = len(shards)

    def body(*refs):
        x_refs, out_refs = refs[:n], refs[n:2 * n]
        send_sems, recv_sems, local_sems = refs[2 * n:]
        x, y, c = _place()
        me, sibling = (x, y, c), (x, y, 1 - c)
        chips = [(1 - x, y), (x, 1 - y), (1 - x, 1 - y)]

        def blk(i, px, py, pc):
            return out_refs[i].at[4 * px + 2 * py + pc]

        def copy(i, k, block, to, src=None):
            return pltpu.make_async_remote_copy(
                src_ref=blk(i, *block) if src is None else src, dst_ref=blk(i, *block),
                send_sem=send_sems.at[7 * i + k], recv_sem=recv_sems.at[7 * i + k], device_id=to,
                device_id_type=MESH_IDS)

        mine = [pltpu.make_async_copy(x_refs[i], blk(i, *me), local_sems.at[i]) for i in range(n)]
        for cp in mine:
            cp.start()
        first = []
        for i in range(n):
            first.append(copy(i, 0, me, sibling, src=x_refs[i]))
            first += [copy(i, 1 + j, me, (*chip, c), src=x_refs[i]) for j, chip in enumerate(chips)]
        for cp in first:
            cp.start()
        passed = []
        for j, chip in enumerate(chips):
            for i in range(n):
                copy(i, 1 + j, (*chip, c), me).wait_recv()
                fwd = copy(i, 4 + j, (*chip, c), sibling)
                fwd.start()
                passed.append(fwd)
        for i in range(n):
            copy(i, 0, sibling, me).wait_recv()
            for j, chip in enumerate(chips):
                copy(i, 4 + j, (*chip, 1 - c), me).wait_recv()
        for cp in first + passed:
            cp.wait_send()
        for cp in mine:
            cp.wait()

    return pl.pallas_call(
        body, name=name, out_shape=[jax.ShapeDtypeStruct((N_DEV,) + s.shape, s.dtype) for s in shards],
        in_specs=[ANY_SPEC] * n, out_specs=[ANY_SPEC] * n,
        scratch_shapes=[pltpu.SemaphoreType.DMA((7 * n,)), pltpu.SemaphoreType.DMA((7 * n,)),
                        pltpu.SemaphoreType.DMA((n,))],
    )(*shards)


def _row_tile(rows):
    return rows if rows <= 512 else next(t for t in range(512, 7, -8) if rows % t == 0)


def _sums_over_ici(chip_sums):
    n = len(chip_sums)

    def copies(in_refs, out_refs, send_sems, recv_sems):
        x, y, c = _place()
        return [pltpu.make_async_remote_copy(
            src_ref=in_refs[i].at[2 * px + py], dst_ref=out_refs[i].at[k], send_sem=send_sems.at[3 * i + k],
            recv_sem=recv_sems.at[3 * i + k], device_id=(px, py, c), device_id_type=MESH_IDS)
            for i in range(n) for k, (px, py) in enumerate(_other_chips(x, y))]

    return _Carry(chip_sums, [jax.ShapeDtypeStruct((3,) + t.shape[1:], t.dtype) for t in chip_sums], 3 * n, copies)


def _reduce_scatter(tag, parts, hosts=None):
    n = len(parts)
    _, _, c = _place()

    def body_pair(*refs):
        p_refs, got_refs = refs[:n], refs[n:2 * n]
        send_sems, recv_sems = refs[2 * n:]
        x, y, c = _place()
        cps = [pltpu.make_async_remote_copy(
            src_ref=p_refs[i].at[1 - c], dst_ref=got_refs[i], send_sem=send_sems.at[i], recv_sem=recv_sems.at[i],
            device_id=(x, y, 1 - c), device_id_type=MESH_IDS) for i in range(n)]
        for cp in cps:
            cp.start()
        for cp in cps:
            cp.wait()

    from_sibling = pl.pallas_call(
        body_pair, name="rs_pair_" + tag, out_shape=[jax.ShapeDtypeStruct(p.shape[1:], p.dtype) for p in parts],
        in_specs=[ANY_SPEC] * n, out_specs=[ANY_SPEC] * n,
        scratch_shapes=[pltpu.SemaphoreType.DMA((n,)), pltpu.SemaphoreType.DMA((n,))],
    )(*parts)

    chip_sums = []
    for i, (part, got) in enumerate(zip(parts, from_sibling)):
        _, _, r, cols = part.shape
        tr = _row_tile(r)

        def body_add(idx_ref, p_ref, g_ref, o_ref):
            o_ref[...] = (p_ref[...].astype(f32) + g_ref[...].astype(f32)).astype(o_ref.dtype)

        chip_sums.append(pl.pallas_call(
            body_add, name="rs_pair_sum_%s_%d" % (tag, i), out_shape=jax.ShapeDtypeStruct((4, r, cols), part.dtype),
            grid_spec=pltpu.PrefetchScalarGridSpec(
                num_scalar_prefetch=1, grid=(4, r // tr),
                in_specs=[pl.BlockSpec((None, None, tr, cols), lambda j, t, idx: (idx[0], j, t, 0)),
                          pl.BlockSpec((None, tr, cols), lambda j, t, idx: (j, t, 0))],
                out_specs=pl.BlockSpec((None, tr, cols), lambda j, t, idx: (j, t, 0))),
            compiler_params=_cparams(("parallel", "parallel")),
        )(jnp.stack([c]).astype(jnp.int32), part, got))

    others = [None] * n
    riding = set()
    for host, which in (hosts or {}).items():
        rider = _sums_over_ici([chip_sums[i] for i in which])
        _CARRIED[host] = rider
        for pos, i in enumerate(which):
            others[i] = functools.partial(lambda r, p: r.outs[p], rider, pos)
        riding.update(which)
    rest = [i for i in range(n) if i not in riding]
    if rest:
        alone = _sums_over_ici([chip_sums[i] for i in rest])

        def body_chips(*refs):
            k = len(rest)
            cps = alone.copies(refs[:k], refs[k:2 * k], *refs[2 * k:])
            for cp in cps:
                cp.start()
            for cp in cps:
                cp.wait()

        from_chips = pl.pallas_call(
            body_chips, name="rs_chips_" + tag, out_shape=alone.out_shapes,
            in_specs=[ANY_SPEC] * len(rest), out_specs=[ANY_SPEC] * len(rest), scratch_shapes=alone.sems(),
        )(*alone.ins)
        for pos, i in enumerate(rest):
            others[i] = functools.partial(lambda got: got, from_chips[pos])
    return list(zip(chip_sums, others))


def _own_index():
    x, y, _ = _place()
    return jnp.stack([2 * x + y]).astype(jnp.int32)


def _own_total(name, chip_sum, others):
    _, r, cols = chip_sum.shape
    tr = _row_tile(r)

    def body(idx_ref, t_ref, g_ref, o_ref):
        o_ref[...] = ((t_ref[...].astype(f32) + g_ref[0].astype(f32)) + g_ref[1].astype(f32)) + g_ref[2].astype(f32)

    return pl.pallas_call(
        body, name=name, out_shape=jax.ShapeDtypeStruct((r, cols), f32),
        grid_spec=pltpu.PrefetchScalarGridSpec(
            num_scalar_prefetch=1, grid=(r // tr,),
            in_specs=[pl.BlockSpec((None, tr, cols), lambda t, idx: (idx[0], t, 0)),
                      pl.BlockSpec((3, tr, cols), lambda t, idx: (0, t, 0))],
            out_specs=pl.BlockSpec((tr, cols), lambda t, idx: (t, 0))),
        compiler_params=_cparams(("parallel",)),
    )(_own_index(), chip_sum, others)


def _adam_update(w, gv, m, v):
    m_new = ADAM_B1 * m + (1.0 - ADAM_B1) * gv
    v_new = ADAM_B2 * v + (1.0 - ADAM_B2) * (gv * gv)
    m_hat = m_new / (1.0 - ADAM_B1 ** ADAM_STEP)
    v_hat = v_new / (1.0 - ADAM_B2 ** ADAM_STEP)
    return -ADAM_LR * (m_hat / (jnp.sqrt(v_hat) + ADAM_EPS) + ADAM_WD * w), m_new, v_new


def _adamw_reduced(name, w, pieces, m, v):
    n_p, rows, cols = w.shape
    tr = _row_tile(rows)

    def body(idx_ref, w_ref, *refs):
        red = refs[:2 * n_p]
        m_ref, v_ref, g_ref, d_ref, nm_ref, nv_ref = refs[2 * n_p:]
        p = pl.program_id(0)
        for q in range(n_p):
            @pl.when(p == q)
            def _(t_ref=red[2 * q], o_ref=red[2 * q + 1]):
                gv = ((t_ref[...].astype(f32) + o_ref[0].astype(f32)) + o_ref[1].astype(f32)) + o_ref[2].astype(f32)
                g_ref[...] = gv
                d_ref[...], nm_ref[...], nv_ref[...] = _adam_update(w_ref[...], gv, m_ref[...], v_ref[...])

    spec = pl.BlockSpec((None, tr, cols), lambda p, t, idx: (p, t, 0))
    red_specs, red_args = [], []
    for q, (chip_sum, others) in enumerate(pieces):
        red_specs.append(pl.BlockSpec((None, tr, cols), lambda p, t, idx, q=q: (idx[0], jnp.where(p == q, t, 0), 0)))
        red_specs.append(pl.BlockSpec((3, tr, cols), lambda p, t, idx, q=q: (0, jnp.where(p == q, t, 0), 0)))
        red_args += [chip_sum, others]
    return pl.pallas_call(
        body, name=name, out_shape=[jax.ShapeDtypeStruct((n_p, rows, cols), f32)] * 4,
        grid_spec=pltpu.PrefetchScalarGridSpec(
            num_scalar_prefetch=1, grid=(n_p, rows // tr),
            in_specs=[spec] + red_specs + [spec, spec], out_specs=[spec] * 4),
        compiler_params=_cparams(("parallel", "parallel")),
    )(_own_index(), w, *red_args, m, v)


def _adamw(name, w, g, m, v):
    rows, cols = w.shape
    tr = _row_tile(rows)

    def body(w_ref, g_ref, m_ref, v_ref, d_ref, nm_ref, nv_ref):
        d_ref[...], nm_ref[...], nv_ref[...] = _adam_update(w_ref[...], g_ref[...], m_ref[...], v_ref[...])

    spec = pl.BlockSpec((tr, cols), lambda i: (i, 0))
    return pl.pallas_call(
        body, name=name, grid=(rows // tr,), in_specs=[spec] * 4, out_specs=[spec] * 3,
        out_shape=[jax.ShapeDtypeStruct((rows, cols), f32)] * 3, compiler_params=_cparams(("parallel",)),
    )(w, g, m, v)


WEIGHT_NAMES = ["norm_w", "final_norm_w", "ffn_gate", "ffn_up", "ffn_down", "w_in", "branch_proj", "w_out",
                "s5_lambda_re", "s5_lambda_im", "s5_log_dt", "s5_b_re", "s5_b_im", "s5_c_re", "s5_c_im", "s5_d",
                "s5_glu_w", "s5_glu_b", "hg_lb_logits", "hg_norm_w", "rg_conv_w", "rg_conv_b", "rg_wa", "rg_ba",
                "rg_wx", "rg_bx", "rg_lambda"]
SHARDED = {"ffn_gate": (3, "gate"), "ffn_up": (3, "up"), "ffn_down": (2, "down"), "w_in": (2, "w_in"),
           "branch_proj": (3, "bp"), "w_out": (1, "w_out"), "s5_glu_w": (1, "glu_w"),
           "norm_w": (2, None), "rg_conv_w": (2, None)}
BIG = ["ffn_gate", "ffn_up", "ffn_down", "w_in", "branch_proj", "w_out", "s5_glu_w"]
PARTS = {"a": [("ffn_gate", 0, 1), ("ffn_up", 0, 1), ("ffn_down", 0, 0)],
         "b": [("ffn_gate", 1, 1), ("ffn_up", 1, 1), ("ffn_down", 1, 0)],
         "mix": [("w_in", None, 1), ("branch_proj", None, 2), ("w_out", None, 0), ("s5_glu_w", None, 0)]}
AG_HOSTS = {"ffn_up_0a": (0, "mix", [0]), "ffn_down_0a": (0, "mix", [1, 2, 3]), "mix_proj_0": (0, "b", [0, 1, 2]),
            "s5_out_0": (1, "a", [0]), "hg_fwd_0": (1, "a", [1]), "rg_fwd_0": (1, "a", [2]),
            "merge_fwd_0": (1, "b", [0]), "ffn_up_0b": (1, "b", [1]), "ffn_down_0b": (1, "b", [2]),
            "s5_scan_fwd_0": (1, "mix", [0, 1]), "mix_out_0": (1, "mix", [2, 3])}
RS_HOSTS = {(1, "b"): {"s5_scan_bwd_1": [0, 1, 2]},
            (1, "mix"): {"ffn_bwd_mid_1a": [1, 2, 3], "ffn_dh_1a": [0]},
            (1, "a"): {"ffn_dh_0b": [0, 1], "merge_bwd_0": [2]},
            (0, "b"): {"s5_scan_bwd_0": [0, 1, 2]},
            (0, "mix"): {"ffn_bwd_mid_0a": [1, 2, 3], "ffn_dh_0a": [0]}}
SMALL_SHARDED = ["norm_w", "rg_conv_w"]
REPLICATED = [n for n in WEIGHT_NAMES if n not in SHARDED]
LANES = 128


PACK_ROWS = 512


def _pack_rows(arrays, names):
    pieces = []
    for n in names:
        flat = arrays[n].reshape(-1)
        pieces.append(jnp.pad(flat, (0, -flat.shape[0] % LANES)).reshape(-1, LANES))
    rows = jnp.concatenate(pieces, axis=0)
    return jnp.pad(rows, ((0, -rows.shape[0] % PACK_ROWS), (0, 0)))


def _unpack_rows(rows, names, like):
    out, r0 = {}, 0
    for n in names:
        size = math.prod(like[n].shape)
        nrows = -(-size // LANES)
        out[n] = rows[r0:r0 + nrows].reshape(-1)[:size].reshape(like[n].shape)
        r0 += nrows
    return out


def _unshard(gathered, axis):
    g = jnp.moveaxis(gathered, 0, axis)
    shp = g.shape
    return g.reshape(shp[:axis] + (shp[axis] * shp[axis + 1],) + shp[axis + 2:])


RELAYOUT_ROWS = 256


def _column_runs(width, first_col):
    total = N_DEV * width
    runs = []
    for j in range(N_DEV):
        start = (width * j + first_col) % total
        head = min(width, total - start)
        runs.append((j, 0, start, head))
        if head < width:
            runs.append((j, head, 0, width - head))
    return runs


def _unshard_columns(name, gathered, first_col=0):
    _, r, c = gathered.shape
    tr = min(RELAYOUT_ROWS, r)
    runs = _column_runs(c, first_col)

    def body(g_ref, o_ref):
        for j, off, dst, length in runs:
            o_ref[:, dst:dst + length] = g_ref[j, :, off:off + length]

    return pl.pallas_call(
        body, name=name, grid=(r // tr,), in_specs=[pl.BlockSpec((N_DEV, tr, c), lambda i: (0, i, 0))],
        out_specs=pl.BlockSpec((tr, N_DEV * c), lambda i: (i, 0)),
        out_shape=jax.ShapeDtypeStruct((r, N_DEV * c), gathered.dtype), compiler_params=_cparams(("parallel",)),
    )(gathered)


def _columns_to_blocks(name, full, first_col=0):
    r, total = full.shape
    c = total // N_DEV
    tr = min(RELAYOUT_ROWS, r)
    runs = _column_runs(c, first_col)

    def body(x_ref, o_ref):
        for j, off, src, length in runs:
            o_ref[j % 2, j // 2, :, off:off + length] = x_ref[:, src:src + length].astype(bf16)

    return pl.pallas_call(
        body, name=name, grid=(r // tr,), in_specs=[pl.BlockSpec((tr, total), lambda i: (i, 0))],
        out_specs=pl.BlockSpec((2, 4, tr, c), lambda i: (0, 0, i, 0)),
        out_shape=jax.ShapeDtypeStruct((2, 4, r, c), bf16), compiler_params=_cparams(("parallel",)),
    )(full)


def _to_blocks(full, axis):
    shp = full.shape
    g = full.reshape(shp[:axis] + (4, 2, shp[axis] // N_DEV) + shp[axis + 1:])
    g = jnp.moveaxis(g, (axis, axis + 1), (1, 0))
    return g.reshape(2, 4, -1, g.shape[-1])


W_IN_SPLIT = IN_TOTAL - GM_WIDTH


def kernel(x, norm_w, final_norm_w, ffn_gate, ffn_up, ffn_down, w_in, branch_proj, w_out, s5_lambda_re, s5_lambda_im, s5_log_dt, s5_b_re, s5_b_im, s5_c_re, s5_c_im, s5_d, s5_glu_w, s5_glu_b, hg_lb_logits, hg_norm_w, rg_conv_w, rg_conv_b, rg_wa, rg_ba, rg_wx, rg_bx, rg_lambda, loss_target, m_norm_w, m_final_norm_w, m_ffn_gate, m_ffn_up, m_ffn_down, m_w_in, m_branch_proj, m_w_out, m_s5_lambda_re, m_s5_lambda_im, m_s5_log_dt, m_s5_b_re, m_s5_b_im, m_s5_c_re, m_s5_c_im, m_s5_d, m_s5_glu_w, m_s5_glu_b, m_hg_lb_logits, m_hg_norm_w, m_rg_conv_w, m_rg_conv_b, m_rg_wa, m_rg_ba, m_rg_wx, m_rg_bx, m_rg_lambda, v_norm_w, v_final_norm_w, v_ffn_gate, v_ffn_up, v_ffn_down, v_w_in, v_branch_proj, v_w_out, v_s5_lambda_re, v_s5_lambda_im, v_s5_log_dt, v_s5_b_re, v_s5_b_im, v_s5_c_re, v_s5_c_im, v_s5_d, v_s5_glu_w, v_s5_glu_b, v_hg_lb_logits, v_hg_norm_w, v_rg_conv_w, v_rg_conv_b, v_rg_wa, v_rg_ba, v_rg_wx, v_rg_bx, v_rg_lambda):
    w = dict(zip(WEIGHT_NAMES, (norm_w, final_norm_w, ffn_gate, ffn_up, ffn_down, w_in, branch_proj, w_out,
                                s5_lambda_re, s5_lambda_im, s5_log_dt, s5_b_re, s5_b_im, s5_c_re, s5_c_im, s5_d,
                                s5_glu_w, s5_glu_b, hg_lb_logits, hg_norm_w, rg_conv_w, rg_conv_b, rg_wa, rg_ba,
                                rg_wx, rg_bx, rg_lambda)))
    m = dict(zip(WEIGHT_NAMES, (m_norm_w, m_final_norm_w, m_ffn_gate, m_ffn_up, m_ffn_down, m_w_in, m_branch_proj,
                                m_w_out, m_s5_lambda_re, m_s5_lambda_im, m_s5_log_dt, m_s5_b_re, m_s5_b_im, m_s5_c_re,
                                m_s5_c_im, m_s5_d, m_s5_glu_w, m_s5_glu_b, m_hg_lb_logits, m_hg_norm_w, m_rg_conv_w,
                                m_rg_conv_b, m_rg_wa, m_rg_ba, m_rg_wx, m_rg_bx, m_rg_lambda)))
    v = dict(zip(WEIGHT_NAMES, (v_norm_w, v_final_norm_w, v_ffn_gate, v_ffn_up, v_ffn_down, v_w_in, v_branch_proj,
                                v_w_out, v_s5_lambda_re, v_s5_lambda_im, v_s5_log_dt, v_s5_b_re, v_s5_b_im, v_s5_c_re,
                                v_s5_c_im, v_s5_d, v_s5_glu_w, v_s5_glu_b, v_hg_lb_logits, v_hg_norm_w, v_rg_conv_w,
                                v_rg_conv_b, v_rg_wa, v_rg_ba, v_rg_wx, v_rg_bx, v_rg_lambda)))
    rows = x.shape[1]

    _CARRIED.clear()

    def shard_of(piece, l):
        n, k, _ = piece
        return (w[n][l] if k is None else w[n][l, k]).astype(bf16)

    def assemble(l, part, gathered):
        full = []
        for j, (piece, g) in enumerate(zip(PARTS[part], gathered)):
            tag = "unshard_%d%s%d" % (l, part, j)
            if piece[0] == "w_in":
                full.append(_unshard_columns(tag, g, first_col=GM_WIDTH))
            elif piece[0] == "branch_proj":
                full.append(_unshard_columns(tag, g.reshape(N_DEV, -1, g.shape[-1])).reshape(N_BRANCH, BRANCH, D_MODEL))
            elif piece[2] == g.ndim - 2:
                full.append(_unshard_columns(tag, g))
            else:
                full.append(_unshard(g, piece[2]))
        return full

    n_a = len(PARTS["a"])
    first = _all_gather("gather_weights", [shard_of(p, 0) for p in PARTS["a"]] + [w[n] for n in SMALL_SHARDED])
    small = {n: w[n] for n in REPLICATED}
    for n, g in zip(SMALL_SHARDED, first[n_a:]):
        small[n] = _unshard(g, SHARDED[n][0])
    riders = {}
    for host, (l, part, which) in AG_HOSTS.items():
        rider = _gather_over_ici([shard_of(PARTS[part][j], l) for j in which])
        _CARRIED[host] = rider
        riders.setdefault((l, part), []).append((which, rider))

    def weights_of(l, part):
        if (l, part) == (0, "a"):
            return assemble(l, part, first[:n_a])
        landed = [None] * len(PARTS[part])
        for which, rider in riders[l, part]:
            for j, buf in zip(which, rider.outs):
                landed[j] = buf
        return assemble(l, part, _gather_forward("gather_forward_%d%s" % (l, part), landed))

    sums = {}

    def blocks_of(l, part, grads):
        out = []
        for j, (piece, g) in enumerate(zip(PARTS[part], grads)):
            tag = "to_blocks_%d%s%d" % (l, part, j)
            if piece[0] == "w_in":
                out.append(_columns_to_blocks(tag, g, first_col=GM_WIDTH))
            elif piece[0] == "branch_proj":
                out.append(_columns_to_blocks(tag, g.reshape(-1, g.shape[-1])))
            elif piece[2] == g.ndim - 1:
                out.append(_columns_to_blocks(tag, g))
            else:
                out.append(_to_blocks(g, piece[2]).astype(bf16))
        return out

    last_grads = []

    def grads_done(l, part, grads):
        if (l, part) in RS_HOSTS:
            sums[l, part] = _reduce_scatter("%d%s" % (l, part), blocks_of(l, part, grads), hosts=RS_HOSTS[l, part])
        else:
            last_grads.extend(blocks_of(l, part, grads))

    loss_part, dx, g_small = _local_step(x[0], loss_target[0], weights_of, small, grads_done)
    loss = lax.psum(loss_part, ("x", "y", "c"))

    parts = last_grads + [_to_blocks(g_small[n], SHARDED[n][0]) for n in SMALL_SHARDED]
    rep_rows = _pack_rows(g_small, REPLICATED)
    rep_slice = rep_rows.shape[0] // N_DEV
    parts.append(rep_rows.reshape(4, 2, rep_slice, LANES).transpose(1, 0, 2, 3))
    last = _reduce_scatter("last", parts)
    sums[0, "a"] = last[:n_a]

    grads, delta, new_m, new_v = {}, {}, {}, {}

    def update(n, pieces):
        shp = w[n].shape
        view = (len(pieces), -1, shp[-1])
        res = _adamw_reduced("adamw_" + n, w[n].reshape(view), [(t, others()) for t, others in pieces],
                             m[n].reshape(view), v[n].reshape(view))
        grads[n], delta[n], new_m[n], new_v[n] = (r.reshape(shp) for r in res)

    for n in BIG:
        update(n, [sums[l, part][j] for l in range(DEPTH) for part in ("a", "b", "mix")
                   for j, piece in enumerate(PARTS[part]) if piece[0] == n])
    for j, n in enumerate(SMALL_SHARDED):
        update(n, [last[n_a + j]])
    rep_mine = _own_total("rs_total_small", last[-1][0], last[-1][1]())
    rep_grads = _all_gather("gather_small_grads", [rep_mine])[0].reshape(-1, LANES)
    res = _adamw("adamw_small", _pack_rows(w, REPLICATED), rep_grads, _pack_rows(m, REPLICATED), _pack_rows(v, REPLICATED))
    for dst, src in zip((grads, delta, new_m, new_v), (rep_grads,) + tuple(res)):
        dst.update(_unpack_rows(src, REPLICATED, w))

    return (loss, dx.reshape(x.shape), *[grads[n] for n in WEIGHT_NAMES], *[delta[n] for n in WEIGHT_NAMES],
            *[new_m[n] for n in WEIGHT_NAMES], *[new_v[n] for n in WEIGHT_NAMES])
```

```python
import functools
import math

import jax
import jax.numpy as jnp
from jax import lax
from jax.experimental import pallas as pl
from jax.experimental.pallas import tpu as pltpu

f32 = jnp.float32
bf16 = jnp.bfloat16

D_MODEL = 1024
DEPTH = 2
BRANCH = 512
N_BRANCH = 3
S5_GROUP = 16
S5_GROUPS = 32
S5_STATE = 64
S5_LANES = S5_GROUPS * S5_STATE
S5_EIG_MAX = -1e-4
HG_HEADS = 4
HG_DK = 128
HG_CHUNK = 32
RG_BLOCKS = 8
RG_BLOCK = 64
RG_C = 8.0
D_FF = 2816
EPS = 1e-6
IN_TOTAL = 6656
GM_WIDTH = N_BRANCH * D_MODEL
N_DEV = 8

ADAM_LR = 0.001
ADAM_B1 = 0.9
ADAM_B2 = 0.999
ADAM_EPS = 1e-08
ADAM_WD = 0.01
ADAM_STEP = 10

VMEM_LIMIT_V7X = 56 * 1024 * 1024
ROW_TILE = 256
FF_TILE = 1408
TOKEN_K = 2048
MXU_COLS = 256


def _cparams(sem):
    return pltpu.CompilerParams(dimension_semantics=sem, vmem_limit_bytes=VMEM_LIMIT_V7X)


MESH_IDS = pl.DeviceIdType.MESH
ANY_SPEC = pl.BlockSpec(memory_space=pl.ANY)


def _place():
    return lax.axis_index("x"), lax.axis_index("y"), lax.axis_index("c")


class _Carry:
    def __init__(self, ins, out_shapes, n_sems, copies):
        self.ins, self.out_shapes, self.n_sems, self.copies = list(ins), list(out_shapes), n_sems, copies
        self.outs = None

    def sems(self):
        return [pltpu.SemaphoreType.DMA((self.n_sems,)), pltpu.SemaphoreType.DMA((self.n_sems,))]

    def start(self, when, *riders):
        @pl.when(when)
        def _():
            for cp in self.copies(*riders):
                cp.start()

    def finish(self, when, *riders):
        @pl.when(when)
        def _():
            for cp in self.copies(*riders):
                cp.wait()


_CARRIED = {}


def _call_with_rider(name, body, grid, in_specs, out_specs, out_shape, scratch, semantics, args):
    carry = _CARRIED.pop(name, None)
    if carry is None:
        return pl.pallas_call(body, name=name, grid=grid, in_specs=in_specs, out_specs=out_specs,
                              out_shape=out_shape, scratch_shapes=scratch, compiler_params=_cparams(semantics))(*args)
    n_in, n_out, nci, nco = len(in_specs), len(out_specs), len(carry.ins), len(carry.out_shapes)

    def kern(*refs):
        ids = [pl.program_id(d) for d in range(len(grid))]
        own = refs[:n_in] + refs[n_in + nci:n_in + nci + n_out] + refs[n_in + nci + n_out + nco:-2]
        riders = (refs[n_in:n_in + nci], refs[n_in + nci + n_out:n_in + nci + n_out + nco]) + tuple(refs[-2:])
        carry.start(functools.reduce(jnp.logical_and, [p == 0 for p in ids]), *riders)
        body(*own)
        carry.finish(functools.reduce(jnp.logical_and, [p == g - 1 for p, g in zip(ids, grid)]), *riders)

    res = pl.pallas_call(
        kern, name=name, grid=grid, in_specs=list(in_specs) + [ANY_SPEC] * nci,
        out_specs=list(out_specs) + [ANY_SPEC] * nco, out_shape=list(out_shape) + carry.out_shapes,
        scratch_shapes=list(scratch) + carry.sems(), compiler_params=_cparams(("arbitrary",) * len(grid)),
    )(*args, *carry.ins)
    carry.outs = res[n_out:]
    return res[:n_out]


def _sigmoid(x):
    return 0.5 * jnp.tanh(0.5 * x) + 0.5


def _sigmoid_small(x):
    return 1.0 / (1.0 + jnp.exp(-x))


_GELU_C = math.sqrt(2.0 / math.pi)


def _gelu(x):
    t = jnp.tanh(_GELU_C * (x + 0.044715 * x * x * x))
    return 0.5 * x * (1.0 + t)


def _gelu_grad(x):
    t = jnp.tanh(_GELU_C * (x + 0.044715 * x * x * x))
    return 0.5 * (1.0 + t) + 0.5 * x * (1.0 - t * t) * _GELU_C * (1.0 + 3.0 * 0.044715 * x * x)


def _expm1(x):
    p = x * (1.0 + x * (0.5 + x * (1.0 / 6 + x * (1.0 / 24 + x * (1.0 / 120 + x * (1.0 / 720))))))
    return jnp.where(jnp.abs(x) < 0.3, p, jnp.exp(x) - 1.0)


def _dot(a, b):
    return jnp.dot(a, b, preferred_element_type=f32)


def _dot_nt(a, b):
    return lax.dot_general(a, b, (((1,), (1,)), ((), ())), preferred_element_type=f32)


def _dot_tn(a, b):
    return lax.dot_general(a, b, (((0,), (0,)), ((), ())), preferred_element_type=f32)


def _bdot(a, b):
    return lax.dot_general(a, b, (((2,), (1,)), ((0,), (0,))), preferred_element_type=f32)


def _bdot_nt(a, b):
    return lax.dot_general(a, b, (((2,), (2,)), ((0,), (0,))), preferred_element_type=f32)


def _rows(shape):
    return lax.broadcasted_iota(jnp.int32, shape, 0)


def _scan_fwd(a, b, n):
    row = _rows(a.shape)
    s = 1
    while s < n:
        valid = row >= s
        sh_a = pltpu.roll(a, s, 0)
        sh_b = pltpu.roll(b, s, 0)
        b = b + a * jnp.where(valid, sh_b, 0.0)
        a = a * jnp.where(valid, sh_a, 1.0)
        s *= 2
    return a, b


def _scan_bwd(a, b, n):
    row = _rows(a.shape)
    s = 1
    while s < n:
        valid = row < n - s
        sh_a = pltpu.roll(a, n - s, 0)
        sh_b = pltpu.roll(b, n - s, 0)
        b = b + a * jnp.where(valid, sh_b, 0.0)
        a = a * jnp.where(valid, sh_a, 1.0)
        s *= 2
    return a, b


def _seg_cumsum(x, n, seg):
    pos = _rows(x.shape) % seg
    s = 1
    while s < seg:
        x = x + jnp.where(pos >= s, pltpu.roll(x, s, 0), 0.0)
        s *= 2
    return x


def _seg_rev_cumsum(x, n, seg):
    pos = _rows(x.shape) % seg
    s = 1
    while s < seg:
        x = x + jnp.where(pos < seg - s, pltpu.roll(x, n - s, 0), 0.0)
        s *= 2
    return x


def _head_mean(x):
    parts = []
    for h in range(HG_HEADS):
        m = jnp.mean(x[:, h * HG_DK:(h + 1) * HG_DK], axis=1, keepdims=True)
        parts.append(jnp.broadcast_to(m, (x.shape[0], HG_DK)))
    return jnp.concatenate(parts, axis=1)


def _mm(name, a_list, b_list, terms, n_acc, mode, m, n, k, tm, tn, tk, out_dtypes, epilogue, extras=(), vecs=(),
        n_part=0, chunk=0):
    tm, tn, tk = min(tm, m), min(tn, n), min(tk, k)
    assert m % tm == 0 and n % tn == 0 and k % tk == 0, (name, m, n, k, tm, tn, tk)
    gk = k // tk
    if mode == "tn":
        a_spec = pl.BlockSpec((tk, tm), lambda i, j, kk: (kk, i))
    else:
        a_spec = pl.BlockSpec((tm, tk), lambda i, j, kk: (i, kk))
    if mode == "nt":
        b_spec = pl.BlockSpec((tn, tk), lambda i, j, kk: (j, kk))
    else:
        b_spec = pl.BlockSpec((tk, tn), lambda i, j, kk: (kk, j))
    o_spec = pl.BlockSpec((tm, tn), lambda i, j, kk: (i, j))
    v_spec = pl.BlockSpec((1, tn), lambda i, j, kk: (0, j))
    p_spec = pl.BlockSpec((None, 1, tn), lambda i, j, kk: (i, 0, j))
    dot = {"nn": _dot, "nt": _dot_nt, "tn": _dot_tn}[mode]
    na, nb, ne, nv, no = len(a_list), len(b_list), len(extras), len(vecs), len(out_dtypes)
    carry = _CARRIED.pop(name, None)
    nci, nco = (len(carry.ins), len(carry.out_shapes)) if carry else (0, 0)
    n_in = na + nb + ne + nv + nci
    grid = (m // tm, n // tn, gk)

    def kern(*refs):
        if carry:
            ids = [pl.program_id(d) for d in range(3)]
            riders = (refs[n_in - nci:n_in], refs[n_in + no + n_part:n_in + no + n_part + nco]) + tuple(refs[-2:])
            carry.start(functools.reduce(jnp.logical_and, [p == 0 for p in ids]), *riders)
        compute(*refs)
        if carry:
            carry.finish(functools.reduce(jnp.logical_and, [p == g - 1 for p, g in zip(ids, grid)]), *riders)

    def compute(*refs):
        a_refs = refs[:na]
        b_refs = refs[na:na + nb]
        e_refs = refs[na + nb:na + nb + ne]
        v_refs = refs[na + nb + ne:na + nb + ne + nv]
        o_refs = refs[n_in:n_in + no + n_part]

        def finish(accs):
            outs = epilogue(accs, [e[...] for e in e_refs], [r[...] for r in v_refs])
            for o, val in zip(o_refs, outs):
                o[...] = val.astype(o.dtype)

        def partial_sums():
            sums = [None] * n_acc
            for ai, bi, ci in terms:
                d = dot(a_refs[ai][...].astype(bf16), b_refs[bi][...].astype(bf16))
                sums[ci] = d if sums[ci] is None else sums[ci] + d
            return sums

        if gk == 1 and chunk:
            assert mode in ("nn", "nt") and tn % chunk == 0
            for c0 in range(0, tn, chunk):
                cols = slice(c0, c0 + chunk)
                sums = [None] * n_acc
                for ai, bi, ci in terms:
                    b_part = b_refs[bi][:, cols] if mode == "nn" else b_refs[bi][cols, :]
                    d = dot(a_refs[ai][...].astype(bf16), b_part.astype(bf16))
                    sums[ci] = d if sums[ci] is None else sums[ci] + d
                outs = epilogue(sums, [e[:, cols] for e in e_refs], [r[:, cols] for r in v_refs])
                for o, val in zip(o_refs, outs):
                    o[:, cols] = val.astype(o.dtype)
            return
        if gk == 1:
            finish(partial_sums())
            return
        acc = refs[n_in + no + n_part + nco]
        kk = pl.program_id(2)

        @pl.when(kk == 0)
        def _():
            acc[...] = jnp.zeros_like(acc)

        for ci, d in enumerate(partial_sums()):
            acc[ci] += d

        @pl.when(kk == gk - 1)
        def _():
            finish([acc[c] for c in range(n_acc)])

    res = pl.pallas_call(
        kern, name=name,
        grid=grid,
        in_specs=[a_spec] * na + [b_spec] * nb + [o_spec] * ne + [v_spec] * nv + [ANY_SPEC] * nci,
        out_specs=[o_spec] * no + [p_spec] * n_part + [ANY_SPEC] * nco,
        out_shape=([jax.ShapeDtypeStruct((m, n), dt) for dt in out_dtypes]
                   + [jax.ShapeDtypeStruct((m // tm, 1, n), f32)] * n_part + (carry.out_shapes if carry else [])),
        scratch_shapes=([pltpu.VMEM((n_acc, tm, tn), f32)] if gk > 1 else []) + (carry.sems() if carry else []),
        compiler_params=_cparams(("arbitrary",) * 3 if carry else ("parallel", "parallel", "arbitrary")),
    )(*a_list, *b_list, *extras, *vecs, *(carry.ins if carry else []))
    if carry:
        carry.outs = res[no + n_part:]
        res = res[:no + n_part]
    return res


def _mm1(name, a, b, mode, m, n, k, tm, tn, tk, out_dtype=f32, scale=None):
    def epi(accs, extras, vecs):
        return [accs[0] if scale is None else accs[0] * scale]
    return _mm(name, [a], [b], [(0, 0, 0)], 1, mode, m, n, k, tm, tn, tk, [out_dtype], epi)[0]


def _rt(name, body, rows, tm, row_ins, consts, row_outs, acc_outs=(), scratch=(), reverse=False):
    tm = min(tm, rows)
    assert rows % tm == 0
    nt = rows // tm

    def tile(i):
        return nt - 1 - i if reverse else i

    in_specs, args = [], []
    for spec in row_ins:
        arr = spec[0]
        if isinstance(spec[1], int):
            in_specs.append(pl.BlockSpec((tm, spec[1]), lambda i, cb=spec[2]: (tile(i), cb)))
        else:
            in_specs.append(pl.BlockSpec(spec[1], lambda i, fn=spec[2]: fn(tile(i))))
        args.append(arr)
    for c in consts:
        in_specs.append(pl.BlockSpec(c.shape, lambda i, nd=c.ndim: (0,) * nd))
        args.append(c)
    out_specs, out_shape = [], []
    for spec in row_outs:
        if isinstance(spec[0], int):
            out_specs.append(pl.BlockSpec((tm, spec[0]), lambda i: (tile(i), 0)))
            out_shape.append(jax.ShapeDtypeStruct((rows, spec[0]), spec[1]))
        else:
            out_specs.append(pl.BlockSpec(spec[1], lambda i, fn=spec[2]: fn(tile(i))))
            out_shape.append(jax.ShapeDtypeStruct(spec[0], spec[3]))
    for shp in acc_outs:
        out_specs.append(pl.BlockSpec(shp, lambda i, nd=len(shp): (0,) * nd))
        out_shape.append(jax.ShapeDtypeStruct(shp, f32))
    n_in = len(args)
    n_row_out = len(row_outs)
    n_acc = len(acc_outs)
    n_out = n_row_out + n_acc
    carry = _CARRIED.pop(name, None)
    nci, nco = (len(carry.ins), len(carry.out_shapes)) if carry else (0, 0)

    def kern(*refs):
        i = pl.program_id(0)
        if carry:
            own = refs[:n_in] + refs[n_in + nci:n_in + nci + n_out] + refs[n_in + nci + n_out + nco:-2]
            riders = (refs[n_in:n_in + nci], refs[n_in + nci + n_out:n_in + nci + n_out + nco]) + tuple(refs[-2:])
            carry.start(i == 0, *riders)
        else:
            own = refs
        acc_refs = own[n_in + n_row_out:n_in + n_out]

        @pl.when(i == 0)
        def _():
            for r in acc_refs:
                r[...] = jnp.zeros_like(r)

        body(i, *own)
        if carry:
            carry.finish(i == nt - 1, *riders)

    res = pl.pallas_call(
        kern, name=name, grid=(nt,), in_specs=in_specs + [ANY_SPEC] * nci, out_specs=out_specs + [ANY_SPEC] * nco,
        out_shape=out_shape + (carry.out_shapes if carry else []),
        scratch_shapes=list(scratch) + (carry.sems() if carry else []), compiler_params=_cparams(("arbitrary",)),
    )(*args, *(carry.ins if carry else []))
    if carry:
        carry.outs = res[n_out:]
        res = res[:n_out]
    return res


def _rms_rows(xv, wv):
    r = lax.rsqrt(jnp.mean(xv * xv, axis=1, keepdims=True) + EPS)
    return (xv * r * wv).astype(bf16)


def _rms_bwd_rows(xv, dhv, wv, dres):
    r = lax.rsqrt(jnp.mean(xv * xv, axis=1, keepdims=True) + EPS)
    xn = xv * r
    dxn = dhv * wv
    dx = dres + r * (dxn - xn * jnp.mean(dxn * xn, axis=1, keepdims=True))
    return [dx, dx.astype(bf16), jnp.sum(dhv * xn, axis=0, keepdims=True)]


def _rms_fwd(name, x, w, rows):
    def body(i, x_ref, w_ref, h_ref):
        h_ref[...] = _rms_rows(x_ref[...], w_ref[...])
    return _rt(name, body, rows, ROW_TILE, [(x, D_MODEL, 0)], [w], [(D_MODEL, bf16)])[0]


def _residual_then_norm(scale):
    def epi(accs, extras, vecs):
        x_out = extras[0] + scale * accs[0]
        return [x_out] + [_rms_rows(x_out, v) for v in vecs]
    return epi


def _norm_bwd_epilogue(accs, extras, vecs):
    return _rms_bwd_rows(extras[0], accs[0], vecs[0], extras[1])


def _loss_head(x, w, target, rows):
    def body(i, x_ref, t_ref, w_ref, dx_ref, dxb_ref, loss_ref, dw_ref):
        xv = x_ref[...]
        r = lax.rsqrt(jnp.mean(xv * xv, axis=1, keepdims=True) + EPS)
        xn = xv * r
        wv = w_ref[...]
        err = xn * wv - t_ref[...]
        part = 0.5 * jnp.sum(jnp.mean(err * err, axis=1, keepdims=True), axis=0, keepdims=True)
        loss_ref[...] += jnp.broadcast_to(part, (1, 128))
        dy = err * (1.0 / D_MODEL)
        dxn = dy * wv
        dx = r * (dxn - xn * jnp.mean(dxn * xn, axis=1, keepdims=True))
        dx_ref[...] = dx
        dxb_ref[...] = dx.astype(bf16)
        dw_ref[...] += jnp.sum(dy * xn, axis=0, keepdims=True)
    return _rt("loss_head", body, rows, ROW_TILE, [(x, D_MODEL, 0), (target, D_MODEL, 0)], [w],
               [(D_MODEL, f32), (D_MODEL, bf16)], acc_outs=[(1, 128), (1, D_MODEL)])


def _ffn_fwd(tag, x, hb, wg, wu, wd, next_nw, rows):
    def epi_up(accs, extras, vecs):
        a, b = accs
        return [a, b, a * _sigmoid(a) * b]
    a, b, s = _mm("ffn_up_" + tag, [hb], [wg, wu], [(0, 0, 0), (0, 1, 1)], 2, "nn", rows, D_FF, D_MODEL,
                  512, D_FF, D_MODEL, [bf16, bf16, bf16], epi_up, chunk=MXU_COLS)
    outs = _mm("ffn_down_" + tag, [s], [wd], [(0, 0, 0)], 1, "nn", rows, D_MODEL, D_FF,
               512, D_MODEL, D_FF, [f32] + [bf16] * len(next_nw), _residual_then_norm(0.5), extras=[x],
               vecs=next_nw)
    return outs[0], (outs[1] if next_nw else None), (x, hb, a, b, s)


def _ffn_bwd(tag, saved, nw, wg, wu, wd, dx, dxb, rows):
    x, hb, a, b, s = saved

    def epi_mid(accs, extras, vecs):
        ds = 0.5 * accs[0]
        av = extras[0].astype(f32)
        bv = extras[1].astype(f32)
        sg = _sigmoid(av)
        return [ds * bv * sg * (1.0 + av * (1.0 - sg)), ds * av * sg]
    da, db = _mm("ffn_bwd_mid_" + tag, [dxb], [wd], [(0, 0, 0)], 1, "nt", rows, D_FF, D_MODEL,
                 512, D_FF, D_MODEL, [bf16, bf16], epi_mid, extras=[a, b], chunk=MXU_COLS)
    d_wd = _mm1("ffn_dwd_" + tag, s, dxb, "tn", D_FF, D_MODEL, rows, FF_TILE, D_MODEL, TOKEN_K, out_dtype=bf16,
                scale=0.5)
    d_wg = _mm1("ffn_dwg_" + tag, hb, da, "tn", D_MODEL, D_FF, rows, D_MODEL, FF_TILE, TOKEN_K, out_dtype=bf16)
    d_wu = _mm1("ffn_dwu_" + tag, hb, db, "tn", D_MODEL, D_FF, rows, D_MODEL, FF_TILE, TOKEN_K, out_dtype=bf16)
    dx_in, dxb_in, d_nw = _mm("ffn_dh_" + tag, [da, db], [wg, wu], [(0, 0, 0), (1, 1, 0)], 1, "nt", rows, D_MODEL,
                              D_FF, 512, D_MODEL, D_FF, [f32, bf16], _norm_bwd_epilogue, extras=[x, dx], vecs=[nw],
                              n_part=1)
    return dx_in, dxb_in, jnp.sum(d_nw, axis=0), d_wg, d_wu, d_wd


S5_CB = 512
SUBLANES = 8
U_COL = GM_WIDTH // BRANCH


def _s5_scan_fwd(tag, proj, b_re, b_im, a_re, a_im, rows):
    tm = min(ROW_TILE, rows)
    nt = rows // tm
    nc = S5_LANES // S5_CB

    def kern(u_ref, bre_ref, bim_ref, ar_ref, ai_ref, xr_ref, xi_ref, pr_s, pi_s, cr_s, ci_s, mr_s, mi_s):
        t = pl.program_id(1)

        @pl.when(t == 0)
        def _():
            row8 = _rows((SUBLANES, S5_CB))
            pr = jnp.broadcast_to(ar_ref[...], (SUBLANES, S5_CB))
            pi = jnp.broadcast_to(ai_ref[...], (SUBLANES, S5_CB))
            s = 1
            while s < SUBLANES:
                sr = pltpu.roll(pr, s, 0)
                si = pltpu.roll(pi, s, 0)
                valid = row8 >= s
                pr, pi = jnp.where(valid, pr * sr - pi * si, pr), jnp.where(valid, pr * si + pi * sr, pi)
                s *= 2
            pr_s[...] = pr
            pi_s[...] = pi
            for k in range(3):
                s = 1 << k
                mr_s[k] = jnp.where(row8 >= s, pr[s - 1:s, :], 0.0)
                mi_s[k] = jnp.where(row8 >= s, pi[s - 1:s, :], 0.0)
            cr_s[...] = jnp.zeros_like(cr_s)
            ci_s[...] = jnp.zeros_like(ci_s)

        ub = u_ref[...].astype(bf16)
        br = _dot(ub, bre_ref[...])
        bi = _dot(ub, bim_ref[...])
        steps = [(mr_s[k], mi_s[k]) for k in range(3)]
        cr = cr_s[...]
        ci = ci_s[...]
        pr = pr_s[...]
        pi = pi_s[...]
        for g in range(tm // SUBLANES):
            sl = slice(g * SUBLANES, (g + 1) * SUBLANES)
            xr = br[sl]
            xi = bi[sl]
            for k, (mr, mi) in enumerate(steps):
                sr = pltpu.roll(xr, 1 << k, 0)
                si = pltpu.roll(xi, 1 << k, 0)
                xr, xi = xr + (mr * sr - mi * si), xi + (mr * si + mi * sr)
            xr, xi = xr + (pr * cr - pi * ci), xi + (pr * ci + pi * cr)
            xr_ref[sl, :] = xr
            xi_ref[sl, :] = xi
            cr = xr[SUBLANES - 1:SUBLANES, :]
            ci = xi[SUBLANES - 1:SUBLANES, :]
        cr_s[...] = cr
        ci_s[...] = ci

    return _call_with_rider(
        "s5_scan_fwd_" + tag, kern, (nc, nt),
        [pl.BlockSpec((tm, BRANCH), lambda c, t: (t, U_COL)),
         pl.BlockSpec((BRANCH, S5_CB), lambda c, t: (0, c)),
         pl.BlockSpec((BRANCH, S5_CB), lambda c, t: (0, c)),
         pl.BlockSpec((1, S5_CB), lambda c, t: (0, c)),
         pl.BlockSpec((1, S5_CB), lambda c, t: (0, c))],
        [pl.BlockSpec((tm, S5_CB), lambda c, t: (t, c))] * 2,
        [jax.ShapeDtypeStruct((rows, S5_LANES), f32)] * 2,
        [pltpu.VMEM((SUBLANES, S5_CB), f32), pltpu.VMEM((SUBLANES, S5_CB), f32),
         pltpu.VMEM((1, S5_CB), f32), pltpu.VMEM((1, S5_CB), f32),
         pltpu.VMEM((3, SUBLANES, S5_CB), f32), pltpu.VMEM((3, SUBLANES, S5_CB), f32)],
        ("parallel", "arbitrary"), (proj, b_re, b_im, a_re, a_im))


def _s5_scan_bwd(tag, dxr, dxi, xr, xi, a_re, a_im, rows):
    tm = min(ROW_TILE, rows)
    nt = rows // tm
    nc = S5_LANES // S5_CB

    def kern(dxr_ref, dxi_ref, xr_ref, xi_ref, ar_ref, ai_ref, gr_ref, gi_ref, dar_ref, dai_ref,
             qr_s, qi_s, cr_s, ci_s, gr_s, gi_s, mr_s, mi_s):
        t = pl.program_id(1)
        row = _rows((tm, S5_CB))
        ng = tm // SUBLANES

        @pl.when(t == 0)
        def _():
            row8 = _rows((SUBLANES, S5_CB))
            qr = jnp.broadcast_to(ar_ref[...], (SUBLANES, S5_CB))
            qi = jnp.broadcast_to(-ai_ref[...], (SUBLANES, S5_CB))
            s = 1
            while s < SUBLANES:
                sr = pltpu.roll(qr, SUBLANES - s, 0)
                si = pltpu.roll(qi, SUBLANES - s, 0)
                valid = row8 < SUBLANES - s
                qr, qi = jnp.where(valid, qr * sr - qi * si, qr), jnp.where(valid, qr * si + qi * sr, qi)
                s *= 2
            qr_s[...] = qr
            qi_s[...] = qi
            for k in range(3):
                s = 1 << k
                mr_s[k] = jnp.where(row8 < SUBLANES - s, qr[SUBLANES - s:SUBLANES - s + 1, :], 0.0)
                mi_s[k] = jnp.where(row8 < SUBLANES - s, qi[SUBLANES - s:SUBLANES - s + 1, :], 0.0)
            cr_s[...] = jnp.zeros_like(cr_s)
            ci_s[...] = jnp.zeros_like(ci_s)
            dar_ref[...] = jnp.zeros_like(dar_ref)
            dai_ref[...] = jnp.zeros_like(dai_ref)

        steps = [(mr_s[k], mi_s[k]) for k in range(3)]
        cr = cr_s[...]
        ci = ci_s[...]
        qr = qr_s[...]
        qi = qi_s[...]
        last8 = _rows((SUBLANES, S5_CB)) == SUBLANES - 1
        acc_r = jnp.zeros((SUBLANES, S5_CB), f32)
        acc_i = jnp.zeros((SUBLANES, S5_CB), f32)
        for g in reversed(range(ng)):
            sl = slice(g * SUBLANES, (g + 1) * SUBLANES)
            gr = dxr_ref[sl, :]
            gi = dxi_ref[sl, :]
            for k, (mr, mi) in enumerate(steps):
                sr = pltpu.roll(gr, SUBLANES - (1 << k), 0)
                si = pltpu.roll(gi, SUBLANES - (1 << k), 0)
                gr, gi = gr + (mr * sr - mi * si), gi + (mr * si + mi * sr)
            gr, gi = gr + (qr * cr - qi * ci), gi + (qr * ci + qi * cr)
            gr_s[sl, :] = gr
            gi_s[sl, :] = gi
            gnr = jnp.where(last8, cr, pltpu.roll(gr, SUBLANES - 1, 0))
            gni = jnp.where(last8, ci, pltpu.roll(gi, SUBLANES - 1, 0))
            xr_v = xr_ref[sl, :]
            xi_v = xi_ref[sl, :]
            acc_r = acc_r + (gnr * xr_v + gni * xi_v)
            acc_i = acc_i + (gni * xr_v - gnr * xi_v)
            cr = gr[0:1, :]
            ci = gi[0:1, :]
        cr_s[...] = cr
        ci_s[...] = ci
        gr_ref[...] = gr_s[...].astype(bf16)
        gi_ref[...] = gi_s[...].astype(bf16)
        dar_ref[...] += jnp.sum(acc_r, axis=0, keepdims=True)
        dai_ref[...] += jnp.sum(acc_i, axis=0, keepdims=True)

    rev = lambda c, t: (nt - 1 - t, c)
    return _call_with_rider(
        "s5_scan_bwd_" + tag, kern, (nc, nt),
        [pl.BlockSpec((tm, S5_CB), rev)] * 4 + [pl.BlockSpec((1, S5_CB), lambda c, t: (0, c))] * 2,
        [pl.BlockSpec((tm, S5_CB), rev)] * 2 + [pl.BlockSpec((1, S5_CB), lambda c, t: (0, c))] * 2,
        [jax.ShapeDtypeStruct((rows, S5_LANES), bf16)] * 2 + [jax.ShapeDtypeStruct((1, S5_LANES), f32)] * 2,
        [pltpu.VMEM((SUBLANES, S5_CB), f32), pltpu.VMEM((SUBLANES, S5_CB), f32),
         pltpu.VMEM((1, S5_CB), f32), pltpu.VMEM((1, S5_CB), f32),
         pltpu.VMEM((tm, S5_CB), f32), pltpu.VMEM((tm, S5_CB), f32),
         pltpu.VMEM((3, SUBLANES, S5_CB), f32), pltpu.VMEM((3, SUBLANES, S5_CB), f32)],
        ("parallel", "arbitrary"), (dxr, dxi, xr, xi, a_re, a_im))


def _s5_fwd(tag, proj, cst, rows):
    xr, xi = _s5_scan_fwd(tag, proj, cst["b_re"].astype(bf16), cst["b_im"].astype(bf16), cst["a_re"], cst["a_im"], rows)

    def body(i, xr_ref, xi_ref, u_ref, cre_ref, cim_ref, d_ref, gw_ref, gb_ref, y_ref, out_ref):
        y = (_dot(xr_ref[...].astype(bf16), cre_ref[...]) + _dot(xi_ref[...].astype(bf16), cim_ref[...])
             + d_ref[...] * u_ref[...])
        y_ref[...] = y
        z = _gelu(y)
        zg = _dot(z.astype(bf16), gw_ref[...]) + gb_ref[...]
        out_ref[...] = (z * _sigmoid(zg)).astype(bf16)

    y, out = _rt("s5_out_" + tag, body, rows, ROW_TILE,
                 [(xr, S5_LANES, 0), (xi, S5_LANES, 0), (proj, BRANCH, U_COL)],
                 [cst["c_re"].astype(bf16), cst["c_im"].astype(bf16), cst["s5_d"], cst["glu_w"], cst["glu_b"]],
                 [(BRANCH, f32), (BRANCH, bf16)])
    return out, (xr, xi, y)


def _s5_bwd(tag, saved, proj, cst, d_out, rows):
    xr, xi, y = saved
    c_re = cst["c_re"].astype(bf16)
    c_im = cst["c_im"].astype(bf16)

    def body(i, do_ref, y_ref, u_ref, xr_ref, xi_ref, cre_ref, cim_ref, gw_ref, gb_ref,
             dxr_ref, dxi_ref, dy_ref, dgw_ref, dgb_ref, dd_ref, dcre_ref, dcim_ref):
        yv = y_ref[...]
        z = _gelu(yv)
        zb = z.astype(bf16)
        gt = _sigmoid(_dot(zb, gw_ref[...]) + gb_ref[...])
        dov = do_ref[...]
        dzg = dov * z * gt * (1.0 - gt)
        dzgb = dzg.astype(bf16)
        dz = dov * gt + _dot_nt(dzgb, gw_ref[...])
        dgw_ref[...] += _dot_tn(zb, dzgb)
        dgb_ref[...] += jnp.sum(dzg, axis=0, keepdims=True)
        dy = dz * _gelu_grad(yv)
        dy_ref[...] = dy
        dd_ref[...] += jnp.sum(dy * u_ref[...], axis=0, keepdims=True)
        dyb = dy.astype(bf16)
        dxr_ref[...] = _dot_nt(dyb, cre_ref[...])
        dxi_ref[...] = _dot_nt(dyb, cim_ref[...])
        dcre_ref[...] += _dot_tn(xr_ref[...].astype(bf16), dyb)
        dcim_ref[...] += _dot_tn(xi_ref[...].astype(bf16), dyb)

    dxr, dxi, dy, d_gw, d_gb, d_d, d_cre, d_cim = _rt(
        "s5_out_bwd_" + tag, body, rows, ROW_TILE,
        [(d_out, BRANCH, 0), (y, BRANCH, 0), (proj, BRANCH, U_COL), (xr, S5_LANES, 0), (xi, S5_LANES, 0)],
        [c_re, c_im, cst["glu_w"], cst["glu_b"]],
        [(S5_LANES, f32), (S5_LANES, f32), (BRANCH, f32)],
        acc_outs=[(BRANCH, BRANCH), (1, BRANCH), (1, BRANCH), (S5_LANES, BRANCH), (S5_LANES, BRANCH)])

    gr, gi, d_ar, d_ai = _s5_scan_bwd(tag, dxr, dxi, xr, xi, cst["a_re"], cst["a_im"], rows)
    b_re = cst["b_re"].astype(bf16)
    b_im = cst["b_im"].astype(bf16)

    def body_in(i, gr_ref, gi_ref, dy_ref, u_ref, bre_ref, bim_ref, d_ref, du_ref, dbre_ref, dbim_ref):
        grv = gr_ref[...]
        giv = gi_ref[...]
        du = _dot_nt(grv, bre_ref[...]) + _dot_nt(giv, bim_ref[...]) + dy_ref[...] * d_ref[...]
        du_ref[...] = du.astype(bf16)
        ub = u_ref[...].astype(bf16)
        dbre_ref[...] += _dot_tn(ub, grv)
        dbim_ref[...] += _dot_tn(ub, giv)

    du, d_bre, d_bim = _rt("s5_in_bwd_" + tag, body_in, rows, ROW_TILE,
                           [(gr, S5_LANES, 0), (gi, S5_LANES, 0), (dy, BRANCH, 0), (proj, BRANCH, U_COL)],
                           [b_re, b_im, cst["s5_d"]], [(BRANCH, bf16)],
                           acc_outs=[(BRANCH, S5_LANES), (BRANCH, S5_LANES)])
    dcst = {"b_re": d_bre, "b_im": d_bim, "a_re": d_ar, "a_im": d_ai, "c_re": d_cre, "c_im": d_cim,
            "s5_d": d_d, "glu_b": d_gb}
    return du, dcst, d_gw


def _hg_prep(q, z, lb):
    qs = _sigmoid(q)
    qh = q * qs
    sg = _sigmoid_small(z)
    fg = lb + (1.0 - lb) * sg
    kk = (1.0 - lb) * (1.0 - sg)
    return qs, qh, sg, fg, kk


def _hg_fwd(tag, proj, cst, rows):
    tm = min(ROW_TILE, rows)
    c_sz = HG_CHUNK
    nch = tm // c_sz
    n_chunks = rows // c_sz

    def body(i, q_ref, z_ref, v_ref, g_ref, lb_ref, nw_ref, out_ref, o_ref, ss_ref, sn_ref, st_s):
        @pl.when(i == 0)
        def _():
            st_s[...] = jnp.zeros_like(st_s)

        lb = lb_ref[...]
        _, qh, sg, fg, kk = _hg_prep(q_ref[...], z_ref[...], lb)
        b = _seg_cumsum(jnp.log(fg), tm, c_sz)
        qhat = (qh * jnp.exp(b)).astype(bf16)
        khat = (kk * jnp.exp(-b)).astype(bf16)
        vb = v_ref[...].astype(bf16)
        b3 = b.reshape(nch, c_sz, BRANCH)
        bl3 = b3[:, c_sz - 1:c_sz, :]
        kdec = (kk.reshape(nch, c_sz, BRANCH) * jnp.exp(bl3 - b3)).astype(bf16)
        ebl = jnp.exp(bl3)
        tril = (lax.broadcasted_iota(jnp.int32, (nch, c_sz, c_sz), 1)
                >= lax.broadcasted_iota(jnp.int32, (nch, c_sz, c_sz), 2))
        o_heads = []
        for h in range(HG_HEADS):
            hl = slice(h * HG_DK, (h + 1) * HG_DK)
            q3 = qhat[:, hl].reshape(nch, c_sz, HG_DK)
            k3 = khat[:, hl].reshape(nch, c_sz, HG_DK)
            v3 = vb[:, hl].reshape(nch, c_sz, HG_DK)
            a_mat = jnp.where(tril, _bdot_nt(q3, k3), 0.0).astype(bf16)
            o3 = _bdot(a_mat, v3)
            st = st_s[hl, :]
            before = []
            for ci in range(nch):
                before.append(st.astype(bf16))
                st = st * ebl[ci][:, hl] + _dot_tn(v3[ci], kdec[ci][:, hl])
                sn_ref[ci, hl, :] = st.astype(bf16)
            st_s[hl, :] = st
            s3 = jnp.stack(before)
            ss_ref[:, hl, :] = s3
            o3 = o3 + _bdot_nt(q3, s3)
            o_heads.append(o3.reshape(tm, HG_DK))
        o = jnp.concatenate(o_heads, axis=1)
        o_ref[...] = o
        r = lax.rsqrt(_head_mean(o * o) + EPS)
        g = g_ref[...]
        out_ref[...] = (o * r * nw_ref[...] * (g * _sigmoid(g))).astype(bf16)

    out, o, ss, sn = _rt(
        "hg_fwd_" + tag, body, rows, tm,
        [(proj, BRANCH, U_COL + 1), (proj, BRANCH, U_COL + 2), (proj, BRANCH, U_COL + 3), (proj, BRANCH, U_COL + 4)],
        [cst["hg_lb"], cst["hg_nw"]],
        [(BRANCH, bf16), (BRANCH, f32),
         ((n_chunks, BRANCH, HG_DK), (nch, BRANCH, HG_DK), lambda t: (t, 0, 0), bf16),
         ((n_chunks, BRANCH, HG_DK), (nch, BRANCH, HG_DK), lambda t: (t, 0, 0), bf16)],
        scratch=[pltpu.VMEM((BRANCH, HG_DK), f32)])
    return out, (o, ss, sn)


def _hg_bwd(tag, saved, proj, cst, d_out, rows):
    o_saved, ss, sn = saved
    tm = min(ROW_TILE, rows)
    c_sz = HG_CHUNK
    nch = tm // c_sz

    def body(i, do_ref, q_ref, z_ref, v_ref, g_ref, o_ref, ss_ref, sn_ref, lb_ref, nw_ref,
             dq_ref, dz_ref, dv_ref, dg_ref, dlb_ref, dnw_ref, dst_s):
        @pl.when(i == 0)
        def _():
            dst_s[...] = jnp.zeros_like(dst_s)

        lb = lb_ref[...]
        q = q_ref[...]
        qs, qh, sg, fg, kk = _hg_prep(q, z_ref[...], lb)
        b = _seg_cumsum(jnp.log(fg), tm, c_sz)
        eb = jnp.exp(b)
        enb = jnp.exp(-b)
        qhat = (qh * eb).astype(bf16)
        khat = (kk * enb).astype(bf16)
        vb = v_ref[...].astype(bf16)
        b3 = b.reshape(nch, c_sz, BRANCH)
        bl3 = b3[:, c_sz - 1:c_sz, :]
        dec3 = jnp.exp(bl3 - b3)
        kdec = (kk.reshape(nch, c_sz, BRANCH) * dec3).astype(bf16)
        ebl = jnp.exp(bl3)
        g = g_ref[...]
        gs = _sigmoid(g)
        o = o_ref[...]
        r = lax.rsqrt(_head_mean(o * o) + EPS)
        oh = o * r
        nw = nw_ref[...]
        dov = do_ref[...]
        don = dov * (g * gs)
        dg_ref[...] = (dov * oh * nw * (gs * (1.0 + g * (1.0 - gs)))).astype(bf16)
        dnw_ref[...] += jnp.sum(don * oh, axis=0, keepdims=True)
        doh = don * nw
        d_o = r * (doh - oh * _head_mean(doh * oh))
        dob = d_o.astype(bf16)
        t_idx = lax.broadcasted_iota(jnp.int32, (nch, c_sz, c_sz), 1)
        s_idx = lax.broadcasted_iota(jnp.int32, (nch, c_sz, c_sz), 2)
        heads = []
        for h in range(HG_HEADS):
            hl = slice(h * HG_DK, (h + 1) * HG_DK)
            q3 = qhat[:, hl].reshape(nch, c_sz, HG_DK)
            k3 = khat[:, hl].reshape(nch, c_sz, HG_DK)
            v3 = vb[:, hl].reshape(nch, c_sz, HG_DK)
            do3 = dob[:, hl].reshape(nch, c_sz, HG_DK)
            s3 = ss_ref[:, hl, :]
            da_mat = jnp.where(t_idx >= s_idx, _bdot_nt(do3, v3), 0.0).astype(bf16)
            a_t = jnp.where(t_idx <= s_idx, _bdot_nt(k3, q3), 0.0).astype(bf16)
            da_t = jnp.where(t_idx <= s_idx, _bdot_nt(v3, do3), 0.0).astype(bf16)
            dqhat = _bdot(do3, s3) + _bdot(da_mat, k3)
            dkhat = _bdot(da_t, q3)
            dst = dst_s[hl, :]
            after = [None] * nch
            for ci in reversed(range(nch)):
                after[ci] = dst
                dst = dst * ebl[ci][:, hl] + _dot_tn(do3[ci], q3[ci])
            dst_s[hl, :] = dst
            ds3 = jnp.stack(after)
            ds3b = ds3.astype(bf16)
            dk_inter = _bdot(v3, ds3b) * dec3[:, :, hl]
            dv3 = _bdot(a_t, do3) + _bdot_nt(kdec[:, :, hl], ds3b)
            flux = jnp.sum(sn_ref[:, hl, :].astype(f32) * ds3, axis=1, keepdims=True)
            heads.append((dqhat.reshape(tm, HG_DK), dkhat.reshape(tm, HG_DK), dk_inter.reshape(tm, HG_DK),
                          dv3.reshape(tm, HG_DK), jnp.broadcast_to(flux, (nch, c_sz, HG_DK)).reshape(tm, HG_DK)))
        dqhat, dkhat, dk_inter, dv, flux = (jnp.concatenate(parts, axis=1) for parts in zip(*heads))
        dv_ref[...] = dv.astype(bf16)
        dqh = dqhat * eb
        dk = dkhat * enb + dk_inter
        db = qhat.astype(f32) * dqhat - khat.astype(f32) * dkhat - kk * dk_inter
        dlf = _seg_rev_cumsum(db, tm, c_sz) + flux
        tt = (1.0 - lb) * sg * (1.0 - sg)
        dz_ref[...] = (dlf * tt / fg - dk * tt).astype(bf16)
        dlb_ref[...] += jnp.sum(dlf * (1.0 - sg) / fg - dk * (1.0 - sg), axis=0, keepdims=True)
        dq_ref[...] = (dqh * (qs * (1.0 + q * (1.0 - qs)))).astype(bf16)

    dq, dz, dv, dg, d_lb, d_nw = _rt(
        "hg_bwd_" + tag, body, rows, tm,
        [(d_out, BRANCH, 0), (proj, BRANCH, U_COL + 1), (proj, BRANCH, U_COL + 2), (proj, BRANCH, U_COL + 3),
         (proj, BRANCH, U_COL + 4), (o_saved, BRANCH, 0), (ss, (nch, BRANCH, HG_DK), lambda t: (t, 0, 0)),
         (sn, (nch, BRANCH, HG_DK), lambda t: (t, 0, 0))],
        [cst["hg_lb"], cst["hg_nw"]],
        [(BRANCH, bf16)] * 4, acc_outs=[(1, BRANCH), (1, BRANCH)],
        scratch=[pltpu.VMEM((BRANCH, HG_DK), f32)],
        reverse=True)
    return dq, dz, dv, dg, {"hg_lb": d_lb, "hg_nw": d_nw}


def _rg_gates(xc, wa_ref, ba_ref, wx_ref, bx_ref, sp8):
    xcb = xc.astype(bf16)
    r = _sigmoid(_dot(xcb, wa_ref[...]) + ba_ref[...])
    ig = _sigmoid(_dot(xcb, wx_ref[...]) + bx_ref[...])
    la = -sp8 * r
    a = jnp.exp(la)
    mult = jnp.sqrt(-_expm1(2.0 * la))
    return xcb, r, ig, a, mult


def _rg_fwd(tag, proj, cst, rows):
    tm = min(ROW_TILE, rows)

    def body(i, xb_ref, gate_ref, cw_ref, cb_ref, wa_ref, ba_ref, wx_ref, bx_ref, sp_ref,
             out_ref, xc_ref, h_ref, hp_ref, prev_s, hc_s):
        @pl.when(i == 0)
        def _():
            prev_s[...] = jnp.zeros_like(prev_s)
            hc_s[...] = jnp.zeros_like(hc_s)

        row = _rows((tm, BRANCH))
        xb = xb_ref[...]
        prev = prev_s[...]
        xc = cb_ref[...] + cw_ref[3:4, :] * xb
        for j in range(1, 4):
            sh = jnp.where(row >= j, pltpu.roll(xb, j, 0), pltpu.roll(prev, j, 0))
            xc = xc + cw_ref[3 - j:4 - j, :] * sh
        prev_s[...] = xb
        xc_ref[...] = xc
        _, r, ig, a, mult = _rg_gates(xc, wa_ref, ba_ref, wx_ref, bx_ref, sp_ref[...])
        bb = mult * ig * xc
        hc = hc_s[...]
        row8 = _rows((SUBLANES, BRANCH))
        for g in range(tm // SUBLANES):
            sl = slice(g * SUBLANES, (g + 1) * SUBLANES)
            a_cum, h_loc = _scan_fwd(a[sl], bb[sl], SUBLANES)
            h = h_loc + a_cum * hc
            h_ref[sl, :] = h
            hp_ref[sl, :] = jnp.where(row8 >= 1, pltpu.roll(h, 1, 0), hc)
            hc = h[SUBLANES - 1:SUBLANES, :]
        hc_s[...] = hc
        out_ref[...] = (h_ref[...] * _gelu(gate_ref[...])).astype(bf16)

    out, xc, h, hp = _rt(
        "rg_fwd_" + tag, body, rows, tm,
        [(proj, BRANCH, U_COL + 5), (proj, BRANCH, U_COL + 6)],
        [cst["rg_cw"], cst["rg_cb"], cst["rg_wa"].astype(bf16), cst["rg_ba"], cst["rg_wx"].astype(bf16),
         cst["rg_bx"], cst["rg_sp8"]],
        [(BRANCH, bf16), (BRANCH, f32), (BRANCH, f32), (BRANCH, f32)],
        scratch=[pltpu.VMEM((tm, BRANCH), f32), pltpu.VMEM((1, BRANCH), f32)])
    return out, (xc, h, hp)


def _rg_bwd(tag, saved, proj, cst, d_out, rows):
    xc_saved, h_saved, hp_saved = saved
    tm = min(ROW_TILE, rows)

    def body(i, do_ref, xb_ref, gate_ref, xc_ref, h_ref, hp_ref, cw_ref, wa_ref, ba_ref, wx_ref, bx_ref, sp_ref,
             dxb_ref, dgate_ref, dcw_ref, dcb_ref, dwa_ref, dba_ref, dwx_ref, dbx_ref, dsp_ref,
             nxt_s, ec_s, gt_s):
        @pl.when(i == 0)
        def _():
            nxt_s[...] = jnp.zeros_like(nxt_s)
            ec_s[...] = jnp.zeros_like(ec_s)

        row = _rows((tm, BRANCH))
        xc = xc_ref[...]
        sp8 = sp_ref[...]
        xcb, r, ig, a, mult = _rg_gates(xc, wa_ref, ba_ref, wx_ref, bx_ref, sp8)
        gate = gate_ref[...]
        dov = do_ref[...]
        dh = dov * _gelu(gate)
        dgate_ref[...] = (dov * h_ref[...] * _gelu_grad(gate)).astype(bf16)
        adh = a * dh
        ec = ec_s[...]
        last8 = _rows((SUBLANES, BRANCH)) == SUBLANES - 1
        for g in reversed(range(tm // SUBLANES)):
            sl = slice(g * SUBLANES, (g + 1) * SUBLANES)
            a_cum, e_loc = _scan_bwd(a[sl], adh[sl], SUBLANES)
            e = e_loc + a_cum * ec
            gt_s[sl, :] = dh[sl] + jnp.where(last8, ec, pltpu.roll(e, SUBLANES - 1, 0))
            ec = e[0:1, :]
        ec_s[...] = ec
        g_tot = gt_s[...]
        d_a = g_tot * hp_ref[...]
        d_mult = g_tot * ig * xc
        d_ix = g_tot * mult
        d_ig = d_ix * xc
        d_xc = d_ix * ig
        d_la = d_a * a - d_mult * (a * a) / mult
        d_r = -d_la * sp8
        dsp_ref[...] += jnp.sum(-d_la * r, axis=0, keepdims=True)
        dzr = d_r * r * (1.0 - r)
        dzi = d_ig * ig * (1.0 - ig)
        dzrb = dzr.astype(bf16)
        dzib = dzi.astype(bf16)
        d_xc = d_xc + _dot_nt(dzrb, wa_ref[...]) + _dot_nt(dzib, wx_ref[...])
        dwa_ref[...] += _dot_tn(xcb, dzrb)
        dwx_ref[...] += _dot_tn(xcb, dzib)
        dba_ref[...] += jnp.sum(dzr, axis=0, keepdims=True)
        dbx_ref[...] += jnp.sum(dzi, axis=0, keepdims=True)
        dcb_ref[...] += jnp.sum(d_xc, axis=0, keepdims=True)
        nxt = nxt_s[...]
        xb = xb_ref[...]
        dxb = cw_ref[3:4, :] * d_xc
        dcw_ref[3:4, :] += jnp.sum(d_xc * xb, axis=0, keepdims=True)
        for j in range(1, 4):
            sh = jnp.where(row < tm - j, pltpu.roll(d_xc, tm - j, 0), pltpu.roll(nxt, tm - j, 0))
            dxb = dxb + cw_ref[3 - j:4 - j, :] * sh
            dcw_ref[3 - j:4 - j, :] += jnp.sum(sh * xb, axis=0, keepdims=True)
        nxt_s[...] = d_xc
        dxb_ref[...] = dxb.astype(bf16)

    wa = cst["rg_wa"].astype(bf16)
    wx = cst["rg_wx"].astype(bf16)
    dxb, dgate, d_cw, d_cb, d_wa, d_ba, d_wx, d_bx, d_sp = _rt(
        "rg_bwd_" + tag, body, rows, tm,
        [(d_out, BRANCH, 0), (proj, BRANCH, U_COL + 5), (proj, BRANCH, U_COL + 6), (xc_saved, BRANCH, 0),
         (h_saved, BRANCH, 0), (hp_saved, BRANCH, 0)],
        [cst["rg_cw"], wa, cst["rg_ba"], wx, cst["rg_bx"], cst["rg_sp8"]],
        [(BRANCH, bf16), (BRANCH, bf16)],
        acc_outs=[(4, BRANCH), (1, BRANCH), (BRANCH, BRANCH), (1, BRANCH), (BRANCH, BRANCH), (1, BRANCH), (1, BRANCH)],
        scratch=[pltpu.VMEM((tm, BRANCH), f32), pltpu.VMEM((1, BRANCH), f32), pltpu.VMEM((tm, BRANCH), f32)],
        reverse=True)
    dcst = {"rg_cw": d_cw, "rg_cb": d_cb, "rg_wa": d_wa, "rg_ba": d_ba, "rg_wx": d_wx, "rg_bx": d_bx, "rg_sp8": d_sp}
    return dxb, dgate, dcst


def _merge_fwd(tag, proj, outs, bp, rows):
    def body(i, ya_ref, yb_ref, yc_ref, gm_ref, p_ref, m_ref):
        acc = None
        for n, y_ref in enumerate((ya_ref, yb_ref, yc_ref)):
            up = _dot(y_ref[...], p_ref[n])
            term = _sigmoid(gm_ref[:, n * D_MODEL:(n + 1) * D_MODEL]) * up
            acc = term if acc is None else acc + term
        m_ref[...] = acc.astype(bf16)
    return _rt("merge_fwd_" + tag, body, rows, ROW_TILE,
               [(outs[0], BRANCH, 0), (outs[1], BRANCH, 0), (outs[2], BRANCH, 0), (proj, GM_WIDTH, 0)],
               [bp], [(D_MODEL, bf16)])[0]


def _merge_bwd(tag, proj, outs, bp, dmerged, rows):
    def body(i, dm_ref, ya_ref, yb_ref, yc_ref, gm_ref, p_ref, da_ref, db_ref, dc_ref, dgm_ref, dp_ref):
        dm = dm_ref[...]
        for n, (y_ref, dy_ref) in enumerate(((ya_ref, da_ref), (yb_ref, db_ref), (yc_ref, dc_ref))):
            yv = y_ref[...]
            up = _dot(yv, p_ref[n])
            gt = _sigmoid(gm_ref[:, n * D_MODEL:(n + 1) * D_MODEL])
            dup = (dm * gt).astype(bf16)
            dgm_ref[:, n * D_MODEL:(n + 1) * D_MODEL] = (dm * up * gt * (1.0 - gt)).astype(bf16)
            dy_ref[...] = _dot_nt(dup, p_ref[n])
            dp_ref[n] += _dot_tn(yv, dup)
    return _rt("merge_bwd_" + tag, body, rows, ROW_TILE,
               [(dmerged, D_MODEL, 0), (outs[0], BRANCH, 0), (outs[1], BRANCH, 0), (outs[2], BRANCH, 0),
                (proj, GM_WIDTH, 0)],
               [bp], [(BRANCH, f32), (BRANCH, f32), (BRANCH, f32), (GM_WIDTH, bf16)],
               acc_outs=[(N_BRANCH, BRANCH, D_MODEL)])


def _block_diag(blocks):
    g, r, c = blocks.shape
    on_diag = (lax.broadcasted_iota(jnp.int32, (g * r, g * c), 0) // r
               == lax.broadcasted_iota(jnp.int32, (g * r, g * c), 1) // c)
    tiled = jnp.broadcast_to(blocks.reshape(g * r, 1, c), (g * r, g, c)).reshape(g * r, g * c)
    return jnp.where(on_diag, tiled, 0.0)


def _prep_consts(sp):
    p = jax.nn.softmax(sp["hg_lb_logits"], axis=0)
    lower = jnp.cumsum(p, axis=0) - p[0]
    out = []
    for l in range(DEPTH):
        lr = jnp.minimum(sp["s5_lambda_re"][l], S5_EIG_MAX)
        li = sp["s5_lambda_im"][l]
        dt = jnp.exp(sp["s5_log_dt"][l])[:, None]
        mag = jnp.exp(lr * dt)
        ar = mag * jnp.cos(li * dt)
        ai = mag * jnp.sin(li * dt)
        den = lr * lr + li * li
        fr = ((ar - 1.0) * lr + ai * li) / den
        fi = (ai * lr - (ar - 1.0) * li) / den
        br, bi = sp["s5_b_re"][l], sp["s5_b_im"][l]
        bbr = fr[..., None] * br - fi[..., None] * bi
        bbi = fr[..., None] * bi + fi[..., None] * br
        c = {
            "a_re": ar.reshape(1, S5_LANES), "a_im": ai.reshape(1, S5_LANES),
            "b_re": _block_diag(bbr.transpose(0, 2, 1)), "b_im": _block_diag(bbi.transpose(0, 2, 1)),
            "c_re": _block_diag(sp["s5_c_re"][l].transpose(0, 2, 1)),
            "c_im": -_block_diag(sp["s5_c_im"][l].transpose(0, 2, 1)),
            "s5_d": sp["s5_d"][l][None], "glu_b": sp["s5_glu_b"][l][None],
            "hg_lb": lower[l][None], "hg_nw": sp["hg_norm_w"][l][None],
            "rg_cw": sp["rg_conv_w"][l], "rg_cb": sp["rg_conv_b"][l][None],
            "rg_wa": _block_diag(sp["rg_wa"][l]), "rg_ba": sp["rg_ba"][l][None],
            "rg_wx": _block_diag(sp["rg_wx"][l]), "rg_bx": sp["rg_bx"][l][None],
            "rg_sp8": (RG_C * jax.nn.softplus(-sp["rg_lambda"][l]))[None],
        }
        out.append(c)
    return out


def _mixer_fwd(tag, x, hb, w_in, bp, w_out, cst, next_nw, rows):
    proj = _mm1("mix_proj_" + tag, hb, w_in, "nn", rows, IN_TOTAL, D_MODEL, 512, IN_TOTAL // 4, D_MODEL)
    cst = dict(cst)
    out_a, sv_a = _s5_fwd(tag, proj, cst, rows)
    out_b, sv_b = _hg_fwd(tag, proj, cst, rows)
    out_c, sv_c = _rg_fwd(tag, proj, cst, rows)
    merged = _merge_fwd(tag, proj, (out_a, out_b, out_c), bp, rows)
    x_out, hb_out = _mm("mix_out_" + tag, [merged], [w_out], [(0, 0, 0)], 1, "nn", rows, D_MODEL, D_MODEL,
                        512, D_MODEL, D_MODEL, [f32, bf16], _residual_then_norm(1.0), extras=[x], vecs=[next_nw])
    return x_out, hb_out, (x, hb, proj, (out_a, out_b, out_c), merged, sv_a, sv_b, sv_c)


def _mixer_bwd(tag, saved, nw, w_in, bp, w_out, cst, dx, dxb, rows):
    x, hb, proj, outs, merged, sv_a, sv_b, sv_c = saved
    d_wout = _mm1("mix_dwout_" + tag, merged, dxb, "tn", D_MODEL, D_MODEL, rows, D_MODEL, D_MODEL, TOKEN_K,
                  out_dtype=bf16)
    dmerged = _mm1("mix_dmerged_" + tag, dxb, w_out, "nt", rows, D_MODEL, D_MODEL, 512, D_MODEL, D_MODEL)
    d_a, d_b, d_c, dgm, d_bp = _merge_bwd(tag, proj, outs, bp, dmerged, rows)
    dxbc, dgatec, dcst_c = _rg_bwd(tag, sv_c, proj, cst, d_c, rows)
    dq, dz, dv, dg, dcst_b = _hg_bwd(tag, sv_b, proj, cst, d_b, rows)
    du, dcst_a, d_glu_w = _s5_bwd(tag, sv_a, proj, cst, d_a, rows)
    dproj = jnp.concatenate([dgm, du, dq, dz, dv, dg, dxbc, dgatec], axis=1)
    d_win = _mm1("mix_dwin_" + tag, hb, dproj, "tn", D_MODEL, IN_TOTAL, rows, D_MODEL, IN_TOTAL // 4, TOKEN_K,
                 out_dtype=bf16)
    dx_in, dxb_in, d_nw = _mm("mix_dh_" + tag, [dproj], [w_in], [(0, 0, 0)], 1, "nt", rows, D_MODEL, IN_TOTAL,
                              512, D_MODEL, IN_TOTAL // 2, [f32, bf16], _norm_bwd_epilogue, extras=[x, dx], vecs=[nw],
                              n_part=1)
    dcst = {**dcst_a, **dcst_b, **dcst_c}
    return dx_in, dxb_in, jnp.sum(d_nw, axis=0), d_win, d_bp, d_wout, d_glu_w, dcst


def _local_step(x, target, weights_of, small, grads_done):
    rows = x.shape[0]
    consts, consts_vjp = jax.vjp(_prep_consts, small)
    norm_w = small["norm_w"]
    saved = []
    h = x
    hb = _rms_fwd("first_norm", x, norm_w[0, 0][None], rows)
    for l in range(DEPTH):
        t = str(l)
        after = [norm_w[l + 1, 0][None]] if l + 1 < DEPTH else []
        wa = weights_of(l, "a")
        h, hb, sv0 = _ffn_fwd(t + "a", h, hb, *wa, [norm_w[l, 1][None]], rows)
        wm = weights_of(l, "mix")
        cst = dict(consts[l])
        cst["glu_w"] = wm[3]
        h, hb, sv1 = _mixer_fwd(t, h, hb, *wm[:3], cst, norm_w[l, 2][None], rows)
        wb = weights_of(l, "b")
        h, hb, sv2 = _ffn_fwd(t + "b", h, hb, *wb, after, rows)
        saved.append((sv0, sv1, sv2, cst, wa, wm, wb))
    dx, dxb, loss, d_fnw = _loss_head(h, small["final_norm_w"][None], target, rows)
    d_norm = [None] * DEPTH
    d_consts = [None] * DEPTH
    for l in reversed(range(DEPTH)):
        t = str(l)
        sv0, sv1, sv2, cst, wa, wm, wb = saved[l]
        dx, dxb, dn2, dg1, du1, dd1 = _ffn_bwd(t + "b", sv2, norm_w[l, 2][None], *wb, dx, dxb, rows)
        grads_done(l, "b", [dg1, du1, dd1])
        dx, dxb, dn1, d_win, d_bp, d_wout, d_glu_w, dcst = _mixer_bwd(
            t, sv1, norm_w[l, 1][None], *wm[:3], cst, dx, dxb, rows)
        grads_done(l, "mix", [d_win, d_bp, d_wout, d_glu_w])
        dx, dxb, dn0, dg0, du0, dd0 = _ffn_bwd(t + "a", sv0, norm_w[l, 0][None], *wa, dx, dxb, rows)
        grads_done(l, "a", [dg0, du0, dd0])
        d_norm[l] = jnp.concatenate([dn0, dn1, dn2], axis=0)
        d_consts[l] = dcst
    (g_small,) = consts_vjp(d_consts)
    g_small = dict(g_small)
    g_small["norm_w"] = g_small["norm_w"] + jnp.stack(d_norm)
    g_small["final_norm_w"] = g_small["final_norm_w"] + d_fnw[0]
    return loss[0, 0], dx, g_small


def _other_chips(x, y):
    return [(1 - x, y), (x, 1 - y), (1 - x, 1 - y)]


def _gather_over_ici(shards):
    n = len(shards)

    def copies(in_refs, out_refs, send_sems, recv_sems):
        x, y, c = _place()
        cps = []
        for i in range(n):
            mine = out_refs[i].at[4 * x + 2 * y + c]
            cps.append(pltpu.make_async_copy(in_refs[i], mine, send_sems.at[5 * i + 4]))
            for k, to in enumerate([(x, y, 1 - c)] + [(px, py, c) for px, py in _other_chips(x, y)]):
                cps.append(pltpu.make_async_remote_copy(
                    src_ref=in_refs[i], dst_ref=mine, send_sem=send_sems.at[5 * i + k],
                    recv_sem=recv_sems.at[5 * i + k], device_id=to, device_id_type=MESH_IDS))
        return cps

    return _Carry(shards, [jax.ShapeDtypeStruct((N_DEV,) + s.shape, s.dtype) for s in shards], 5 * n, copies)


def _gather_forward(name, landings):
    n = len(landings)

    def body(*refs):
        in_refs, out_refs = refs[:n], refs[n:2 * n]
        send_sems, recv_sems = refs[2 * n:]
        x, y, c = _place()
        cps = []
        for i in range(n):
            for j, (px, py) in enumerate(_other_chips(x, y)):
                block = 4 * px + 2 * py + c
                cps.append(pltpu.make_async_remote_copy(
                    src_ref=in_refs[i].at[block], dst_ref=out_refs[i].at[block], send_sem=send_sems.at[3 * i + j],
                    recv_sem=recv_sems.at[3 * i + j], device_id=(x, y, 1 - c), device_id_type=MESH_IDS))
        for cp in cps:
            cp.start()
        for cp in cps:
            cp.wait()

    return pl.pallas_call(
        body, name=name, out_shape=[jax.ShapeDtypeStruct(a.shape, a.dtype) for a in landings],
        in_specs=[ANY_SPEC] * n, out_specs=[ANY_SPEC] * n, input_output_aliases={i: i for i in range(n)},
        scratch_shapes=[pltpu.SemaphoreType.DMA((3 * n,)), pltpu.SemaphoreType.DMA((3 * n,))],
    )(*landings)


def _all_gather(name, shards):
    n = len(shards)

    def body(*refs):
        x_refs, out_refs = refs[:n], refs[n:2 * n]
        send_sems, recv_sems, local_sems = refs[2 * n:]
        x, y, c = _place()
        me, sibling = (x, y, c), (x, y, 1 - c)
        chips = [(1 - x, y), (x, 1 - y), (1 - x, 1 - y)]

        def blk(i, px, py, pc):
            return out_refs[i].at[4 * px + 2 * py + pc]

        def copy(i, k, block, to, src=None):
            return pltpu.make_async_remote_copy(
                src_ref=blk(i, *block) if src is None else src, dst_ref=blk(i, *block),
                send_sem=send_sems.at[7 * i + k], recv_sem=recv_sems.at[7 * i + k], device_id=to,
                device_id_type=MESH_IDS)

        mine = [pltpu.make_async_copy(x_refs[i], blk(i, *me), local_sems.at[i]) for i in range(n)]
        for cp in mine:
            cp.start()
        first = []
        for i in range(n):
            first.append(copy(i, 0, me, sibling, src=x_refs[i]))
            first += [copy(i, 1 + j, me, (*chip, c), src=x_refs[i]) for j, chip in enumerate(chips)]
        for cp in first:
            cp.start()
        passed = []
        for j, chip in enumerate(chips):
            for i in range(n):
                copy(i, 1 + j, (*chip, c), me).wait_recv()
                fwd = copy(i, 4 + j, (*chip, c), sibling)
                fwd.start()
                passed.append(fwd)
        for i in range(n):
            copy(i, 0, sibling, me).wait_recv()
            for j, chip in enumerate(chips):
                copy(i, 4 + j, (*chip, 1 - c), me).wait_recv()
        for cp in first + passed:
            cp.wait_send()
        for cp in mine:
            cp.wait()

    return pl.pallas_call(
        body, name=name, out_shape=[jax.ShapeDtypeStruct((N_DEV,) + s.shape, s.dtype) for s in shards],
        in_specs=[ANY_SPEC] * n, out_specs=[ANY_SPEC] * n,
        scratch_shapes=[pltpu.SemaphoreType.DMA((7 * n,)), pltpu.SemaphoreType.DMA((7 * n,)),
                        pltpu.SemaphoreType.DMA((n,))],
    )(*shards)


def _row_tile(rows):
    return rows if rows <= 512 else next(t for t in range(512, 7, -8) if rows % t == 0)


def _sums_over_ici(chip_sums):
    n = len(chip_sums)

    def copies(in_refs, out_refs, send_sems, recv_sems):
        x, y, c = _place()
        return [pltpu.make_async_remote_copy(
            src_ref=in_refs[i].at[2 * px + py], dst_ref=out_refs[i].at[k], send_sem=send_sems.at[3 * i + k],
            recv_sem=recv_sems.at[3 * i + k], device_id=(px, py, c), device_id_type=MESH_IDS)
            for i in range(n) for k, (px, py) in enumerate(_other_chips(x, y))]

    return _Carry(chip_sums, [jax.ShapeDtypeStruct((3,) + t.shape[1:], t.dtype) for t in chip_sums], 3 * n, copies)


def _reduce_scatter(tag, parts, hosts=None):
    n = len(parts)
    _, _, c = _place()

    def body_pair(*refs):
        p_refs, got_refs = refs[:n], refs[n:2 * n]
        send_sems, recv_sems = refs[2 * n:]
        x, y, c = _place()
        cps = [pltpu.make_async_remote_copy(
            src_ref=p_refs[i].at[:, 1 - c], dst_ref=got_refs[i], send_sem=send_sems.at[i], recv_sem=recv_sems.at[i],
            device_id=(x, y, 1 - c), device_id_type=MESH_IDS) for i in range(n)]
        for cp in cps:
            cp.start()
        for cp in cps:
            cp.wait()

    from_sibling = pl.pallas_call(
        body_pair, name="rs_pair_" + tag,
        out_shape=[jax.ShapeDtypeStruct((4,) + p.shape[2:], p.dtype) for p in parts],
        in_specs=[ANY_SPEC] * n, out_specs=[ANY_SPEC] * n,
        scratch_shapes=[pltpu.SemaphoreType.DMA((n,)), pltpu.SemaphoreType.DMA((n,))],
    )(*parts)

    def body_add(idx_ref, *refs):
        p = pl.program_id(0)
        for q in range(n):
            @pl.when(p == q)
            def _(p_ref=refs[q], g_ref=refs[n + q], o_ref=refs[2 * n + q]):
                o_ref[...] = (p_ref[...].astype(f32) + g_ref[...].astype(f32)).astype(o_ref.dtype)

    def at(q):
        return lambda p, j, idx: jnp.clip(j + 4 * (p - q), 0, 3)

    in_specs, out_specs = [], []
    for q, part in enumerate(parts):
        in_specs.append(pl.BlockSpec((None, None) + part.shape[2:],
                                     lambda p, j, idx, blk=at(q): (blk(p, j, idx), idx[0], 0, 0)))
    for q, part in enumerate(parts):
        spec = pl.BlockSpec((None,) + part.shape[2:], lambda p, j, idx, blk=at(q): (blk(p, j, idx), 0, 0))
        in_specs.append(spec)
        out_specs.append(spec)
    chip_sums = pl.pallas_call(
        body_add, name="rs_pair_sum_" + tag,
        out_shape=[jax.ShapeDtypeStruct((4,) + p.shape[2:], p.dtype) for p in parts],
        grid_spec=pltpu.PrefetchScalarGridSpec(num_scalar_prefetch=1, grid=(n, 4), in_specs=in_specs,
                                               out_specs=out_specs),
        compiler_params=_cparams(("arbitrary", "arbitrary")),
    )(jnp.stack([c]).astype(jnp.int32), *parts, *from_sibling)

    others = [None] * n
    riding = set()
    for host, which in (hosts or {}).items():
        rider = _sums_over_ici([chip_sums[i] for i in which])
        _CARRIED[host] = rider
        for pos, i in enumerate(which):
            others[i] = functools.partial(lambda r, p: r.outs[p], rider, pos)
        riding.update(which)
    rest = [i for i in range(n) if i not in riding]
    if rest:
        alone = _sums_over_ici([chip_sums[i] for i in rest])

        def body_chips(*refs):
            k = len(rest)
            cps = alone.copies(refs[:k], refs[k:2 * k], *refs[2 * k:])
            for cp in cps:
                cp.start()
            for cp in cps:
                cp.wait()

        from_chips = pl.pallas_call(
            body_chips, name="rs_chips_" + tag, out_shape=alone.out_shapes,
            in_specs=[ANY_SPEC] * len(rest), out_specs=[ANY_SPEC] * len(rest), scratch_shapes=alone.sems(),
        )(*alone.ins)
        for pos, i in enumerate(rest):
            others[i] = functools.partial(lambda got: got, from_chips[pos])
    return list(zip(chip_sums, others))


def _own_index():
    x, y, _ = _place()
    return jnp.stack([2 * x + y]).astype(jnp.int32)


def _own_total(name, chip_sum, others):
    _, r, cols = chip_sum.shape
    tr = _row_tile(r)

    def body(idx_ref, t_ref, g_ref, o_ref):
        o_ref[...] = ((t_ref[...].astype(f32) + g_ref[0].astype(f32)) + g_ref[1].astype(f32)) + g_ref[2].astype(f32)

    return pl.pallas_call(
        body, name=name, out_shape=jax.ShapeDtypeStruct((r, cols), f32),
        grid_spec=pltpu.PrefetchScalarGridSpec(
            num_scalar_prefetch=1, grid=(r // tr,),
            in_specs=[pl.BlockSpec((None, tr, cols), lambda t, idx: (idx[0], t, 0)),
                      pl.BlockSpec((3, tr, cols), lambda t, idx: (0, t, 0))],
            out_specs=pl.BlockSpec((tr, cols), lambda t, idx: (t, 0))),
        compiler_params=_cparams(("parallel",)),
    )(_own_index(), chip_sum, others)


def _adam_update(w, gv, m, v):
    m_new = ADAM_B1 * m + (1.0 - ADAM_B1) * gv
    v_new = ADAM_B2 * v + (1.0 - ADAM_B2) * (gv * gv)
    m_hat = m_new / (1.0 - ADAM_B1 ** ADAM_STEP)
    v_hat = v_new / (1.0 - ADAM_B2 ** ADAM_STEP)
    return -ADAM_LR * (m_hat / (jnp.sqrt(v_hat) + ADAM_EPS) + ADAM_WD * w), m_new, v_new


def _adamw_reduced(name, w, pieces, m, v):
    n_p, rows, cols = w.shape
    tr = _row_tile(rows)

    def body(idx_ref, w_ref, *refs):
        red = refs[:2 * n_p]
        m_ref, v_ref, g_ref, d_ref, nm_ref, nv_ref = refs[2 * n_p:]
        p = pl.program_id(0)
        for q in range(n_p):
            @pl.when(p == q)
            def _(t_ref=red[2 * q], o_ref=red[2 * q + 1]):
                gv = ((t_ref[...].astype(f32) + o_ref[0].astype(f32)) + o_ref[1].astype(f32)) + o_ref[2].astype(f32)
                g_ref[...] = gv
                d_ref[...], nm_ref[...], nv_ref[...] = _adam_update(w_ref[...], gv, m_ref[...], v_ref[...])

    spec = pl.BlockSpec((None, tr, cols), lambda p, t, idx: (p, t, 0))
    red_specs, red_args = [], []
    for q, (chip_sum, others) in enumerate(pieces):
        red_specs.append(pl.BlockSpec((None, tr, cols), lambda p, t, idx, q=q: (idx[0], jnp.where(p == q, t, 0), 0)))
        red_specs.append(pl.BlockSpec((3, tr, cols), lambda p, t, idx, q=q: (0, jnp.where(p == q, t, 0), 0)))
        red_args += [chip_sum, others]
    return pl.pallas_call(
        body, name=name, out_shape=[jax.ShapeDtypeStruct((n_p, rows, cols), f32)] * 4,
        grid_spec=pltpu.PrefetchScalarGridSpec(
            num_scalar_prefetch=1, grid=(n_p, rows // tr),
            in_specs=[spec] + red_specs + [spec, spec], out_specs=[spec] * 4),
        compiler_params=_cparams(("parallel", "parallel")),
    )(_own_index(), w, *red_args, m, v)


def _adamw(name, w, g, m, v):
    rows, cols = w.shape
    tr = _row_tile(rows)

    def body(w_ref, g_ref, m_ref, v_ref, d_ref, nm_ref, nv_ref):
        d_ref[...], nm_ref[...], nv_ref[...] = _adam_update(w_ref[...], g_ref[...], m_ref[...], v_ref[...])

    spec = pl.BlockSpec((tr, cols), lambda i: (i, 0))
    return pl.pallas_call(
        body, name=name, grid=(rows // tr,), in_specs=[spec] * 4, out_specs=[spec] * 3,
        out_shape=[jax.ShapeDtypeStruct((rows, cols), f32)] * 3, compiler_params=_cparams(("parallel",)),
    )(w, g, m, v)


WEIGHT_NAMES = ["norm_w", "final_norm_w", "ffn_gate", "ffn_up", "ffn_down", "w_in", "branch_proj", "w_out",
                "s5_lambda_re", "s5_lambda_im", "s5_log_dt", "s5_b_re", "s5_b_im", "s5_c_re", "s5_c_im", "s5_d",
                "s5_glu_w", "s5_glu_b", "hg_lb_logits", "hg_norm_w", "rg_conv_w", "rg_conv_b", "rg_wa", "rg_ba",
                "rg_wx", "rg_bx", "rg_lambda"]
SHARDED = {"ffn_gate": (3, "gate"), "ffn_up": (3, "up"), "ffn_down": (2, "down"), "w_in": (2, "w_in"),
           "branch_proj": (3, "bp"), "w_out": (1, "w_out"), "s5_glu_w": (1, "glu_w"),
           "norm_w": (2, None), "rg_conv_w": (2, None)}
BIG = ["ffn_gate", "ffn_up", "ffn_down", "w_in", "branch_proj", "w_out", "s5_glu_w"]
PARTS = {"a": [("ffn_gate", 0, 1), ("ffn_up", 0, 1), ("ffn_down", 0, 0)],
         "b": [("ffn_gate", 1, 1), ("ffn_up", 1, 1), ("ffn_down", 1, 0)],
         "mix": [("w_in", None, 1), ("branch_proj", None, 2), ("w_out", None, 0), ("s5_glu_w", None, 0)]}
AG_HOSTS = {"ffn_up_0a": (0, "mix", [0]), "ffn_down_0a": (0, "mix", [1, 2, 3]), "mix_proj_0": (0, "b", [0, 1, 2]),
            "s5_out_0": (1, "a", [0]), "hg_fwd_0": (1, "a", [1]), "rg_fwd_0": (1, "a", [2]),
            "merge_fwd_0": (1, "b", [0]), "ffn_up_0b": (1, "b", [1]), "ffn_down_0b": (1, "b", [2]),
            "s5_scan_fwd_0": (1, "mix", [0, 1]), "mix_out_0": (1, "mix", [2, 3])}
RS_HOSTS = {(1, "b"): {"s5_scan_bwd_1": [0, 1, 2]},
            (1, "mix"): {"ffn_bwd_mid_1a": [1, 2, 3], "ffn_dh_1a": [0]},
            (1, "a"): {"ffn_dh_0b": [0, 1], "merge_bwd_0": [2]},
            (0, "b"): {"s5_scan_bwd_0": [0, 1, 2]},
            (0, "mix"): {"ffn_bwd_mid_0a": [1, 2, 3], "ffn_dh_0a": [0]}}
SMALL_SHARDED = ["norm_w", "rg_conv_w"]
REPLICATED = [n for n in WEIGHT_NAMES if n not in SHARDED]
LANES = 128


PACK_ROWS = 512


def _pack_rows(arrays, names):
    pieces = []
    for n in names:
        flat = arrays[n].reshape(-1)
        pieces.append(jnp.pad(flat, (0, -flat.shape[0] % LANES)).reshape(-1, LANES))
    rows = jnp.concatenate(pieces, axis=0)
    return jnp.pad(rows, ((0, -rows.shape[0] % PACK_ROWS), (0, 0)))


def _unpack_rows(rows, names, like):
    out, r0 = {}, 0
    for n in names:
        size = math.prod(like[n].shape)
        nrows = -(-size // LANES)
        out[n] = rows[r0:r0 + nrows].reshape(-1)[:size].reshape(like[n].shape)
        r0 += nrows
    return out


def _unshard(gathered, axis):
    g = jnp.moveaxis(gathered, 0, axis)
    shp = g.shape
    return g.reshape(shp[:axis] + (shp[axis] * shp[axis + 1],) + shp[axis + 2:])


RELAYOUT_ROWS = 256


def _column_runs(width, first_col):
    total = N_DEV * width
    runs = []
    for j in range(N_DEV):
        start = (width * j + first_col) % total
        head = min(width, total - start)
        runs.append((j, 0, start, head))
        if head < width:
            runs.append((j, head, 0, width - head))
    return runs


def _unshard_columns(name, gathered, first_col=0):
    _, r, c = gathered.shape
    tr = min(RELAYOUT_ROWS, r)
    runs = _column_runs(c, first_col)

    def body(g_ref, o_ref):
        for j, off, dst, length in runs:
            o_ref[:, dst:dst + length] = g_ref[j, :, off:off + length]

    return pl.pallas_call(
        body, name=name, grid=(r // tr,), in_specs=[pl.BlockSpec((N_DEV, tr, c), lambda i: (0, i, 0))],
        out_specs=pl.BlockSpec((tr, N_DEV * c), lambda i: (i, 0)),
        out_shape=jax.ShapeDtypeStruct((r, N_DEV * c), gathered.dtype), compiler_params=_cparams(("parallel",)),
    )(gathered)


def _columns_to_blocks(name, full, first_col=0):
    r, total = full.shape
    c = total // N_DEV
    tr = min(RELAYOUT_ROWS, r)
    runs = _column_runs(c, first_col)

    def body(x_ref, o_ref):
        for j, off, src, length in runs:
            o_ref[j // 2, j % 2, :, off:off + length] = x_ref[:, src:src + length].astype(bf16)

    return pl.pallas_call(
        body, name=name, grid=(r // tr,), in_specs=[pl.BlockSpec((tr, total), lambda i: (i, 0))],
        out_specs=pl.BlockSpec((4, 2, tr, c), lambda i: (0, 0, i, 0)),
        out_shape=jax.ShapeDtypeStruct((4, 2, r, c), bf16), compiler_params=_cparams(("parallel",)),
    )(full)


def _to_blocks(full, axis):
    shp = full.shape
    g = full.reshape(shp[:axis] + (4, 2, shp[axis] // N_DEV) + shp[axis + 1:])
    g = jnp.moveaxis(g, (axis, axis + 1), (0, 1))
    return g.reshape(4, 2, -1, g.shape[-1])


W_IN_SPLIT = IN_TOTAL - GM_WIDTH


def kernel(x, norm_w, final_norm_w, ffn_gate, ffn_up, ffn_down, w_in, branch_proj, w_out, s5_lambda_re, s5_lambda_im, s5_log_dt, s5_b_re, s5_b_im, s5_c_re, s5_c_im, s5_d, s5_glu_w, s5_glu_b, hg_lb_logits, hg_norm_w, rg_conv_w, rg_conv_b, rg_wa, rg_ba, rg_wx, rg_bx, rg_lambda, loss_target, m_norm_w, m_final_norm_w, m_ffn_gate, m_ffn_up, m_ffn_down, m_w_in, m_branch_proj, m_w_out, m_s5_lambda_re, m_s5_lambda_im, m_s5_log_dt, m_s5_b_re, m_s5_b_im, m_s5_c_re, m_s5_c_im, m_s5_d, m_s5_glu_w, m_s5_glu_b, m_hg_lb_logits, m_hg_norm_w, m_rg_conv_w, m_rg_conv_b, m_rg_wa, m_rg_ba, m_rg_wx, m_rg_bx, m_rg_lambda, v_norm_w, v_final_norm_w, v_ffn_gate, v_ffn_up, v_ffn_down, v_w_in, v_branch_proj, v_w_out, v_s5_lambda_re, v_s5_lambda_im, v_s5_log_dt, v_s5_b_re, v_s5_b_im, v_s5_c_re, v_s5_c_im, v_s5_d, v_s5_glu_w, v_s5_glu_b, v_hg_lb_logits, v_hg_norm_w, v_rg_conv_w, v_rg_conv_b, v_rg_wa, v_rg_ba, v_rg_wx, v_rg_bx, v_rg_lambda):
    w = dict(zip(WEIGHT_NAMES, (norm_w, final_norm_w, ffn_gate, ffn_up, ffn_down, w_in, branch_proj, w_out,
                                s5_lambda_re, s5_lambda_im, s5_log_dt, s5_b_re, s5_b_im, s5_c_re, s5_c_im, s5_d,
                                s5_glu_w, s5_glu_b, hg_lb_logits, hg_norm_w, rg_conv_w, rg_conv_b, rg_wa, rg_ba,
                                rg_wx, rg_bx, rg_lambda)))
    m = dict(zip(WEIGHT_NAMES, (m_norm_w, m_final_norm_w, m_ffn_gate, m_ffn_up, m_ffn_down, m_w_in, m_branch_proj,
                                m_w_out, m_s5_lambda_re, m_s5_lambda_im, m_s5_log_dt, m_s5_b_re, m_s5_b_im, m_s5_c_re,
                                m_s5_c_im, m_s5_d, m_s5_glu_w, m_s5_glu_b, m_hg_lb_logits, m_hg_norm_w, m_rg_conv_w,
                                m_rg_conv_b, m_rg_wa, m_rg_ba, m_rg_wx, m_rg_bx, m_rg_lambda)))
    v = dict(zip(WEIGHT_NAMES, (v_norm_w, v_final_norm_w, v_ffn_gate, v_ffn_up, v_ffn_down, v_w_in, v_branch_proj,
                                v_w_out, v_s5_lambda_re, v_s5_lambda_im, v_s5_log_dt, v_s5_b_re, v_s5_b_im, v_s5_c_re,
                                v_s5_c_im, v_s5_d, v_s5_glu_w, v_s5_glu_b, v_hg_lb_logits, v_hg_norm_w, v_rg_conv_w,
                                v_rg_conv_b, v_rg_wa, v_rg_ba, v_rg_wx, v_rg_bx, v_rg_lambda)))
    rows = x.shape[1]

    _CARRIED.clear()

    def shard_of(piece, l):
        n, k, _ = piece
        return (w[n][l] if k is None else w[n][l, k]).astype(bf16)

    def assemble(l, part, gathered):
        full = []
        for j, (piece, g) in enumerate(zip(PARTS[part], gathered)):
            tag = "unshard_%d%s%d" % (l, part, j)
            if piece[0] == "w_in":
                full.append(_unshard_columns(tag, g, first_col=GM_WIDTH))
            elif piece[0] == "branch_proj":
                full.append(_unshard_columns(tag, g.reshape(N_DEV, -1, g.shape[-1])).reshape(N_BRANCH, BRANCH, D_MODEL))
            elif piece[2] == g.ndim - 2:
                full.append(_unshard_columns(tag, g))
            else:
                full.append(_unshard(g, piece[2]))
        return full

    n_a = len(PARTS["a"])
    first = _all_gather("gather_weights", [shard_of(p, 0) for p in PARTS["a"]] + [w[n] for n in SMALL_SHARDED])
    small = {n: w[n] for n in REPLICATED}
    for n, g in zip(SMALL_SHARDED, first[n_a:]):
        small[n] = _unshard(g, SHARDED[n][0])
    riders = {}
    for host, (l, part, which) in AG_HOSTS.items():
        rider = _gather_over_ici([shard_of(PARTS[part][j], l) for j in which])
        _CARRIED[host] = rider
        riders.setdefault((l, part), []).append((which, rider))

    def weights_of(l, part):
        if (l, part) == (0, "a"):
            return assemble(l, part, first[:n_a])
        landed = [None] * len(PARTS[part])
        for which, rider in riders[l, part]:
            for j, buf in zip(which, rider.outs):
                landed[j] = buf
        return assemble(l, part, _gather_forward("gather_forward_%d%s" % (l, part), landed))

    sums = {}

    def blocks_of(l, part, grads):
        out = []
        for j, (piece, g) in enumerate(zip(PARTS[part], grads)):
            tag = "to_blocks_%d%s%d" % (l, part, j)
            if piece[0] == "w_in":
                out.append(_columns_to_blocks(tag, g, first_col=GM_WIDTH))
            elif piece[0] == "branch_proj":
                out.append(_columns_to_blocks(tag, g.reshape(-1, g.shape[-1])))
            elif piece[2] == g.ndim - 1:
                out.append(_columns_to_blocks(tag, g))
            else:
                out.append(_to_blocks(g, piece[2]).astype(bf16))
        return out

    last_grads = []

    def grads_done(l, part, grads):
        if (l, part) in RS_HOSTS:
            sums[l, part] = _reduce_scatter("%d%s" % (l, part), blocks_of(l, part, grads), hosts=RS_HOSTS[l, part])
        else:
            last_grads.extend(blocks_of(l, part, grads))

    loss_part, dx, g_small = _local_step(x[0], loss_target[0], weights_of, small, grads_done)
    loss = lax.psum(loss_part, ("x", "y", "c"))

    parts = last_grads + [_to_blocks(g_small[n], SHARDED[n][0]) for n in SMALL_SHARDED]
    rep_rows = _pack_rows(g_small, REPLICATED)
    rep_slice = rep_rows.shape[0] // N_DEV
    parts.append(rep_rows.reshape(4, 2, rep_slice, LANES))
    last = _reduce_scatter("last", parts)
    sums[0, "a"] = last[:n_a]

    grads, delta, new_m, new_v = {}, {}, {}, {}

    def update(n, pieces):
        shp = w[n].shape
        view = (len(pieces), -1, shp[-1])
        res = _adamw_reduced("adamw_" + n, w[n].reshape(view), [(t, others()) for t, others in pieces],
                             m[n].reshape(view), v[n].reshape(view))
        grads[n], delta[n], new_m[n], new_v[n] = (r.reshape(shp) for r in res)

    for n in BIG:
        update(n, [sums[l, part][j] for l in range(DEPTH) for part in ("a", "b", "mix")
                   for j, piece in enumerate(PARTS[part]) if piece[0] == n])
    for j, n in enumerate(SMALL_SHARDED):
        update(n, [last[n_a + j]])
    rep_mine = _own_total("rs_total_small", last[-1][0], last[-1][1]())
    rep_grads = _all_gather("gather_small_grads", [rep_mine])[0].reshape(-1, LANES)
    res = _adamw("adamw_small", _pack_rows(w, REPLICATED), rep_grads, _pack_rows(m, REPLICATED), _pack_rows(v, REPLICATED))
    for dst, src in zip((grads, delta, new_m, new_v), (rep_grads,) + tuple(res)):
        dst.update(_unpack_rows(src, REPLICATED, w))

    return (loss, dx.reshape(x.shape), *[grads[n] for n in WEIGHT_NAMES], *[delta[n] for n in WEIGHT_NAMES],
            *[new_m[n] for n in WEIGHT_NAMES], *[new_v[n] for n in WEIGHT_NAMES])
```

```python
import functools
import math

import jax
import jax.numpy as jnp
from jax import lax
from jax.experimental import pallas as pl
from jax.experimental.pallas import tpu as pltpu

f32 = jnp.float32
bf16 = jnp.bfloat16

D_MODEL = 1024
DEPTH = 2
BRANCH = 512
N_BRANCH = 3
S5_GROUP = 16
S5_GROUPS = 32
S5_STATE = 64
S5_LANES = S5_GROUPS * S5_STATE
S5_EIG_MAX = -1e-4
HG_HEADS = 4
HG_DK = 128
HG_CHUNK = 32
RG_BLOCKS = 8
RG_BLOCK = 64
RG_C = 8.0
D_FF = 2816
EPS = 1e-6
IN_TOTAL = 6656
GM_WIDTH = N_BRANCH * D_MODEL
N_DEV = 8

ADAM_LR = 0.001
ADAM_B1 = 0.9
ADAM_B2 = 0.999
ADAM_EPS = 1e-08
ADAM_WD = 0.01
ADAM_STEP = 10

VMEM_LIMIT_V7X = 56 * 1024 * 1024
ROW_TILE = 256
FF_TILE = 1408
TOKEN_K = 4096
MXU_COLS = 256


def _cparams(sem):
    return pltpu.CompilerParams(dimension_semantics=sem, vmem_limit_bytes=VMEM_LIMIT_V7X)


MESH_IDS = pl.DeviceIdType.MESH
ANY_SPEC = pl.BlockSpec(memory_space=pl.ANY)


def _place():
    return lax.axis_index("x"), lax.axis_index("y"), lax.axis_index("c")


class _Carry:
    def __init__(self, ins, out_shapes, n_sems, copies):
        self.ins, self.out_shapes, self.n_sems, self.copies = list(ins), list(out_shapes), n_sems, copies
        self.outs = None

    def sems(self):
        return [pltpu.SemaphoreType.DMA((self.n_sems,)), pltpu.SemaphoreType.DMA((self.n_sems,))]

    def start(self, when, *riders):
        @pl.when(when)
        def _():
            for cp in self.copies(*riders):
                cp.start()

    def finish(self, when, *riders):
        @pl.when(when)
        def _():
            for cp in self.copies(*riders):
                cp.wait()


_CARRIED = {}


def _call_with_rider(name, body, grid, in_specs, out_specs, out_shape, scratch, semantics, args):
    carry = _CARRIED.pop(name, None)
    if carry is None:
        return pl.pallas_call(body, name=name, grid=grid, in_specs=in_specs, out_specs=out_specs,
                              out_shape=out_shape, scratch_shapes=scratch, compiler_params=_cparams(semantics))(*args)
    n_in, n_out, nci, nco = len(in_specs), len(out_specs), len(carry.ins), len(carry.out_shapes)

    def kern(*refs):
        ids = [pl.program_id(d) for d in range(len(grid))]
        own = refs[:n_in] + refs[n_in + nci:n_in + nci + n_out] + refs[n_in + nci + n_out + nco:-2]
        riders = (refs[n_in:n_in + nci], refs[n_in + nci + n_out:n_in + nci + n_out + nco]) + tuple(refs[-2:])
        carry.start(functools.reduce(jnp.logical_and, [p == 0 for p in ids]), *riders)
        body(*own)
        carry.finish(functools.reduce(jnp.logical_and, [p == g - 1 for p, g in zip(ids, grid)]), *riders)

    res = pl.pallas_call(
        kern, name=name, grid=grid, in_specs=list(in_specs) + [ANY_SPEC] * nci,
        out_specs=list(out_specs) + [ANY_SPEC] * nco, out_shape=list(out_shape) + carry.out_shapes,
        scratch_shapes=list(scratch) + carry.sems(), compiler_params=_cparams(("arbitrary",) * len(grid)),
    )(*args, *carry.ins)
    carry.outs = res[n_out:]
    return res[:n_out]


def _sigmoid(x):
    return 0.5 * jnp.tanh(0.5 * x) + 0.5


def _sigmoid_small(x):
    return 1.0 / (1.0 + jnp.exp(-x))


_GELU_C = math.sqrt(2.0 / math.pi)


def _gelu(x):
    t = jnp.tanh(_GELU_C * (x + 0.044715 * x * x * x))
    return 0.5 * x * (1.0 + t)


def _gelu_grad(x):
    t = jnp.tanh(_GELU_C * (x + 0.044715 * x * x * x))
    return 0.5 * (1.0 + t) + 0.5 * x * (1.0 - t * t) * _GELU_C * (1.0 + 3.0 * 0.044715 * x * x)


def _expm1(x):
    p = x * (1.0 + x * (0.5 + x * (1.0 / 6 + x * (1.0 / 24 + x * (1.0 / 120 + x * (1.0 / 720))))))
    return jnp.where(jnp.abs(x) < 0.3, p, jnp.exp(x) - 1.0)


def _dot(a, b):
    return jnp.dot(a, b, preferred_element_type=f32)


def _dot_nt(a, b):
    return lax.dot_general(a, b, (((1,), (1,)), ((), ())), preferred_element_type=f32)


def _dot_tn(a, b):
    return lax.dot_general(a, b, (((0,), (0,)), ((), ())), preferred_element_type=f32)


def _bdot(a, b):
    return lax.dot_general(a, b, (((2,), (1,)), ((0,), (0,))), preferred_element_type=f32)


def _bdot_nt(a, b):
    return lax.dot_general(a, b, (((2,), (2,)), ((0,), (0,))), preferred_element_type=f32)


def _rows(shape):
    return lax.broadcasted_iota(jnp.int32, shape, 0)


def _scan_fwd(a, b, n):
    row = _rows(a.shape)
    s = 1
    while s < n:
        valid = row >= s
        sh_a = pltpu.roll(a, s, 0)
        sh_b = pltpu.roll(b, s, 0)
        b = b + a * jnp.where(valid, sh_b, 0.0)
        a = a * jnp.where(valid, sh_a, 1.0)
        s *= 2
    return a, b


def _scan_bwd(a, b, n):
    row = _rows(a.shape)
    s = 1
    while s < n:
        valid = row < n - s
        sh_a = pltpu.roll(a, n - s, 0)
        sh_b = pltpu.roll(b, n - s, 0)
        b = b + a * jnp.where(valid, sh_b, 0.0)
        a = a * jnp.where(valid, sh_a, 1.0)
        s *= 2
    return a, b


def _seg_cumsum(x, n, seg):
    pos = _rows(x.shape) % seg
    s = 1
    while s < seg:
        x = x + jnp.where(pos >= s, pltpu.roll(x, s, 0), 0.0)
        s *= 2
    return x


def _seg_rev_cumsum(x, n, seg):
    pos = _rows(x.shape) % seg
    s = 1
    while s < seg:
        x = x + jnp.where(pos < seg - s, pltpu.roll(x, n - s, 0), 0.0)
        s *= 2
    return x


def _head_mean(x):
    parts = []
    for h in range(HG_HEADS):
        m = jnp.mean(x[:, h * HG_DK:(h + 1) * HG_DK], axis=1, keepdims=True)
        parts.append(jnp.broadcast_to(m, (x.shape[0], HG_DK)))
    return jnp.concatenate(parts, axis=1)


def _mm(name, a_list, b_list, terms, n_acc, mode, m, n, k, tm, tn, tk, out_dtypes, epilogue, extras=(), vecs=(),
        n_part=0, chunk=0):
    tm, tn, tk = min(tm, m), min(tn, n), min(tk, k)
    assert m % tm == 0 and n % tn == 0 and k % tk == 0, (name, m, n, k, tm, tn, tk)
    gk = k // tk
    if mode == "tn":
        a_spec = pl.BlockSpec((tk, tm), lambda i, j, kk: (kk, i))
    else:
        a_spec = pl.BlockSpec((tm, tk), lambda i, j, kk: (i, kk))
    if mode == "nt":
        b_spec = pl.BlockSpec((tn, tk), lambda i, j, kk: (j, kk))
    else:
        b_spec = pl.BlockSpec((tk, tn), lambda i, j, kk: (kk, j))
    o_spec = pl.BlockSpec((tm, tn), lambda i, j, kk: (i, j))
    v_spec = pl.BlockSpec((1, tn), lambda i, j, kk: (0, j))
    p_spec = pl.BlockSpec((None, 1, tn), lambda i, j, kk: (i, 0, j))
    dot = {"nn": _dot, "nt": _dot_nt, "tn": _dot_tn}[mode]
    na, nb, ne, nv, no = len(a_list), len(b_list), len(extras), len(vecs), len(out_dtypes)
    carry = _CARRIED.pop(name, None)
    nci, nco = (len(carry.ins), len(carry.out_shapes)) if carry else (0, 0)
    n_in = na + nb + ne + nv + nci
    grid = (m // tm, n // tn, gk)

    def kern(*refs):
        if carry:
            ids = [pl.program_id(d) for d in range(3)]
            riders = (refs[n_in - nci:n_in], refs[n_in + no + n_part:n_in + no + n_part + nco]) + tuple(refs[-2:])
            carry.start(functools.reduce(jnp.logical_and, [p == 0 for p in ids]), *riders)
        compute(*refs)
        if carry:
            carry.finish(functools.reduce(jnp.logical_and, [p == g - 1 for p, g in zip(ids, grid)]), *riders)

    def compute(*refs):
        a_refs = refs[:na]
        b_refs = refs[na:na + nb]
        e_refs = refs[na + nb:na + nb + ne]
        v_refs = refs[na + nb + ne:na + nb + ne + nv]
        o_refs = refs[n_in:n_in + no + n_part]

        def finish(accs):
            outs = epilogue(accs, [e[...] for e in e_refs], [r[...] for r in v_refs])
            for o, val in zip(o_refs, outs):
                o[...] = val.astype(o.dtype)

        def partial_sums():
            sums = [None] * n_acc
            for ai, bi, ci in terms:
                d = dot(a_refs[ai][...].astype(bf16), b_refs[bi][...].astype(bf16))
                sums[ci] = d if sums[ci] is None else sums[ci] + d
            return sums

        if gk == 1 and chunk:
            assert mode in ("nn", "nt") and tn % chunk == 0
            for c0 in range(0, tn, chunk):
                cols = slice(c0, c0 + chunk)
                sums = [None] * n_acc
                for ai, bi, ci in terms:
                    b_part = b_refs[bi][:, cols] if mode == "nn" else b_refs[bi][cols, :]
                    d = dot(a_refs[ai][...].astype(bf16), b_part.astype(bf16))
                    sums[ci] = d if sums[ci] is None else sums[ci] + d
                outs = epilogue(sums, [e[:, cols] for e in e_refs], [r[:, cols] for r in v_refs])
                for o, val in zip(o_refs, outs):
                    o[:, cols] = val.astype(o.dtype)
            return
        if gk == 1:
            finish(partial_sums())
            return
        acc = refs[n_in + no + n_part + nco]
        kk = pl.program_id(2)

        @pl.when(kk == 0)
        def _():
            acc[...] = jnp.zeros_like(acc)

        for ci, d in enumerate(partial_sums()):
            acc[ci] += d

        @pl.when(kk == gk - 1)
        def _():
            finish([acc[c] for c in range(n_acc)])

    res = pl.pallas_call(
        kern, name=name,
        grid=grid,
        in_specs=[a_spec] * na + [b_spec] * nb + [o_spec] * ne + [v_spec] * nv + [ANY_SPEC] * nci,
        out_specs=[o_spec] * no + [p_spec] * n_part + [ANY_SPEC] * nco,
        out_shape=([jax.ShapeDtypeStruct((m, n), dt) for dt in out_dtypes]
                   + [jax.ShapeDtypeStruct((m // tm, 1, n), f32)] * n_part + (carry.out_shapes if carry else [])),
        scratch_shapes=([pltpu.VMEM((n_acc, tm, tn), f32)] if gk > 1 else []) + (carry.sems() if carry else []),
        compiler_params=_cparams(("arbitrary",) * 3 if carry else ("parallel", "parallel", "arbitrary")),
    )(*a_list, *b_list, *extras, *vecs, *(carry.ins if carry else []))
    if carry:
        carry.outs = res[no + n_part:]
        res = res[:no + n_part]
    return res


def _mm1(name, a, b, mode, m, n, k, tm, tn, tk, out_dtype=f32, scale=None):
    def epi(accs, extras, vecs):
        return [accs[0] if scale is None else accs[0] * scale]
    return _mm(name, [a], [b], [(0, 0, 0)], 1, mode, m, n, k, tm, tn, tk, [out_dtype], epi)[0]


def _rt(name, body, rows, tm, row_ins, consts, row_outs, acc_outs=(), scratch=(), reverse=False):
    tm = min(tm, rows)
    assert rows % tm == 0
    nt = rows // tm

    def tile(i):
        return nt - 1 - i if reverse else i

    in_specs, args = [], []
    for spec in row_ins:
        arr = spec[0]
        if isinstance(spec[1], int):
            in_specs.append(pl.BlockSpec((tm, spec[1]), lambda i, cb=spec[2]: (tile(i), cb)))
        else:
            in_specs.append(pl.BlockSpec(spec[1], lambda i, fn=spec[2]: fn(tile(i))))
        args.append(arr)
    for c in consts:
        in_specs.append(pl.BlockSpec(c.shape, lambda i, nd=c.ndim: (0,) * nd))
        args.append(c)
    out_specs, out_shape = [], []
    for spec in row_outs:
        if isinstance(spec[0], int):
            out_specs.append(pl.BlockSpec((tm, spec[0]), lambda i: (tile(i), 0)))
            out_shape.append(jax.ShapeDtypeStruct((rows, spec[0]), spec[1]))
        else:
            out_specs.append(pl.BlockSpec(spec[1], lambda i, fn=spec[2]: fn(tile(i))))
            out_shape.append(jax.ShapeDtypeStruct(spec[0], spec[3]))
    for shp in acc_outs:
        out_specs.append(pl.BlockSpec(shp, lambda i, nd=len(shp): (0,) * nd))
        out_shape.append(jax.ShapeDtypeStruct(shp, f32))
    n_in = len(args)
    n_row_out = len(row_outs)
    n_acc = len(acc_outs)
    n_out = n_row_out + n_acc
    carry = _CARRIED.pop(name, None)
    nci, nco = (len(carry.ins), len(carry.out_shapes)) if carry else (0, 0)

    def kern(*refs):
        i = pl.program_id(0)
        if carry:
            own = refs[:n_in] + refs[n_in + nci:n_in + nci + n_out] + refs[n_in + nci + n_out + nco:-2]
            riders = (refs[n_in:n_in + nci], refs[n_in + nci + n_out:n_in + nci + n_out + nco]) + tuple(refs[-2:])
            carry.start(i == 0, *riders)
        else:
            own = refs
        acc_refs = own[n_in + n_row_out:n_in + n_out]

        @pl.when(i == 0)
        def _():
            for r in acc_refs:
                r[...] = jnp.zeros_like(r)

        body(i, *own)
        if carry:
            carry.finish(i == nt - 1, *riders)

    res = pl.pallas_call(
        kern, name=name, grid=(nt,), in_specs=in_specs + [ANY_SPEC] * nci, out_specs=out_specs + [ANY_SPEC] * nco,
        out_shape=out_shape + (carry.out_shapes if carry else []),
        scratch_shapes=list(scratch) + (carry.sems() if carry else []), compiler_params=_cparams(("arbitrary",)),
    )(*args, *(carry.ins if carry else []))
    if carry:
        carry.outs = res[n_out:]
        res = res[:n_out]
    return res


def _rms_rows(xv, wv):
    r = lax.rsqrt(jnp.mean(xv * xv, axis=1, keepdims=True) + EPS)
    return (xv * r * wv).astype(bf16)


def _rms_bwd_rows(xv, dhv, wv, dres):
    r = lax.rsqrt(jnp.mean(xv * xv, axis=1, keepdims=True) + EPS)
    xn = xv * r
    dxn = dhv * wv
    dx = dres + r * (dxn - xn * jnp.mean(dxn * xn, axis=1, keepdims=True))
    return [dx, dx.astype(bf16), jnp.sum(dhv * xn, axis=0, keepdims=True)]


def _rms_fwd(name, x, w, rows):
    def body(i, x_ref, w_ref, h_ref):
        h_ref[...] = _rms_rows(x_ref[...], w_ref[...])
    return _rt(name, body, rows, ROW_TILE, [(x, D_MODEL, 0)], [w], [(D_MODEL, bf16)])[0]


def _residual_then_norm(scale):
    def epi(accs, extras, vecs):
        x_out = extras[0] + scale * accs[0]
        return [x_out] + [_rms_rows(x_out, v) for v in vecs]
    return epi


def _norm_bwd_epilogue(accs, extras, vecs):
    return _rms_bwd_rows(extras[0], accs[0], vecs[0], extras[1])


def _loss_head(x, w, target, rows):
    def body(i, x_ref, t_ref, w_ref, dx_ref, dxb_ref, loss_ref, dw_ref):
        xv = x_ref[...]
        r = lax.rsqrt(jnp.mean(xv * xv, axis=1, keepdims=True) + EPS)
        xn = xv * r
        wv = w_ref[...]
        err = xn * wv - t_ref[...]
        part = 0.5 * jnp.sum(jnp.mean(err * err, axis=1, keepdims=True), axis=0, keepdims=True)
        loss_ref[...] += jnp.broadcast_to(part, (1, 128))
        dy = err * (1.0 / D_MODEL)
        dxn = dy * wv
        dx = r * (dxn - xn * jnp.mean(dxn * xn, axis=1, keepdims=True))
        dx_ref[...] = dx
        dxb_ref[...] = dx.astype(bf16)
        dw_ref[...] += jnp.sum(dy * xn, axis=0, keepdims=True)
    return _rt("loss_head", body, rows, ROW_TILE, [(x, D_MODEL, 0), (target, D_MODEL, 0)], [w],
               [(D_MODEL, f32), (D_MODEL, bf16)], acc_outs=[(1, 128), (1, D_MODEL)])


def _ffn_fwd(tag, x, hb, wg, wu, wd, next_nw, rows):
    def epi_up(accs, extras, vecs):
        a, b = accs
        return [a, b, a * _sigmoid(a) * b]
    a, b, s = _mm("ffn_up_" + tag, [hb], [wg, wu], [(0, 0, 0), (0, 1, 1)], 2, "nn", rows, D_FF, D_MODEL,
                  512, D_FF, D_MODEL, [bf16, bf16, bf16], epi_up, chunk=MXU_COLS)
    outs = _mm("ffn_down_" + tag, [s], [wd], [(0, 0, 0)], 1, "nn", rows, D_MODEL, D_FF,
               512, D_MODEL, D_FF, [f32] + [bf16] * len(next_nw), _residual_then_norm(0.5), extras=[x],
               vecs=next_nw)
    return outs[0], (outs[1] if next_nw else None), (x, hb, a, b, s)


def _ffn_bwd(tag, saved, nw, wg, wu, wd, dx, dxb, rows):
    x, hb, a, b, s = saved

    def epi_mid(accs, extras, vecs):
        ds = 0.5 * accs[0]
        av = extras[0].astype(f32)
        bv = extras[1].astype(f32)
        sg = _sigmoid(av)
        return [ds * bv * sg * (1.0 + av * (1.0 - sg)), ds * av * sg]
    da, db = _mm("ffn_bwd_mid_" + tag, [dxb], [wd], [(0, 0, 0)], 1, "nt", rows, D_FF, D_MODEL,
                 512, D_FF, D_MODEL, [bf16, bf16], epi_mid, extras=[a, b], chunk=MXU_COLS)
    d_wd = _mm1("ffn_dwd_" + tag, s, dxb, "tn", D_FF, D_MODEL, rows, FF_TILE, D_MODEL, TOKEN_K, out_dtype=bf16,
                scale=0.5)
    d_wg = _mm1("ffn_dwg_" + tag, hb, da, "tn", D_MODEL, D_FF, rows, D_MODEL, FF_TILE, TOKEN_K, out_dtype=bf16)
    d_wu = _mm1("ffn_dwu_" + tag, hb, db, "tn", D_MODEL, D_FF, rows, D_MODEL, FF_TILE, TOKEN_K, out_dtype=bf16)
    dx_in, dxb_in, d_nw = _mm("ffn_dh_" + tag, [da, db], [wg, wu], [(0, 0, 0), (1, 1, 0)], 1, "nt", rows, D_MODEL,
                              D_FF, 512, D_MODEL, D_FF, [f32, bf16], _norm_bwd_epilogue, extras=[x, dx], vecs=[nw],
                              n_part=1)
    return dx_in, dxb_in, jnp.sum(d_nw, axis=0), d_wg, d_wu, d_wd


S5_CB = 512
SUBLANES = 8
U_COL = GM_WIDTH // BRANCH


def _s5_scan_fwd(tag, proj, b_re, b_im, a_re, a_im, rows):
    tm = min(ROW_TILE, rows)
    nt = rows // tm
    nc = S5_LANES // S5_CB

    def kern(u_ref, bre_ref, bim_ref, ar_ref, ai_ref, xr_ref, xi_ref, pr_s, pi_s, cr_s, ci_s, mr_s, mi_s):
        t = pl.program_id(1)

        @pl.when(t == 0)
        def _():
            row8 = _rows((SUBLANES, S5_CB))
            pr = jnp.broadcast_to(ar_ref[...], (SUBLANES, S5_CB))
            pi = jnp.broadcast_to(ai_ref[...], (SUBLANES, S5_CB))
            s = 1
            while s < SUBLANES:
                sr = pltpu.roll(pr, s, 0)
                si = pltpu.roll(pi, s, 0)
                valid = row8 >= s
                pr, pi = jnp.where(valid, pr * sr - pi * si, pr), jnp.where(valid, pr * si + pi * sr, pi)
                s *= 2
            pr_s[...] = pr
            pi_s[...] = pi
            for k in range(3):
                s = 1 << k
                mr_s[k] = jnp.where(row8 >= s, pr[s - 1:s, :], 0.0)
                mi_s[k] = jnp.where(row8 >= s, pi[s - 1:s, :], 0.0)
            cr_s[...] = jnp.zeros_like(cr_s)
            ci_s[...] = jnp.zeros_like(ci_s)

        ub = u_ref[...].astype(bf16)
        br = _dot(ub, bre_ref[...])
        bi = _dot(ub, bim_ref[...])
        steps = [(mr_s[k], mi_s[k]) for k in range(3)]
        cr = cr_s[...]
        ci = ci_s[...]
        pr = pr_s[...]
        pi = pi_s[...]
        for g in range(tm // SUBLANES):
            sl = slice(g * SUBLANES, (g + 1) * SUBLANES)
            xr = br[sl]
            xi = bi[sl]
            for k, (mr, mi) in enumerate(steps):
                sr = pltpu.roll(xr, 1 << k, 0)
                si = pltpu.roll(xi, 1 << k, 0)
                xr, xi = xr + (mr * sr - mi * si), xi + (mr * si + mi * sr)
            xr, xi = xr + (pr * cr - pi * ci), xi + (pr * ci + pi * cr)
            xr_ref[sl, :] = xr
            xi_ref[sl, :] = xi
            cr = xr[SUBLANES - 1:SUBLANES, :]
            ci = xi[SUBLANES - 1:SUBLANES, :]
        cr_s[...] = cr
        ci_s[...] = ci

    return _call_with_rider(
        "s5_scan_fwd_" + tag, kern, (nc, nt),
        [pl.BlockSpec((tm, BRANCH), lambda c, t: (t, U_COL)),
         pl.BlockSpec((BRANCH, S5_CB), lambda c, t: (0, c)),
         pl.BlockSpec((BRANCH, S5_CB), lambda c, t: (0, c)),
         pl.BlockSpec((1, S5_CB), lambda c, t: (0, c)),
         pl.BlockSpec((1, S5_CB), lambda c, t: (0, c))],
        [pl.BlockSpec((tm, S5_CB), lambda c, t: (t, c))] * 2,
        [jax.ShapeDtypeStruct((rows, S5_LANES), f32)] * 2,
        [pltpu.VMEM((SUBLANES, S5_CB), f32), pltpu.VMEM((SUBLANES, S5_CB), f32),
         pltpu.VMEM((1, S5_CB), f32), pltpu.VMEM((1, S5_CB), f32),
         pltpu.VMEM((3, SUBLANES, S5_CB), f32), pltpu.VMEM((3, SUBLANES, S5_CB), f32)],
        ("parallel", "arbitrary"), (proj, b_re, b_im, a_re, a_im))


def _s5_scan_bwd(tag, dxr, dxi, xr, xi, a_re, a_im, rows):
    tm = min(ROW_TILE, rows)
    nt = rows // tm
    nc = S5_LANES // S5_CB

    def kern(dxr_ref, dxi_ref, xr_ref, xi_ref, ar_ref, ai_ref, gr_ref, gi_ref, dar_ref, dai_ref,
             qr_s, qi_s, cr_s, ci_s, gr_s, gi_s, mr_s, mi_s):
        t = pl.program_id(1)
        row = _rows((tm, S5_CB))
        ng = tm // SUBLANES

        @pl.when(t == 0)
        def _():
            row8 = _rows((SUBLANES, S5_CB))
            qr = jnp.broadcast_to(ar_ref[...], (SUBLANES, S5_CB))
            qi = jnp.broadcast_to(-ai_ref[...], (SUBLANES, S5_CB))
            s = 1
            while s < SUBLANES:
                sr = pltpu.roll(qr, SUBLANES - s, 0)
                si = pltpu.roll(qi, SUBLANES - s, 0)
                valid = row8 < SUBLANES - s
                qr, qi = jnp.where(valid, qr * sr - qi * si, qr), jnp.where(valid, qr * si + qi * sr, qi)
                s *= 2
            qr_s[...] = qr
            qi_s[...] = qi
            for k in range(3):
                s = 1 << k
                mr_s[k] = jnp.where(row8 < SUBLANES - s, qr[SUBLANES - s:SUBLANES - s + 1, :], 0.0)
                mi_s[k] = jnp.where(row8 < SUBLANES - s, qi[SUBLANES - s:SUBLANES - s + 1, :], 0.0)
            cr_s[...] = jnp.zeros_like(cr_s)
            ci_s[...] = jnp.zeros_like(ci_s)
            dar_ref[...] = jnp.zeros_like(dar_ref)
            dai_ref[...] = jnp.zeros_like(dai_ref)

        steps = [(mr_s[k], mi_s[k]) for k in range(3)]
        cr = cr_s[...]
        ci = ci_s[...]
        qr = qr_s[...]
        qi = qi_s[...]
        last8 = _rows((SUBLANES, S5_CB)) == SUBLANES - 1
        acc_r = jnp.zeros((SUBLANES, S5_CB), f32)
        acc_i = jnp.zeros((SUBLANES, S5_CB), f32)
        for g in reversed(range(ng)):
            sl = slice(g * SUBLANES, (g + 1) * SUBLANES)
            gr = dxr_ref[sl, :]
            gi = dxi_ref[sl, :]
            for k, (mr, mi) in enumerate(steps):
                sr = pltpu.roll(gr, SUBLANES - (1 << k), 0)
                si = pltpu.roll(gi, SUBLANES - (1 << k), 0)
                gr, gi = gr + (mr * sr - mi * si), gi + (mr * si + mi * sr)
            gr, gi = gr + (qr * cr - qi * ci), gi + (qr * ci + qi * cr)
            gr_s[sl, :] = gr
            gi_s[sl, :] = gi
            gnr = jnp.where(last8, cr, pltpu.roll(gr, SUBLANES - 1, 0))
            gni = jnp.where(last8, ci, pltpu.roll(gi, SUBLANES - 1, 0))
            xr_v = xr_ref[sl, :]
            xi_v = xi_ref[sl, :]
            acc_r = acc_r + (gnr * xr_v + gni * xi_v)
            acc_i = acc_i + (gni * xr_v - gnr * xi_v)
            cr = gr[0:1, :]
            ci = gi[0:1, :]
        cr_s[...] = cr
        ci_s[...] = ci
        gr_ref[...] = gr_s[...].astype(bf16)
        gi_ref[...] = gi_s[...].astype(bf16)
        dar_ref[...] += jnp.sum(acc_r, axis=0, keepdims=True)
        dai_ref[...] += jnp.sum(acc_i, axis=0, keepdims=True)

    rev = lambda c, t: (nt - 1 - t, c)
    return _call_with_rider(
        "s5_scan_bwd_" + tag, kern, (nc, nt),
        [pl.BlockSpec((tm, S5_CB), rev)] * 4 + [pl.BlockSpec((1, S5_CB), lambda c, t: (0, c))] * 2,
        [pl.BlockSpec((tm, S5_CB), rev)] * 2 + [pl.BlockSpec((1, S5_CB), lambda c, t: (0, c))] * 2,
        [jax.ShapeDtypeStruct((rows, S5_LANES), bf16)] * 2 + [jax.ShapeDtypeStruct((1, S5_LANES), f32)] * 2,
        [pltpu.VMEM((SUBLANES, S5_CB), f32), pltpu.VMEM((SUBLANES, S5_CB), f32),
         pltpu.VMEM((1, S5_CB), f32), pltpu.VMEM((1, S5_CB), f32),
         pltpu.VMEM((tm, S5_CB), f32), pltpu.VMEM((tm, S5_CB), f32),
         pltpu.VMEM((3, SUBLANES, S5_CB), f32), pltpu.VMEM((3, SUBLANES, S5_CB), f32)],
        ("parallel", "arbitrary"), (dxr, dxi, xr, xi, a_re, a_im))


def _s5_fwd(tag, proj, cst, rows):
    xr, xi = _s5_scan_fwd(tag, proj, cst["b_re"].astype(bf16), cst["b_im"].astype(bf16), cst["a_re"], cst["a_im"], rows)

    def body(i, xr_ref, xi_ref, u_ref, cre_ref, cim_ref, d_ref, gw_ref, gb_ref, y_ref, out_ref):
        y = (_dot(xr_ref[...].astype(bf16), cre_ref[...]) + _dot(xi_ref[...].astype(bf16), cim_ref[...])
             + d_ref[...] * u_ref[...])
        y_ref[...] = y
        z = _gelu(y)
        zg = _dot(z.astype(bf16), gw_ref[...]) + gb_ref[...]
        out_ref[...] = (z * _sigmoid(zg)).astype(bf16)

    y, out = _rt("s5_out_" + tag, body, rows, ROW_TILE,
                 [(xr, S5_LANES, 0), (xi, S5_LANES, 0), (proj, BRANCH, U_COL)],
                 [cst["c_re"].astype(bf16), cst["c_im"].astype(bf16), cst["s5_d"], cst["glu_w"], cst["glu_b"]],
                 [(BRANCH, f32), (BRANCH, bf16)])
    return out, (xr, xi, y)


def _s5_bwd(tag, saved, proj, cst, d_out, rows):
    xr, xi, y = saved
    c_re = cst["c_re"].astype(bf16)
    c_im = cst["c_im"].astype(bf16)

    def body(i, do_ref, y_ref, u_ref, xr_ref, xi_ref, cre_ref, cim_ref, gw_ref, gb_ref,
             dxr_ref, dxi_ref, dy_ref, dgw_ref, dgb_ref, dd_ref, dcre_ref, dcim_ref):
        yv = y_ref[...]
        z = _gelu(yv)
        zb = z.astype(bf16)
        gt = _sigmoid(_dot(zb, gw_ref[...]) + gb_ref[...])
        dov = do_ref[...]
        dzg = dov * z * gt * (1.0 - gt)
        dzgb = dzg.astype(bf16)
        dz = dov * gt + _dot_nt(dzgb, gw_ref[...])
        dgw_ref[...] += _dot_tn(zb, dzgb)
        dgb_ref[...] += jnp.sum(dzg, axis=0, keepdims=True)
        dy = dz * _gelu_grad(yv)
        dy_ref[...] = dy
        dd_ref[...] += jnp.sum(dy * u_ref[...], axis=0, keepdims=True)
        dyb = dy.astype(bf16)
        dxr_ref[...] = _dot_nt(dyb, cre_ref[...])
        dxi_ref[...] = _dot_nt(dyb, cim_ref[...])
        dcre_ref[...] += _dot_tn(xr_ref[...].astype(bf16), dyb)
        dcim_ref[...] += _dot_tn(xi_ref[...].astype(bf16), dyb)

    dxr, dxi, dy, d_gw, d_gb, d_d, d_cre, d_cim = _rt(
        "s5_out_bwd_" + tag, body, rows, ROW_TILE,
        [(d_out, BRANCH, 0), (y, BRANCH, 0), (proj, BRANCH, U_COL), (xr, S5_LANES, 0), (xi, S5_LANES, 0)],
        [c_re, c_im, cst["glu_w"], cst["glu_b"]],
        [(S5_LANES, f32), (S5_LANES, f32), (BRANCH, f32)],
        acc_outs=[(BRANCH, BRANCH), (1, BRANCH), (1, BRANCH), (S5_LANES, BRANCH), (S5_LANES, BRANCH)])

    gr, gi, d_ar, d_ai = _s5_scan_bwd(tag, dxr, dxi, xr, xi, cst["a_re"], cst["a_im"], rows)
    b_re = cst["b_re"].astype(bf16)
    b_im = cst["b_im"].astype(bf16)

    def body_in(i, gr_ref, gi_ref, dy_ref, u_ref, bre_ref, bim_ref, d_ref, du_ref, dbre_ref, dbim_ref):
        grv = gr_ref[...]
        giv = gi_ref[...]
        du = _dot_nt(grv, bre_ref[...]) + _dot_nt(giv, bim_ref[...]) + dy_ref[...] * d_ref[...]
        du_ref[...] = du.astype(bf16)
        ub = u_ref[...].astype(bf16)
        dbre_ref[...] += _dot_tn(ub, grv)
        dbim_ref[...] += _dot_tn(ub, giv)

    du, d_bre, d_bim = _rt("s5_in_bwd_" + tag, body_in, rows, ROW_TILE,
                           [(gr, S5_LANES, 0), (gi, S5_LANES, 0), (dy, BRANCH, 0), (proj, BRANCH, U_COL)],
                           [b_re, b_im, cst["s5_d"]], [(BRANCH, bf16)],
                           acc_outs=[(BRANCH, S5_LANES), (BRANCH, S5_LANES)])
    dcst = {"b_re": d_bre, "b_im": d_bim, "a_re": d_ar, "a_im": d_ai, "c_re": d_cre, "c_im": d_cim,
            "s5_d": d_d, "glu_b": d_gb}
    return du, dcst, d_gw


def _hg_prep(q, z, lb):
    qs = _sigmoid(q)
    qh = q * qs
    sg = _sigmoid_small(z)
    fg = lb + (1.0 - lb) * sg
    kk = (1.0 - lb) * (1.0 - sg)
    return qs, qh, sg, fg, kk


def _hg_fwd(tag, proj, cst, rows):
    tm = min(ROW_TILE, rows)
    c_sz = HG_CHUNK
    nch = tm // c_sz
    n_chunks = rows // c_sz

    def body(i, q_ref, z_ref, v_ref, g_ref, lb_ref, nw_ref, out_ref, o_ref, ss_ref, sn_ref, st_s):
        @pl.when(i == 0)
        def _():
            st_s[...] = jnp.zeros_like(st_s)

        lb = lb_ref[...]
        _, qh, sg, fg, kk = _hg_prep(q_ref[...], z_ref[...], lb)
        b = _seg_cumsum(jnp.log(fg), tm, c_sz)
        qhat = (qh * jnp.exp(b)).astype(bf16)
        khat = (kk * jnp.exp(-b)).astype(bf16)
        vb = v_ref[...].astype(bf16)
        b3 = b.reshape(nch, c_sz, BRANCH)
        bl3 = b3[:, c_sz - 1:c_sz, :]
        kdec = (kk.reshape(nch, c_sz, BRANCH) * jnp.exp(bl3 - b3)).astype(bf16)
        ebl = jnp.exp(bl3)
        tril = (lax.broadcasted_iota(jnp.int32, (nch, c_sz, c_sz), 1)
                >= lax.broadcasted_iota(jnp.int32, (nch, c_sz, c_sz), 2))
        o_heads = []
        for h in range(HG_HEADS):
            hl = slice(h * HG_DK, (h + 1) * HG_DK)
            q3 = qhat[:, hl].reshape(nch, c_sz, HG_DK)
            k3 = khat[:, hl].reshape(nch, c_sz, HG_DK)
            v3 = vb[:, hl].reshape(nch, c_sz, HG_DK)
            a_mat = jnp.where(tril, _bdot_nt(q3, k3), 0.0).astype(bf16)
            o3 = _bdot(a_mat, v3)
            st = st_s[hl, :]
            before = []
            for ci in range(nch):
                before.append(st.astype(bf16))
                st = st * ebl[ci][:, hl] + _dot_tn(v3[ci], kdec[ci][:, hl])
                sn_ref[ci, hl, :] = st.astype(bf16)
            st_s[hl, :] = st
            s3 = jnp.stack(before)
            ss_ref[:, hl, :] = s3
            o3 = o3 + _bdot_nt(q3, s3)
            o_heads.append(o3.reshape(tm, HG_DK))
        o = jnp.concatenate(o_heads, axis=1)
        o_ref[...] = o
        r = lax.rsqrt(_head_mean(o * o) + EPS)
        g = g_ref[...]
        out_ref[...] = (o * r * nw_ref[...] * (g * _sigmoid(g))).astype(bf16)

    out, o, ss, sn = _rt(
        "hg_fwd_" + tag, body, rows, tm,
        [(proj, BRANCH, U_COL + 1), (proj, BRANCH, U_COL + 2), (proj, BRANCH, U_COL + 3), (proj, BRANCH, U_COL + 4)],
        [cst["hg_lb"], cst["hg_nw"]],
        [(BRANCH, bf16), (BRANCH, f32),
         ((n_chunks, BRANCH, HG_DK), (nch, BRANCH, HG_DK), lambda t: (t, 0, 0), bf16),
         ((n_chunks, BRANCH, HG_DK), (nch, BRANCH, HG_DK), lambda t: (t, 0, 0), bf16)],
        scratch=[pltpu.VMEM((BRANCH, HG_DK), f32)])
    return out, (o, ss, sn)


def _hg_bwd(tag, saved, proj, cst, d_out, rows):
    o_saved, ss, sn = saved
    tm = min(ROW_TILE, rows)
    c_sz = HG_CHUNK
    nch = tm // c_sz

    def body(i, do_ref, q_ref, z_ref, v_ref, g_ref, o_ref, ss_ref, sn_ref, lb_ref, nw_ref,
             dq_ref, dz_ref, dv_ref, dg_ref, dlb_ref, dnw_ref, dst_s):
        @pl.when(i == 0)
        def _():
            dst_s[...] = jnp.zeros_like(dst_s)

        lb = lb_ref[...]
        q = q_ref[...]
        qs, qh, sg, fg, kk = _hg_prep(q, z_ref[...], lb)
        b = _seg_cumsum(jnp.log(fg), tm, c_sz)
        eb = jnp.exp(b)
        enb = jnp.exp(-b)
        qhat = (qh * eb).astype(bf16)
        khat = (kk * enb).astype(bf16)
        vb = v_ref[...].astype(bf16)
        b3 = b.reshape(nch, c_sz, BRANCH)
        bl3 = b3[:, c_sz - 1:c_sz, :]
        dec3 = jnp.exp(bl3 - b3)
        kdec = (kk.reshape(nch, c_sz, BRANCH) * dec3).astype(bf16)
        ebl = jnp.exp(bl3)
        g = g_ref[...]
        gs = _sigmoid(g)
        o = o_ref[...]
        r = lax.rsqrt(_head_mean(o * o) + EPS)
        oh = o * r
        nw = nw_ref[...]
        dov = do_ref[...]
        don = dov * (g * gs)
        dg_ref[...] = (dov * oh * nw * (gs * (1.0 + g * (1.0 - gs)))).astype(bf16)
        dnw_ref[...] += jnp.sum(don * oh, axis=0, keepdims=True)
        doh = don * nw
        d_o = r * (doh - oh * _head_mean(doh * oh))
        dob = d_o.astype(bf16)
        t_idx = lax.broadcasted_iota(jnp.int32, (nch, c_sz, c_sz), 1)
        s_idx = lax.broadcasted_iota(jnp.int32, (nch, c_sz, c_sz), 2)
        heads = []
        for h in range(HG_HEADS):
            hl = slice(h * HG_DK, (h + 1) * HG_DK)
            q3 = qhat[:, hl].reshape(nch, c_sz, HG_DK)
            k3 = khat[:, hl].reshape(nch, c_sz, HG_DK)
            v3 = vb[:, hl].reshape(nch, c_sz, HG_DK)
            do3 = dob[:, hl].reshape(nch, c_sz, HG_DK)
            s3 = ss_ref[:, hl, :]
            da_mat = jnp.where(t_idx >= s_idx, _bdot_nt(do3, v3), 0.0).astype(bf16)
            a_t = jnp.where(t_idx <= s_idx, _bdot_nt(k3, q3), 0.0).astype(bf16)
            da_t = jnp.where(t_idx <= s_idx, _bdot_nt(v3, do3), 0.0).astype(bf16)
            dqhat = _bdot(do3, s3) + _bdot(da_mat, k3)
            dkhat = _bdot(da_t, q3)
            dst = dst_s[hl, :]
            after = [None] * nch
            for ci in reversed(range(nch)):
                after[ci] = dst
                dst = dst * ebl[ci][:, hl] + _dot_tn(do3[ci], q3[ci])
            dst_s[hl, :] = dst
            ds3 = jnp.stack(after)
            ds3b = ds3.astype(bf16)
            dk_inter = _bdot(v3, ds3b) * dec3[:, :, hl]
            dv3 = _bdot(a_t, do3) + _bdot_nt(kdec[:, :, hl], ds3b)
            flux = jnp.sum(sn_ref[:, hl, :].astype(f32) * ds3, axis=1, keepdims=True)
            heads.append((dqhat.reshape(tm, HG_DK), dkhat.reshape(tm, HG_DK), dk_inter.reshape(tm, HG_DK),
                          dv3.reshape(tm, HG_DK), jnp.broadcast_to(flux, (nch, c_sz, HG_DK)).reshape(tm, HG_DK)))
        dqhat, dkhat, dk_inter, dv, flux = (jnp.concatenate(parts, axis=1) for parts in zip(*heads))
        dv_ref[...] = dv.astype(bf16)
        dqh = dqhat * eb
        dk = dkhat * enb + dk_inter
        db = qhat.astype(f32) * dqhat - khat.astype(f32) * dkhat - kk * dk_inter
        dlf = _seg_rev_cumsum(db, tm, c_sz) + flux
        tt = (1.0 - lb) * sg * (1.0 - sg)
        dz_ref[...] = (dlf * tt / fg - dk * tt).astype(bf16)
        dlb_ref[...] += jnp.sum(dlf * (1.0 - sg) / fg - dk * (1.0 - sg), axis=0, keepdims=True)
        dq_ref[...] = (dqh * (qs * (1.0 + q * (1.0 - qs)))).astype(bf16)

    dq, dz, dv, dg, d_lb, d_nw = _rt(
        "hg_bwd_" + tag, body, rows, tm,
        [(d_out, BRANCH, 0), (proj, BRANCH, U_COL + 1), (proj, BRANCH, U_COL + 2), (proj, BRANCH, U_COL + 3),
         (proj, BRANCH, U_COL + 4), (o_saved, BRANCH, 0), (ss, (nch, BRANCH, HG_DK), lambda t: (t, 0, 0)),
         (sn, (nch, BRANCH, HG_DK), lambda t: (t, 0, 0))],
        [cst["hg_lb"], cst["hg_nw"]],
        [(BRANCH, bf16)] * 4, acc_outs=[(1, BRANCH), (1, BRANCH)],
        scratch=[pltpu.VMEM((BRANCH, HG_DK), f32)],
        reverse=True)
    return dq, dz, dv, dg, {"hg_lb": d_lb, "hg_nw": d_nw}


def _rg_gates(xc, wa_ref, ba_ref, wx_ref, bx_ref, sp8):
    xcb = xc.astype(bf16)
    r = _sigmoid(_dot(xcb, wa_ref[...]) + ba_ref[...])
    ig = _sigmoid(_dot(xcb, wx_ref[...]) + bx_ref[...])
    la = -sp8 * r
    a = jnp.exp(la)
    mult = jnp.sqrt(-_expm1(2.0 * la))
    return xcb, r, ig, a, mult


def _rg_fwd(tag, proj, cst, rows):
    tm = min(ROW_TILE, rows)

    def body(i, xb_ref, gate_ref, cw_ref, cb_ref, wa_ref, ba_ref, wx_ref, bx_ref, sp_ref,
             out_ref, xc_ref, h_ref, hp_ref, prev_s, hc_s):
        @pl.when(i == 0)
        def _():
            prev_s[...] = jnp.zeros_like(prev_s)
            hc_s[...] = jnp.zeros_like(hc_s)

        row = _rows((tm, BRANCH))
        xb = xb_ref[...]
        prev = prev_s[...]
        xc = cb_ref[...] + cw_ref[3:4, :] * xb
        for j in range(1, 4):
            sh = jnp.where(row >= j, pltpu.roll(xb, j, 0), pltpu.roll(prev, j, 0))
            xc = xc + cw_ref[3 - j:4 - j, :] * sh
        prev_s[...] = xb
        xc_ref[...] = xc
        _, r, ig, a, mult = _rg_gates(xc, wa_ref, ba_ref, wx_ref, bx_ref, sp_ref[...])
        bb = mult * ig * xc
        hc = hc_s[...]
        row8 = _rows((SUBLANES, BRANCH))
        for g in range(tm // SUBLANES):
            sl = slice(g * SUBLANES, (g + 1) * SUBLANES)
            a_cum, h_loc = _scan_fwd(a[sl], bb[sl], SUBLANES)
            h = h_loc + a_cum * hc
            h_ref[sl, :] = h
            hp_ref[sl, :] = jnp.where(row8 >= 1, pltpu.roll(h, 1, 0), hc)
            hc = h[SUBLANES - 1:SUBLANES, :]
        hc_s[...] = hc
        out_ref[...] = (h_ref[...] * _gelu(gate_ref[...])).astype(bf16)

    out, xc, h, hp = _rt(
        "rg_fwd_" + tag, body, rows, tm,
        [(proj, BRANCH, U_COL + 5), (proj, BRANCH, U_COL + 6)],
        [cst["rg_cw"], cst["rg_cb"], cst["rg_wa"].astype(bf16), cst["rg_ba"], cst["rg_wx"].astype(bf16),
         cst["rg_bx"], cst["rg_sp8"]],
        [(BRANCH, bf16), (BRANCH, f32), (BRANCH, f32), (BRANCH, f32)],
        scratch=[pltpu.VMEM((tm, BRANCH), f32), pltpu.VMEM((1, BRANCH), f32)])
    return out, (xc, h, hp)


def _rg_bwd(tag, saved, proj, cst, d_out, rows):
    xc_saved, h_saved, hp_saved = saved
    tm = min(ROW_TILE, rows)

    def body(i, do_ref, xb_ref, gate_ref, xc_ref, h_ref, hp_ref, cw_ref, wa_ref, ba_ref, wx_ref, bx_ref, sp_ref,
             dxb_ref, dgate_ref, dcw_ref, dcb_ref, dwa_ref, dba_ref, dwx_ref, dbx_ref, dsp_ref,
             nxt_s, ec_s, gt_s):
        @pl.when(i == 0)
        def _():
            nxt_s[...] = jnp.zeros_like(nxt_s)
            ec_s[...] = jnp.zeros_like(ec_s)

        row = _rows((tm, BRANCH))
        xc = xc_ref[...]
        sp8 = sp_ref[...]
        xcb, r, ig, a, mult = _rg_gates(xc, wa_ref, ba_ref, wx_ref, bx_ref, sp8)
        gate = gate_ref[...]
        dov = do_ref[...]
        dh = dov * _gelu(gate)
        dgate_ref[...] = (dov * h_ref[...] * _gelu_grad(gate)).astype(bf16)
        adh = a * dh
        ec = ec_s[...]
        last8 = _rows((SUBLANES, BRANCH)) == SUBLANES - 1
        for g in reversed(range(tm // SUBLANES)):
            sl = slice(g * SUBLANES, (g + 1) * SUBLANES)
            a_cum, e_loc = _scan_bwd(a[sl], adh[sl], SUBLANES)
            e = e_loc + a_cum * ec
            gt_s[sl, :] = dh[sl] + jnp.where(last8, ec, pltpu.roll(e, SUBLANES - 1, 0))
            ec = e[0:1, :]
        ec_s[...] = ec
        g_tot = gt_s[...]
        d_a = g_tot * hp_ref[...]
        d_mult = g_tot * ig * xc
        d_ix = g_tot * mult
        d_ig = d_ix * xc
        d_xc = d_ix * ig
        d_la = d_a * a - d_mult * (a * a) / mult
        d_r = -d_la * sp8
        dsp_ref[...] += jnp.sum(-d_la * r, axis=0, keepdims=True)
        dzr = d_r * r * (1.0 - r)
        dzi = d_ig * ig * (1.0 - ig)
        dzrb = dzr.astype(bf16)
        dzib = dzi.astype(bf16)
        d_xc = d_xc + _dot_nt(dzrb, wa_ref[...]) + _dot_nt(dzib, wx_ref[...])
        dwa_ref[...] += _dot_tn(xcb, dzrb)
        dwx_ref[...] += _dot_tn(xcb, dzib)
        dba_ref[...] += jnp.sum(dzr, axis=0, keepdims=True)
        dbx_ref[...] += jnp.sum(dzi, axis=0, keepdims=True)
        dcb_ref[...] += jnp.sum(d_xc, axis=0, keepdims=True)
        nxt = nxt_s[...]
        xb = xb_ref[...]
        dxb = cw_ref[3:4, :] * d_xc
        dcw_ref[3:4, :] += jnp.sum(d_xc * xb, axis=0, keepdims=True)
        for j in range(1, 4):
            sh = jnp.where(row < tm - j, pltpu.roll(d_xc, tm - j, 0), pltpu.roll(nxt, tm - j, 0))
            dxb = dxb + cw_ref[3 - j:4 - j, :] * sh
            dcw_ref[3 - j:4 - j, :] += jnp.sum(sh * xb, axis=0, keepdims=True)
        nxt_s[...] = d_xc
        dxb_ref[...] = dxb.astype(bf16)

    wa = cst["rg_wa"].astype(bf16)
    wx = cst["rg_wx"].astype(bf16)
    dxb, dgate, d_cw, d_cb, d_wa, d_ba, d_wx, d_bx, d_sp = _rt(
        "rg_bwd_" + tag, body, rows, tm,
        [(d_out, BRANCH, 0), (proj, BRANCH, U_COL + 5), (proj, BRANCH, U_COL + 6), (xc_saved, BRANCH, 0),
         (h_saved, BRANCH, 0), (hp_saved, BRANCH, 0)],
        [cst["rg_cw"], wa, cst["rg_ba"], wx, cst["rg_bx"], cst["rg_sp8"]],
        [(BRANCH, bf16), (BRANCH, bf16)],
        acc_outs=[(4, BRANCH), (1, BRANCH), (BRANCH, BRANCH), (1, BRANCH), (BRANCH, BRANCH), (1, BRANCH), (1, BRANCH)],
        scratch=[pltpu.VMEM((tm, BRANCH), f32), pltpu.VMEM((1, BRANCH), f32), pltpu.VMEM((tm, BRANCH), f32)],
        reverse=True)
    dcst = {"rg_cw": d_cw, "rg_cb": d_cb, "rg_wa": d_wa, "rg_ba": d_ba, "rg_wx": d_wx, "rg_bx": d_bx, "rg_sp8": d_sp}
    return dxb, dgate, dcst


def _merge_fwd(tag, proj, outs, bp, rows):
    def body(i, ya_ref, yb_ref, yc_ref, gm_ref, p_ref, m_ref):
        acc = None
        for n, y_ref in enumerate((ya_ref, yb_ref, yc_ref)):
            up = _dot(y_ref[...], p_ref[n])
            term = _sigmoid(gm_ref[:, n * D_MODEL:(n + 1) * D_MODEL]) * up
            acc = term if acc is None else acc + term
        m_ref[...] = acc.astype(bf16)
    return _rt("merge_fwd_" + tag, body, rows, ROW_TILE,
               [(outs[0], BRANCH, 0), (outs[1], BRANCH, 0), (outs[2], BRANCH, 0), (proj, GM_WIDTH, 0)],
               [bp], [(D_MODEL, bf16)])[0]


def _merge_bwd(tag, proj, outs, bp, dmerged, rows):
    def body(i, dm_ref, ya_ref, yb_ref, yc_ref, gm_ref, p_ref, da_ref, db_ref, dc_ref, dgm_ref, dp_ref):
        dm = dm_ref[...]
        for n, (y_ref, dy_ref) in enumerate(((ya_ref, da_ref), (yb_ref, db_ref), (yc_ref, dc_ref))):
            yv = y_ref[...]
            up = _dot(yv, p_ref[n])
            gt = _sigmoid(gm_ref[:, n * D_MODEL:(n + 1) * D_MODEL])
            dup = (dm * gt).astype(bf16)
            dgm_ref[:, n * D_MODEL:(n + 1) * D_MODEL] = (dm * up * gt * (1.0 - gt)).astype(bf16)
            dy_ref[...] = _dot_nt(dup, p_ref[n])
            dp_ref[n] += _dot_tn(yv, dup)
    return _rt("merge_bwd_" + tag, body, rows, ROW_TILE,
               [(dmerged, D_MODEL, 0), (outs[0], BRANCH, 0), (outs[1], BRANCH, 0), (outs[2], BRANCH, 0),
                (proj, GM_WIDTH, 0)],
               [bp], [(BRANCH, f32), (BRANCH, f32), (BRANCH, f32), (GM_WIDTH, bf16)],
               acc_outs=[(N_BRANCH, BRANCH, D_MODEL)])


def _block_diag(blocks):
    g, r, c = blocks.shape
    on_diag = (lax.broadcasted_iota(jnp.int32, (g * r, g * c), 0) // r
               == lax.broadcasted_iota(jnp.int32, (g * r, g * c), 1) // c)
    tiled = jnp.broadcast_to(blocks.reshape(g * r, 1, c), (g * r, g, c)).reshape(g * r, g * c)
    return jnp.where(on_diag, tiled, 0.0)


def _prep_consts(sp):
    p = jax.nn.softmax(sp["hg_lb_logits"], axis=0)
    lower = jnp.cumsum(p, axis=0) - p[0]
    out = []
    for l in range(DEPTH):
        lr = jnp.minimum(sp["s5_lambda_re"][l], S5_EIG_MAX)
        li = sp["s5_lambda_im"][l]
        dt = jnp.exp(sp["s5_log_dt"][l])[:, None]
        mag = jnp.exp(lr * dt)
        ar = mag * jnp.cos(li * dt)
        ai = mag * jnp.sin(li * dt)
        den = lr * lr + li * li
        fr = ((ar - 1.0) * lr + ai * li) / den
        fi = (ai * lr - (ar - 1.0) * li) / den
        br, bi = sp["s5_b_re"][l], sp["s5_b_im"][l]
        bbr = fr[..., None] * br - fi[..., None] * bi
        bbi = fr[..., None] * bi + fi[..., None] * br
        c = {
            "a_re": ar.reshape(1, S5_LANES), "a_im": ai.reshape(1, S5_LANES),
            "b_re": _block_diag(bbr.transpose(0, 2, 1)), "b_im": _block_diag(bbi.transpose(0, 2, 1)),
            "c_re": _block_diag(sp["s5_c_re"][l].transpose(0, 2, 1)),
            "c_im": -_block_diag(sp["s5_c_im"][l].transpose(0, 2, 1)),
            "s5_d": sp["s5_d"][l][None], "glu_b": sp["s5_glu_b"][l][None],
            "hg_lb": lower[l][None], "hg_nw": sp["hg_norm_w"][l][None],
            "rg_cw": sp["rg_conv_w"][l], "rg_cb": sp["rg_conv_b"][l][None],
            "rg_wa": _block_diag(sp["rg_wa"][l]), "rg_ba": sp["rg_ba"][l][None],
            "rg_wx": _block_diag(sp["rg_wx"][l]), "rg_bx": sp["rg_bx"][l][None],
            "rg_sp8": (RG_C * jax.nn.softplus(-sp["rg_lambda"][l]))[None],
        }
        out.append(c)
    return out


def _mixer_fwd(tag, x, hb, w_in, bp, w_out, cst, next_nw, rows):
    proj = _mm1("mix_proj_" + tag, hb, w_in, "nn", rows, IN_TOTAL, D_MODEL, 512, IN_TOTAL // 4, D_MODEL)
    cst = dict(cst)
    out_a, sv_a = _s5_fwd(tag, proj, cst, rows)
    out_b, sv_b = _hg_fwd(tag, proj, cst, rows)
    out_c, sv_c = _rg_fwd(tag, proj, cst, rows)
    merged = _merge_fwd(tag, proj, (out_a, out_b, out_c), bp, rows)
    x_out, hb_out = _mm("mix_out_" + tag, [merged], [w_out], [(0, 0, 0)], 1, "nn", rows, D_MODEL, D_MODEL,
                        512, D_MODEL, D_MODEL, [f32, bf16], _residual_then_norm(1.0), extras=[x], vecs=[next_nw])
    return x_out, hb_out, (x, hb, proj, (out_a, out_b, out_c), merged, sv_a, sv_b, sv_c)


def _mixer_bwd(tag, saved, nw, w_in, bp, w_out, cst, dx, dxb, rows):
    x, hb, proj, outs, merged, sv_a, sv_b, sv_c = saved
    d_wout = _mm1("mix_dwout_" + tag, merged, dxb, "tn", D_MODEL, D_MODEL, rows, D_MODEL, D_MODEL, TOKEN_K,
                  out_dtype=bf16)
    dmerged = _mm1("mix_dmerged_" + tag, dxb, w_out, "nt", rows, D_MODEL, D_MODEL, 512, D_MODEL, D_MODEL)
    d_a, d_b, d_c, dgm, d_bp = _merge_bwd(tag, proj, outs, bp, dmerged, rows)
    dxbc, dgatec, dcst_c = _rg_bwd(tag, sv_c, proj, cst, d_c, rows)
    dq, dz, dv, dg, dcst_b = _hg_bwd(tag, sv_b, proj, cst, d_b, rows)
    du, dcst_a, d_glu_w = _s5_bwd(tag, sv_a, proj, cst, d_a, rows)
    dproj = jnp.concatenate([dgm, du, dq, dz, dv, dg, dxbc, dgatec], axis=1)
    d_win = _mm1("mix_dwin_" + tag, hb, dproj, "tn", D_MODEL, IN_TOTAL, rows, D_MODEL, IN_TOTAL // 4, TOKEN_K,
                 out_dtype=bf16)
    dx_in, dxb_in, d_nw = _mm("mix_dh_" + tag, [dproj], [w_in], [(0, 0, 0)], 1, "nt", rows, D_MODEL, IN_TOTAL,
                              512, D_MODEL, IN_TOTAL // 2, [f32, bf16], _norm_bwd_epilogue, extras=[x, dx], vecs=[nw],
                              n_part=1)
    dcst = {**dcst_a, **dcst_b, **dcst_c}
    return dx_in, dxb_in, jnp.sum(d_nw, axis=0), d_win, d_bp, d_wout, d_glu_w, dcst


def _local_step(x, target, weights_of, small, grads_done):
    rows = x.shape[0]
    consts, consts_vjp = jax.vjp(_prep_consts, small)
    norm_w = small["norm_w"]
    saved = []
    h = x
    hb = _rms_fwd("first_norm", x, norm_w[0, 0][None], rows)
    for l in range(DEPTH):
        t = str(l)
        after = [norm_w[l + 1, 0][None]] if l + 1 < DEPTH else []
        wa = weights_of(l, "a")
        h, hb, sv0 = _ffn_fwd(t + "a", h, hb, *wa, [norm_w[l, 1][None]], rows)
        wm = weights_of(l, "mix")
        cst = dict(consts[l])
        cst["glu_w"] = wm[3]
        h, hb, sv1 = _mixer_fwd(t, h, hb, *wm[:3], cst, norm_w[l, 2][None], rows)
        wb = weights_of(l, "b")
        h, hb, sv2 = _ffn_fwd(t + "b", h, hb, *wb, after, rows)
        saved.append((sv0, sv1, sv2, cst, wa, wm, wb))
    dx, dxb, loss, d_fnw = _loss_head(h, small["final_norm_w"][None], target, rows)
    d_norm = [None] * DEPTH
    d_consts = [None] * DEPTH
    for l in reversed(range(DEPTH)):
        t = str(l)
        sv0, sv1, sv2, cst, wa, wm, wb = saved[l]
        dx, dxb, dn2, dg1, du1, dd1 = _ffn_bwd(t + "b", sv2, norm_w[l, 2][None], *wb, dx, dxb, rows)
        grads_done(l, "b", [dg1, du1, dd1])
        dx, dxb, dn1, d_win, d_bp, d_wout, d_glu_w, dcst = _mixer_bwd(
            t, sv1, norm_w[l, 1][None], *wm[:3], cst, dx, dxb, rows)
        grads_done(l, "mix", [d_win, d_bp, d_wout, d_glu_w])
        dx, dxb, dn0, dg0, du0, dd0 = _ffn_bwd(t + "a", sv0, norm_w[l, 0][None], *wa, dx, dxb, rows)
        grads_done(l, "a", [dg0, du0, dd0])
        d_norm[l] = jnp.concatenate([dn0, dn1, dn2], axis=0)
        d_consts[l] = dcst
    (g_small,) = consts_vjp(d_consts)
    g_small = dict(g_small)
    g_small["norm_w"] = g_small["norm_w"] + jnp.stack(d_norm)
    g_small["final_norm_w"] = g_small["final_norm_w"] + d_fnw[0]
    return loss[0, 0], dx, g_small


def _other_chips(x, y):
    return [(1 - x, y), (x, 1 - y), (1 - x, 1 - y)]


def _gather_over_ici(shards):
    n = len(shards)

    def copies(in_refs, out_refs, send_sems, recv_sems):
        x, y, c = _place()
        cps = []
        for i in range(n):
            mine = out_refs[i].at[4 * x + 2 * y + c]
            cps.append(pltpu.make_async_copy(in_refs[i], mine, send_sems.at[5 * i + 4]))
            for k, to in enumerate([(x, y, 1 - c)] + [(px, py, c) for px, py in _other_chips(x, y)]):
                cps.append(pltpu.make_async_remote_copy(
                    src_ref=in_refs[i], dst_ref=mine, send_sem=send_sems.at[5 * i + k],
                    recv_sem=recv_sems.at[5 * i + k], device_id=to, device_id_type=MESH_IDS))
        return cps

    return _Carry(shards, [jax.ShapeDtypeStruct((N_DEV,) + s.shape, s.dtype) for s in shards], 5 * n, copies)


def _gather_forward(name, landings):
    n = len(landings)

    def body(*refs):
        in_refs, out_refs = refs[:n], refs[n:2 * n]
        send_sems, recv_sems = refs[2 * n:]
        x, y, c = _place()
        cps = []
        for i in range(n):
            for j, (px, py) in enumerate(_other_chips(x, y)):
                block = 4 * px + 2 * py + c
                cps.append(pltpu.make_async_remote_copy(
                    src_ref=in_refs[i].at[block], dst_ref=out_refs[i].at[block], send_sem=send_sems.at[3 * i + j],
                    recv_sem=recv_sems.at[3 * i + j], device_id=(x, y, 1 - c), device_id_type=MESH_IDS))
        for cp in cps:
            cp.start()
        for cp in cps:
            cp.wait()

    return pl.pallas_call(
        body, name=name, out_shape=[jax.ShapeDtypeStruct(a.shape, a.dtype) for a in landings],
        in_specs=[ANY_SPEC] * n, out_specs=[ANY_SPEC] * n, input_output_aliases={i: i for i in range(n)},
        scratch_shapes=[pltpu.SemaphoreType.DMA((3 * n,)), pltpu.SemaphoreType.DMA((3 * n,))],
    )(*landings)


def _all_gather(name, shards):
    n = len(shards)

    def body(*refs):
        x_refs, out_refs = refs[:n], refs[n:2 * n]
        send_sems, recv_sems, local_sems = refs[2 * n:]
        x, y, c = _place()
        me, sibling = (x, y, c), (x, y, 1 - c)
        chips = [(1 - x, y), (x, 1 - y), (1 - x, 1 - y)]

        def blk(i, px, py, pc):
            return out_refs[i].at[4 * px + 2 * py + pc]

        def copy(i, k, block, to, src=None):
            return pltpu.make_async_remote_copy(
                src_ref=blk(i, *block) if src is None else src, dst_ref=blk(i, *block),
                send_sem=send_sems.at[7 * i + k], recv_sem=recv_sems.at[7 * i + k], device_id=to,
                device_id_type=MESH_IDS)

        mine = [pltpu.make_async_copy(x_refs[i], blk(i, *me), local_sems.at[i]) for i in range(n)]
        for cp in mine:
            cp.start()
        first = []
        for i in range(n):
            first.append(copy(i, 0, me, sibling, src=x_refs[i]))
            first += [copy(i, 1 + j, me, (*chip, c), src=x_refs[i]) for j, chip in enumerate(chips)]
        for cp in first:
            cp.start()
        passed = []
        for j, chip in enumerate(chips):
            for i in range(n):
                copy(i, 1 + j, (*chip, c), me).wait_recv()
                fwd = copy(i, 4 + j, (*chip, c), sibling)
                fwd.start()
                passed.append(fwd)
        for i in range(n):
            copy(i, 0, sibling, me).wait_recv()
            for j, chip in enumerate(chips):
                copy(i, 4 + j, (*chip, 1 - c), me).wait_recv()
        for cp in first + passed:
            cp.wait_send()
        for cp in mine:
            cp.wait()

    return pl.pallas_call(
        body, name=name, out_shape=[jax.ShapeDtypeStruct((N_DEV,) + s.shape, s.dtype) for s in shards],
        in_specs=[ANY_SPEC] * n, out_specs=[ANY_SPEC] * n,
        scratch_shapes=[pltpu.SemaphoreType.DMA((7 * n,)), pltpu.SemaphoreType.DMA((7 * n,)),
                        pltpu.SemaphoreType.DMA((n,))],
    )(*shards)


def _row_tile(rows):
    return rows if rows <= 512 else next(t for t in range(512, 7, -8) if rows % t == 0)


def _sums_over_ici(chip_sums):
    n = len(chip_sums)

    def copies(in_refs, out_refs, send_sems, recv_sems):
        x, y, c = _place()
        return [pltpu.make_async_remote_copy(
            src_ref=in_refs[i].at[2 * px + py], dst_ref=out_refs[i].at[k], send_sem=send_sems.at[3 * i + k],
            recv_sem=recv_sems.at[3 * i + k], device_id=(px, py, c), device_id_type=MESH_IDS)
            for i in range(n) for k, (px, py) in enumerate(_other_chips(x, y))]

    return _Carry(chip_sums, [jax.ShapeDtypeStruct((3,) + t.shape[1:], t.dtype) for t in chip_sums], 3 * n, copies)


def _reduce_scatter(tag, parts, hosts=None):
    n = len(parts)
    _, _, c = _place()

    def body_pair(*refs):
        p_refs, got_refs = refs[:n], refs[n:2 * n]
        send_sems, recv_sems = refs[2 * n:]
        x, y, c = _place()
        cps = [pltpu.make_async_remote_copy(
            src_ref=p_refs[i].at[:, 1 - c], dst_ref=got_refs[i], send_sem=send_sems.at[i], recv_sem=recv_sems.at[i],
            device_id=(x, y, 1 - c), device_id_type=MESH_IDS) for i in range(n)]
        for cp in cps:
            cp.start()
        for cp in cps:
            cp.wait()

    from_sibling = pl.pallas_call(
        body_pair, name="rs_pair_" + tag,
        out_shape=[jax.ShapeDtypeStruct((4,) + p.shape[2:], p.dtype) for p in parts],
        in_specs=[ANY_SPEC] * n, out_specs=[ANY_SPEC] * n,
        scratch_shapes=[pltpu.SemaphoreType.DMA((n,)), pltpu.SemaphoreType.DMA((n,))],
    )(*parts)

    def body_add(idx_ref, *refs):
        p = pl.program_id(0)
        for q in range(n):
            @pl.when(p == q)
            def _(p_ref=refs[q], g_ref=refs[n + q], o_ref=refs[2 * n + q]):
                o_ref[...] = (p_ref[...].astype(f32) + g_ref[...].astype(f32)).astype(o_ref.dtype)

    def at(q):
        return lambda p, j, idx: jnp.clip(j + 4 * (p - q), 0, 3)

    in_specs, out_specs = [], []
    for q, part in enumerate(parts):
        in_specs.append(pl.BlockSpec((None, None) + part.shape[2:],
                                     lambda p, j, idx, blk=at(q): (blk(p, j, idx), idx[0], 0, 0)))
    for q, part in enumerate(parts):
        spec = pl.BlockSpec((None,) + part.shape[2:], lambda p, j, idx, blk=at(q): (blk(p, j, idx), 0, 0))
        in_specs.append(spec)
        out_specs.append(spec)
    chip_sums = pl.pallas_call(
        body_add, name="rs_pair_sum_" + tag,
        out_shape=[jax.ShapeDtypeStruct((4,) + p.shape[2:], p.dtype) for p in parts],
        grid_spec=pltpu.PrefetchScalarGridSpec(num_scalar_prefetch=1, grid=(n, 4), in_specs=in_specs,
                                               out_specs=out_specs),
        compiler_params=_cparams(("arbitrary", "arbitrary")),
    )(jnp.stack([c]).astype(jnp.int32), *parts, *from_sibling)

    others = [None] * n
    riding = set()
    for host, which in (hosts or {}).items():
        rider = _sums_over_ici([chip_sums[i] for i in which])
        _CARRIED[host] = rider
        for pos, i in enumerate(which):
            others[i] = functools.partial(lambda r, p: r.outs[p], rider, pos)
        riding.update(which)
    rest = [i for i in range(n) if i not in riding]
    if rest:
        alone = _sums_over_ici([chip_sums[i] for i in rest])

        def body_chips(*refs):
            k = len(rest)
            cps = alone.copies(refs[:k], refs[k:2 * k], *refs[2 * k:])
            for cp in cps:
                cp.start()
            for cp in cps:
                cp.wait()

        from_chips = pl.pallas_call(
            body_chips, name="rs_chips_" + tag, out_shape=alone.out_shapes,
            in_specs=[ANY_SPEC] * len(rest), out_specs=[ANY_SPEC] * len(rest), scratch_shapes=alone.sems(),
        )(*alone.ins)
        for pos, i in enumerate(rest):
            others[i] = functools.partial(lambda got: got, from_chips[pos])
    return list(zip(chip_sums, others))


def _own_index():
    x, y, _ = _place()
    return jnp.stack([2 * x + y]).astype(jnp.int32)


def _own_total(name, chip_sum, others):
    _, r, cols = chip_sum.shape
    tr = _row_tile(r)

    def body(idx_ref, t_ref, g_ref, o_ref):
        o_ref[...] = ((t_ref[...].astype(f32) + g_ref[0].astype(f32)) + g_ref[1].astype(f32)) + g_ref[2].astype(f32)

    return pl.pallas_call(
        body, name=name, out_shape=jax.ShapeDtypeStruct((r, cols), f32),
        grid_spec=pltpu.PrefetchScalarGridSpec(
            num_scalar_prefetch=1, grid=(r // tr,),
            in_specs=[pl.BlockSpec((None, tr, cols), lambda t, idx: (idx[0], t, 0)),
                      pl.BlockSpec((3, tr, cols), lambda t, idx: (0, t, 0))],
            out_specs=pl.BlockSpec((tr, cols), lambda t, idx: (t, 0))),
        compiler_params=_cparams(("parallel",)),
    )(_own_index(), chip_sum, others)


def _adam_update(w, gv, m, v):
    m_new = ADAM_B1 * m + (1.0 - ADAM_B1) * gv
    v_new = ADAM_B2 * v + (1.0 - ADAM_B2) * (gv * gv)
    m_hat = m_new / (1.0 - ADAM_B1 ** ADAM_STEP)
    v_hat = v_new / (1.0 - ADAM_B2 ** ADAM_STEP)
    return -ADAM_LR * (m_hat / (jnp.sqrt(v_hat) + ADAM_EPS) + ADAM_WD * w), m_new, v_new


def _adamw_reduced(name, w, pieces, m, v):
    n_p, rows, cols = w.shape
    tr = _row_tile(rows)

    def body(idx_ref, w_ref, *refs):
        red = refs[:2 * n_p]
        m_ref, v_ref, g_ref, d_ref, nm_ref, nv_ref = refs[2 * n_p:]
        p = pl.program_id(0)
        for q in range(n_p):
            @pl.when(p == q)
            def _(t_ref=red[2 * q], o_ref=red[2 * q + 1]):
                gv = ((t_ref[...].astype(f32) + o_ref[0].astype(f32)) + o_ref[1].astype(f32)) + o_ref[2].astype(f32)
                g_ref[...] = gv
                d_ref[...], nm_ref[...], nv_ref[...] = _adam_update(w_ref[...], gv, m_ref[...], v_ref[...])

    spec = pl.BlockSpec((None, tr, cols), lambda p, t, idx: (p, t, 0))
    red_specs, red_args = [], []
    for q, (chip_sum, others) in enumerate(pieces):
        red_specs.append(pl.BlockSpec((None, tr, cols), lambda p, t, idx, q=q: (idx[0], jnp.where(p == q, t, 0), 0)))
        red_specs.append(pl.BlockSpec((3, tr, cols), lambda p, t, idx, q=q: (0, jnp.where(p == q, t, 0), 0)))
        red_args += [chip_sum, others]
    return pl.pallas_call(
        body, name=name, out_shape=[jax.ShapeDtypeStruct((n_p, rows, cols), f32)] * 4,
        grid_spec=pltpu.PrefetchScalarGridSpec(
            num_scalar_prefetch=1, grid=(n_p, rows // tr),
            in_specs=[spec] + red_specs + [spec, spec], out_specs=[spec] * 4),
        compiler_params=_cparams(("parallel", "parallel")),
    )(_own_index(), w, *red_args, m, v)


def _adamw(name, w, g, m, v):
    rows, cols = w.shape
    tr = _row_tile(rows)

    def body(w_ref, g_ref, m_ref, v_ref, d_ref, nm_ref, nv_ref):
        d_ref[...], nm_ref[...], nv_ref[...] = _adam_update(w_ref[...], g_ref[...], m_ref[...], v_ref[...])

    spec = pl.BlockSpec((tr, cols), lambda i: (i, 0))
    return pl.pallas_call(
        body, name=name, grid=(rows // tr,), in_specs=[spec] * 4, out_specs=[spec] * 3,
        out_shape=[jax.ShapeDtypeStruct((rows, cols), f32)] * 3, compiler_params=_cparams(("parallel",)),
    )(w, g, m, v)


WEIGHT_NAMES = ["norm_w", "final_norm_w", "ffn_gate", "ffn_up", "ffn_down", "w_in", "branch_proj", "w_out",
                "s5_lambda_re", "s5_lambda_im", "s5_log_dt", "s5_b_re", "s5_b_im", "s5_c_re", "s5_c_im", "s5_d",
                "s5_glu_w", "s5_glu_b", "hg_lb_logits", "hg_norm_w", "rg_conv_w", "rg_conv_b", "rg_wa", "rg_ba",
                "rg_wx", "rg_bx", "rg_lambda"]
SHARDED = {"ffn_gate": (3, "gate"), "ffn_up": (3, "up"), "ffn_down": (2, "down"), "w_in": (2, "w_in"),
           "branch_proj": (3, "bp"), "w_out": (1, "w_out"), "s5_glu_w": (1, "glu_w"),
           "norm_w": (2, None), "rg_conv_w": (2, None)}
BIG = ["ffn_gate", "ffn_up", "ffn_down", "w_in", "branch_proj", "w_out", "s5_glu_w"]
PARTS = {"a": [("ffn_gate", 0, 1), ("ffn_up", 0, 1), ("ffn_down", 0, 0)],
         "b": [("ffn_gate", 1, 1), ("ffn_up", 1, 1), ("ffn_down", 1, 0)],
         "mix": [("w_in", None, 1), ("branch_proj", None, 2), ("w_out", None, 0), ("s5_glu_w", None, 0)]}
AG_HOSTS = {"ffn_up_0a": (0, "mix", [0]), "ffn_down_0a": (0, "mix", [1, 2, 3]), "mix_proj_0": (0, "b", [0, 1, 2]),
            "s5_out_0": (1, "a", [0]), "hg_fwd_0": (1, "a", [1]), "rg_fwd_0": (1, "a", [2]),
            "merge_fwd_0": (1, "b", [0]), "ffn_up_0b": (1, "b", [1]), "ffn_down_0b": (1, "b", [2]),
            "s5_scan_fwd_0": (1, "mix", [0, 1]), "mix_out_0": (1, "mix", [2, 3])}
RS_HOSTS = {(1, "b"): {"s5_scan_bwd_1": [0, 1, 2]},
            (1, "mix"): {"ffn_bwd_mid_1a": [1, 2, 3], "mix_dh_0": [0]},
            (1, "a"): {"mix_dwin_0": [0, 1], "merge_bwd_0": [2]},
            (0, "b"): {"s5_scan_bwd_0": [0, 1, 2]},
            (0, "mix"): {"ffn_bwd_mid_0a": [1, 2, 3], "ffn_dh_0a": [0]}}
SMALL_SHARDED = ["norm_w", "rg_conv_w"]
REPLICATED = [n for n in WEIGHT_NAMES if n not in SHARDED]
LANES = 128


PACK_ROWS = 512


def _pack_rows(arrays, names):
    pieces = []
    for n in names:
        flat = arrays[n].reshape(-1)
        pieces.append(jnp.pad(flat, (0, -flat.shape[0] % LANES)).reshape(-1, LANES))
    rows = jnp.concatenate(pieces, axis=0)
    return jnp.pad(rows, ((0, -rows.shape[0] % PACK_ROWS), (0, 0)))


def _unpack_rows(rows, names, like):
    out, r0 = {}, 0
    for n in names:
        size = math.prod(like[n].shape)
        nrows = -(-size // LANES)
        out[n] = rows[r0:r0 + nrows].reshape(-1)[:size].reshape(like[n].shape)
        r0 += nrows
    return out


def _unshard(gathered, axis):
    g = jnp.moveaxis(gathered, 0, axis)
    shp = g.shape
    return g.reshape(shp[:axis] + (shp[axis] * shp[axis + 1],) + shp[axis + 2:])


RELAYOUT_ROWS = 256


def _column_runs(width, first_col):
    total = N_DEV * width
    runs = []
    for j in range(N_DEV):
        start = (width * j + first_col) % total
        head = min(width, total - start)
        runs.append((j, 0, start, head))
        if head < width:
            runs.append((j, head, 0, width - head))
    return runs


def _unshard_columns(name, gathered, first_col=0):
    _, r, c = gathered.shape
    tr = min(RELAYOUT_ROWS, r)
    runs = _column_runs(c, first_col)

    def body(g_ref, o_ref):
        for j, off, dst, length in runs:
            o_ref[:, dst:dst + length] = g_ref[j, :, off:off + length]

    return pl.pallas_call(
        body, name=name, grid=(r // tr,), in_specs=[pl.BlockSpec((N_DEV, tr, c), lambda i: (0, i, 0))],
        out_specs=pl.BlockSpec((tr, N_DEV * c), lambda i: (i, 0)),
        out_shape=jax.ShapeDtypeStruct((r, N_DEV * c), gathered.dtype), compiler_params=_cparams(("parallel",)),
    )(gathered)


def _columns_to_blocks(name, full, first_col=0):
    r, total = full.shape
    c = total // N_DEV
    tr = min(RELAYOUT_ROWS, r)
    runs = _column_runs(c, first_col)

    def body(x_ref, o_ref):
        for j, off, src, length in runs:
            o_ref[j // 2, j % 2, :, off:off + length] = x_ref[:, src:src + length].astype(bf16)

    return pl.pallas_call(
        body, name=name, grid=(r // tr,), in_specs=[pl.BlockSpec((tr, total), lambda i: (i, 0))],
        out_specs=pl.BlockSpec((4, 2, tr, c), lambda i: (0, 0, i, 0)),
        out_shape=jax.ShapeDtypeStruct((4, 2, r, c), bf16), compiler_params=_cparams(("parallel",)),
    )(full)


def _to_blocks(full, axis):
    shp = full.shape
    g = full.reshape(shp[:axis] + (4, 2, shp[axis] // N_DEV) + shp[axis + 1:])
    g = jnp.moveaxis(g, (axis, axis + 1), (0, 1))
    return g.reshape(4, 2, -1, g.shape[-1])


W_IN_SPLIT = IN_TOTAL - GM_WIDTH


def kernel(x, norm_w, final_norm_w, ffn_gate, ffn_up, ffn_down, w_in, branch_proj, w_out, s5_lambda_re, s5_lambda_im, s5_log_dt, s5_b_re, s5_b_im, s5_c_re, s5_c_im, s5_d, s5_glu_w, s5_glu_b, hg_lb_logits, hg_norm_w, rg_conv_w, rg_conv_b, rg_wa, rg_ba, rg_wx, rg_bx, rg_lambda, loss_target, m_norm_w, m_final_norm_w, m_ffn_gate, m_ffn_up, m_ffn_down, m_w_in, m_branch_proj, m_w_out, m_s5_lambda_re, m_s5_lambda_im, m_s5_log_dt, m_s5_b_re, m_s5_b_im, m_s5_c_re, m_s5_c_im, m_s5_d, m_s5_glu_w, m_s5_glu_b, m_hg_lb_logits, m_hg_norm_w, m_rg_conv_w, m_rg_conv_b, m_rg_wa, m_rg_ba, m_rg_wx, m_rg_bx, m_rg_lambda, v_norm_w, v_final_norm_w, v_ffn_gate, v_ffn_up, v_ffn_down, v_w_in, v_branch_proj, v_w_out, v_s5_lambda_re, v_s5_lambda_im, v_s5_log_dt, v_s5_b_re, v_s5_b_im, v_s5_c_re, v_s5_c_im, v_s5_d, v_s5_glu_w, v_s5_glu_b, v_hg_lb_logits, v_hg_norm_w, v_rg_conv_w, v_rg_conv_b, v_rg_wa, v_rg_ba, v_rg_wx, v_rg_bx, v_rg_lambda):
    w = dict(zip(WEIGHT_NAMES, (norm_w, final_norm_w, ffn_gate, ffn_up, ffn_down, w_in, branch_proj, w_out,
                                s5_lambda_re, s5_lambda_im, s5_log_dt, s5_b_re, s5_b_im, s5_c_re, s5_c_im, s5_d,
                                s5_glu_w, s5_glu_b, hg_lb_logits, hg_norm_w, rg_conv_w, rg_conv_b, rg_wa, rg_ba,
                                rg_wx, rg_bx, rg_lambda)))
    m = dict(zip(WEIGHT_NAMES, (m_norm_w, m_final_norm_w, m_ffn_gate, m_ffn_up, m_ffn_down, m_w_in, m_branch_proj,
                                m_w_out, m_s5_lambda_re, m_s5_lambda_im, m_s5_log_dt, m_s5_b_re, m_s5_b_im, m_s5_c_re,
                                m_s5_c_im, m_s5_d, m_s5_glu_w, m_s5_glu_b, m_hg_lb_logits, m_hg_norm_w, m_rg_conv_w,
                                m_rg_conv_b, m_rg_wa, m_rg_ba, m_rg_wx, m_rg_bx, m_rg_lambda)))
    v = dict(zip(WEIGHT_NAMES, (v_norm_w, v_final_norm_w, v_ffn_gate, v_ffn_up, v_ffn_down, v_w_in, v_branch_proj,
                                v_w_out, v_s5_lambda_re, v_s5_lambda_im, v_s5_log_dt, v_s5_b_re, v_s5_b_im, v_s5_c_re,
                                v_s5_c_im, v_s5_d, v_s5_glu_w, v_s5_glu_b, v_hg_lb_logits, v_hg_norm_w, v_rg_conv_w,
                                v_rg_conv_b, v_rg_wa, v_rg_ba, v_rg_wx, v_rg_bx, v_rg_lambda)))
    rows = x.shape[1]

    _CARRIED.clear()

    def shard_of(piece, l):
        n, k, _ = piece
        return (w[n][l] if k is None else w[n][l, k]).astype(bf16)

    def assemble(l, part, gathered):
        full = []
        for j, (piece, g) in enumerate(zip(PARTS[part], gathered)):
            tag = "unshard_%d%s%d" % (l, part, j)
            if piece[0] == "w_in":
                full.append(_unshard_columns(tag, g, first_col=GM_WIDTH))
            elif piece[0] == "branch_proj":
                full.append(_unshard_columns(tag, g.reshape(N_DEV, -1, g.shape[-1])).reshape(N_BRANCH, BRANCH, D_MODEL))
            elif piece[2] == g.ndim - 2:
                full.append(_unshard_columns(tag, g))
            else:
                full.append(_unshard(g, piece[2]))
        return full

    n_a = len(PARTS["a"])
    first = _all_gather("gather_weights", [shard_of(p, 0) for p in PARTS["a"]] + [w[n] for n in SMALL_SHARDED])
    small = {n: w[n] for n in REPLICATED}
    for n, g in zip(SMALL_SHARDED, first[n_a:]):
        small[n] = _unshard(g, SHARDED[n][0])
    riders = {}
    for host, (l, part, which) in AG_HOSTS.items():
        rider = _gather_over_ici([shard_of(PARTS[part][j], l) for j in which])
        _CARRIED[host] = rider
        riders.setdefault((l, part), []).append((which, rider))

    def weights_of(l, part):
        if (l, part) == (0, "a"):
            return assemble(l, part, first[:n_a])
        landed = [None] * len(PARTS[part])
        for which, rider in riders[l, part]:
            for j, buf in zip(which, rider.outs):
                landed[j] = buf
        return assemble(l, part, _gather_forward("gather_forward_%d%s" % (l, part), landed))

    sums = {}

    def blocks_of(l, part, grads):
        out = []
        for j, (piece, g) in enumerate(zip(PARTS[part], grads)):
            tag = "to_blocks_%d%s%d" % (l, part, j)
            if piece[0] == "w_in":
                out.append(_columns_to_blocks(tag, g, first_col=GM_WIDTH))
            elif piece[0] == "branch_proj":
                out.append(_columns_to_blocks(tag, g.reshape(-1, g.shape[-1])))
            elif piece[2] == g.ndim - 1:
                out.append(_columns_to_blocks(tag, g))
            else:
                out.append(_to_blocks(g, piece[2]).astype(bf16))
        return out

    last_grads = []

    def grads_done(l, part, grads):
        if (l, part) in RS_HOSTS:
            sums[l, part] = _reduce_scatter("%d%s" % (l, part), blocks_of(l, part, grads), hosts=RS_HOSTS[l, part])
        else:
            last_grads.extend(blocks_of(l, part, grads))

    loss_part, dx, g_small = _local_step(x[0], loss_target[0], weights_of, small, grads_done)
    loss = lax.psum(loss_part, ("x", "y", "c"))

    parts = last_grads + [_to_blocks(g_small[n], SHARDED[n][0]) for n in SMALL_SHARDED]
    rep_rows = _pack_rows(g_small, REPLICATED)
    rep_slice = rep_rows.shape[0] // N_DEV
    parts.append(rep_rows.reshape(4, 2, rep_slice, LANES))
    last = _reduce_scatter("last", parts)
    sums[0, "a"] = last[:n_a]

    grads, delta, new_m, new_v = {}, {}, {}, {}

    def update(n, pieces):
        shp = w[n].shape
        view = (len(pieces), -1, shp[-1])
        res = _adamw_reduced("adamw_" + n, w[n].reshape(view), [(t, others()) for t, others in pieces],
                             m[n].reshape(view), v[n].reshape(view))
        grads[n], delta[n], new_m[n], new_v[n] = (r.reshape(shp) for r in res)

    for n in BIG:
        update(n, [sums[l, part][j] for l in range(DEPTH) for part in ("a", "b", "mix")
                   for j, piece in enumerate(PARTS[part]) if piece[0] == n])
    for j, n in enumerate(SMALL_SHARDED):
        update(n, [last[n_a + j]])
    rep_mine = _own_total("rs_total_small", last[-1][0], last[-1][1]())
    rep_grads = _all_gather("gather_small_grads", [rep_mine])[0].reshape(-1, LANES)
    res = _adamw("adamw_small", _pack_rows(w, REPLICATED), rep_grads, _pack_rows(m, REPLICATED), _pack_rows(v, REPLICATED))
    for dst, src in zip((grads, delta, new_m, new_v), (rep_grads,) + tuple(res)):
        dst.update(_unpack_rows(src, REPLICATED, w))

    return (loss, dx.reshape(x.shape), *[grads[n] for n in WEIGHT_NAMES], *[delta[n] for n in WEIGHT_NAMES],
            *[new_m[n] for n in WEIGHT_NAMES], *[new_v[n] for n in WEIGHT_NAMES])
```

```python
import functools
import math

import jax
import jax.numpy as jnp
from jax import lax
from jax.experimental import pallas as pl
from jax.experimental.pallas import tpu as pltpu

f32 = jnp.float32
bf16 = jnp.bfloat16

D_MODEL = 1024
DEPTH = 2
BRANCH = 512
N_BRANCH = 3
S5_GROUP = 16
S5_GROUPS = 32
S5_STATE = 64
S5_LANES = S5_GROUPS * S5_STATE
S5_EIG_MAX = -1e-4
HG_HEADS = 4
HG_DK = 128
HG_CHUNK = 32
RG_BLOCKS = 8
RG_BLOCK = 64
RG_C = 8.0
D_FF = 2816
EPS = 1e-6
IN_TOTAL = 6656
GM_WIDTH = N_BRANCH * D_MODEL
N_DEV = 8

ADAM_LR = 0.001
ADAM_B1 = 0.9
ADAM_B2 = 0.999
ADAM_EPS = 1e-08
ADAM_WD = 0.01
ADAM_STEP = 10

VMEM_LIMIT_V7X = 56 * 1024 * 1024
ROW_TILE = 256
FF_TILE = 1408
TOKEN_K = 4096
MXU_COLS = 256


def _cparams(sem):
    return pltpu.CompilerParams(dimension_semantics=sem, vmem_limit_bytes=VMEM_LIMIT_V7X)


MESH_IDS = pl.DeviceIdType.MESH
ANY_SPEC = pl.BlockSpec(memory_space=pl.ANY)


def _place():
    return lax.axis_index("x"), lax.axis_index("y"), lax.axis_index("c")


class _Carry:
    def __init__(self, ins, out_shapes, n_sems, copies):
        self.ins, self.out_shapes, self.n_sems, self.copies = list(ins), list(out_shapes), n_sems, copies
        self.outs = None

    def sems(self):
        return [pltpu.SemaphoreType.DMA((self.n_sems,)), pltpu.SemaphoreType.DMA((self.n_sems,))]

    def start(self, when, *riders):
        @pl.when(when)
        def _():
            for cp in self.copies(*riders):
                cp.start()

    def finish(self, when, *riders):
        @pl.when(when)
        def _():
            for cp in self.copies(*riders):
                cp.wait()


_CARRIED = {}


def _call_with_rider(name, body, grid, in_specs, out_specs, out_shape, scratch, semantics, args):
    carry = _CARRIED.pop(name, None)
    if carry is None:
        return pl.pallas_call(body, name=name, grid=grid, in_specs=in_specs, out_specs=out_specs,
                              out_shape=out_shape, scratch_shapes=scratch, compiler_params=_cparams(semantics))(*args)
    n_in, n_out, nci, nco = len(in_specs), len(out_specs), len(carry.ins), len(carry.out_shapes)

    def kern(*refs):
        ids = [pl.program_id(d) for d in range(len(grid))]
        own = refs[:n_in] + refs[n_in + nci:n_in + nci + n_out] + refs[n_in + nci + n_out + nco:-2]
        riders = (refs[n_in:n_in + nci], refs[n_in + nci + n_out:n_in + nci + n_out + nco]) + tuple(refs[-2:])
        carry.start(functools.reduce(jnp.logical_and, [p == 0 for p in ids]), *riders)
        body(*own)
        carry.finish(functools.reduce(jnp.logical_and, [p == g - 1 for p, g in zip(ids, grid)]), *riders)

    res = pl.pallas_call(
        kern, name=name, grid=grid, in_specs=list(in_specs) + [ANY_SPEC] * nci,
        out_specs=list(out_specs) + [ANY_SPEC] * nco, out_shape=list(out_shape) + carry.out_shapes,
        scratch_shapes=list(scratch) + carry.sems(), compiler_params=_cparams(("arbitrary",) * len(grid)),
    )(*args, *carry.ins)
    carry.outs = res[n_out:]
    return res[:n_out]


def _sigmoid(x):
    return 0.5 * jnp.tanh(0.5 * x) + 0.5


def _sigmoid_small(x):
    return 1.0 / (1.0 + jnp.exp(-x))


_GELU_C = math.sqrt(2.0 / math.pi)


def _gelu(x):
    t = jnp.tanh(_GELU_C * (x + 0.044715 * x * x * x))
    return 0.5 * x * (1.0 + t)


def _gelu_grad(x):
    t = jnp.tanh(_GELU_C * (x + 0.044715 * x * x * x))
    return 0.5 * (1.0 + t) + 0.5 * x * (1.0 - t * t) * _GELU_C * (1.0 + 3.0 * 0.044715 * x * x)


def _expm1(x):
    p = x * (1.0 + x * (0.5 + x * (1.0 / 6 + x * (1.0 / 24 + x * (1.0 / 120 + x * (1.0 / 720))))))
    return jnp.where(jnp.abs(x) < 0.3, p, jnp.exp(x) - 1.0)


def _dot(a, b):
    return jnp.dot(a, b, preferred_element_type=f32)


def _dot_nt(a, b):
    return lax.dot_general(a, b, (((1,), (1,)), ((), ())), preferred_element_type=f32)


def _dot_tn(a, b):
    return lax.dot_general(a, b, (((0,), (0,)), ((), ())), preferred_element_type=f32)


def _bdot(a, b):
    return lax.dot_general(a, b, (((2,), (1,)), ((0,), (0,))), preferred_element_type=f32)


def _bdot_nt(a, b):
    return lax.dot_general(a, b, (((2,), (2,)), ((0,), (0,))), preferred_element_type=f32)


def _rows(shape):
    return lax.broadcasted_iota(jnp.int32, shape, 0)


def _scan_fwd(a, b, n):
    row = _rows(a.shape)
    s = 1
    while s < n:
        valid = row >= s
        sh_a = pltpu.roll(a, s, 0)
        sh_b = pltpu.roll(b, s, 0)
        b = b + a * jnp.where(valid, sh_b, 0.0)
        a = a * jnp.where(valid, sh_a, 1.0)
        s *= 2
    return a, b


def _scan_bwd(a, b, n):
    row = _rows(a.shape)
    s = 1
    while s < n:
        valid = row < n - s
        sh_a = pltpu.roll(a, n - s, 0)
        sh_b = pltpu.roll(b, n - s, 0)
        b = b + a * jnp.where(valid, sh_b, 0.0)
        a = a * jnp.where(valid, sh_a, 1.0)
        s *= 2
    return a, b


def _seg_cumsum(x, n, seg):
    pos = _rows(x.shape) % seg
    s = 1
    while s < seg:
        x = x + jnp.where(pos >= s, pltpu.roll(x, s, 0), 0.0)
        s *= 2
    return x


def _seg_rev_cumsum(x, n, seg):
    pos = _rows(x.shape) % seg
    s = 1
    while s < seg:
        x = x + jnp.where(pos < seg - s, pltpu.roll(x, n - s, 0), 0.0)
        s *= 2
    return x


def _head_mean(x):
    parts = []
    for h in range(HG_HEADS):
        m = jnp.mean(x[:, h * HG_DK:(h + 1) * HG_DK], axis=1, keepdims=True)
        parts.append(jnp.broadcast_to(m, (x.shape[0], HG_DK)))
    return jnp.concatenate(parts, axis=1)


def _mm(name, a_list, b_list, terms, n_acc, mode, m, n, k, tm, tn, tk, out_dtypes, epilogue, extras=(), vecs=(),
        n_part=0, chunk=0):
    tm, tn, tk = min(tm, m), min(tn, n), min(tk, k)
    assert m % tm == 0 and n % tn == 0 and k % tk == 0, (name, m, n, k, tm, tn, tk)
    gk = k // tk
    if mode == "tn":
        a_spec = pl.BlockSpec((tk, tm), lambda i, j, kk: (kk, i))
    else:
        a_spec = pl.BlockSpec((tm, tk), lambda i, j, kk: (i, kk))
    if mode == "nt":
        b_spec = pl.BlockSpec((tn, tk), lambda i, j, kk: (j, kk))
    else:
        b_spec = pl.BlockSpec((tk, tn), lambda i, j, kk: (kk, j))
    o_spec = pl.BlockSpec((tm, tn), lambda i, j, kk: (i, j))
    v_spec = pl.BlockSpec((1, tn), lambda i, j, kk: (0, j))
    p_spec = pl.BlockSpec((None, 1, tn), lambda i, j, kk: (i, 0, j))
    dot = {"nn": _dot, "nt": _dot_nt, "tn": _dot_tn}[mode]
    na, nb, ne, nv, no = len(a_list), len(b_list), len(extras), len(vecs), len(out_dtypes)
    carry = _CARRIED.pop(name, None)
    nci, nco = (len(carry.ins), len(carry.out_shapes)) if carry else (0, 0)
    n_in = na + nb + ne + nv + nci
    grid = (m // tm, n // tn, gk)

    def kern(*refs):
        if carry:
            ids = [pl.program_id(d) for d in range(3)]
            riders = (refs[n_in - nci:n_in], refs[n_in + no + n_part:n_in + no + n_part + nco]) + tuple(refs[-2:])
            carry.start(functools.reduce(jnp.logical_and, [p == 0 for p in ids]), *riders)
        compute(*refs)
        if carry:
            carry.finish(functools.reduce(jnp.logical_and, [p == g - 1 for p, g in zip(ids, grid)]), *riders)

    def compute(*refs):
        a_refs = refs[:na]
        b_refs = refs[na:na + nb]
        e_refs = refs[na + nb:na + nb + ne]
        v_refs = refs[na + nb + ne:na + nb + ne + nv]
        o_refs = refs[n_in:n_in + no + n_part]

        def finish(accs):
            outs = epilogue(accs, [e[...] for e in e_refs], [r[...] for r in v_refs])
            for o, val in zip(o_refs, outs):
                o[...] = val.astype(o.dtype)

        def partial_sums():
            sums = [None] * n_acc
            for ai, bi, ci in terms:
                d = dot(a_refs[ai][...].astype(bf16), b_refs[bi][...].astype(bf16))
                sums[ci] = d if sums[ci] is None else sums[ci] + d
            return sums

        if gk == 1 and chunk:
            assert mode in ("nn", "nt") and tn % chunk == 0
            for c0 in range(0, tn, chunk):
                cols = slice(c0, c0 + chunk)
                sums = [None] * n_acc
                for ai, bi, ci in terms:
                    b_part = b_refs[bi][:, cols] if mode == "nn" else b_refs[bi][cols, :]
                    d = dot(a_refs[ai][...].astype(bf16), b_part.astype(bf16))
                    sums[ci] = d if sums[ci] is None else sums[ci] + d
                outs = epilogue(sums, [e[:, cols] for e in e_refs], [r[:, cols] for r in v_refs])
                for o, val in zip(o_refs, outs):
                    o[:, cols] = val.astype(o.dtype)
            return
        if gk == 1:
            finish(partial_sums())
            return
        acc = refs[n_in + no + n_part + nco]
        kk = pl.program_id(2)

        @pl.when(kk == 0)
        def _():
            acc[...] = jnp.zeros_like(acc)

        for ci, d in enumerate(partial_sums()):
            acc[ci] += d

        @pl.when(kk == gk - 1)
        def _():
            finish([acc[c] for c in range(n_acc)])

    res = pl.pallas_call(
        kern, name=name,
        grid=grid,
        in_specs=[a_spec] * na + [b_spec] * nb + [o_spec] * ne + [v_spec] * nv + [ANY_SPEC] * nci,
        out_specs=[o_spec] * no + [p_spec] * n_part + [ANY_SPEC] * nco,
        out_shape=([jax.ShapeDtypeStruct((m, n), dt) for dt in out_dtypes]
                   + [jax.ShapeDtypeStruct((m // tm, 1, n), f32)] * n_part + (carry.out_shapes if carry else [])),
        scratch_shapes=([pltpu.VMEM((n_acc, tm, tn), f32)] if gk > 1 else []) + (carry.sems() if carry else []),
        compiler_params=_cparams(("arbitrary",) * 3 if carry else ("parallel", "parallel", "arbitrary")),
    )(*a_list, *b_list, *extras, *vecs, *(carry.ins if carry else []))
    if carry:
        carry.outs = res[no + n_part:]
        res = res[:no + n_part]
    return res


def _mm1(name, a, b, mode, m, n, k, tm, tn, tk, out_dtype=f32, scale=None):
    def epi(accs, extras, vecs):
        return [accs[0] if scale is None else accs[0] * scale]
    return _mm(name, [a], [b], [(0, 0, 0)], 1, mode, m, n, k, tm, tn, tk, [out_dtype], epi)[0]


def _rt(name, body, rows, tm, row_ins, consts, row_outs, acc_outs=(), scratch=(), reverse=False):
    tm = min(tm, rows)
    assert rows % tm == 0
    nt = rows // tm

    def tile(i):
        return nt - 1 - i if reverse else i

    in_specs, args = [], []
    for spec in row_ins:
        arr = spec[0]
        if isinstance(spec[1], int):
            in_specs.append(pl.BlockSpec((tm, spec[1]), lambda i, cb=spec[2]: (tile(i), cb)))
        else:
            in_specs.append(pl.BlockSpec(spec[1], lambda i, fn=spec[2]: fn(tile(i))))
        args.append(arr)
    for c in consts:
        in_specs.append(pl.BlockSpec(c.shape, lambda i, nd=c.ndim: (0,) * nd))
        args.append(c)
    out_specs, out_shape = [], []
    for spec in row_outs:
        if isinstance(spec[0], int):
            out_specs.append(pl.BlockSpec((tm, spec[0]), lambda i: (tile(i), 0)))
            out_shape.append(jax.ShapeDtypeStruct((rows, spec[0]), spec[1]))
        else:
            out_specs.append(pl.BlockSpec(spec[1], lambda i, fn=spec[2]: fn(tile(i))))
            out_shape.append(jax.ShapeDtypeStruct(spec[0], spec[3]))
    for shp in acc_outs:
        out_specs.append(pl.BlockSpec(shp, lambda i, nd=len(shp): (0,) * nd))
        out_shape.append(jax.ShapeDtypeStruct(shp, f32))
    n_in = len(args)
    n_row_out = len(row_outs)
    n_acc = len(acc_outs)
    n_out = n_row_out + n_acc
    carry = _CARRIED.pop(name, None)
    nci, nco = (len(carry.ins), len(carry.out_shapes)) if carry else (0, 0)

    def kern(*refs):
        i = pl.program_id(0)
        if carry:
            own = refs[:n_in] + refs[n_in + nci:n_in + nci + n_out] + refs[n_in + nci + n_out + nco:-2]
            riders = (refs[n_in:n_in + nci], refs[n_in + nci + n_out:n_in + nci + n_out + nco]) + tuple(refs[-2:])
            carry.start(i == 0, *riders)
        else:
            own = refs
        acc_refs = own[n_in + n_row_out:n_in + n_out]

        @pl.when(i == 0)
        def _():
            for r in acc_refs:
                r[...] = jnp.zeros_like(r)

        body(i, *own)
        if carry:
            carry.finish(i == nt - 1, *riders)

    res = pl.pallas_call(
        kern, name=name, grid=(nt,), in_specs=in_specs + [ANY_SPEC] * nci, out_specs=out_specs + [ANY_SPEC] * nco,
        out_shape=out_shape + (carry.out_shapes if carry else []),
        scratch_shapes=list(scratch) + (carry.sems() if carry else []), compiler_params=_cparams(("arbitrary",)),
    )(*args, *(carry.ins if carry else []))
    if carry:
        carry.outs = res[n_out:]
        res = res[:n_out]
    return res


def _rms_rows(xv, wv):
    r = lax.rsqrt(jnp.mean(xv * xv, axis=1, keepdims=True) + EPS)
    return (xv * r * wv).astype(bf16)


def _rms_bwd_rows(xv, dhv, wv, dres):
    r = lax.rsqrt(jnp.mean(xv * xv, axis=1, keepdims=True) + EPS)
    xn = xv * r
    dxn = dhv * wv
    dx = dres + r * (dxn - xn * jnp.mean(dxn * xn, axis=1, keepdims=True))
    return [dx, dx.astype(bf16), jnp.sum(dhv * xn, axis=0, keepdims=True)]


def _rms_fwd(name, x, w, rows):
    def body(i, x_ref, w_ref, h_ref):
        h_ref[...] = _rms_rows(x_ref[...], w_ref[...])
    return _rt(name, body, rows, ROW_TILE, [(x, D_MODEL, 0)], [w], [(D_MODEL, bf16)])[0]


def _residual_then_norm(scale):
    def epi(accs, extras, vecs):
        x_out = extras[0] + scale * accs[0]
        return [x_out] + [_rms_rows(x_out, v) for v in vecs]
    return epi


def _norm_bwd_epilogue(accs, extras, vecs):
    return _rms_bwd_rows(extras[0], accs[0], vecs[0], extras[1])


def _loss_head(x, w, target, rows):
    def body(i, x_ref, t_ref, w_ref, dx_ref, dxb_ref, loss_ref, dw_ref):
        xv = x_ref[...]
        r = lax.rsqrt(jnp.mean(xv * xv, axis=1, keepdims=True) + EPS)
        xn = xv * r
        wv = w_ref[...]
        err = xn * wv - t_ref[...]
        part = 0.5 * jnp.sum(jnp.mean(err * err, axis=1, keepdims=True), axis=0, keepdims=True)
        loss_ref[...] += jnp.broadcast_to(part, (1, 128))
        dy = err * (1.0 / D_MODEL)
        dxn = dy * wv
        dx = r * (dxn - xn * jnp.mean(dxn * xn, axis=1, keepdims=True))
        dx_ref[...] = dx
        dxb_ref[...] = dx.astype(bf16)
        dw_ref[...] += jnp.sum(dy * xn, axis=0, keepdims=True)
    return _rt("loss_head", body, rows, ROW_TILE, [(x, D_MODEL, 0), (target, D_MODEL, 0)], [w],
               [(D_MODEL, f32), (D_MODEL, bf16)], acc_outs=[(1, 128), (1, D_MODEL)])


def _ffn_fwd(tag, x, hb, wg_t, wu_t, wd, next_nw, rows):
    def epi_up(accs, extras, vecs):
        a, b = accs
        return [a, b, a * _sigmoid(a) * b]
    a, b, s = _mm("ffn_up_" + tag, [hb], [wg_t, wu_t], [(0, 0, 0), (0, 1, 1)], 2, "nt", rows, D_FF, D_MODEL,
                  512, D_FF, D_MODEL, [bf16, bf16, bf16], epi_up, chunk=MXU_COLS)
    outs = _mm("ffn_down_" + tag, [s], [wd], [(0, 0, 0)], 1, "nn", rows, D_MODEL, D_FF,
               512, D_MODEL, D_FF, [f32] + [bf16] * len(next_nw), _residual_then_norm(0.5), extras=[x],
               vecs=next_nw)
    return outs[0], (outs[1] if next_nw else None), (x, hb, a, b, s)


def _ffn_bwd(tag, saved, nw, wg_t, wu_t, wd, dx, dxb, rows):
    x, hb, a, b, s = saved

    def epi_mid(accs, extras, vecs):
        ds = 0.5 * accs[0]
        av = extras[0].astype(f32)
        bv = extras[1].astype(f32)
        sg = _sigmoid(av)
        return [ds * bv * sg * (1.0 + av * (1.0 - sg)), ds * av * sg]
    da, db = _mm("ffn_bwd_mid_" + tag, [dxb], [wd], [(0, 0, 0)], 1, "nt", rows, D_FF, D_MODEL,
                 512, D_FF, D_MODEL, [bf16, bf16], epi_mid, extras=[a, b], chunk=MXU_COLS)
    d_wd = _mm1("ffn_dwd_" + tag, s, dxb, "tn", D_FF, D_MODEL, rows, FF_TILE, D_MODEL, TOKEN_K, out_dtype=bf16,
                scale=0.5)
    d_wg_t = _mm1("ffn_dwg_" + tag, da, hb, "tn", D_FF, D_MODEL, rows, FF_TILE, D_MODEL, TOKEN_K, out_dtype=bf16)
    d_wu_t = _mm1("ffn_dwu_" + tag, db, hb, "tn", D_FF, D_MODEL, rows, FF_TILE, D_MODEL, TOKEN_K, out_dtype=bf16)
    dx_in, dxb_in, d_nw = _mm("ffn_dh_" + tag, [da, db], [wg_t, wu_t], [(0, 0, 0), (1, 1, 0)], 1, "nn", rows,
                              D_MODEL, D_FF, 512, D_MODEL, D_FF, [f32, bf16], _norm_bwd_epilogue, extras=[x, dx],
                              vecs=[nw], n_part=1)
    return dx_in, dxb_in, jnp.sum(d_nw, axis=0), d_wg_t, d_wu_t, d_wd


S5_CB = 512
SUBLANES = 8
U_COL = GM_WIDTH // BRANCH


def _s5_scan_fwd(tag, proj, b_re, b_im, a_re, a_im, rows):
    tm = min(ROW_TILE, rows)
    nt = rows // tm
    nc = S5_LANES // S5_CB

    def kern(u_ref, bre_ref, bim_ref, ar_ref, ai_ref, xr_ref, xi_ref, pr_s, pi_s, cr_s, ci_s, mr_s, mi_s):
        t = pl.program_id(1)

        @pl.when(t == 0)
        def _():
            row8 = _rows((SUBLANES, S5_CB))
            pr = jnp.broadcast_to(ar_ref[...], (SUBLANES, S5_CB))
            pi = jnp.broadcast_to(ai_ref[...], (SUBLANES, S5_CB))
            s = 1
            while s < SUBLANES:
                sr = pltpu.roll(pr, s, 0)
                si = pltpu.roll(pi, s, 0)
                valid = row8 >= s
                pr, pi = jnp.where(valid, pr * sr - pi * si, pr), jnp.where(valid, pr * si + pi * sr, pi)
                s *= 2
            pr_s[...] = pr
            pi_s[...] = pi
            for k in range(3):
                s = 1 << k
                mr_s[k] = jnp.where(row8 >= s, pr[s - 1:s, :], 0.0)
                mi_s[k] = jnp.where(row8 >= s, pi[s - 1:s, :], 0.0)
            cr_s[...] = jnp.zeros_like(cr_s)
            ci_s[...] = jnp.zeros_like(ci_s)

        ub = u_ref[...].astype(bf16)
        br = _dot(ub, bre_ref[...])
        bi = _dot(ub, bim_ref[...])
        steps = [(mr_s[k], mi_s[k]) for k in range(3)]
        cr = cr_s[...]
        ci = ci_s[...]
        pr = pr_s[...]
        pi = pi_s[...]
        for g in range(tm // SUBLANES):
            sl = slice(g * SUBLANES, (g + 1) * SUBLANES)
            xr = br[sl]
            xi = bi[sl]
            for k, (mr, mi) in enumerate(steps):
                sr = pltpu.roll(xr, 1 << k, 0)
                si = pltpu.roll(xi, 1 << k, 0)
                xr, xi = xr + (mr * sr - mi * si), xi + (mr * si + mi * sr)
            xr, xi = xr + (pr * cr - pi * ci), xi + (pr * ci + pi * cr)
            xr_ref[sl, :] = xr
            xi_ref[sl, :] = xi
            cr = xr[SUBLANES - 1:SUBLANES, :]
            ci = xi[SUBLANES - 1:SUBLANES, :]
        cr_s[...] = cr
        ci_s[...] = ci

    return _call_with_rider(
        "s5_scan_fwd_" + tag, kern, (nc, nt),
        [pl.BlockSpec((tm, BRANCH), lambda c, t: (t, U_COL)),
         pl.BlockSpec((BRANCH, S5_CB), lambda c, t: (0, c)),
         pl.BlockSpec((BRANCH, S5_CB), lambda c, t: (0, c)),
         pl.BlockSpec((1, S5_CB), lambda c, t: (0, c)),
         pl.BlockSpec((1, S5_CB), lambda c, t: (0, c))],
        [pl.BlockSpec((tm, S5_CB), lambda c, t: (t, c))] * 2,
        [jax.ShapeDtypeStruct((rows, S5_LANES), f32)] * 2,
        [pltpu.VMEM((SUBLANES, S5_CB), f32), pltpu.VMEM((SUBLANES, S5_CB), f32),
         pltpu.VMEM((1, S5_CB), f32), pltpu.VMEM((1, S5_CB), f32),
         pltpu.VMEM((3, SUBLANES, S5_CB), f32), pltpu.VMEM((3, SUBLANES, S5_CB), f32)],
        ("parallel", "arbitrary"), (proj, b_re, b_im, a_re, a_im))


def _s5_scan_bwd(tag, dxr, dxi, xr, xi, a_re, a_im, rows):
    tm = min(ROW_TILE, rows)
    nt = rows // tm
    nc = S5_LANES // S5_CB

    def kern(dxr_ref, dxi_ref, xr_ref, xi_ref, ar_ref, ai_ref, gr_ref, gi_ref, dar_ref, dai_ref,
             qr_s, qi_s, cr_s, ci_s, gr_s, gi_s, mr_s, mi_s):
        t = pl.program_id(1)
        row = _rows((tm, S5_CB))
        ng = tm // SUBLANES

        @pl.when(t == 0)
        def _():
            row8 = _rows((SUBLANES, S5_CB))
            qr = jnp.broadcast_to(ar_ref[...], (SUBLANES, S5_CB))
            qi = jnp.broadcast_to(-ai_ref[...], (SUBLANES, S5_CB))
            s = 1
            while s < SUBLANES:
                sr = pltpu.roll(qr, SUBLANES - s, 0)
                si = pltpu.roll(qi, SUBLANES - s, 0)
                valid = row8 < SUBLANES - s
                qr, qi = jnp.where(valid, qr * sr - qi * si, qr), jnp.where(valid, qr * si + qi * sr, qi)
                s *= 2
            qr_s[...] = qr
            qi_s[...] = qi
            for k in range(3):
                s = 1 << k
                mr_s[k] = jnp.where(row8 < SUBLANES - s, qr[SUBLANES - s:SUBLANES - s + 1, :], 0.0)
                mi_s[k] = jnp.where(row8 < SUBLANES - s, qi[SUBLANES - s:SUBLANES - s + 1, :], 0.0)
            cr_s[...] = jnp.zeros_like(cr_s)
            ci_s[...] = jnp.zeros_like(ci_s)
            dar_ref[...] = jnp.zeros_like(dar_ref)
            dai_ref[...] = jnp.zeros_like(dai_ref)

        steps = [(mr_s[k], mi_s[k]) for k in range(3)]
        cr = cr_s[...]
        ci = ci_s[...]
        qr = qr_s[...]
        qi = qi_s[...]
        last8 = _rows((SUBLANES, S5_CB)) == SUBLANES - 1
        acc_r = jnp.zeros((SUBLANES, S5_CB), f32)
        acc_i = jnp.zeros((SUBLANES, S5_CB), f32)
        for g in reversed(range(ng)):
            sl = slice(g * SUBLANES, (g + 1) * SUBLANES)
            gr = dxr_ref[sl, :]
            gi = dxi_ref[sl, :]
            for k, (mr, mi) in enumerate(steps):
                sr = pltpu.roll(gr, SUBLANES - (1 << k), 0)
                si = pltpu.roll(gi, SUBLANES - (1 << k), 0)
                gr, gi = gr + (mr * sr - mi * si), gi + (mr * si + mi * sr)
            gr, gi = gr + (qr * cr - qi * ci), gi + (qr * ci + qi * cr)
            gr_s[sl, :] = gr
            gi_s[sl, :] = gi
            gnr = jnp.where(last8, cr, pltpu.roll(gr, SUBLANES - 1, 0))
            gni = jnp.where(last8, ci, pltpu.roll(gi, SUBLANES - 1, 0))
            xr_v = xr_ref[sl, :]
            xi_v = xi_ref[sl, :]
            acc_r = acc_r + (gnr * xr_v + gni * xi_v)
            acc_i = acc_i + (gni * xr_v - gnr * xi_v)
            cr = gr[0:1, :]
            ci = gi[0:1, :]
        cr_s[...] = cr
        ci_s[...] = ci
        gr_ref[...] = gr_s[...].astype(bf16)
        gi_ref[...] = gi_s[...].astype(bf16)
        dar_ref[...] += jnp.sum(acc_r, axis=0, keepdims=True)
        dai_ref[...] += jnp.sum(acc_i, axis=0, keepdims=True)

    rev = lambda c, t: (nt - 1 - t, c)
    return _call_with_rider(
        "s5_scan_bwd_" + tag, kern, (nc, nt),
        [pl.BlockSpec((tm, S5_CB), rev)] * 4 + [pl.BlockSpec((1, S5_CB), lambda c, t: (0, c))] * 2,
        [pl.BlockSpec((tm, S5_CB), rev)] * 2 + [pl.BlockSpec((1, S5_CB), lambda c, t: (0, c))] * 2,
        [jax.ShapeDtypeStruct((rows, S5_LANES), bf16)] * 2 + [jax.ShapeDtypeStruct((1, S5_LANES), f32)] * 2,
        [pltpu.VMEM((SUBLANES, S5_CB), f32), pltpu.VMEM((SUBLANES, S5_CB), f32),
         pltpu.VMEM((1, S5_CB), f32), pltpu.VMEM((1, S5_CB), f32),
         pltpu.VMEM((tm, S5_CB), f32), pltpu.VMEM((tm, S5_CB), f32),
         pltpu.VMEM((3, SUBLANES, S5_CB), f32), pltpu.VMEM((3, SUBLANES, S5_CB), f32)],
        ("parallel", "arbitrary"), (dxr, dxi, xr, xi, a_re, a_im))


def _s5_fwd(tag, proj, cst, rows):
    xr, xi = _s5_scan_fwd(tag, proj, cst["b_re"].astype(bf16), cst["b_im"].astype(bf16), cst["a_re"], cst["a_im"], rows)

    def body(i, xr_ref, xi_ref, u_ref, cre_ref, cim_ref, d_ref, gw_ref, gb_ref, y_ref, out_ref):
        y = (_dot(xr_ref[...].astype(bf16), cre_ref[...]) + _dot(xi_ref[...].astype(bf16), cim_ref[...])
             + d_ref[...] * u_ref[...])
        y_ref[...] = y
        z = _gelu(y)
        zg = _dot(z.astype(bf16), gw_ref[...]) + gb_ref[...]
        out_ref[...] = (z * _sigmoid(zg)).astype(bf16)

    y, out = _rt("s5_out_" + tag, body, rows, ROW_TILE,
                 [(xr, S5_LANES, 0), (xi, S5_LANES, 0), (proj, BRANCH, U_COL)],
                 [cst["c_re"].astype(bf16), cst["c_im"].astype(bf16), cst["s5_d"], cst["glu_w"], cst["glu_b"]],
                 [(BRANCH, f32), (BRANCH, bf16)])
    return out, (xr, xi, y)


def _s5_bwd(tag, saved, proj, cst, d_out, rows):
    xr, xi, y = saved
    c_re = cst["c_re"].astype(bf16)
    c_im = cst["c_im"].astype(bf16)

    def body(i, do_ref, y_ref, u_ref, xr_ref, xi_ref, cre_ref, cim_ref, gw_ref, gb_ref,
             dxr_ref, dxi_ref, dy_ref, dgw_ref, dgb_ref, dd_ref, dcre_ref, dcim_ref):
        yv = y_ref[...]
        z = _gelu(yv)
        zb = z.astype(bf16)
        gt = _sigmoid(_dot(zb, gw_ref[...]) + gb_ref[...])
        dov = do_ref[...]
        dzg = dov * z * gt * (1.0 - gt)
        dzgb = dzg.astype(bf16)
        dz = dov * gt + _dot_nt(dzgb, gw_ref[...])
        dgw_ref[...] += _dot_tn(zb, dzgb)
        dgb_ref[...] += jnp.sum(dzg, axis=0, keepdims=True)
        dy = dz * _gelu_grad(yv)
        dy_ref[...] = dy
        dd_ref[...] += jnp.sum(dy * u_ref[...], axis=0, keepdims=True)
        dyb = dy.astype(bf16)
        dxr_ref[...] = _dot_nt(dyb, cre_ref[...])
        dxi_ref[...] = _dot_nt(dyb, cim_ref[...])
        dcre_ref[...] += _dot_tn(xr_ref[...].astype(bf16), dyb)
        dcim_ref[...] += _dot_tn(xi_ref[...].astype(bf16), dyb)

    dxr, dxi, dy, d_gw, d_gb, d_d, d_cre, d_cim = _rt(
        "s5_out_bwd_" + tag, body, rows, ROW_TILE,
        [(d_out, BRANCH, 0), (y, BRANCH, 0), (proj, BRANCH, U_COL), (xr, S5_LANES, 0), (xi, S5_LANES, 0)],
        [c_re, c_im, cst["glu_w"], cst["glu_b"]],
        [(S5_LANES, f32), (S5_LANES, f32), (BRANCH, f32)],
        acc_outs=[(BRANCH, BRANCH), (1, BRANCH), (1, BRANCH), (S5_LANES, BRANCH), (S5_LANES, BRANCH)])

    gr, gi, d_ar, d_ai = _s5_scan_bwd(tag, dxr, dxi, xr, xi, cst["a_re"], cst["a_im"], rows)
    b_re = cst["b_re"].astype(bf16)
    b_im = cst["b_im"].astype(bf16)

    def body_in(i, gr_ref, gi_ref, dy_ref, u_ref, bre_ref, bim_ref, d_ref, du_ref, dbre_ref, dbim_ref):
        grv = gr_ref[...]
        giv = gi_ref[...]
        du = _dot_nt(grv, bre_ref[...]) + _dot_nt(giv, bim_ref[...]) + dy_ref[...] * d_ref[...]
        du_ref[...] = du.astype(bf16)
        ub = u_ref[...].astype(bf16)
        dbre_ref[...] += _dot_tn(ub, grv)
        dbim_ref[...] += _dot_tn(ub, giv)

    du, d_bre, d_bim = _rt("s5_in_bwd_" + tag, body_in, rows, ROW_TILE,
                           [(gr, S5_LANES, 0), (gi, S5_LANES, 0), (dy, BRANCH, 0), (proj, BRANCH, U_COL)],
                           [b_re, b_im, cst["s5_d"]], [(BRANCH, bf16)],
                           acc_outs=[(BRANCH, S5_LANES), (BRANCH, S5_LANES)])
    dcst = {"b_re": d_bre, "b_im": d_bim, "a_re": d_ar, "a_im": d_ai, "c_re": d_cre, "c_im": d_cim,
            "s5_d": d_d, "glu_b": d_gb}
    return du, dcst, d_gw


def _hg_prep(q, z, lb):
    qs = _sigmoid(q)
    qh = q * qs
    sg = _sigmoid_small(z)
    fg = lb + (1.0 - lb) * sg
    kk = (1.0 - lb) * (1.0 - sg)
    return qs, qh, sg, fg, kk


def _hg_fwd(tag, proj, cst, rows):
    tm = min(ROW_TILE, rows)
    c_sz = HG_CHUNK
    nch = tm // c_sz
    n_chunks = rows // c_sz

    def body(i, q_ref, z_ref, v_ref, g_ref, lb_ref, nw_ref, out_ref, o_ref, ss_ref, sn_ref, st_s):
        @pl.when(i == 0)
        def _():
            st_s[...] = jnp.zeros_like(st_s)

        lb = lb_ref[...]
        _, qh, sg, fg, kk = _hg_prep(q_ref[...], z_ref[...], lb)
        b = _seg_cumsum(jnp.log(fg), tm, c_sz)
        qhat = (qh * jnp.exp(b)).astype(bf16)
        khat = (kk * jnp.exp(-b)).astype(bf16)
        vb = v_ref[...].astype(bf16)
        b3 = b.reshape(nch, c_sz, BRANCH)
        bl3 = b3[:, c_sz - 1:c_sz, :]
        kdec = (kk.reshape(nch, c_sz, BRANCH) * jnp.exp(bl3 - b3)).astype(bf16)
        ebl = jnp.exp(bl3)
        tril = (lax.broadcasted_iota(jnp.int32, (nch, c_sz, c_sz), 1)
                >= lax.broadcasted_iota(jnp.int32, (nch, c_sz, c_sz), 2))
        o_heads = []
        for h in range(HG_HEADS):
            hl = slice(h * HG_DK, (h + 1) * HG_DK)
            q3 = qhat[:, hl].reshape(nch, c_sz, HG_DK)
            k3 = khat[:, hl].reshape(nch, c_sz, HG_DK)
            v3 = vb[:, hl].reshape(nch, c_sz, HG_DK)
            a_mat = jnp.where(tril, _bdot_nt(q3, k3), 0.0).astype(bf16)
            o3 = _bdot(a_mat, v3)
            st = st_s[hl, :]
            before = []
            for ci in range(nch):
                before.append(st.astype(bf16))
                st = st * ebl[ci][:, hl] + _dot_tn(v3[ci], kdec[ci][:, hl])
                sn_ref[ci, hl, :] = st.astype(bf16)
            st_s[hl, :] = st
            s3 = jnp.stack(before)
            ss_ref[:, hl, :] = s3
            o3 = o3 + _bdot_nt(q3, s3)
            o_heads.append(o3.reshape(tm, HG_DK))
        o = jnp.concatenate(o_heads, axis=1)
        o_ref[...] = o
        r = lax.rsqrt(_head_mean(o * o) + EPS)
        g = g_ref[...]
        out_ref[...] = (o * r * nw_ref[...] * (g * _sigmoid(g))).astype(bf16)

    out, o, ss, sn = _rt(
        "hg_fwd_" + tag, body, rows, tm,
        [(proj, BRANCH, U_COL + 1), (proj, BRANCH, U_COL + 2), (proj, BRANCH, U_COL + 3), (proj, BRANCH, U_COL + 4)],
        [cst["hg_lb"], cst["hg_nw"]],
        [(BRANCH, bf16), (BRANCH, f32),
         ((n_chunks, BRANCH, HG_DK), (nch, BRANCH, HG_DK), lambda t: (t, 0, 0), bf16),
         ((n_chunks, BRANCH, HG_DK), (nch, BRANCH, HG_DK), lambda t: (t, 0, 0), bf16)],
        scratch=[pltpu.VMEM((BRANCH, HG_DK), f32)])
    return out, (o, ss, sn)


def _hg_bwd(tag, saved, proj, cst, d_out, rows):
    o_saved, ss, sn = saved
    tm = min(ROW_TILE, rows)
    c_sz = HG_CHUNK
    nch = tm // c_sz

    def body(i, do_ref, q_ref, z_ref, v_ref, g_ref, o_ref, ss_ref, sn_ref, lb_ref, nw_ref,
             dq_ref, dz_ref, dv_ref, dg_ref, dlb_ref, dnw_ref, dst_s):
        @pl.when(i == 0)
        def _():
            dst_s[...] = jnp.zeros_like(dst_s)

        lb = lb_ref[...]
        q = q_ref[...]
        qs, qh, sg, fg, kk = _hg_prep(q, z_ref[...], lb)
        b = _seg_cumsum(jnp.log(fg), tm, c_sz)
        eb = jnp.exp(b)
        enb = jnp.exp(-b)
        qhat = (qh * eb).astype(bf16)
        khat = (kk * enb).astype(bf16)
        vb = v_ref[...].astype(bf16)
        b3 = b.reshape(nch, c_sz, BRANCH)
        bl3 = b3[:, c_sz - 1:c_sz, :]
        dec3 = jnp.exp(bl3 - b3)
        kdec = (kk.reshape(nch, c_sz, BRANCH) * dec3).astype(bf16)
        ebl = jnp.exp(bl3)
        g = g_ref[...]
        gs = _sigmoid(g)
        o = o_ref[...]
        r = lax.rsqrt(_head_mean(o * o) + EPS)
        oh = o * r
        nw = nw_ref[...]
        dov = do_ref[...]
        don = dov * (g * gs)
        dg_ref[...] = (dov * oh * nw * (gs * (1.0 + g * (1.0 - gs)))).astype(bf16)
        dnw_ref[...] += jnp.sum(don * oh, axis=0, keepdims=True)
        doh = don * nw
        d_o = r * (doh - oh * _head_mean(doh * oh))
        dob = d_o.astype(bf16)
        t_idx = lax.broadcasted_iota(jnp.int32, (nch, c_sz, c_sz), 1)
        s_idx = lax.broadcasted_iota(jnp.int32, (nch, c_sz, c_sz), 2)
        heads = []
        for h in range(HG_HEADS):
            hl = slice(h * HG_DK, (h + 1) * HG_DK)
            q3 = qhat[:, hl].reshape(nch, c_sz, HG_DK)
            k3 = khat[:, hl].reshape(nch, c_sz, HG_DK)
            v3 = vb[:, hl].reshape(nch, c_sz, HG_DK)
            do3 = dob[:, hl].reshape(nch, c_sz, HG_DK)
            s3 = ss_ref[:, hl, :]
            da_mat = jnp.where(t_idx >= s_idx, _bdot_nt(do3, v3), 0.0).astype(bf16)
            a_t = jnp.where(t_idx <= s_idx, _bdot_nt(k3, q3), 0.0).astype(bf16)
            da_t = jnp.where(t_idx <= s_idx, _bdot_nt(v3, do3), 0.0).astype(bf16)
            dqhat = _bdot(do3, s3) + _bdot(da_mat, k3)
            dkhat = _bdot(da_t, q3)
            dst = dst_s[hl, :]
            after = [None] * nch
            for ci in reversed(range(nch)):
                after[ci] = dst
                dst = dst * ebl[ci][:, hl] + _dot_tn(do3[ci], q3[ci])
            dst_s[hl, :] = dst
            ds3 = jnp.stack(after)
            ds3b = ds3.astype(bf16)
            dk_inter = _bdot(v3, ds3b) * dec3[:, :, hl]
            dv3 = _bdot(a_t, do3) + _bdot_nt(kdec[:, :, hl], ds3b)
            flux = jnp.sum(sn_ref[:, hl, :].astype(f32) * ds3, axis=1, keepdims=True)
            heads.append((dqhat.reshape(tm, HG_DK), dkhat.reshape(tm, HG_DK), dk_inter.reshape(tm, HG_DK),
                          dv3.reshape(tm, HG_DK), jnp.broadcast_to(flux, (nch, c_sz, HG_DK)).reshape(tm, HG_DK)))
        dqhat, dkhat, dk_inter, dv, flux = (jnp.concatenate(parts, axis=1) for parts in zip(*heads))
        dv_ref[...] = dv.astype(bf16)
        dqh = dqhat * eb
        dk = dkhat * enb + dk_inter
        db = qhat.astype(f32) * dqhat - khat.astype(f32) * dkhat - kk * dk_inter
        dlf = _seg_rev_cumsum(db, tm, c_sz) + flux
        tt = (1.0 - lb) * sg * (1.0 - sg)
        dz_ref[...] = (dlf * tt / fg - dk * tt).astype(bf16)
        dlb_ref[...] += jnp.sum(dlf * (1.0 - sg) / fg - dk * (1.0 - sg), axis=0, keepdims=True)
        dq_ref[...] = (dqh * (qs * (1.0 + q * (1.0 - qs)))).astype(bf16)

    dq, dz, dv, dg, d_lb, d_nw = _rt(
        "hg_bwd_" + tag, body, rows, tm,
        [(d_out, BRANCH, 0), (proj, BRANCH, U_COL + 1), (proj, BRANCH, U_COL + 2), (proj, BRANCH, U_COL + 3),
         (proj, BRANCH, U_COL + 4), (o_saved, BRANCH, 0), (ss, (nch, BRANCH, HG_DK), lambda t: (t, 0, 0)),
         (sn, (nch, BRANCH, HG_DK), lambda t: (t, 0, 0))],
        [cst["hg_lb"], cst["hg_nw"]],
        [(BRANCH, bf16)] * 4, acc_outs=[(1, BRANCH), (1, BRANCH)],
        scratch=[pltpu.VMEM((BRANCH, HG_DK), f32)],
        reverse=True)
    return dq, dz, dv, dg, {"hg_lb": d_lb, "hg_nw": d_nw}


def _rg_gates(xc, wa_ref, ba_ref, wx_ref, bx_ref, sp8):
    xcb = xc.astype(bf16)
    r = _sigmoid(_dot(xcb, wa_ref[...]) + ba_ref[...])
    ig = _sigmoid(_dot(xcb, wx_ref[...]) + bx_ref[...])
    la = -sp8 * r
    a = jnp.exp(la)
    mult = jnp.sqrt(-_expm1(2.0 * la))
    return xcb, r, ig, a, mult


def _rg_fwd(tag, proj, cst, rows):
    tm = min(ROW_TILE, rows)

    def body(i, xb_ref, gate_ref, cw_ref, cb_ref, wa_ref, ba_ref, wx_ref, bx_ref, sp_ref,
             out_ref, xc_ref, h_ref, hp_ref, prev_s, hc_s):
        @pl.when(i == 0)
        def _():
            prev_s[...] = jnp.zeros_like(prev_s)
            hc_s[...] = jnp.zeros_like(hc_s)

        row = _rows((tm, BRANCH))
        xb = xb_ref[...]
        prev = prev_s[...]
        xc = cb_ref[...] + cw_ref[3:4, :] * xb
        for j in range(1, 4):
            sh = jnp.where(row >= j, pltpu.roll(xb, j, 0), pltpu.roll(prev, j, 0))
            xc = xc + cw_ref[3 - j:4 - j, :] * sh
        prev_s[...] = xb
        xc_ref[...] = xc
        _, r, ig, a, mult = _rg_gates(xc, wa_ref, ba_ref, wx_ref, bx_ref, sp_ref[...])
        bb = mult * ig * xc
        hc = hc_s[...]
        row8 = _rows((SUBLANES, BRANCH))
        for g in range(tm // SUBLANES):
            sl = slice(g * SUBLANES, (g + 1) * SUBLANES)
            a_cum, h_loc = _scan_fwd(a[sl], bb[sl], SUBLANES)
            h = h_loc + a_cum * hc
            h_ref[sl, :] = h
            hp_ref[sl, :] = jnp.where(row8 >= 1, pltpu.roll(h, 1, 0), hc)
            hc = h[SUBLANES - 1:SUBLANES, :]
        hc_s[...] = hc
        out_ref[...] = (h_ref[...] * _gelu(gate_ref[...])).astype(bf16)

    out, xc, h, hp = _rt(
        "rg_fwd_" + tag, body, rows, tm,
        [(proj, BRANCH, U_COL + 5), (proj, BRANCH, U_COL + 6)],
        [cst["rg_cw"], cst["rg_cb"], cst["rg_wa"].astype(bf16), cst["rg_ba"], cst["rg_wx"].astype(bf16),
         cst["rg_bx"], cst["rg_sp8"]],
        [(BRANCH, bf16), (BRANCH, f32), (BRANCH, f32), (BRANCH, f32)],
        scratch=[pltpu.VMEM((tm, BRANCH), f32), pltpu.VMEM((1, BRANCH), f32)])
    return out, (xc, h, hp)


def _rg_bwd(tag, saved, proj, cst, d_out, rows):
    xc_saved, h_saved, hp_saved = saved
    tm = min(ROW_TILE, rows)

    def body(i, do_ref, xb_ref, gate_ref, xc_ref, h_ref, hp_ref, cw_ref, wa_ref, ba_ref, wx_ref, bx_ref, sp_ref,
             dxb_ref, dgate_ref, dcw_ref, dcb_ref, dwa_ref, dba_ref, dwx_ref, dbx_ref, dsp_ref,
             nxt_s, ec_s, gt_s):
        @pl.when(i == 0)
        def _():
            nxt_s[...] = jnp.zeros_like(nxt_s)
            ec_s[...] = jnp.zeros_like(ec_s)

        row = _rows((tm, BRANCH))
        xc = xc_ref[...]
        sp8 = sp_ref[...]
        xcb, r, ig, a, mult = _rg_gates(xc, wa_ref, ba_ref, wx_ref, bx_ref, sp8)
        gate = gate_ref[...]
        dov = do_ref[...]
        dh = dov * _gelu(gate)
        dgate_ref[...] = (dov * h_ref[...] * _gelu_grad(gate)).astype(bf16)
        adh = a * dh
        ec = ec_s[...]
        last8 = _rows((SUBLANES, BRANCH)) == SUBLANES - 1
        for g in reversed(range(tm // SUBLANES)):
            sl = slice(g * SUBLANES, (g + 1) * SUBLANES)
            a_cum, e_loc = _scan_bwd(a[sl], adh[sl], SUBLANES)
            e = e_loc + a_cum * ec
            gt_s[sl, :] = dh[sl] + jnp.where(last8, ec, pltpu.roll(e, SUBLANES - 1, 0))
            ec = e[0:1, :]
        ec_s[...] = ec
        g_tot = gt_s[...]
        d_a = g_tot * hp_ref[...]
        d_mult = g_tot * ig * xc
        d_ix = g_tot * mult
        d_ig = d_ix * xc
        d_xc = d_ix * ig
        d_la = d_a * a - d_mult * (a * a) / mult
        d_r = -d_la * sp8
        dsp_ref[...] += jnp.sum(-d_la * r, axis=0, keepdims=True)
        dzr = d_r * r * (1.0 - r)
        dzi = d_ig * ig * (1.0 - ig)
        dzrb = dzr.astype(bf16)
        dzib = dzi.astype(bf16)
        d_xc = d_xc + _dot_nt(dzrb, wa_ref[...]) + _dot_nt(dzib, wx_ref[...])
        dwa_ref[...] += _dot_tn(xcb, dzrb)
        dwx_ref[...] += _dot_tn(xcb, dzib)
        dba_ref[...] += jnp.sum(dzr, axis=0, keepdims=True)
        dbx_ref[...] += jnp.sum(dzi, axis=0, keepdims=True)
        dcb_ref[...] += jnp.sum(d_xc, axis=0, keepdims=True)
        nxt = nxt_s[...]
        xb = xb_ref[...]
        dxb = cw_ref[3:4, :] * d_xc
        dcw_ref[3:4, :] += jnp.sum(d_xc * xb, axis=0, keepdims=True)
        for j in range(1, 4):
            sh = jnp.where(row < tm - j, pltpu.roll(d_xc, tm - j, 0), pltpu.roll(nxt, tm - j, 0))
            dxb = dxb + cw_ref[3 - j:4 - j, :] * sh
            dcw_ref[3 - j:4 - j, :] += jnp.sum(sh * xb, axis=0, keepdims=True)
        nxt_s[...] = d_xc
        dxb_ref[...] = dxb.astype(bf16)

    wa = cst["rg_wa"].astype(bf16)
    wx = cst["rg_wx"].astype(bf16)
    dxb, dgate, d_cw, d_cb, d_wa, d_ba, d_wx, d_bx, d_sp = _rt(
        "rg_bwd_" + tag, body, rows, tm,
        [(d_out, BRANCH, 0), (proj, BRANCH, U_COL + 5), (proj, BRANCH, U_COL + 6), (xc_saved, BRANCH, 0),
         (h_saved, BRANCH, 0), (hp_saved, BRANCH, 0)],
        [cst["rg_cw"], wa, cst["rg_ba"], wx, cst["rg_bx"], cst["rg_sp8"]],
        [(BRANCH, bf16), (BRANCH, bf16)],
        acc_outs=[(4, BRANCH), (1, BRANCH), (BRANCH, BRANCH), (1, BRANCH), (BRANCH, BRANCH), (1, BRANCH), (1, BRANCH)],
        scratch=[pltpu.VMEM((tm, BRANCH), f32), pltpu.VMEM((1, BRANCH), f32), pltpu.VMEM((tm, BRANCH), f32)],
        reverse=True)
    dcst = {"rg_cw": d_cw, "rg_cb": d_cb, "rg_wa": d_wa, "rg_ba": d_ba, "rg_wx": d_wx, "rg_bx": d_bx, "rg_sp8": d_sp}
    return dxb, dgate, dcst


def _merge_fwd(tag, proj, outs, bp, rows):
    def body(i, ya_ref, yb_ref, yc_ref, gm_ref, p_ref, m_ref):
        acc = None
        for n, y_ref in enumerate((ya_ref, yb_ref, yc_ref)):
            up = _dot(y_ref[...], p_ref[n])
            term = _sigmoid(gm_ref[:, n * D_MODEL:(n + 1) * D_MODEL]) * up
            acc = term if acc is None else acc + term
        m_ref[...] = acc.astype(bf16)
    return _rt("merge_fwd_" + tag, body, rows, ROW_TILE,
               [(outs[0], BRANCH, 0), (outs[1], BRANCH, 0), (outs[2], BRANCH, 0), (proj, GM_WIDTH, 0)],
               [bp], [(D_MODEL, bf16)])[0]


def _merge_bwd(tag, proj, outs, bp, dmerged, rows):
    def body(i, dm_ref, ya_ref, yb_ref, yc_ref, gm_ref, p_ref, da_ref, db_ref, dc_ref, dgm_ref, dp_ref):
        dm = dm_ref[...]
        for n, (y_ref, dy_ref) in enumerate(((ya_ref, da_ref), (yb_ref, db_ref), (yc_ref, dc_ref))):
            yv = y_ref[...]
            up = _dot(yv, p_ref[n])
            gt = _sigmoid(gm_ref[:, n * D_MODEL:(n + 1) * D_MODEL])
            dup = (dm * gt).astype(bf16)
            dgm_ref[:, n * D_MODEL:(n + 1) * D_MODEL] = (dm * up * gt * (1.0 - gt)).astype(bf16)
            dy_ref[...] = _dot_nt(dup, p_ref[n])
            dp_ref[n] += _dot_tn(yv, dup)
    return _rt("merge_bwd_" + tag, body, rows, ROW_TILE,
               [(dmerged, D_MODEL, 0), (outs[0], BRANCH, 0), (outs[1], BRANCH, 0), (outs[2], BRANCH, 0),
                (proj, GM_WIDTH, 0)],
               [bp], [(BRANCH, f32), (BRANCH, f32), (BRANCH, f32), (GM_WIDTH, bf16)],
               acc_outs=[(N_BRANCH, BRANCH, D_MODEL)])


def _block_diag(blocks):
    g, r, c = blocks.shape
    on_diag = (lax.broadcasted_iota(jnp.int32, (g * r, g * c), 0) // r
               == lax.broadcasted_iota(jnp.int32, (g * r, g * c), 1) // c)
    tiled = jnp.broadcast_to(blocks.reshape(g * r, 1, c), (g * r, g, c)).reshape(g * r, g * c)
    return jnp.where(on_diag, tiled, 0.0)


def _prep_consts(sp):
    p = jax.nn.softmax(sp["hg_lb_logits"], axis=0)
    lower = jnp.cumsum(p, axis=0) - p[0]
    out = []
    for l in range(DEPTH):
        lr = jnp.minimum(sp["s5_lambda_re"][l], S5_EIG_MAX)
        li = sp["s5_lambda_im"][l]
        dt = jnp.exp(sp["s5_log_dt"][l])[:, None]
        mag = jnp.exp(lr * dt)
        ar = mag * jnp.cos(li * dt)
        ai = mag * jnp.sin(li * dt)
        den = lr * lr + li * li
        fr = ((ar - 1.0) * lr + ai * li) / den
        fi = (ai * lr - (ar - 1.0) * li) / den
        br, bi = sp["s5_b_re"][l], sp["s5_b_im"][l]
        bbr = fr[..., None] * br - fi[..., None] * bi
        bbi = fr[..., None] * bi + fi[..., None] * br
        c = {
            "a_re": ar.reshape(1, S5_LANES), "a_im": ai.reshape(1, S5_LANES),
            "b_re": _block_diag(bbr.transpose(0, 2, 1)), "b_im": _block_diag(bbi.transpose(0, 2, 1)),
            "c_re": _block_diag(sp["s5_c_re"][l].transpose(0, 2, 1)),
            "c_im": -_block_diag(sp["s5_c_im"][l].transpose(0, 2, 1)),
            "s5_d": sp["s5_d"][l][None], "glu_b": sp["s5_glu_b"][l][None],
            "hg_lb": lower[l][None], "hg_nw": sp["hg_norm_w"][l][None],
            "rg_cw": sp["rg_conv_w"][l], "rg_cb": sp["rg_conv_b"][l][None],
            "rg_wa": _block_diag(sp["rg_wa"][l]), "rg_ba": sp["rg_ba"][l][None],
            "rg_wx": _block_diag(sp["rg_wx"][l]), "rg_bx": sp["rg_bx"][l][None],
            "rg_sp8": (RG_C * jax.nn.softplus(-sp["rg_lambda"][l]))[None],
        }
        out.append(c)
    return out


def _mixer_fwd(tag, x, hb, w_in, bp, w_out, cst, next_nw, rows):
    proj = _mm1("mix_proj_" + tag, hb, w_in, "nn", rows, IN_TOTAL, D_MODEL, 512, IN_TOTAL // 4, D_MODEL)
    cst = dict(cst)
    out_a, sv_a = _s5_fwd(tag, proj, cst, rows)
    out_b, sv_b = _hg_fwd(tag, proj, cst, rows)
    out_c, sv_c = _rg_fwd(tag, proj, cst, rows)
    merged = _merge_fwd(tag, proj, (out_a, out_b, out_c), bp, rows)
    x_out, hb_out = _mm("mix_out_" + tag, [merged], [w_out], [(0, 0, 0)], 1, "nn", rows, D_MODEL, D_MODEL,
                        512, D_MODEL, D_MODEL, [f32, bf16], _residual_then_norm(1.0), extras=[x], vecs=[next_nw])
    return x_out, hb_out, (x, hb, proj, (out_a, out_b, out_c), merged, sv_a, sv_b, sv_c)


def _mixer_bwd(tag, saved, nw, w_in, bp, w_out, cst, dx, dxb, rows):
    x, hb, proj, outs, merged, sv_a, sv_b, sv_c = saved
    d_wout = _mm1("mix_dwout_" + tag, merged, dxb, "tn", D_MODEL, D_MODEL, rows, D_MODEL, D_MODEL, TOKEN_K,
                  out_dtype=bf16)
    dmerged = _mm1("mix_dmerged_" + tag, dxb, w_out, "nt", rows, D_MODEL, D_MODEL, 512, D_MODEL, D_MODEL)
    d_a, d_b, d_c, dgm, d_bp = _merge_bwd(tag, proj, outs, bp, dmerged, rows)
    dxbc, dgatec, dcst_c = _rg_bwd(tag, sv_c, proj, cst, d_c, rows)
    dq, dz, dv, dg, dcst_b = _hg_bwd(tag, sv_b, proj, cst, d_b, rows)
    du, dcst_a, d_glu_w = _s5_bwd(tag, sv_a, proj, cst, d_a, rows)
    dproj = jnp.concatenate([dgm, du, dq, dz, dv, dg, dxbc, dgatec], axis=1)
    d_win = _mm1("mix_dwin_" + tag, hb, dproj, "tn", D_MODEL, IN_TOTAL, rows, D_MODEL, IN_TOTAL // 4, TOKEN_K,
                 out_dtype=bf16)
    dx_in, dxb_in, d_nw = _mm("mix_dh_" + tag, [dproj], [w_in], [(0, 0, 0)], 1, "nt", rows, D_MODEL, IN_TOTAL,
                              512, D_MODEL, IN_TOTAL // 2, [f32, bf16], _norm_bwd_epilogue, extras=[x, dx], vecs=[nw],
                              n_part=1)
    dcst = {**dcst_a, **dcst_b, **dcst_c}
    return dx_in, dxb_in, jnp.sum(d_nw, axis=0), d_win, d_bp, d_wout, d_glu_w, dcst


def _local_step(x, target, weights_of, small, grads_done):
    rows = x.shape[0]
    consts, consts_vjp = jax.vjp(_prep_consts, small)
    norm_w = small["norm_w"]
    saved = []
    h = x
    hb = _rms_fwd("first_norm", x, norm_w[0, 0][None], rows)
    for l in range(DEPTH):
        t = str(l)
        after = [norm_w[l + 1, 0][None]] if l + 1 < DEPTH else []
        wa = weights_of(l, "a")
        h, hb, sv0 = _ffn_fwd(t + "a", h, hb, *wa, [norm_w[l, 1][None]], rows)
        wm = weights_of(l, "mix")
        cst = dict(consts[l])
        cst["glu_w"] = wm[3]
        h, hb, sv1 = _mixer_fwd(t, h, hb, *wm[:3], cst, norm_w[l, 2][None], rows)
        wb = weights_of(l, "b")
        h, hb, sv2 = _ffn_fwd(t + "b", h, hb, *wb, after, rows)
        saved.append((sv0, sv1, sv2, cst, wa, wm, wb))
    dx, dxb, loss, d_fnw = _loss_head(h, small["final_norm_w"][None], target, rows)
    d_norm = [None] * DEPTH
    d_consts = [None] * DEPTH
    for l in reversed(range(DEPTH)):
        t = str(l)
        sv0, sv1, sv2, cst, wa, wm, wb = saved[l]
        dx, dxb, dn2, dg1, du1, dd1 = _ffn_bwd(t + "b", sv2, norm_w[l, 2][None], *wb, dx, dxb, rows)
        grads_done(l, "b", [dg1, du1, dd1])
        dx, dxb, dn1, d_win, d_bp, d_wout, d_glu_w, dcst = _mixer_bwd(
            t, sv1, norm_w[l, 1][None], *wm[:3], cst, dx, dxb, rows)
        grads_done(l, "mix", [d_win, d_bp, d_wout, d_glu_w])
        dx, dxb, dn0, dg0, du0, dd0 = _ffn_bwd(t + "a", sv0, norm_w[l, 0][None], *wa, dx, dxb, rows)
        grads_done(l, "a", [dg0, du0, dd0])
        d_norm[l] = jnp.concatenate([dn0, dn1, dn2], axis=0)
        d_consts[l] = dcst
    (g_small,) = consts_vjp(d_consts)
    g_small = dict(g_small)
    g_small["norm_w"] = g_small["norm_w"] + jnp.stack(d_norm)
    g_small["final_norm_w"] = g_small["final_norm_w"] + d_fnw[0]
    return loss[0, 0], dx, g_small


def _other_chips(x, y):
    return [(1 - x, y), (x, 1 - y), (1 - x, 1 - y)]


def _gather_over_ici(shards):
    n = len(shards)

    def copies(in_refs, out_refs, send_sems, recv_sems):
        x, y, c = _place()
        cps = []
        for i in range(n):
            mine = out_refs[i].at[4 * x + 2 * y + c]
            cps.append(pltpu.make_async_copy(in_refs[i], mine, send_sems.at[5 * i + 4]))
            for k, to in enumerate([(x, y, 1 - c)] + [(px, py, c) for px, py in _other_chips(x, y)]):
                cps.append(pltpu.make_async_remote_copy(
                    src_ref=in_refs[i], dst_ref=mine, send_sem=send_sems.at[5 * i + k],
                    recv_sem=recv_sems.at[5 * i + k], device_id=to, device_id_type=MESH_IDS))
        return cps

    return _Carry(shards, [jax.ShapeDtypeStruct((N_DEV,) + s.shape, s.dtype) for s in shards], 5 * n, copies)


def _gather_forward(name, landings):
    n = len(landings)

    def body(*refs):
        in_refs, out_refs = refs[:n], refs[n:2 * n]
        send_sems, recv_sems = refs[2 * n:]
        x, y, c = _place()
        cps = []
        for i in range(n):
            for j, (px, py) in enumerate(_other_chips(x, y)):
                block = 4 * px + 2 * py + c
                cps.append(pltpu.make_async_remote_copy(
                    src_ref=in_refs[i].at[block], dst_ref=out_refs[i].at[block], send_sem=send_sems.at[3 * i + j],
                    recv_sem=recv_sems.at[3 * i + j], device_id=(x, y, 1 - c), device_id_type=MESH_IDS))
        for cp in cps:
            cp.start()
        for cp in cps:
            cp.wait()

    return pl.pallas_call(
        body, name=name, out_shape=[jax.ShapeDtypeStruct(a.shape, a.dtype) for a in landings],
        in_specs=[ANY_SPEC] * n, out_specs=[ANY_SPEC] * n, input_output_aliases={i: i for i in range(n)},
        scratch_shapes=[pltpu.SemaphoreType.DMA((3 * n,)), pltpu.SemaphoreType.DMA((3 * n,))],
    )(*landings)


def _all_gather(name, shards):
    n = len(shards)

    def body(*refs):
        x_refs, out_refs = refs[:n], refs[n:2 * n]
        send_sems, recv_sems, local_sems = refs[2 * n:]
        x, y, c = _place()
        me, sibling = (x, y, c), (x, y, 1 - c)
        chips = [(1 - x, y), (x, 1 - y), (1 - x, 1 - y)]

        def blk(i, px, py, pc):
            return out_refs[i].at[4 * px + 2 * py + pc]

        def copy(i, k, block, to, src=None):
            return pltpu.make_async_remote_copy(
                src_ref=blk(i, *block) if src is None else src, dst_ref=blk(i, *block),
                send_sem=send_sems.at[7 * i + k], recv_sem=recv_sems.at[7 * i + k], device_id=to,
                device_id_type=MESH_IDS)

        mine = [pltpu.make_async_copy(x_refs[i], blk(i, *me), local_sems.at[i]) for i in range(n)]
        for cp in mine:
            cp.start()
        first = []
        for i in range(n):
            first.append(copy(i, 0, me, sibling, src=x_refs[i]))
            first += [copy(i, 1 + j, me, (*chip, c), src=x_refs[i]) for j, chip in enumerate(chips)]
        for cp in first:
            cp.start()
        passed = []
        for j, chip in enumerate(chips):
            for i in range(n):
                copy(i, 1 + j, (*chip, c), me).wait_recv()
                fwd = copy(i, 4 + j, (*chip, c), sibling)
                fwd.start()
                passed.append(fwd)
        for i in range(n):
            copy(i, 0, sibling, me).wait_recv()
            for j, chip in enumerate(chips):
                copy(i, 4 + j, (*chip, 1 - c), me).wait_recv()
        for cp in first + passed:
            cp.wait_send()
        for cp in mine:
            cp.wait()

    return pl.pallas_call(
        body, name=name, out_shape=[jax.ShapeDtypeStruct((N_DEV,) + s.shape, s.dtype) for s in shards],
        in_specs=[ANY_SPEC] * n, out_specs=[ANY_SPEC] * n,
        scratch_shapes=[pltpu.SemaphoreType.DMA((7 * n,)), pltpu.SemaphoreType.DMA((7 * n,)),
                        pltpu.SemaphoreType.DMA((n,))],
    )(*shards)


def _row_tile(rows):
    return rows if rows <= 512 else next(t for t in range(512, 7, -8) if rows % t == 0)


def _sums_over_ici(chip_sums):
    n = len(chip_sums)

    def copies(in_refs, out_refs, send_sems, recv_sems):
        x, y, c = _place()
        return [pltpu.make_async_remote_copy(
            src_ref=in_refs[i].at[2 * px + py], dst_ref=out_refs[i].at[k], send_sem=send_sems.at[3 * i + k],
            recv_sem=recv_sems.at[3 * i + k], device_id=(px, py, c), device_id_type=MESH_IDS)
            for i in range(n) for k, (px, py) in enumerate(_other_chips(x, y))]

    return _Carry(chip_sums, [jax.ShapeDtypeStruct((3,) + t.shape[1:], t.dtype) for t in chip_sums], 3 * n, copies)


def _reduce_scatter(tag, parts, hosts=None):
    n = len(parts)
    _, _, c = _place()

    def body_pair(*refs):
        p_refs, got_refs = refs[:n], refs[n:2 * n]
        send_sems, recv_sems = refs[2 * n:]
        x, y, c = _place()
        cps = [pltpu.make_async_remote_copy(
            src_ref=p_refs[i].at[:, 1 - c], dst_ref=got_refs[i], send_sem=send_sems.at[i], recv_sem=recv_sems.at[i],
            device_id=(x, y, 1 - c), device_id_type=MESH_IDS) for i in range(n)]
        for cp in cps:
            cp.start()
        for cp in cps:
            cp.wait()

    from_sibling = pl.pallas_call(
        body_pair, name="rs_pair_" + tag,
        out_shape=[jax.ShapeDtypeStruct((4,) + p.shape[2:], p.dtype) for p in parts],
        in_specs=[ANY_SPEC] * n, out_specs=[ANY_SPEC] * n,
        scratch_shapes=[pltpu.SemaphoreType.DMA((n,)), pltpu.SemaphoreType.DMA((n,))],
    )(*parts)

    def body_add(idx_ref, *refs):
        p = pl.program_id(0)
        for q in range(n):
            @pl.when(p == q)
            def _(p_ref=refs[q], g_ref=refs[n + q], o_ref=refs[2 * n + q]):
                o_ref[...] = (p_ref[...].astype(f32) + g_ref[...].astype(f32)).astype(o_ref.dtype)

    def at(q):
        return lambda p, j, idx: jnp.clip(j + 4 * (p - q), 0, 3)

    in_specs, out_specs = [], []
    for q, part in enumerate(parts):
        in_specs.append(pl.BlockSpec((None, None) + part.shape[2:],
                                     lambda p, j, idx, blk=at(q): (blk(p, j, idx), idx[0], 0, 0)))
    for q, part in enumerate(parts):
        spec = pl.BlockSpec((None,) + part.shape[2:], lambda p, j, idx, blk=at(q): (blk(p, j, idx), 0, 0))
        in_specs.append(spec)
        out_specs.append(spec)
    chip_sums = pl.pallas_call(
        body_add, name="rs_pair_sum_" + tag,
        out_shape=[jax.ShapeDtypeStruct((4,) + p.shape[2:], p.dtype) for p in parts],
        grid_spec=pltpu.PrefetchScalarGridSpec(num_scalar_prefetch=1, grid=(n, 4), in_specs=in_specs,
                                               out_specs=out_specs),
        compiler_params=_cparams(("arbitrary", "arbitrary")),
    )(jnp.stack([c]).astype(jnp.int32), *parts, *from_sibling)

    others = [None] * n
    riding = set()
    for host, which in (hosts or {}).items():
        rider = _sums_over_ici([chip_sums[i] for i in which])
        _CARRIED[host] = rider
        for pos, i in enumerate(which):
            others[i] = functools.partial(lambda r, p: r.outs[p], rider, pos)
        riding.update(which)
    rest = [i for i in range(n) if i not in riding]
    if rest:
        alone = _sums_over_ici([chip_sums[i] for i in rest])

        def body_chips(*refs):
            k = len(rest)
            cps = alone.copies(refs[:k], refs[k:2 * k], *refs[2 * k:])
            for cp in cps:
                cp.start()
            for cp in cps:
                cp.wait()

        from_chips = pl.pallas_call(
            body_chips, name="rs_chips_" + tag, out_shape=alone.out_shapes,
            in_specs=[ANY_SPEC] * len(rest), out_specs=[ANY_SPEC] * len(rest), scratch_shapes=alone.sems(),
        )(*alone.ins)
        for pos, i in enumerate(rest):
            others[i] = functools.partial(lambda got: got, from_chips[pos])
    return list(zip(chip_sums, others))


def _own_index():
    x, y, _ = _place()
    return jnp.stack([2 * x + y]).astype(jnp.int32)


def _own_total(name, chip_sum, others):
    _, r, cols = chip_sum.shape
    tr = _row_tile(r)

    def body(idx_ref, t_ref, g_ref, o_ref):
        o_ref[...] = ((t_ref[...].astype(f32) + g_ref[0].astype(f32)) + g_ref[1].astype(f32)) + g_ref[2].astype(f32)

    return pl.pallas_call(
        body, name=name, out_shape=jax.ShapeDtypeStruct((r, cols), f32),
        grid_spec=pltpu.PrefetchScalarGridSpec(
            num_scalar_prefetch=1, grid=(r // tr,),
            in_specs=[pl.BlockSpec((None, tr, cols), lambda t, idx: (idx[0], t, 0)),
                      pl.BlockSpec((3, tr, cols), lambda t, idx: (0, t, 0))],
            out_specs=pl.BlockSpec((tr, cols), lambda t, idx: (t, 0))),
        compiler_params=_cparams(("parallel",)),
    )(_own_index(), chip_sum, others)


def _adam_update(w, gv, m, v):
    m_new = ADAM_B1 * m + (1.0 - ADAM_B1) * gv
    v_new = ADAM_B2 * v + (1.0 - ADAM_B2) * (gv * gv)
    m_hat = m_new / (1.0 - ADAM_B1 ** ADAM_STEP)
    v_hat = v_new / (1.0 - ADAM_B2 ** ADAM_STEP)
    return -ADAM_LR * (m_hat / (jnp.sqrt(v_hat) + ADAM_EPS) + ADAM_WD * w), m_new, v_new


def _adamw_reduced(name, w, pieces, m, v):
    n_p, rows, cols = w.shape
    tr = _row_tile(rows)

    def body(idx_ref, w_ref, *refs):
        red = refs[:2 * n_p]
        m_ref, v_ref, g_ref, d_ref, nm_ref, nv_ref = refs[2 * n_p:]
        p = pl.program_id(0)
        for q in range(n_p):
            @pl.when(p == q)
            def _(t_ref=red[2 * q], o_ref=red[2 * q + 1]):
                gv = ((t_ref[...].astype(f32) + o_ref[0].astype(f32)) + o_ref[1].astype(f32)) + o_ref[2].astype(f32)
                g_ref[...] = gv
                d_ref[...], nm_ref[...], nv_ref[...] = _adam_update(w_ref[...], gv, m_ref[...], v_ref[...])

    spec = pl.BlockSpec((None, tr, cols), lambda p, t, idx: (p, t, 0))
    red_specs, red_args = [], []
    for q, (chip_sum, others) in enumerate(pieces):
        red_specs.append(pl.BlockSpec((None, tr, cols), lambda p, t, idx, q=q: (idx[0], jnp.where(p == q, t, 0), 0)))
        red_specs.append(pl.BlockSpec((3, tr, cols), lambda p, t, idx, q=q: (0, jnp.where(p == q, t, 0), 0)))
        red_args += [chip_sum, others]
    return pl.pallas_call(
        body, name=name, out_shape=[jax.ShapeDtypeStruct((n_p, rows, cols), f32)] * 4,
        grid_spec=pltpu.PrefetchScalarGridSpec(
            num_scalar_prefetch=1, grid=(n_p, rows // tr),
            in_specs=[spec] + red_specs + [spec, spec], out_specs=[spec] * 4),
        compiler_params=_cparams(("parallel", "parallel")),
    )(_own_index(), w, *red_args, m, v)


def _adamw(name, w, g, m, v):
    rows, cols = w.shape
    tr = _row_tile(rows)

    def body(w_ref, g_ref, m_ref, v_ref, d_ref, nm_ref, nv_ref):
        d_ref[...], nm_ref[...], nv_ref[...] = _adam_update(w_ref[...], g_ref[...], m_ref[...], v_ref[...])

    spec = pl.BlockSpec((tr, cols), lambda i: (i, 0))
    return pl.pallas_call(
        body, name=name, grid=(rows // tr,), in_specs=[spec] * 4, out_specs=[spec] * 3,
        out_shape=[jax.ShapeDtypeStruct((rows, cols), f32)] * 3, compiler_params=_cparams(("parallel",)),
    )(w, g, m, v)


WEIGHT_NAMES = ["norm_w", "final_norm_w", "ffn_gate", "ffn_up", "ffn_down", "w_in", "branch_proj", "w_out",
                "s5_lambda_re", "s5_lambda_im", "s5_log_dt", "s5_b_re", "s5_b_im", "s5_c_re", "s5_c_im", "s5_d",
                "s5_glu_w", "s5_glu_b", "hg_lb_logits", "hg_norm_w", "rg_conv_w", "rg_conv_b", "rg_wa", "rg_ba",
                "rg_wx", "rg_bx", "rg_lambda"]
SHARDED = {"ffn_gate": (3, "gate"), "ffn_up": (3, "up"), "ffn_down": (2, "down"), "w_in": (2, "w_in"),
           "branch_proj": (3, "bp"), "w_out": (1, "w_out"), "s5_glu_w": (1, "glu_w"),
           "norm_w": (2, None), "rg_conv_w": (2, None)}
BIG = ["ffn_gate", "ffn_up", "ffn_down", "w_in", "branch_proj", "w_out", "s5_glu_w"]
TRANSPOSED = ("ffn_gate", "ffn_up")
PARTS = {"a": [("ffn_gate", 0, 0), ("ffn_up", 0, 0), ("ffn_down", 0, 0)],
         "b": [("ffn_gate", 1, 0), ("ffn_up", 1, 0), ("ffn_down", 1, 0)],
         "mix": [("w_in", None, 1), ("branch_proj", None, 2), ("w_out", None, 0), ("s5_glu_w", None, 0)]}
AG_HOSTS = {"ffn_up_0a": (0, "mix", [0]), "ffn_down_0a": (0, "mix", [1, 2, 3]), "mix_proj_0": (0, "b", [0, 1, 2]),
            "s5_out_0": (1, "a", [0]), "hg_fwd_0": (1, "a", [1]), "rg_fwd_0": (1, "a", [2]),
            "merge_fwd_0": (1, "b", [0]), "ffn_up_0b": (1, "b", [1]), "ffn_down_0b": (1, "b", [2]),
            "s5_scan_fwd_0": (1, "mix", [0, 1]), "mix_out_0": (1, "mix", [2, 3])}
RS_HOSTS = {(1, "b"): {"s5_scan_bwd_1": [0, 1, 2]},
            (1, "mix"): {"ffn_bwd_mid_1a": [1, 2, 3], "mix_dh_0": [0]},
            (1, "a"): {"mix_dwin_0": [0, 1], "merge_bwd_0": [2]},
            (0, "b"): {"s5_scan_bwd_0": [0, 1, 2]},
            (0, "mix"): {"ffn_bwd_mid_0a": [1, 2, 3], "ffn_dh_0a": [0]}}
SMALL_SHARDED = ["norm_w", "rg_conv_w"]
REPLICATED = [n for n in WEIGHT_NAMES if n not in SHARDED]
LANES = 128


PACK_ROWS = 512


def _pack_rows(arrays, names):
    pieces = []
    for n in names:
        flat = arrays[n].reshape(-1)
        pieces.append(jnp.pad(flat, (0, -flat.shape[0] % LANES)).reshape(-1, LANES))
    rows = jnp.concatenate(pieces, axis=0)
    return jnp.pad(rows, ((0, -rows.shape[0] % PACK_ROWS), (0, 0)))


def _unpack_rows(rows, names, like):
    out, r0 = {}, 0
    for n in names:
        size = math.prod(like[n].shape)
        nrows = -(-size // LANES)
        out[n] = rows[r0:r0 + nrows].reshape(-1)[:size].reshape(like[n].shape)
        r0 += nrows
    return out


def _unshard(gathered, axis):
    g = jnp.moveaxis(gathered, 0, axis)
    shp = g.shape
    return g.reshape(shp[:axis] + (shp[axis] * shp[axis + 1],) + shp[axis + 2:])


RELAYOUT_ROWS = 256


def _column_runs(width, first_col):
    total = N_DEV * width
    runs = []
    for j in range(N_DEV):
        start = (width * j + first_col) % total
        head = min(width, total - start)
        runs.append((j, 0, start, head))
        if head < width:
            runs.append((j, head, 0, width - head))
    return runs


def _unshard_columns(name, gathered, first_col=0):
    _, r, c = gathered.shape
    tr = min(RELAYOUT_ROWS, r)
    runs = _column_runs(c, first_col)

    def body(g_ref, o_ref):
        for j, off, dst, length in runs:
            o_ref[:, dst:dst + length] = g_ref[j, :, off:off + length]

    return pl.pallas_call(
        body, name=name, grid=(r // tr,), in_specs=[pl.BlockSpec((N_DEV, tr, c), lambda i: (0, i, 0))],
        out_specs=pl.BlockSpec((tr, N_DEV * c), lambda i: (i, 0)),
        out_shape=jax.ShapeDtypeStruct((r, N_DEV * c), gathered.dtype), compiler_params=_cparams(("parallel",)),
    )(gathered)


def _columns_to_blocks(name, full, first_col=0):
    r, total = full.shape
    c = total // N_DEV
    tr = min(RELAYOUT_ROWS, r)
    runs = _column_runs(c, first_col)

    def body(x_ref, o_ref):
        for j, off, src, length in runs:
            o_ref[j // 2, j % 2, :, off:off + length] = x_ref[:, src:src + length].astype(bf16)

    return pl.pallas_call(
        body, name=name, grid=(r // tr,), in_specs=[pl.BlockSpec((tr, total), lambda i: (i, 0))],
        out_specs=pl.BlockSpec((4, 2, tr, c), lambda i: (0, 0, i, 0)),
        out_shape=jax.ShapeDtypeStruct((4, 2, r, c), bf16), compiler_params=_cparams(("parallel",)),
    )(full)


def _to_blocks(full, axis):
    shp = full.shape
    g = full.reshape(shp[:axis] + (4, 2, shp[axis] // N_DEV) + shp[axis + 1:])
    g = jnp.moveaxis(g, (axis, axis + 1), (0, 1))
    return g.reshape(4, 2, -1, g.shape[-1])


W_IN_SPLIT = IN_TOTAL - GM_WIDTH


def kernel(x, norm_w, final_norm_w, ffn_gate, ffn_up, ffn_down, w_in, branch_proj, w_out, s5_lambda_re, s5_lambda_im, s5_log_dt, s5_b_re, s5_b_im, s5_c_re, s5_c_im, s5_d, s5_glu_w, s5_glu_b, hg_lb_logits, hg_norm_w, rg_conv_w, rg_conv_b, rg_wa, rg_ba, rg_wx, rg_bx, rg_lambda, loss_target, m_norm_w, m_final_norm_w, m_ffn_gate, m_ffn_up, m_ffn_down, m_w_in, m_branch_proj, m_w_out, m_s5_lambda_re, m_s5_lambda_im, m_s5_log_dt, m_s5_b_re, m_s5_b_im, m_s5_c_re, m_s5_c_im, m_s5_d, m_s5_glu_w, m_s5_glu_b, m_hg_lb_logits, m_hg_norm_w, m_rg_conv_w, m_rg_conv_b, m_rg_wa, m_rg_ba, m_rg_wx, m_rg_bx, m_rg_lambda, v_norm_w, v_final_norm_w, v_ffn_gate, v_ffn_up, v_ffn_down, v_w_in, v_branch_proj, v_w_out, v_s5_lambda_re, v_s5_lambda_im, v_s5_log_dt, v_s5_b_re, v_s5_b_im, v_s5_c_re, v_s5_c_im, v_s5_d, v_s5_glu_w, v_s5_glu_b, v_hg_lb_logits, v_hg_norm_w, v_rg_conv_w, v_rg_conv_b, v_rg_wa, v_rg_ba, v_rg_wx, v_rg_bx, v_rg_lambda):
    w = dict(zip(WEIGHT_NAMES, (norm_w, final_norm_w, ffn_gate, ffn_up, ffn_down, w_in, branch_proj, w_out,
                                s5_lambda_re, s5_lambda_im, s5_log_dt, s5_b_re, s5_b_im, s5_c_re, s5_c_im, s5_d,
                                s5_glu_w, s5_glu_b, hg_lb_logits, hg_norm_w, rg_conv_w, rg_conv_b, rg_wa, rg_ba,
                                rg_wx, rg_bx, rg_lambda)))
    m = dict(zip(WEIGHT_NAMES, (m_norm_w, m_final_norm_w, m_ffn_gate, m_ffn_up, m_ffn_down, m_w_in, m_branch_proj,
                                m_w_out, m_s5_lambda_re, m_s5_lambda_im, m_s5_log_dt, m_s5_b_re, m_s5_b_im, m_s5_c_re,
                                m_s5_c_im, m_s5_d, m_s5_glu_w, m_s5_glu_b, m_hg_lb_logits, m_hg_norm_w, m_rg_conv_w,
                                m_rg_conv_b, m_rg_wa, m_rg_ba, m_rg_wx, m_rg_bx, m_rg_lambda)))
    v = dict(zip(WEIGHT_NAMES, (v_norm_w, v_final_norm_w, v_ffn_gate, v_ffn_up, v_ffn_down, v_w_in, v_branch_proj,
                                v_w_out, v_s5_lambda_re, v_s5_lambda_im, v_s5_log_dt, v_s5_b_re, v_s5_b_im, v_s5_c_re,
                                v_s5_c_im, v_s5_d, v_s5_glu_w, v_s5_glu_b, v_hg_lb_logits, v_hg_norm_w, v_rg_conv_w,
                                v_rg_conv_b, v_rg_wa, v_rg_ba, v_rg_wx, v_rg_bx, v_rg_lambda)))
    rows = x.shape[1]

    _CARRIED.clear()

    def shard_of(piece, l):
        n, k, _ = piece
        shard = w[n][l] if k is None else w[n][l, k]
        return (jnp.swapaxes(shard, 0, 1) if n in TRANSPOSED else shard).astype(bf16)

    def assemble(l, part, gathered):
        full = []
        for j, (piece, g) in enumerate(zip(PARTS[part], gathered)):
            tag = "unshard_%d%s%d" % (l, part, j)
            if piece[0] == "w_in":
                full.append(_unshard_columns(tag, g, first_col=GM_WIDTH))
            elif piece[0] == "branch_proj":
                full.append(_unshard_columns(tag, g.reshape(N_DEV, -1, g.shape[-1])).reshape(N_BRANCH, BRANCH, D_MODEL))
            elif piece[2] == g.ndim - 2:
                full.append(_unshard_columns(tag, g))
            else:
                full.append(_unshard(g, piece[2]))
        return full

    n_a = len(PARTS["a"])
    first = _all_gather("gather_weights", [shard_of(p, 0) for p in PARTS["a"]] + [w[n] for n in SMALL_SHARDED])
    small = {n: w[n] for n in REPLICATED}
    for n, g in zip(SMALL_SHARDED, first[n_a:]):
        small[n] = _unshard(g, SHARDED[n][0])
    riders = {}
    for host, (l, part, which) in AG_HOSTS.items():
        rider = _gather_over_ici([shard_of(PARTS[part][j], l) for j in which])
        _CARRIED[host] = rider
        riders.setdefault((l, part), []).append((which, rider))

    def weights_of(l, part):
        if (l, part) == (0, "a"):
            return assemble(l, part, first[:n_a])
        landed = [None] * len(PARTS[part])
        for which, rider in riders[l, part]:
            for j, buf in zip(which, rider.outs):
                landed[j] = buf
        return assemble(l, part, _gather_forward("gather_forward_%d%s" % (l, part), landed))

    sums = {}

    def blocks_of(l, part, grads):
        out = []
        for j, (piece, g) in enumerate(zip(PARTS[part], grads)):
            tag = "to_blocks_%d%s%d" % (l, part, j)
            if piece[0] == "w_in":
                out.append(_columns_to_blocks(tag, g, first_col=GM_WIDTH))
            elif piece[0] == "branch_proj":
                out.append(_columns_to_blocks(tag, g.reshape(-1, g.shape[-1])))
            elif piece[2] == g.ndim - 1:
                out.append(_columns_to_blocks(tag, g))
            else:
                out.append(_to_blocks(g, piece[2]).astype(bf16))
        return out

    last_grads = []

    def grads_done(l, part, grads):
        if (l, part) in RS_HOSTS:
            sums[l, part] = _reduce_scatter("%d%s" % (l, part), blocks_of(l, part, grads), hosts=RS_HOSTS[l, part])
        else:
            last_grads.extend(blocks_of(l, part, grads))

    loss_part, dx, g_small = _local_step(x[0], loss_target[0], weights_of, small, grads_done)
    loss = lax.psum(loss_part, ("x", "y", "c"))

    parts = last_grads + [_to_blocks(g_small[n], SHARDED[n][0]) for n in SMALL_SHARDED]
    rep_rows = _pack_rows(g_small, REPLICATED)
    rep_slice = rep_rows.shape[0] // N_DEV
    parts.append(rep_rows.reshape(4, 2, rep_slice, LANES))
    last = _reduce_scatter("last", parts)
    sums[0, "a"] = last[:n_a]

    grads, delta, new_m, new_v = {}, {}, {}, {}

    def update(n, pieces):
        def view(a):
            a = jnp.swapaxes(a, -1, -2) if n in TRANSPOSED else a
            return a.reshape(len(pieces), -1, a.shape[-1])

        def back(r):
            shp = w[n].shape
            if n in TRANSPOSED:
                return jnp.swapaxes(r.reshape(shp[:-2] + (shp[-1], shp[-2])), -1, -2)
            return r.reshape(shp)

        res = _adamw_reduced("adamw_" + n, view(w[n]), [(t, others()) for t, others in pieces], view(m[n]), view(v[n]))
        grads[n], delta[n], new_m[n], new_v[n] = (back(r) for r in res)

    for n in BIG:
        update(n, [sums[l, part][j] for l in range(DEPTH) for part in ("a", "b", "mix")
                   for j, piece in enumerate(PARTS[part]) if piece[0] == n])
    for j, n in enumerate(SMALL_SHARDED):
        update(n, [last[n_a + j]])
    rep_mine = _own_total("rs_total_small", last[-1][0], last[-1][1]())
    rep_grads = _all_gather("gather_small_grads", [rep_mine])[0].reshape(-1, LANES)
    res = _adamw("adamw_small", _pack_rows(w, REPLICATED), rep_grads, _pack_rows(m, REPLICATED), _pack_rows(v, REPLICATED))
    for dst, src in zip((grads, delta, new_m, new_v), (rep_grads,) + tuple(res)):
        dst.update(_unpack_rows(src, REPLICATED, w))

    return (loss, dx.reshape(x.shape), *[grads[n] for n in WEIGHT_NAMES], *[delta[n] for n in WEIGHT_NAMES],
            *[new_m[n] for n in WEIGHT_NAMES], *[new_v[n] for n in WEIGHT_NAMES])
```

```python
import functools
import math

import jax
import jax.numpy as jnp
from jax import lax
from jax.experimental import pallas as pl
from jax.experimental.pallas import tpu as pltpu

f32 = jnp.float32
bf16 = jnp.bfloat16

D_MODEL = 1024
DEPTH = 2
BRANCH = 512
N_BRANCH = 3
S5_GROUP = 16
S5_GROUPS = 32
S5_STATE = 64
S5_LANES = S5_GROUPS * S5_STATE
S5_EIG_MAX = -1e-4
HG_HEADS = 4
HG_DK = 128
HG_CHUNK = 32
RG_BLOCKS = 8
RG_BLOCK = 64
RG_C = 8.0
D_FF = 2816
EPS = 1e-6
IN_TOTAL = 6656
GM_WIDTH = N_BRANCH * D_MODEL
N_DEV = 8

ADAM_LR = 0.001
ADAM_B1 = 0.9
ADAM_B2 = 0.999
ADAM_EPS = 1e-08
ADAM_WD = 0.01
ADAM_STEP = 10

VMEM_LIMIT_V7X = 56 * 1024 * 1024
ROW_TILE = 256
FF_TILE = 1408
TOKEN_K = 4096
MXU_COLS = 256


def _cparams(sem):
    return pltpu.CompilerParams(dimension_semantics=sem, vmem_limit_bytes=VMEM_LIMIT_V7X)


MESH_IDS = pl.DeviceIdType.MESH
ANY_SPEC = pl.BlockSpec(memory_space=pl.ANY)


def _place():
    return lax.axis_index("x"), lax.axis_index("y"), lax.axis_index("c")


class _Carry:
    def __init__(self, ins, out_shapes, n_sems, copies):
        self.ins, self.out_shapes, self.n_sems, self.copies = list(ins), list(out_shapes), n_sems, copies
        self.outs = None

    def sems(self):
        return [pltpu.SemaphoreType.DMA((self.n_sems,)), pltpu.SemaphoreType.DMA((self.n_sems,))]

    def start(self, when, *riders):
        @pl.when(when)
        def _():
            for cp in self.copies(*riders):
                cp.start()

    def finish(self, when, *riders):
        @pl.when(when)
        def _():
            for cp in self.copies(*riders):
                cp.wait()


_CARRIED = {}


def _call_with_rider(name, body, grid, in_specs, out_specs, out_shape, scratch, semantics, args):
    carry = _CARRIED.pop(name, None)
    if carry is None:
        return pl.pallas_call(body, name=name, grid=grid, in_specs=in_specs, out_specs=out_specs,
                              out_shape=out_shape, scratch_shapes=scratch, compiler_params=_cparams(semantics))(*args)
    n_in, n_out, nci, nco = len(in_specs), len(out_specs), len(carry.ins), len(carry.out_shapes)

    def kern(*refs):
        ids = [pl.program_id(d) for d in range(len(grid))]
        own = refs[:n_in] + refs[n_in + nci:n_in + nci + n_out] + refs[n_in + nci + n_out + nco:-2]
        riders = (refs[n_in:n_in + nci], refs[n_in + nci + n_out:n_in + nci + n_out + nco]) + tuple(refs[-2:])
        carry.start(functools.reduce(jnp.logical_and, [p == 0 for p in ids]), *riders)
        body(*own)
        carry.finish(functools.reduce(jnp.logical_and, [p == g - 1 for p, g in zip(ids, grid)]), *riders)

    res = pl.pallas_call(
        kern, name=name, grid=grid, in_specs=list(in_specs) + [ANY_SPEC] * nci,
        out_specs=list(out_specs) + [ANY_SPEC] * nco, out_shape=list(out_shape) + carry.out_shapes,
        scratch_shapes=list(scratch) + carry.sems(), compiler_params=_cparams(("arbitrary",) * len(grid)),
    )(*args, *carry.ins)
    carry.outs = res[n_out:]
    return res[:n_out]


def _sigmoid(x):
    return 0.5 * jnp.tanh(0.5 * x) + 0.5


def _sigmoid_small(x):
    return 1.0 / (1.0 + jnp.exp(-x))


_GELU_C = math.sqrt(2.0 / math.pi)


def _gelu(x):
    t = jnp.tanh(_GELU_C * (x + 0.044715 * x * x * x))
    return 0.5 * x * (1.0 + t)


def _gelu_grad(x):
    t = jnp.tanh(_GELU_C * (x + 0.044715 * x * x * x))
    return 0.5 * (1.0 + t) + 0.5 * x * (1.0 - t * t) * _GELU_C * (1.0 + 3.0 * 0.044715 * x * x)


def _expm1(x):
    p = x * (1.0 + x * (0.5 + x * (1.0 / 6 + x * (1.0 / 24 + x * (1.0 / 120 + x * (1.0 / 720))))))
    return jnp.where(jnp.abs(x) < 0.3, p, jnp.exp(x) - 1.0)


def _dot(a, b):
    return jnp.dot(a, b, preferred_element_type=f32)


def _dot_nt(a, b):
    return lax.dot_general(a, b, (((1,), (1,)), ((), ())), preferred_element_type=f32)


def _dot_tn(a, b):
    return lax.dot_general(a, b, (((0,), (0,)), ((), ())), preferred_element_type=f32)


def _bdot(a, b):
    return lax.dot_general(a, b, (((2,), (1,)), ((0,), (0,))), preferred_element_type=f32)


def _bdot_nt(a, b):
    return lax.dot_general(a, b, (((2,), (2,)), ((0,), (0,))), preferred_element_type=f32)


def _rows(shape):
    return lax.broadcasted_iota(jnp.int32, shape, 0)


def _scan_fwd(a, b, n):
    row = _rows(a.shape)
    s = 1
    while s < n:
        valid = row >= s
        sh_a = pltpu.roll(a, s, 0)
        sh_b = pltpu.roll(b, s, 0)
        b = b + a * jnp.where(valid, sh_b, 0.0)
        a = a * jnp.where(valid, sh_a, 1.0)
        s *= 2
    return a, b


def _scan_bwd(a, b, n):
    row = _rows(a.shape)
    s = 1
    while s < n:
        valid = row < n - s
        sh_a = pltpu.roll(a, n - s, 0)
        sh_b = pltpu.roll(b, n - s, 0)
        b = b + a * jnp.where(valid, sh_b, 0.0)
        a = a * jnp.where(valid, sh_a, 1.0)
        s *= 2
    return a, b


def _seg_cumsum(x, n, seg):
    pos = _rows(x.shape) % seg
    s = 1
    while s < seg:
        x = x + jnp.where(pos >= s, pltpu.roll(x, s, 0), 0.0)
        s *= 2
    return x


def _seg_rev_cumsum(x, n, seg):
    pos = _rows(x.shape) % seg
    s = 1
    while s < seg:
        x = x + jnp.where(pos < seg - s, pltpu.roll(x, n - s, 0), 0.0)
        s *= 2
    return x


def _head_mean(x):
    parts = []
    for h in range(HG_HEADS):
        m = jnp.mean(x[:, h * HG_DK:(h + 1) * HG_DK], axis=1, keepdims=True)
        parts.append(jnp.broadcast_to(m, (x.shape[0], HG_DK)))
    return jnp.concatenate(parts, axis=1)


def _mm(name, a_list, b_list, terms, n_acc, mode, m, n, k, tm, tn, tk, out_dtypes, epilogue, extras=(), vecs=(),
        n_part=0, chunk=0):
    tm, tn, tk = min(tm, m), min(tn, n), min(tk, k)
    assert m % tm == 0 and n % tn == 0 and k % tk == 0, (name, m, n, k, tm, tn, tk)
    gk = k // tk
    if mode == "tn":
        a_spec = pl.BlockSpec((tk, tm), lambda i, j, kk: (kk, i))
    else:
        a_spec = pl.BlockSpec((tm, tk), lambda i, j, kk: (i, kk))
    if mode == "nt":
        b_spec = pl.BlockSpec((tn, tk), lambda i, j, kk: (j, kk))
    else:
        b_spec = pl.BlockSpec((tk, tn), lambda i, j, kk: (kk, j))
    o_spec = pl.BlockSpec((tm, tn), lambda i, j, kk: (i, j))
    v_spec = pl.BlockSpec((1, tn), lambda i, j, kk: (0, j))
    p_spec = pl.BlockSpec((None, 1, tn), lambda i, j, kk: (i, 0, j))
    dot = {"nn": _dot, "nt": _dot_nt, "tn": _dot_tn}[mode]
    na, nb, ne, nv, no = len(a_list), len(b_list), len(extras), len(vecs), len(out_dtypes)
    carry = _CARRIED.pop(name, None)
    nci, nco = (len(carry.ins), len(carry.out_shapes)) if carry else (0, 0)
    n_in = na + nb + ne + nv + nci
    grid = (m // tm, n // tn, gk)

    def kern(*refs):
        if carry:
            ids = [pl.program_id(d) for d in range(3)]
            riders = (refs[n_in - nci:n_in], refs[n_in + no + n_part:n_in + no + n_part + nco]) + tuple(refs[-2:])
            carry.start(functools.reduce(jnp.logical_and, [p == 0 for p in ids]), *riders)
        compute(*refs)
        if carry:
            carry.finish(functools.reduce(jnp.logical_and, [p == g - 1 for p, g in zip(ids, grid)]), *riders)

    def compute(*refs):
        a_refs = refs[:na]
        b_refs = refs[na:na + nb]
        e_refs = refs[na + nb:na + nb + ne]
        v_refs = refs[na + nb + ne:na + nb + ne + nv]
        o_refs = refs[n_in:n_in + no + n_part]

        def finish(accs):
            outs = epilogue(accs, [e[...] for e in e_refs], [r[...] for r in v_refs])
            for o, val in zip(o_refs, outs):
                o[...] = val.astype(o.dtype)

        def partial_sums():
            sums = [None] * n_acc
            for ai, bi, ci in terms:
                d = dot(a_refs[ai][...].astype(bf16), b_refs[bi][...].astype(bf16))
                sums[ci] = d if sums[ci] is None else sums[ci] + d
            return sums

        if gk == 1 and chunk:
            assert mode in ("nn", "nt") and tn % chunk == 0
            for c0 in range(0, tn, chunk):
                cols = slice(c0, c0 + chunk)
                sums = [None] * n_acc
                for ai, bi, ci in terms:
                    b_part = b_refs[bi][:, cols] if mode == "nn" else b_refs[bi][cols, :]
                    d = dot(a_refs[ai][...].astype(bf16), b_part.astype(bf16))
                    sums[ci] = d if sums[ci] is None else sums[ci] + d
                outs = epilogue(sums, [e[:, cols] for e in e_refs], [r[:, cols] for r in v_refs])
                for o, val in zip(o_refs, outs):
                    o[:, cols] = val.astype(o.dtype)
            return
        if gk == 1:
            finish(partial_sums())
            return
        acc = refs[n_in + no + n_part + nco]
        kk = pl.program_id(2)

        @pl.when(kk == 0)
        def _():
            acc[...] = jnp.zeros_like(acc)

        for ci, d in enumerate(partial_sums()):
            acc[ci] += d

        @pl.when(kk == gk - 1)
        def _():
            finish([acc[c] for c in range(n_acc)])

    res = pl.pallas_call(
        kern, name=name,
        grid=grid,
        in_specs=[a_spec] * na + [b_spec] * nb + [o_spec] * ne + [v_spec] * nv + [ANY_SPEC] * nci,
        out_specs=[o_spec] * no + [p_spec] * n_part + [ANY_SPEC] * nco,
        out_shape=([jax.ShapeDtypeStruct((m, n), dt) for dt in out_dtypes]
                   + [jax.ShapeDtypeStruct((m // tm, 1, n), f32)] * n_part + (carry.out_shapes if carry else [])),
        scratch_shapes=([pltpu.VMEM((n_acc, tm, tn), f32)] if gk > 1 else []) + (carry.sems() if carry else []),
        compiler_params=_cparams(("arbitrary",) * 3 if carry else ("parallel", "parallel", "arbitrary")),
    )(*a_list, *b_list, *extras, *vecs, *(carry.ins if carry else []))
    if carry:
        carry.outs = res[no + n_part:]
        res = res[:no + n_part]
    return res


def _mm1(name, a, b, mode, m, n, k, tm, tn, tk, out_dtype=f32, scale=None):
    def epi(accs, extras, vecs):
        return [accs[0] if scale is None else accs[0] * scale]
    return _mm(name, [a], [b], [(0, 0, 0)], 1, mode, m, n, k, tm, tn, tk, [out_dtype], epi)[0]


def _rt(name, body, rows, tm, row_ins, consts, row_outs, acc_outs=(), scratch=(), reverse=False):
    tm = min(tm, rows)
    assert rows % tm == 0
    nt = rows // tm

    def tile(i):
        return nt - 1 - i if reverse else i

    in_specs, args = [], []
    for spec in row_ins:
        arr = spec[0]
        if isinstance(spec[1], int):
            in_specs.append(pl.BlockSpec((tm, spec[1]), lambda i, cb=spec[2]: (tile(i), cb)))
        else:
            in_specs.append(pl.BlockSpec(spec[1], lambda i, fn=spec[2]: fn(tile(i))))
        args.append(arr)
    for c in consts:
        in_specs.append(pl.BlockSpec(c.shape, lambda i, nd=c.ndim: (0,) * nd))
        args.append(c)
    out_specs, out_shape = [], []
    for spec in row_outs:
        if isinstance(spec[0], int):
            out_specs.append(pl.BlockSpec((tm, spec[0]), lambda i: (tile(i), 0)))
            out_shape.append(jax.ShapeDtypeStruct((rows, spec[0]), spec[1]))
        else:
            out_specs.append(pl.BlockSpec(spec[1], lambda i, fn=spec[2]: fn(tile(i))))
            out_shape.append(jax.ShapeDtypeStruct(spec[0], spec[3]))
    for shp in acc_outs:
        out_specs.append(pl.BlockSpec(shp, lambda i, nd=len(shp): (0,) * nd))
        out_shape.append(jax.ShapeDtypeStruct(shp, f32))
    n_in = len(args)
    n_row_out = len(row_outs)
    n_acc = len(acc_outs)
    n_out = n_row_out + n_acc
    carry = _CARRIED.pop(name, None)
    nci, nco = (len(carry.ins), len(carry.out_shapes)) if carry else (0, 0)

    def kern(*refs):
        i = pl.program_id(0)
        if carry:
            own = refs[:n_in] + refs[n_in + nci:n_in + nci + n_out] + refs[n_in + nci + n_out + nco:-2]
            riders = (refs[n_in:n_in + nci], refs[n_in + nci + n_out:n_in + nci + n_out + nco]) + tuple(refs[-2:])
            carry.start(i == 0, *riders)
        else:
            own = refs
        acc_refs = own[n_in + n_row_out:n_in + n_out]

        @pl.when(i == 0)
        def _():
            for r in acc_refs:
                r[...] = jnp.zeros_like(r)

        body(i, *own)
        if carry:
            carry.finish(i == nt - 1, *riders)

    res = pl.pallas_call(
        kern, name=name, grid=(nt,), in_specs=in_specs + [ANY_SPEC] * nci, out_specs=out_specs + [ANY_SPEC] * nco,
        out_shape=out_shape + (carry.out_shapes if carry else []),
        scratch_shapes=list(scratch) + (carry.sems() if carry else []), compiler_params=_cparams(("arbitrary",)),
    )(*args, *(carry.ins if carry else []))
    if carry:
        carry.outs = res[n_out:]
        res = res[:n_out]
    return res


def _rms_rows(xv, wv):
    r = lax.rsqrt(jnp.mean(xv * xv, axis=1, keepdims=True) + EPS)
    return (xv * r * wv).astype(bf16)


def _rms_bwd_rows(xv, dhv, wv, dres):
    r = lax.rsqrt(jnp.mean(xv * xv, axis=1, keepdims=True) + EPS)
    xn = xv * r
    dxn = dhv * wv
    dx = dres + r * (dxn - xn * jnp.mean(dxn * xn, axis=1, keepdims=True))
    return [dx, dx.astype(bf16), jnp.sum(dhv * xn, axis=0, keepdims=True)]


def _rms_fwd(name, x, w, rows):
    def body(i, x_ref, w_ref, h_ref):
        h_ref[...] = _rms_rows(x_ref[...], w_ref[...])
    return _rt(name, body, rows, ROW_TILE, [(x, D_MODEL, 0)], [w], [(D_MODEL, bf16)])[0]


def _residual_then_norm(scale):
    def epi(accs, extras, vecs):
        x_out = extras[0] + scale * accs[0]
        return [x_out] + [_rms_rows(x_out, v) for v in vecs]
    return epi


def _norm_bwd_epilogue(accs, extras, vecs):
    return _rms_bwd_rows(extras[0], accs[0], vecs[0], extras[1])


def _loss_head(x, w, target, rows):
    def body(i, x_ref, t_ref, w_ref, dx_ref, dxb_ref, loss_ref, dw_ref):
        xv = x_ref[...]
        r = lax.rsqrt(jnp.mean(xv * xv, axis=1, keepdims=True) + EPS)
        xn = xv * r
        wv = w_ref[...]
        err = xn * wv - t_ref[...]
        part = 0.5 * jnp.sum(jnp.mean(err * err, axis=1, keepdims=True), axis=0, keepdims=True)
        loss_ref[...] += jnp.broadcast_to(part, (1, 128))
        dy = err * (1.0 / D_MODEL)
        dxn = dy * wv
        dx = r * (dxn - xn * jnp.mean(dxn * xn, axis=1, keepdims=True))
        dx_ref[...] = dx
        dxb_ref[...] = dx.astype(bf16)
        dw_ref[...] += jnp.sum(dy * xn, axis=0, keepdims=True)
    return _rt("loss_head", body, rows, ROW_TILE, [(x, D_MODEL, 0), (target, D_MODEL, 0)], [w],
               [(D_MODEL, f32), (D_MODEL, bf16)], acc_outs=[(1, 128), (1, D_MODEL)])


def _ffn_fwd(tag, x, hb, wg_t, wu_t, wd, next_nw, rows):
    def epi_up(accs, extras, vecs):
        a, b = accs
        return [a, b, a * _sigmoid(a) * b]
    a, b, s = _mm("ffn_up_" + tag, [hb], [wg_t, wu_t], [(0, 0, 0), (0, 1, 1)], 2, "nt", rows, D_FF, D_MODEL,
                  512, D_FF, D_MODEL, [bf16, bf16, bf16], epi_up, chunk=MXU_COLS)
    outs = _mm("ffn_down_" + tag, [s], [wd], [(0, 0, 0)], 1, "nn", rows, D_MODEL, D_FF,
               512, D_MODEL, D_FF, [f32] + [bf16] * len(next_nw), _residual_then_norm(0.5), extras=[x],
               vecs=next_nw)
    return outs[0], (outs[1] if next_nw else None), (x, hb, a, b, s)


def _ffn_bwd(tag, saved, nw, wg_t, wu_t, wd, dx, dxb, rows):
    x, hb, a, b, s = saved

    def epi_mid(accs, extras, vecs):
        ds = 0.5 * accs[0]
        av = extras[0].astype(f32)
        bv = extras[1].astype(f32)
        sg = _sigmoid(av)
        return [ds * bv * sg * (1.0 + av * (1.0 - sg)), ds * av * sg]
    da, db = _mm("ffn_bwd_mid_" + tag, [dxb], [wd], [(0, 0, 0)], 1, "nt", rows, D_FF, D_MODEL,
                 512, D_FF, D_MODEL, [bf16, bf16], epi_mid, extras=[a, b], chunk=MXU_COLS)
    d_wd = _mm1("ffn_dwd_" + tag, s, dxb, "tn", D_FF, D_MODEL, rows, FF_TILE, D_MODEL, TOKEN_K, out_dtype=bf16,
                scale=0.5)
    d_wg_t = _mm1("ffn_dwg_" + tag, da, hb, "tn", D_FF, D_MODEL, rows, FF_TILE, D_MODEL, TOKEN_K, out_dtype=bf16)
    d_wu_t = _mm1("ffn_dwu_" + tag, db, hb, "tn", D_FF, D_MODEL, rows, FF_TILE, D_MODEL, TOKEN_K, out_dtype=bf16)
    dx_in, dxb_in, d_nw = _mm("ffn_dh_" + tag, [da, db], [wg_t, wu_t], [(0, 0, 0), (1, 1, 0)], 1, "nn", rows,
                              D_MODEL, D_FF, 512, D_MODEL, D_FF, [f32, bf16], _norm_bwd_epilogue, extras=[x, dx],
                              vecs=[nw], n_part=1)
    return dx_in, dxb_in, jnp.sum(d_nw, axis=0), d_wg_t, d_wu_t, d_wd


S5_CB = 512
SUBLANES = 8
U_COL = GM_WIDTH // BRANCH


def _s5_scan_fwd(tag, proj, b_re, b_im, a_re, a_im, rows):
    tm = min(ROW_TILE, rows)
    nt = rows // tm
    nc = S5_LANES // S5_CB

    def kern(u_ref, bre_ref, bim_ref, ar_ref, ai_ref, xr_ref, xi_ref, pr_s, pi_s, cr_s, ci_s, mr_s, mi_s):
        t = pl.program_id(1)

        @pl.when(t == 0)
        def _():
            row8 = _rows((SUBLANES, S5_CB))
            pr = jnp.broadcast_to(ar_ref[...], (SUBLANES, S5_CB))
            pi = jnp.broadcast_to(ai_ref[...], (SUBLANES, S5_CB))
            s = 1
            while s < SUBLANES:
                sr = pltpu.roll(pr, s, 0)
                si = pltpu.roll(pi, s, 0)
                valid = row8 >= s
                pr, pi = jnp.where(valid, pr * sr - pi * si, pr), jnp.where(valid, pr * si + pi * sr, pi)
                s *= 2
            pr_s[...] = pr
            pi_s[...] = pi
            for k in range(3):
                s = 1 << k
                mr_s[k] = jnp.where(row8 >= s, pr[s - 1:s, :], 0.0)
                mi_s[k] = jnp.where(row8 >= s, pi[s - 1:s, :], 0.0)
            cr_s[...] = jnp.zeros_like(cr_s)
            ci_s[...] = jnp.zeros_like(ci_s)

        ub = u_ref[...].astype(bf16)
        br = _dot(ub, bre_ref[...])
        bi = _dot(ub, bim_ref[...])
        steps = [(mr_s[k], mi_s[k]) for k in range(3)]
        cr = cr_s[...]
        ci = ci_s[...]
        pr = pr_s[...]
        pi = pi_s[...]
        for g in range(tm // SUBLANES):
            sl = slice(g * SUBLANES, (g + 1) * SUBLANES)
            xr = br[sl]
            xi = bi[sl]
            for k, (mr, mi) in enumerate(steps):
                sr = pltpu.roll(xr, 1 << k, 0)
                si = pltpu.roll(xi, 1 << k, 0)
                xr, xi = xr + (mr * sr - mi * si), xi + (mr * si + mi * sr)
            xr, xi = xr + (pr * cr - pi * ci), xi + (pr * ci + pi * cr)
            xr_ref[sl, :] = xr
            xi_ref[sl, :] = xi
            cr = xr[SUBLANES - 1:SUBLANES, :]
            ci = xi[SUBLANES - 1:SUBLANES, :]
        cr_s[...] = cr
        ci_s[...] = ci

    return _call_with_rider(
        "s5_scan_fwd_" + tag, kern, (nc, nt),
        [pl.BlockSpec((tm, BRANCH), lambda c, t: (t, U_COL)),
         pl.BlockSpec((BRANCH, S5_CB), lambda c, t: (0, c)),
         pl.BlockSpec((BRANCH, S5_CB), lambda c, t: (0, c)),
         pl.BlockSpec((1, S5_CB), lambda c, t: (0, c)),
         pl.BlockSpec((1, S5_CB), lambda c, t: (0, c))],
        [pl.BlockSpec((tm, S5_CB), lambda c, t: (t, c))] * 2,
        [jax.ShapeDtypeStruct((rows, S5_LANES), f32)] * 2,
        [pltpu.VMEM((SUBLANES, S5_CB), f32), pltpu.VMEM((SUBLANES, S5_CB), f32),
         pltpu.VMEM((1, S5_CB), f32), pltpu.VMEM((1, S5_CB), f32),
         pltpu.VMEM((3, SUBLANES, S5_CB), f32), pltpu.VMEM((3, SUBLANES, S5_CB), f32)],
        ("parallel", "arbitrary"), (proj, b_re, b_im, a_re, a_im))


def _s5_scan_bwd(tag, dxr, dxi, xr, xi, a_re, a_im, rows):
    tm = min(ROW_TILE, rows)
    nt = rows // tm
    nc = S5_LANES // S5_CB

    def kern(dxr_ref, dxi_ref, xr_ref, xi_ref, ar_ref, ai_ref, gr_ref, gi_ref, dar_ref, dai_ref,
             qr_s, qi_s, cr_s, ci_s, gr_s, gi_s, mr_s, mi_s):
        t = pl.program_id(1)
        row = _rows((tm, S5_CB))
        ng = tm // SUBLANES

        @pl.when(t == 0)
        def _():
            row8 = _rows((SUBLANES, S5_CB))
            qr = jnp.broadcast_to(ar_ref[...], (SUBLANES, S5_CB))
            qi = jnp.broadcast_to(-ai_ref[...], (SUBLANES, S5_CB))
            s = 1
            while s < SUBLANES:
                sr = pltpu.roll(qr, SUBLANES - s, 0)
                si = pltpu.roll(qi, SUBLANES - s, 0)
                valid = row8 < SUBLANES - s
                qr, qi = jnp.where(valid, qr * sr - qi * si, qr), jnp.where(valid, qr * si + qi * sr, qi)
                s *= 2
            qr_s[...] = qr
            qi_s[...] = qi
            for k in range(3):
                s = 1 << k
                mr_s[k] = jnp.where(row8 < SUBLANES - s, qr[SUBLANES - s:SUBLANES - s + 1, :], 0.0)
                mi_s[k] = jnp.where(row8 < SUBLANES - s, qi[SUBLANES - s:SUBLANES - s + 1, :], 0.0)
            cr_s[...] = jnp.zeros_like(cr_s)
            ci_s[...] = jnp.zeros_like(ci_s)
            dar_ref[...] = jnp.zeros_like(dar_ref)
            dai_ref[...] = jnp.zeros_like(dai_ref)

        steps = [(mr_s[k], mi_s[k]) for k in range(3)]
        cr = cr_s[...]
        ci = ci_s[...]
        qr = qr_s[...]
        qi = qi_s[...]
        last8 = _rows((SUBLANES, S5_CB)) == SUBLANES - 1
        acc_r = jnp.zeros((SUBLANES, S5_CB), f32)
        acc_i = jnp.zeros((SUBLANES, S5_CB), f32)
        for g in reversed(range(ng)):
            sl = slice(g * SUBLANES, (g + 1) * SUBLANES)
            gr = dxr_ref[sl, :]
            gi = dxi_ref[sl, :]
            for k, (mr, mi) in enumerate(steps):
                sr = pltpu.roll(gr, SUBLANES - (1 << k), 0)
                si = pltpu.roll(gi, SUBLANES - (1 << k), 0)
                gr, gi = gr + (mr * sr - mi * si), gi + (mr * si + mi * sr)
            gr, gi = gr + (qr * cr - qi * ci), gi + (qr * ci + qi * cr)
            gr_s[sl, :] = gr
            gi_s[sl, :] = gi
            gnr = jnp.where(last8, cr, pltpu.roll(gr, SUBLANES - 1, 0))
            gni = jnp.where(last8, ci, pltpu.roll(gi, SUBLANES - 1, 0))
            xr_v = xr_ref[sl, :]
            xi_v = xi_ref[sl, :]
            acc_r = acc_r + (gnr * xr_v + gni * xi_v)
            acc_i = acc_i + (gni * xr_v - gnr * xi_v)
            cr = gr[0:1, :]
            ci = gi[0:1, :]
        cr_s[...] = cr
        ci_s[...] = ci
        gr_ref[...] = gr_s[...].astype(bf16)
        gi_ref[...] = gi_s[...].astype(bf16)
        dar_ref[...] += jnp.sum(acc_r, axis=0, keepdims=True)
        dai_ref[...] += jnp.sum(acc_i, axis=0, keepdims=True)

    rev = lambda c, t: (nt - 1 - t, c)
    return _call_with_rider(
        "s5_scan_bwd_" + tag, kern, (nc, nt),
        [pl.BlockSpec((tm, S5_CB), rev)] * 4 + [pl.BlockSpec((1, S5_CB), lambda c, t: (0, c))] * 2,
        [pl.BlockSpec((tm, S5_CB), rev)] * 2 + [pl.BlockSpec((1, S5_CB), lambda c, t: (0, c))] * 2,
        [jax.ShapeDtypeStruct((rows, S5_LANES), bf16)] * 2 + [jax.ShapeDtypeStruct((1, S5_LANES), f32)] * 2,
        [pltpu.VMEM((SUBLANES, S5_CB), f32), pltpu.VMEM((SUBLANES, S5_CB), f32),
         pltpu.VMEM((1, S5_CB), f32), pltpu.VMEM((1, S5_CB), f32),
         pltpu.VMEM((tm, S5_CB), f32), pltpu.VMEM((tm, S5_CB), f32),
         pltpu.VMEM((3, SUBLANES, S5_CB), f32), pltpu.VMEM((3, SUBLANES, S5_CB), f32)],
        ("parallel", "arbitrary"), (dxr, dxi, xr, xi, a_re, a_im))


def _s5_fwd(tag, proj, cst, rows):
    xr, xi = _s5_scan_fwd(tag, proj, cst["b_re"].astype(bf16), cst["b_im"].astype(bf16), cst["a_re"], cst["a_im"], rows)

    def body(i, xr_ref, xi_ref, u_ref, cre_ref, cim_ref, d_ref, gw_ref, gb_ref, y_ref, out_ref):
        y = (_dot(xr_ref[...].astype(bf16), cre_ref[...]) + _dot(xi_ref[...].astype(bf16), cim_ref[...])
             + d_ref[...] * u_ref[...])
        y_ref[...] = y
        z = _gelu(y)
        zg = _dot(z.astype(bf16), gw_ref[...]) + gb_ref[...]
        out_ref[...] = (z * _sigmoid(zg)).astype(bf16)

    y, out = _rt("s5_out_" + tag, body, rows, ROW_TILE,
                 [(xr, S5_LANES, 0), (xi, S5_LANES, 0), (proj, BRANCH, U_COL)],
                 [cst["c_re"].astype(bf16), cst["c_im"].astype(bf16), cst["s5_d"], cst["glu_w"], cst["glu_b"]],
                 [(BRANCH, f32), (BRANCH, bf16)])
    return out, (xr, xi, y)


def _s5_bwd(tag, saved, proj, cst, d_out, rows):
    xr, xi, y = saved
    c_re = cst["c_re"].astype(bf16)
    c_im = cst["c_im"].astype(bf16)

    def body(i, do_ref, y_ref, u_ref, xr_ref, xi_ref, cre_ref, cim_ref, gw_ref, gb_ref,
             dxr_ref, dxi_ref, dy_ref, dgw_ref, dgb_ref, dd_ref, dcre_ref, dcim_ref):
        yv = y_ref[...]
        z = _gelu(yv)
        zb = z.astype(bf16)
        gt = _sigmoid(_dot(zb, gw_ref[...]) + gb_ref[...])
        dov = do_ref[...]
        dzg = dov * z * gt * (1.0 - gt)
        dzgb = dzg.astype(bf16)
        dz = dov * gt + _dot_nt(dzgb, gw_ref[...])
        dgw_ref[...] += _dot_tn(zb, dzgb)
        dgb_ref[...] += jnp.sum(dzg, axis=0, keepdims=True)
        dy = dz * _gelu_grad(yv)
        dy_ref[...] = dy
        dd_ref[...] += jnp.sum(dy * u_ref[...], axis=0, keepdims=True)
        dyb = dy.astype(bf16)
        dxr_ref[...] = _dot_nt(dyb, cre_ref[...])
        dxi_ref[...] = _dot_nt(dyb, cim_ref[...])
        dcre_ref[...] += _dot_tn(xr_ref[...].astype(bf16), dyb)
        dcim_ref[...] += _dot_tn(xi_ref[...].astype(bf16), dyb)

    dxr, dxi, dy, d_gw, d_gb, d_d, d_cre, d_cim = _rt(
        "s5_out_bwd_" + tag, body, rows, ROW_TILE,
        [(d_out, BRANCH, 0), (y, BRANCH, 0), (proj, BRANCH, U_COL), (xr, S5_LANES, 0), (xi, S5_LANES, 0)],
        [c_re, c_im, cst["glu_w"], cst["glu_b"]],
        [(S5_LANES, f32), (S5_LANES, f32), (BRANCH, f32)],
        acc_outs=[(BRANCH, BRANCH), (1, BRANCH), (1, BRANCH), (S5_LANES, BRANCH), (S5_LANES, BRANCH)])

    gr, gi, d_ar, d_ai = _s5_scan_bwd(tag, dxr, dxi, xr, xi, cst["a_re"], cst["a_im"], rows)
    b_re = cst["b_re"].astype(bf16)
    b_im = cst["b_im"].astype(bf16)

    def body_in(i, gr_ref, gi_ref, dy_ref, u_ref, bre_ref, bim_ref, d_ref, du_ref, dbre_ref, dbim_ref):
        grv = gr_ref[...]
        giv = gi_ref[...]
        du = _dot_nt(grv, bre_ref[...]) + _dot_nt(giv, bim_ref[...]) + dy_ref[...] * d_ref[...]
        du_ref[...] = du.astype(bf16)
        ub = u_ref[...].astype(bf16)
        dbre_ref[...] += _dot_tn(ub, grv)
        dbim_ref[...] += _dot_tn(ub, giv)

    du, d_bre, d_bim = _rt("s5_in_bwd_" + tag, body_in, rows, ROW_TILE,
                           [(gr, S5_LANES, 0), (gi, S5_LANES, 0), (dy, BRANCH, 0), (proj, BRANCH, U_COL)],
                           [b_re, b_im, cst["s5_d"]], [(BRANCH, bf16)],
                           acc_outs=[(BRANCH, S5_LANES), (BRANCH, S5_LANES)])
    dcst = {"b_re": d_bre, "b_im": d_bim, "a_re": d_ar, "a_im": d_ai, "c_re": d_cre, "c_im": d_cim,
            "s5_d": d_d, "glu_b": d_gb}
    return du, dcst, d_gw


def _hg_prep(q, z, lb):
    qs = _sigmoid(q)
    qh = q * qs
    sg = _sigmoid_small(z)
    fg = lb + (1.0 - lb) * sg
    kk = (1.0 - lb) * (1.0 - sg)
    return qs, qh, sg, fg, kk


def _hg_fwd(tag, proj, cst, rows):
    tm = min(ROW_TILE, rows)
    c_sz = HG_CHUNK
    nch = tm // c_sz
    n_chunks = rows // c_sz

    def body(i, q_ref, z_ref, v_ref, g_ref, lb_ref, nw_ref, out_ref, o_ref, ss_ref, sn_ref, st_s):
        @pl.when(i == 0)
        def _():
            st_s[...] = jnp.zeros_like(st_s)

        lb = lb_ref[...]
        _, qh, sg, fg, kk = _hg_prep(q_ref[...], z_ref[...], lb)
        b = _seg_cumsum(jnp.log(fg), tm, c_sz)
        qhat = (qh * jnp.exp(b)).astype(bf16)
        khat = (kk * jnp.exp(-b)).astype(bf16)
        vb = v_ref[...].astype(bf16)
        b3 = b.reshape(nch, c_sz, BRANCH)
        bl3 = b3[:, c_sz - 1:c_sz, :]
        kdec = (kk.reshape(nch, c_sz, BRANCH) * jnp.exp(bl3 - b3)).astype(bf16)
        ebl = jnp.exp(bl3)
        tril = (lax.broadcasted_iota(jnp.int32, (nch, c_sz, c_sz), 1)
                >= lax.broadcasted_iota(jnp.int32, (nch, c_sz, c_sz), 2))
        o_heads = []
        for h in range(HG_HEADS):
            hl = slice(h * HG_DK, (h + 1) * HG_DK)
            q3 = qhat[:, hl].reshape(nch, c_sz, HG_DK)
            k3 = khat[:, hl].reshape(nch, c_sz, HG_DK)
            v3 = vb[:, hl].reshape(nch, c_sz, HG_DK)
            a_mat = jnp.where(tril, _bdot_nt(q3, k3), 0.0).astype(bf16)
            o3 = _bdot(a_mat, v3)
            st = st_s[hl, :]
            before = []
            for ci in range(nch):
                before.append(st.astype(bf16))
                st = st * ebl[ci][:, hl] + _dot_tn(v3[ci], kdec[ci][:, hl])
                sn_ref[ci, hl, :] = st.astype(bf16)
            st_s[hl, :] = st
            s3 = jnp.stack(before)
            ss_ref[:, hl, :] = s3
            o3 = o3 + _bdot_nt(q3, s3)
            o_heads.append(o3.reshape(tm, HG_DK))
        o = jnp.concatenate(o_heads, axis=1)
        o_ref[...] = o
        r = lax.rsqrt(_head_mean(o * o) + EPS)
        g = g_ref[...]
        out_ref[...] = (o * r * nw_ref[...] * (g * _sigmoid(g))).astype(bf16)

    out, o, ss, sn = _rt(
        "hg_fwd_" + tag, body, rows, tm,
        [(proj, BRANCH, U_COL + 1), (proj, BRANCH, U_COL + 2), (proj, BRANCH, U_COL + 3), (proj, BRANCH, U_COL + 4)],
        [cst["hg_lb"], cst["hg_nw"]],
        [(BRANCH, bf16), (BRANCH, f32),
         ((n_chunks, BRANCH, HG_DK), (nch, BRANCH, HG_DK), lambda t: (t, 0, 0), bf16),
         ((n_chunks, BRANCH, HG_DK), (nch, BRANCH, HG_DK), lambda t: (t, 0, 0), bf16)],
        scratch=[pltpu.VMEM((BRANCH, HG_DK), f32)])
    return out, (o, ss, sn)


def _hg_bwd(tag, saved, proj, cst, d_out, rows):
    o_saved, ss, sn = saved
    tm = min(ROW_TILE, rows)
    c_sz = HG_CHUNK
    nch = tm // c_sz

    def body(i, do_ref, q_ref, z_ref, v_ref, g_ref, o_ref, ss_ref, sn_ref, lb_ref, nw_ref,
             dq_ref, dz_ref, dv_ref, dg_ref, dlb_ref, dnw_ref, dst_s):
        @pl.when(i == 0)
        def _():
            dst_s[...] = jnp.zeros_like(dst_s)

        lb = lb_ref[...]
        q = q_ref[...]
        qs, qh, sg, fg, kk = _hg_prep(q, z_ref[...], lb)
        b = _seg_cumsum(jnp.log(fg), tm, c_sz)
        eb = jnp.exp(b)
        enb = jnp.exp(-b)
        qhat = (qh * eb).astype(bf16)
        khat = (kk * enb).astype(bf16)
        vb = v_ref[...].astype(bf16)
        b3 = b.reshape(nch, c_sz, BRANCH)
        bl3 = b3[:, c_sz - 1:c_sz, :]
        dec3 = jnp.exp(bl3 - b3)
        kdec = (kk.reshape(nch, c_sz, BRANCH) * dec3).astype(bf16)
        ebl = jnp.exp(bl3)
        g = g_ref[...]
        gs = _sigmoid(g)
        o = o_ref[...]
        r = lax.rsqrt(_head_mean(o * o) + EPS)
        oh = o * r
        nw = nw_ref[...]
        dov = do_ref[...]
        don = dov * (g * gs)
        dg_ref[...] = (dov * oh * nw * (gs * (1.0 + g * (1.0 - gs)))).astype(bf16)
        dnw_ref[...] += jnp.sum(don * oh, axis=0, keepdims=True)
        doh = don * nw
        d_o = r * (doh - oh * _head_mean(doh * oh))
        dob = d_o.astype(bf16)
        t_idx = lax.broadcasted_iota(jnp.int32, (nch, c_sz, c_sz), 1)
        s_idx = lax.broadcasted_iota(jnp.int32, (nch, c_sz, c_sz), 2)
        heads = []
        for h in range(HG_HEADS):
            hl = slice(h * HG_DK, (h + 1) * HG_DK)
            q3 = qhat[:, hl].reshape(nch, c_sz, HG_DK)
            k3 = khat[:, hl].reshape(nch, c_sz, HG_DK)
            v3 = vb[:, hl].reshape(nch, c_sz, HG_DK)
            do3 = dob[:, hl].reshape(nch, c_sz, HG_DK)
            s3 = ss_ref[:, hl, :]
            da_mat = jnp.where(t_idx >= s_idx, _bdot_nt(do3, v3), 0.0).astype(bf16)
            a_t = jnp.where(t_idx <= s_idx, _bdot_nt(k3, q3), 0.0).astype(bf16)
            da_t = jnp.where(t_idx <= s_idx, _bdot_nt(v3, do3), 0.0).astype(bf16)
            dqhat = _bdot(do3, s3) + _bdot(da_mat, k3)
            dkhat = _bdot(da_t, q3)
            dst = dst_s[hl, :]
            after = [None] * nch
            for ci in reversed(range(nch)):
                after[ci] = dst
                dst = dst * ebl[ci][:, hl] + _dot_tn(do3[ci], q3[ci])
            dst_s[hl, :] = dst
            ds3 = jnp.stack(after)
            ds3b = ds3.astype(bf16)
            dk_inter = _bdot(v3, ds3b) * dec3[:, :, hl]
            dv3 = _bdot(a_t, do3) + _bdot_nt(kdec[:, :, hl], ds3b)
            flux = jnp.sum(sn_ref[:, hl, :].astype(f32) * ds3, axis=1, keepdims=True)
            heads.append((dqhat.reshape(tm, HG_DK), dkhat.reshape(tm, HG_DK), dk_inter.reshape(tm, HG_DK),
                          dv3.reshape(tm, HG_DK), jnp.broadcast_to(flux, (nch, c_sz, HG_DK)).reshape(tm, HG_DK)))
        dqhat, dkhat, dk_inter, dv, flux = (jnp.concatenate(parts, axis=1) for parts in zip(*heads))
        dv_ref[...] = dv.astype(bf16)
        dqh = dqhat * eb
        dk = dkhat * enb + dk_inter
        db = qhat.astype(f32) * dqhat - khat.astype(f32) * dkhat - kk * dk_inter
        dlf = _seg_rev_cumsum(db, tm, c_sz) + flux
        tt = (1.0 - lb) * sg * (1.0 - sg)
        dz_ref[...] = (dlf * tt / fg - dk * tt).astype(bf16)
        dlb_ref[...] += jnp.sum(dlf * (1.0 - sg) / fg - dk * (1.0 - sg), axis=0, keepdims=True)
        dq_ref[...] = (dqh * (qs * (1.0 + q * (1.0 - qs)))).astype(bf16)

    dq, dz, dv, dg, d_lb, d_nw = _rt(
        "hg_bwd_" + tag, body, rows, tm,
        [(d_out, BRANCH, 0), (proj, BRANCH, U_COL + 1), (proj, BRANCH, U_COL + 2), (proj, BRANCH, U_COL + 3),
         (proj, BRANCH, U_COL + 4), (o_saved, BRANCH, 0), (ss, (nch, BRANCH, HG_DK), lambda t: (t, 0, 0)),
         (sn, (nch, BRANCH, HG_DK), lambda t: (t, 0, 0))],
        [cst["hg_lb"], cst["hg_nw"]],
        [(BRANCH, bf16)] * 4, acc_outs=[(1, BRANCH), (1, BRANCH)],
        scratch=[pltpu.VMEM((BRANCH, HG_DK), f32)],
        reverse=True)
    return dq, dz, dv, dg, {"hg_lb": d_lb, "hg_nw": d_nw}


def _rg_gates(xc, wa_ref, ba_ref, wx_ref, bx_ref, sp8):
    xcb = xc.astype(bf16)
    r = _sigmoid(_dot(xcb, wa_ref[...]) + ba_ref[...])
    ig = _sigmoid(_dot(xcb, wx_ref[...]) + bx_ref[...])
    la = -sp8 * r
    a = jnp.exp(la)
    mult = jnp.sqrt(-_expm1(2.0 * la))
    return xcb, r, ig, a, mult


def _rg_fwd(tag, proj, cst, rows):
    tm = min(ROW_TILE, rows)

    def body(i, xb_ref, gate_ref, cw_ref, cb_ref, wa_ref, ba_ref, wx_ref, bx_ref, sp_ref,
             out_ref, xc_ref, h_ref, hp_ref, prev_s, hc_s):
        @pl.when(i == 0)
        def _():
            prev_s[...] = jnp.zeros_like(prev_s)
            hc_s[...] = jnp.zeros_like(hc_s)

        row = _rows((tm, BRANCH))
        xb = xb_ref[...]
        prev = prev_s[...]
        xc = cb_ref[...] + cw_ref[3:4, :] * xb
        for j in range(1, 4):
            sh = jnp.where(row >= j, pltpu.roll(xb, j, 0), pltpu.roll(prev, j, 0))
            xc = xc + cw_ref[3 - j:4 - j, :] * sh
        prev_s[...] = xb
        xc_ref[...] = xc
        _, r, ig, a, mult = _rg_gates(xc, wa_ref, ba_ref, wx_ref, bx_ref, sp_ref[...])
        bb = mult * ig * xc
        hc = hc_s[...]
        row8 = _rows((SUBLANES, BRANCH))
        for g in range(tm // SUBLANES):
            sl = slice(g * SUBLANES, (g + 1) * SUBLANES)
            a_cum, h_loc = _scan_fwd(a[sl], bb[sl], SUBLANES)
            h = h_loc + a_cum * hc
            h_ref[sl, :] = h
            hp_ref[sl, :] = jnp.where(row8 >= 1, pltpu.roll(h, 1, 0), hc)
            hc = h[SUBLANES - 1:SUBLANES, :]
        hc_s[...] = hc
        out_ref[...] = (h_ref[...] * _gelu(gate_ref[...])).astype(bf16)

    out, xc, h, hp = _rt(
        "rg_fwd_" + tag, body, rows, tm,
        [(proj, BRANCH, U_COL + 5), (proj, BRANCH, U_COL + 6)],
        [cst["rg_cw"], cst["rg_cb"], cst["rg_wa"].astype(bf16), cst["rg_ba"], cst["rg_wx"].astype(bf16),
         cst["rg_bx"], cst["rg_sp8"]],
        [(BRANCH, bf16), (BRANCH, f32), (BRANCH, f32), (BRANCH, f32)],
        scratch=[pltpu.VMEM((tm, BRANCH), f32), pltpu.VMEM((1, BRANCH), f32)])
    return out, (xc, h, hp)


def _rg_bwd(tag, saved, proj, cst, d_out, rows):
    xc_saved, h_saved, hp_saved = saved
    tm = min(ROW_TILE, rows)

    def body(i, do_ref, xb_ref, gate_ref, xc_ref, h_ref, hp_ref, cw_ref, wa_ref, ba_ref, wx_ref, bx_ref, sp_ref,
             dxb_ref, dgate_ref, dcw_ref, dcb_ref, dwa_ref, dba_ref, dwx_ref, dbx_ref, dsp_ref,
             nxt_s, ec_s, gt_s):
        @pl.when(i == 0)
        def _():
            nxt_s[...] = jnp.zeros_like(nxt_s)
            ec_s[...] = jnp.zeros_like(ec_s)

        row = _rows((tm, BRANCH))
        xc = xc_ref[...]
        sp8 = sp_ref[...]
        xcb, r, ig, a, mult = _rg_gates(xc, wa_ref, ba_ref, wx_ref, bx_ref, sp8)
        gate = gate_ref[...]
        dov = do_ref[...]
        dh = dov * _gelu(gate)
        dgate_ref[...] = (dov * h_ref[...] * _gelu_grad(gate)).astype(bf16)
        adh = a * dh
        ec = ec_s[...]
        last8 = _rows((SUBLANES, BRANCH)) == SUBLANES - 1
        for g in reversed(range(tm // SUBLANES)):
            sl = slice(g * SUBLANES, (g + 1) * SUBLANES)
            a_cum, e_loc = _scan_bwd(a[sl], adh[sl], SUBLANES)
            e = e_loc + a_cum * ec
            gt_s[sl, :] = dh[sl] + jnp.where(last8, ec, pltpu.roll(e, SUBLANES - 1, 0))
            ec = e[0:1, :]
        ec_s[...] = ec
        g_tot = gt_s[...]
        d_a = g_tot * hp_ref[...]
        d_mult = g_tot * ig * xc
        d_ix = g_tot * mult
        d_ig = d_ix * xc
        d_xc = d_ix * ig
        d_la = d_a * a - d_mult * (a * a) / mult
        d_r = -d_la * sp8
        dsp_ref[...] += jnp.sum(-d_la * r, axis=0, keepdims=True)
        dzr = d_r * r * (1.0 - r)
        dzi = d_ig * ig * (1.0 - ig)
        dzrb = dzr.astype(bf16)
        dzib = dzi.astype(bf16)
        d_xc = d_xc + _dot_nt(dzrb, wa_ref[...]) + _dot_nt(dzib, wx_ref[...])
        dwa_ref[...] += _dot_tn(xcb, dzrb)
        dwx_ref[...] += _dot_tn(xcb, dzib)
        dba_ref[...] += jnp.sum(dzr, axis=0, keepdims=True)
        dbx_ref[...] += jnp.sum(dzi, axis=0, keepdims=True)
        dcb_ref[...] += jnp.sum(d_xc, axis=0, keepdims=True)
        nxt = nxt_s[...]
        xb = xb_ref[...]
        dxb = cw_ref[3:4, :] * d_xc
        dcw_ref[3:4, :] += jnp.sum(d_xc * xb, axis=0, keepdims=True)
        for j in range(1, 4):
            sh = jnp.where(row < tm - j, pltpu.roll(d_xc, tm - j, 0), pltpu.roll(nxt, tm - j, 0))
            dxb = dxb + cw_ref[3 - j:4 - j, :] * sh
            dcw_ref[3 - j:4 - j, :] += jnp.sum(sh * xb, axis=0, keepdims=True)
        nxt_s[...] = d_xc
        dxb_ref[...] = dxb.astype(bf16)

    wa = cst["rg_wa"].astype(bf16)
    wx = cst["rg_wx"].astype(bf16)
    dxb, dgate, d_cw, d_cb, d_wa, d_ba, d_wx, d_bx, d_sp = _rt(
        "rg_bwd_" + tag, body, rows, tm,
        [(d_out, BRANCH, 0), (proj, BRANCH, U_COL + 5), (proj, BRANCH, U_COL + 6), (xc_saved, BRANCH, 0),
         (h_saved, BRANCH, 0), (hp_saved, BRANCH, 0)],
        [cst["rg_cw"], wa, cst["rg_ba"], wx, cst["rg_bx"], cst["rg_sp8"]],
        [(BRANCH, bf16), (BRANCH, bf16)],
        acc_outs=[(4, BRANCH), (1, BRANCH), (BRANCH, BRANCH), (1, BRANCH), (BRANCH, BRANCH), (1, BRANCH), (1, BRANCH)],
        scratch=[pltpu.VMEM((tm, BRANCH), f32), pltpu.VMEM((1, BRANCH), f32), pltpu.VMEM((tm, BRANCH), f32)],
        reverse=True)
    dcst = {"rg_cw": d_cw, "rg_cb": d_cb, "rg_wa": d_wa, "rg_ba": d_ba, "rg_wx": d_wx, "rg_bx": d_bx, "rg_sp8": d_sp}
    return dxb, dgate, dcst


def _merge_fwd(tag, proj, outs, bp, rows):
    def body(i, ya_ref, yb_ref, yc_ref, gm_ref, p_ref, m_ref):
        acc = None
        for n, y_ref in enumerate((ya_ref, yb_ref, yc_ref)):
            up = _dot(y_ref[...], p_ref[n])
            term = _sigmoid(gm_ref[:, n * D_MODEL:(n + 1) * D_MODEL]) * up
            acc = term if acc is None else acc + term
        m_ref[...] = acc.astype(bf16)
    return _rt("merge_fwd_" + tag, body, rows, ROW_TILE,
               [(outs[0], BRANCH, 0), (outs[1], BRANCH, 0), (outs[2], BRANCH, 0), (proj, GM_WIDTH, 0)],
               [bp], [(D_MODEL, bf16)])[0]


def _merge_bwd(tag, proj, outs, bp, dmerged, rows):
    def body(i, dm_ref, ya_ref, yb_ref, yc_ref, gm_ref, p_ref, da_ref, db_ref, dc_ref, dgm_ref, dp_ref):
        dm = dm_ref[...]
        for n, (y_ref, dy_ref) in enumerate(((ya_ref, da_ref), (yb_ref, db_ref), (yc_ref, dc_ref))):
            yv = y_ref[...]
            up = _dot(yv, p_ref[n])
            gt = _sigmoid(gm_ref[:, n * D_MODEL:(n + 1) * D_MODEL])
            dup = (dm * gt).astype(bf16)
            dgm_ref[:, n * D_MODEL:(n + 1) * D_MODEL] = (dm * up * gt * (1.0 - gt)).astype(bf16)
            dy_ref[...] = _dot_nt(dup, p_ref[n])
            dp_ref[n] += _dot_tn(yv, dup)
    return _rt("merge_bwd_" + tag, body, rows, ROW_TILE,
               [(dmerged, D_MODEL, 0), (outs[0], BRANCH, 0), (outs[1], BRANCH, 0), (outs[2], BRANCH, 0),
                (proj, GM_WIDTH, 0)],
               [bp], [(BRANCH, f32), (BRANCH, f32), (BRANCH, f32), (GM_WIDTH, bf16)],
               acc_outs=[(N_BRANCH, BRANCH, D_MODEL)])


def _block_diag(blocks):
    g, r, c = blocks.shape
    on_diag = (lax.broadcasted_iota(jnp.int32, (g * r, g * c), 0) // r
               == lax.broadcasted_iota(jnp.int32, (g * r, g * c), 1) // c)
    tiled = jnp.broadcast_to(blocks.reshape(g * r, 1, c), (g * r, g, c)).reshape(g * r, g * c)
    return jnp.where(on_diag, tiled, 0.0)


def _prep_consts(sp):
    p = jax.nn.softmax(sp["hg_lb_logits"], axis=0)
    lower = jnp.cumsum(p, axis=0) - p[0]
    out = []
    for l in range(DEPTH):
        lr = jnp.minimum(sp["s5_lambda_re"][l], S5_EIG_MAX)
        li = sp["s5_lambda_im"][l]
        dt = jnp.exp(sp["s5_log_dt"][l])[:, None]
        mag = jnp.exp(lr * dt)
        ar = mag * jnp.cos(li * dt)
        ai = mag * jnp.sin(li * dt)
        den = lr * lr + li * li
        fr = ((ar - 1.0) * lr + ai * li) / den
        fi = (ai * lr - (ar - 1.0) * li) / den
        br, bi = sp["s5_b_re"][l], sp["s5_b_im"][l]
        bbr = fr[..., None] * br - fi[..., None] * bi
        bbi = fr[..., None] * bi + fi[..., None] * br
        c = {
            "a_re": ar.reshape(1, S5_LANES), "a_im": ai.reshape(1, S5_LANES),
            "b_re": _block_diag(bbr.transpose(0, 2, 1)), "b_im": _block_diag(bbi.transpose(0, 2, 1)),
            "c_re": _block_diag(sp["s5_c_re"][l].transpose(0, 2, 1)),
            "c_im": -_block_diag(sp["s5_c_im"][l].transpose(0, 2, 1)),
            "s5_d": sp["s5_d"][l][None], "glu_b": sp["s5_glu_b"][l][None],
            "hg_lb": lower[l][None], "hg_nw": sp["hg_norm_w"][l][None],
            "rg_cw": sp["rg_conv_w"][l], "rg_cb": sp["rg_conv_b"][l][None],
            "rg_wa": _block_diag(sp["rg_wa"][l]), "rg_ba": sp["rg_ba"][l][None],
            "rg_wx": _block_diag(sp["rg_wx"][l]), "rg_bx": sp["rg_bx"][l][None],
            "rg_sp8": (RG_C * jax.nn.softplus(-sp["rg_lambda"][l]))[None],
        }
        out.append(c)
    return out


def _mixer_fwd(tag, x, hb, w_in, bp, w_out, cst, next_nw, rows):
    proj = _mm1("mix_proj_" + tag, hb, w_in, "nt", rows, IN_TOTAL, D_MODEL, 512, IN_TOTAL // 4, D_MODEL)
    cst = dict(cst)
    out_a, sv_a = _s5_fwd(tag, proj, cst, rows)
    out_b, sv_b = _hg_fwd(tag, proj, cst, rows)
    out_c, sv_c = _rg_fwd(tag, proj, cst, rows)
    merged = _merge_fwd(tag, proj, (out_a, out_b, out_c), bp, rows)
    x_out, hb_out = _mm("mix_out_" + tag, [merged], [w_out], [(0, 0, 0)], 1, "nn", rows, D_MODEL, D_MODEL,
                        512, D_MODEL, D_MODEL, [f32, bf16], _residual_then_norm(1.0), extras=[x], vecs=[next_nw])
    return x_out, hb_out, (x, hb, proj, (out_a, out_b, out_c), merged, sv_a, sv_b, sv_c)


def _mixer_bwd(tag, saved, nw, w_in, bp, w_out, cst, dx, dxb, rows):
    x, hb, proj, outs, merged, sv_a, sv_b, sv_c = saved
    d_wout = _mm1("mix_dwout_" + tag, merged, dxb, "tn", D_MODEL, D_MODEL, rows, D_MODEL, D_MODEL, TOKEN_K,
                  out_dtype=bf16)
    dmerged = _mm1("mix_dmerged_" + tag, dxb, w_out, "nt", rows, D_MODEL, D_MODEL, 512, D_MODEL, D_MODEL)
    d_a, d_b, d_c, dgm, d_bp = _merge_bwd(tag, proj, outs, bp, dmerged, rows)
    dxbc, dgatec, dcst_c = _rg_bwd(tag, sv_c, proj, cst, d_c, rows)
    dq, dz, dv, dg, dcst_b = _hg_bwd(tag, sv_b, proj, cst, d_b, rows)
    du, dcst_a, d_glu_w = _s5_bwd(tag, sv_a, proj, cst, d_a, rows)
    dproj = jnp.concatenate([dgm, du, dq, dz, dv, dg, dxbc, dgatec], axis=1)
    d_win = _mm1("mix_dwin_" + tag, dproj, hb, "tn", IN_TOTAL, D_MODEL, rows, IN_TOTAL // 4, D_MODEL, TOKEN_K // 2,
                 out_dtype=bf16)
    dx_in, dxb_in, d_nw = _mm("mix_dh_" + tag, [dproj], [w_in], [(0, 0, 0)], 1, "nn", rows, D_MODEL, IN_TOTAL,
                              512, D_MODEL, IN_TOTAL // 2, [f32, bf16], _norm_bwd_epilogue, extras=[x, dx], vecs=[nw],
                              n_part=1)
    dcst = {**dcst_a, **dcst_b, **dcst_c}
    return dx_in, dxb_in, jnp.sum(d_nw, axis=0), d_win, d_bp, d_wout, d_glu_w, dcst


def _local_step(x, target, weights_of, small, grads_done):
    rows = x.shape[0]
    consts, consts_vjp = jax.vjp(_prep_consts, small)
    norm_w = small["norm_w"]
    saved = []
    h = x
    hb = _rms_fwd("first_norm", x, norm_w[0, 0][None], rows)
    for l in range(DEPTH):
        t = str(l)
        after = [norm_w[l + 1, 0][None]] if l + 1 < DEPTH else []
        wa = weights_of(l, "a")
        h, hb, sv0 = _ffn_fwd(t + "a", h, hb, *wa, [norm_w[l, 1][None]], rows)
        wm = weights_of(l, "mix")
        cst = dict(consts[l])
        cst["glu_w"] = wm[3]
        h, hb, sv1 = _mixer_fwd(t, h, hb, *wm[:3], cst, norm_w[l, 2][None], rows)
        wb = weights_of(l, "b")
        h, hb, sv2 = _ffn_fwd(t + "b", h, hb, *wb, after, rows)
        saved.append((sv0, sv1, sv2, cst, wa, wm, wb))
    dx, dxb, loss, d_fnw = _loss_head(h, small["final_norm_w"][None], target, rows)
    d_norm = [None] * DEPTH
    d_consts = [None] * DEPTH
    for l in reversed(range(DEPTH)):
        t = str(l)
        sv0, sv1, sv2, cst, wa, wm, wb = saved[l]
        dx, dxb, dn2, dg1, du1, dd1 = _ffn_bwd(t + "b", sv2, norm_w[l, 2][None], *wb, dx, dxb, rows)
        grads_done(l, "b", [dg1, du1, dd1])
        dx, dxb, dn1, d_win, d_bp, d_wout, d_glu_w, dcst = _mixer_bwd(
            t, sv1, norm_w[l, 1][None], *wm[:3], cst, dx, dxb, rows)
        grads_done(l, "mix", [d_win, d_bp, d_wout, d_glu_w])
        dx, dxb, dn0, dg0, du0, dd0 = _ffn_bwd(t + "a", sv0, norm_w[l, 0][None], *wa, dx, dxb, rows)
        grads_done(l, "a", [dg0, du0, dd0])
        d_norm[l] = jnp.concatenate([dn0, dn1, dn2], axis=0)
        d_consts[l] = dcst
    (g_small,) = consts_vjp(d_consts)
    g_small = dict(g_small)
    g_small["norm_w"] = g_small["norm_w"] + jnp.stack(d_norm)
    g_small["final_norm_w"] = g_small["final_norm_w"] + d_fnw[0]
    return loss[0, 0], dx, g_small


def _other_chips(x, y):
    return [(1 - x, y), (x, 1 - y), (1 - x, 1 - y)]


def _gather_over_ici(shards):
    n = len(shards)

    def copies(in_refs, out_refs, send_sems, recv_sems):
        x, y, c = _place()
        cps = []
        for i in range(n):
            mine = out_refs[i].at[4 * x + 2 * y + c]
            cps.append(pltpu.make_async_copy(in_refs[i], mine, send_sems.at[5 * i + 4]))
            for k, to in enumerate([(x, y, 1 - c)] + [(px, py, c) for px, py in _other_chips(x, y)]):
                cps.append(pltpu.make_async_remote_copy(
                    src_ref=in_refs[i], dst_ref=mine, send_sem=send_sems.at[5 * i + k],
                    recv_sem=recv_sems.at[5 * i + k], device_id=to, device_id_type=MESH_IDS))
        return cps

    return _Carry(shards, [jax.ShapeDtypeStruct((N_DEV,) + s.shape, s.dtype) for s in shards], 5 * n, copies)


def _gather_forward(name, landings):
    n = len(landings)

    def body(*refs):
        in_refs, out_refs = refs[:n], refs[n:2 * n]
        send_sems, recv_sems = refs[2 * n:]
        x, y, c = _place()
        cps = []
        for i in range(n):
            for j, (px, py) in enumerate(_other_chips(x, y)):
                block = 4 * px + 2 * py + c
                cps.append(pltpu.make_async_remote_copy(
                    src_ref=in_refs[i].at[block], dst_ref=out_refs[i].at[block], send_sem=send_sems.at[3 * i + j],
                    recv_sem=recv_sems.at[3 * i + j], device_id=(x, y, 1 - c), device_id_type=MESH_IDS))
        for cp in cps:
            cp.start()
        for cp in cps:
            cp.wait()

    return pl.pallas_call(
        body, name=name, out_shape=[jax.ShapeDtypeStruct(a.shape, a.dtype) for a in landings],
        in_specs=[ANY_SPEC] * n, out_specs=[ANY_SPEC] * n, input_output_aliases={i: i for i in range(n)},
        scratch_shapes=[pltpu.SemaphoreType.DMA((3 * n,)), pltpu.SemaphoreType.DMA((3 * n,))],
    )(*landings)


def _all_gather(name, shards):
    n = len(shards)

    def body(*refs):
        x_refs, out_refs = refs[:n], refs[n:2 * n]
        send_sems, recv_sems, local_sems = refs[2 * n:]
        x, y, c = _place()
        me, sibling = (x, y, c), (x, y, 1 - c)
        chips = [(1 - x, y), (x, 1 - y), (1 - x, 1 - y)]

        def blk(i, px, py, pc):
            return out_refs[i].at[4 * px + 2 * py + pc]

        def copy(i, k, block, to, src=None):
            return pltpu.make_async_remote_copy(
                src_ref=blk(i, *block) if src is None else src, dst_ref=blk(i, *block),
                send_sem=send_sems.at[7 * i + k], recv_sem=recv_sems.at[7 * i + k], device_id=to,
                device_id_type=MESH_IDS)

        mine = [pltpu.make_async_copy(x_refs[i], blk(i, *me), local_sems.at[i]) for i in range(n)]
        for cp in mine:
            cp.start()
        first = []
        for i in range(n):
            first.append(copy(i, 0, me, sibling, src=x_refs[i]))
            first += [copy(i, 1 + j, me, (*chip, c), src=x_refs[i]) for j, chip in enumerate(chips)]
        for cp in first:
            cp.start()
        passed = []
        for j, chip in enumerate(chips):
            for i in range(n):
                copy(i, 1 + j, (*chip, c), me).wait_recv()
                fwd = copy(i, 4 + j, (*chip, c), sibling)
                fwd.start()
                passed.append(fwd)
        for i in range(n):
            copy(i, 0, sibling, me).wait_recv()
            for j, chip in enumerate(chips):
                copy(i, 4 + j, (*chip, 1 - c), me).wait_recv()
        for cp in first + passed:
            cp.wait_send()
        for cp in mine:
            cp.wait()

    return pl.pallas_call(
        body, name=name, out_shape=[jax.ShapeDtypeStruct((N_DEV,) + s.shape, s.dtype) for s in shards],
        in_specs=[ANY_SPEC] * n, out_specs=[ANY_SPEC] * n,
        scratch_shapes=[pltpu.SemaphoreType.DMA((7 * n,)), pltpu.SemaphoreType.DMA((7 * n,)),
                        pltpu.SemaphoreType.DMA((n,))],
    )(*shards)


def _row_tile(rows):
    return rows if rows <= 512 else next(t for t in range(512, 7, -8) if rows % t == 0)


def _sums_over_ici(chip_sums):
    n = len(chip_sums)

    def copies(in_refs, out_refs, send_sems, recv_sems):
        x, y, c = _place()
        return [pltpu.make_async_remote_copy(
            src_ref=in_refs[i].at[2 * px + py], dst_ref=out_refs[i].at[k], send_sem=send_sems.at[3 * i + k],
            recv_sem=recv_sems.at[3 * i + k], device_id=(px, py, c), device_id_type=MESH_IDS)
            for i in range(n) for k, (px, py) in enumerate(_other_chips(x, y))]

    return _Carry(chip_sums, [jax.ShapeDtypeStruct((3,) + t.shape[1:], t.dtype) for t in chip_sums], 3 * n, copies)


def _reduce_scatter(tag, parts, hosts=None):
    n = len(parts)
    _, _, c = _place()

    def body_pair(*refs):
        p_refs, got_refs = refs[:n], refs[n:2 * n]
        send_sems, recv_sems = refs[2 * n:]
        x, y, c = _place()
        cps = [pltpu.make_async_remote_copy(
            src_ref=p_refs[i].at[:, 1 - c], dst_ref=got_refs[i], send_sem=send_sems.at[i], recv_sem=recv_sems.at[i],
            device_id=(x, y, 1 - c), device_id_type=MESH_IDS) for i in range(n)]
        for cp in cps:
            cp.start()
        for cp in cps:
            cp.wait()

    from_sibling = pl.pallas_call(
        body_pair, name="rs_pair_" + tag,
        out_shape=[jax.ShapeDtypeStruct((4,) + p.shape[2:], p.dtype) for p in parts],
        in_specs=[ANY_SPEC] * n, out_specs=[ANY_SPEC] * n,
        scratch_shapes=[pltpu.SemaphoreType.DMA((n,)), pltpu.SemaphoreType.DMA((n,))],
    )(*parts)

    def body_add(idx_ref, *refs):
        p = pl.program_id(0)
        for q in range(n):
            @pl.when(p == q)
            def _(p_ref=refs[q], g_ref=refs[n + q], o_ref=refs[2 * n + q]):
                o_ref[...] = (p_ref[...].astype(f32) + g_ref[...].astype(f32)).astype(o_ref.dtype)

    def at(q):
        return lambda p, j, idx: jnp.clip(j + 4 * (p - q), 0, 3)

    in_specs, out_specs = [], []
    for q, part in enumerate(parts):
        in_specs.append(pl.BlockSpec((None, None) + part.shape[2:],
                                     lambda p, j, idx, blk=at(q): (blk(p, j, idx), idx[0], 0, 0)))
    for q, part in enumerate(parts):
        spec = pl.BlockSpec((None,) + part.shape[2:], lambda p, j, idx, blk=at(q): (blk(p, j, idx), 0, 0))
        in_specs.append(spec)
        out_specs.append(spec)
    chip_sums = pl.pallas_call(
        body_add, name="rs_pair_sum_" + tag,
        out_shape=[jax.ShapeDtypeStruct((4,) + p.shape[2:], p.dtype) for p in parts],
        grid_spec=pltpu.PrefetchScalarGridSpec(num_scalar_prefetch=1, grid=(n, 4), in_specs=in_specs,
                                               out_specs=out_specs),
        compiler_params=_cparams(("arbitrary", "arbitrary")),
    )(jnp.stack([c]).astype(jnp.int32), *parts, *from_sibling)

    others = [None] * n
    riding = set()
    for host, which in (hosts or {}).items():
        rider = _sums_over_ici([chip_sums[i] for i in which])
        _CARRIED[host] = rider
        for pos, i in enumerate(which):
            others[i] = functools.partial(lambda r, p: r.outs[p], rider, pos)
        riding.update(which)
    rest = [i for i in range(n) if i not in riding]
    if rest:
        alone = _sums_over_ici([chip_sums[i] for i in rest])

        def body_chips(*refs):
            k = len(rest)
            cps = alone.copies(refs[:k], refs[k:2 * k], *refs[2 * k:])
            for cp in cps:
                cp.start()
            for cp in cps:
                cp.wait()

        from_chips = pl.pallas_call(
            body_chips, name="rs_chips_" + tag, out_shape=alone.out_shapes,
            in_specs=[ANY_SPEC] * len(rest), out_specs=[ANY_SPEC] * len(rest), scratch_shapes=alone.sems(),
        )(*alone.ins)
        for pos, i in enumerate(rest):
            others[i] = functools.partial(lambda got: got, from_chips[pos])
    return list(zip(chip_sums, others))


def _own_index():
    x, y, _ = _place()
    return jnp.stack([2 * x + y]).astype(jnp.int32)


def _own_total(name, chip_sum, others):
    _, r, cols = chip_sum.shape
    tr = _row_tile(r)

    def body(idx_ref, t_ref, g_ref, o_ref):
        o_ref[...] = ((t_ref[...].astype(f32) + g_ref[0].astype(f32)) + g_ref[1].astype(f32)) + g_ref[2].astype(f32)

    return pl.pallas_call(
        body, name=name, out_shape=jax.ShapeDtypeStruct((r, cols), f32),
        grid_spec=pltpu.PrefetchScalarGridSpec(
            num_scalar_prefetch=1, grid=(r // tr,),
            in_specs=[pl.BlockSpec((None, tr, cols), lambda t, idx: (idx[0], t, 0)),
                      pl.BlockSpec((3, tr, cols), lambda t, idx: (0, t, 0))],
            out_specs=pl.BlockSpec((tr, cols), lambda t, idx: (t, 0))),
        compiler_params=_cparams(("parallel",)),
    )(_own_index(), chip_sum, others)


def _adam_update(w, gv, m, v):
    m_new = ADAM_B1 * m + (1.0 - ADAM_B1) * gv
    v_new = ADAM_B2 * v + (1.0 - ADAM_B2) * (gv * gv)
    m_hat = m_new / (1.0 - ADAM_B1 ** ADAM_STEP)
    v_hat = v_new / (1.0 - ADAM_B2 ** ADAM_STEP)
    return -ADAM_LR * (m_hat / (jnp.sqrt(v_hat) + ADAM_EPS) + ADAM_WD * w), m_new, v_new


def _adamw_reduced(name, w, pieces, m, v):
    n_p, rows, cols = w.shape
    tr = _row_tile(rows)

    def body(idx_ref, w_ref, *refs):
        red = refs[:2 * n_p]
        m_ref, v_ref, g_ref, d_ref, nm_ref, nv_ref = refs[2 * n_p:]
        p = pl.program_id(0)
        for q in range(n_p):
            @pl.when(p == q)
            def _(t_ref=red[2 * q], o_ref=red[2 * q + 1]):
                gv = ((t_ref[...].astype(f32) + o_ref[0].astype(f32)) + o_ref[1].astype(f32)) + o_ref[2].astype(f32)
                g_ref[...] = gv
                d_ref[...], nm_ref[...], nv_ref[...] = _adam_update(w_ref[...], gv, m_ref[...], v_ref[...])

    spec = pl.BlockSpec((None, tr, cols), lambda p, t, idx: (p, t, 0))
    red_specs, red_args = [], []
    for q, (chip_sum, others) in enumerate(pieces):
        red_specs.append(pl.BlockSpec((None, tr, cols), lambda p, t, idx, q=q: (idx[0], jnp.where(p == q, t, 0), 0)))
        red_specs.append(pl.BlockSpec((3, tr, cols), lambda p, t, idx, q=q: (0, jnp.where(p == q, t, 0), 0)))
        red_args += [chip_sum, others]
    return pl.pallas_call(
        body, name=name, out_shape=[jax.ShapeDtypeStruct((n_p, rows, cols), f32)] * 4,
        grid_spec=pltpu.PrefetchScalarGridSpec(
            num_scalar_prefetch=1, grid=(n_p, rows // tr),
            in_specs=[spec] + red_specs + [spec, spec], out_specs=[spec] * 4),
        compiler_params=_cparams(("parallel", "parallel")),
    )(_own_index(), w, *red_args, m, v)


def _adamw(name, w, g, m, v):
    rows, cols = w.shape
    tr = _row_tile(rows)

    def body(w_ref, g_ref, m_ref, v_ref, d_ref, nm_ref, nv_ref):
        d_ref[...], nm_ref[...], nv_ref[...] = _adam_update(w_ref[...], g_ref[...], m_ref[...], v_ref[...])

    spec = pl.BlockSpec((tr, cols), lambda i: (i, 0))
    return pl.pallas_call(
        body, name=name, grid=(rows // tr,), in_specs=[spec] * 4, out_specs=[spec] * 3,
        out_shape=[jax.ShapeDtypeStruct((rows, cols), f32)] * 3, compiler_params=_cparams(("parallel",)),
    )(w, g, m, v)


WEIGHT_NAMES = ["norm_w", "final_norm_w", "ffn_gate", "ffn_up", "ffn_down", "w_in", "branch_proj", "w_out",
                "s5_lambda_re", "s5_lambda_im", "s5_log_dt", "s5_b_re", "s5_b_im", "s5_c_re", "s5_c_im", "s5_d",
                "s5_glu_w", "s5_glu_b", "hg_lb_logits", "hg_norm_w", "rg_conv_w", "rg_conv_b", "rg_wa", "rg_ba",
                "rg_wx", "rg_bx", "rg_lambda"]
SHARDED = {"ffn_gate": (3, "gate"), "ffn_up": (3, "up"), "ffn_down": (2, "down"), "w_in": (2, "w_in"),
           "branch_proj": (3, "bp"), "w_out": (1, "w_out"), "s5_glu_w": (1, "glu_w"),
           "norm_w": (2, None), "rg_conv_w": (2, None)}
BIG = ["ffn_gate", "ffn_up", "ffn_down", "w_in", "branch_proj", "w_out", "s5_glu_w"]
TRANSPOSED = ("ffn_gate", "ffn_up", "w_in")
PARTS = {"a": [("ffn_gate", 0, 0), ("ffn_up", 0, 0), ("ffn_down", 0, 0)],
         "b": [("ffn_gate", 1, 0), ("ffn_up", 1, 0), ("ffn_down", 1, 0)],
         "mix": [("w_in", None, 0), ("branch_proj", None, 2), ("w_out", None, 0), ("s5_glu_w", None, 0)]}
AG_HOSTS = {"ffn_up_0a": (0, "mix", [0]), "ffn_down_0a": (0, "mix", [1, 2, 3]), "mix_proj_0": (0, "b", [0, 1, 2]),
            "s5_out_0": (1, "a", [0]), "hg_fwd_0": (1, "a", [1]), "rg_fwd_0": (1, "a", [2]),
            "merge_fwd_0": (1, "b", [0]), "ffn_up_0b": (1, "b", [1]), "ffn_down_0b": (1, "b", [2]),
            "s5_scan_fwd_0": (1, "mix", [0, 1]), "mix_out_0": (1, "mix", [2, 3])}
RS_HOSTS = {(1, "b"): {"s5_scan_bwd_1": [0, 1, 2]},
            (1, "mix"): {"ffn_bwd_mid_1a": [1, 2, 3], "mix_dh_0": [0]},
            (1, "a"): {"mix_dwin_0": [0, 1], "merge_bwd_0": [2]},
            (0, "b"): {"s5_scan_bwd_0": [0, 1, 2]},
            (0, "mix"): {"ffn_bwd_mid_0a": [1, 2, 3], "ffn_dh_0a": [0]}}
SMALL_SHARDED = ["norm_w", "rg_conv_w"]
REPLICATED = [n for n in WEIGHT_NAMES if n not in SHARDED]
LANES = 128


PACK_ROWS = 512


def _pack_rows(arrays, names):
    pieces = []
    for n in names:
        flat = arrays[n].reshape(-1)
        pieces.append(jnp.pad(flat, (0, -flat.shape[0] % LANES)).reshape(-1, LANES))
    rows = jnp.concatenate(pieces, axis=0)
    return jnp.pad(rows, ((0, -rows.shape[0] % PACK_ROWS), (0, 0)))


def _unpack_rows(rows, names, like):
    out, r0 = {}, 0
    for n in names:
        size = math.prod(like[n].shape)
        nrows = -(-size // LANES)
        out[n] = rows[r0:r0 + nrows].reshape(-1)[:size].reshape(like[n].shape)
        r0 += nrows
    return out


def _unshard(gathered, axis):
    g = jnp.moveaxis(gathered, 0, axis)
    shp = g.shape
    return g.reshape(shp[:axis] + (shp[axis] * shp[axis + 1],) + shp[axis + 2:])


RELAYOUT_ROWS = 256


def _column_runs(width, first_col):
    total = N_DEV * width
    runs = []
    for j in range(N_DEV):
        start = (width * j + first_col) % total
        head = min(width, total - start)
        runs.append((j, 0, start, head))
        if head < width:
            runs.append((j, head, 0, width - head))
    return runs


def _unshard_columns(name, gathered, first_col=0):
    _, r, c = gathered.shape
    tr = min(RELAYOUT_ROWS, r)
    runs = _column_runs(c, first_col)

    def body(g_ref, o_ref):
        for j, off, dst, length in runs:
            o_ref[:, dst:dst + length] = g_ref[j, :, off:off + length]

    return pl.pallas_call(
        body, name=name, grid=(r // tr,), in_specs=[pl.BlockSpec((N_DEV, tr, c), lambda i: (0, i, 0))],
        out_specs=pl.BlockSpec((tr, N_DEV * c), lambda i: (i, 0)),
        out_shape=jax.ShapeDtypeStruct((r, N_DEV * c), gathered.dtype), compiler_params=_cparams(("parallel",)),
    )(gathered)


def _columns_to_blocks(name, full, first_col=0):
    r, total = full.shape
    c = total // N_DEV
    tr = min(RELAYOUT_ROWS, r)
    runs = _column_runs(c, first_col)

    def body(x_ref, o_ref):
        for j, off, src, length in runs:
            o_ref[j // 2, j % 2, :, off:off + length] = x_ref[:, src:src + length].astype(bf16)

    return pl.pallas_call(
        body, name=name, grid=(r // tr,), in_specs=[pl.BlockSpec((tr, total), lambda i: (i, 0))],
        out_specs=pl.BlockSpec((4, 2, tr, c), lambda i: (0, 0, i, 0)),
        out_shape=jax.ShapeDtypeStruct((4, 2, r, c), bf16), compiler_params=_cparams(("parallel",)),
    )(full)


def _to_blocks(full, axis):
    shp = full.shape
    g = full.reshape(shp[:axis] + (4, 2, shp[axis] // N_DEV) + shp[axis + 1:])
    g = jnp.moveaxis(g, (axis, axis + 1), (0, 1))
    return g.reshape(4, 2, -1, g.shape[-1])


W_IN_SPLIT = IN_TOTAL - GM_WIDTH


def kernel(x, norm_w, final_norm_w, ffn_gate, ffn_up, ffn_down, w_in, branch_proj, w_out, s5_lambda_re, s5_lambda_im, s5_log_dt, s5_b_re, s5_b_im, s5_c_re, s5_c_im, s5_d, s5_glu_w, s5_glu_b, hg_lb_logits, hg_norm_w, rg_conv_w, rg_conv_b, rg_wa, rg_ba, rg_wx, rg_bx, rg_lambda, loss_target, m_norm_w, m_final_norm_w, m_ffn_gate, m_ffn_up, m_ffn_down, m_w_in, m_branch_proj, m_w_out, m_s5_lambda_re, m_s5_lambda_im, m_s5_log_dt, m_s5_b_re, m_s5_b_im, m_s5_c_re, m_s5_c_im, m_s5_d, m_s5_glu_w, m_s5_glu_b, m_hg_lb_logits, m_hg_norm_w, m_rg_conv_w, m_rg_conv_b, m_rg_wa, m_rg_ba, m_rg_wx, m_rg_bx, m_rg_lambda, v_norm_w, v_final_norm_w, v_ffn_gate, v_ffn_up, v_ffn_down, v_w_in, v_branch_proj, v_w_out, v_s5_lambda_re, v_s5_lambda_im, v_s5_log_dt, v_s5_b_re, v_s5_b_im, v_s5_c_re, v_s5_c_im, v_s5_d, v_s5_glu_w, v_s5_glu_b, v_hg_lb_logits, v_hg_norm_w, v_rg_conv_w, v_rg_conv_b, v_rg_wa, v_rg_ba, v_rg_wx, v_rg_bx, v_rg_lambda):
    w = dict(zip(WEIGHT_NAMES, (norm_w, final_norm_w, ffn_gate, ffn_up, ffn_down, w_in, branch_proj, w_out,
                                s5_lambda_re, s5_lambda_im, s5_log_dt, s5_b_re, s5_b_im, s5_c_re, s5_c_im, s5_d,
                                s5_glu_w, s5_glu_b, hg_lb_logits, hg_norm_w, rg_conv_w, rg_conv_b, rg_wa, rg_ba,
                                rg_wx, rg_bx, rg_lambda)))
    m = dict(zip(WEIGHT_NAMES, (m_norm_w, m_final_norm_w, m_ffn_gate, m_ffn_up, m_ffn_down, m_w_in, m_branch_proj,
                                m_w_out, m_s5_lambda_re, m_s5_lambda_im, m_s5_log_dt, m_s5_b_re, m_s5_b_im, m_s5_c_re,
                                m_s5_c_im, m_s5_d, m_s5_glu_w, m_s5_glu_b, m_hg_lb_logits, m_hg_norm_w, m_rg_conv_w,
                                m_rg_conv_b, m_rg_wa, m_rg_ba, m_rg_wx, m_rg_bx, m_rg_lambda)))
    v = dict(zip(WEIGHT_NAMES, (v_norm_w, v_final_norm_w, v_ffn_gate, v_ffn_up, v_ffn_down, v_w_in, v_branch_proj,
                                v_w_out, v_s5_lambda_re, v_s5_lambda_im, v_s5_log_dt, v_s5_b_re, v_s5_b_im, v_s5_c_re,
                                v_s5_c_im, v_s5_d, v_s5_glu_w, v_s5_glu_b, v_hg_lb_logits, v_hg_norm_w, v_rg_conv_w,
                                v_rg_conv_b, v_rg_wa, v_rg_ba, v_rg_wx, v_rg_bx, v_rg_lambda)))
    rows = x.shape[1]

    _CARRIED.clear()

    def shard_of(piece, l):
        n, k, _ = piece
        shard = w[n][l] if k is None else w[n][l, k]
        return (jnp.swapaxes(shard, 0, 1) if n in TRANSPOSED else shard).astype(bf16)

    def assemble(l, part, gathered):
        full = []
        for j, (piece, g) in enumerate(zip(PARTS[part], gathered)):
            tag = "unshard_%d%s%d" % (l, part, j)
            if piece[0] == "w_in":
                rows_nat = _unshard(g, 0)
                full.append(jnp.concatenate([rows_nat[W_IN_SPLIT:], rows_nat[:W_IN_SPLIT]], axis=0))
            elif piece[0] == "branch_proj":
                full.append(_unshard_columns(tag, g.reshape(N_DEV, -1, g.shape[-1])).reshape(N_BRANCH, BRANCH, D_MODEL))
            elif piece[2] == g.ndim - 2:
                full.append(_unshard_columns(tag, g))
            else:
                full.append(_unshard(g, piece[2]))
        return full

    n_a = len(PARTS["a"])
    first = _all_gather("gather_weights", [shard_of(p, 0) for p in PARTS["a"]] + [w[n] for n in SMALL_SHARDED])
    small = {n: w[n] for n in REPLICATED}
    for n, g in zip(SMALL_SHARDED, first[n_a:]):
        small[n] = _unshard(g, SHARDED[n][0])
    riders = {}
    for host, (l, part, which) in AG_HOSTS.items():
        rider = _gather_over_ici([shard_of(PARTS[part][j], l) for j in which])
        _CARRIED[host] = rider
        riders.setdefault((l, part), []).append((which, rider))

    def weights_of(l, part):
        if (l, part) == (0, "a"):
            return assemble(l, part, first[:n_a])
        landed = [None] * len(PARTS[part])
        for which, rider in riders[l, part]:
            for j, buf in zip(which, rider.outs):
                landed[j] = buf
        return assemble(l, part, _gather_forward("gather_forward_%d%s" % (l, part), landed))

    sums = {}

    def blocks_of(l, part, grads):
        out = []
        for j, (piece, g) in enumerate(zip(PARTS[part], grads)):
            tag = "to_blocks_%d%s%d" % (l, part, j)
            if piece[0] == "w_in":
                out.append(_to_blocks(jnp.concatenate([g[GM_WIDTH:], g[:GM_WIDTH]], axis=0), 0))
            elif piece[0] == "branch_proj":
                out.append(_columns_to_blocks(tag, g.reshape(-1, g.shape[-1])))
            elif piece[2] == g.ndim - 1:
                out.append(_columns_to_blocks(tag, g))
            else:
                out.append(_to_blocks(g, piece[2]).astype(bf16))
        return out

    last_grads = []

    def grads_done(l, part, grads):
        if (l, part) in RS_HOSTS:
            sums[l, part] = _reduce_scatter("%d%s" % (l, part), blocks_of(l, part, grads), hosts=RS_HOSTS[l, part])
        else:
            last_grads.extend(blocks_of(l, part, grads))

    loss_part, dx, g_small = _local_step(x[0], loss_target[0], weights_of, small, grads_done)
    loss = lax.psum(loss_part, ("x", "y", "c"))

    parts = last_grads + [_to_blocks(g_small[n], SHARDED[n][0]) for n in SMALL_SHARDED]
    rep_rows = _pack_rows(g_small, REPLICATED)
    rep_slice = rep_rows.shape[0] // N_DEV
    parts.append(rep_rows.reshape(4, 2, rep_slice, LANES))
    last = _reduce_scatter("last", parts)
    sums[0, "a"] = last[:n_a]

    grads, delta, new_m, new_v = {}, {}, {}, {}

    def update(n, pieces):
        def view(a):
            a = jnp.swapaxes(a, -1, -2) if n in TRANSPOSED else a
            return a.reshape(len(pieces), -1, a.shape[-1])

        def back(r):
            shp = w[n].shape
            if n in TRANSPOSED:
                return jnp.swapaxes(r.reshape(shp[:-2] + (shp[-1], shp[-2])), -1, -2)
            return r.reshape(shp)

        res = _adamw_reduced("adamw_" + n, view(w[n]), [(t, others()) for t, others in pieces], view(m[n]), view(v[n]))
        grads[n], delta[n], new_m[n], new_v[n] = (back(r) for r in res)

    for n in BIG:
        update(n, [sums[l, part][j] for l in range(DEPTH) for part in ("a", "b", "mix")
                   for j, piece in enumerate(PARTS[part]) if piece[0] == n])
    for j, n in enumerate(SMALL_SHARDED):
        update(n, [last[n_a + j]])
    rep_mine = _own_total("rs_total_small", last[-1][0], last[-1][1]())
    rep_grads = _all_gather("gather_small_grads", [rep_mine])[0].reshape(-1, LANES)
    res = _adamw("adamw_small", _pack_rows(w, REPLICATED), rep_grads, _pack_rows(m, REPLICATED), _pack_rows(v, REPLICATED))
    for dst, src in zip((grads, delta, new_m, new_v), (rep_grads,) + tuple(res)):
        dst.update(_unpack_rows(src, REPLICATED, w))

    return (loss, dx.reshape(x.shape), *[grads[n] for n in WEIGHT_NAMES], *[delta[n] for n in WEIGHT_NAMES],
            *[new_m[n] for n in WEIGHT_NAMES], *[new_v[n] for n in WEIGHT_NAMES])
```

```python
import functools
import math

import jax
import jax.numpy as jnp
from jax import lax
from jax.experimental import pallas as pl
from jax.experimental.pallas import tpu as pltpu

f32 = jnp.float32
bf16 = jnp.bfloat16

D_MODEL = 1024
DEPTH = 2
BRANCH = 512
N_BRANCH = 3
S5_GROUP = 16
S5_GROUPS = 32
S5_STATE = 64
S5_LANES = S5_GROUPS * S5_STATE
S5_EIG_MAX = -1e-4
HG_HEADS = 4
HG_DK = 128
HG_CHUNK = 32
RG_BLOCKS = 8
RG_BLOCK = 64
RG_C = 8.0
D_FF = 2816
EPS = 1e-6
IN_TOTAL = 6656
GM_WIDTH = N_BRANCH * D_MODEL
N_DEV = 8

ADAM_LR = 0.001
ADAM_B1 = 0.9
ADAM_B2 = 0.999
ADAM_EPS = 1e-08
ADAM_WD = 0.01
ADAM_STEP = 10

VMEM_LIMIT_V7X = 56 * 1024 * 1024
ROW_TILE = 256
FF_TILE = 1408
TOKEN_K = 4096
MXU_COLS = 256


def _cparams(sem):
    return pltpu.CompilerParams(dimension_semantics=sem, vmem_limit_bytes=VMEM_LIMIT_V7X)


MESH_IDS = pl.DeviceIdType.MESH
ANY_SPEC = pl.BlockSpec(memory_space=pl.ANY)


def _place():
    return lax.axis_index("x"), lax.axis_index("y"), lax.axis_index("c")


class _Carry:
    def __init__(self, ins, out_shapes, n_sems, copies):
        self.ins, self.out_shapes, self.n_sems, self.copies = list(ins), list(out_shapes), n_sems, copies
        self.outs = None

    def sems(self):
        return [pltpu.SemaphoreType.DMA((self.n_sems,)), pltpu.SemaphoreType.DMA((self.n_sems,))]

    def start(self, when, *riders):
        @pl.when(when)
        def _():
            for cp in self.copies(*riders):
                cp.start()

    def finish(self, when, *riders):
        @pl.when(when)
        def _():
            for cp in self.copies(*riders):
                cp.wait()


_CARRIED = {}


def _call_with_rider(name, body, grid, in_specs, out_specs, out_shape, scratch, semantics, args):
    carry = _CARRIED.pop(name, None)
    if carry is None:
        return pl.pallas_call(body, name=name, grid=grid, in_specs=in_specs, out_specs=out_specs,
                              out_shape=out_shape, scratch_shapes=scratch, compiler_params=_cparams(semantics))(*args)
    n_in, n_out, nci, nco = len(in_specs), len(out_specs), len(carry.ins), len(carry.out_shapes)

    def kern(*refs):
        ids = [pl.program_id(d) for d in range(len(grid))]
        own = refs[:n_in] + refs[n_in + nci:n_in + nci + n_out] + refs[n_in + nci + n_out + nco:-2]
        riders = (refs[n_in:n_in + nci], refs[n_in + nci + n_out:n_in + nci + n_out + nco]) + tuple(refs[-2:])
        carry.start(functools.reduce(jnp.logical_and, [p == 0 for p in ids]), *riders)
        body(*own)
        carry.finish(functools.reduce(jnp.logical_and, [p == g - 1 for p, g in zip(ids, grid)]), *riders)

    res = pl.pallas_call(
        kern, name=name, grid=grid, in_specs=list(in_specs) + [ANY_SPEC] * nci,
        out_specs=list(out_specs) + [ANY_SPEC] * nco, out_shape=list(out_shape) + carry.out_shapes,
        scratch_shapes=list(scratch) + carry.sems(), compiler_params=_cparams(("arbitrary",) * len(grid)),
    )(*args, *carry.ins)
    carry.outs = res[n_out:]
    return res[:n_out]


def _sigmoid(x):
    return 0.5 * jnp.tanh(0.5 * x) + 0.5


def _sigmoid_small(x):
    return 1.0 / (1.0 + jnp.exp(-x))


_GELU_C = math.sqrt(2.0 / math.pi)


def _gelu(x):
    t = jnp.tanh(_GELU_C * (x + 0.044715 * x * x * x))
    return 0.5 * x * (1.0 + t)


def _gelu_grad(x):
    t = jnp.tanh(_GELU_C * (x + 0.044715 * x * x * x))
    return 0.5 * (1.0 + t) + 0.5 * x * (1.0 - t * t) * _GELU_C * (1.0 + 3.0 * 0.044715 * x * x)


def _expm1(x):
    p = x * (1.0 + x * (0.5 + x * (1.0 / 6 + x * (1.0 / 24 + x * (1.0 / 120 + x * (1.0 / 720))))))
    return jnp.where(jnp.abs(x) < 0.3, p, jnp.exp(x) - 1.0)


def _dot(a, b):
    return jnp.dot(a, b, preferred_element_type=f32)


def _dot_nt(a, b):
    return lax.dot_general(a, b, (((1,), (1,)), ((), ())), preferred_element_type=f32)


def _dot_tn(a, b):
    return lax.dot_general(a, b, (((0,), (0,)), ((), ())), preferred_element_type=f32)


def _bdot(a, b):
    return lax.dot_general(a, b, (((2,), (1,)), ((0,), (0,))), preferred_element_type=f32)


def _bdot_nt(a, b):
    return lax.dot_general(a, b, (((2,), (2,)), ((0,), (0,))), preferred_element_type=f32)


def _rows(shape):
    return lax.broadcasted_iota(jnp.int32, shape, 0)


def _scan_fwd(a, b, n):
    row = _rows(a.shape)
    s = 1
    while s < n:
        valid = row >= s
        sh_a = pltpu.roll(a, s, 0)
        sh_b = pltpu.roll(b, s, 0)
        b = b + a * jnp.where(valid, sh_b, 0.0)
        a = a * jnp.where(valid, sh_a, 1.0)
        s *= 2
    return a, b


def _scan_bwd(a, b, n):
    row = _rows(a.shape)
    s = 1
    while s < n:
        valid = row < n - s
        sh_a = pltpu.roll(a, n - s, 0)
        sh_b = pltpu.roll(b, n - s, 0)
        b = b + a * jnp.where(valid, sh_b, 0.0)
        a = a * jnp.where(valid, sh_a, 1.0)
        s *= 2
    return a, b


def _seg_cumsum(x, n, seg):
    pos = _rows(x.shape) % seg
    s = 1
    while s < seg:
        x = x + jnp.where(pos >= s, pltpu.roll(x, s, 0), 0.0)
        s *= 2
    return x


def _seg_rev_cumsum(x, n, seg):
    pos = _rows(x.shape) % seg
    s = 1
    while s < seg:
        x = x + jnp.where(pos < seg - s, pltpu.roll(x, n - s, 0), 0.0)
        s *= 2
    return x


def _head_mean(x):
    parts = []
    for h in range(HG_HEADS):
        m = jnp.mean(x[:, h * HG_DK:(h + 1) * HG_DK], axis=1, keepdims=True)
        parts.append(jnp.broadcast_to(m, (x.shape[0], HG_DK)))
    return jnp.concatenate(parts, axis=1)


def _mm(name, a_list, b_list, terms, n_acc, mode, m, n, k, tm, tn, tk, out_dtypes, epilogue, extras=(), vecs=(),
        n_part=0, chunk=0):
    tm, tn, tk = min(tm, m), min(tn, n), min(tk, k)
    assert m % tm == 0 and n % tn == 0 and k % tk == 0, (name, m, n, k, tm, tn, tk)
    gk = k // tk
    if mode == "tn":
        a_spec = pl.BlockSpec((tk, tm), lambda i, j, kk: (kk, i))
    else:
        a_spec = pl.BlockSpec((tm, tk), lambda i, j, kk: (i, kk))
    if mode == "nt":
        b_spec = pl.BlockSpec((tn, tk), lambda i, j, kk: (j, kk))
    else:
        b_spec = pl.BlockSpec((tk, tn), lambda i, j, kk: (kk, j))
    o_spec = pl.BlockSpec((tm, tn), lambda i, j, kk: (i, j))
    v_spec = pl.BlockSpec((1, tn), lambda i, j, kk: (0, j))
    p_spec = pl.BlockSpec((None, 1, tn), lambda i, j, kk: (i, 0, j))
    dot = {"nn": _dot, "nt": _dot_nt, "tn": _dot_tn}[mode]
    na, nb, ne, nv, no = len(a_list), len(b_list), len(extras), len(vecs), len(out_dtypes)
    carry = _CARRIED.pop(name, None)
    nci, nco = (len(carry.ins), len(carry.out_shapes)) if carry else (0, 0)
    n_in = na + nb + ne + nv + nci
    grid = (m // tm, n // tn, gk)

    def kern(*refs):
        if carry:
            ids = [pl.program_id(d) for d in range(3)]
            riders = (refs[n_in - nci:n_in], refs[n_in + no + n_part:n_in + no + n_part + nco]) + tuple(refs[-2:])
            carry.start(functools.reduce(jnp.logical_and, [p == 0 for p in ids]), *riders)
        compute(*refs)
        if carry:
            carry.finish(functools.reduce(jnp.logical_and, [p == g - 1 for p, g in zip(ids, grid)]), *riders)

    def compute(*refs):
        a_refs = refs[:na]
        b_refs = refs[na:na + nb]
        e_refs = refs[na + nb:na + nb + ne]
        v_refs = refs[na + nb + ne:na + nb + ne + nv]
        o_refs = refs[n_in:n_in + no + n_part]

        def finish(accs):
            outs = epilogue(accs, [e[...] for e in e_refs], [r[...] for r in v_refs])
            for o, val in zip(o_refs, outs):
                o[...] = val.astype(o.dtype)

        def partial_sums():
            sums = [None] * n_acc
            for ai, bi, ci in terms:
                d = dot(a_refs[ai][...].astype(bf16), b_refs[bi][...].astype(bf16))
                sums[ci] = d if sums[ci] is None else sums[ci] + d
            return sums

        if gk == 1 and chunk:
            assert mode in ("nn", "nt") and tn % chunk == 0
            for c0 in range(0, tn, chunk):
                cols = slice(c0, c0 + chunk)
                sums = [None] * n_acc
                for ai, bi, ci in terms:
                    b_part = b_refs[bi][:, cols] if mode == "nn" else b_refs[bi][cols, :]
                    d = dot(a_refs[ai][...].astype(bf16), b_part.astype(bf16))
                    sums[ci] = d if sums[ci] is None else sums[ci] + d
                outs = epilogue(sums, [e[:, cols] for e in e_refs], [r[:, cols] for r in v_refs])
                for o, val in zip(o_refs, outs):
                    o[:, cols] = val.astype(o.dtype)
            return
        if gk == 1:
            finish(partial_sums())
            return
        acc = refs[n_in + no + n_part + nco]
        kk = pl.program_id(2)

        @pl.when(kk == 0)
        def _():
            acc[...] = jnp.zeros_like(acc)

        for ci, d in enumerate(partial_sums()):
            acc[ci] += d

        @pl.when(kk == gk - 1)
        def _():
            finish([acc[c] for c in range(n_acc)])

    res = pl.pallas_call(
        kern, name=name,
        grid=grid,
        in_specs=[a_spec] * na + [b_spec] * nb + [o_spec] * ne + [v_spec] * nv + [ANY_SPEC] * nci,
        out_specs=[o_spec] * no + [p_spec] * n_part + [ANY_SPEC] * nco,
        out_shape=([jax.ShapeDtypeStruct((m, n), dt) for dt in out_dtypes]
                   + [jax.ShapeDtypeStruct((m // tm, 1, n), f32)] * n_part + (carry.out_shapes if carry else [])),
        scratch_shapes=([pltpu.VMEM((n_acc, tm, tn), f32)] if gk > 1 else []) + (carry.sems() if carry else []),
        compiler_params=_cparams(("arbitrary",) * 3 if carry else ("parallel", "parallel", "arbitrary")),
    )(*a_list, *b_list, *extras, *vecs, *(carry.ins if carry else []))
    if carry:
        carry.outs = res[no + n_part:]
        res = res[:no + n_part]
    return res


def _mm1(name, a, b, mode, m, n, k, tm, tn, tk, out_dtype=f32, scale=None):
    def epi(accs, extras, vecs):
        return [accs[0] if scale is None else accs[0] * scale]
    return _mm(name, [a], [b], [(0, 0, 0)], 1, mode, m, n, k, tm, tn, tk, [out_dtype], epi)[0]


def _rt(name, body, rows, tm, row_ins, consts, row_outs, acc_outs=(), scratch=(), reverse=False):
    tm = min(tm, rows)
    assert rows % tm == 0
    nt = rows // tm

    def tile(i):
        return nt - 1 - i if reverse else i

    in_specs, args = [], []
    for spec in row_ins:
        arr = spec[0]
        if isinstance(spec[1], int):
            in_specs.append(pl.BlockSpec((tm, spec[1]), lambda i, cb=spec[2]: (tile(i), cb)))
        else:
            in_specs.append(pl.BlockSpec(spec[1], lambda i, fn=spec[2]: fn(tile(i))))
        args.append(arr)
    for c in consts:
        in_specs.append(pl.BlockSpec(c.shape, lambda i, nd=c.ndim: (0,) * nd))
        args.append(c)
    out_specs, out_shape = [], []
    for spec in row_outs:
        if isinstance(spec[0], int):
            out_specs.append(pl.BlockSpec((tm, spec[0]), lambda i: (tile(i), 0)))
            out_shape.append(jax.ShapeDtypeStruct((rows, spec[0]), spec[1]))
        else:
            out_specs.append(pl.BlockSpec(spec[1], lambda i, fn=spec[2]: fn(tile(i))))
            out_shape.append(jax.ShapeDtypeStruct(spec[0], spec[3]))
    for shp in acc_outs:
        out_specs.append(pl.BlockSpec(shp, lambda i, nd=len(shp): (0,) * nd))
        out_shape.append(jax.ShapeDtypeStruct(shp, f32))
    n_in = len(args)
    n_row_out = len(row_outs)
    n_acc = len(acc_outs)
    n_out = n_row_out + n_acc
    carry = _CARRIED.pop(name, None)
    nci, nco = (len(carry.ins), len(carry.out_shapes)) if carry else (0, 0)

    def kern(*refs):
        i = pl.program_id(0)
        if carry:
            own = refs[:n_in] + refs[n_in + nci:n_in + nci + n_out] + refs[n_in + nci + n_out + nco:-2]
            riders = (refs[n_in:n_in + nci], refs[n_in + nci + n_out:n_in + nci + n_out + nco]) + tuple(refs[-2:])
            carry.start(i == 0, *riders)
        else:
            own = refs
        acc_refs = own[n_in + n_row_out:n_in + n_out]

        @pl.when(i == 0)
        def _():
            for r in acc_refs:
                r[...] = jnp.zeros_like(r)

        body(i, *own)
        if carry:
            carry.finish(i == nt - 1, *riders)

    res = pl.pallas_call(
        kern, name=name, grid=(nt,), in_specs=in_specs + [ANY_SPEC] * nci, out_specs=out_specs + [ANY_SPEC] * nco,
        out_shape=out_shape + (carry.out_shapes if carry else []),
        scratch_shapes=list(scratch) + (carry.sems() if carry else []), compiler_params=_cparams(("arbitrary",)),
    )(*args, *(carry.ins if carry else []))
    if carry:
        carry.outs = res[n_out:]
        res = res[:n_out]
    return res


def _rms_rows(xv, wv):
    r = lax.rsqrt(jnp.mean(xv * xv, axis=1, keepdims=True) + EPS)
    return (xv * r * wv).astype(bf16)


def _rms_bwd_rows(xv, dhv, wv, dres):
    r = lax.rsqrt(jnp.mean(xv * xv, axis=1, keepdims=True) + EPS)
    xn = xv * r
    dxn = dhv * wv
    dx = dres + r * (dxn - xn * jnp.mean(dxn * xn, axis=1, keepdims=True))
    return [dx, dx.astype(bf16), jnp.sum(dhv * xn, axis=0, keepdims=True)]


def _rms_fwd(name, x, w, rows):
    def body(i, x_ref, w_ref, h_ref):
        h_ref[...] = _rms_rows(x_ref[...], w_ref[...])
    return _rt(name, body, rows, ROW_TILE, [(x, D_MODEL, 0)], [w], [(D_MODEL, bf16)])[0]


def _residual_then_norm(scale):
    def epi(accs, extras, vecs):
        x_out = extras[0] + scale * accs[0]
        return [x_out] + [_rms_rows(x_out, v) for v in vecs]
    return epi


def _norm_bwd_epilogue(accs, extras, vecs):
    return _rms_bwd_rows(extras[0], accs[0], vecs[0], extras[1])


def _loss_head(x, w, target, rows):
    def body(i, x_ref, t_ref, w_ref, dx_ref, dxb_ref, loss_ref, dw_ref):
        xv = x_ref[...]
        r = lax.rsqrt(jnp.mean(xv * xv, axis=1, keepdims=True) + EPS)
        xn = xv * r
        wv = w_ref[...]
        err = xn * wv - t_ref[...]
        part = 0.5 * jnp.sum(jnp.mean(err * err, axis=1, keepdims=True), axis=0, keepdims=True)
        loss_ref[...] += jnp.broadcast_to(part, (1, 128))
        dy = err * (1.0 / D_MODEL)
        dxn = dy * wv
        dx = r * (dxn - xn * jnp.mean(dxn * xn, axis=1, keepdims=True))
        dx_ref[...] = dx
        dxb_ref[...] = dx.astype(bf16)
        dw_ref[...] += jnp.sum(dy * xn, axis=0, keepdims=True)
    return _rt("loss_head", body, rows, ROW_TILE, [(x, D_MODEL, 0), (target, D_MODEL, 0)], [w],
               [(D_MODEL, f32), (D_MODEL, bf16)], acc_outs=[(1, 128), (1, D_MODEL)])


def _ffn_fwd(tag, x, hb, wg_t, wu_t, wd, next_nw, rows):
    def epi_up(accs, extras, vecs):
        a, b = accs
        return [a, b, a * _sigmoid(a) * b]
    a, b, s = _mm("ffn_up_" + tag, [hb], [wg_t, wu_t], [(0, 0, 0), (0, 1, 1)], 2, "nt", rows, D_FF, D_MODEL,
                  512, D_FF, D_MODEL, [bf16, bf16, bf16], epi_up, chunk=MXU_COLS)
    outs = _mm("ffn_down_" + tag, [s], [wd], [(0, 0, 0)], 1, "nn", rows, D_MODEL, D_FF,
               512, D_MODEL, D_FF, [f32] + [bf16] * len(next_nw), _residual_then_norm(0.5), extras=[x],
               vecs=next_nw)
    return outs[0], (outs[1] if next_nw else None), (x, hb, a, b, s)


def _ffn_bwd(tag, saved, nw, wg_t, wu_t, wd, dx, dxb, rows):
    x, hb, a, b, s = saved

    def epi_mid(accs, extras, vecs):
        ds = 0.5 * accs[0]
        av = extras[0].astype(f32)
        bv = extras[1].astype(f32)
        sg = _sigmoid(av)
        return [ds * bv * sg * (1.0 + av * (1.0 - sg)), ds * av * sg]
    da, db = _mm("ffn_bwd_mid_" + tag, [dxb], [wd], [(0, 0, 0)], 1, "nt", rows, D_FF, D_MODEL,
                 512, D_FF, D_MODEL, [bf16, bf16], epi_mid, extras=[a, b], chunk=MXU_COLS)
    d_wd = _mm1("ffn_dwd_" + tag, s, dxb, "tn", D_FF, D_MODEL, rows, FF_TILE, D_MODEL, TOKEN_K, out_dtype=bf16,
                scale=0.5)
    d_wg_t = _mm1("ffn_dwg_" + tag, da, hb, "tn", D_FF, D_MODEL, rows, FF_TILE, D_MODEL, TOKEN_K, out_dtype=bf16)
    d_wu_t = _mm1("ffn_dwu_" + tag, db, hb, "tn", D_FF, D_MODEL, rows, FF_TILE, D_MODEL, TOKEN_K, out_dtype=bf16)
    dx_in, dxb_in, d_nw = _mm("ffn_dh_" + tag, [da, db], [wg_t, wu_t], [(0, 0, 0), (1, 1, 0)], 1, "nn", rows,
                              D_MODEL, D_FF, 512, D_MODEL, D_FF, [f32, bf16], _norm_bwd_epilogue, extras=[x, dx],
                              vecs=[nw], n_part=1)
    return dx_in, dxb_in, jnp.sum(d_nw, axis=0), d_wg_t, d_wu_t, d_wd


S5_CB = 512
SUBLANES = 8
U_COL = GM_WIDTH // BRANCH


def _s5_scan_fwd(tag, proj, b_re, b_im, a_re, a_im, rows):
    tm = min(ROW_TILE, rows)
    nt = rows // tm
    nc = S5_LANES // S5_CB

    def kern(u_ref, bre_ref, bim_ref, ar_ref, ai_ref, xr_ref, xi_ref, pr_s, pi_s, cr_s, ci_s, mr_s, mi_s):
        t = pl.program_id(1)

        @pl.when(t == 0)
        def _():
            row8 = _rows((SUBLANES, S5_CB))
            pr = jnp.broadcast_to(ar_ref[...], (SUBLANES, S5_CB))
            pi = jnp.broadcast_to(ai_ref[...], (SUBLANES, S5_CB))
            s = 1
            while s < SUBLANES:
                sr = pltpu.roll(pr, s, 0)
                si = pltpu.roll(pi, s, 0)
                valid = row8 >= s
                pr, pi = jnp.where(valid, pr * sr - pi * si, pr), jnp.where(valid, pr * si + pi * sr, pi)
                s *= 2
            pr_s[...] = pr
            pi_s[...] = pi
            for k in range(3):
                s = 1 << k
                mr_s[k] = jnp.where(row8 >= s, pr[s - 1:s, :], 0.0)
                mi_s[k] = jnp.where(row8 >= s, pi[s - 1:s, :], 0.0)
            cr_s[...] = jnp.zeros_like(cr_s)
            ci_s[...] = jnp.zeros_like(ci_s)

        ub = u_ref[...].astype(bf16)
        br = _dot(ub, bre_ref[...])
        bi = _dot(ub, bim_ref[...])
        steps = [(mr_s[k], mi_s[k]) for k in range(3)]
        cr = cr_s[...]
        ci = ci_s[...]
        pr = pr_s[...]
        pi = pi_s[...]
        for g in range(tm // SUBLANES):
            sl = slice(g * SUBLANES, (g + 1) * SUBLANES)
            xr = br[sl]
            xi = bi[sl]
            for k, (mr, mi) in enumerate(steps):
                sr = pltpu.roll(xr, 1 << k, 0)
                si = pltpu.roll(xi, 1 << k, 0)
                xr, xi = xr + (mr * sr - mi * si), xi + (mr * si + mi * sr)
            xr, xi = xr + (pr * cr - pi * ci), xi + (pr * ci + pi * cr)
            xr_ref[sl, :] = xr
            xi_ref[sl, :] = xi
            cr = xr[SUBLANES - 1:SUBLANES, :]
            ci = xi[SUBLANES - 1:SUBLANES, :]
        cr_s[...] = cr
        ci_s[...] = ci

    return _call_with_rider(
        "s5_scan_fwd_" + tag, kern, (nc, nt),
        [pl.BlockSpec((tm, BRANCH), lambda c, t: (t, U_COL)),
         pl.BlockSpec((BRANCH, S5_CB), lambda c, t: (0, c)),
         pl.BlockSpec((BRANCH, S5_CB), lambda c, t: (0, c)),
         pl.BlockSpec((1, S5_CB), lambda c, t: (0, c)),
         pl.BlockSpec((1, S5_CB), lambda c, t: (0, c))],
        [pl.BlockSpec((tm, S5_CB), lambda c, t: (t, c))] * 2,
        [jax.ShapeDtypeStruct((rows, S5_LANES), f32)] * 2,
        [pltpu.VMEM((SUBLANES, S5_CB), f32), pltpu.VMEM((SUBLANES, S5_CB), f32),
         pltpu.VMEM((1, S5_CB), f32), pltpu.VMEM((1, S5_CB), f32),
         pltpu.VMEM((3, SUBLANES, S5_CB), f32), pltpu.VMEM((3, SUBLANES, S5_CB), f32)],
        ("parallel", "arbitrary"), (proj, b_re, b_im, a_re, a_im))


def _s5_scan_bwd(tag, dxr, dxi, xr, xi, a_re, a_im, rows):
    tm = min(ROW_TILE, rows)
    nt = rows // tm
    nc = S5_LANES // S5_CB

    def kern(dxr_ref, dxi_ref, xr_ref, xi_ref, ar_ref, ai_ref, gr_ref, gi_ref, dar_ref, dai_ref,
             qr_s, qi_s, cr_s, ci_s, gr_s, gi_s, mr_s, mi_s):
        t = pl.program_id(1)
        row = _rows((tm, S5_CB))
        ng = tm // SUBLANES

        @pl.when(t == 0)
        def _():
            row8 = _rows((SUBLANES, S5_CB))
            qr = jnp.broadcast_to(ar_ref[...], (SUBLANES, S5_CB))
            qi = jnp.broadcast_to(-ai_ref[...], (SUBLANES, S5_CB))
            s = 1
            while s < SUBLANES:
                sr = pltpu.roll(qr, SUBLANES - s, 0)
                si = pltpu.roll(qi, SUBLANES - s, 0)
                valid = row8 < SUBLANES - s
                qr, qi = jnp.where(valid, qr * sr - qi * si, qr), jnp.where(valid, qr * si + qi * sr, qi)
                s *= 2
            qr_s[...] = qr
            qi_s[...] = qi
            for k in range(3):
                s = 1 << k
                mr_s[k] = jnp.where(row8 < SUBLANES - s, qr[SUBLANES - s:SUBLANES - s + 1, :], 0.0)
                mi_s[k] = jnp.where(row8 < SUBLANES - s, qi[SUBLANES - s:SUBLANES - s + 1, :], 0.0)
            cr_s[...] = jnp.zeros_like(cr_s)
            ci_s[...] = jnp.zeros_like(ci_s)
            dar_ref[...] = jnp.zeros_like(dar_ref)
            dai_ref[...] = jnp.zeros_like(dai_ref)

        steps = [(mr_s[k], mi_s[k]) for k in range(3)]
        cr = cr_s[...]
        ci = ci_s[...]
        qr = qr_s[...]
        qi = qi_s[...]
        last8 = _rows((SUBLANES, S5_CB)) == SUBLANES - 1
        acc_r = jnp.zeros((SUBLANES, S5_CB), f32)
        acc_i = jnp.zeros((SUBLANES, S5_CB), f32)
        for g in reversed(range(ng)):
            sl = slice(g * SUBLANES, (g + 1) * SUBLANES)
            gr = dxr_ref[sl, :]
            gi = dxi_ref[sl, :]
            for k, (mr, mi) in enumerate(steps):
                sr = pltpu.roll(gr, SUBLANES - (1 << k), 0)
                si = pltpu.roll(gi, SUBLANES - (1 << k), 0)
                gr, gi = gr + (mr * sr - mi * si), gi + (mr * si + mi * sr)
            gr, gi = gr + (qr * cr - qi * ci), gi + (qr * ci + qi * cr)
            gr_s[sl, :] = gr
            gi_s[sl, :] = gi
            gnr = jnp.where(last8, cr, pltpu.roll(gr, SUBLANES - 1, 0))
            gni = jnp.where(last8, ci, pltpu.roll(gi, SUBLANES - 1, 0))
            xr_v = xr_ref[sl, :]
            xi_v = xi_ref[sl, :]
            acc_r = acc_r + (gnr * xr_v + gni * xi_v)
            acc_i = acc_i + (gni * xr_v - gnr * xi_v)
            cr = gr[0:1, :]
            ci = gi[0:1, :]
        cr_s[...] = cr
        ci_s[...] = ci
        gr_ref[...] = gr_s[...].astype(bf16)
        gi_ref[...] = gi_s[...].astype(bf16)
        dar_ref[...] += jnp.sum(acc_r, axis=0, keepdims=True)
        dai_ref[...] += jnp.sum(acc_i, axis=0, keepdims=True)

    rev = lambda c, t: (nt - 1 - t, c)
    return _call_with_rider(
        "s5_scan_bwd_" + tag, kern, (nc, nt),
        [pl.BlockSpec((tm, S5_CB), rev)] * 4 + [pl.BlockSpec((1, S5_CB), lambda c, t: (0, c))] * 2,
        [pl.BlockSpec((tm, S5_CB), rev)] * 2 + [pl.BlockSpec((1, S5_CB), lambda c, t: (0, c))] * 2,
        [jax.ShapeDtypeStruct((rows, S5_LANES), bf16)] * 2 + [jax.ShapeDtypeStruct((1, S5_LANES), f32)] * 2,
        [pltpu.VMEM((SUBLANES, S5_CB), f32), pltpu.VMEM((SUBLANES, S5_CB), f32),
         pltpu.VMEM((1, S5_CB), f32), pltpu.VMEM((1, S5_CB), f32),
         pltpu.VMEM((tm, S5_CB), f32), pltpu.VMEM((tm, S5_CB), f32),
         pltpu.VMEM((3, SUBLANES, S5_CB), f32), pltpu.VMEM((3, SUBLANES, S5_CB), f32)],
        ("parallel", "arbitrary"), (dxr, dxi, xr, xi, a_re, a_im))


def _s5_fwd(tag, proj, cst, rows):
    xr, xi = _s5_scan_fwd(tag, proj, cst["b_re"].astype(bf16), cst["b_im"].astype(bf16), cst["a_re"], cst["a_im"], rows)

    def body(i, xr_ref, xi_ref, u_ref, cre_ref, cim_ref, d_ref, gw_ref, gb_ref, y_ref, out_ref):
        y = (_dot(xr_ref[...].astype(bf16), cre_ref[...]) + _dot(xi_ref[...].astype(bf16), cim_ref[...])
             + d_ref[...] * u_ref[...])
        y_ref[...] = y
        z = _gelu(y)
        zg = _dot(z.astype(bf16), gw_ref[...]) + gb_ref[...]
        out_ref[...] = (z * _sigmoid(zg)).astype(bf16)

    y, out = _rt("s5_out_" + tag, body, rows, ROW_TILE,
                 [(xr, S5_LANES, 0), (xi, S5_LANES, 0), (proj, BRANCH, U_COL)],
                 [cst["c_re"].astype(bf16), cst["c_im"].astype(bf16), cst["s5_d"], cst["glu_w"], cst["glu_b"]],
                 [(BRANCH, f32), (BRANCH, bf16)])
    return out, (xr, xi, y)


def _s5_bwd(tag, saved, proj, cst, d_out, rows):
    xr, xi, y = saved
    c_re = cst["c_re"].astype(bf16)
    c_im = cst["c_im"].astype(bf16)

    def body(i, do_ref, y_ref, u_ref, xr_ref, xi_ref, cre_ref, cim_ref, gw_ref, gb_ref,
             dxr_ref, dxi_ref, dy_ref, dgw_ref, dgb_ref, dd_ref, dcre_ref, dcim_ref):
        yv = y_ref[...]
        z = _gelu(yv)
        zb = z.astype(bf16)
        gt = _sigmoid(_dot(zb, gw_ref[...]) + gb_ref[...])
        dov = do_ref[...]
        dzg = dov * z * gt * (1.0 - gt)
        dzgb = dzg.astype(bf16)
        dz = dov * gt + _dot_nt(dzgb, gw_ref[...])
        dgw_ref[...] += _dot_tn(zb, dzgb)
        dgb_ref[...] += jnp.sum(dzg, axis=0, keepdims=True)
        dy = dz * _gelu_grad(yv)
        dy_ref[...] = dy
        dd_ref[...] += jnp.sum(dy * u_ref[...], axis=0, keepdims=True)
        dyb = dy.astype(bf16)
        dxr_ref[...] = _dot_nt(dyb, cre_ref[...])
        dxi_ref[...] = _dot_nt(dyb, cim_ref[...])
        dcre_ref[...] += _dot_tn(xr_ref[...].astype(bf16), dyb)
        dcim_ref[...] += _dot_tn(xi_ref[...].astype(bf16), dyb)

    dxr, dxi, dy, d_gw, d_gb, d_d, d_cre, d_cim = _rt(
        "s5_out_bwd_" + tag, body, rows, ROW_TILE,
        [(d_out, BRANCH, 0), (y, BRANCH, 0), (proj, BRANCH, U_COL), (xr, S5_LANES, 0), (xi, S5_LANES, 0)],
        [c_re, c_im, cst["glu_w"], cst["glu_b"]],
        [(S5_LANES, f32), (S5_LANES, f32), (BRANCH, f32)],
        acc_outs=[(BRANCH, BRANCH), (1, BRANCH), (1, BRANCH), (S5_LANES, BRANCH), (S5_LANES, BRANCH)])

    gr, gi, d_ar, d_ai = _s5_scan_bwd(tag, dxr, dxi, xr, xi, cst["a_re"], cst["a_im"], rows)
    b_re = cst["b_re"].astype(bf16)
    b_im = cst["b_im"].astype(bf16)

    def body_in(i, gr_ref, gi_ref, dy_ref, u_ref, bre_ref, bim_ref, d_ref, du_ref, dbre_ref, dbim_ref):
        grv = gr_ref[...]
        giv = gi_ref[...]
        du = _dot_nt(grv, bre_ref[...]) + _dot_nt(giv, bim_ref[...]) + dy_ref[...] * d_ref[...]
        du_ref[...] = du.astype(bf16)
        ub = u_ref[...].astype(bf16)
        dbre_ref[...] += _dot_tn(ub, grv)
        dbim_ref[...] += _dot_tn(ub, giv)

    du, d_bre, d_bim = _rt("s5_in_bwd_" + tag, body_in, rows, ROW_TILE,
                           [(gr, S5_LANES, 0), (gi, S5_LANES, 0), (dy, BRANCH, 0), (proj, BRANCH, U_COL)],
                           [b_re, b_im, cst["s5_d"]], [(BRANCH, bf16)],
                           acc_outs=[(BRANCH, S5_LANES), (BRANCH, S5_LANES)])
    dcst = {"b_re": d_bre, "b_im": d_bim, "a_re": d_ar, "a_im": d_ai, "c_re": d_cre, "c_im": d_cim,
            "s5_d": d_d, "glu_b": d_gb}
    return du, dcst, d_gw


def _hg_prep(q, z, lb):
    qs = _sigmoid(q)
    qh = q * qs
    sg = _sigmoid_small(z)
    fg = lb + (1.0 - lb) * sg
    kk = (1.0 - lb) * (1.0 - sg)
    return qs, qh, sg, fg, kk


def _hg_fwd(tag, proj, cst, rows):
    tm = min(ROW_TILE, rows)
    c_sz = HG_CHUNK
    nch = tm // c_sz
    n_chunks = rows // c_sz

    def body(i, q_ref, z_ref, v_ref, g_ref, lb_ref, nw_ref, out_ref, o_ref, ss_ref, sn_ref, st_s):
        @pl.when(i == 0)
        def _():
            st_s[...] = jnp.zeros_like(st_s)

        lb = lb_ref[...]
        _, qh, sg, fg, kk = _hg_prep(q_ref[...], z_ref[...], lb)
        b = _seg_cumsum(jnp.log(fg), tm, c_sz)
        qhat = (qh * jnp.exp(b)).astype(bf16)
        khat = (kk * jnp.exp(-b)).astype(bf16)
        vb = v_ref[...].astype(bf16)
        b3 = b.reshape(nch, c_sz, BRANCH)
        bl3 = b3[:, c_sz - 1:c_sz, :]
        kdec = (kk.reshape(nch, c_sz, BRANCH) * jnp.exp(bl3 - b3)).astype(bf16)
        ebl = jnp.exp(bl3)
        tril = (lax.broadcasted_iota(jnp.int32, (nch, c_sz, c_sz), 1)
                >= lax.broadcasted_iota(jnp.int32, (nch, c_sz, c_sz), 2))
        o_heads = []
        for h in range(HG_HEADS):
            hl = slice(h * HG_DK, (h + 1) * HG_DK)
            q3 = qhat[:, hl].reshape(nch, c_sz, HG_DK)
            k3 = khat[:, hl].reshape(nch, c_sz, HG_DK)
            v3 = vb[:, hl].reshape(nch, c_sz, HG_DK)
            a_mat = jnp.where(tril, _bdot_nt(q3, k3), 0.0).astype(bf16)
            o3 = _bdot(a_mat, v3)
            st = st_s[hl, :]
            before = []
            for ci in range(nch):
                before.append(st.astype(bf16))
                st = st * ebl[ci][:, hl] + _dot_tn(v3[ci], kdec[ci][:, hl])
                sn_ref[ci, hl, :] = st.astype(bf16)
            st_s[hl, :] = st
            s3 = jnp.stack(before)
            ss_ref[:, hl, :] = s3
            o3 = o3 + _bdot_nt(q3, s3)
            o_heads.append(o3.reshape(tm, HG_DK))
        o = jnp.concatenate(o_heads, axis=1)
        o_ref[...] = o
        r = lax.rsqrt(_head_mean(o * o) + EPS)
        g = g_ref[...]
        out_ref[...] = (o * r * nw_ref[...] * (g * _sigmoid(g))).astype(bf16)

    out, o, ss, sn = _rt(
        "hg_fwd_" + tag, body, rows, tm,
        [(proj, BRANCH, U_COL + 1), (proj, BRANCH, U_COL + 2), (proj, BRANCH, U_COL + 3), (proj, BRANCH, U_COL + 4)],
        [cst["hg_lb"], cst["hg_nw"]],
        [(BRANCH, bf16), (BRANCH, f32),
         ((n_chunks, BRANCH, HG_DK), (nch, BRANCH, HG_DK), lambda t: (t, 0, 0), bf16),
         ((n_chunks, BRANCH, HG_DK), (nch, BRANCH, HG_DK), lambda t: (t, 0, 0), bf16)],
        scratch=[pltpu.VMEM((BRANCH, HG_DK), f32)])
    return out, (o, ss, sn)


def _hg_bwd(tag, saved, proj, cst, d_out, rows):
    o_saved, ss, sn = saved
    tm = min(ROW_TILE, rows)
    c_sz = HG_CHUNK
    nch = tm // c_sz

    def body(i, do_ref, q_ref, z_ref, v_ref, g_ref, o_ref, ss_ref, sn_ref, lb_ref, nw_ref,
             dq_ref, dz_ref, dv_ref, dg_ref, dlb_ref, dnw_ref, dst_s):
        @pl.when(i == 0)
        def _():
            dst_s[...] = jnp.zeros_like(dst_s)

        lb = lb_ref[...]
        q = q_ref[...]
        qs, qh, sg, fg, kk = _hg_prep(q, z_ref[...], lb)
        b = _seg_cumsum(jnp.log(fg), tm, c_sz)
        eb = jnp.exp(b)
        enb = jnp.exp(-b)
        qhat = (qh * eb).astype(bf16)
        khat = (kk * enb).astype(bf16)
        vb = v_ref[...].astype(bf16)
        b3 = b.reshape(nch, c_sz, BRANCH)
        bl3 = b3[:, c_sz - 1:c_sz, :]
        dec3 = jnp.exp(bl3 - b3)
        kdec = (kk.reshape(nch, c_sz, BRANCH) * dec3).astype(bf16)
        ebl = jnp.exp(bl3)
        g = g_ref[...]
        gs = _sigmoid(g)
        o = o_ref[...]
        r = lax.rsqrt(_head_mean(o * o) + EPS)
        oh = o * r
        nw = nw_ref[...]
        dov = do_ref[...]
        don = dov * (g * gs)
        dg_ref[...] = (dov * oh * nw * (gs * (1.0 + g * (1.0 - gs)))).astype(bf16)
        dnw_ref[...] += jnp.sum(don * oh, axis=0, keepdims=True)
        doh = don * nw
        d_o = r * (doh - oh * _head_mean(doh * oh))
        dob = d_o.astype(bf16)
        t_idx = lax.broadcasted_iota(jnp.int32, (nch, c_sz, c_sz), 1)
        s_idx = lax.broadcasted_iota(jnp.int32, (nch, c_sz, c_sz), 2)
        heads = []
        for h in range(HG_HEADS):
            hl = slice(h * HG_DK, (h + 1) * HG_DK)
            q3 = qhat[:, hl].reshape(nch, c_sz, HG_DK)
            k3 = khat[:, hl].reshape(nch, c_sz, HG_DK)
            v3 = vb[:, hl].reshape(nch, c_sz, HG_DK)
            do3 = dob[:, hl].reshape(nch, c_sz, HG_DK)
            s3 = ss_ref[:, hl, :]
            da_mat = jnp.where(t_idx >= s_idx, _bdot_nt(do3, v3), 0.0).astype(bf16)
            a_t = jnp.where(t_idx <= s_idx, _bdot_nt(k3, q3), 0.0).astype(bf16)
            da_t = jnp.where(t_idx <= s_idx, _bdot_nt(v3, do3), 0.0).astype(bf16)
            dqhat = _bdot(do3, s3) + _bdot(da_mat, k3)
            dkhat = _bdot(da_t, q3)
            dst = dst_s[hl, :]
            after = [None] * nch
            for ci in reversed(range(nch)):
                after[ci] = dst
                dst = dst * ebl[ci][:, hl] + _dot_tn(do3[ci], q3[ci])
            dst_s[hl, :] = dst
            ds3 = jnp.stack(after)
            ds3b = ds3.astype(bf16)
            dk_inter = _bdot(v3, ds3b) * dec3[:, :, hl]
            dv3 = _bdot(a_t, do3) + _bdot_nt(kdec[:, :, hl], ds3b)
            flux = jnp.sum(sn_ref[:, hl, :].astype(f32) * ds3, axis=1, keepdims=True)
            heads.append((dqhat.reshape(tm, HG_DK), dkhat.reshape(tm, HG_DK), dk_inter.reshape(tm, HG_DK),
                          dv3.reshape(tm, HG_DK), jnp.broadcast_to(flux, (nch, c_sz, HG_DK)).reshape(tm, HG_DK)))
        dqhat, dkhat, dk_inter, dv, flux = (jnp.concatenate(parts, axis=1) for parts in zip(*heads))
        dv_ref[...] = dv.astype(bf16)
        dqh = dqhat * eb
        dk = dkhat * enb + dk_inter
        db = qhat.astype(f32) * dqhat - khat.astype(f32) * dkhat - kk * dk_inter
        dlf = _seg_rev_cumsum(db, tm, c_sz) + flux
        tt = (1.0 - lb) * sg * (1.0 - sg)
        dz_ref[...] = (dlf * tt / fg - dk * tt).astype(bf16)
        dlb_ref[...] += jnp.sum(dlf * (1.0 - sg) / fg - dk * (1.0 - sg), axis=0, keepdims=True)
        dq_ref[...] = (dqh * (qs * (1.0 + q * (1.0 - qs)))).astype(bf16)

    dq, dz, dv, dg, d_lb, d_nw = _rt(
        "hg_bwd_" + tag, body, rows, tm,
        [(d_out, BRANCH, 0), (proj, BRANCH, U_COL + 1), (proj, BRANCH, U_COL + 2), (proj, BRANCH, U_COL + 3),
         (proj, BRANCH, U_COL + 4), (o_saved, BRANCH, 0), (ss, (nch, BRANCH, HG_DK), lambda t: (t, 0, 0)),
         (sn, (nch, BRANCH, HG_DK), lambda t: (t, 0, 0))],
        [cst["hg_lb"], cst["hg_nw"]],
        [(BRANCH, bf16)] * 4, acc_outs=[(1, BRANCH), (1, BRANCH)],
        scratch=[pltpu.VMEM((BRANCH, HG_DK), f32)],
        reverse=True)
    return dq, dz, dv, dg, {"hg_lb": d_lb, "hg_nw": d_nw}


def _rg_gates(xc, wa_ref, ba_ref, wx_ref, bx_ref, sp8):
    xcb = xc.astype(bf16)
    r = _sigmoid(_dot(xcb, wa_ref[...]) + ba_ref[...])
    ig = _sigmoid(_dot(xcb, wx_ref[...]) + bx_ref[...])
    la = -sp8 * r
    a = jnp.exp(la)
    mult = jnp.sqrt(-_expm1(2.0 * la))
    return xcb, r, ig, a, mult


def _rg_fwd(tag, proj, cst, rows):
    tm = min(ROW_TILE, rows)

    def body(i, xb_ref, gate_ref, cw_ref, cb_ref, wa_ref, ba_ref, wx_ref, bx_ref, sp_ref,
             out_ref, xc_ref, h_ref, hp_ref, prev_s, hc_s):
        @pl.when(i == 0)
        def _():
            prev_s[...] = jnp.zeros_like(prev_s)
            hc_s[...] = jnp.zeros_like(hc_s)

        row = _rows((tm, BRANCH))
        xb = xb_ref[...]
        prev = prev_s[...]
        xc = cb_ref[...] + cw_ref[3:4, :] * xb
        for j in range(1, 4):
            sh = jnp.where(row >= j, pltpu.roll(xb, j, 0), pltpu.roll(prev, j, 0))
            xc = xc + cw_ref[3 - j:4 - j, :] * sh
        prev_s[...] = xb
        xc_ref[...] = xc
        _, r, ig, a, mult = _rg_gates(xc, wa_ref, ba_ref, wx_ref, bx_ref, sp_ref[...])
        bb = mult * ig * xc
        hc = hc_s[...]
        row8 = _rows((SUBLANES, BRANCH))
        for g in range(tm // SUBLANES):
            sl = slice(g * SUBLANES, (g + 1) * SUBLANES)
            a_cum, h_loc = _scan_fwd(a[sl], bb[sl], SUBLANES)
            h = h_loc + a_cum * hc
            h_ref[sl, :] = h
            hp_ref[sl, :] = jnp.where(row8 >= 1, pltpu.roll(h, 1, 0), hc)
            hc = h[SUBLANES - 1:SUBLANES, :]
        hc_s[...] = hc
        out_ref[...] = (h_ref[...] * _gelu(gate_ref[...])).astype(bf16)

    out, xc, h, hp = _rt(
        "rg_fwd_" + tag, body, rows, tm,
        [(proj, BRANCH, U_COL + 5), (proj, BRANCH, U_COL + 6)],
        [cst["rg_cw"], cst["rg_cb"], cst["rg_wa"].astype(bf16), cst["rg_ba"], cst["rg_wx"].astype(bf16),
         cst["rg_bx"], cst["rg_sp8"]],
        [(BRANCH, bf16), (BRANCH, f32), (BRANCH, f32), (BRANCH, f32)],
        scratch=[pltpu.VMEM((tm, BRANCH), f32), pltpu.VMEM((1, BRANCH), f32)])
    return out, (xc, h, hp)


def _rg_bwd(tag, saved, proj, cst, d_out, rows):
    xc_saved, h_saved, hp_saved = saved
    tm = min(ROW_TILE, rows)

    def body(i, do_ref, xb_ref, gate_ref, xc_ref, h_ref, hp_ref, cw_ref, wa_ref, ba_ref, wx_ref, bx_ref, sp_ref,
             dxb_ref, dgate_ref, dcw_ref, dcb_ref, dwa_ref, dba_ref, dwx_ref, dbx_ref, dsp_ref,
             nxt_s, ec_s, gt_s):
        @pl.when(i == 0)
        def _():
            nxt_s[...] = jnp.zeros_like(nxt_s)
            ec_s[...] = jnp.zeros_like(ec_s)

        row = _rows((tm, BRANCH))
        xc = xc_ref[...]
        sp8 = sp_ref[...]
        xcb, r, ig, a, mult = _rg_gates(xc, wa_ref, ba_ref, wx_ref, bx_ref, sp8)
        gate = gate_ref[...]
        dov = do_ref[...]
        dh = dov * _gelu(gate)
        dgate_ref[...] = (dov * h_ref[...] * _gelu_grad(gate)).astype(bf16)
        adh = a * dh
        ec = ec_s[...]
        last8 = _rows((SUBLANES, BRANCH)) == SUBLANES - 1
        for g in reversed(range(tm // SUBLANES)):
            sl = slice(g * SUBLANES, (g + 1) * SUBLANES)
            a_cum, e_loc = _scan_bwd(a[sl], adh[sl], SUBLANES)
            e = e_loc + a_cum * ec
            gt_s[sl, :] = dh[sl] + jnp.where(last8, ec, pltpu.roll(e, SUBLANES - 1, 0))
            ec = e[0:1, :]
        ec_s[...] = ec
        g_tot = gt_s[...]
        d_a = g_tot * hp_ref[...]
        d_mult = g_tot * ig * xc
        d_ix = g_tot * mult
        d_ig = d_ix * xc
        d_xc = d_ix * ig
        d_la = d_a * a - d_mult * (a * a) / mult
        d_r = -d_la * sp8
        dsp_ref[...] += jnp.sum(-d_la * r, axis=0, keepdims=True)
        dzr = d_r * r * (1.0 - r)
        dzi = d_ig * ig * (1.0 - ig)
        dzrb = dzr.astype(bf16)
        dzib = dzi.astype(bf16)
        d_xc = d_xc + _dot_nt(dzrb, wa_ref[...]) + _dot_nt(dzib, wx_ref[...])
        dwa_ref[...] += _dot_tn(xcb, dzrb)
        dwx_ref[...] += _dot_tn(xcb, dzib)
        dba_ref[...] += jnp.sum(dzr, axis=0, keepdims=True)
        dbx_ref[...] += jnp.sum(dzi, axis=0, keepdims=True)
        dcb_ref[...] += jnp.sum(d_xc, axis=0, keepdims=True)
        nxt = nxt_s[...]
        xb = xb_ref[...]
        dxb = cw_ref[3:4, :] * d_xc
        dcw_ref[3:4, :] += jnp.sum(d_xc * xb, axis=0, keepdims=True)
        for j in range(1, 4):
            sh = jnp.where(row < tm - j, pltpu.roll(d_xc, tm - j, 0), pltpu.roll(nxt, tm - j, 0))
            dxb = dxb + cw_ref[3 - j:4 - j, :] * sh
            dcw_ref[3 - j:4 - j, :] += jnp.sum(sh * xb, axis=0, keepdims=True)
        nxt_s[...] = d_xc
        dxb_ref[...] = dxb.astype(bf16)

    wa = cst["rg_wa"].astype(bf16)
    wx = cst["rg_wx"].astype(bf16)
    dxb, dgate, d_cw, d_cb, d_wa, d_ba, d_wx, d_bx, d_sp = _rt(
        "rg_bwd_" + tag, body, rows, tm,
        [(d_out, BRANCH, 0), (proj, BRANCH, U_COL + 5), (proj, BRANCH, U_COL + 6), (xc_saved, BRANCH, 0),
         (h_saved, BRANCH, 0), (hp_saved, BRANCH, 0)],
        [cst["rg_cw"], wa, cst["rg_ba"], wx, cst["rg_bx"], cst["rg_sp8"]],
        [(BRANCH, bf16), (BRANCH, bf16)],
        acc_outs=[(4, BRANCH), (1, BRANCH), (BRANCH, BRANCH), (1, BRANCH), (BRANCH, BRANCH), (1, BRANCH), (1, BRANCH)],
        scratch=[pltpu.VMEM((tm, BRANCH), f32), pltpu.VMEM((1, BRANCH), f32), pltpu.VMEM((tm, BRANCH), f32)],
        reverse=True)
    dcst = {"rg_cw": d_cw, "rg_cb": d_cb, "rg_wa": d_wa, "rg_ba": d_ba, "rg_wx": d_wx, "rg_bx": d_bx, "rg_sp8": d_sp}
    return dxb, dgate, dcst


def _merge_fwd(tag, proj, outs, bp, rows):
    def body(i, ya_ref, yb_ref, yc_ref, gm_ref, p_ref, m_ref):
        acc = None
        for n, y_ref in enumerate((ya_ref, yb_ref, yc_ref)):
            up = _dot(y_ref[...], p_ref[n])
            term = _sigmoid(gm_ref[:, n * D_MODEL:(n + 1) * D_MODEL]) * up
            acc = term if acc is None else acc + term
        m_ref[...] = acc.astype(bf16)
    return _rt("merge_fwd_" + tag, body, rows, ROW_TILE,
               [(outs[0], BRANCH, 0), (outs[1], BRANCH, 0), (outs[2], BRANCH, 0), (proj, GM_WIDTH, 0)],
               [bp], [(D_MODEL, bf16)])[0]


def _merge_bwd(tag, proj, outs, bp, dmerged, rows):
    def body(i, dm_ref, ya_ref, yb_ref, yc_ref, gm_ref, p_ref, da_ref, db_ref, dc_ref, dgm_ref, dp_ref):
        dm = dm_ref[...]
        for n, (y_ref, dy_ref) in enumerate(((ya_ref, da_ref), (yb_ref, db_ref), (yc_ref, dc_ref))):
            yv = y_ref[...]
            up = _dot(yv, p_ref[n])
            gt = _sigmoid(gm_ref[:, n * D_MODEL:(n + 1) * D_MODEL])
            dup = (dm * gt).astype(bf16)
            dgm_ref[:, n * D_MODEL:(n + 1) * D_MODEL] = (dm * up * gt * (1.0 - gt)).astype(bf16)
            dy_ref[...] = _dot_nt(dup, p_ref[n])
            dp_ref[n] += _dot_tn(yv, dup)
    return _rt("merge_bwd_" + tag, body, rows, ROW_TILE,
               [(dmerged, D_MODEL, 0), (outs[0], BRANCH, 0), (outs[1], BRANCH, 0), (outs[2], BRANCH, 0),
                (proj, GM_WIDTH, 0)],
               [bp], [(BRANCH, f32), (BRANCH, f32), (BRANCH, f32), (GM_WIDTH, bf16)],
               acc_outs=[(N_BRANCH, BRANCH, D_MODEL)])


def _block_diag(blocks):
    g, r, c = blocks.shape
    on_diag = (lax.broadcasted_iota(jnp.int32, (g * r, g * c), 0) // r
               == lax.broadcasted_iota(jnp.int32, (g * r, g * c), 1) // c)
    tiled = jnp.broadcast_to(blocks.reshape(g * r, 1, c), (g * r, g, c)).reshape(g * r, g * c)
    return jnp.where(on_diag, tiled, 0.0)


def _prep_consts(sp):
    p = jax.nn.softmax(sp["hg_lb_logits"], axis=0)
    lower = jnp.cumsum(p, axis=0) - p[0]
    out = []
    for l in range(DEPTH):
        lr = jnp.minimum(sp["s5_lambda_re"][l], S5_EIG_MAX)
        li = sp["s5_lambda_im"][l]
        dt = jnp.exp(sp["s5_log_dt"][l])[:, None]
        mag = jnp.exp(lr * dt)
        ar = mag * jnp.cos(li * dt)
        ai = mag * jnp.sin(li * dt)
        den = lr * lr + li * li
        fr = ((ar - 1.0) * lr + ai * li) / den
        fi = (ai * lr - (ar - 1.0) * li) / den
        br, bi = sp["s5_b_re"][l], sp["s5_b_im"][l]
        bbr = fr[..., None] * br - fi[..., None] * bi
        bbi = fr[..., None] * bi + fi[..., None] * br
        c = {
            "a_re": ar.reshape(1, S5_LANES), "a_im": ai.reshape(1, S5_LANES),
            "b_re": _block_diag(bbr.transpose(0, 2, 1)), "b_im": _block_diag(bbi.transpose(0, 2, 1)),
            "c_re": _block_diag(sp["s5_c_re"][l].transpose(0, 2, 1)),
            "c_im": -_block_diag(sp["s5_c_im"][l].transpose(0, 2, 1)),
            "s5_d": sp["s5_d"][l][None], "glu_b": sp["s5_glu_b"][l][None],
            "hg_lb": lower[l][None], "hg_nw": sp["hg_norm_w"][l][None],
            "rg_cw": sp["rg_conv_w"][l], "rg_cb": sp["rg_conv_b"][l][None],
            "rg_wa": _block_diag(sp["rg_wa"][l]), "rg_ba": sp["rg_ba"][l][None],
            "rg_wx": _block_diag(sp["rg_wx"][l]), "rg_bx": sp["rg_bx"][l][None],
            "rg_sp8": (RG_C * jax.nn.softplus(-sp["rg_lambda"][l]))[None],
        }
        out.append(c)
    return out


def _mixer_fwd(tag, x, hb, w_in, bp, w_out, cst, next_nw, rows):
    proj = _mm1("mix_proj_" + tag, hb, w_in, "nn", rows, IN_TOTAL, D_MODEL, 512, IN_TOTAL // 4, D_MODEL)
    cst = dict(cst)
    out_a, sv_a = _s5_fwd(tag, proj, cst, rows)
    out_b, sv_b = _hg_fwd(tag, proj, cst, rows)
    out_c, sv_c = _rg_fwd(tag, proj, cst, rows)
    merged = _merge_fwd(tag, proj, (out_a, out_b, out_c), bp, rows)
    x_out, hb_out = _mm("mix_out_" + tag, [merged], [w_out], [(0, 0, 0)], 1, "nn", rows, D_MODEL, D_MODEL,
                        512, D_MODEL, D_MODEL, [f32, bf16], _residual_then_norm(1.0), extras=[x], vecs=[next_nw])
    return x_out, hb_out, (x, hb, proj, (out_a, out_b, out_c), merged, sv_a, sv_b, sv_c)


def _mixer_bwd(tag, saved, nw, w_in, bp, w_out, cst, dx, dxb, rows):
    x, hb, proj, outs, merged, sv_a, sv_b, sv_c = saved
    d_wout = _mm1("mix_dwout_" + tag, merged, dxb, "tn", D_MODEL, D_MODEL, rows, D_MODEL, D_MODEL, TOKEN_K,
                  out_dtype=bf16)
    dmerged = _mm1("mix_dmerged_" + tag, dxb, w_out, "nt", rows, D_MODEL, D_MODEL, 512, D_MODEL, D_MODEL)
    d_a, d_b, d_c, dgm, d_bp = _merge_bwd(tag, proj, outs, bp, dmerged, rows)
    dxbc, dgatec, dcst_c = _rg_bwd(tag, sv_c, proj, cst, d_c, rows)
    dq, dz, dv, dg, dcst_b = _hg_bwd(tag, sv_b, proj, cst, d_b, rows)
    du, dcst_a, d_glu_w = _s5_bwd(tag, sv_a, proj, cst, d_a, rows)
    dproj = jnp.concatenate([dgm, du, dq, dz, dv, dg, dxbc, dgatec], axis=1)
    d_win = _mm1("mix_dwin_" + tag, hb, dproj, "tn", D_MODEL, IN_TOTAL, rows, D_MODEL, IN_TOTAL // 4, TOKEN_K,
                 out_dtype=bf16)
    dx_in, dxb_in, d_nw = _mm("mix_dh_" + tag, [dproj], [w_in], [(0, 0, 0)], 1, "nt", rows, D_MODEL, IN_TOTAL,
                              512, D_MODEL, IN_TOTAL // 2, [f32, bf16], _norm_bwd_epilogue, extras=[x, dx], vecs=[nw],
                              n_part=1)
    dcst = {**dcst_a, **dcst_b, **dcst_c}
    return dx_in, dxb_in, jnp.sum(d_nw, axis=0), d_win, d_bp, d_wout, d_glu_w, dcst


def _local_step(x, target, weights_of, small, grads_done):
    rows = x.shape[0]
    consts, consts_vjp = jax.vjp(_prep_consts, small)
    norm_w = small["norm_w"]
    saved = []
    h = x
    hb = _rms_fwd("first_norm", x, norm_w[0, 0][None], rows)
    for l in range(DEPTH):
        t = str(l)
        after = [norm_w[l + 1, 0][None]] if l + 1 < DEPTH else []
        wa = weights_of(l, "a")
        h, hb, sv0 = _ffn_fwd(t + "a", h, hb, *wa, [norm_w[l, 1][None]], rows)
        wm = weights_of(l, "mix")
        cst = dict(consts[l])
        cst["glu_w"] = wm[3]
        h, hb, sv1 = _mixer_fwd(t, h, hb, *wm[:3], cst, norm_w[l, 2][None], rows)
        wb = weights_of(l, "b")
        h, hb, sv2 = _ffn_fwd(t + "b", h, hb, *wb, after, rows)
        saved.append((sv0, sv1, sv2, cst, wa, wm, wb))
    dx, dxb, loss, d_fnw = _loss_head(h, small["final_norm_w"][None], target, rows)
    d_norm = [None] * DEPTH
    d_consts = [None] * DEPTH
    for l in reversed(range(DEPTH)):
        t = str(l)
        sv0, sv1, sv2, cst, wa, wm, wb = saved[l]
        dx, dxb, dn2, dg1, du1, dd1 = _ffn_bwd(t + "b", sv2, norm_w[l, 2][None], *wb, dx, dxb, rows)
        grads_done(l, "b", [dg1, du1, dd1])
        dx, dxb, dn1, d_win, d_bp, d_wout, d_glu_w, dcst = _mixer_bwd(
            t, sv1, norm_w[l, 1][None], *wm[:3], cst, dx, dxb, rows)
        grads_done(l, "mix", [d_win, d_bp, d_wout, d_glu_w])
        dx, dxb, dn0, dg0, du0, dd0 = _ffn_bwd(t + "a", sv0, norm_w[l, 0][None], *wa, dx, dxb, rows)
        grads_done(l, "a", [dg0, du0, dd0])
        d_norm[l] = jnp.concatenate([dn0, dn1, dn2], axis=0)
        d_consts[l] = dcst
    (g_small,) = consts_vjp(d_consts)
    g_small = dict(g_small)
    g_small["norm_w"] = g_small["norm_w"] + jnp.stack(d_norm)
    g_small["final_norm_w"] = g_small["final_norm_w"] + d_fnw[0]
    return loss[0, 0], dx, g_small


def _other_chips(x, y):
    return [(1 - x, y), (x, 1 - y), (1 - x, 1 - y)]


def _gather_over_ici(shards):
    n = len(shards)

    def copies(in_refs, out_refs, send_sems, recv_sems):
        x, y, c = _place()
        cps = []
        for i in range(n):
            mine = out_refs[i].at[4 * x + 2 * y + c]
            cps.append(pltpu.make_async_copy(in_refs[i], mine, send_sems.at[5 * i + 4]))
            for k, to in enumerate([(x, y, 1 - c)] + [(px, py, c) for px, py in _other_chips(x, y)]):
                cps.append(pltpu.make_async_remote_copy(
                    src_ref=in_refs[i], dst_ref=mine, send_sem=send_sems.at[5 * i + k],
                    recv_sem=recv_sems.at[5 * i + k], device_id=to, device_id_type=MESH_IDS))
        return cps

    return _Carry(shards, [jax.ShapeDtypeStruct((N_DEV,) + s.shape, s.dtype) for s in shards], 5 * n, copies)


def _gather_forward(name, landings):
    n = len(landings)

    def body(*refs):
        in_refs, out_refs = refs[:n], refs[n:2 * n]
        send_sems, recv_sems = refs[2 * n:]
        x, y, c = _place()
        cps = []
        for i in range(n):
            for j, (px, py) in enumerate(_other_chips(x, y)):
                block = 4 * px + 2 * py + c
                cps.append(pltpu.make_async_remote_copy(
                    src_ref=in_refs[i].at[block], dst_ref=out_refs[i].at[block], send_sem=send_sems.at[3 * i + j],
                    recv_sem=recv_sems.at[3 * i + j], device_id=(x, y, 1 - c), device_id_type=MESH_IDS))
        for cp in cps:
            cp.start()
        for cp in cps:
            cp.wait()

    return pl.pallas_call(
        body, name=name, out_shape=[jax.ShapeDtypeStruct(a.shape, a.dtype) for a in landings],
        in_specs=[ANY_SPEC] * n, out_specs=[ANY_SPEC] * n, input_output_aliases={i: i for i in range(n)},
        scratch_shapes=[pltpu.SemaphoreType.DMA((3 * n,)), pltpu.SemaphoreType.DMA((3 * n,))],
    )(*landings)


def _all_gather(name, shards):
    n = len(shards)

    def body(*refs):
        x_refs, out_refs = refs[:n], refs[n:2 * n]
        send_sems, recv_sems, local_sems = refs[2 * n:]
        x, y, c = _place()
        me, sibling = (x, y, c), (x, y, 1 - c)
        chips = [(1 - x, y), (x, 1 - y), (1 - x, 1 - y)]

        def blk(i, px, py, pc):
            return out_refs[i].at[4 * px + 2 * py + pc]

        def copy(i, k, block, to, src=None):
            return pltpu.make_async_remote_copy(
                src_ref=blk(i, *block) if src is None else src, dst_ref=blk(i, *block),
                send_sem=send_sems.at[7 * i + k], recv_sem=recv_sems.at[7 * i + k], device_id=to,
                device_id_type=MESH_IDS)

        mine = [pltpu.make_async_copy(x_refs[i], blk(i, *me), local_sems.at[i]) for i in range(n)]
        for cp in mine:
            cp.start()
        first = []
        for i in range(n):
            first.append(copy(i, 0, me, sibling, src=x_refs[i]))
            first += [copy(i, 1 + j, me, (*chip, c), src=x_refs[i]) for j, chip in enumerate(chips)]
        for cp in first:
            cp.start()
        passed = []
        for j, chip in enumerate(chips):
            for i in range(n):
                copy(i, 1 + j, (*chip, c), me).wait_recv()
                fwd = copy(i, 4 + j, (*chip, c), sibling)
                fwd.start()
                passed.append(fwd)
        for i in range(n):
            copy(i, 0, sibling, me).wait_recv()
            for j, chip in enumerate(chips):
                copy(i, 4 + j, (*chip, 1 - c), me).wait_recv()
        for cp in first + passed:
            cp.wait_send()
        for cp in mine:
            cp.wait()

    return pl.pallas_call(
        body, name=name, out_shape=[jax.ShapeDtypeStruct((N_DEV,) + s.shape, s.dtype) for s in shards],
        in_specs=[ANY_SPEC] * n, out_specs=[ANY_SPEC] * n,
        scratch_shapes=[pltpu.SemaphoreType.DMA((7 * n,)), pltpu.SemaphoreType.DMA((7 * n,)),
                        pltpu.SemaphoreType.DMA((n,))],
    )(*shards)


def _row_tile(rows):
    return rows if rows <= 512 else next(t for t in range(512, 7, -8) if rows % t == 0)


def _sums_over_ici(chip_sums):
    n = len(chip_sums)

    def copies(in_refs, out_refs, send_sems, recv_sems):
        x, y, c = _place()
        return [pltpu.make_async_remote_copy(
            src_ref=in_refs[i].at[2 * px + py], dst_ref=out_refs[i].at[k], send_sem=send_sems.at[3 * i + k],
            recv_sem=recv_sems.at[3 * i + k], device_id=(px, py, c), device_id_type=MESH_IDS)
            for i in range(n) for k, (px, py) in enumerate(_other_chips(x, y))]

    return _Carry(chip_sums, [jax.ShapeDtypeStruct((3,) + t.shape[1:], t.dtype) for t in chip_sums], 3 * n, copies)


def _reduce_scatter(tag, parts, hosts=None):
    n = len(parts)
    _, _, c = _place()

    def body_pair(*refs):
        p_refs, got_refs = refs[:n], refs[n:2 * n]
        send_sems, recv_sems = refs[2 * n:]
        x, y, c = _place()
        cps = [pltpu.make_async_remote_copy(
            src_ref=p_refs[i].at[:, 1 - c], dst_ref=got_refs[i], send_sem=send_sems.at[i], recv_sem=recv_sems.at[i],
            device_id=(x, y, 1 - c), device_id_type=MESH_IDS) for i in range(n)]
        for cp in cps:
            cp.start()
        for cp in cps:
            cp.wait()

    from_sibling = pl.pallas_call(
        body_pair, name="rs_pair_" + tag,
        out_shape=[jax.ShapeDtypeStruct((4,) + p.shape[2:], p.dtype) for p in parts],
        in_specs=[ANY_SPEC] * n, out_specs=[ANY_SPEC] * n,
        scratch_shapes=[pltpu.SemaphoreType.DMA((n,)), pltpu.SemaphoreType.DMA((n,))],
    )(*parts)

    def body_add(idx_ref, *refs):
        p = pl.program_id(0)
        for q in range(n):
            @pl.when(p == q)
            def _(p_ref=refs[q], g_ref=refs[n + q], o_ref=refs[2 * n + q]):
                o_ref[...] = (p_ref[...].astype(f32) + g_ref[...].astype(f32)).astype(o_ref.dtype)

    def at(q):
        return lambda p, j, idx: jnp.clip(j + 4 * (p - q), 0, 3)

    in_specs, out_specs = [], []
    for q, part in enumerate(parts):
        in_specs.append(pl.BlockSpec((None, None) + part.shape[2:],
                                     lambda p, j, idx, blk=at(q): (blk(p, j, idx), idx[0], 0, 0)))
    for q, part in enumerate(parts):
        spec = pl.BlockSpec((None,) + part.shape[2:], lambda p, j, idx, blk=at(q): (blk(p, j, idx), 0, 0))
        in_specs.append(spec)
        out_specs.append(spec)
    chip_sums = pl.pallas_call(
        body_add, name="rs_pair_sum_" + tag,
        out_shape=[jax.ShapeDtypeStruct((4,) + p.shape[2:], p.dtype) for p in parts],
        grid_spec=pltpu.PrefetchScalarGridSpec(num_scalar_prefetch=1, grid=(n, 4), in_specs=in_specs,
                                               out_specs=out_specs),
        compiler_params=_cparams(("arbitrary", "arbitrary")),
    )(jnp.stack([c]).astype(jnp.int32), *parts, *from_sibling)

    others = [None] * n
    riding = set()
    for host, which in (hosts or {}).items():
        rider = _sums_over_ici([chip_sums[i] for i in which])
        _CARRIED[host] = rider
        for pos, i in enumerate(which):
            others[i] = functools.partial(lambda r, p: r.outs[p], rider, pos)
        riding.update(which)
    rest = [i for i in range(n) if i not in riding]
    if rest:
        alone = _sums_over_ici([chip_sums[i] for i in rest])

        def body_chips(*refs):
            k = len(rest)
            cps = alone.copies(refs[:k], refs[k:2 * k], *refs[2 * k:])
            for cp in cps:
                cp.start()
            for cp in cps:
                cp.wait()

        from_chips = pl.pallas_call(
            body_chips, name="rs_chips_" + tag, out_shape=alone.out_shapes,
            in_specs=[ANY_SPEC] * len(rest), out_specs=[ANY_SPEC] * len(rest), scratch_shapes=alone.sems(),
        )(*alone.ins)
        for pos, i in enumerate(rest):
            others[i] = functools.partial(lambda got: got, from_chips[pos])
    return list(zip(chip_sums, others))


def _own_index():
    x, y, _ = _place()
    return jnp.stack([2 * x + y]).astype(jnp.int32)


def _own_total(name, chip_sum, others):
    _, r, cols = chip_sum.shape
    tr = _row_tile(r)

    def body(idx_ref, t_ref, g_ref, o_ref):
        o_ref[...] = ((t_ref[...].astype(f32) + g_ref[0].astype(f32)) + g_ref[1].astype(f32)) + g_ref[2].astype(f32)

    return pl.pallas_call(
        body, name=name, out_shape=jax.ShapeDtypeStruct((r, cols), f32),
        grid_spec=pltpu.PrefetchScalarGridSpec(
            num_scalar_prefetch=1, grid=(r // tr,),
            in_specs=[pl.BlockSpec((None, tr, cols), lambda t, idx: (idx[0], t, 0)),
                      pl.BlockSpec((3, tr, cols), lambda t, idx: (0, t, 0))],
            out_specs=pl.BlockSpec((tr, cols), lambda t, idx: (t, 0))),
        compiler_params=_cparams(("parallel",)),
    )(_own_index(), chip_sum, others)


def _adam_update(w, gv, m, v):
    m_new = ADAM_B1 * m + (1.0 - ADAM_B1) * gv
    v_new = ADAM_B2 * v + (1.0 - ADAM_B2) * (gv * gv)
    m_hat = m_new / (1.0 - ADAM_B1 ** ADAM_STEP)
    v_hat = v_new / (1.0 - ADAM_B2 ** ADAM_STEP)
    return -ADAM_LR * (m_hat / (jnp.sqrt(v_hat) + ADAM_EPS) + ADAM_WD * w), m_new, v_new


def _adamw_reduced(name, w, pieces, m, v):
    n_p, rows, cols = w.shape
    tr = _row_tile(rows)

    def body(idx_ref, w_ref, *refs):
        red = refs[:2 * n_p]
        m_ref, v_ref, g_ref, d_ref, nm_ref, nv_ref = refs[2 * n_p:]
        p = pl.program_id(0)
        for q in range(n_p):
            @pl.when(p == q)
            def _(t_ref=red[2 * q], o_ref=red[2 * q + 1]):
                gv = ((t_ref[...].astype(f32) + o_ref[0].astype(f32)) + o_ref[1].astype(f32)) + o_ref[2].astype(f32)
                g_ref[...] = gv
                d_ref[...], nm_ref[...], nv_ref[...] = _adam_update(w_ref[...], gv, m_ref[...], v_ref[...])

    spec = pl.BlockSpec((None, tr, cols), lambda p, t, idx: (p, t, 0))
    red_specs, red_args = [], []
    for q, (chip_sum, others) in enumerate(pieces):
        red_specs.append(pl.BlockSpec((None, tr, cols), lambda p, t, idx, q=q: (idx[0], jnp.where(p == q, t, 0), 0)))
        red_specs.append(pl.BlockSpec((3, tr, cols), lambda p, t, idx, q=q: (0, jnp.where(p == q, t, 0), 0)))
        red_args += [chip_sum, others]
    return pl.pallas_call(
        body, name=name, out_shape=[jax.ShapeDtypeStruct((n_p, rows, cols), f32)] * 4,
        grid_spec=pltpu.PrefetchScalarGridSpec(
            num_scalar_prefetch=1, grid=(n_p, rows // tr),
            in_specs=[spec] + red_specs + [spec, spec], out_specs=[spec] * 4),
        compiler_params=_cparams(("parallel", "parallel")),
    )(_own_index(), w, *red_args, m, v)


def _adamw(name, w, g, m, v):
    rows, cols = w.shape
    tr = _row_tile(rows)

    def body(w_ref, g_ref, m_ref, v_ref, d_ref, nm_ref, nv_ref):
        d_ref[...], nm_ref[...], nv_ref[...] = _adam_update(w_ref[...], g_ref[...], m_ref[...], v_ref[...])

    spec = pl.BlockSpec((tr, cols), lambda i: (i, 0))
    return pl.pallas_call(
        body, name=name, grid=(rows // tr,), in_specs=[spec] * 4, out_specs=[spec] * 3,
        out_shape=[jax.ShapeDtypeStruct((rows, cols), f32)] * 3, compiler_params=_cparams(("parallel",)),
    )(w, g, m, v)


WEIGHT_NAMES = ["norm_w", "final_norm_w", "ffn_gate", "ffn_up", "ffn_down", "w_in", "branch_proj", "w_out",
                "s5_lambda_re", "s5_lambda_im", "s5_log_dt", "s5_b_re", "s5_b_im", "s5_c_re", "s5_c_im", "s5_d",
                "s5_glu_w", "s5_glu_b", "hg_lb_logits", "hg_norm_w", "rg_conv_w", "rg_conv_b", "rg_wa", "rg_ba",
                "rg_wx", "rg_bx", "rg_lambda"]
SHARDED = {"ffn_gate": (3, "gate"), "ffn_up": (3, "up"), "ffn_down": (2, "down"), "w_in": (2, "w_in"),
           "branch_proj": (3, "bp"), "w_out": (1, "w_out"), "s5_glu_w": (1, "glu_w"),
           "norm_w": (2, None), "rg_conv_w": (2, None)}
BIG = ["ffn_gate", "ffn_up", "ffn_down", "w_in", "branch_proj", "w_out", "s5_glu_w"]
TRANSPOSED = ("ffn_gate", "ffn_up")
PARTS = {"a": [("ffn_gate", 0, 0), ("ffn_up", 0, 0), ("ffn_down", 0, 0)],
         "b": [("ffn_gate", 1, 0), ("ffn_up", 1, 0), ("ffn_down", 1, 0)],
         "mix": [("w_in", None, 1), ("branch_proj", None, 2), ("w_out", None, 0), ("s5_glu_w", None, 0)]}
AG_HOSTS = {"ffn_up_0a": (0, "mix", [0]), "ffn_down_0a": (0, "mix", [1, 2, 3]), "mix_proj_0": (0, "b", [0, 1, 2]),
            "s5_out_0": (1, "a", [0]), "hg_fwd_0": (1, "a", [1]), "rg_fwd_0": (1, "a", [2]),
            "merge_fwd_0": (1, "b", [0]), "ffn_up_0b": (1, "b", [1]), "ffn_down_0b": (1, "b", [2]),
            "s5_scan_fwd_0": (1, "mix", [0, 1]), "mix_out_0": (1, "mix", [2, 3])}
RS_HOSTS = {(1, "b"): {"s5_scan_bwd_1": [0, 1, 2]},
            (1, "mix"): {"ffn_bwd_mid_1a": [1, 2, 3], "mix_dh_0": [0]},
            (1, "a"): {"mix_dwin_0": [0, 1], "merge_bwd_0": [2]},
            (0, "b"): {"s5_scan_bwd_0": [0, 1, 2]},
            (0, "mix"): {"ffn_bwd_mid_0a": [1, 2, 3], "ffn_dh_0a": [0]}}
SMALL_SHARDED = ["norm_w", "rg_conv_w"]
REPLICATED = [n for n in WEIGHT_NAMES if n not in SHARDED]
LANES = 128


PACK_ROWS = 512


def _pack_rows(arrays, names):
    pieces = []
    for n in names:
        flat = arrays[n].reshape(-1)
        pieces.append(jnp.pad(flat, (0, -flat.shape[0] % LANES)).reshape(-1, LANES))
    rows = jnp.concatenate(pieces, axis=0)
    return jnp.pad(rows, ((0, -rows.shape[0] % PACK_ROWS), (0, 0)))


def _unpack_rows(rows, names, like):
    out, r0 = {}, 0
    for n in names:
        size = math.prod(like[n].shape)
        nrows = -(-size // LANES)
        out[n] = rows[r0:r0 + nrows].reshape(-1)[:size].reshape(like[n].shape)
        r0 += nrows
    return out


def _unshard(gathered, axis):
    g = jnp.moveaxis(gathered, 0, axis)
    shp = g.shape
    return g.reshape(shp[:axis] + (shp[axis] * shp[axis + 1],) + shp[axis + 2:])


RELAYOUT_ROWS = 256


def _column_runs(width, first_col):
    total = N_DEV * width
    runs = []
    for j in range(N_DEV):
        start = (width * j + first_col) % total
        head = min(width, total - start)
        runs.append((j, 0, start, head))
        if head < width:
            runs.append((j, head, 0, width - head))
    return runs


def _unshard_columns(name, gathered, first_col=0):
    _, r, c = gathered.shape
    tr = min(RELAYOUT_ROWS, r)
    runs = _column_runs(c, first_col)

    def body(g_ref, o_ref):
        for j, off, dst, length in runs:
            o_ref[:, dst:dst + length] = g_ref[j, :, off:off + length]

    return pl.pallas_call(
        body, name=name, grid=(r // tr,), in_specs=[pl.BlockSpec((N_DEV, tr, c), lambda i: (0, i, 0))],
        out_specs=pl.BlockSpec((tr, N_DEV * c), lambda i: (i, 0)),
        out_shape=jax.ShapeDtypeStruct((r, N_DEV * c), gathered.dtype), compiler_params=_cparams(("parallel",)),
    )(gathered)


def _columns_to_blocks(name, full, first_col=0):
    r, total = full.shape
    c = total // N_DEV
    tr = min(RELAYOUT_ROWS, r)
    runs = _column_runs(c, first_col)

    def body(x_ref, o_ref):
        for j, off, src, length in runs:
            o_ref[j // 2, j % 2, :, off:off + length] = x_ref[:, src:src + length].astype(bf16)

    return pl.pallas_call(
        body, name=name, grid=(r // tr,), in_specs=[pl.BlockSpec((tr, total), lambda i: (i, 0))],
        out_specs=pl.BlockSpec((4, 2, tr, c), lambda i: (0, 0, i, 0)),
        out_shape=jax.ShapeDtypeStruct((4, 2, r, c), bf16), compiler_params=_cparams(("parallel",)),
    )(full)


def _to_blocks(full, axis):
    shp = full.shape
    g = full.reshape(shp[:axis] + (4, 2, shp[axis] // N_DEV) + shp[axis + 1:])
    g = jnp.moveaxis(g, (axis, axis + 1), (0, 1))
    return g.reshape(4, 2, -1, g.shape[-1])


W_IN_SPLIT = IN_TOTAL - GM_WIDTH


def kernel(x, norm_w, final_norm_w, ffn_gate, ffn_up, ffn_down, w_in, branch_proj, w_out, s5_lambda_re, s5_lambda_im, s5_log_dt, s5_b_re, s5_b_im, s5_c_re, s5_c_im, s5_d, s5_glu_w, s5_glu_b, hg_lb_logits, hg_norm_w, rg_conv_w, rg_conv_b, rg_wa, rg_ba, rg_wx, rg_bx, rg_lambda, loss_target, m_norm_w, m_final_norm_w, m_ffn_gate, m_ffn_up, m_ffn_down, m_w_in, m_branch_proj, m_w_out, m_s5_lambda_re, m_s5_lambda_im, m_s5_log_dt, m_s5_b_re, m_s5_b_im, m_s5_c_re, m_s5_c_im, m_s5_d, m_s5_glu_w, m_s5_glu_b, m_hg_lb_logits, m_hg_norm_w, m_rg_conv_w, m_rg_conv_b, m_rg_wa, m_rg_ba, m_rg_wx, m_rg_bx, m_rg_lambda, v_norm_w, v_final_norm_w, v_ffn_gate, v_ffn_up, v_ffn_down, v_w_in, v_branch_proj, v_w_out, v_s5_lambda_re, v_s5_lambda_im, v_s5_log_dt, v_s5_b_re, v_s5_b_im, v_s5_c_re, v_s5_c_im, v_s5_d, v_s5_glu_w, v_s5_glu_b, v_hg_lb_logits, v_hg_norm_w, v_rg_conv_w, v_rg_conv_b, v_rg_wa, v_rg_ba, v_rg_wx, v_rg_bx, v_rg_lambda):
    w = dict(zip(WEIGHT_NAMES, (norm_w, final_norm_w, ffn_gate, ffn_up, ffn_down, w_in, branch_proj, w_out,
                                s5_lambda_re, s5_lambda_im, s5_log_dt, s5_b_re, s5_b_im, s5_c_re, s5_c_im, s5_d,
                                s5_glu_w, s5_glu_b, hg_lb_logits, hg_norm_w, rg_conv_w, rg_conv_b, rg_wa, rg_ba,
                                rg_wx, rg_bx, rg_lambda)))
    m = dict(zip(WEIGHT_NAMES, (m_norm_w, m_final_norm_w, m_ffn_gate, m_ffn_up, m_ffn_down, m_w_in, m_branch_proj,
                                m_w_out, m_s5_lambda_re, m_s5_lambda_im, m_s5_log_dt, m_s5_b_re, m_s5_b_im, m_s5_c_re,
                                m_s5_c_im, m_s5_d, m_s5_glu_w, m_s5_glu_b, m_hg_lb_logits, m_hg_norm_w, m_rg_conv_w,
                                m_rg_conv_b, m_rg_wa, m_rg_ba, m_rg_wx, m_rg_bx, m_rg_lambda)))
    v = dict(zip(WEIGHT_NAMES, (v_norm_w, v_final_norm_w, v_ffn_gate, v_ffn_up, v_ffn_down, v_w_in, v_branch_proj,
                                v_w_out, v_s5_lambda_re, v_s5_lambda_im, v_s5_log_dt, v_s5_b_re, v_s5_b_im, v_s5_c_re,
                                v_s5_c_im, v_s5_d, v_s5_glu_w, v_s5_glu_b, v_hg_lb_logits, v_hg_norm_w, v_rg_conv_w,
                                v_rg_conv_b, v_rg_wa, v_rg_ba, v_rg_wx, v_rg_bx, v_rg_lambda)))
    rows = x.shape[1]

    _CARRIED.clear()

    def shard_of(piece, l):
        n, k, _ = piece
        shard = w[n][l] if k is None else w[n][l, k]
        return (jnp.swapaxes(shard, 0, 1) if n in TRANSPOSED else shard).astype(bf16)

    def assemble(l, part, gathered):
        full = []
        for j, (piece, g) in enumerate(zip(PARTS[part], gathered)):
            tag = "unshard_%d%s%d" % (l, part, j)
            if piece[0] == "w_in":
                full.append(_unshard_columns(tag, g, first_col=GM_WIDTH))
            elif piece[0] == "branch_proj":
                full.append(_unshard_columns(tag, g.reshape(N_DEV, -1, g.shape[-1])).reshape(N_BRANCH, BRANCH, D_MODEL))
            elif piece[2] == g.ndim - 2:
                full.append(_unshard_columns(tag, g))
            else:
                full.append(_unshard(g, piece[2]))
        return full

    n_a = len(PARTS["a"])
    first = _all_gather("gather_weights", [shard_of(p, 0) for p in PARTS["a"]] + [w[n] for n in SMALL_SHARDED])
    small = {n: w[n] for n in REPLICATED}
    for n, g in zip(SMALL_SHARDED, first[n_a:]):
        small[n] = _unshard(g, SHARDED[n][0])
    riders = {}
    for host, (l, part, which) in AG_HOSTS.items():
        rider = _gather_over_ici([shard_of(PARTS[part][j], l) for j in which])
        _CARRIED[host] = rider
        riders.setdefault((l, part), []).append((which, rider))

    forwarded = {}

    def weights_of(l, part):
        if (l, part) == (0, "a"):
            return assemble(l, part, first[:n_a])
        group = [(l, p) for p in ("a", "mix", "b")] if l == 1 else [(l, part)]
        if (l, part) not in forwarded:
            landed = {key: [None] * len(PARTS[key[1]]) for key in group}
            for key in group:
                for which, rider in riders[key]:
                    for j, buf in zip(which, rider.outs):
                        landed[key][j] = buf
            done = _gather_forward("gather_forward_%d%s" % (l, "" if l == 1 else part),
                                   [buf for key in group for buf in landed[key]])
            for key in group:
                forwarded[key], done = done[:len(PARTS[key[1]])], done[len(PARTS[key[1]]):]
        return assemble(l, part, forwarded[l, part])

    sums = {}

    def blocks_of(l, part, grads):
        out = []
        for j, (piece, g) in enumerate(zip(PARTS[part], grads)):
            tag = "to_blocks_%d%s%d" % (l, part, j)
            if piece[0] == "w_in":
                out.append(_columns_to_blocks(tag, g, first_col=GM_WIDTH))
            elif piece[0] == "branch_proj":
                out.append(_columns_to_blocks(tag, g.reshape(-1, g.shape[-1])))
            elif piece[2] == g.ndim - 1:
                out.append(_columns_to_blocks(tag, g))
            else:
                out.append(_to_blocks(g, piece[2]).astype(bf16))
        return out

    last_grads = []

    def grads_done(l, part, grads):
        if (l, part) in RS_HOSTS:
            sums[l, part] = _reduce_scatter("%d%s" % (l, part), blocks_of(l, part, grads), hosts=RS_HOSTS[l, part])
        else:
            last_grads.extend(blocks_of(l, part, grads))

    loss_part, dx, g_small = _local_step(x[0], loss_target[0], weights_of, small, grads_done)
    loss = lax.psum(loss_part, ("x", "y", "c"))

    parts = last_grads + [_to_blocks(g_small[n], SHARDED[n][0]) for n in SMALL_SHARDED]
    rep_rows = _pack_rows(g_small, REPLICATED)
    rep_slice = rep_rows.shape[0] // N_DEV
    parts.append(rep_rows.reshape(4, 2, rep_slice, LANES))
    last = _reduce_scatter("last", parts)
    sums[0, "a"] = last[:n_a]

    grads, delta, new_m, new_v = {}, {}, {}, {}

    def update(n, pieces):
        def view(a):
            a = jnp.swapaxes(a, -1, -2) if n in TRANSPOSED else a
            return a.reshape(len(pieces), -1, a.shape[-1])

        def back(r):
            shp = w[n].shape
            if n in TRANSPOSED:
                return jnp.swapaxes(r.reshape(shp[:-2] + (shp[-1], shp[-2])), -1, -2)
            return r.reshape(shp)

        res = _adamw_reduced("adamw_" + n, view(w[n]), [(t, others()) for t, others in pieces], view(m[n]), view(v[n]))
        grads[n], delta[n], new_m[n], new_v[n] = (back(r) for r in res)

    for n in BIG:
        update(n, [sums[l, part][j] for l in range(DEPTH) for part in ("a", "b", "mix")
                   for j, piece in enumerate(PARTS[part]) if piece[0] == n])
    for j, n in enumerate(SMALL_SHARDED):
        update(n, [last[n_a + j]])
    rep_mine = _own_total("rs_total_small", last[-1][0], last[-1][1]())
    rep_grads = _all_gather("gather_small_grads", [rep_mine])[0].reshape(-1, LANES)
    res = _adamw("adamw_small", _pack_rows(w, REPLICATED), rep_grads, _pack_rows(m, REPLICATED), _pack_rows(v, REPLICATED))
    for dst, src in zip((grads, delta, new_m, new_v), (rep_grads,) + tuple(res)):
        dst.update(_unpack_rows(src, REPLICATED, w))

    return (loss, dx.reshape(x.shape), *[grads[n] for n in WEIGHT_NAMES], *[delta[n] for n in WEIGHT_NAMES],
            *[new_m[n] for n in WEIGHT_NAMES], *[new_v[n] for n in WEIGHT_NAMES])
```

```python
import functools
import math

import jax
import jax.numpy as jnp
from jax import lax
from jax.experimental import pallas as pl
from jax.experimental.pallas import tpu as pltpu

f32 = jnp.float32
bf16 = jnp.bfloat16

D_MODEL = 1024
DEPTH = 2
BRANCH = 512
N_BRANCH = 3
S5_GROUP = 16
S5_GROUPS = 32
S5_STATE = 64
S5_LANES = S5_GROUPS * S5_STATE
S5_EIG_MAX = -1e-4
HG_HEADS = 4
HG_DK = 128
HG_CHUNK = 32
RG_BLOCKS = 8
RG_BLOCK = 64
RG_C = 8.0
D_FF = 2816
EPS = 1e-6
IN_TOTAL = 6656
GM_WIDTH = N_BRANCH * D_MODEL
N_DEV = 8

ADAM_LR = 0.001
ADAM_B1 = 0.9
ADAM_B2 = 0.999
ADAM_EPS = 1e-08
ADAM_WD = 0.01
ADAM_STEP = 10

VMEM_LIMIT_V7X = 56 * 1024 * 1024
ROW_TILE = 256
FF_TILE = 1408
TOKEN_K = 4096
MXU_COLS = 256


def _cparams(sem):
    return pltpu.CompilerParams(dimension_semantics=sem, vmem_limit_bytes=VMEM_LIMIT_V7X)


MESH_IDS = pl.DeviceIdType.MESH
ANY_SPEC = pl.BlockSpec(memory_space=pl.ANY)


def _place():
    return lax.axis_index("x"), lax.axis_index("y"), lax.axis_index("c")


class _Carry:
    def __init__(self, ins, out_shapes, n_sems, copies):
        self.ins, self.out_shapes, self.n_sems, self.copies = list(ins), list(out_shapes), n_sems, copies
        self.outs = None

    def sems(self):
        return [pltpu.SemaphoreType.DMA((self.n_sems,)), pltpu.SemaphoreType.DMA((self.n_sems,))]

    def start(self, when, *riders):
        @pl.when(when)
        def _():
            for cp in self.copies(*riders):
                cp.start()

    def finish(self, when, *riders):
        @pl.when(when)
        def _():
            for cp in self.copies(*riders):
                cp.wait()


_CARRIED = {}


def _call_with_rider(name, body, grid, in_specs, out_specs, out_shape, scratch, semantics, args):
    carry = _CARRIED.pop(name, None)
    if carry is None:
        return pl.pallas_call(body, name=name, grid=grid, in_specs=in_specs, out_specs=out_specs,
                              out_shape=out_shape, scratch_shapes=scratch, compiler_params=_cparams(semantics))(*args)
    n_in, n_out, nci, nco = len(in_specs), len(out_specs), len(carry.ins), len(carry.out_shapes)

    def kern(*refs):
        ids = [pl.program_id(d) for d in range(len(grid))]
        own = refs[:n_in] + refs[n_in + nci:n_in + nci + n_out] + refs[n_in + nci + n_out + nco:-2]
        riders = (refs[n_in:n_in + nci], refs[n_in + nci + n_out:n_in + nci + n_out + nco]) + tuple(refs[-2:])
        carry.start(functools.reduce(jnp.logical_and, [p == 0 for p in ids]), *riders)
        body(*own)
        carry.finish(functools.reduce(jnp.logical_and, [p == g - 1 for p, g in zip(ids, grid)]), *riders)

    res = pl.pallas_call(
        kern, name=name, grid=grid, in_specs=list(in_specs) + [ANY_SPEC] * nci,
        out_specs=list(out_specs) + [ANY_SPEC] * nco, out_shape=list(out_shape) + carry.out_shapes,
        scratch_shapes=list(scratch) + carry.sems(), compiler_params=_cparams(("arbitrary",) * len(grid)),
    )(*args, *carry.ins)
    carry.outs = res[n_out:]
    return res[:n_out]


def _sigmoid(x):
    return 0.5 * jnp.tanh(0.5 * x) + 0.5


def _sigmoid_small(x):
    return 1.0 / (1.0 + jnp.exp(-x))


_GELU_C = math.sqrt(2.0 / math.pi)


def _gelu(x):
    t = jnp.tanh(_GELU_C * (x + 0.044715 * x * x * x))
    return 0.5 * x * (1.0 + t)


def _gelu_grad(x):
    t = jnp.tanh(_GELU_C * (x + 0.044715 * x * x * x))
    return 0.5 * (1.0 + t) + 0.5 * x * (1.0 - t * t) * _GELU_C * (1.0 + 3.0 * 0.044715 * x * x)


def _expm1(x):
    p = x * (1.0 + x * (0.5 + x * (1.0 / 6 + x * (1.0 / 24 + x * (1.0 / 120 + x * (1.0 / 720))))))
    return jnp.where(jnp.abs(x) < 0.3, p, jnp.exp(x) - 1.0)


def _dot(a, b):
    return jnp.dot(a, b, preferred_element_type=f32)


def _dot_nt(a, b):
    return lax.dot_general(a, b, (((1,), (1,)), ((), ())), preferred_element_type=f32)


def _dot_tn(a, b):
    return lax.dot_general(a, b, (((0,), (0,)), ((), ())), preferred_element_type=f32)


def _bdot(a, b):
    return lax.dot_general(a, b, (((2,), (1,)), ((0,), (0,))), preferred_element_type=f32)


def _bdot_nt(a, b):
    return lax.dot_general(a, b, (((2,), (2,)), ((0,), (0,))), preferred_element_type=f32)


def _rows(shape):
    return lax.broadcasted_iota(jnp.int32, shape, 0)


def _scan_fwd(a, b, n):
    row = _rows(a.shape)
    s = 1
    while s < n:
        valid = row >= s
        sh_a = pltpu.roll(a, s, 0)
        sh_b = pltpu.roll(b, s, 0)
        b = b + a * jnp.where(valid, sh_b, 0.0)
        a = a * jnp.where(valid, sh_a, 1.0)
        s *= 2
    return a, b


def _scan_bwd(a, b, n):
    row = _rows(a.shape)
    s = 1
    while s < n:
        valid = row < n - s
        sh_a = pltpu.roll(a, n - s, 0)
        sh_b = pltpu.roll(b, n - s, 0)
        b = b + a * jnp.where(valid, sh_b, 0.0)
        a = a * jnp.where(valid, sh_a, 1.0)
        s *= 2
    return a, b


def _seg_cumsum(x, n, seg):
    pos = _rows(x.shape) % seg
    s = 1
    while s < seg:
        x = x + jnp.where(pos >= s, pltpu.roll(x, s, 0), 0.0)
        s *= 2
    return x


def _seg_rev_cumsum(x, n, seg):
    pos = _rows(x.shape) % seg
    s = 1
    while s < seg:
        x = x + jnp.where(pos < seg - s, pltpu.roll(x, n - s, 0), 0.0)
        s *= 2
    return x


def _head_mean(x):
    parts = []
    for h in range(HG_HEADS):
        m = jnp.mean(x[:, h * HG_DK:(h + 1) * HG_DK], axis=1, keepdims=True)
        parts.append(jnp.broadcast_to(m, (x.shape[0], HG_DK)))
    return jnp.concatenate(parts, axis=1)


def _mm(name, a_list, b_list, terms, n_acc, mode, m, n, k, tm, tn, tk, out_dtypes, epilogue, extras=(), vecs=(),
        n_part=0, chunk=0):
    tm, tn, tk = min(tm, m), min(tn, n), min(tk, k)
    assert m % tm == 0 and n % tn == 0 and k % tk == 0, (name, m, n, k, tm, tn, tk)
    gk = k // tk
    if mode == "tn":
        a_spec = pl.BlockSpec((tk, tm), lambda i, j, kk: (kk, i))
    else:
        a_spec = pl.BlockSpec((tm, tk), lambda i, j, kk: (i, kk))
    if mode == "nt":
        b_spec = pl.BlockSpec((tn, tk), lambda i, j, kk: (j, kk))
    else:
        b_spec = pl.BlockSpec((tk, tn), lambda i, j, kk: (kk, j))
    o_spec = pl.BlockSpec((tm, tn), lambda i, j, kk: (i, j))
    v_spec = pl.BlockSpec((1, tn), lambda i, j, kk: (0, j))
    p_spec = pl.BlockSpec((None, 1, tn), lambda i, j, kk: (i, 0, j))
    dot = {"nn": _dot, "nt": _dot_nt, "tn": _dot_tn}[mode]
    na, nb, ne, nv, no = len(a_list), len(b_list), len(extras), len(vecs), len(out_dtypes)
    carry = _CARRIED.pop(name, None)
    nci, nco = (len(carry.ins), len(carry.out_shapes)) if carry else (0, 0)
    n_in = na + nb + ne + nv + nci
    grid = (m // tm, n // tn, gk)

    def kern(*refs):
        if carry:
            ids = [pl.program_id(d) for d in range(3)]
            riders = (refs[n_in - nci:n_in], refs[n_in + no + n_part:n_in + no + n_part + nco]) + tuple(refs[-2:])
            carry.start(functools.reduce(jnp.logical_and, [p == 0 for p in ids]), *riders)
        compute(*refs)
        if carry:
            carry.finish(functools.reduce(jnp.logical_and, [p == g - 1 for p, g in zip(ids, grid)]), *riders)

    def compute(*refs):
        a_refs = refs[:na]
        b_refs = refs[na:na + nb]
        e_refs = refs[na + nb:na + nb + ne]
        v_refs = refs[na + nb + ne:na + nb + ne + nv]
        o_refs = refs[n_in:n_in + no + n_part]

        def finish(accs):
            outs = epilogue(accs, [e[...] for e in e_refs], [r[...] for r in v_refs])
            for o, val in zip(o_refs, outs):
                o[...] = val.astype(o.dtype)

        def partial_sums():
            sums = [None] * n_acc
            for ai, bi, ci in terms:
                d = dot(a_refs[ai][...].astype(bf16), b_refs[bi][...].astype(bf16))
                sums[ci] = d if sums[ci] is None else sums[ci] + d
            return sums

        if gk == 1 and chunk:
            assert mode in ("nn", "nt") and tn % chunk == 0
            for c0 in range(0, tn, chunk):
                cols = slice(c0, c0 + chunk)
                sums = [None] * n_acc
                for ai, bi, ci in terms:
                    b_part = b_refs[bi][:, cols] if mode == "nn" else b_refs[bi][cols, :]
                    d = dot(a_refs[ai][...].astype(bf16), b_part.astype(bf16))
                    sums[ci] = d if sums[ci] is None else sums[ci] + d
                outs = epilogue(sums, [e[:, cols] for e in e_refs], [r[:, cols] for r in v_refs])
                for o, val in zip(o_refs, outs):
                    o[:, cols] = val.astype(o.dtype)
            return
        if gk == 1:
            finish(partial_sums())
            return
        acc = refs[n_in + no + n_part + nco]
        kk = pl.program_id(2)

        @pl.when(kk == 0)
        def _():
            acc[...] = jnp.zeros_like(acc)

        for ci, d in enumerate(partial_sums()):
            acc[ci] += d

        @pl.when(kk == gk - 1)
        def _():
            finish([acc[c] for c in range(n_acc)])

    res = pl.pallas_call(
        kern, name=name,
        grid=grid,
        in_specs=[a_spec] * na + [b_spec] * nb + [o_spec] * ne + [v_spec] * nv + [ANY_SPEC] * nci,
        out_specs=[o_spec] * no + [p_spec] * n_part + [ANY_SPEC] * nco,
        out_shape=([jax.ShapeDtypeStruct((m, n), dt) for dt in out_dtypes]
                   + [jax.ShapeDtypeStruct((m // tm, 1, n), f32)] * n_part + (carry.out_shapes if carry else [])),
        scratch_shapes=([pltpu.VMEM((n_acc, tm, tn), f32)] if gk > 1 else []) + (carry.sems() if carry else []),
        compiler_params=_cparams(("arbitrary",) * 3 if carry else ("parallel", "parallel", "arbitrary")),
    )(*a_list, *b_list, *extras, *vecs, *(carry.ins if carry else []))
    if carry:
        carry.outs = res[no + n_part:]
        res = res[:no + n_part]
    return res


def _mm1(name, a, b, mode, m, n, k, tm, tn, tk, out_dtype=f32, scale=None):
    def epi(accs, extras, vecs):
        return [accs[0] if scale is None else accs[0] * scale]
    return _mm(name, [a], [b], [(0, 0, 0)], 1, mode, m, n, k, tm, tn, tk, [out_dtype], epi)[0]


def _rt(name, body, rows, tm, row_ins, consts, row_outs, acc_outs=(), scratch=(), reverse=False):
    tm = min(tm, rows)
    assert rows % tm == 0
    nt = rows // tm

    def tile(i):
        return nt - 1 - i if reverse else i

    in_specs, args = [], []
    for spec in row_ins:
        arr = spec[0]
        if isinstance(spec[1], int):
            in_specs.append(pl.BlockSpec((tm, spec[1]), lambda i, cb=spec[2]: (tile(i), cb)))
        else:
            in_specs.append(pl.BlockSpec(spec[1], lambda i, fn=spec[2]: fn(tile(i))))
        args.append(arr)
    for c in consts:
        in_specs.append(pl.BlockSpec(c.shape, lambda i, nd=c.ndim: (0,) * nd))
        args.append(c)
    out_specs, out_shape = [], []
    for spec in row_outs:
        if isinstance(spec[0], int):
            out_specs.append(pl.BlockSpec((tm, spec[0]), lambda i: (tile(i), 0)))
            out_shape.append(jax.ShapeDtypeStruct((rows, spec[0]), spec[1]))
        else:
            out_specs.append(pl.BlockSpec(spec[1], lambda i, fn=spec[2]: fn(tile(i))))
            out_shape.append(jax.ShapeDtypeStruct(spec[0], spec[3]))
    for shp in acc_outs:
        out_specs.append(pl.BlockSpec(shp, lambda i, nd=len(shp): (0,) * nd))
        out_shape.append(jax.ShapeDtypeStruct(shp, f32))
    n_in = len(args)
    n_row_out = len(row_outs)
    n_acc = len(acc_outs)
    n_out = n_row_out + n_acc
    carry = _CARRIED.pop(name, None)
    nci, nco = (len(carry.ins), len(carry.out_shapes)) if carry else (0, 0)

    def kern(*refs):
        i = pl.program_id(0)
        if carry:
            own = refs[:n_in] + refs[n_in + nci:n_in + nci + n_out] + refs[n_in + nci + n_out + nco:-2]
            riders = (refs[n_in:n_in + nci], refs[n_in + nci + n_out:n_in + nci + n_out + nco]) + tuple(refs[-2:])
            carry.start(i == 0, *riders)
        else:
            own = refs
        acc_refs = own[n_in + n_row_out:n_in + n_out]

        @pl.when(i == 0)
        def _():
            for r in acc_refs:
                r[...] = jnp.zeros_like(r)

        body(i, *own)
        if carry:
            carry.finish(i == nt - 1, *riders)

    res = pl.pallas_call(
        kern, name=name, grid=(nt,), in_specs=in_specs + [ANY_SPEC] * nci, out_specs=out_specs + [ANY_SPEC] * nco,
        out_shape=out_shape + (carry.out_shapes if carry else []),
        scratch_shapes=list(scratch) + (carry.sems() if carry else []), compiler_params=_cparams(("arbitrary",)),
    )(*args, *(carry.ins if carry else []))
    if carry:
        carry.outs = res[n_out:]
        res = res[:n_out]
    return res


def _rms_rows(xv, wv):
    r = lax.rsqrt(jnp.mean(xv * xv, axis=1, keepdims=True) + EPS)
    return (xv * r * wv).astype(bf16)


def _rms_bwd_rows(xv, dhv, wv, dres):
    r = lax.rsqrt(jnp.mean(xv * xv, axis=1, keepdims=True) + EPS)
    xn = xv * r
    dxn = dhv * wv
    dx = dres + r * (dxn - xn * jnp.mean(dxn * xn, axis=1, keepdims=True))
    return [dx, dx.astype(bf16), jnp.sum(dhv * xn, axis=0, keepdims=True)]


def _rms_fwd(name, x, w, rows):
    def body(i, x_ref, w_ref, h_ref):
        h_ref[...] = _rms_rows(x_ref[...], w_ref[...])
    return _rt(name, body, rows, ROW_TILE, [(x, D_MODEL, 0)], [w], [(D_MODEL, bf16)])[0]


def _residual_then_norm(scale):
    def epi(accs, extras, vecs):
        x_out = extras[0] + scale * accs[0]
        return [x_out] + [_rms_rows(x_out, v) for v in vecs]
    return epi


def _norm_bwd_epilogue(accs, extras, vecs):
    return _rms_bwd_rows(extras[0], accs[0], vecs[0], extras[1])


def _loss_head(x, w, target, rows):
    def body(i, x_ref, t_ref, w_ref, dx_ref, dxb_ref, loss_ref, dw_ref):
        xv = x_ref[...]
        r = lax.rsqrt(jnp.mean(xv * xv, axis=1, keepdims=True) + EPS)
        xn = xv * r
        wv = w_ref[...]
        err = xn * wv - t_ref[...]
        part = 0.5 * jnp.sum(jnp.mean(err * err, axis=1, keepdims=True), axis=0, keepdims=True)
        loss_ref[...] += jnp.broadcast_to(part, (1, 128))
        dy = err * (1.0 / D_MODEL)
        dxn = dy * wv
        dx = r * (dxn - xn * jnp.mean(dxn * xn, axis=1, keepdims=True))
        dx_ref[...] = dx
        dxb_ref[...] = dx.astype(bf16)
        dw_ref[...] += jnp.sum(dy * xn, axis=0, keepdims=True)
    return _rt("loss_head", body, rows, ROW_TILE, [(x, D_MODEL, 0), (target, D_MODEL, 0)], [w],
               [(D_MODEL, f32), (D_MODEL, bf16)], acc_outs=[(1, 128), (1, D_MODEL)])


def _ffn_fwd(tag, x, hb, wg_t, wu_t, wd, next_nw, rows):
    def epi_up(accs, extras, vecs):
        a, b = accs
        return [a, b, a * _sigmoid(a) * b]
    a, b, s = _mm("ffn_up_" + tag, [hb], [wg_t, wu_t], [(0, 0, 0), (0, 1, 1)], 2, "nt", rows, D_FF, D_MODEL,
                  512, D_FF, D_MODEL, [bf16, bf16, bf16], epi_up, chunk=MXU_COLS)
    outs = _mm("ffn_down_" + tag, [s], [wd], [(0, 0, 0)], 1, "nn", rows, D_MODEL, D_FF,
               512, D_MODEL, D_FF, [f32] + [bf16] * len(next_nw), _residual_then_norm(0.5), extras=[x],
               vecs=next_nw)
    return outs[0], (outs[1] if next_nw else None), (x, hb, a, b, s)


def _ffn_bwd(tag, saved, nw, wg_t, wu_t, wd, dx, dxb, rows):
    x, hb, a, b, s = saved

    def epi_mid(accs, extras, vecs):
        ds = 0.5 * accs[0]
        av = extras[0].astype(f32)
        bv = extras[1].astype(f32)
        sg = _sigmoid(av)
        return [ds * bv * sg * (1.0 + av * (1.0 - sg)), ds * av * sg]
    da, db = _mm("ffn_bwd_mid_" + tag, [dxb], [wd], [(0, 0, 0)], 1, "nt", rows, D_FF, D_MODEL,
                 512, D_FF, D_MODEL, [bf16, bf16], epi_mid, extras=[a, b], chunk=MXU_COLS)
    d_wd = _mm1("ffn_dwd_" + tag, s, dxb, "tn", D_FF, D_MODEL, rows, FF_TILE, D_MODEL, TOKEN_K, out_dtype=bf16,
                scale=0.5)
    d_wg_t = _mm1("ffn_dwg_" + tag, da, hb, "tn", D_FF, D_MODEL, rows, FF_TILE, D_MODEL, TOKEN_K, out_dtype=bf16)
    d_wu_t = _mm1("ffn_dwu_" + tag, db, hb, "tn", D_FF, D_MODEL, rows, FF_TILE, D_MODEL, TOKEN_K, out_dtype=bf16)
    dx_in, dxb_in, d_nw = _mm("ffn_dh_" + tag, [da, db], [wg_t, wu_t], [(0, 0, 0), (1, 1, 0)], 1, "nn", rows,
                              D_MODEL, D_FF, 512, D_MODEL, D_FF, [f32, bf16], _norm_bwd_epilogue, extras=[x, dx],
                              vecs=[nw], n_part=1)
    return dx_in, dxb_in, jnp.sum(d_nw, axis=0), d_wg_t, d_wu_t, d_wd


S5_CB = 512
SUBLANES = 8
U_COL = GM_WIDTH // BRANCH


def _s5_scan_fwd(tag, proj, b_re, b_im, a_re, a_im, rows):
    tm = min(ROW_TILE, rows)
    nt = rows // tm
    nc = S5_LANES // S5_CB

    def kern(u_ref, bre_ref, bim_ref, ar_ref, ai_ref, xr_ref, xi_ref, pr_s, pi_s, cr_s, ci_s, mr_s, mi_s):
        t = pl.program_id(1)

        @pl.when(t == 0)
        def _():
            row8 = _rows((SUBLANES, S5_CB))
            pr = jnp.broadcast_to(ar_ref[...], (SUBLANES, S5_CB))
            pi = jnp.broadcast_to(ai_ref[...], (SUBLANES, S5_CB))
            s = 1
            while s < SUBLANES:
                sr = pltpu.roll(pr, s, 0)
                si = pltpu.roll(pi, s, 0)
                valid = row8 >= s
                pr, pi = jnp.where(valid, pr * sr - pi * si, pr), jnp.where(valid, pr * si + pi * sr, pi)
                s *= 2
            pr_s[...] = pr
            pi_s[...] = pi
            for k in range(3):
                s = 1 << k
                mr_s[k] = jnp.where(row8 >= s, pr[s - 1:s, :], 0.0)
                mi_s[k] = jnp.where(row8 >= s, pi[s - 1:s, :], 0.0)
            cr_s[...] = jnp.zeros_like(cr_s)
            ci_s[...] = jnp.zeros_like(ci_s)

        ub = u_ref[...].astype(bf16)
        br = _dot(ub, bre_ref[...])
        bi = _dot(ub, bim_ref[...])
        steps = [(mr_s[k], mi_s[k]) for k in range(3)]
        cr = cr_s[...]
        ci = ci_s[...]
        pr = pr_s[...]
        pi = pi_s[...]
        for g in range(tm // SUBLANES):
            sl = slice(g * SUBLANES, (g + 1) * SUBLANES)
            xr = br[sl]
            xi = bi[sl]
            for k, (mr, mi) in enumerate(steps):
                sr = pltpu.roll(xr, 1 << k, 0)
                si = pltpu.roll(xi, 1 << k, 0)
                xr, xi = xr + (mr * sr - mi * si), xi + (mr * si + mi * sr)
            xr, xi = xr + (pr * cr - pi * ci), xi + (pr * ci + pi * cr)
            xr_ref[sl, :] = xr
            xi_ref[sl, :] = xi
            cr = xr[SUBLANES - 1:SUBLANES, :]
            ci = xi[SUBLANES - 1:SUBLANES, :]
        cr_s[...] = cr
        ci_s[...] = ci

    return _call_with_rider(
        "s5_scan_fwd_" + tag, kern, (nc, nt),
        [pl.BlockSpec((tm, BRANCH), lambda c, t: (t, U_COL)),
         pl.BlockSpec((BRANCH, S5_CB), lambda c, t: (0, c)),
         pl.BlockSpec((BRANCH, S5_CB), lambda c, t: (0, c)),
         pl.BlockSpec((1, S5_CB), lambda c, t: (0, c)),
         pl.BlockSpec((1, S5_CB), lambda c, t: (0, c))],
        [pl.BlockSpec((tm, S5_CB), lambda c, t: (t, c))] * 2,
        [jax.ShapeDtypeStruct((rows, S5_LANES), f32)] * 2,
        [pltpu.VMEM((SUBLANES, S5_CB), f32), pltpu.VMEM((SUBLANES, S5_CB), f32),
         pltpu.VMEM((1, S5_CB), f32), pltpu.VMEM((1, S5_CB), f32),
         pltpu.VMEM((3, SUBLANES, S5_CB), f32), pltpu.VMEM((3, SUBLANES, S5_CB), f32)],
        ("parallel", "arbitrary"), (proj, b_re, b_im, a_re, a_im))


def _s5_scan_bwd(tag, dxr, dxi, xr, xi, a_re, a_im, rows):
    tm = min(ROW_TILE, rows)
    nt = rows // tm
    nc = S5_LANES // S5_CB

    def kern(dxr_ref, dxi_ref, xr_ref, xi_ref, ar_ref, ai_ref, gr_ref, gi_ref, dar_ref, dai_ref,
             qr_s, qi_s, cr_s, ci_s, gr_s, gi_s, mr_s, mi_s):
        t = pl.program_id(1)
        row = _rows((tm, S5_CB))
        ng = tm // SUBLANES

        @pl.when(t == 0)
        def _():
            row8 = _rows((SUBLANES, S5_CB))
            qr = jnp.broadcast_to(ar_ref[...], (SUBLANES, S5_CB))
            qi = jnp.broadcast_to(-ai_ref[...], (SUBLANES, S5_CB))
            s = 1
            while s < SUBLANES:
                sr = pltpu.roll(qr, SUBLANES - s, 0)
                si = pltpu.roll(qi, SUBLANES - s, 0)
                valid = row8 < SUBLANES - s
                qr, qi = jnp.where(valid, qr * sr - qi * si, qr), jnp.where(valid, qr * si + qi * sr, qi)
                s *= 2
            qr_s[...] = qr
            qi_s[...] = qi
            for k in range(3):
                s = 1 << k
                mr_s[k] = jnp.where(row8 < SUBLANES - s, qr[SUBLANES - s:SUBLANES - s + 1, :], 0.0)
                mi_s[k] = jnp.where(row8 < SUBLANES - s, qi[SUBLANES - s:SUBLANES - s + 1, :], 0.0)
            cr_s[...] = jnp.zeros_like(cr_s)
            ci_s[...] = jnp.zeros_like(ci_s)
            dar_ref[...] = jnp.zeros_like(dar_ref)
            dai_ref[...] = jnp.zeros_like(dai_ref)

        steps = [(mr_s[k], mi_s[k]) for k in range(3)]
        cr = cr_s[...]
        ci = ci_s[...]
        qr = qr_s[...]
        qi = qi_s[...]
        last8 = _rows((SUBLANES, S5_CB)) == SUBLANES - 1
        acc_r = jnp.zeros((SUBLANES, S5_CB), f32)
        acc_i = jnp.zeros((SUBLANES, S5_CB), f32)
        for g in reversed(range(ng)):
            sl = slice(g * SUBLANES, (g + 1) * SUBLANES)
            gr = dxr_ref[sl, :]
            gi = dxi_ref[sl, :]
            for k, (mr, mi) in enumerate(steps):
                sr = pltpu.roll(gr, SUBLANES - (1 << k), 0)
                si = pltpu.roll(gi, SUBLANES - (1 << k), 0)
                gr, gi = gr + (mr * sr - mi * si), gi + (mr * si + mi * sr)
            gr, gi = gr + (qr * cr - qi * ci), gi + (qr * ci + qi * cr)
            gr_s[sl, :] = gr
            gi_s[sl, :] = gi
            gnr = jnp.where(last8, cr, pltpu.roll(gr, SUBLANES - 1, 0))
            gni = jnp.where(last8, ci, pltpu.roll(gi, SUBLANES - 1, 0))
            xr_v = xr_ref[sl, :]
            xi_v = xi_ref[sl, :]
            acc_r = acc_r + (gnr * xr_v + gni * xi_v)
            acc_i = acc_i + (gni * xr_v - gnr * xi_v)
            cr = gr[0:1, :]
            ci = gi[0:1, :]
        cr_s[...] = cr
        ci_s[...] = ci
        gr_ref[...] = gr_s[...].astype(bf16)
        gi_ref[...] = gi_s[...].astype(bf16)
        dar_ref[...] += jnp.sum(acc_r, axis=0, keepdims=True)
        dai_ref[...] += jnp.sum(acc_i, axis=0, keepdims=True)

    rev = lambda c, t: (nt - 1 - t, c)
    return _call_with_rider(
        "s5_scan_bwd_" + tag, kern, (nc, nt),
        [pl.BlockSpec((tm, S5_CB), rev)] * 4 + [pl.BlockSpec((1, S5_CB), lambda c, t: (0, c))] * 2,
        [pl.BlockSpec((tm, S5_CB), rev)] * 2 + [pl.BlockSpec((1, S5_CB), lambda c, t: (0, c))] * 2,
        [jax.ShapeDtypeStruct((rows, S5_LANES), bf16)] * 2 + [jax.ShapeDtypeStruct((1, S5_LANES), f32)] * 2,
        [pltpu.VMEM((SUBLANES, S5_CB), f32), pltpu.VMEM((SUBLANES, S5_CB), f32),
         pltpu.VMEM((1, S5_CB), f32), pltpu.VMEM((1, S5_CB), f32),
         pltpu.VMEM((tm, S5_CB), f32), pltpu.VMEM((tm, S5_CB), f32),
         pltpu.VMEM((3, SUBLANES, S5_CB), f32), pltpu.VMEM((3, SUBLANES, S5_CB), f32)],
        ("parallel", "arbitrary"), (dxr, dxi, xr, xi, a_re, a_im))


def _s5_fwd(tag, proj, cst, rows):
    xr, xi = _s5_scan_fwd(tag, proj, cst["b_re"].astype(bf16), cst["b_im"].astype(bf16), cst["a_re"], cst["a_im"], rows)

    def body(i, xr_ref, xi_ref, u_ref, cre_ref, cim_ref, d_ref, gw_ref, gb_ref, y_ref, out_ref):
        y = (_dot(xr_ref[...].astype(bf16), cre_ref[...]) + _dot(xi_ref[...].astype(bf16), cim_ref[...])
             + d_ref[...] * u_ref[...])
        y_ref[...] = y
        z = _gelu(y)
        zg = _dot(z.astype(bf16), gw_ref[...]) + gb_ref[...]
        out_ref[...] = (z * _sigmoid(zg)).astype(bf16)

    y, out = _rt("s5_out_" + tag, body, rows, ROW_TILE,
                 [(xr, S5_LANES, 0), (xi, S5_LANES, 0), (proj, BRANCH, U_COL)],
                 [cst["c_re"].astype(bf16), cst["c_im"].astype(bf16), cst["s5_d"], cst["glu_w"], cst["glu_b"]],
                 [(BRANCH, f32), (BRANCH, bf16)])
    return out, (xr, xi, y)


def _s5_bwd(tag, saved, proj, cst, d_out, rows):
    xr, xi, y = saved
    c_re = cst["c_re"].astype(bf16)
    c_im = cst["c_im"].astype(bf16)

    def body(i, do_ref, y_ref, u_ref, xr_ref, xi_ref, cre_ref, cim_ref, gw_ref, gb_ref,
             dxr_ref, dxi_ref, dy_ref, dgw_ref, dgb_ref, dd_ref, dcre_ref, dcim_ref):
        yv = y_ref[...]
        z = _gelu(yv)
        zb = z.astype(bf16)
        gt = _sigmoid(_dot(zb, gw_ref[...]) + gb_ref[...])
        dov = do_ref[...]
        dzg = dov * z * gt * (1.0 - gt)
        dzgb = dzg.astype(bf16)
        dz = dov * gt + _dot_nt(dzgb, gw_ref[...])
        dgw_ref[...] += _dot_tn(zb, dzgb)
        dgb_ref[...] += jnp.sum(dzg, axis=0, keepdims=True)
        dy = dz * _gelu_grad(yv)
        dy_ref[...] = dy
        dd_ref[...] += jnp.sum(dy * u_ref[...], axis=0, keepdims=True)
        dyb = dy.astype(bf16)
        dxr_ref[...] = _dot_nt(dyb, cre_ref[...])
        dxi_ref[...] = _dot_nt(dyb, cim_ref[...])
        dcre_ref[...] += _dot_tn(xr_ref[...].astype(bf16), dyb)
        dcim_ref[...] += _dot_tn(xi_ref[...].astype(bf16), dyb)

    dxr, dxi, dy, d_gw, d_gb, d_d, d_cre, d_cim = _rt(
        "s5_out_bwd_" + tag, body, rows, ROW_TILE,
        [(d_out, BRANCH, 0), (y, BRANCH, 0), (proj, BRANCH, U_COL), (xr, S5_LANES, 0), (xi, S5_LANES, 0)],
        [c_re, c_im, cst["glu_w"], cst["glu_b"]],
        [(S5_LANES, f32), (S5_LANES, f32), (BRANCH, f32)],
        acc_outs=[(BRANCH, BRANCH), (1, BRANCH), (1, BRANCH), (S5_LANES, BRANCH), (S5_LANES, BRANCH)])

    gr, gi, d_ar, d_ai = _s5_scan_bwd(tag, dxr, dxi, xr, xi, cst["a_re"], cst["a_im"], rows)
    b_re = cst["b_re"].astype(bf16)
    b_im = cst["b_im"].astype(bf16)

    def body_in(i, gr_ref, gi_ref, dy_ref, u_ref, bre_ref, bim_ref, d_ref, du_ref, dbre_ref, dbim_ref):
        grv = gr_ref[...]
        giv = gi_ref[...]
        du = _dot_nt(grv, bre_ref[...]) + _dot_nt(giv, bim_ref[...]) + dy_ref[...] * d_ref[...]
        du_ref[...] = du.astype(bf16)
        ub = u_ref[...].astype(bf16)
        dbre_ref[...] += _dot_tn(ub, grv)
        dbim_ref[...] += _dot_tn(ub, giv)

    du, d_bre, d_bim = _rt("s5_in_bwd_" + tag, body_in, rows, ROW_TILE,
                           [(gr, S5_LANES, 0), (gi, S5_LANES, 0), (dy, BRANCH, 0), (proj, BRANCH, U_COL)],
                           [b_re, b_im, cst["s5_d"]], [(BRANCH, bf16)],
                           acc_outs=[(BRANCH, S5_LANES), (BRANCH, S5_LANES)])
    dcst = {"b_re": d_bre, "b_im": d_bim, "a_re": d_ar, "a_im": d_ai, "c_re": d_cre, "c_im": d_cim,
            "s5_d": d_d, "glu_b": d_gb}
    return du, dcst, d_gw


def _hg_prep(q, z, lb):
    qs = _sigmoid(q)
    qh = q * qs
    sg = _sigmoid_small(z)
    fg = lb + (1.0 - lb) * sg
    kk = (1.0 - lb) * (1.0 - sg)
    return qs, qh, sg, fg, kk


def _hg_fwd(tag, proj, cst, rows):
    tm = min(ROW_TILE, rows)
    c_sz = HG_CHUNK
    nch = tm // c_sz
    n_chunks = rows // c_sz

    def body(i, q_ref, z_ref, v_ref, g_ref, lb_ref, nw_ref, out_ref, o_ref, ss_ref, sn_ref, st_s):
        @pl.when(i == 0)
        def _():
            st_s[...] = jnp.zeros_like(st_s)

        lb = lb_ref[...]
        _, qh, sg, fg, kk = _hg_prep(q_ref[...], z_ref[...], lb)
        b = _seg_cumsum(jnp.log(fg), tm, c_sz)
        qhat = (qh * jnp.exp(b)).astype(bf16)
        khat = (kk * jnp.exp(-b)).astype(bf16)
        vb = v_ref[...].astype(bf16)
        b3 = b.reshape(nch, c_sz, BRANCH)
        bl3 = b3[:, c_sz - 1:c_sz, :]
        kdec = (kk.reshape(nch, c_sz, BRANCH) * jnp.exp(bl3 - b3)).astype(bf16)
        ebl = jnp.exp(bl3)
        tril = (lax.broadcasted_iota(jnp.int32, (nch, c_sz, c_sz), 1)
                >= lax.broadcasted_iota(jnp.int32, (nch, c_sz, c_sz), 2))
        o_heads = []
        for h in range(HG_HEADS):
            hl = slice(h * HG_DK, (h + 1) * HG_DK)
            q3 = qhat[:, hl].reshape(nch, c_sz, HG_DK)
            k3 = khat[:, hl].reshape(nch, c_sz, HG_DK)
            v3 = vb[:, hl].reshape(nch, c_sz, HG_DK)
            a_mat = jnp.where(tril, _bdot_nt(q3, k3), 0.0).astype(bf16)
            o3 = _bdot(a_mat, v3)
            st = st_s[hl, :]
            before = []
            for ci in range(nch):
                before.append(st.astype(bf16))
                st = st * ebl[ci][:, hl] + _dot_tn(v3[ci], kdec[ci][:, hl])
                sn_ref[ci, hl, :] = st.astype(bf16)
            st_s[hl, :] = st
            s3 = jnp.stack(before)
            ss_ref[:, hl, :] = s3
            o3 = o3 + _bdot_nt(q3, s3)
            o_heads.append(o3.reshape(tm, HG_DK))
        o = jnp.concatenate(o_heads, axis=1)
        o_ref[...] = o
        r = lax.rsqrt(_head_mean(o * o) + EPS)
        g = g_ref[...]
        out_ref[...] = (o * r * nw_ref[...] * (g * _sigmoid(g))).astype(bf16)

    out, o, ss, sn = _rt(
        "hg_fwd_" + tag, body, rows, tm,
        [(proj, BRANCH, U_COL + 1), (proj, BRANCH, U_COL + 2), (proj, BRANCH, U_COL + 3), (proj, BRANCH, U_COL + 4)],
        [cst["hg_lb"], cst["hg_nw"]],
        [(BRANCH, bf16), (BRANCH, f32),
         ((n_chunks, BRANCH, HG_DK), (nch, BRANCH, HG_DK), lambda t: (t, 0, 0), bf16),
         ((n_chunks, BRANCH, HG_DK), (nch, BRANCH, HG_DK), lambda t: (t, 0, 0), bf16)],
        scratch=[pltpu.VMEM((BRANCH, HG_DK), f32)])
    return out, (o, ss, sn)


def _hg_bwd(tag, saved, proj, cst, d_out, rows):
    o_saved, ss, sn = saved
    tm = min(ROW_TILE, rows)
    c_sz = HG_CHUNK
    nch = tm // c_sz

    def body(i, do_ref, q_ref, z_ref, v_ref, g_ref, o_ref, ss_ref, sn_ref, lb_ref, nw_ref,
             dq_ref, dz_ref, dv_ref, dg_ref, dlb_ref, dnw_ref, dst_s):
        @pl.when(i == 0)
        def _():
            dst_s[...] = jnp.zeros_like(dst_s)

        lb = lb_ref[...]
        q = q_ref[...]
        qs, qh, sg, fg, kk = _hg_prep(q, z_ref[...], lb)
        b = _seg_cumsum(jnp.log(fg), tm, c_sz)
        eb = jnp.exp(b)
        enb = jnp.exp(-b)
        qhat = (qh * eb).astype(bf16)
        khat = (kk * enb).astype(bf16)
        vb = v_ref[...].astype(bf16)
        b3 = b.reshape(nch, c_sz, BRANCH)
        bl3 = b3[:, c_sz - 1:c_sz, :]
        dec3 = jnp.exp(bl3 - b3)
        kdec = (kk.reshape(nch, c_sz, BRANCH) * dec3).astype(bf16)
        ebl = jnp.exp(bl3)
        g = g_ref[...]
        gs = _sigmoid(g)
        o = o_ref[...]
        r = lax.rsqrt(_head_mean(o * o) + EPS)
        oh = o * r
        nw = nw_ref[...]
        dov = do_ref[...]
        don = dov * (g * gs)
        dg_ref[...] = (dov * oh * nw * (gs * (1.0 + g * (1.0 - gs)))).astype(bf16)
        dnw_ref[...] += jnp.sum(don * oh, axis=0, keepdims=True)
        doh = don * nw
        d_o = r * (doh - oh * _head_mean(doh * oh))
        dob = d_o.astype(bf16)
        t_idx = lax.broadcasted_iota(jnp.int32, (nch, c_sz, c_sz), 1)
        s_idx = lax.broadcasted_iota(jnp.int32, (nch, c_sz, c_sz), 2)
        heads = []
        for h in range(HG_HEADS):
            hl = slice(h * HG_DK, (h + 1) * HG_DK)
            q3 = qhat[:, hl].reshape(nch, c_sz, HG_DK)
            k3 = khat[:, hl].reshape(nch, c_sz, HG_DK)
            v3 = vb[:, hl].reshape(nch, c_sz, HG_DK)
            do3 = dob[:, hl].reshape(nch, c_sz, HG_DK)
            s3 = ss_ref[:, hl, :]
            da_mat = jnp.where(t_idx >= s_idx, _bdot_nt(do3, v3), 0.0).astype(bf16)
            a_t = jnp.where(t_idx <= s_idx, _bdot_nt(k3, q3), 0.0).astype(bf16)
            da_t = jnp.where(t_idx <= s_idx, _bdot_nt(v3, do3), 0.0).astype(bf16)
            dqhat = _bdot(do3, s3) + _bdot(da_mat, k3)
            dkhat = _bdot(da_t, q3)
            dst = dst_s[hl, :]
            after = [None] * nch
            for ci in reversed(range(nch)):
                after[ci] = dst
                dst = dst * ebl[ci][:, hl] + _dot_tn(do3[ci], q3[ci])
            dst_s[hl, :] = dst
            ds3 = jnp.stack(after)
            ds3b = ds3.astype(bf16)
            dk_inter = _bdot(v3, ds3b) * dec3[:, :, hl]
            dv3 = _bdot(a_t, do3) + _bdot_nt(kdec[:, :, hl], ds3b)
            flux = jnp.sum(sn_ref[:, hl, :].astype(f32) * ds3, axis=1, keepdims=True)
            heads.append((dqhat.reshape(tm, HG_DK), dkhat.reshape(tm, HG_DK), dk_inter.reshape(tm, HG_DK),
                          dv3.reshape(tm, HG_DK), jnp.broadcast_to(flux, (nch, c_sz, HG_DK)).reshape(tm, HG_DK)))
        dqhat, dkhat, dk_inter, dv, flux = (jnp.concatenate(parts, axis=1) for parts in zip(*heads))
        dv_ref[...] = dv.astype(bf16)
        dqh = dqhat * eb
        dk = dkhat * enb + dk_inter
        db = qhat.astype(f32) * dqhat - khat.astype(f32) * dkhat - kk * dk_inter
        dlf = _seg_rev_cumsum(db, tm, c_sz) + flux
        tt = (1.0 - lb) * sg * (1.0 - sg)
        dz_ref[...] = (dlf * tt / fg - dk * tt).astype(bf16)
        dlb_ref[...] += jnp.sum(dlf * (1.0 - sg) / fg - dk * (1.0 - sg), axis=0, keepdims=True)
        dq_ref[...] = (dqh * (qs * (1.0 + q * (1.0 - qs)))).astype(bf16)

    dq, dz, dv, dg, d_lb, d_nw = _rt(
        "hg_bwd_" + tag, body, rows, tm,
        [(d_out, BRANCH, 0), (proj, BRANCH, U_COL + 1), (proj, BRANCH, U_COL + 2), (proj, BRANCH, U_COL + 3),
         (proj, BRANCH, U_COL + 4), (o_saved, BRANCH, 0), (ss, (nch, BRANCH, HG_DK), lambda t: (t, 0, 0)),
         (sn, (nch, BRANCH, HG_DK), lambda t: (t, 0, 0))],
        [cst["hg_lb"], cst["hg_nw"]],
        [(BRANCH, bf16)] * 4, acc_outs=[(1, BRANCH), (1, BRANCH)],
        scratch=[pltpu.VMEM((BRANCH, HG_DK), f32)],
        reverse=True)
    return dq, dz, dv, dg, {"hg_lb": d_lb, "hg_nw": d_nw}


def _rg_gates(xc, wa_ref, ba_ref, wx_ref, bx_ref, sp8):
    xcb = xc.astype(bf16)
    r = _sigmoid(_dot(xcb, wa_ref[...]) + ba_ref[...])
    ig = _sigmoid(_dot(xcb, wx_ref[...]) + bx_ref[...])
    la = -sp8 * r
    a = jnp.exp(la)
    mult = jnp.sqrt(-_expm1(2.0 * la))
    return xcb, r, ig, a, mult


def _rg_fwd(tag, proj, cst, rows):
    tm = min(ROW_TILE, rows)

    def body(i, xb_ref, gate_ref, cw_ref, cb_ref, wa_ref, ba_ref, wx_ref, bx_ref, sp_ref,
             out_ref, xc_ref, h_ref, hp_ref, prev_s, hc_s):
        @pl.when(i == 0)
        def _():
            prev_s[...] = jnp.zeros_like(prev_s)
            hc_s[...] = jnp.zeros_like(hc_s)

        row = _rows((tm, BRANCH))
        xb = xb_ref[...]
        prev = prev_s[...]
        xc = cb_ref[...] + cw_ref[3:4, :] * xb
        for j in range(1, 4):
            sh = jnp.where(row >= j, pltpu.roll(xb, j, 0), pltpu.roll(prev, j, 0))
            xc = xc + cw_ref[3 - j:4 - j, :] * sh
        prev_s[...] = xb
        xc_ref[...] = xc
        _, r, ig, a, mult = _rg_gates(xc, wa_ref, ba_ref, wx_ref, bx_ref, sp_ref[...])
        bb = mult * ig * xc
        hc = hc_s[...]
        row8 = _rows((SUBLANES, BRANCH))
        for g in range(tm // SUBLANES):
            sl = slice(g * SUBLANES, (g + 1) * SUBLANES)
            a_cum, h_loc = _scan_fwd(a[sl], bb[sl], SUBLANES)
            h = h_loc + a_cum * hc
            h_ref[sl, :] = h
            hp_ref[sl, :] = jnp.where(row8 >= 1, pltpu.roll(h, 1, 0), hc)
            hc = h[SUBLANES - 1:SUBLANES, :]
        hc_s[...] = hc
        out_ref[...] = (h_ref[...] * _gelu(gate_ref[...])).astype(bf16)

    out, xc, h, hp = _rt(
        "rg_fwd_" + tag, body, rows, tm,
        [(proj, BRANCH, U_COL + 5), (proj, BRANCH, U_COL + 6)],
        [cst["rg_cw"], cst["rg_cb"], cst["rg_wa"].astype(bf16), cst["rg_ba"], cst["rg_wx"].astype(bf16),
         cst["rg_bx"], cst["rg_sp8"]],
        [(BRANCH, bf16), (BRANCH, f32), (BRANCH, f32), (BRANCH, f32)],
        scratch=[pltpu.VMEM((tm, BRANCH), f32), pltpu.VMEM((1, BRANCH), f32)])
    return out, (xc, h, hp)


def _rg_bwd(tag, saved, proj, cst, d_out, rows):
    xc_saved, h_saved, hp_saved = saved
    tm = min(ROW_TILE, rows)

    def body(i, do_ref, xb_ref, gate_ref, xc_ref, h_ref, hp_ref, cw_ref, wa_ref, ba_ref, wx_ref, bx_ref, sp_ref,
             dxb_ref, dgate_ref, dcw_ref, dcb_ref, dwa_ref, dba_ref, dwx_ref, dbx_ref, dsp_ref,
             nxt_s, ec_s, gt_s):
        @pl.when(i == 0)
        def _():
            nxt_s[...] = jnp.zeros_like(nxt_s)
            ec_s[...] = jnp.zeros_like(ec_s)

        row = _rows((tm, BRANCH))
        xc = xc_ref[...]
        sp8 = sp_ref[...]
        xcb, r, ig, a, mult = _rg_gates(xc, wa_ref, ba_ref, wx_ref, bx_ref, sp8)
        gate = gate_ref[...]
        dov = do_ref[...]
        dh = dov * _gelu(gate)
        dgate_ref[...] = (dov * h_ref[...] * _gelu_grad(gate)).astype(bf16)
        adh = a * dh
        ec = ec_s[...]
        last8 = _rows((SUBLANES, BRANCH)) == SUBLANES - 1
        for g in reversed(range(tm // SUBLANES)):
            sl = slice(g * SUBLANES, (g + 1) * SUBLANES)
            a_cum, e_loc = _scan_bwd(a[sl], adh[sl], SUBLANES)
            e = e_loc + a_cum * ec
            gt_s[sl, :] = dh[sl] + jnp.where(last8, ec, pltpu.roll(e, SUBLANES - 1, 0))
            ec = e[0:1, :]
        ec_s[...] = ec
        g_tot = gt_s[...]
        d_a = g_tot * hp_ref[...]
        d_mult = g_tot * ig * xc
        d_ix = g_tot * mult
        d_ig = d_ix * xc
        d_xc = d_ix * ig
        d_la = d_a * a - d_mult * (a * a) / mult
        d_r = -d_la * sp8
        dsp_ref[...] += jnp.sum(-d_la * r, axis=0, keepdims=True)
        dzr = d_r * r * (1.0 - r)
        dzi = d_ig * ig * (1.0 - ig)
        dzrb = dzr.astype(bf16)
        dzib = dzi.astype(bf16)
        d_xc = d_xc + _dot_nt(dzrb, wa_ref[...]) + _dot_nt(dzib, wx_ref[...])
        dwa_ref[...] += _dot_tn(xcb, dzrb)
        dwx_ref[...] += _dot_tn(xcb, dzib)
        dba_ref[...] += jnp.sum(dzr, axis=0, keepdims=True)
        dbx_ref[...] += jnp.sum(dzi, axis=0, keepdims=True)
        dcb_ref[...] += jnp.sum(d_xc, axis=0, keepdims=True)
        nxt = nxt_s[...]
        xb = xb_ref[...]
        dxb = cw_ref[3:4, :] * d_xc
        dcw_ref[3:4, :] += jnp.sum(d_xc * xb, axis=0, keepdims=True)
        for j in range(1, 4):
            sh = jnp.where(row < tm - j, pltpu.roll(d_xc, tm - j, 0), pltpu.roll(nxt, tm - j, 0))
            dxb = dxb + cw_ref[3 - j:4 - j, :] * sh
            dcw_ref[3 - j:4 - j, :] += jnp.sum(sh * xb, axis=0, keepdims=True)
        nxt_s[...] = d_xc
        dxb_ref[...] = dxb.astype(bf16)

    wa = cst["rg_wa"].astype(bf16)
    wx = cst["rg_wx"].astype(bf16)
    dxb, dgate, d_cw, d_cb, d_wa, d_ba, d_wx, d_bx, d_sp = _rt(
        "rg_bwd_" + tag, body, rows, tm,
        [(d_out, BRANCH, 0), (proj, BRANCH, U_COL + 5), (proj, BRANCH, U_COL + 6), (xc_saved, BRANCH, 0),
         (h_saved, BRANCH, 0), (hp_saved, BRANCH, 0)],
        [cst["rg_cw"], wa, cst["rg_ba"], wx, cst["rg_bx"], cst["rg_sp8"]],
        [(BRANCH, bf16), (BRANCH, bf16)],
        acc_outs=[(4, BRANCH), (1, BRANCH), (BRANCH, BRANCH), (1, BRANCH), (BRANCH, BRANCH), (1, BRANCH), (1, BRANCH)],
        scratch=[pltpu.VMEM((tm, BRANCH), f32), pltpu.VMEM((1, BRANCH), f32), pltpu.VMEM((tm, BRANCH), f32)],
        reverse=True)
    dcst = {"rg_cw": d_cw, "rg_cb": d_cb, "rg_wa": d_wa, "rg_ba": d_ba, "rg_wx": d_wx, "rg_bx": d_bx, "rg_sp8": d_sp}
    return dxb, dgate, dcst


def _merge_fwd(tag, proj, outs, bp, rows):
    def body(i, ya_ref, yb_ref, yc_ref, gm_ref, p_ref, m_ref):
        acc = None
        for n, y_ref in enumerate((ya_ref, yb_ref, yc_ref)):
            up = _dot(y_ref[...], p_ref[n])
            term = _sigmoid(gm_ref[:, n * D_MODEL:(n + 1) * D_MODEL]) * up
            acc = term if acc is None else acc + term
        m_ref[...] = acc.astype(bf16)
    return _rt("merge_fwd_" + tag, body, rows, ROW_TILE,
               [(outs[0], BRANCH, 0), (outs[1], BRANCH, 0), (outs[2], BRANCH, 0), (proj, GM_WIDTH, 0)],
               [bp], [(D_MODEL, bf16)])[0]


def _merge_bwd(tag, proj, outs, bp, dmerged, rows):
    def body(i, dm_ref, ya_ref, yb_ref, yc_ref, gm_ref, p_ref, da_ref, db_ref, dc_ref, dgm_ref, dp_ref):
        dm = dm_ref[...]
        for n, (y_ref, dy_ref) in enumerate(((ya_ref, da_ref), (yb_ref, db_ref), (yc_ref, dc_ref))):
            yv = y_ref[...]
            up = _dot(yv, p_ref[n])
            gt = _sigmoid(gm_ref[:, n * D_MODEL:(n + 1) * D_MODEL])
            dup = (dm * gt).astype(bf16)
            dgm_ref[:, n * D_MODEL:(n + 1) * D_MODEL] = (dm * up * gt * (1.0 - gt)).astype(bf16)
            dy_ref[...] = _dot_nt(dup, p_ref[n])
            dp_ref[n] += _dot_tn(yv, dup)
    return _rt("merge_bwd_" + tag, body, rows, ROW_TILE,
               [(dmerged, D_MODEL, 0), (outs[0], BRANCH, 0), (outs[1], BRANCH, 0), (outs[2], BRANCH, 0),
                (proj, GM_WIDTH, 0)],
               [bp], [(BRANCH, f32), (BRANCH, f32), (BRANCH, f32), (GM_WIDTH, bf16)],
               acc_outs=[(N_BRANCH, BRANCH, D_MODEL)])


def _block_diag(blocks):
    g, r, c = blocks.shape
    on_diag = (lax.broadcasted_iota(jnp.int32, (g * r, g * c), 0) // r
               == lax.broadcasted_iota(jnp.int32, (g * r, g * c), 1) // c)
    tiled = jnp.broadcast_to(blocks.reshape(g * r, 1, c), (g * r, g, c)).reshape(g * r, g * c)
    return jnp.where(on_diag, tiled, 0.0)


def _prep_consts(sp):
    p = jax.nn.softmax(sp["hg_lb_logits"], axis=0)
    lower = jnp.cumsum(p, axis=0) - p[0]
    out = []
    for l in range(DEPTH):
        lr = jnp.minimum(sp["s5_lambda_re"][l], S5_EIG_MAX)
        li = sp["s5_lambda_im"][l]
        dt = jnp.exp(sp["s5_log_dt"][l])[:, None]
        mag = jnp.exp(lr * dt)
        ar = mag * jnp.cos(li * dt)
        ai = mag * jnp.sin(li * dt)
        den = lr * lr + li * li
        fr = ((ar - 1.0) * lr + ai * li) / den
        fi = (ai * lr - (ar - 1.0) * li) / den
        br, bi = sp["s5_b_re"][l], sp["s5_b_im"][l]
        bbr = fr[..., None] * br - fi[..., None] * bi
        bbi = fr[..., None] * bi + fi[..., None] * br
        c = {
            "a_re": ar.reshape(1, S5_LANES), "a_im": ai.reshape(1, S5_LANES),
            "b_re": _block_diag(bbr.transpose(0, 2, 1)), "b_im": _block_diag(bbi.transpose(0, 2, 1)),
            "c_re": _block_diag(sp["s5_c_re"][l].transpose(0, 2, 1)),
            "c_im": -_block_diag(sp["s5_c_im"][l].transpose(0, 2, 1)),
            "s5_d": sp["s5_d"][l][None], "glu_b": sp["s5_glu_b"][l][None],
            "hg_lb": lower[l][None], "hg_nw": sp["hg_norm_w"][l][None],
            "rg_cw": sp["rg_conv_w"][l], "rg_cb": sp["rg_conv_b"][l][None],
            "rg_wa": _block_diag(sp["rg_wa"][l]), "rg_ba": sp["rg_ba"][l][None],
            "rg_wx": _block_diag(sp["rg_wx"][l]), "rg_bx": sp["rg_bx"][l][None],
            "rg_sp8": (RG_C * jax.nn.softplus(-sp["rg_lambda"][l]))[None],
        }
        out.append(c)
    return out


def _mixer_fwd(tag, x, hb, w_in, bp, w_out, cst, next_nw, rows):
    proj = _mm1("mix_proj_" + tag, hb, w_in, "nn", rows, IN_TOTAL, D_MODEL, 512, IN_TOTAL // 4, D_MODEL)
    cst = dict(cst)
    out_a, sv_a = _s5_fwd(tag, proj, cst, rows)
    out_b, sv_b = _hg_fwd(tag, proj, cst, rows)
    out_c, sv_c = _rg_fwd(tag, proj, cst, rows)
    merged = _merge_fwd(tag, proj, (out_a, out_b, out_c), bp, rows)
    x_out, hb_out = _mm("mix_out_" + tag, [merged], [w_out], [(0, 0, 0)], 1, "nn", rows, D_MODEL, D_MODEL,
                        512, D_MODEL, D_MODEL, [f32, bf16], _residual_then_norm(1.0), extras=[x], vecs=[next_nw])
    return x_out, hb_out, (x, hb, proj, (out_a, out_b, out_c), merged, sv_a, sv_b, sv_c)


def _mixer_bwd(tag, saved, nw, w_in, bp, w_out, cst, dx, dxb, rows):
    x, hb, proj, outs, merged, sv_a, sv_b, sv_c = saved
    d_wout = _mm1("mix_dwout_" + tag, merged, dxb, "tn", D_MODEL, D_MODEL, rows, D_MODEL, D_MODEL, TOKEN_K,
                  out_dtype=bf16)
    dmerged = _mm1("mix_dmerged_" + tag, dxb, w_out, "nt", rows, D_MODEL, D_MODEL, 512, D_MODEL, D_MODEL)
    d_a, d_b, d_c, dgm, d_bp = _merge_bwd(tag, proj, outs, bp, dmerged, rows)
    dxbc, dgatec, dcst_c = _rg_bwd(tag, sv_c, proj, cst, d_c, rows)
    dq, dz, dv, dg, dcst_b = _hg_bwd(tag, sv_b, proj, cst, d_b, rows)
    du, dcst_a, d_glu_w = _s5_bwd(tag, sv_a, proj, cst, d_a, rows)
    dproj = jnp.concatenate([dgm, du, dq, dz, dv, dg, dxbc, dgatec], axis=1)
    d_win = _mm1("mix_dwin_" + tag, hb, dproj, "tn", D_MODEL, IN_TOTAL, rows, D_MODEL, IN_TOTAL // 4, TOKEN_K,
                 out_dtype=bf16)
    dx_in, dxb_in, d_nw = _mm("mix_dh_" + tag, [dproj], [w_in], [(0, 0, 0)], 1, "nt", rows, D_MODEL, IN_TOTAL,
                              512, D_MODEL, IN_TOTAL // 2, [f32, bf16], _norm_bwd_epilogue, extras=[x, dx], vecs=[nw],
                              n_part=1)
    dcst = {**dcst_a, **dcst_b, **dcst_c}
    return dx_in, dxb_in, jnp.sum(d_nw, axis=0), d_win, d_bp, d_wout, d_glu_w, dcst


def _local_step(x, target, weights_of, small, grads_done):
    rows = x.shape[0]
    consts, consts_vjp = jax.vjp(_prep_consts, small)
    norm_w = small["norm_w"]
    saved = []
    h = x
    hb = _rms_fwd("first_norm", x, norm_w[0, 0][None], rows)
    for l in range(DEPTH):
        t = str(l)
        after = [norm_w[l + 1, 0][None]] if l + 1 < DEPTH else []
        wa = weights_of(l, "a")
        h, hb, sv0 = _ffn_fwd(t + "a", h, hb, *wa, [norm_w[l, 1][None]], rows)
        wm = weights_of(l, "mix")
        cst = dict(consts[l])
        cst["glu_w"] = wm[3]
        h, hb, sv1 = _mixer_fwd(t, h, hb, *wm[:3], cst, norm_w[l, 2][None], rows)
        wb = weights_of(l, "b")
        h, hb, sv2 = _ffn_fwd(t + "b", h, hb, *wb, after, rows)
        saved.append((sv0, sv1, sv2, cst, wa, wm, wb))
    dx, dxb, loss, d_fnw = _loss_head(h, small["final_norm_w"][None], target, rows)
    d_norm = [None] * DEPTH
    d_consts = [None] * DEPTH
    for l in reversed(range(DEPTH)):
        t = str(l)
        sv0, sv1, sv2, cst, wa, wm, wb = saved[l]
        dx, dxb, dn2, dg1, du1, dd1 = _ffn_bwd(t + "b", sv2, norm_w[l, 2][None], *wb, dx, dxb, rows)
        grads_done(l, "b", [dg1, du1, dd1])
        dx, dxb, dn1, d_win, d_bp, d_wout, d_glu_w, dcst = _mixer_bwd(
            t, sv1, norm_w[l, 1][None], *wm[:3], cst, dx, dxb, rows)
        grads_done(l, "mix", [d_win, d_bp, d_wout, d_glu_w])
        dx, dxb, dn0, dg0, du0, dd0 = _ffn_bwd(t + "a", sv0, norm_w[l, 0][None], *wa, dx, dxb, rows)
        grads_done(l, "a", [dg0, du0, dd0])
        d_norm[l] = jnp.concatenate([dn0, dn1, dn2], axis=0)
        d_consts[l] = dcst
    (g_small,) = consts_vjp(d_consts)
    g_small = dict(g_small)
    g_small["norm_w"] = g_small["norm_w"] + jnp.stack(d_norm)
    g_small["final_norm_w"] = g_small["final_norm_w"] + d_fnw[0]
    return loss[0, 0], dx, g_small


def _other_chips(x, y):
    return [(1 - x, y), (x, 1 - y), (1 - x, 1 - y)]


def _gather_over_ici(shards):
    n = len(shards)

    def copies(in_refs, out_refs, send_sems, recv_sems):
        x, y, c = _place()
        cps = []
        for i in range(n):
            mine = out_refs[i].at[4 * x + 2 * y + c]
            cps.append(pltpu.make_async_copy(in_refs[i], mine, send_sems.at[5 * i + 4]))
            for k, to in enumerate([(x, y, 1 - c)] + [(px, py, c) for px, py in _other_chips(x, y)]):
                cps.append(pltpu.make_async_remote_copy(
                    src_ref=in_refs[i], dst_ref=mine, send_sem=send_sems.at[5 * i + k],
                    recv_sem=recv_sems.at[5 * i + k], device_id=to, device_id_type=MESH_IDS))
        return cps

    return _Carry(shards, [jax.ShapeDtypeStruct((N_DEV,) + s.shape, s.dtype) for s in shards], 5 * n, copies)


def _gather_forward(name, landings):
    n = len(landings)

    def body(*refs):
        in_refs, out_refs = refs[:n], refs[n:2 * n]
        send_sems, recv_sems = refs[2 * n:]
        x, y, c = _place()
        cps = []
        for i in range(n):
            for j, (px, py) in enumerate(_other_chips(x, y)):
                block = 4 * px + 2 * py + c
                cps.append(pltpu.make_async_remote_copy(
                    src_ref=in_refs[i].at[block], dst_ref=out_refs[i].at[block], send_sem=send_sems.at[3 * i + j],
                    recv_sem=recv_sems.at[3 * i + j], device_id=(x, y, 1 - c), device_id_type=MESH_IDS))
        for cp in cps:
            cp.start()
        for cp in cps:
            cp.wait()

    return pl.pallas_call(
        body, name=name, out_shape=[jax.ShapeDtypeStruct(a.shape, a.dtype) for a in landings],
        in_specs=[ANY_SPEC] * n, out_specs=[ANY_SPEC] * n, input_output_aliases={i: i for i in range(n)},
        scratch_shapes=[pltpu.SemaphoreType.DMA((3 * n,)), pltpu.SemaphoreType.DMA((3 * n,))],
    )(*landings)


def _all_gather(name, shards):
    n = len(shards)

    def body(*refs):
        x_refs, out_refs = refs[:n], refs[n:2 * n]
        send_sems, recv_sems, local_sems = refs[2 * n:]
        x, y, c = _place()
        me, sibling = (x, y, c), (x, y, 1 - c)
        chips = [(1 - x, y), (x, 1 - y), (1 - x, 1 - y)]

        def blk(i, px, py, pc):
            return out_refs[i].at[4 * px + 2 * py + pc]

        def copy(i, k, block, to, src=None):
            return pltpu.make_async_remote_copy(
                src_ref=blk(i, *block) if src is None else src, dst_ref=blk(i, *block),
                send_sem=send_sems.at[7 * i + k], recv_sem=recv_sems.at[7 * i + k], device_id=to,
                device_id_type=MESH_IDS)

        mine = [pltpu.make_async_copy(x_refs[i], blk(i, *me), local_sems.at[i]) for i in range(n)]
        for cp in mine:
            cp.start()
        first = []
        for i in range(n):
            first.append(copy(i, 0, me, sibling, src=x_refs[i]))
            first += [copy(i, 1 + j, me, (*chip, c), src=x_refs[i]) for j, chip in enumerate(chips)]
        for cp in first:
            cp.start()
        passed = []
        for j, chip in enumerate(chips):
            for i in range(n):
                copy(i, 1 + j, (*chip, c), me).wait_recv()
                fwd = copy(i, 4 + j, (*chip, c), sibling)
                fwd.start()
                passed.append(fwd)
        for i in range(n):
            copy(i, 0, sibling, me).wait_recv()
            for j, chip in enumerate(chips):
                copy(i, 4 + j, (*chip, 1 - c), me).wait_recv()
        for cp in first + passed:
            cp.wait_send()
        for cp in mine:
            cp.wait()

    return pl.pallas_call(
        body, name=name, out_shape=[jax.ShapeDtypeStruct((N_DEV,) + s.shape, s.dtype) for s in shards],
        in_specs=[ANY_SPEC] * n, out_specs=[ANY_SPEC] * n,
        scratch_shapes=[pltpu.SemaphoreType.DMA((7 * n,)), pltpu.SemaphoreType.DMA((7 * n,)),
                        pltpu.SemaphoreType.DMA((n,))],
    )(*shards)


def _row_tile(rows):
    return rows if rows <= 512 else next(t for t in range(512, 7, -8) if rows % t == 0)


def _sums_over_ici(chip_sums):
    n = len(chip_sums)

    def copies(in_refs, out_refs, send_sems, recv_sems):
        x, y, c = _place()
        return [pltpu.make_async_remote_copy(
            src_ref=in_refs[i].at[2 * px + py], dst_ref=out_refs[i].at[k], send_sem=send_sems.at[3 * i + k],
            recv_sem=recv_sems.at[3 * i + k], device_id=(px, py, c), device_id_type=MESH_IDS)
            for i in range(n) for k, (px, py) in enumerate(_other_chips(x, y))]

    return _Carry(chip_sums, [jax.ShapeDtypeStruct((3,) + t.shape[1:], t.dtype) for t in chip_sums], 3 * n, copies)


def _reduce_scatter(tag, parts, hosts=None):
    n = len(parts)
    _, _, c = _place()

    def body_pair(*refs):
        p_refs, got_refs = refs[:n], refs[n:2 * n]
        send_sems, recv_sems = refs[2 * n:]
        x, y, c = _place()
        cps = [pltpu.make_async_remote_copy(
            src_ref=p_refs[i].at[:, 1 - c], dst_ref=got_refs[i], send_sem=send_sems.at[i], recv_sem=recv_sems.at[i],
            device_id=(x, y, 1 - c), device_id_type=MESH_IDS) for i in range(n)]
        for cp in cps:
            cp.start()
        for cp in cps:
            cp.wait()

    from_sibling = pl.pallas_call(
        body_pair, name="rs_pair_" + tag,
        out_shape=[jax.ShapeDtypeStruct((4,) + p.shape[2:], p.dtype) for p in parts],
        in_specs=[ANY_SPEC] * n, out_specs=[ANY_SPEC] * n,
        scratch_shapes=[pltpu.SemaphoreType.DMA((n,)), pltpu.SemaphoreType.DMA((n,))],
    )(*parts)

    def body_add(idx_ref, *refs):
        p = pl.program_id(0)
        for q in range(n):
            @pl.when(p == q)
            def _(p_ref=refs[q], g_ref=refs[n + q], o_ref=refs[2 * n + q]):
                o_ref[...] = (p_ref[...].astype(f32) + g_ref[...].astype(f32)).astype(o_ref.dtype)

    def at(q):
        return lambda p, j, idx: jnp.clip(j + 4 * (p - q), 0, 3)

    in_specs, out_specs = [], []
    for q, part in enumerate(parts):
        in_specs.append(pl.BlockSpec((None, None) + part.shape[2:],
                                     lambda p, j, idx, blk=at(q): (blk(p, j, idx), idx[0], 0, 0)))
    for q, part in enumerate(parts):
        spec = pl.BlockSpec((None,) + part.shape[2:], lambda p, j, idx, blk=at(q): (blk(p, j, idx), 0, 0))
        in_specs.append(spec)
        out_specs.append(spec)
    chip_sums = pl.pallas_call(
        body_add, name="rs_pair_sum_" + tag,
        out_shape=[jax.ShapeDtypeStruct((4,) + p.shape[2:], p.dtype) for p in parts],
        grid_spec=pltpu.PrefetchScalarGridSpec(num_scalar_prefetch=1, grid=(n, 4), in_specs=in_specs,
                                               out_specs=out_specs),
        compiler_params=_cparams(("arbitrary", "arbitrary")),
    )(jnp.stack([c]).astype(jnp.int32), *parts, *from_sibling)

    others = [None] * n
    riding = set()
    for host, which in (hosts or {}).items():
        rider = _sums_over_ici([chip_sums[i] for i in which])
        _CARRIED[host] = rider
        for pos, i in enumerate(which):
            others[i] = functools.partial(lambda r, p: r.outs[p], rider, pos)
        riding.update(which)
    rest = [i for i in range(n) if i not in riding]
    if rest:
        alone = _sums_over_ici([chip_sums[i] for i in rest])

        def body_chips(*refs):
            k = len(rest)
            cps = alone.copies(refs[:k], refs[k:2 * k], *refs[2 * k:])
            for cp in cps:
                cp.start()
            for cp in cps:
                cp.wait()

        from_chips = pl.pallas_call(
            body_chips, name="rs_chips_" + tag, out_shape=alone.out_shapes,
            in_specs=[ANY_SPEC] * len(rest), out_specs=[ANY_SPEC] * len(rest), scratch_shapes=alone.sems(),
        )(*alone.ins)
        for pos, i in enumerate(rest):
            others[i] = functools.partial(lambda got: got, from_chips[pos])
    return list(zip(chip_sums, others))


def _own_index():
    x, y, _ = _place()
    return jnp.stack([2 * x + y]).astype(jnp.int32)


def _own_total(name, chip_sum, others):
    _, r, cols = chip_sum.shape
    tr = _row_tile(r)

    def body(idx_ref, t_ref, g_ref, o_ref):
        o_ref[...] = ((t_ref[...].astype(f32) + g_ref[0].astype(f32)) + g_ref[1].astype(f32)) + g_ref[2].astype(f32)

    return pl.pallas_call(
        body, name=name, out_shape=jax.ShapeDtypeStruct((r, cols), f32),
        grid_spec=pltpu.PrefetchScalarGridSpec(
            num_scalar_prefetch=1, grid=(r // tr,),
            in_specs=[pl.BlockSpec((None, tr, cols), lambda t, idx: (idx[0], t, 0)),
                      pl.BlockSpec((3, tr, cols), lambda t, idx: (0, t, 0))],
            out_specs=pl.BlockSpec((tr, cols), lambda t, idx: (t, 0))),
        compiler_params=_cparams(("parallel",)),
    )(_own_index(), chip_sum, others)


def _adam_update(w, gv, m, v):
    m_new = ADAM_B1 * m + (1.0 - ADAM_B1) * gv
    v_new = ADAM_B2 * v + (1.0 - ADAM_B2) * (gv * gv)
    m_hat = m_new / (1.0 - ADAM_B1 ** ADAM_STEP)
    v_hat = v_new / (1.0 - ADAM_B2 ** ADAM_STEP)
    return -ADAM_LR * (m_hat / (jnp.sqrt(v_hat) + ADAM_EPS) + ADAM_WD * w), m_new, v_new


def _adamw_reduced(name, w, pieces, m, v):
    n_p, rows, cols = w.shape
    tr = _row_tile(rows)

    def body(idx_ref, w_ref, *refs):
        red = refs[:2 * n_p]
        m_ref, v_ref, g_ref, d_ref, nm_ref, nv_ref = refs[2 * n_p:]
        p = pl.program_id(0)
        for q in range(n_p):
            @pl.when(p == q)
            def _(t_ref=red[2 * q], o_ref=red[2 * q + 1]):
                gv = ((t_ref[...].astype(f32) + o_ref[0].astype(f32)) + o_ref[1].astype(f32)) + o_ref[2].astype(f32)
                g_ref[...] = gv
                d_ref[...], nm_ref[...], nv_ref[...] = _adam_update(w_ref[...], gv, m_ref[...], v_ref[...])

    spec = pl.BlockSpec((None, tr, cols), lambda p, t, idx: (p, t, 0))
    red_specs, red_args = [], []
    for q, (chip_sum, others) in enumerate(pieces):
        red_specs.append(pl.BlockSpec((None, tr, cols), lambda p, t, idx, q=q: (idx[0], jnp.where(p == q, t, 0), 0)))
        red_specs.append(pl.BlockSpec((3, tr, cols), lambda p, t, idx, q=q: (0, jnp.where(p == q, t, 0), 0)))
        red_args += [chip_sum, others]
    return pl.pallas_call(
        body, name=name, out_shape=[jax.ShapeDtypeStruct((n_p, rows, cols), f32)] * 4,
        grid_spec=pltpu.PrefetchScalarGridSpec(
            num_scalar_prefetch=1, grid=(n_p, rows // tr),
            in_specs=[spec] + red_specs + [spec, spec], out_specs=[spec] * 4),
        compiler_params=_cparams(("parallel", "parallel")),
    )(_own_index(), w, *red_args, m, v)


def _adamw(name, w, g, m, v):
    rows, cols = w.shape
    tr = _row_tile(rows)

    def body(w_ref, g_ref, m_ref, v_ref, d_ref, nm_ref, nv_ref):
        d_ref[...], nm_ref[...], nv_ref[...] = _adam_update(w_ref[...], g_ref[...], m_ref[...], v_ref[...])

    spec = pl.BlockSpec((tr, cols), lambda i: (i, 0))
    return pl.pallas_call(
        body, name=name, grid=(rows // tr,), in_specs=[spec] * 4, out_specs=[spec] * 3,
        out_shape=[jax.ShapeDtypeStruct((rows, cols), f32)] * 3, compiler_params=_cparams(("parallel",)),
    )(w, g, m, v)


WEIGHT_NAMES = ["norm_w", "final_norm_w", "ffn_gate", "ffn_up", "ffn_down", "w_in", "branch_proj", "w_out",
                "s5_lambda_re", "s5_lambda_im", "s5_log_dt", "s5_b_re", "s5_b_im", "s5_c_re", "s5_c_im", "s5_d",
                "s5_glu_w", "s5_glu_b", "hg_lb_logits", "hg_norm_w", "rg_conv_w", "rg_conv_b", "rg_wa", "rg_ba",
                "rg_wx", "rg_bx", "rg_lambda"]
SHARDED = {"ffn_gate": (3, "gate"), "ffn_up": (3, "up"), "ffn_down": (2, "down"), "w_in": (2, "w_in"),
           "branch_proj": (3, "bp"), "w_out": (1, "w_out"), "s5_glu_w": (1, "glu_w"),
           "norm_w": (2, None), "rg_conv_w": (2, None)}
BIG = ["ffn_gate", "ffn_up", "ffn_down", "w_in", "branch_proj", "w_out", "s5_glu_w"]
TRANSPOSED = ("ffn_gate", "ffn_up")
PARTS = {"a": [("ffn_gate", 0, 0), ("ffn_up", 0, 0), ("ffn_down", 0, 0)],
         "b": [("ffn_gate", 1, 0), ("ffn_up", 1, 0), ("ffn_down", 1, 0)],
         "mix": [("w_in", None, 1), ("branch_proj", None, 2), ("w_out", None, 0), ("s5_glu_w", None, 0)]}
AG_HOSTS = {"ffn_up_0a": (0, "mix", [0]), "ffn_down_0a": (0, "mix", [1, 2, 3]), "mix_proj_0": (0, "b", [0, 1, 2]),
            "s5_out_0": (1, "a", [0]), "hg_fwd_0": (1, "a", [1]), "rg_fwd_0": (1, "a", [2]),
            "merge_fwd_0": (1, "b", [0]), "ffn_up_0b": (1, "b", [1]), "ffn_down_0b": (1, "b", [2]),
            "s5_scan_fwd_0": (1, "mix", [0, 1]), "mix_out_0": (1, "mix", [2, 3])}
RS_HOSTS = {(1, "b"): {"s5_scan_bwd_1": [0, 1, 2]},
            (1, "mix"): {"ffn_bwd_mid_1a": [1, 2, 3], "mix_dh_0": [0]},
            (1, "a"): {"mix_dwin_0": [0, 1], "merge_bwd_0": [2]},
            (0, "b"): {"s5_scan_bwd_0": [0, 1, 2]},
            (0, "mix"): {"ffn_bwd_mid_0a": [0], "ffn_dh_0a": [1, 2, 3]}}
SMALL_SHARDED = ["norm_w", "rg_conv_w"]
REPLICATED = [n for n in WEIGHT_NAMES if n not in SHARDED]
LANES = 128


PACK_ROWS = 512


def _pack_rows(arrays, names):
    pieces = []
    for n in names:
        flat = arrays[n].reshape(-1)
        pieces.append(jnp.pad(flat, (0, -flat.shape[0] % LANES)).reshape(-1, LANES))
    rows = jnp.concatenate(pieces, axis=0)
    return jnp.pad(rows, ((0, -rows.shape[0] % PACK_ROWS), (0, 0)))


def _unpack_rows(rows, names, like):
    out, r0 = {}, 0
    for n in names:
        size = math.prod(like[n].shape)
        nrows = -(-size // LANES)
        out[n] = rows[r0:r0 + nrows].reshape(-1)[:size].reshape(like[n].shape)
        r0 += nrows
    return out


def _unshard(gathered, axis):
    g = jnp.moveaxis(gathered, 0, axis)
    shp = g.shape
    return g.reshape(shp[:axis] + (shp[axis] * shp[axis + 1],) + shp[axis + 2:])


RELAYOUT_ROWS = 256


def _column_runs(width, first_col):
    total = N_DEV * width
    runs = []
    for j in range(N_DEV):
        start = (width * j + first_col) % total
        head = min(width, total - start)
        runs.append((j, 0, start, head))
        if head < width:
            runs.append((j, head, 0, width - head))
    return runs


def _unshard_columns(name, gathered, first_col=0):
    _, r, c = gathered.shape
    tr = min(RELAYOUT_ROWS, r)
    runs = _column_runs(c, first_col)

    def body(g_ref, o_ref):
        for j, off, dst, length in runs:
            o_ref[:, dst:dst + length] = g_ref[j, :, off:off + length]

    return pl.pallas_call(
        body, name=name, grid=(r // tr,), in_specs=[pl.BlockSpec((N_DEV, tr, c), lambda i: (0, i, 0))],
        out_specs=pl.BlockSpec((tr, N_DEV * c), lambda i: (i, 0)),
        out_shape=jax.ShapeDtypeStruct((r, N_DEV * c), gathered.dtype), compiler_params=_cparams(("parallel",)),
    )(gathered)


def _columns_to_blocks(name, full, first_col=0):
    r, total = full.shape
    c = total // N_DEV
    tr = min(RELAYOUT_ROWS, r)
    runs = _column_runs(c, first_col)

    def body(x_ref, o_ref):
        for j, off, src, length in runs:
            o_ref[j // 2, j % 2, :, off:off + length] = x_ref[:, src:src + length].astype(bf16)

    return pl.pallas_call(
        body, name=name, grid=(r // tr,), in_specs=[pl.BlockSpec((tr, total), lambda i: (i, 0))],
        out_specs=pl.BlockSpec((4, 2, tr, c), lambda i: (0, 0, i, 0)),
        out_shape=jax.ShapeDtypeStruct((4, 2, r, c), bf16), compiler_params=_cparams(("parallel",)),
    )(full)


def _to_blocks(full, axis):
    shp = full.shape
    g = full.reshape(shp[:axis] + (4, 2, shp[axis] // N_DEV) + shp[axis + 1:])
    g = jnp.moveaxis(g, (axis, axis + 1), (0, 1))
    return g.reshape(4, 2, -1, g.shape[-1])


W_IN_SPLIT = IN_TOTAL - GM_WIDTH


def kernel(x, norm_w, final_norm_w, ffn_gate, ffn_up, ffn_down, w_in, branch_proj, w_out, s5_lambda_re, s5_lambda_im, s5_log_dt, s5_b_re, s5_b_im, s5_c_re, s5_c_im, s5_d, s5_glu_w, s5_glu_b, hg_lb_logits, hg_norm_w, rg_conv_w, rg_conv_b, rg_wa, rg_ba, rg_wx, rg_bx, rg_lambda, loss_target, m_norm_w, m_final_norm_w, m_ffn_gate, m_ffn_up, m_ffn_down, m_w_in, m_branch_proj, m_w_out, m_s5_lambda_re, m_s5_lambda_im, m_s5_log_dt, m_s5_b_re, m_s5_b_im, m_s5_c_re, m_s5_c_im, m_s5_d, m_s5_glu_w, m_s5_glu_b, m_hg_lb_logits, m_hg_norm_w, m_rg_conv_w, m_rg_conv_b, m_rg_wa, m_rg_ba, m_rg_wx, m_rg_bx, m_rg_lambda, v_norm_w, v_final_norm_w, v_ffn_gate, v_ffn_up, v_ffn_down, v_w_in, v_branch_proj, v_w_out, v_s5_lambda_re, v_s5_lambda_im, v_s5_log_dt, v_s5_b_re, v_s5_b_im, v_s5_c_re, v_s5_c_im, v_s5_d, v_s5_glu_w, v_s5_glu_b, v_hg_lb_logits, v_hg_norm_w, v_rg_conv_w, v_rg_conv_b, v_rg_wa, v_rg_ba, v_rg_wx, v_rg_bx, v_rg_lambda):
    w = dict(zip(WEIGHT_NAMES, (norm_w, final_norm_w, ffn_gate, ffn_up, ffn_down, w_in, branch_proj, w_out,
                                s5_lambda_re, s5_lambda_im, s5_log_dt, s5_b_re, s5_b_im, s5_c_re, s5_c_im, s5_d,
                                s5_glu_w, s5_glu_b, hg_lb_logits, hg_norm_w, rg_conv_w, rg_conv_b, rg_wa, rg_ba,
                                rg_wx, rg_bx, rg_lambda)))
    m = dict(zip(WEIGHT_NAMES, (m_norm_w, m_final_norm_w, m_ffn_gate, m_ffn_up, m_ffn_down, m_w_in, m_branch_proj,
                                m_w_out, m_s5_lambda_re, m_s5_lambda_im, m_s5_log_dt, m_s5_b_re, m_s5_b_im, m_s5_c_re,
                                m_s5_c_im, m_s5_d, m_s5_glu_w, m_s5_glu_b, m_hg_lb_logits, m_hg_norm_w, m_rg_conv_w,
                                m_rg_conv_b, m_rg_wa, m_rg_ba, m_rg_wx, m_rg_bx, m_rg_lambda)))
    v = dict(zip(WEIGHT_NAMES, (v_norm_w, v_final_norm_w, v_ffn_gate, v_ffn_up, v_ffn_down, v_w_in, v_branch_proj,
                                v_w_out, v_s5_lambda_re, v_s5_lambda_im, v_s5_log_dt, v_s5_b_re, v_s5_b_im, v_s5_c_re,
                                v_s5_c_im, v_s5_d, v_s5_glu_w, v_s5_glu_b, v_hg_lb_logits, v_hg_norm_w, v_rg_conv_w,
                                v_rg_conv_b, v_rg_wa, v_rg_ba, v_rg_wx, v_rg_bx, v_rg_lambda)))
    rows = x.shape[1]

    _CARRIED.clear()

    def shard_of(piece, l):
        n, k, _ = piece
        shard = w[n][l] if k is None else w[n][l, k]
        return (jnp.swapaxes(shard, 0, 1) if n in TRANSPOSED else shard).astype(bf16)

    def assemble(l, part, gathered):
        full = []
        for j, (piece, g) in enumerate(zip(PARTS[part], gathered)):
            tag = "unshard_%d%s%d" % (l, part, j)
            if piece[0] == "w_in":
                full.append(_unshard_columns(tag, g, first_col=GM_WIDTH))
            elif piece[0] == "branch_proj":
                full.append(_unshard_columns(tag, g.reshape(N_DEV, -1, g.shape[-1])).reshape(N_BRANCH, BRANCH, D_MODEL))
            elif piece[2] == g.ndim - 2:
                full.append(_unshard_columns(tag, g))
            else:
                full.append(_unshard(g, piece[2]))
        return full

    n_a = len(PARTS["a"])
    first = _all_gather("gather_weights", [shard_of(p, 0) for p in PARTS["a"]] + [w[n] for n in SMALL_SHARDED])
    small = {n: w[n] for n in REPLICATED}
    for n, g in zip(SMALL_SHARDED, first[n_a:]):
        small[n] = _unshard(g, SHARDED[n][0])
    riders = {}
    for host, (l, part, which) in AG_HOSTS.items():
        rider = _gather_over_ici([shard_of(PARTS[part][j], l) for j in which])
        _CARRIED[host] = rider
        riders.setdefault((l, part), []).append((which, rider))

    forwarded = {}

    def weights_of(l, part):
        if (l, part) == (0, "a"):
            return assemble(l, part, first[:n_a])
        group = [(l, p) for p in ("a", "mix", "b")] if l == 1 else [(l, part)]
        if (l, part) not in forwarded:
            landed = {key: [None] * len(PARTS[key[1]]) for key in group}
            for key in group:
                for which, rider in riders[key]:
                    for j, buf in zip(which, rider.outs):
                        landed[key][j] = buf
            done = _gather_forward("gather_forward_%d%s" % (l, "" if l == 1 else part),
                                   [buf for key in group for buf in landed[key]])
            for key in group:
                forwarded[key], done = done[:len(PARTS[key[1]])], done[len(PARTS[key[1]]):]
        return assemble(l, part, forwarded[l, part])

    sums = {}

    def blocks_of(l, part, grads):
        out = []
        for j, (piece, g) in enumerate(zip(PARTS[part], grads)):
            tag = "to_blocks_%d%s%d" % (l, part, j)
            if piece[0] == "w_in":
                out.append(_columns_to_blocks(tag, g, first_col=GM_WIDTH))
            elif piece[0] == "branch_proj":
                out.append(_columns_to_blocks(tag, g.reshape(-1, g.shape[-1])))
            elif piece[2] == g.ndim - 1:
                out.append(_columns_to_blocks(tag, g))
            else:
                out.append(_to_blocks(g, piece[2]).astype(bf16))
        return out

    last_grads = []

    def grads_done(l, part, grads):
        if (l, part) in RS_HOSTS:
            sums[l, part] = _reduce_scatter("%d%s" % (l, part), blocks_of(l, part, grads), hosts=RS_HOSTS[l, part])
        else:
            last_grads.extend(blocks_of(l, part, grads))

    loss_part, dx, g_small = _local_step(x[0], loss_target[0], weights_of, small, grads_done)
    loss = lax.psum(loss_part, ("x", "y", "c"))

    parts = last_grads + [_to_blocks(g_small[n], SHARDED[n][0]) for n in SMALL_SHARDED]
    rep_rows = _pack_rows(g_small, REPLICATED)
    rep_slice = rep_rows.shape[0] // N_DEV
    parts.append(rep_rows.reshape(4, 2, rep_slice, LANES))
    last = _reduce_scatter("last", parts)
    sums[0, "a"] = last[:n_a]

    grads, delta, new_m, new_v = {}, {}, {}, {}

    def update(n, pieces):
        def view(a):
            a = jnp.swapaxes(a, -1, -2) if n in TRANSPOSED else a
            return a.reshape(len(pieces), -1, a.shape[-1])

        def back(r):
            shp = w[n].shape
            if n in TRANSPOSED:
                return jnp.swapaxes(r.reshape(shp[:-2] + (shp[-1], shp[-2])), -1, -2)
            return r.reshape(shp)

        res = _adamw_reduced("adamw_" + n, view(w[n]), [(t, others()) for t, others in pieces], view(m[n]), view(v[n]))
        grads[n], delta[n], new_m[n], new_v[n] = (back(r) for r in res)

    for n in BIG:
        update(n, [sums[l, part][j] for l in range(DEPTH) for part in ("a", "b", "mix")
                   for j, piece in enumerate(PARTS[part]) if piece[0] == n])
    for j, n in enumerate(SMALL_SHARDED):
        update(n, [last[n_a + j]])
    rep_mine = _own_total("rs_total_small", last[-1][0], last[-1][1]())
    rep_grads = _all_gather("gather_small_grads", [rep_mine])[0].reshape(-1, LANES)
    res = _adamw("adamw_small", _pack_rows(w, REPLICATED), rep_grads, _pack_rows(m, REPLICATED), _pack_rows(v, REPLICATED))
    for dst, src in zip((grads, delta, new_m, new_v), (rep_grads,) + tuple(res)):
        dst.update(_unpack_rows(src, REPLICATED, w))

    return (loss, dx.reshape(x.shape), *[grads[n] for n in WEIGHT_NAMES], *[delta[n] for n in WEIGHT_NAMES],
            *[new_m[n] for n in WEIGHT_NAMES], *[new_v[n] for n in WEIGHT_NAMES])
```
